```python
import math
import jax, jax.numpy as jnp
from jax import lax
import numpy as np

D_MODEL = 1024
BATCH = 8
SEQ = 8192
DEPTH = 2

N_MIXERS = 2
EPS = 1e-6

M_HEADS = 8
M_QK_DIM = D_MODEL // (2 * M_HEADS)
M_V_DIM = D_MODEL // M_HEADS
M_CHUNK = 64
GATE_CAP = 15.0
M_IN = 2 * M_HEADS * M_QK_DIM + M_HEADS * M_V_DIM + D_MODEL + 2 * M_HEADS

A_HEAD_DIM = 64
A_Q_HEADS = D_MODEL // A_HEAD_DIM
A_KV_HEADS = 2
A_GROUP = A_Q_HEADS // A_KV_HEADS
WINDOW = 128
A_BLOCK = 128
A_IN = (A_Q_HEADS + 2 * A_KV_HEADS) * A_HEAD_DIM

D_FF = int(math.ceil(8 * D_MODEL / 3 / 128)) * 128
CONV_W = 3

N_MLSTM_LAYERS = (DEPTH + 1) // 2
N_ATTN_LAYERS = DEPTH // 2

kernel_name = "hybrid_mlstm_swa_sink_convffn"


def rmsnorm(x, g):
    xf = x.astype(jnp.float32)
    y = xf * lax.rsqrt(jnp.mean(xf * xf, axis=-1, keepdims=True) + EPS)
    return (y * g.astype(jnp.float32)).astype(x.dtype)


def softcap(z, cap):
    return cap * jnp.tanh(z / cap)


def mlstm_mixer(h, w_in, gate_bias, head_norm, w_out):
    B, S, _ = h.shape
    H, dk, dv, L = M_HEADS, M_QK_DIM, M_V_DIM, M_CHUNK
    NC = S // L
    proj = h @ w_in
    splits = np.cumsum([H * dk, H * dk, H * dv, D_MODEL, H])
    q, k, v, o, ig_raw, fg_raw = jnp.split(proj, splits.tolist(), axis=-1)
    ig = softcap(ig_raw.astype(jnp.float32) + gate_bias[0].astype(jnp.float32), GATE_CAP)
    lf = jax.nn.log_sigmoid(softcap(fg_raw.astype(jnp.float32) + gate_bias[1].astype(jnp.float32), GATE_CAP))

    def to_chunks(t, d):
        return t.astype(jnp.float32).reshape(B, NC, L, H, d).transpose(1, 0, 3, 2, 4)

    def gate_chunks(t):
        return t.reshape(B, NC, L, H).transpose(1, 0, 3, 2)

    qc = to_chunks(q, dk) * (dk ** -0.5)
    kc = to_chunks(k, dk)
    vc = to_chunks(v, dv)
    igc, lfc = gate_chunks(ig), gate_chunks(lf)
    causal = jnp.tril(jnp.ones((L, L), dtype=bool))

    def body(carry, xs):
        C, n, m = carry
        qb, kb, vb, igb, lfb = xs
        b = jnp.cumsum(lfb, axis=-1)
        inter = b + m[..., None]
        dlog = b[..., :, None] - b[..., None, :] + igb[..., None, :]
        dlog = jnp.where(causal, dlog, -jnp.inf)
        m_t = jnp.maximum(inter, jnp.max(dlog, axis=-1))
        w = jnp.einsum('bhtd,bhsd->bhts', qb, kb) * jnp.exp(dlog - m_t[..., None])
        s_inter = jnp.exp(inter - m_t)
        num = jnp.einsum('bhts,bhsv->bhtv', w, vb) + s_inter[..., None] * jnp.einsum('bhtd,bhdv->bhtv', qb, C)
        den = jnp.sum(w, axis=-1) + s_inter * jnp.einsum('bhtd,bhd->bht', qb, n)
        hout = num / jnp.maximum(jnp.abs(den), jnp.exp(-m_t))[..., None]
        bL = b[..., -1]
        tail = bL[..., None] - b + igb
        m_new = jnp.maximum(bL + m, jnp.max(tail, axis=-1))
        ws = jnp.exp(tail - m_new[..., None])
        decay = jnp.exp(bL + m - m_new)
        C_new = decay[..., None, None] * C + jnp.einsum('bhs,bhsd,bhsv->bhdv', ws, kb, vb)
        n_new = decay[..., None] * n + jnp.einsum('bhs,bhsd->bhd', ws, kb)
        return (C_new, n_new, m_new), hout

    init = (jnp.zeros((B, H, dk, dv), jnp.float32),
            jnp.zeros((B, H, dk), jnp.float32),
            jnp.zeros((B, H), jnp.float32))
    _, hs = lax.scan(body, init, (qc, kc, vc, igc, lfc))
    hs = hs.transpose(1, 0, 3, 2, 4).reshape(B, S, H, dv)
    hs = hs * lax.rsqrt(jnp.mean(hs * hs, axis=-1, keepdims=True) + EPS)
    hs = hs * head_norm.astype(jnp.float32).reshape(H, dv)
    gated = jax.nn.sigmoid(o.astype(jnp.float32)) * hs.reshape(B, S, H * dv)
    return gated.astype(h.dtype) @ w_out


def swa_sink_mixer(h, w_in, b_in, sinks, w_out, b_out):
    B, S, _ = h.shape
    Hq, Hkv, G, dh, BLK = A_Q_HEADS, A_KV_HEADS, A_GROUP, A_HEAD_DIM, A_BLOCK
    NB = S // BLK
    proj = h @ w_in + b_in
    q, k, v = jnp.split(proj, [Hq * dh, (Hq + Hkv) * dh], axis=-1)
    q = q.reshape(B, NB, BLK, Hkv, G, dh) * (dh ** -0.5)
    k = k.reshape(B, NB, BLK, Hkv, dh)
    v = v.reshape(B, NB, BLK, Hkv, dh)

    def with_prev(t):
        prev = jnp.pad(t, ((0, 0), (1, 0), (0, 0), (0, 0), (0, 0)))[:, :-1]
        return jnp.concatenate([prev, t], axis=2)

    kc, vc = with_prev(k), with_prev(v)
    s = jnp.einsum('bnqhgd,bnkhd->bnhgqk', q, kc).astype(jnp.float32)
    qi = jnp.arange(BLK)[:, None]
    ku = jnp.arange(2 * BLK)[None, :]
    diff = qi - (ku - BLK)
    band = (diff >= 0) & (diff < WINDOW)
    kpos = jnp.arange(NB)[:, None, None] * BLK + ku[None] - BLK
    mask = band[None] & (kpos >= 0)
    s = jnp.where(mask[None, :, None, None], s, -jnp.inf)
    sink = sinks.astype(jnp.float32).reshape(1, 1, Hkv, G, 1, 1)
    mx = jnp.maximum(jnp.max(s, axis=-1, keepdims=True), sink)
    p = jnp.exp(s - mx)
    p = p / (jnp.sum(p, axis=-1, keepdims=True) + jnp.exp(sink - mx))
    o = jnp.einsum('bnhgqk,bnkhd->bnqhgd', p.astype(vc.dtype), vc).reshape(B, S, Hq * dh)
    return o @ w_out + b_out


def conv_ffn(h, w_up, conv_w, conv_b, w_down):
    S = h.shape[1]
    u = h @ w_up
    gate, val = jnp.split(u, 2, axis=-1)
    gp = jnp.pad(gate, ((0, 0), (CONV_W - 1, 0), (0, 0)))
    gate_c = conv_b + sum(conv_w[j] * gp[:, j:j + S] for j in range(CONV_W))
    return (jax.nn.silu(gate_c) * val) @ w_down


def _fwd_setup_inputs(seed: int = 0) -> dict:
    key = jax.random.key(seed)
    ks = jax.random.split(key, 24)
    nrm = lambda k, shape, s: jax.random.normal(k, shape, jnp.float32) * s
    LM, LA = N_MLSTM_LAYERS, N_ATTN_LAYERS
    f_bias = jnp.broadcast_to(jnp.linspace(3.0, 6.0, M_HEADS), (LM, M_HEADS)) + nrm(ks[2], (LM, M_HEADS), 0.1)
    i_bias = nrm(ks[3], (LM, M_HEADS), 0.1)
    gain = lambda k: 1.0 + nrm(k, (DEPTH, D_MODEL), 0.02)
    return {
        "x": nrm(ks[0], (BATCH, SEQ, D_MODEL), 1.0),
        "m_w_in": nrm(ks[1], (LM, D_MODEL, M_IN), D_MODEL ** -0.5),
        "m_gate_bias": jnp.stack([i_bias, f_bias], axis=1),
        "m_head_norm": 1.0 + nrm(ks[4], (LM, M_HEADS * M_V_DIM), 0.02),
        "m_w_out": nrm(ks[5], (LM, M_HEADS * M_V_DIM, D_MODEL), (M_HEADS * M_V_DIM) ** -0.5),
        "a_w_in": nrm(ks[6], (LA, D_MODEL, A_IN), D_MODEL ** -0.5),
        "a_b_in": nrm(ks[7], (LA, A_IN), 0.02),
        "a_sinks": nrm(ks[8], (LA, A_Q_HEADS), 0.5),
        "a_w_out": nrm(ks[9], (LA, A_Q_HEADS * A_HEAD_DIM, D_MODEL), (A_Q_HEADS * A_HEAD_DIM) ** -0.5),
        "a_b_out": nrm(ks[10], (LA, D_MODEL), 0.02),
        "norm_mix_pre": gain(ks[11]),
        "norm_mix_post": gain(ks[12]),
        "norm_ffn_pre": gain(ks[13]),
        "norm_ffn_post": gain(ks[14]),
        "f_w_up": nrm(ks[15], (DEPTH, D_MODEL, 2 * D_FF), D_MODEL ** -0.5),
        "f_conv_w": nrm(ks[16], (DEPTH, CONV_W, D_FF), CONV_W ** -0.5),
        "f_conv_b": nrm(ks[17], (DEPTH, D_FF), 0.02),
        "f_w_down": nrm(ks[18], (DEPTH, D_FF, D_MODEL), D_FF ** -0.5),
    }


def _fwd_reference(x, m_w_in, m_gate_bias, m_head_norm, m_w_out,
              a_w_in, a_b_in, a_sinks, a_w_out, a_b_out,
              norm_mix_pre, norm_mix_post, norm_ffn_pre, norm_ffn_post,
              f_w_up, f_conv_w, f_conv_b, f_w_down):
    h = x
    for i in range(DEPTH):
        z = rmsnorm(h, norm_mix_pre[i])
        if i % N_MIXERS == 0:
            j = i // N_MIXERS
            z = mlstm_mixer(z, m_w_in[j], m_gate_bias[j], m_head_norm[j], m_w_out[j])
        else:
            j = i // N_MIXERS
            z = swa_sink_mixer(z, a_w_in[j], a_b_in[j], a_sinks[j], a_w_out[j], a_b_out[j])
        h = h + rmsnorm(z, norm_mix_post[i])
        z = conv_ffn(rmsnorm(h, norm_ffn_pre[i]), f_w_up[i], f_conv_w[i], f_conv_b[i], f_w_down[i])
        h = h + rmsnorm(z, norm_ffn_post[i])
    return h


import jax as _jax
import jax.numpy as _jnp

TWIN_FORMAT = 'train_step'
FWD_PARAMS = ['x', 'm_w_in', 'm_gate_bias', 'm_head_norm', 'm_w_out', 'a_w_in', 'a_b_in', 'a_sinks', 'a_w_out', 'a_b_out', 'norm_mix_pre', 'norm_mix_post', 'norm_ffn_pre', 'norm_ffn_post', 'f_w_up', 'f_conv_w', 'f_conv_b', 'f_w_down']
TWIN_WEIGHTS = ['m_w_in', 'm_gate_bias', 'm_head_norm', 'm_w_out', 'a_w_in', 'a_b_in', 'a_sinks', 'a_w_out', 'a_b_out', 'norm_mix_pre', 'norm_mix_post', 'norm_ffn_pre', 'norm_ffn_post', 'f_w_up', 'f_conv_w', 'f_conv_b', 'f_w_down']
TWIN_DIFF_INPUT = 'x'
TWIN_INPUTS = ['x', 'm_w_in', 'm_gate_bias', 'm_head_norm', 'm_w_out', 'a_w_in', 'a_b_in', 'a_sinks', 'a_w_out', 'a_b_out', 'norm_mix_pre', 'norm_mix_post', 'norm_ffn_pre', 'norm_ffn_post', 'f_w_up', 'f_conv_w', 'f_conv_b', 'f_w_down', 'loss_target', 'm_m_w_in', 'm_m_gate_bias', 'm_m_head_norm', 'm_m_w_out', 'm_a_w_in', 'm_a_b_in', 'm_a_sinks', 'm_a_w_out', 'm_a_b_out', 'm_norm_mix_pre', 'm_norm_mix_post', 'm_norm_ffn_pre', 'm_norm_ffn_post', 'm_f_w_up', 'm_f_conv_w', 'm_f_conv_b', 'm_f_w_down', 'v_m_w_in', 'v_m_gate_bias', 'v_m_head_norm', 'v_m_w_out', 'v_a_w_in', 'v_a_b_in', 'v_a_sinks', 'v_a_w_out', 'v_a_b_out', 'v_norm_mix_pre', 'v_norm_mix_post', 'v_norm_ffn_pre', 'v_norm_ffn_post', 'v_f_w_up', 'v_f_conv_w', 'v_f_conv_b', 'v_f_w_down']
TWIN_OUTPUTS = ['loss', 'grad_x', 'grad_m_w_in', 'grad_m_gate_bias', 'grad_m_head_norm', 'grad_m_w_out', 'grad_a_w_in', 'grad_a_b_in', 'grad_a_sinks', 'grad_a_w_out', 'grad_a_b_out', 'grad_norm_mix_pre', 'grad_norm_mix_post', 'grad_norm_ffn_pre', 'grad_norm_ffn_post', 'grad_f_w_up', 'grad_f_conv_w', 'grad_f_conv_b', 'grad_f_w_down', 'delta_m_w_in', 'delta_m_gate_bias', 'delta_m_head_norm', 'delta_m_w_out', 'delta_a_w_in', 'delta_a_b_in', 'delta_a_sinks', 'delta_a_w_out', 'delta_a_b_out', 'delta_norm_mix_pre', 'delta_norm_mix_post', 'delta_norm_ffn_pre', 'delta_norm_ffn_post', 'delta_f_w_up', 'delta_f_conv_w', 'delta_f_conv_b', 'delta_f_w_down', 'new_m_m_w_in', 'new_m_m_gate_bias', 'new_m_m_head_norm', 'new_m_m_w_out', 'new_m_a_w_in', 'new_m_a_b_in', 'new_m_a_sinks', 'new_m_a_w_out', 'new_m_a_b_out', 'new_m_norm_mix_pre', 'new_m_norm_mix_post', 'new_m_norm_ffn_pre', 'new_m_norm_ffn_post', 'new_m_f_w_up', 'new_m_f_conv_w', 'new_m_f_conv_b', 'new_m_f_w_down', 'new_v_m_w_in', 'new_v_m_gate_bias', 'new_v_m_head_norm', 'new_v_m_w_out', 'new_v_a_w_in', 'new_v_a_b_in', 'new_v_a_sinks', 'new_v_a_w_out', 'new_v_a_b_out', 'new_v_norm_mix_pre', 'new_v_norm_mix_post', 'new_v_norm_ffn_pre', 'new_v_norm_ffn_post', 'new_v_f_w_up', 'new_v_f_conv_w', 'new_v_f_conv_b', 'new_v_f_w_down']
TWIN_LEAF_KINDS = {'loss': 'loss', 'grad_x': 'grad_x', 'grad_m_w_in': 'grad_w', 'grad_m_gate_bias': 'grad_w', 'grad_m_head_norm': 'grad_w', 'grad_m_w_out': 'grad_w', 'grad_a_w_in': 'grad_w', 'grad_a_b_in': 'grad_w', 'grad_a_sinks': 'grad_w', 'grad_a_w_out': 'grad_w', 'grad_a_b_out': 'grad_w', 'grad_norm_mix_pre': 'grad_w', 'grad_norm_mix_post': 'grad_w', 'grad_norm_ffn_pre': 'grad_w', 'grad_norm_ffn_post': 'grad_w', 'grad_f_w_up': 'grad_w', 'grad_f_conv_w': 'grad_w', 'grad_f_conv_b': 'grad_w', 'grad_f_w_down': 'grad_w', 'delta_m_w_in': 'delta_w', 'delta_m_gate_bias': 'delta_w', 'delta_m_head_norm': 'delta_w', 'delta_m_w_out': 'delta_w', 'delta_a_w_in': 'delta_w', 'delta_a_b_in': 'delta_w', 'delta_a_sinks': 'delta_w', 'delta_a_w_out': 'delta_w', 'delta_a_b_out': 'delta_w', 'delta_norm_mix_pre': 'delta_w', 'delta_norm_mix_post': 'delta_w', 'delta_norm_ffn_pre': 'delta_w', 'delta_norm_ffn_post': 'delta_w', 'delta_f_w_up': 'delta_w', 'delta_f_conv_w': 'delta_w', 'delta_f_conv_b': 'delta_w', 'delta_f_w_down': 'delta_w', 'new_m_m_w_in': 'new_m', 'new_m_m_gate_bias': 'new_m', 'new_m_m_head_norm': 'new_m', 'new_m_m_w_out': 'new_m', 'new_m_a_w_in': 'new_m', 'new_m_a_b_in': 'new_m', 'new_m_a_sinks': 'new_m', 'new_m_a_w_out': 'new_m', 'new_m_a_b_out': 'new_m', 'new_m_norm_mix_pre': 'new_m', 'new_m_norm_mix_post': 'new_m', 'new_m_norm_ffn_pre': 'new_m', 'new_m_norm_ffn_post': 'new_m', 'new_m_f_w_up': 'new_m', 'new_m_f_conv_w': 'new_m', 'new_m_f_conv_b': 'new_m', 'new_m_f_w_down': 'new_m', 'new_v_m_w_in': 'new_v', 'new_v_m_gate_bias': 'new_v', 'new_v_m_head_norm': 'new_v', 'new_v_m_w_out': 'new_v', 'new_v_a_w_in': 'new_v', 'new_v_a_b_in': 'new_v', 'new_v_a_sinks': 'new_v', 'new_v_a_w_out': 'new_v', 'new_v_a_b_out': 'new_v', 'new_v_norm_mix_pre': 'new_v', 'new_v_norm_mix_post': 'new_v', 'new_v_norm_ffn_pre': 'new_v', 'new_v_norm_ffn_post': 'new_v', 'new_v_f_w_up': 'new_v', 'new_v_f_conv_w': 'new_v', 'new_v_f_conv_b': 'new_v', 'new_v_f_w_down': 'new_v'}


def _forward(args):
    return _fwd_reference(*[args[k] for k in FWD_PARAMS])


def _output_shape():
    def fwd():
        inp = _fwd_setup_inputs(0)
        return _fwd_reference(*[inp[k] for k in FWD_PARAMS])
    out = _jax.eval_shape(fwd)
    return out.shape, out.dtype

N_MICROBATCH = 1
ADAM_LR = 0.001
ADAM_B1 = 0.9
ADAM_B2 = 0.999
ADAM_EPS = 1e-08
ADAM_WD = 0.01
ADAM_STEP = 10
PER_EXAMPLE_BATCH_AXIS = {'x': 0, 'loss_target': 0}
SHARED_INPUTS = []
_WEIGHT_DTYPES = {'m_w_in': _jnp.float32, 'm_gate_bias': _jnp.float32, 'm_head_norm': _jnp.float32, 'm_w_out': _jnp.float32, 'a_w_in': _jnp.float32, 'a_b_in': _jnp.float32, 'a_sinks': _jnp.float32, 'a_w_out': _jnp.float32, 'a_b_out': _jnp.float32, 'norm_mix_pre': _jnp.float32, 'norm_mix_post': _jnp.float32, 'norm_ffn_pre': _jnp.float32, 'norm_ffn_post': _jnp.float32, 'f_w_up': _jnp.float32, 'f_conv_w': _jnp.float32, 'f_conv_b': _jnp.float32, 'f_w_down': _jnp.float32}
MOMENT_SCALE = {'m_w_in': 1.677975e+00, 'm_gate_bias': 4.642964e+00, 'm_head_norm': 1.808438e+00, 'm_w_out': 1.575016e+00, 'a_w_in': 1.362557e+00, 'a_b_in': 2.259858e+01, 'a_sinks': 2.796260e-01, 'a_w_out': 1.117130e+00, 'a_b_out': 2.742699e+01, 'norm_mix_pre': 2.401637e+00, 'norm_mix_post': 6.393554e+01, 'norm_ffn_pre': 1.484127e+00, 'norm_ffn_post': 6.353939e+01, 'f_w_up': 6.358210e-01, 'f_conv_w': 6.781631e-01, 'f_conv_b': 1.514453e+00, 'f_w_down': 1.292854e+00}


def _to_microbatches(a, axis):
    t = _jnp.moveaxis(a, axis, 0)
    t = t.reshape((N_MICROBATCH, t.shape[0] // N_MICROBATCH) + t.shape[1:])
    return _jnp.moveaxis(t, 1, axis + 1)


def setup_inputs(seed: int = 0) -> dict:
    inp = _fwd_setup_inputs(seed)
    key = _jax.random.fold_in(_jax.random.key(seed), 7919)
    shape, _ = _output_shape()
    out = dict(inp)
    out["loss_target"] = _jax.random.normal(_jax.random.fold_in(key, 0), shape, _jnp.float32)
    for i, name in enumerate(TWIN_WEIGHTS):
        w = inp[name].astype(_jnp.float32)
        if MOMENT_SCALE is None:
            s = _jnp.sqrt(_jnp.mean(_jnp.square(w)) + 1e-30)
        else:
            s = MOMENT_SCALE[name]
        km, kv = _jax.random.split(_jax.random.fold_in(key, i + 1))
        out[name] = w
        out["m_" + name] = s * _jax.random.normal(km, w.shape, _jnp.float32)
        out["v_" + name] = (s * s) * _jax.random.uniform(kv, w.shape, _jnp.float32, 0.5, 1.5)
    if N_MICROBATCH > 1:
        for name, axis in PER_EXAMPLE_BATCH_AXIS.items():
            out[name] = _to_microbatches(out[name], axis)
    return {'x': out['x'], 'm_w_in': out['m_w_in'], 'm_gate_bias': out['m_gate_bias'], 'm_head_norm': out['m_head_norm'], 'm_w_out': out['m_w_out'], 'a_w_in': out['a_w_in'], 'a_b_in': out['a_b_in'], 'a_sinks': out['a_sinks'], 'a_w_out': out['a_w_out'], 'a_b_out': out['a_b_out'], 'norm_mix_pre': out['norm_mix_pre'], 'norm_mix_post': out['norm_mix_post'], 'norm_ffn_pre': out['norm_ffn_pre'], 'norm_ffn_post': out['norm_ffn_post'], 'f_w_up': out['f_w_up'], 'f_conv_w': out['f_conv_w'], 'f_conv_b': out['f_conv_b'], 'f_w_down': out['f_w_down'], 'loss_target': out['loss_target'], 'm_m_w_in': out['m_m_w_in'], 'm_m_gate_bias': out['m_m_gate_bias'], 'm_m_head_norm': out['m_m_head_norm'], 'm_m_w_out': out['m_m_w_out'], 'm_a_w_in': out['m_a_w_in'], 'm_a_b_in': out['m_a_b_in'], 'm_a_sinks': out['m_a_sinks'], 'm_a_w_out': out['m_a_w_out'], 'm_a_b_out': out['m_a_b_out'], 'm_norm_mix_pre': out['m_norm_mix_pre'], 'm_norm_mix_post': out['m_norm_mix_post'], 'm_norm_ffn_pre': out['m_norm_ffn_pre'], 'm_norm_ffn_post': out['m_norm_ffn_post'], 'm_f_w_up': out['m_f_w_up'], 'm_f_conv_w': out['m_f_conv_w'], 'm_f_conv_b': out['m_f_conv_b'], 'm_f_w_down': out['m_f_w_down'], 'v_m_w_in': out['v_m_w_in'], 'v_m_gate_bias': out['v_m_gate_bias'], 'v_m_head_norm': out['v_m_head_norm'], 'v_m_w_out': out['v_m_w_out'], 'v_a_w_in': out['v_a_w_in'], 'v_a_b_in': out['v_a_b_in'], 'v_a_sinks': out['v_a_sinks'], 'v_a_w_out': out['v_a_w_out'], 'v_a_b_out': out['v_a_b_out'], 'v_norm_mix_pre': out['v_norm_mix_pre'], 'v_norm_mix_post': out['v_norm_mix_post'], 'v_norm_ffn_pre': out['v_norm_ffn_pre'], 'v_norm_ffn_post': out['v_norm_ffn_post'], 'v_f_w_up': out['v_f_w_up'], 'v_f_conv_w': out['v_f_conv_w'], 'v_f_conv_b': out['v_f_conv_b'], 'v_f_w_down': out['v_f_w_down']}


def _loss(weights, diff, rest, loss_target):
    with _jax.named_scope("forward"):
        args = {**rest, TWIN_DIFF_INPUT: diff, **{k: w.astype(_WEIGHT_DTYPES[k]) for k, w in weights.items()}}
        y = _forward(args)
    with _jax.named_scope("loss_head"):
        err = _jnp.square(y.astype(_jnp.float32) - loss_target)
        return 0.5 * _jnp.sum(_jnp.mean(err, axis=-1)) if err.ndim else 0.5 * err


def _adamw(w, g, m, v):
    m = ADAM_B1 * m + (1.0 - ADAM_B1) * g
    v = ADAM_B2 * v + (1.0 - ADAM_B2) * _jnp.square(g)
    m_hat = m / (1.0 - ADAM_B1 ** ADAM_STEP)
    v_hat = v / (1.0 - ADAM_B2 ** ADAM_STEP)
    delta = -ADAM_LR * (m_hat / (_jnp.sqrt(v_hat) + ADAM_EPS) + ADAM_WD * w)
    return delta, m, v


def reference(x, m_w_in, m_gate_bias, m_head_norm, m_w_out, a_w_in, a_b_in, a_sinks, a_w_out, a_b_out, norm_mix_pre, norm_mix_post, norm_ffn_pre, norm_ffn_post, f_w_up, f_conv_w, f_conv_b, f_w_down, loss_target, m_m_w_in, m_m_gate_bias, m_m_head_norm, m_m_w_out, m_a_w_in, m_a_b_in, m_a_sinks, m_a_w_out, m_a_b_out, m_norm_mix_pre, m_norm_mix_post, m_norm_ffn_pre, m_norm_ffn_post, m_f_w_up, m_f_conv_w, m_f_conv_b, m_f_w_down, v_m_w_in, v_m_gate_bias, v_m_head_norm, v_m_w_out, v_a_w_in, v_a_b_in, v_a_sinks, v_a_w_out, v_a_b_out, v_norm_mix_pre, v_norm_mix_post, v_norm_ffn_pre, v_norm_ffn_post, v_f_w_up, v_f_conv_w, v_f_conv_b, v_f_w_down):
    given = dict(x=x, m_w_in=m_w_in, m_gate_bias=m_gate_bias, m_head_norm=m_head_norm, m_w_out=m_w_out, a_w_in=a_w_in, a_b_in=a_b_in, a_sinks=a_sinks, a_w_out=a_w_out, a_b_out=a_b_out, norm_mix_pre=norm_mix_pre, norm_mix_post=norm_mix_post, norm_ffn_pre=norm_ffn_pre, norm_ffn_post=norm_ffn_post, f_w_up=f_w_up, f_conv_w=f_conv_w, f_conv_b=f_conv_b, f_w_down=f_w_down, loss_target=loss_target, m_m_w_in=m_m_w_in, m_m_gate_bias=m_m_gate_bias, m_m_head_norm=m_m_head_norm, m_m_w_out=m_m_w_out, m_a_w_in=m_a_w_in, m_a_b_in=m_a_b_in, m_a_sinks=m_a_sinks, m_a_w_out=m_a_w_out, m_a_b_out=m_a_b_out, m_norm_mix_pre=m_norm_mix_pre, m_norm_mix_post=m_norm_mix_post, m_norm_ffn_pre=m_norm_ffn_pre, m_norm_ffn_post=m_norm_ffn_post, m_f_w_up=m_f_w_up, m_f_conv_w=m_f_conv_w, m_f_conv_b=m_f_conv_b, m_f_w_down=m_f_w_down, v_m_w_in=v_m_w_in, v_m_gate_bias=v_m_gate_bias, v_m_head_norm=v_m_head_norm, v_m_w_out=v_m_w_out, v_a_w_in=v_a_w_in, v_a_b_in=v_a_b_in, v_a_sinks=v_a_sinks, v_a_w_out=v_a_w_out, v_a_b_out=v_a_b_out, v_norm_mix_pre=v_norm_mix_pre, v_norm_mix_post=v_norm_mix_post, v_norm_ffn_pre=v_norm_ffn_pre, v_norm_ffn_post=v_norm_ffn_post, v_f_w_up=v_f_w_up, v_f_conv_w=v_f_conv_w, v_f_conv_b=v_f_conv_b, v_f_w_down=v_f_w_down)
    weights = {n: given[n] for n in TWIN_WEIGHTS}
    shared = {n: given[n] for n in SHARED_INPUTS}
    per_example = {n: given[n] for n in ['x']}
    grad_fn = _jax.value_and_grad(_loss, argnums=(0, 1))

    def one_microbatch(ex, loss_target):
        ex = dict(ex)
        diff = ex.pop(TWIN_DIFF_INPUT)
        return grad_fn(weights, diff, {**shared, **ex}, loss_target)

    if N_MICROBATCH == 1:
        loss, (grad_w, grad_x) = one_microbatch(per_example, given["loss_target"])
    else:
        def body(carry, xs):
            loss_sum, grad_sum = carry
            l_k, (gw_k, gx_k) = one_microbatch(xs[0], xs[1])
            with _jax.named_scope("update"):
                return (loss_sum + l_k, _jax.tree.map(_jnp.add, grad_sum, gw_k)), gx_k

        init = (_jnp.zeros((), _jnp.float32), _jax.tree.map(_jnp.zeros_like, weights))
        (loss, grad_w), grad_x = _jax.lax.scan(body, init, (per_example, given["loss_target"]))
    with _jax.named_scope("update"):
        delta_w, new_m, new_v = {}, {}, {}
        for n in TWIN_WEIGHTS:
            delta_w[n], new_m[n], new_v[n] = _adamw(weights[n], grad_w[n], given["m_" + n], given["v_" + n])
    return (loss, grad_x, *[grad_w[n] for n in TWIN_WEIGHTS], *[delta_w[n] for n in TWIN_WEIGHTS],
            *[new_m[n] for n in TWIN_WEIGHTS], *[new_v[n] for n in TWIN_WEIGHTS])
```

```python
import functools
import math

import jax
import jax.numpy as jnp
from jax import lax
from jax.experimental import pallas as pl
from jax.experimental.pallas import tpu as pltpu

F32 = jnp.float32
MXU_DTYPE = jnp.bfloat16
WIRE_DTYPE = jnp.bfloat16
MESH = pl.DeviceIdType.MESH

D_MODEL = 1024
EPS = 1e-6
M_HEADS, M_QK, M_V, M_CHUNK = 8, 64, 128, 64
GATE_CAP = 15.0
A_DH, A_QH, A_KVH, A_G, A_BLK = 64, 16, 2, 8, 128
D_FF = 2816
N_CHIPS = 4
LANES = 128
VMEM_LIMIT = 48 * 1024 * 1024

ADAM_LR, ADAM_B1, ADAM_B2, ADAM_EPS, ADAM_WD, ADAM_STEP = 0.001, 0.9, 0.999, 1e-08, 0.01, 10


def _cparams(sem):
    return pltpu.CompilerParams(dimension_semantics=sem, vmem_limit_bytes=VMEM_LIMIT)


def _pick(n, cands):
    for c in cands:
        if n % c == 0:
            return c
    return n


def _mm(a, b, *, ta=False, tb=False, out_dtype=F32, bias=None, add=None, name):
    if ta:
        K, M = a.shape
    else:
        M, K = a.shape
    if tb:
        N, Kb = b.shape
    else:
        Kb, N = b.shape
    assert K == Kb, (a.shape, b.shape)
    tm = _pick(M, (512, 256, 128))
    tn = _pick(N, (512, 256, 128))
    tk = K if K <= 4096 else _pick(K, (2048, 1024, 512, 256, 128))
    nk = K // tk
    dn = (((0 if ta else 1,), (1 if tb else 0,)), ((), ()))
    has_bias, has_add = bias is not None, add is not None

    def body(*refs):
        a_ref, b_ref = refs[0], refs[1]
        pos = 2
        bias_ref = add_ref = None
        if has_bias:
            bias_ref = refs[pos]
            pos += 1
        if has_add:
            add_ref = refs[pos]
            pos += 1
        o_ref = refs[pos]
        acc_ref = refs[pos + 1] if nk > 1 else None

        def finish(r):
            if has_bias:
                r = r + bias_ref[...]
            if has_add:
                r = r + add_ref[...]
            o_ref[...] = r.astype(out_dtype)

        part = lax.dot_general(a_ref[...], b_ref[...], dn, preferred_element_type=F32)
        if nk == 1:
            finish(part)
        else:
            k = pl.program_id(2)

            @pl.when(k == 0)
            def _():
                acc_ref[...] = part

            @pl.when(k > 0)
            def _():
                acc_ref[...] += part

            @pl.when(k == nk - 1)
            def _():
                finish(acc_ref[...])

    a_spec = pl.BlockSpec((tk, tm), lambda i, j, k: (k, i)) if ta else pl.BlockSpec((tm, tk), lambda i, j, k: (i, k))
    b_spec = pl.BlockSpec((tn, tk), lambda i, j, k: (j, k)) if tb else pl.BlockSpec((tk, tn), lambda i, j, k: (k, j))
    in_specs, args = [a_spec, b_spec], [a, b]
    if has_bias:
        in_specs.append(pl.BlockSpec((1, tn), lambda i, j, k: (0, j)))
        args.append(bias)
    if has_add:
        in_specs.append(pl.BlockSpec((tm, tn), lambda i, j, k: (i, j)))
        args.append(add)
    return pl.pallas_call(
        body, name=name,
        grid=(M // tm, N // tn, nk),
        in_specs=in_specs,
        out_specs=pl.BlockSpec((tm, tn), lambda i, j, k: (i, j)),
        out_shape=jax.ShapeDtypeStruct((M, N), out_dtype),
        scratch_shapes=[pltpu.VMEM((tm, tn), F32)] if nk > 1 else [],
        compiler_params=_cparams(("parallel", "parallel", "arbitrary")),
    )(*args)


ROW_TILE = 512


def _rms_fwd(x, g, *, res=None, out_dtype, name):
    S, D = x.shape
    tm = _pick(S, (ROW_TILE, 256, 128))
    has_res = res is not None

    def body(*refs):
        x_ref, g_ref = refs[0], refs[1]
        res_ref = refs[2] if has_res else None
        o_ref = refs[-1]
        xv = x_ref[...]
        y = xv * lax.rsqrt(jnp.mean(xv * xv, axis=-1, keepdims=True) + EPS) * g_ref[...]
        if has_res:
            y = res_ref[...] + y
        o_ref[...] = y.astype(out_dtype)

    row = pl.BlockSpec((tm, D), lambda i: (i, 0))
    in_specs, args = [row, pl.BlockSpec((1, D), lambda i: (0, 0))], [x, g]
    if has_res:
        in_specs.append(row)
        args.append(res)
    return pl.pallas_call(
        body, name=name, grid=(S // tm,), in_specs=in_specs, out_specs=row,
        out_shape=jax.ShapeDtypeStruct((S, D), out_dtype),
        compiler_params=_cparams(("parallel",)),
    )(*args)


def _rms_bwd(x, g, dy, *, res=None, out_dtype, name):
    S, D = x.shape
    tm = _pick(S, (ROW_TILE, 256, 128))
    has_res = res is not None

    def body(*refs):
        x_ref, g_ref, dy_ref = refs[0], refs[1], refs[2]
        res_ref = refs[3] if has_res else None
        dx_ref, dg_ref, cs_ref = refs[-3], refs[-2], refs[-1]
        i = pl.program_id(0)
        xv = x_ref[...]
        dyv = dy_ref[...].astype(F32)
        r = lax.rsqrt(jnp.mean(xv * xv, axis=-1, keepdims=True) + EPS)
        xh = xv * r
        gd = dyv * g_ref[...]
        dx = r * (gd - xh * jnp.mean(gd * xh, axis=-1, keepdims=True))
        cs = jnp.sum(dx, axis=0, keepdims=True)
        if has_res:
            dx = res_ref[...] + dx
        dx_ref[...] = dx.astype(out_dtype)
        dg = jnp.sum(dyv * xh, axis=0, keepdims=True)

        @pl.when(i == 0)
        def _():
            dg_ref[...] = dg
            cs_ref[...] = cs

        @pl.when(i > 0)
        def _():
            dg_ref[...] += dg
            cs_ref[...] += cs

    row = pl.BlockSpec((tm, D), lambda i: (i, 0))
    vec = pl.BlockSpec((1, D), lambda i: (0, 0))
    in_specs, args = [row, vec, row], [x, g, dy]
    if has_res:
        in_specs.append(row)
        args.append(res)
    return pl.pallas_call(
        body, name=name, grid=(S // tm,), in_specs=in_specs, out_specs=[row, vec, vec],
        out_shape=[jax.ShapeDtypeStruct((S, D), out_dtype), jax.ShapeDtypeStruct((1, D), F32),
                   jax.ShapeDtypeStruct((1, D), F32)],
        compiler_params=_cparams(("arbitrary",)),
    )(*args)


def _cast_colsum(x, *, out_dtype, name):
    S, N = x.shape
    tm = _pick(S, (ROW_TILE, 256, 128))

    def body(x_ref, o_ref, cs_ref):
        i = pl.program_id(0)
        xv = x_ref[...]
        o_ref[...] = xv.astype(out_dtype)
        cs = jnp.sum(xv, axis=0, keepdims=True)

        @pl.when(i == 0)
        def _():
            cs_ref[...] = cs

        @pl.when(i > 0)
        def _():
            cs_ref[...] += cs

    row = pl.BlockSpec((tm, N), lambda i: (i, 0))
    vec = pl.BlockSpec((1, N), lambda i: (0, 0))
    return pl.pallas_call(
        body, name=name, grid=(S // tm,), in_specs=[row], out_specs=[row, vec],
        out_shape=[jax.ShapeDtypeStruct((S, N), out_dtype), jax.ShapeDtypeStruct((1, N), F32)],
        compiler_params=_cparams(("arbitrary",)),
    )(x)


def _loss_head(y, target, *, name):
    S, D = y.shape
    tm = _pick(S, (ROW_TILE, 256, 128))

    def body(y_ref, t_ref, loss_ref, dy_ref):
        i = pl.program_id(0)
        err = y_ref[...] - t_ref[...]
        dy_ref[...] = err * (1.0 / D)
        part = 0.5 * jnp.sum(jnp.mean(err * err, axis=-1, keepdims=True), axis=0, keepdims=True)

        @pl.when(i == 0)
        def _():
            loss_ref[...] = part

        @pl.when(i > 0)
        def _():
            loss_ref[...] += part

    row = pl.BlockSpec((tm, D), lambda i: (i, 0))
    return pl.pallas_call(
        body, name=name, grid=(S // tm,), in_specs=[row, row],
        out_specs=[pl.BlockSpec((1, 1), lambda i: (0, 0)), row],
        out_shape=[jax.ShapeDtypeStruct((1, 1), F32), jax.ShapeDtypeStruct((S, D), F32)],
        compiler_params=_cparams(("arbitrary",)),
    )(y, target)


def _mx(t):
    return t.astype(MXU_DTYPE)


def _mxf(t):
    return t.astype(MXU_DTYPE).astype(F32)


def _rawdot(a, b, ca, cb):
    return lax.dot_general(_mx(a), _mx(b), (((ca,), (cb,)), ((), ())), preferred_element_type=F32)


@functools.partial(jax.custom_vjp, nondiff_argnums=(2, 3))
def _dot(a, b, ca, cb):
    return _rawdot(a, b, ca, cb)


def _dot_fwd(a, b, ca, cb):
    return _rawdot(a, b, ca, cb), (a, b)


def _dot_bwd(ca, cb, res, g):
    a, b = res
    bj = 1 if cb == 0 else 0
    ai = 0 if ca == 1 else 1
    da = _rawdot(g, b, 1, bj) if ca == 1 else _rawdot(b, g, bj, 1)
    db = _rawdot(a, g, ai, 0) if cb == 0 else _rawdot(g, a, 0, ai)
    return da.astype(a.dtype), db.astype(b.dtype)


_dot.defvjp(_dot_fwd, _dot_bwd)


def _softcap(z):
    return GATE_CAP * jnp.tanh(z / GATE_CAP)


def _log_sigmoid(z):
    return jnp.minimum(z, 0.0) - jnp.log(1.0 + jnp.exp(-jnp.abs(z)))


def _mlstm_chunk(q, k, v, o, igr, fgr, bi, bf, hn, C, n, m):
    L = q.shape[0]
    ti = lax.broadcasted_iota(jnp.int32, (L, L), 0)
    si = lax.broadcasted_iota(jnp.int32, (L, L), 1)
    eye = (ti == si).astype(F32)
    lower = (si <= ti)
    ig = _softcap(igr + bi)
    lf = _log_sigmoid(_softcap(fgr + bf))
    ig_row = jnp.sum(eye * ig, axis=0, keepdims=True)
    lf_row = jnp.sum(eye * lf, axis=0, keepdims=True)
    b = jnp.sum(jnp.where(lower, lf_row, 0.0), axis=1, keepdims=True)
    b_row = jnp.sum(jnp.where(ti <= si, lf, 0.0), axis=0, keepdims=True)
    inter = b + m
    dlog = jnp.where(lower, b - b_row + ig_row, -jnp.inf)
    m_t = jnp.maximum(inter, jnp.max(dlog, axis=-1, keepdims=True))
    qs = q * (M_QK ** -0.5)
    w = _dot(qs, k, 1, 1) * jnp.exp(dlog - m_t)
    s_inter = jnp.exp(inter - m_t)
    num = _dot(w, v, 1, 0) + s_inter * _dot(qs, C, 1, 0)
    den = jnp.sum(w, axis=-1, keepdims=True) + s_inter * jnp.sum(_mxf(qs) * _mxf(n), axis=-1, keepdims=True)
    hout = num / jnp.maximum(jnp.abs(den), jnp.exp(-m_t))
    bL = jnp.sum(lf, axis=0, keepdims=True)
    tail = bL - b + ig
    m_new = jnp.maximum(bL + m, jnp.max(tail, axis=0, keepdims=True))
    ws = jnp.exp(tail - m_new)
    decay = jnp.exp(bL + m - m_new)
    wk = ws * k
    C_new = decay * C + _dot(wk, v, 0, 0)
    n_new = decay * n + jnp.sum(_mxf(ws) * _mxf(k), axis=0, keepdims=True)
    hs = hout * lax.rsqrt(jnp.mean(hout * hout, axis=-1, keepdims=True) + EPS) * hn
    gated = jax.nn.sigmoid(o) * hs
    return gated, C_new, n_new, m_new


def _mlstm_specs(S, NC, rev):
    H, L = M_HEADS, M_CHUNK
    cc = (lambda c: NC - 1 - c) if rev else (lambda c: c)
    qk = pl.BlockSpec((None, L, M_QK), lambda h, c: (h, cc(c), 0))
    v_in = pl.BlockSpec((L, M_V), lambda h, c: (cc(c), 2 * M_HEADS * M_QK // M_V + h))
    o_in = pl.BlockSpec((L, M_V), lambda h, c: (cc(c), 2 * M_HEADS * M_QK // M_V + M_HEADS + h))
    gate = pl.BlockSpec((None, None, L, 1), lambda h, c: (h, cc(c), 0, 0))
    scal = pl.BlockSpec((None, 1, 1), lambda h, c: (h, 0, 0))
    hn = pl.BlockSpec((1, M_V), lambda h, c: (0, h))
    hv = pl.BlockSpec((L, M_V), lambda h, c: (cc(c), h))
    Cs = pl.BlockSpec((None, None, M_QK, M_V), lambda h, c: (h, cc(c), 0, 0))
    ns = pl.BlockSpec((None, None, 1, M_QK), lambda h, c: (h, cc(c), 0, 0))
    ms = pl.BlockSpec((None, None, 1, 1), lambda h, c: (h, cc(c), 0, 0))
    return qk, v_in, o_in, gate, scal, hn, hv, Cs, ns, ms


def _mlstm_fwd(q, k, proj, igr, fgr, bi, bf, hn, *, name):
    H, S, _ = q.shape
    NC = S // M_CHUNK
    qk, v_in, o_in, gate, scal, hns, hv, Cs, ns, ms = _mlstm_specs(S, NC, False)

    def body(q_ref, k_ref, v_ref, o_ref, ig_ref, fg_ref, bi_ref, bf_ref, hn_ref,
             gated_ref, C_all, n_all, m_all, C_s, n_s, m_s):
        c = pl.program_id(1)

        @pl.when(c == 0)
        def _():
            C_s[...] = jnp.zeros_like(C_s)
            n_s[...] = jnp.zeros_like(n_s)
            m_s[...] = jnp.zeros_like(m_s)

        C, n, m = C_s[...], n_s[...], m_s[...]
        C_all[...] = C
        n_all[...] = n
        m_all[...] = m
        gated, Cn, nn, mn = _mlstm_chunk(q_ref[...], k_ref[...], v_ref[...], o_ref[...], ig_ref[...], fg_ref[...],
                                         bi_ref[...], bf_ref[...], hn_ref[...], C, n, m)
        gated_ref[...] = gated.astype(gated_ref.dtype)
        C_s[...] = Cn
        n_s[...] = nn
        m_s[...] = mn

    return pl.pallas_call(
        body, name=name, grid=(H, NC),
        in_specs=[qk, qk, v_in, o_in, gate, gate, scal, scal, hns],
        out_specs=[hv, Cs, ns, ms],
        out_shape=[jax.ShapeDtypeStruct((S, H * M_V), MXU_DTYPE),
                   jax.ShapeDtypeStruct((H, NC, M_QK, M_V), F32),
                   jax.ShapeDtypeStruct((H, NC, 1, M_QK), F32),
                   jax.ShapeDtypeStruct((H, NC, 1, 1), F32)],
        scratch_shapes=[pltpu.VMEM((M_QK, M_V), F32), pltpu.VMEM((1, M_QK), F32), pltpu.VMEM((1, 1), F32)],
        compiler_params=_cparams(("parallel", "arbitrary")),
    )(q, k, proj, proj, igr, fgr, bi, bf, hn)


def _mlstm_bwd(q, k, proj, igr, fgr, bi, bf, hn, C_all, n_all, m_all, dgated, *, name):
    H, S, _ = q.shape
    NC = S // M_CHUNK
    qk, v_in, o_in, gate, scal, hns, hv, Cs, ns, ms = _mlstm_specs(S, NC, True)

    def body(q_ref, k_ref, v_ref, o_ref, ig_ref, fg_ref, bi_ref, bf_ref, hn_ref, C_ref, n_ref, m_ref, dg_ref,
             dq_ref, dk_ref, dv_ref, do_ref, dig_ref, dfg_ref, dbi_ref, dbf_ref, dhn_ref, dC_s, dn_s, dm_s):
        c = pl.program_id(1)

        @pl.when(c == 0)
        def _():
            dC_s[...] = jnp.zeros_like(dC_s)
            dn_s[...] = jnp.zeros_like(dn_s)
            dm_s[...] = jnp.zeros_like(dm_s)

        prim = (q_ref[...], k_ref[...], v_ref[...], o_ref[...], ig_ref[...], fg_ref[...],
                bi_ref[...], bf_ref[...], hn_ref[...], C_ref[...], n_ref[...], m_ref[...])
        _, vjp = jax.vjp(_mlstm_chunk, *prim)
        dq, dk, dv, do, dig, dfg, dbi, dbf, dhn, dC, dn, dm = vjp(
            (dg_ref[...].astype(F32), dC_s[...], dn_s[...], dm_s[...]))
        dq_ref[...] = dq.astype(dq_ref.dtype)
        dk_ref[...] = dk.astype(dk_ref.dtype)
        dv_ref[...] = dv.astype(dv_ref.dtype)
        do_ref[...] = do.astype(do_ref.dtype)
        dig_ref[...] = dig
        dfg_ref[...] = dfg
        dC_s[...] = dC
        dn_s[...] = dn
        dm_s[...] = dm

        @pl.when(c == 0)
        def _():
            dbi_ref[...] = dbi
            dbf_ref[...] = dbf
            dhn_ref[...] = dhn

        @pl.when(c > 0)
        def _():
            dbi_ref[...] += dbi
            dbf_ref[...] += dbf
            dhn_ref[...] += dhn

    return pl.pallas_call(
        body, name=name, grid=(H, NC),
        in_specs=[qk, qk, v_in, o_in, gate, gate, scal, scal, hns, Cs, ns, ms, hv],
        out_specs=[qk, qk, hv, hv, gate, gate, scal, scal, hns],
        out_shape=[jax.ShapeDtypeStruct((H, S, M_QK), MXU_DTYPE), jax.ShapeDtypeStruct((H, S, M_QK), MXU_DTYPE),
                   jax.ShapeDtypeStruct((S, H * M_V), MXU_DTYPE), jax.ShapeDtypeStruct((S, H * M_V), MXU_DTYPE),
                   jax.ShapeDtypeStruct((H, NC, M_CHUNK, 1), F32), jax.ShapeDtypeStruct((H, NC, M_CHUNK, 1), F32),
                   jax.ShapeDtypeStruct((H, 1, 1), F32), jax.ShapeDtypeStruct((H, 1, 1), F32),
                   jax.ShapeDtypeStruct((1, H * M_V), F32)],
        scratch_shapes=[pltpu.VMEM((M_QK, M_V), F32), pltpu.VMEM((1, M_QK), F32), pltpu.VMEM((1, 1), F32)],
        compiler_params=_cparams(("parallel", "arbitrary")),
    )(q, k, proj, proj, igr, fgr, bi, bf, hn, C_all, n_all, m_all, dgated)


def _attn_block(qg, kp, kc, vp, vc, sinks, first):
    G, B, dh = qg.shape
    R = G * B
    q2 = qg.reshape(R, dh) * (dh ** -0.5)
    kk = jnp.concatenate([kp, kc], axis=0)
    vv = jnp.concatenate([vp, vc], axis=0)
    s = _dot(q2, kk, 1, 1)
    qi = lax.broadcasted_iota(jnp.int32, (R, 2 * B), 0) & (B - 1)
    ku = lax.broadcasted_iota(jnp.int32, (R, 2 * B), 1)
    diff = qi - (ku - B)
    mask = (diff >= 0) & (diff < B) & ((ku >= B) | jnp.logical_not(first))
    s = jnp.where(mask, s, -jnp.inf)
    onehot = (lax.shift_right_logical(lax.broadcasted_iota(jnp.int32, (R, G), 0), B.bit_length() - 1)
              == lax.broadcasted_iota(jnp.int32, (R, G), 1))
    sink = jnp.sum(jnp.where(onehot, sinks, 0.0), axis=1, keepdims=True)
    mx = jnp.maximum(jnp.max(s, axis=-1, keepdims=True), sink)
    p = jnp.exp(s - mx)
    p = p / (jnp.sum(p, axis=-1, keepdims=True) + jnp.exp(sink - mx))
    o = _dot(p, vv, 1, 0)
    return o.reshape(G, B, dh)


def _attn_fwd(q, k, v, sinks, *, name):
    Hq, S, dh = q.shape
    NB = S // A_BLK
    qs = pl.BlockSpec((A_G, A_BLK, dh), lambda h, n: (h, n, 0))
    cur = pl.BlockSpec((None, A_BLK, dh), lambda h, n: (h, n, 0))
    prev = pl.BlockSpec((None, A_BLK, dh), lambda h, n: (h, jnp.maximum(n - 1, 0), 0))
    sk = pl.BlockSpec((None, 1, A_G), lambda h, n: (h, 0, 0))

    def body(q_ref, kp_ref, kc_ref, vp_ref, vc_ref, s_ref, o_ref):
        first = pl.program_id(1) == 0
        o = _attn_block(q_ref[...], kp_ref[...], kc_ref[...], vp_ref[...], vc_ref[...], s_ref[...], first)
        o_ref[...] = o.astype(o_ref.dtype)

    return pl.pallas_call(
        body, name=name, grid=(A_KVH, NB),
        in_specs=[qs, prev, cur, prev, cur, sk], out_specs=qs,
        out_shape=jax.ShapeDtypeStruct((Hq, S, dh), MXU_DTYPE),
        compiler_params=_cparams(("parallel", "parallel")),
    )(q, k, k, v, v, sinks)


def _attn_bwd(q, k, v, sinks, do, *, name):
    Hq, S, dh = q.shape
    NB = S // A_BLK
    last = NB - 1
    qs = pl.BlockSpec((A_G, A_BLK, dh), lambda h, n: (h, jnp.minimum(n, last), 0))
    cur = pl.BlockSpec((None, A_BLK, dh), lambda h, n: (h, jnp.minimum(n, last), 0))
    prev = pl.BlockSpec((None, A_BLK, dh), lambda h, n: (h, jnp.clip(n - 1, 0, last), 0))
    sk = pl.BlockSpec((None, 1, A_G), lambda h, n: (h, 0, 0))

    def body(q_ref, kp_ref, kc_ref, vp_ref, vc_ref, s_ref, do_ref,
             dq_ref, dk_ref, dv_ref, ds_ref, dk_s, dv_s):
        n = pl.program_id(1)

        @pl.when(n == 0)
        def _():
            dk_s[...] = jnp.zeros_like(dk_s)
            dv_s[...] = jnp.zeros_like(dv_s)
            ds_ref[...] = jnp.zeros_like(ds_ref)

        @pl.when(n < NB)
        def _():
            first = n == 0
            f = functools.partial(_attn_block, first=first)
            _, vjp = jax.vjp(f, q_ref[...], kp_ref[...], kc_ref[...], vp_ref[...], vc_ref[...], s_ref[...])
            dq, dkp, dkc, dvp, dvc, ds = vjp(do_ref[...].astype(F32))
            dq_ref[...] = dq
            dk_ref[...] = dk_s[...] + dkp
            dv_ref[...] = dv_s[...] + dvp
            dk_s[...] = dkc
            dv_s[...] = dvc
            ds_ref[...] += ds

        @pl.when(n == NB)
        def _():
            dk_ref[...] = dk_s[...]
            dv_ref[...] = dv_s[...]

    lag = pl.BlockSpec((None, A_BLK, dh), lambda h, n: (h, jnp.maximum(n - 1, 0), 0))
    return pl.pallas_call(
        body, name=name, grid=(A_KVH, NB + 1),
        in_specs=[qs, prev, cur, prev, cur, sk, qs],
        out_specs=[qs, lag, lag, sk],
        out_shape=[jax.ShapeDtypeStruct((Hq, S, dh), F32), jax.ShapeDtypeStruct((A_KVH, S, dh), F32),
                   jax.ShapeDtypeStruct((A_KVH, S, dh), F32), jax.ShapeDtypeStruct((A_KVH, 1, A_G), F32)],
        scratch_shapes=[pltpu.VMEM((A_BLK, dh), F32), pltpu.VMEM((A_BLK, dh), F32)],
        compiler_params=_cparams(("parallel", "arbitrary")),
    )(q, k, k, v, v, sinks, do)


HALO = 8


def _shift_rows(t, j):
    if j == 0:
        return t
    return pltpu.roll(t, j % t.shape[0], 0)


def _conv_gate(gate_ext, cw, cb):
    return cb + cw[0:1, :] * _shift_rows(gate_ext, 2) + cw[1:2, :] * _shift_rows(gate_ext, 1) + cw[2:3, :] * gate_ext


def _convgate_fwd(u, cw, cb, *, name):
    S, F2 = u.shape
    F = F2 // 2
    tm = _pick(S, (256, 128))
    hb = tm // HALO
    urow = pl.BlockSpec((tm, F2), lambda i: (i, 0))
    uprev = pl.BlockSpec((HALO, F), lambda i: (jnp.maximum(i * hb - 1, 0), 0))

    def body(u_ref, up_ref, cw_ref, cb_ref, a_ref):
        i = pl.program_id(0)
        gate = u_ref[:, :F]
        val = u_ref[:, F:]
        prev = jnp.where(i > 0, up_ref[...], 0.0)
        gc = _conv_gate(jnp.concatenate([prev, gate], axis=0), cw_ref[...], cb_ref[...])[HALO:]
        a_ref[...] = (gc * jax.nn.sigmoid(gc) * val).astype(a_ref.dtype)

    return pl.pallas_call(
        body, name=name, grid=(S // tm,),
        in_specs=[urow, uprev, pl.BlockSpec((3, F), lambda i: (0, 0)), pl.BlockSpec((1, F), lambda i: (0, 0))],
        out_specs=pl.BlockSpec((tm, F), lambda i: (i, 0)),
        out_shape=jax.ShapeDtypeStruct((S, F), MXU_DTYPE),
        compiler_params=_cparams(("parallel",)),
    )(u, u, cw, cb)


def _convgate_bwd(u, da, cw, cb, *, name):
    S, F2 = u.shape
    F = F2 // 2
    tm = _pick(S, (128,))
    hb = tm // HALO
    nt = S // tm
    nh = S // HALO
    urow = pl.BlockSpec((tm, F2), lambda i: (i, 0))
    uprev = pl.BlockSpec((HALO, F), lambda i: (jnp.maximum(i * hb - 1, 0), 0))
    unext = pl.BlockSpec((HALO, F2), lambda i: (jnp.minimum((i + 1) * hb, nh - 1), 0))
    darow = pl.BlockSpec((tm, F), lambda i: (i, 0))
    danext = pl.BlockSpec((HALO, F), lambda i: (jnp.minimum((i + 1) * hb, nh - 1), 0))

    def body(u_ref, up_ref, un_ref, da_ref, dan_ref, cw_ref, cb_ref, du_ref, dcw_ref, dcb_ref):
        i = pl.program_id(0)
        cwv = cw_ref[...]
        prev = jnp.where(i > 0, up_ref[...], 0.0)
        gate_ext = jnp.concatenate([prev, u_ref[:, :F], un_ref[:, :F]], axis=0)
        val_ext = jnp.concatenate([u_ref[:, F:], un_ref[:, F:]], axis=0)
        da_next = jnp.where(i < nt - 1, dan_ref[...].astype(F32), 0.0)
        da_ext = jnp.concatenate([da_ref[...].astype(F32), da_next], axis=0)
        gc = _conv_gate(gate_ext, cwv, cb_ref[...])[HALO:]
        sg = jax.nn.sigmoid(gc)
        silu = gc * sg
        dval = da_ext * silu
        dgc = da_ext * val_ext * (sg * (1.0 + gc * (1.0 - sg)))
        dgate = cwv[2:3, :] * dgc + cwv[1:2, :] * _shift_rows(dgc, -1) + cwv[0:1, :] * _shift_rows(dgc, -2)
        du_ref[:, :F] = dgate[:tm].astype(du_ref.dtype)
        du_ref[:, F:] = dval[:tm].astype(du_ref.dtype)
        dgc_c = dgc[:tm]
        g0 = gate_ext[HALO:HALO + tm]
        g1 = _shift_rows(gate_ext, 1)[HALO:HALO + tm]
        g2 = _shift_rows(gate_ext, 2)[HALO:HALO + tm]
        dcw = jnp.concatenate([jnp.sum(dgc_c * g2, axis=0, keepdims=True),
                               jnp.sum(dgc_c * g1, axis=0, keepdims=True),
                               jnp.sum(dgc_c * g0, axis=0, keepdims=True)], axis=0)
        dcb = jnp.sum(dgc_c, axis=0, keepdims=True)

        @pl.when(i == 0)
        def _():
            dcw_ref[...] = dcw
            dcb_ref[...] = dcb

        @pl.when(i > 0)
        def _():
            dcw_ref[...] += dcw
            dcb_ref[...] += dcb

    return pl.pallas_call(
        body, name=name, grid=(nt,),
        in_specs=[urow, uprev, unext, darow, danext,
                  pl.BlockSpec((3, F), lambda i: (0, 0)), pl.BlockSpec((1, F), lambda i: (0, 0))],
        out_specs=[urow, pl.BlockSpec((3, F), lambda i: (0, 0)), pl.BlockSpec((1, F), lambda i: (0, 0))],
        out_shape=[jax.ShapeDtypeStruct((S, F2), MXU_DTYPE), jax.ShapeDtypeStruct((3, F), F32),
                   jax.ShapeDtypeStruct((1, F), F32)],
        compiler_params=_cparams(("arbitrary",)),
    )(u, u, u, da, da, cw, cb)


def _adamw_math(w, g, m, v):
    m = ADAM_B1 * m + (1.0 - ADAM_B1) * g
    v = ADAM_B2 * v + (1.0 - ADAM_B2) * (g * g)
    m_hat = m / (1.0 - ADAM_B1 ** ADAM_STEP)
    v_hat = v / (1.0 - ADAM_B2 ** ADAM_STEP)
    delta = -ADAM_LR * (m_hat / (jnp.sqrt(v_hat) + ADAM_EPS) + ADAM_WD * w)
    return delta, m, v


def _adamw_big(w, g, m, v, *, name):
    R, C = w.shape
    tr = _pick(R, (256, 128, 64, 32, 16, 8))

    def body(w_ref, g_ref, m_ref, v_ref, d_ref, nm_ref, nv_ref):
        d, nm, nv = _adamw_math(w_ref[...], g_ref[...], m_ref[...], v_ref[...])
        d_ref[...] = d
        nm_ref[...] = nm
        nv_ref[...] = nv

    blk = pl.BlockSpec((tr, C), lambda i: (i, 0))
    sh = jax.ShapeDtypeStruct((R, C), F32)
    return pl.pallas_call(
        body, name=name, grid=(R // tr,), in_specs=[blk] * 4, out_specs=[blk] * 3, out_shape=[sh] * 3,
        compiler_params=_cparams(("parallel",)),
    )(w, g, m, v)


def _adamw_small(items, *, name):
    n = len(items)

    def body(*refs):
        ins, outs = refs[:4 * n], refs[4 * n:]
        for t in range(n):
            w, g, m, v = (r[...] for r in ins[4 * t:4 * t + 4])
            d, nm, nv = _adamw_math(w, g, m, v)
            outs[3 * t][...] = d
            outs[3 * t + 1][...] = nm
            outs[3 * t + 2][...] = nv

    flat = [a for it in items for a in it]
    out_shape = [jax.ShapeDtypeStruct(it[0].shape, F32) for it in items for _ in range(3)]
    vm = pl.BlockSpec(memory_space=pltpu.VMEM)
    res = pl.pallas_call(body, name=name, in_specs=[vm] * len(flat), out_specs=[vm] * len(out_shape),
                         out_shape=out_shape)(*flat)
    return [tuple(res[3 * t:3 * t + 3]) for t in range(n)]


def _place():
    return lax.axis_index("x"), lax.axis_index("y"), lax.axis_index("c")


_FLIPS = ((1, 0), (0, 1), (1, 1))


def _gather_shards(wp, *, name):
    R = wp.shape[0]
    Hh = R // 2
    any_spec = pl.BlockSpec(memory_space=pl.ANY)

    def body(w_ref, out_ref, send_sems, recv_sems, local_sem):
        x, y, c = _place()
        me = 2 * x + y
        half = pl.ds(pl.multiple_of(c * Hh, 16), Hh)
        sibling = (x, y, 1 - c)
        mine = pltpu.make_async_copy(w_ref, out_ref.at[me], local_sem)
        mine.start()
        chips = [(x ^ fx, y ^ fy) for fx, fy in _FLIPS]

        def over_ici(j):
            px, py = chips[j]
            return pltpu.make_async_remote_copy(
                src_ref=w_ref.at[half], dst_ref=out_ref.at[me, half],
                send_sem=send_sems.at[j], recv_sem=recv_sems.at[j], device_id=(px, py, c), device_id_type=MESH)

        def landed(j):
            px, py = chips[j]
            blk = out_ref.at[2 * px + py, half]
            return pltpu.make_async_remote_copy(
                src_ref=blk, dst_ref=blk, send_sem=send_sems.at[j], recv_sem=recv_sems.at[j],
                device_id=(px, py, c), device_id_type=MESH)

        def to_sibling(j):
            px, py = chips[j]
            blk = out_ref.at[2 * px + py, half]
            return pltpu.make_async_remote_copy(
                src_ref=blk, dst_ref=blk, send_sem=send_sems.at[3 + j], recv_sem=recv_sems.at[3 + j],
                device_id=sibling, device_id_type=MESH)

        def from_sibling(j):
            px, py = chips[j]
            other = pl.ds(pl.multiple_of((1 - c) * Hh, 16), Hh)
            blk = out_ref.at[2 * px + py, other]
            return pltpu.make_async_remote_copy(
                src_ref=blk, dst_ref=blk, send_sem=send_sems.at[3 + j], recv_sem=recv_sems.at[3 + j],
                device_id=sibling, device_id_type=MESH)

        firsts = [over_ici(j) for j in range(3)]
        for cp in firsts:
            cp.start()
        passed = []
        for j in range(3):
            landed(j).wait_recv()
            cp = to_sibling(j)
            cp.start()
            passed.append(cp)
        for j in range(3):
            from_sibling(j).wait_recv()
        for cp in firsts + passed:
            cp.wait_send()
        mine.wait()

    return pl.pallas_call(
        body, name=name, in_specs=[any_spec], out_specs=any_spec,
        out_shape=jax.ShapeDtypeStruct((N_CHIPS, R, LANES), wp.dtype),
        scratch_shapes=[pltpu.SemaphoreType.DMA((6,)), pltpu.SemaphoreType.DMA((6,)), pltpu.SemaphoreType.DMA],
    )(wp)


def _swap_halves(p, *, name):
    _, R, _ = p.shape
    Hh = R // 2
    any_spec = pl.BlockSpec(memory_space=pl.ANY)

    def body(p_ref, out_ref, send_sem, recv_sem):
        x, y, c = _place()
        give = pl.ds(pl.multiple_of((1 - c) * Hh, 8), Hh)
        cp = pltpu.make_async_remote_copy(
            src_ref=p_ref.at[:, give], dst_ref=out_ref, send_sem=send_sem, recv_sem=recv_sem,
            device_id=(x, y, 1 - c), device_id_type=MESH)
        cp.start()
        cp.wait()

    return pl.pallas_call(
        body, name=name, in_specs=[any_spec], out_specs=any_spec,
        out_shape=jax.ShapeDtypeStruct((N_CHIPS, Hh, LANES), p.dtype),
        scratch_shapes=[pltpu.SemaphoreType.DMA, pltpu.SemaphoreType.DMA],
    )(p)


def _add_halves(p, got, c_arr, *, name):
    _, R, _ = p.shape
    Hh = R // 2
    tr = _pick(Hh, (1504, 752, 376, 94 * 8, 8))
    nb = Hh // tr

    def body(c_ref, p_ref, g_ref, o_ref):
        o_ref[...] = (p_ref[...] + g_ref[...]).astype(o_ref.dtype)

    return pl.pallas_call(
        body, name=name,
        grid_spec=pltpu.PrefetchScalarGridSpec(
            num_scalar_prefetch=1, grid=(N_CHIPS, nb),
            in_specs=[pl.BlockSpec((None, tr, LANES), lambda j, i, c_ref: (j, c_ref[0] * nb + i, 0)),
                      pl.BlockSpec((None, tr, LANES), lambda j, i, c_ref: (j, i, 0))],
            out_specs=pl.BlockSpec((None, tr, LANES), lambda j, i, c_ref: (j, i, 0))),
        out_shape=jax.ShapeDtypeStruct((N_CHIPS, Hh, LANES), WIRE_DTYPE),
        compiler_params=_cparams(("parallel", "parallel")),
    )(c_arr, p, got)


def _exchange_chips(qv, *, name):
    any_spec = pl.BlockSpec(memory_space=pl.ANY)

    def body(q_ref, out_ref, send_sems, recv_sems, local_sem):
        x, y, c = _place()
        me = 2 * x + y
        mine = pltpu.make_async_copy(q_ref.at[me], out_ref.at[me], local_sem)
        mine.start()
        chips = [(x ^ fx, y ^ fy) for fx, fy in _FLIPS]
        sends = []
        for j, (px, py) in enumerate(chips):
            cp = pltpu.make_async_remote_copy(
                src_ref=q_ref.at[2 * px + py], dst_ref=out_ref.at[me],
                send_sem=send_sems.at[j], recv_sem=recv_sems.at[j], device_id=(px, py, c), device_id_type=MESH)
            cp.start()
            sends.append(cp)
        for j, (px, py) in enumerate(chips):
            blk = out_ref.at[2 * px + py]
            pltpu.make_async_remote_copy(
                src_ref=blk, dst_ref=blk, send_sem=send_sems.at[j], recv_sem=recv_sems.at[j],
                device_id=(px, py, c), device_id_type=MESH).wait_recv()
        for cp in sends:
            cp.wait_send()
        mine.wait()

    return pl.pallas_call(
        body, name=name, in_specs=[any_spec], out_specs=any_spec,
        out_shape=jax.ShapeDtypeStruct(qv.shape, qv.dtype),
        scratch_shapes=[pltpu.SemaphoreType.DMA((3,)), pltpu.SemaphoreType.DMA((3,)), pltpu.SemaphoreType.DMA],
    )(qv)


def _sum_chips(r2, *, name):
    _, Hh, _ = r2.shape
    tr = _pick(Hh, (1504, 752, 376, 8))

    def body(r_ref, o_ref):
        acc = r_ref[0].astype(F32)
        for j in range(1, N_CHIPS):
            acc = acc + r_ref[j].astype(F32)
        o_ref[...] = acc

    return pl.pallas_call(
        body, name=name, grid=(Hh // tr,),
        in_specs=[pl.BlockSpec((N_CHIPS, tr, LANES), lambda i: (0, i, 0))],
        out_specs=pl.BlockSpec((tr, LANES), lambda i: (i, 0)),
        out_shape=jax.ShapeDtypeStruct((Hh, LANES), F32),
        compiler_params=_cparams(("parallel",)),
    )(r2)


def _join_halves(fh, *, name):
    Hh = fh.shape[0]
    any_spec = pl.BlockSpec(memory_space=pl.ANY)

    def body(f_ref, out_ref, send_sem, recv_sem, local_sem):
        x, y, c = _place()
        my_rows = out_ref.at[pl.ds(pl.multiple_of(c * Hh, 8), Hh)]
        its_rows = out_ref.at[pl.ds(pl.multiple_of((1 - c) * Hh, 8), Hh)]
        mine = pltpu.make_async_copy(f_ref, my_rows, local_sem)
        mine.start()
        cp = pltpu.make_async_remote_copy(
            src_ref=f_ref, dst_ref=my_rows, send_sem=send_sem, recv_sem=recv_sem,
            device_id=(x, y, 1 - c), device_id_type=MESH)
        cp.start()
        pltpu.make_async_remote_copy(
            src_ref=its_rows, dst_ref=its_rows, send_sem=send_sem, recv_sem=recv_sem,
            device_id=(x, y, 1 - c), device_id_type=MESH).wait_recv()
        cp.wait_send()
        mine.wait()

    return pl.pallas_call(
        body, name=name, in_specs=[any_spec], out_specs=any_spec,
        out_shape=jax.ShapeDtypeStruct((2 * Hh, LANES), fh.dtype),
        scratch_shapes=[pltpu.SemaphoreType.DMA, pltpu.SemaphoreType.DMA, pltpu.SemaphoreType.DMA],
    )(fh)


def _allreduce_small(buf, *, name):
    R = buf.shape[0]
    vm = pl.BlockSpec(memory_space=pltpu.VMEM)

    def body(b_ref, o_ref, slots, send_sems, recv_sems):
        x, y, c = _place()
        me = 4 * x + 2 * y + c
        slots[me] = b_ref[...]
        sends = []
        for kk in range(1, 8):
            fx, fy, fc = (kk >> 2) & 1, (kk >> 1) & 1, kk & 1
            cp = pltpu.make_async_remote_copy(
                src_ref=b_ref, dst_ref=slots.at[me], send_sem=send_sems.at[kk - 1], recv_sem=recv_sems.at[kk - 1],
                device_id=(x ^ fx, y ^ fy, c ^ fc), device_id_type=MESH)
            cp.start()
            sends.append(cp)
        for kk in range(1, 8):
            fx, fy, fc = (kk >> 2) & 1, (kk >> 1) & 1, kk & 1
            peer = 4 * (x ^ fx) + 2 * (y ^ fy) + (c ^ fc)
            pltpu.make_async_remote_copy(
                src_ref=b_ref, dst_ref=slots.at[peer], send_sem=send_sems.at[kk - 1], recv_sem=recv_sems.at[kk - 1],
                device_id=(x ^ fx, y ^ fy, c ^ fc), device_id_type=MESH).wait_recv()
        for cp in sends:
            cp.wait_send()
        acc = slots[0]
        for d in range(1, 8):
            acc = acc + slots[d]
        o_ref[...] = acc

    return pl.pallas_call(
        body, name=name, in_specs=[vm], out_specs=vm,
        out_shape=jax.ShapeDtypeStruct((R, LANES), F32),
        scratch_shapes=[pltpu.VMEM((8, R, LANES), F32), pltpu.SemaphoreType.DMA((7,)), pltpu.SemaphoreType.DMA((7,))],
    )(buf)


def _pad_rows(v, mult=8 * LANES):
    flat = v.reshape(-1)
    n = flat.shape[0]
    tot = -(-n // mult) * mult
    return jnp.pad(flat, (0, tot - n)).reshape(-1, LANES)


def _pack_small(parts):
    return jnp.concatenate([_pad_rows(p.astype(F32)) for p in parts], axis=0)


def _unpack_small(buf, shapes):
    out, r = [], 0
    for sh in shapes:
        n = math.prod(sh)
        rows = -(-n // (8 * LANES)) * 8
        out.append(buf[r:r + rows].reshape(-1)[:n].reshape(sh))
        r += rows
    return out


def _cols_to_shards(w):
    *lead, K, N = w.shape
    t = w.reshape(*lead, K, N_CHIPS, N // N_CHIPS)
    return jnp.moveaxis(t, -2, 0)


def _shards_to_cols(t):
    t = jnp.moveaxis(t, 0, -2)
    *lead, K, _, n = t.shape
    return t.reshape(*lead, K, N_CHIPS * n)


def _rows_to_shards(w):
    *lead, K, N = w.shape
    t = w.reshape(*lead, N_CHIPS, K // N_CHIPS, N)
    return jnp.moveaxis(t, -3, 0)


def _shards_to_rows(t):
    t = jnp.moveaxis(t, 0, -3)
    *lead, _, k, N = t.shape
    return t.reshape(*lead, N_CHIPS * k, N)


_BIG = (("m_w_in", True), ("m_w_out", False), ("a_w_in", True), ("a_w_out", False), ("f_w_up", True), ("f_w_down", False))


def _heads_major(t, H):
    S = t.shape[0]
    return t.reshape(S, H, -1).transpose(1, 0, 2)


def _heads_minor(t):
    H, S, d = t.shape
    return t.transpose(1, 0, 2).reshape(S, H * d)


def kernel(x, m_w_in, m_gate_bias, m_head_norm, m_w_out, a_w_in, a_b_in, a_sinks, a_w_out, a_b_out, norm_mix_pre, norm_mix_post, norm_ffn_pre, norm_ffn_post, f_w_up, f_conv_w, f_conv_b, f_w_down, loss_target, m_m_w_in, m_m_gate_bias, m_m_head_norm, m_m_w_out, m_a_w_in, m_a_b_in, m_a_sinks, m_a_w_out, m_a_b_out, m_norm_mix_pre, m_norm_mix_post, m_norm_ffn_pre, m_norm_ffn_post, m_f_w_up, m_f_conv_w, m_f_conv_b, m_f_w_down, v_m_w_in, v_m_gate_bias, v_m_head_norm, v_m_w_out, v_a_w_in, v_a_b_in, v_a_sinks, v_a_w_out, v_a_b_out, v_norm_mix_pre, v_norm_mix_post, v_norm_ffn_pre, v_norm_ffn_post, v_f_w_up, v_f_conv_w, v_f_conv_b, v_f_w_down):
    params = dict(m_w_in=m_w_in, m_gate_bias=m_gate_bias, m_head_norm=m_head_norm, m_w_out=m_w_out, a_w_in=a_w_in,
                  a_b_in=a_b_in, a_sinks=a_sinks, a_w_out=a_w_out, a_b_out=a_b_out, norm_mix_pre=norm_mix_pre,
                  norm_mix_post=norm_mix_post, norm_ffn_pre=norm_ffn_pre, norm_ffn_post=norm_ffn_post, f_w_up=f_w_up,
                  f_conv_w=f_conv_w, f_conv_b=f_conv_b, f_w_down=f_w_down)
    mom1 = dict(m_w_in=m_m_w_in, m_gate_bias=m_m_gate_bias, m_head_norm=m_m_head_norm, m_w_out=m_m_w_out,
                a_w_in=m_a_w_in, a_b_in=m_a_b_in, a_sinks=m_a_sinks, a_w_out=m_a_w_out, a_b_out=m_a_b_out,
                norm_mix_pre=m_norm_mix_pre, norm_mix_post=m_norm_mix_post, norm_ffn_pre=m_norm_ffn_pre,
                norm_ffn_post=m_norm_ffn_post, f_w_up=m_f_w_up, f_conv_w=m_f_conv_w, f_conv_b=m_f_conv_b,
                f_w_down=m_f_w_down)
    mom2 = dict(m_w_in=v_m_w_in, m_gate_bias=v_m_gate_bias, m_head_norm=v_m_head_norm, m_w_out=v_m_w_out,
                a_w_in=v_a_w_in, a_b_in=v_a_b_in, a_sinks=v_a_sinks, a_w_out=v_a_w_out, a_b_out=v_a_b_out,
                norm_mix_pre=v_norm_mix_pre, norm_mix_post=v_norm_mix_post, norm_ffn_pre=v_norm_ffn_pre,
                norm_ffn_post=v_norm_ffn_post, f_w_up=v_f_w_up, f_conv_w=v_f_conv_w, f_conv_b=v_f_conv_b,
                f_w_down=v_f_w_down)
    order = list(params)

    mx, my, mc = _place()
    chip = 2 * mx + my
    h0 = x[0]
    target = loss_target[0]
    S = h0.shape[0]
    NC = S // M_CHUNK

    piece_rows = [math.prod(params[n].shape) // LANES for n, _ in _BIG]
    wp = jnp.concatenate([params[n].astype(MXU_DTYPE).reshape(-1, LANES) for n, _ in _BIG], axis=0)
    wg = _gather_shards(wp, name="gather_weights")
    full, r = {}, 0
    for (n, by_cols), rows in zip(_BIG, piece_rows):
        t = wg[:, r:r + rows].reshape((N_CHIPS,) + params[n].shape)
        full[n] = _shards_to_cols(t) if by_cols else _shards_to_rows(t)
        r += rows

    def in_place(shard, axis):
        width = shard.shape[axis]
        z = jnp.zeros(shard.shape[:axis] + (N_CHIPS * width,) + shard.shape[axis + 1:], F32)
        contrib = jnp.where(mc == 0, shard, 0.0)
        return lax.dynamic_update_slice_in_dim(z, contrib, chip * width, axis)

    sm_in = [in_place(a_b_in, 1), in_place(a_b_out, 1), in_place(f_conv_w, 2)]
    sm_full = _unpack_small(_allreduce_small(_pack_small(sm_in), name="gather_small"), [t.shape for t in sm_in])
    b_in_full, b_out_full, conv_w_full = sm_full

    W_in = full["m_w_in"][0]
    n_main = 2 * M_HEADS * M_QK + M_HEADS * M_V + D_MODEL
    W_main = W_in[:, :n_main]
    W_gate = jnp.pad(W_in[:, n_main:], ((0, 0), (0, LANES - 2 * M_HEADS)))
    W_mout = full["m_w_out"][0]
    A_in, A_out = full["a_w_in"][0], full["a_w_out"][0]
    W_up, W_down = full["f_w_up"], full["f_w_down"]

    grads = {}

    def ffn_fwd(i, h1):
        z1 = _rms_fwd(h1, norm_ffn_pre[i:i + 1], out_dtype=MXU_DTYPE, name=f"ffn_pre_norm{i}")
        u = _mm(z1, W_up[i], name=f"ffn_up{i}")
        a = _convgate_fwd(u, conv_w_full[i], f_conv_b[i:i + 1], name=f"ffn_act{i}")
        zf = _mm(a, W_down[i], name=f"ffn_down{i}")
        h2 = _rms_fwd(zf, norm_ffn_post[i:i + 1], res=h1, out_dtype=F32, name=f"ffn_post_norm{i}")
        return h2, (z1, u, a, zf)

    z0 = _rms_fwd(h0, norm_mix_pre[0:1], out_dtype=MXU_DTYPE, name="mix_pre_norm0")
    proj = _mm(z0, W_main, name="mlstm_proj")
    gates = _mm(z0, W_gate, name="mlstm_gate_proj")
    mq = _heads_major(proj[:, :M_HEADS * M_QK], M_HEADS)
    mk = _heads_major(proj[:, M_HEADS * M_QK:2 * M_HEADS * M_QK], M_HEADS)

    def gate_chunks(t):
        return t.reshape(NC, M_CHUNK, M_HEADS).transpose(2, 0, 1)[..., None]

    igr = gate_chunks(gates[:, :M_HEADS])
    fgr = gate_chunks(gates[:, M_HEADS:2 * M_HEADS])
    bi = m_gate_bias[0, 0].reshape(M_HEADS, 1, 1)
    bf = m_gate_bias[0, 1].reshape(M_HEADS, 1, 1)
    gated, C_all, n_all, m_all = _mlstm_fwd(mq, mk, proj, igr, fgr, bi, bf, m_head_norm, name="mlstm_fwd")
    zm0 = _mm(gated, W_mout, name="mlstm_out")
    h1 = _rms_fwd(zm0, norm_mix_post[0:1], res=h0, out_dtype=F32, name="mix_post_norm0")
    h2, ffn0 = ffn_fwd(0, h1)

    y0 = _rms_fwd(h2, norm_mix_pre[1:2], out_dtype=MXU_DTYPE, name="mix_pre_norm1")
    aproj = _mm(y0, A_in, bias=b_in_full, name="attn_proj")
    nq = A_QH * A_DH
    aq = _heads_major(aproj[:, :nq], A_QH)
    ak = _heads_major(aproj[:, nq:nq + A_KVH * A_DH], A_KVH)
    av = _heads_major(aproj[:, nq + A_KVH * A_DH:], A_KVH)
    sinks = a_sinks.reshape(A_KVH, 1, A_G)
    ao = _heads_minor(_attn_fwd(aq, ak, av, sinks, name="attn_fwd"))
    zm1 = _mm(ao, A_out, bias=b_out_full, name="attn_out")
    h3 = _rms_fwd(zm1, norm_mix_post[1:2], res=h2, out_dtype=F32, name="mix_post_norm1")
    h4, ffn1 = ffn_fwd(1, h3)

    loss_part, dh = _loss_head(h4, target, name="loss_head")
    loss = lax.psum(loss_part[0, 0], ("x", "y", "c"))

    g_post, g_fpre, g_mpost, g_mpre = [None, None], [None, None], [None, None], [None, None]
    dW_up, dW_down, dconv_w, dconv_b = [None, None], [None, None], [None, None], [None, None]

    def ffn_bwd(i, dh2, h1_, saved):
        z1, u, a, zf = saved
        dzf, g_post[i], _ = _rms_bwd(zf, norm_ffn_post[i:i + 1], dh2, out_dtype=MXU_DTYPE, name=f"ffn_post_norm_bwd{i}")
        da = _mm(dzf, W_down[i], tb=True, name=f"ffn_down_dx{i}")
        dW_down[i] = _mm(a, dzf, ta=True, name=f"ffn_down_dw{i}")
        du, dconv_w[i], dconv_b[i] = _convgate_bwd(u, da, conv_w_full[i], f_conv_b[i:i + 1], name=f"ffn_act_bwd{i}")
        dz1 = _mm(du, W_up[i], tb=True, name=f"ffn_up_dx{i}")
        dW_up[i] = _mm(z1, du, ta=True, name=f"ffn_up_dw{i}")
        dh1, g_fpre[i], _ = _rms_bwd(h1_, norm_ffn_pre[i:i + 1], dz1, res=dh2, out_dtype=F32, name=f"ffn_pre_norm_bwd{i}")
        return dh1

    dh3 = ffn_bwd(1, dh, h3, ffn1)
    dzm1, g_mpost[1], db_out = _rms_bwd(zm1, norm_mix_post[1:2], dh3, out_dtype=MXU_DTYPE, name="mix_post_norm_bwd1")
    dao = _heads_major(_mm(dzm1, A_out, tb=True, name="attn_out_dx"), A_QH)
    dA_out = _mm(ao, dzm1, ta=True, name="attn_out_dw")
    daq, dak, dav, dsinks = _attn_bwd(aq, ak, av, sinks, dao, name="attn_bwd")
    daproj = jnp.concatenate([_heads_minor(daq), _heads_minor(dak), _heads_minor(dav)], axis=1)
    daproj_mx, db_in = _cast_colsum(daproj, out_dtype=MXU_DTYPE, name="attn_proj_bias_bwd")
    dy0 = _mm(daproj_mx, A_in, tb=True, name="attn_proj_dx")
    dA_in = _mm(y0, daproj_mx, ta=True, name="attn_proj_dw")
    dh2, g_mpre[1], _ = _rms_bwd(h2, norm_mix_pre[1:2], dy0, res=dh3, out_dtype=F32, name="mix_pre_norm_bwd1")

    dh1 = ffn_bwd(0, dh2, h1, ffn0)
    dzm0, g_mpost[0], _ = _rms_bwd(zm0, norm_mix_post[0:1], dh1, out_dtype=MXU_DTYPE, name="mix_post_norm_bwd0")
    dgated = _mm(dzm0, W_mout, tb=True, name="mlstm_out_dx")
    dW_mout = _mm(gated, dzm0, ta=True, name="mlstm_out_dw")
    dmq, dmk, dmv, dmo, digr, dfgr, dbi, dbf, dhn = _mlstm_bwd(
        mq, mk, proj, igr, fgr, bi, bf, m_head_norm, C_all, n_all, m_all, dgated, name="mlstm_bwd")
    dproj = jnp.concatenate([_heads_minor(dmq), _heads_minor(dmk), dmv, dmo], axis=1)

    def gate_rows(t):
        return t[..., 0].transpose(1, 2, 0).reshape(S, M_HEADS)

    dgates = jnp.concatenate([gate_rows(digr), gate_rows(dfgr)], axis=1)
    dgates = jnp.pad(dgates, ((0, 0), (0, LANES - 2 * M_HEADS))).astype(MXU_DTYPE)
    dz0 = _mm(dproj, W_main, tb=True, add=_mm(dgates, W_gate, tb=True, name="mlstm_gate_proj_dx"), name="mlstm_proj_dx")
    dW_min = jnp.concatenate([_mm(z0, dproj, ta=True, name="mlstm_proj_dw"),
                              _mm(z0, dgates, ta=True, name="mlstm_gate_proj_dw")[:, :2 * M_HEADS]], axis=1)
    grad_x, g_mpre[0], _ = _rms_bwd(h0, norm_mix_pre[0:1], dz0, res=dh1, out_dtype=F32, name="mix_pre_norm_bwd0")

    big_g = dict(m_w_in=dW_min[None], m_w_out=dW_mout[None], a_w_in=dA_in[None], a_w_out=dA_out[None],
                 f_w_up=jnp.stack(dW_up), f_w_down=jnp.stack(dW_down))
    packed = jnp.concatenate(
        [(_cols_to_shards(big_g[n]) if by_cols else _rows_to_shards(big_g[n])).reshape(N_CHIPS, -1, LANES)
         for n, by_cols in _BIG], axis=1)
    got = _swap_halves(packed, name="grad_swap_halves")
    c_arr = jnp.reshape(mc, (1,)).astype(jnp.int32)
    chip_part = _add_halves(packed, got, c_arr, name="grad_add_halves")
    from_chips = _exchange_chips(chip_part, name="grad_exchange_chips")
    my_half = _sum_chips(from_chips, name="grad_sum_chips")
    shard_g = _join_halves(my_half, name="grad_join_halves")
    r = 0
    for (n, _), rows in zip(_BIG, piece_rows):
        grads[n] = shard_g[r:r + rows].reshape(params[n].shape)
        r += rows

    small_g = [
        jnp.stack([dbi.reshape(M_HEADS), dbf.reshape(M_HEADS)])[None],
        dhn,
        dsinks.reshape(1, A_QH),
        db_in, db_out,
        jnp.concatenate(g_mpre), jnp.concatenate(g_mpost), jnp.concatenate(g_fpre), jnp.concatenate(g_post),
        jnp.stack(dconv_w), jnp.concatenate(dconv_b),
    ]
    small_names = ["m_gate_bias", "m_head_norm", "a_sinks", "a_b_in", "a_b_out", "norm_mix_pre", "norm_mix_post",
                   "norm_ffn_pre", "norm_ffn_post", "f_conv_w", "f_conv_b"]
    red = _unpack_small(_allreduce_small(_pack_small(small_g), name="reduce_small"), [t.shape for t in small_g])
    for n, t in zip(small_names, red):
        if n in ("a_b_in", "a_b_out", "f_conv_w"):
            axis = t.ndim - 1
            width = params[n].shape[axis]
            t = lax.dynamic_slice_in_dim(t, chip * width, width, axis)
        grads[n] = t

    deltas, new_m, new_v = {}, {}, {}
    for n, _ in _BIG:
        sh = params[n].shape
        two = lambda t: t.reshape(-1, sh[-1])
        d, nm, nv = _adamw_big(two(params[n]), two(grads[n]), two(mom1[n]), two(mom2[n]), name=f"adamw_{n}")
        deltas[n], new_m[n], new_v[n] = d.reshape(sh), nm.reshape(sh), nv.reshape(sh)
    two = lambda t: t.reshape(-1, t.shape[-1])
    res = _adamw_small([(two(params[n]), two(grads[n]), two(mom1[n]), two(mom2[n])) for n in small_names],
                       name="adamw_small")
    for n, (d, nm, nv) in zip(small_names, res):
        sh = params[n].shape
        deltas[n], new_m[n], new_v[n] = d.reshape(sh), nm.reshape(sh), nv.reshape(sh)

    return (loss, grad_x[None], *[grads[n] for n in order], *[deltas[n] for n in order],
            *[new_m[n] for n in order], *[new_v[n] for n in order])
```

```python
import functools
import math

import jax
import jax.numpy as jnp
from jax import lax
from jax.experimental import pallas as pl
from jax.experimental.pallas import tpu as pltpu

F32 = jnp.float32
MXU_DTYPE = jnp.bfloat16
WIRE_DTYPE = jnp.bfloat16
MESH = pl.DeviceIdType.MESH

D_MODEL = 1024
EPS = 1e-6
M_HEADS, M_QK, M_V, M_CHUNK = 8, 64, 128, 64
GATE_CAP = 15.0
A_DH, A_QH, A_KVH, A_G, A_BLK = 64, 16, 2, 8, 128
D_FF = 2816
N_CHIPS = 4
LANES = 128
VMEM_LIMIT = 48 * 1024 * 1024

ADAM_LR, ADAM_B1, ADAM_B2, ADAM_EPS, ADAM_WD, ADAM_STEP = 0.001, 0.9, 0.999, 1e-08, 0.01, 10


def _cparams(sem):
    return pltpu.CompilerParams(dimension_semantics=sem, vmem_limit_bytes=VMEM_LIMIT)


def _pick(n, cands):
    for c in cands:
        if n % c == 0:
            return c
    return n


def _mm(a, b, *, ta=False, tb=False, out_dtype=F32, bias=None, add=None, name):
    if ta:
        K, M = a.shape
    else:
        M, K = a.shape
    if tb:
        N, Kb = b.shape
    else:
        Kb, N = b.shape
    assert K == Kb, (a.shape, b.shape)
    tm = _pick(M, (1024, 1408, 512, 256, 128))
    tn = _pick(N, (1024, 1408, 1280, 512, 256, 128))
    tk = K if K <= 2816 else _pick(K, (2816, 2048, 1408, 1024, 512, 256, 128))
    nk = K // tk
    dn = (((0 if ta else 1,), (1 if tb else 0,)), ((), ()))
    has_bias, has_add = bias is not None, add is not None

    def body(*refs):
        a_ref, b_ref = refs[0], refs[1]
        pos = 2
        bias_ref = add_ref = None
        if has_bias:
            bias_ref = refs[pos]
            pos += 1
        if has_add:
            add_ref = refs[pos]
            pos += 1
        o_ref = refs[pos]
        acc_ref = refs[pos + 1] if nk > 1 else None

        def finish(r):
            if has_bias:
                r = r + bias_ref[...]
            if has_add:
                r = r + add_ref[...]
            o_ref[...] = r.astype(out_dtype)

        part = lax.dot_general(a_ref[...], b_ref[...], dn, preferred_element_type=F32)
        if nk == 1:
            finish(part)
        else:
            k = pl.program_id(2)

            @pl.when(k == 0)
            def _():
                acc_ref[...] = part

            @pl.when(k > 0)
            def _():
                acc_ref[...] += part

            @pl.when(k == nk - 1)
            def _():
                finish(acc_ref[...])

    a_spec = pl.BlockSpec((tk, tm), lambda i, j, k: (k, i)) if ta else pl.BlockSpec((tm, tk), lambda i, j, k: (i, k))
    b_spec = pl.BlockSpec((tn, tk), lambda i, j, k: (j, k)) if tb else pl.BlockSpec((tk, tn), lambda i, j, k: (k, j))
    in_specs, args = [a_spec, b_spec], [a, b]
    if has_bias:
        in_specs.append(pl.BlockSpec((1, tn), lambda i, j, k: (0, j)))
        args.append(bias)
    if has_add:
        in_specs.append(pl.BlockSpec((tm, tn), lambda i, j, k: (i, j)))
        args.append(add)
    return pl.pallas_call(
        body, name=name,
        grid=(M // tm, N // tn, nk),
        in_specs=in_specs,
        out_specs=pl.BlockSpec((tm, tn), lambda i, j, k: (i, j)),
        out_shape=jax.ShapeDtypeStruct((M, N), out_dtype),
        scratch_shapes=[pltpu.VMEM((tm, tn), F32)] if nk > 1 else [],
        compiler_params=_cparams(("parallel", "parallel", "arbitrary")),
    )(*args)


ROW_TILE = 512


def _rms_fwd(x, g, *, res=None, out_dtype, name):
    S, D = x.shape
    tm = _pick(S, (ROW_TILE, 256, 128))
    has_res = res is not None

    def body(*refs):
        x_ref, g_ref = refs[0], refs[1]
        res_ref = refs[2] if has_res else None
        o_ref = refs[-1]
        xv = x_ref[...]
        y = xv * lax.rsqrt(jnp.mean(xv * xv, axis=-1, keepdims=True) + EPS) * g_ref[...]
        if has_res:
            y = res_ref[...] + y
        o_ref[...] = y.astype(out_dtype)

    row = pl.BlockSpec((tm, D), lambda i: (i, 0))
    in_specs, args = [row, pl.BlockSpec((1, D), lambda i: (0, 0))], [x, g]
    if has_res:
        in_specs.append(row)
        args.append(res)
    return pl.pallas_call(
        body, name=name, grid=(S // tm,), in_specs=in_specs, out_specs=row,
        out_shape=jax.ShapeDtypeStruct((S, D), out_dtype),
        compiler_params=_cparams(("parallel",)),
    )(*args)


def _rms_bwd(x, g, dy, *, res=None, out_dtype, name):
    S, D = x.shape
    tm = _pick(S, (ROW_TILE, 256, 128))
    has_res = res is not None

    def body(*refs):
        x_ref, g_ref, dy_ref = refs[0], refs[1], refs[2]
        res_ref = refs[3] if has_res else None
        dx_ref, dg_ref, cs_ref = refs[-3], refs[-2], refs[-1]
        i = pl.program_id(0)
        xv = x_ref[...]
        dyv = dy_ref[...].astype(F32)
        r = lax.rsqrt(jnp.mean(xv * xv, axis=-1, keepdims=True) + EPS)
        xh = xv * r
        gd = dyv * g_ref[...]
        dx = r * (gd - xh * jnp.mean(gd * xh, axis=-1, keepdims=True))
        cs = jnp.sum(dx, axis=0, keepdims=True)
        if has_res:
            dx = res_ref[...] + dx
        dx_ref[...] = dx.astype(out_dtype)
        dg = jnp.sum(dyv * xh, axis=0, keepdims=True)

        @pl.when(i == 0)
        def _():
            dg_ref[...] = dg
            cs_ref[...] = cs

        @pl.when(i > 0)
        def _():
            dg_ref[...] += dg
            cs_ref[...] += cs

    row = pl.BlockSpec((tm, D), lambda i: (i, 0))
    vec = pl.BlockSpec((1, D), lambda i: (0, 0))
    in_specs, args = [row, vec, row], [x, g, dy]
    if has_res:
        in_specs.append(row)
        args.append(res)
    return pl.pallas_call(
        body, name=name, grid=(S // tm,), in_specs=in_specs, out_specs=[row, vec, vec],
        out_shape=[jax.ShapeDtypeStruct((S, D), out_dtype), jax.ShapeDtypeStruct((1, D), F32),
                   jax.ShapeDtypeStruct((1, D), F32)],
        compiler_params=_cparams(("arbitrary",)),
    )(*args)


def _cast_colsum(x, *, out_dtype, name):
    S, N = x.shape
    tm = _pick(S, (ROW_TILE, 256, 128))

    def body(x_ref, o_ref, cs_ref):
        i = pl.program_id(0)
        xv = x_ref[...]
        o_ref[...] = xv.astype(out_dtype)
        cs = jnp.sum(xv, axis=0, keepdims=True)

        @pl.when(i == 0)
        def _():
            cs_ref[...] = cs

        @pl.when(i > 0)
        def _():
            cs_ref[...] += cs

    row = pl.BlockSpec((tm, N), lambda i: (i, 0))
    vec = pl.BlockSpec((1, N), lambda i: (0, 0))
    return pl.pallas_call(
        body, name=name, grid=(S // tm,), in_specs=[row], out_specs=[row, vec],
        out_shape=[jax.ShapeDtypeStruct((S, N), out_dtype), jax.ShapeDtypeStruct((1, N), F32)],
        compiler_params=_cparams(("arbitrary",)),
    )(x)


def _loss_head(y, target, *, name):
    S, D = y.shape
    tm = _pick(S, (ROW_TILE, 256, 128))

    def body(y_ref, t_ref, loss_ref, dy_ref):
        i = pl.program_id(0)
        err = y_ref[...] - t_ref[...]
        dy_ref[...] = err * (1.0 / D)
        part = 0.5 * jnp.sum(jnp.mean(err * err, axis=-1, keepdims=True), axis=0, keepdims=True)

        @pl.when(i == 0)
        def _():
            loss_ref[...] = part

        @pl.when(i > 0)
        def _():
            loss_ref[...] += part

    row = pl.BlockSpec((tm, D), lambda i: (i, 0))
    return pl.pallas_call(
        body, name=name, grid=(S // tm,), in_specs=[row, row],
        out_specs=[pl.BlockSpec((1, 1), lambda i: (0, 0)), row],
        out_shape=[jax.ShapeDtypeStruct((1, 1), F32), jax.ShapeDtypeStruct((S, D), F32)],
        compiler_params=_cparams(("arbitrary",)),
    )(y, target)


def _mx(t):
    return t.astype(MXU_DTYPE)


def _mxf(t):
    return t.astype(MXU_DTYPE).astype(F32)


def _rawdot(a, b, ca, cb):
    return lax.dot_general(_mx(a), _mx(b), (((ca,), (cb,)), ((), ())), preferred_element_type=F32)


@functools.partial(jax.custom_vjp, nondiff_argnums=(2, 3))
def _dot(a, b, ca, cb):
    return _rawdot(a, b, ca, cb)


def _dot_fwd(a, b, ca, cb):
    return _rawdot(a, b, ca, cb), (a, b)


def _dot_bwd(ca, cb, res, g):
    a, b = res
    bj = 1 if cb == 0 else 0
    ai = 0 if ca == 1 else 1
    da = _rawdot(g, b, 1, bj) if ca == 1 else _rawdot(b, g, bj, 1)
    db = _rawdot(a, g, ai, 0) if cb == 0 else _rawdot(g, a, 0, ai)
    return da.astype(a.dtype), db.astype(b.dtype)


_dot.defvjp(_dot_fwd, _dot_bwd)


def _softcap(z):
    return GATE_CAP * jnp.tanh(z / GATE_CAP)


def _log_sigmoid(z):
    return jnp.minimum(z, 0.0) - jnp.log(1.0 + jnp.exp(-jnp.abs(z)))


def _mlstm_head(h, q, k, v, o, G, bias, hn, C, n, m):
    L = q.shape[0]
    ti = lax.broadcasted_iota(jnp.int32, (L, L), 0)
    si = lax.broadcasted_iota(jnp.int32, (L, L), 1)
    eye = (ti == si).astype(F32)
    lower = (si <= ti)
    lane = lax.broadcasted_iota(jnp.int32, G.shape, 1)
    Gb = G + bias
    ig = _softcap(jnp.sum(jnp.where(lane == h, Gb, 0.0), axis=1, keepdims=True))
    lf = _log_sigmoid(_softcap(jnp.sum(jnp.where(lane == M_HEADS + h, Gb, 0.0), axis=1, keepdims=True)))
    ig_row = jnp.sum(eye * ig, axis=0, keepdims=True)
    lf_row = jnp.sum(eye * lf, axis=0, keepdims=True)
    b = jnp.sum(jnp.where(lower, lf_row, 0.0), axis=1, keepdims=True)
    b_row = jnp.sum(jnp.where(ti <= si, lf, 0.0), axis=0, keepdims=True)
    inter = b + m
    dlog = jnp.where(lower, b - b_row + ig_row, -jnp.inf)
    m_t = jnp.maximum(inter, jnp.max(dlog, axis=-1, keepdims=True))
    qs = q * (M_QK ** -0.5)
    w = _dot(qs, k, 1, 1) * jnp.exp(dlog - m_t)
    s_inter = jnp.exp(inter - m_t)
    num = _dot(w, v, 1, 0) + s_inter * _dot(qs, C, 1, 0)
    den = jnp.sum(w, axis=-1, keepdims=True) + s_inter * jnp.sum(_mxf(qs) * _mxf(n), axis=-1, keepdims=True)
    hout = num / jnp.maximum(jnp.abs(den), jnp.exp(-m_t))
    bL = jnp.sum(lf, axis=0, keepdims=True)
    tail = bL - b + ig
    m_new = jnp.maximum(bL + m, jnp.max(tail, axis=0, keepdims=True))
    ws = jnp.exp(tail - m_new)
    decay = jnp.exp(bL + m - m_new)
    wk = ws * k
    C_new = decay * C + _dot(wk, v, 0, 0)
    n_new = decay * n + jnp.sum(_mxf(ws) * _mxf(k), axis=0, keepdims=True)
    hs = hout * lax.rsqrt(jnp.mean(hout * hout, axis=-1, keepdims=True) + EPS) * hn
    gated = jax.nn.sigmoid(o) * hs
    return gated, C_new, n_new, m_new


M_OFF_Q, M_OFF_K, M_OFF_V, M_OFF_O, M_OFF_G = 0, M_HEADS * LANES, 2 * M_HEADS * LANES, 3 * M_HEADS * LANES, 4 * M_HEADS * LANES
M_PROJ = M_OFF_G + LANES


def _head_cols(off, h):
    return slice(off + h * LANES, off + (h + 1) * LANES)


def _mlstm_specs(NC, rev):
    H, L = M_HEADS, M_CHUNK
    cc = (lambda c: NC - 1 - c) if rev else (lambda c: c)
    proj = pl.BlockSpec((L, M_PROJ), lambda c: (cc(c), 0))
    vec = pl.BlockSpec((1, LANES), lambda c: (0, 0))
    hn = pl.BlockSpec((1, H * M_V), lambda c: (0, 0))
    hv = pl.BlockSpec((L, H * M_V), lambda c: (cc(c), 0))
    Cs = pl.BlockSpec((None, H, LANES, M_V), lambda c: (cc(c), 0, 0, 0))
    ns = pl.BlockSpec((None, H, 1, LANES), lambda c: (cc(c), 0, 0, 0))
    ms = pl.BlockSpec((None, H, 1, 1), lambda c: (cc(c), 0, 0, 0))
    return proj, vec, hn, hv, Cs, ns, ms


_MLSTM_STATE = [pltpu.VMEM((M_HEADS, LANES, M_V), F32), pltpu.VMEM((M_HEADS, 1, LANES), F32),
                pltpu.VMEM((M_HEADS, 1, 1), F32)]


def _mlstm_fwd(proj, bias, hn, *, name):
    S = proj.shape[0]
    H, NC = M_HEADS, S // M_CHUNK
    ps, vec, hns, hv, Cs, ns, ms = _mlstm_specs(NC, False)

    def body(p_ref, b_ref, hn_ref, gated_ref, C_all, n_all, m_all, C_s, n_s, m_s):
        @pl.when(pl.program_id(0) == 0)
        def _():
            C_s[...] = jnp.zeros_like(C_s)
            n_s[...] = jnp.zeros_like(n_s)
            m_s[...] = jnp.zeros_like(m_s)

        G = p_ref[:, M_OFF_G:M_OFF_G + LANES]
        bias_v = b_ref[...]
        for h in range(H):
            C, n, m = C_s[h], n_s[h], m_s[h]
            C_all[h] = C
            n_all[h] = n
            m_all[h] = m
            gated, Cn, nn, mn = _mlstm_head(
                h, p_ref[:, _head_cols(M_OFF_Q, h)], p_ref[:, _head_cols(M_OFF_K, h)], p_ref[:, _head_cols(M_OFF_V, h)],
                p_ref[:, _head_cols(M_OFF_O, h)], G, bias_v, hn_ref[:, _head_cols(0, h)], C, n, m)
            gated_ref[:, _head_cols(0, h)] = gated.astype(gated_ref.dtype)
            C_s[h] = Cn
            n_s[h] = nn
            m_s[h] = mn

    return pl.pallas_call(
        body, name=name, grid=(NC,),
        in_specs=[ps, vec, hns],
        out_specs=[hv, Cs, ns, ms],
        out_shape=[jax.ShapeDtypeStruct((S, H * M_V), MXU_DTYPE),
                   jax.ShapeDtypeStruct((NC, H, LANES, M_V), F32),
                   jax.ShapeDtypeStruct((NC, H, 1, LANES), F32),
                   jax.ShapeDtypeStruct((NC, H, 1, 1), F32)],
        scratch_shapes=list(_MLSTM_STATE),
        compiler_params=_cparams(("arbitrary",)),
    )(proj, bias, hn)


def _mlstm_bwd(proj, bias, hn, C_all, n_all, m_all, dgated, *, name):
    S = proj.shape[0]
    H, NC = M_HEADS, S // M_CHUNK
    ps, vec, hns, hv, Cs, ns, ms = _mlstm_specs(NC, True)

    def body(p_ref, b_ref, hn_ref, C_ref, n_ref, m_ref, dg_ref, dp_ref, db_ref, dhn_ref, dC_s, dn_s, dm_s):
        @pl.when(pl.program_id(0) == 0)
        def _():
            dC_s[...] = jnp.zeros_like(dC_s)
            dn_s[...] = jnp.zeros_like(dn_s)
            dm_s[...] = jnp.zeros_like(dm_s)
            db_ref[...] = jnp.zeros_like(db_ref)
            dhn_ref[...] = jnp.zeros_like(dhn_ref)

        G = p_ref[:, M_OFF_G:M_OFF_G + LANES]
        bias_v = b_ref[...]
        dG = jnp.zeros_like(G)
        dbias = jnp.zeros_like(bias_v)
        for h in range(H):
            prim = (p_ref[:, _head_cols(M_OFF_Q, h)], p_ref[:, _head_cols(M_OFF_K, h)], p_ref[:, _head_cols(M_OFF_V, h)],
                    p_ref[:, _head_cols(M_OFF_O, h)], G, bias_v, hn_ref[:, _head_cols(0, h)], C_ref[h], n_ref[h], m_ref[h])
            _, vjp = jax.vjp(functools.partial(_mlstm_head, h), *prim)
            dq, dk, dv, do, dGh, dbh, dhnh, dC, dn, dm = vjp(
                (dg_ref[:, _head_cols(0, h)].astype(F32), dC_s[h], dn_s[h], dm_s[h]))
            dp_ref[:, _head_cols(M_OFF_Q, h)] = dq.astype(dp_ref.dtype)
            dp_ref[:, _head_cols(M_OFF_K, h)] = dk.astype(dp_ref.dtype)
            dp_ref[:, _head_cols(M_OFF_V, h)] = dv.astype(dp_ref.dtype)
            dp_ref[:, _head_cols(M_OFF_O, h)] = do.astype(dp_ref.dtype)
            dG = dG + dGh
            dbias = dbias + dbh
            dhn_ref[:, _head_cols(0, h)] += dhnh
            dC_s[h] = dC
            dn_s[h] = dn
            dm_s[h] = dm
        dp_ref[:, M_OFF_G:M_OFF_G + LANES] = dG.astype(dp_ref.dtype)
        db_ref[...] += dbias

    return pl.pallas_call(
        body, name=name, grid=(NC,),
        in_specs=[ps, vec, hns, Cs, ns, ms, hv],
        out_specs=[ps, vec, hns],
        out_shape=[jax.ShapeDtypeStruct((S, M_PROJ), MXU_DTYPE), jax.ShapeDtypeStruct((1, LANES), F32),
                   jax.ShapeDtypeStruct((1, H * M_V), F32)],
        scratch_shapes=list(_MLSTM_STATE),
        compiler_params=_cparams(("arbitrary",)),
    )(proj, bias, hn, C_all, n_all, m_all, dgated)


A_NQ = A_QH * A_DH
A_NKV = 2 * A_KVH * A_DH
A_PAIRS = A_G // 2


def _attn_group(Ps, KLO, KHI, VLO, VHI, sinks, first):
    B = Ps[0].shape[0]
    R = len(Ps) * B
    q2 = jnp.concatenate(Ps, axis=0) * (A_DH ** -0.5)
    s = jnp.concatenate([_dot(q2, KLO, 1, 1), _dot(q2, KHI, 1, 1)], axis=0)
    qi = lax.broadcasted_iota(jnp.int32, (2 * R, 2 * B), 0) & (B - 1)
    ku = lax.broadcasted_iota(jnp.int32, (2 * R, 2 * B), 1)
    diff = qi - (ku - B)
    mask = (diff >= 0) & (diff < B) & ((ku >= B) | jnp.logical_not(first))
    s = jnp.where(mask, s, -jnp.inf)
    ri = lax.broadcasted_iota(jnp.int32, (2 * R, A_G), 0)
    head = 2 * lax.shift_right_logical(ri & (R - 1), B.bit_length() - 1) + lax.shift_right_logical(ri, R.bit_length() - 1)
    onehot = head == lax.broadcasted_iota(jnp.int32, (2 * R, A_G), 1)
    sink = jnp.sum(jnp.where(onehot, sinks, 0.0), axis=1, keepdims=True)
    mx = jnp.maximum(jnp.max(s, axis=-1, keepdims=True), sink)
    p = jnp.exp(s - mx)
    p = p / (jnp.sum(p, axis=-1, keepdims=True) + jnp.exp(sink - mx))
    o = _dot(p[:R], VLO, 1, 0) + _dot(p[R:], VHI, 1, 0)
    return tuple(o[j * B:(j + 1) * B] for j in range(len(Ps)))


def _swap_halves_of_lanes(t):
    return pltpu.roll(t, LANES // 2, 1)


def _kv_operands(kvp_ref, kvc_ref, h):
    kk = jnp.concatenate([kvp_ref[:, :LANES], kvc_ref[:, :LANES]], axis=0)
    vv = jnp.concatenate([kvp_ref[:, LANES:], kvc_ref[:, LANES:]], axis=0)
    low = lax.broadcasted_iota(jnp.int32, kk.shape, 1) < A_DH
    own = low if h == 0 else jnp.logical_not(low)
    k_own = jnp.where(own, kk, 0.0)
    v_own = jnp.where(own, vv, 0.0)
    k_oth, v_oth = _swap_halves_of_lanes(k_own), _swap_halves_of_lanes(v_own)
    if h == 0:
        return own, k_own, k_oth, v_own, v_oth
    return own, k_oth, k_own, v_oth, v_own


def _pair_cols(h, j):
    c = (h * A_PAIRS + j) * LANES
    return slice(c, c + LANES)


def _attn_fwd(proj, sinks, *, name):
    S = proj.shape[0]
    NB = S // A_BLK
    kv_blk = A_NQ // A_NKV
    qs = pl.BlockSpec((A_BLK, A_NQ), lambda n: (n, 0))
    cur = pl.BlockSpec((A_BLK, A_NKV), lambda n: (n, kv_blk))
    prev = pl.BlockSpec((A_BLK, A_NKV), lambda n: (jnp.maximum(n - 1, 0), kv_blk))
    sk = pl.BlockSpec((A_KVH, A_G), lambda n: (0, 0))

    def body(q_ref, kvp_ref, kvc_ref, s_ref, o_ref):
        first = pl.program_id(0) == 0
        for h in range(A_KVH):
            _, KLO, KHI, VLO, VHI = _kv_operands(kvp_ref, kvc_ref, h)
            Ps = tuple(q_ref[:, _pair_cols(h, j)] for j in range(A_PAIRS))
            outs = _attn_group(Ps, KLO, KHI, VLO, VHI, s_ref[h:h + 1, :], first)
            for j in range(A_PAIRS):
                o_ref[:, _pair_cols(h, j)] = outs[j].astype(o_ref.dtype)

    return pl.pallas_call(
        body, name=name, grid=(NB,),
        in_specs=[qs, prev, cur, sk], out_specs=qs,
        out_shape=jax.ShapeDtypeStruct((S, A_NQ), MXU_DTYPE),
        compiler_params=_cparams(("parallel",)),
    )(proj, proj, proj, sinks)


def _attn_bwd(proj, sinks, do, *, name):
    S = proj.shape[0]
    NB = S // A_BLK
    last = NB - 1
    kv_blk = A_NQ // A_NKV
    qs = pl.BlockSpec((A_BLK, A_NQ), lambda n: (jnp.minimum(n, last), 0))
    cur = pl.BlockSpec((A_BLK, A_NKV), lambda n: (jnp.minimum(n, last), kv_blk))
    prev = pl.BlockSpec((A_BLK, A_NKV), lambda n: (jnp.clip(n - 1, 0, last), kv_blk))
    sk = pl.BlockSpec((A_KVH, A_G), lambda n: (0, 0))
    lag = pl.BlockSpec((A_BLK, A_NKV), lambda n: (jnp.maximum(n - 1, 0), 0))
    cq_spec = pl.BlockSpec((1, A_NQ), lambda n: (0, 0))
    ckv_spec = pl.BlockSpec((1, A_NKV), lambda n: (0, 0))

    def body(q_ref, kvp_ref, kvc_ref, s_ref, do_ref, dq_ref, dkv_ref, ds_ref, cq_ref, ckv_ref, keep):
        n = pl.program_id(0)

        @pl.when(n == 0)
        def _():
            keep[...] = jnp.zeros_like(keep)
            ds_ref[...] = jnp.zeros_like(ds_ref)
            cq_ref[...] = jnp.zeros_like(cq_ref)
            ckv_ref[...] = jnp.zeros_like(ckv_ref)

        @pl.when(n < NB)
        def _():
            f = functools.partial(_attn_group, first=n == 0)
            dk = jnp.zeros((2 * A_BLK, LANES), F32)
            dv = jnp.zeros((2 * A_BLK, LANES), F32)
            ds_rows = []
            for h in range(A_KVH):
                own, KLO, KHI, VLO, VHI = _kv_operands(kvp_ref, kvc_ref, h)
                Ps = tuple(q_ref[:, _pair_cols(h, j)] for j in range(A_PAIRS))
                _, vjp = jax.vjp(f, Ps, KLO, KHI, VLO, VHI, s_ref[h:h + 1, :])
                dPs, dKLO, dKHI, dVLO, dVHI, dsk = vjp(
                    tuple(do_ref[:, _pair_cols(h, j)].astype(F32) for j in range(A_PAIRS)))
                for j in range(A_PAIRS):
                    dq_ref[:, _pair_cols(h, j)] = dPs[j].astype(dq_ref.dtype)
                    cq_ref[:, _pair_cols(h, j)] += jnp.sum(dPs[j], axis=0, keepdims=True)
                dk_own, dk_oth = (dKLO, dKHI) if h == 0 else (dKHI, dKLO)
                dv_own, dv_oth = (dVLO, dVHI) if h == 0 else (dVHI, dVLO)
                dk = dk + jnp.where(own, dk_own, 0.0) + _swap_halves_of_lanes(jnp.where(own, 0.0, dk_oth))
                dv = dv + jnp.where(own, dv_own, 0.0) + _swap_halves_of_lanes(jnp.where(own, 0.0, dv_oth))
                ds_rows.append(dsk)
            ds_ref[...] += jnp.concatenate(ds_rows, axis=0)
            dkv = jnp.concatenate([dk, dv], axis=1)
            done = keep[...] + dkv[:A_BLK]
            dkv_ref[...] = done.astype(dkv_ref.dtype)
            ckv_ref[...] += jnp.sum(done, axis=0, keepdims=True)
            keep[...] = dkv[A_BLK:]

        @pl.when(n == NB)
        def _():
            done = keep[...]
            dkv_ref[...] = done.astype(dkv_ref.dtype)
            ckv_ref[...] += jnp.sum(done, axis=0, keepdims=True)

    return pl.pallas_call(
        body, name=name, grid=(NB + 1,),
        in_specs=[qs, prev, cur, sk, qs],
        out_specs=[qs, lag, sk, cq_spec, ckv_spec],
        out_shape=[jax.ShapeDtypeStruct((S, A_NQ), MXU_DTYPE), jax.ShapeDtypeStruct((S, A_NKV), MXU_DTYPE),
                   jax.ShapeDtypeStruct((A_KVH, A_G), F32), jax.ShapeDtypeStruct((1, A_NQ), F32),
                   jax.ShapeDtypeStruct((1, A_NKV), F32)],
        scratch_shapes=[pltpu.VMEM((A_BLK, A_NKV), F32)],
        compiler_params=_cparams(("arbitrary",)),
    )(proj, proj, proj, sinks, do)


HALO = 8


def _shift_rows(t, j):
    if j == 0:
        return t
    return pltpu.roll(t, j % t.shape[0], 0)


def _conv_gate(gate_ext, cw, cb):
    return cb + cw[0:1, :] * _shift_rows(gate_ext, 2) + cw[1:2, :] * _shift_rows(gate_ext, 1) + cw[2:3, :] * gate_ext


def _convgate_fwd(u, cw, cb, *, name):
    S, F2 = u.shape
    F = F2 // 2
    tm = _pick(S, (256, 128))
    hb = tm // HALO
    urow = pl.BlockSpec((tm, F2), lambda i: (i, 0))
    uprev = pl.BlockSpec((HALO, F), lambda i: (jnp.maximum(i * hb - 1, 0), 0))

    def body(u_ref, up_ref, cw_ref, cb_ref, a_ref):
        i = pl.program_id(0)
        gate = u_ref[:, :F]
        val = u_ref[:, F:]
        prev = jnp.where(i > 0, up_ref[...], 0.0)
        gc = _conv_gate(jnp.concatenate([prev, gate], axis=0), cw_ref[...], cb_ref[...])[HALO:]
        a_ref[...] = (gc * jax.nn.sigmoid(gc) * val).astype(a_ref.dtype)

    return pl.pallas_call(
        body, name=name, grid=(S // tm,),
        in_specs=[urow, uprev, pl.BlockSpec((3, F), lambda i: (0, 0)), pl.BlockSpec((1, F), lambda i: (0, 0))],
        out_specs=pl.BlockSpec((tm, F), lambda i: (i, 0)),
        out_shape=jax.ShapeDtypeStruct((S, F), MXU_DTYPE),
        compiler_params=_cparams(("parallel",)),
    )(u, u, cw, cb)


def _convgate_bwd(u, da, cw, cb, *, name):
    S, F2 = u.shape
    F = F2 // 2
    tm = _pick(S, (128,))
    hb = tm // HALO
    nt = S // tm
    nh = S // HALO
    urow = pl.BlockSpec((tm, F2), lambda i: (i, 0))
    uprev = pl.BlockSpec((HALO, F), lambda i: (jnp.maximum(i * hb - 1, 0), 0))
    unext = pl.BlockSpec((HALO, F2), lambda i: (jnp.minimum((i + 1) * hb, nh - 1), 0))
    darow = pl.BlockSpec((tm, F), lambda i: (i, 0))
    danext = pl.BlockSpec((HALO, F), lambda i: (jnp.minimum((i + 1) * hb, nh - 1), 0))

    def body(u_ref, up_ref, un_ref, da_ref, dan_ref, cw_ref, cb_ref, du_ref, dcw_ref, dcb_ref):
        i = pl.program_id(0)
        cwv = cw_ref[...]
        prev = jnp.where(i > 0, up_ref[...], 0.0)
        gate_ext = jnp.concatenate([prev, u_ref[:, :F], un_ref[:, :F]], axis=0)
        val_ext = jnp.concatenate([u_ref[:, F:], un_ref[:, F:]], axis=0)
        da_next = jnp.where(i < nt - 1, dan_ref[...].astype(F32), 0.0)
        da_ext = jnp.concatenate([da_ref[...].astype(F32), da_next], axis=0)
        gc = _conv_gate(gate_ext, cwv, cb_ref[...])[HALO:]
        sg = jax.nn.sigmoid(gc)
        silu = gc * sg
        dval = da_ext * silu
        dgc = da_ext * val_ext * (sg * (1.0 + gc * (1.0 - sg)))
        dgate = cwv[2:3, :] * dgc + cwv[1:2, :] * _shift_rows(dgc, -1) + cwv[0:1, :] * _shift_rows(dgc, -2)
        du_ref[:, :F] = dgate[:tm].astype(du_ref.dtype)
        du_ref[:, F:] = dval[:tm].astype(du_ref.dtype)
        dgc_c = dgc[:tm]
        g0 = gate_ext[HALO:HALO + tm]
        g1 = _shift_rows(gate_ext, 1)[HALO:HALO + tm]
        g2 = _shift_rows(gate_ext, 2)[HALO:HALO + tm]
        dcw = jnp.concatenate([jnp.sum(dgc_c * g2, axis=0, keepdims=True),
                               jnp.sum(dgc_c * g1, axis=0, keepdims=True),
                               jnp.sum(dgc_c * g0, axis=0, keepdims=True)], axis=0)
        dcb = jnp.sum(dgc_c, axis=0, keepdims=True)

        @pl.when(i == 0)
        def _():
            dcw_ref[...] = dcw
            dcb_ref[...] = dcb

        @pl.when(i > 0)
        def _():
            dcw_ref[...] += dcw
            dcb_ref[...] += dcb

    return pl.pallas_call(
        body, name=name, grid=(nt,),
        in_specs=[urow, uprev, unext, darow, danext,
                  pl.BlockSpec((3, F), lambda i: (0, 0)), pl.BlockSpec((1, F), lambda i: (0, 0))],
        out_specs=[urow, pl.BlockSpec((3, F), lambda i: (0, 0)), pl.BlockSpec((1, F), lambda i: (0, 0))],
        out_shape=[jax.ShapeDtypeStruct((S, F2), MXU_DTYPE), jax.ShapeDtypeStruct((3, F), F32),
                   jax.ShapeDtypeStruct((1, F), F32)],
        compiler_params=_cparams(("arbitrary",)),
    )(u, u, u, da, da, cw, cb)


def _adamw_math(w, g, m, v):
    m = ADAM_B1 * m + (1.0 - ADAM_B1) * g
    v = ADAM_B2 * v + (1.0 - ADAM_B2) * (g * g)
    m_hat = m / (1.0 - ADAM_B1 ** ADAM_STEP)
    v_hat = v / (1.0 - ADAM_B2 ** ADAM_STEP)
    delta = -ADAM_LR * (m_hat / (jnp.sqrt(v_hat) + ADAM_EPS) + ADAM_WD * w)
    return delta, m, v


def _adamw_big(w, g, m, v, *, name):
    R, C = w.shape
    tr = _pick(R, (256, 128, 64, 32, 16, 8))

    def body(w_ref, g_ref, m_ref, v_ref, d_ref, nm_ref, nv_ref):
        d, nm, nv = _adamw_math(w_ref[...], g_ref[...], m_ref[...], v_ref[...])
        d_ref[...] = d
        nm_ref[...] = nm
        nv_ref[...] = nv

    blk = pl.BlockSpec((tr, C), lambda i: (i, 0))
    sh = jax.ShapeDtypeStruct((R, C), F32)
    return pl.pallas_call(
        body, name=name, grid=(R // tr,), in_specs=[blk] * 4, out_specs=[blk] * 3, out_shape=[sh] * 3,
        compiler_params=_cparams(("parallel",)),
    )(w, g, m, v)


def _adamw_small(items, *, name):
    n = len(items)

    def body(*refs):
        ins, outs = refs[:4 * n], refs[4 * n:]
        for t in range(n):
            w, g, m, v = (r[...] for r in ins[4 * t:4 * t + 4])
            d, nm, nv = _adamw_math(w, g, m, v)
            outs[3 * t][...] = d
            outs[3 * t + 1][...] = nm
            outs[3 * t + 2][...] = nv

    flat = [a for it in items for a in it]
    out_shape = [jax.ShapeDtypeStruct(it[0].shape, F32) for it in items for _ in range(3)]
    vm = pl.BlockSpec(memory_space=pltpu.VMEM)
    res = pl.pallas_call(body, name=name, in_specs=[vm] * len(flat), out_specs=[vm] * len(out_shape),
                         out_shape=out_shape)(*flat)
    return [tuple(res[3 * t:3 * t + 3]) for t in range(n)]


def _place():
    return lax.axis_index("x"), lax.axis_index("y"), lax.axis_index("c")


_FLIPS = ((1, 0), (0, 1), (1, 1))


def _gather_shards(wp, *, name):
    R = wp.shape[0]
    Hh = R // 2
    any_spec = pl.BlockSpec(memory_space=pl.ANY)

    def body(w_ref, out_ref, send_sems, recv_sems, local_sem):
        x, y, c = _place()
        me = 2 * x + y
        half = pl.ds(pl.multiple_of(c * Hh, 16), Hh)
        sibling = (x, y, 1 - c)
        mine = pltpu.make_async_copy(w_ref, out_ref.at[me], local_sem)
        mine.start()
        chips = [(x ^ fx, y ^ fy) for fx, fy in _FLIPS]

        def over_ici(j):
            px, py = chips[j]
            return pltpu.make_async_remote_copy(
                src_ref=w_ref.at[half], dst_ref=out_ref.at[me, half],
                send_sem=send_sems.at[j], recv_sem=recv_sems.at[j], device_id=(px, py, c), device_id_type=MESH)

        def landed(j):
            px, py = chips[j]
            blk = out_ref.at[2 * px + py, half]
            return pltpu.make_async_remote_copy(
                src_ref=blk, dst_ref=blk, send_sem=send_sems.at[j], recv_sem=recv_sems.at[j],
                device_id=(px, py, c), device_id_type=MESH)

        def to_sibling(j):
            px, py = chips[j]
            blk = out_ref.at[2 * px + py, half]
            return pltpu.make_async_remote_copy(
                src_ref=blk, dst_ref=blk, send_sem=send_sems.at[3 + j], recv_sem=recv_sems.at[3 + j],
                device_id=sibling, device_id_type=MESH)

        def from_sibling(j):
            px, py = chips[j]
            other = pl.ds(pl.multiple_of((1 - c) * Hh, 16), Hh)
            blk = out_ref.at[2 * px + py, other]
            return pltpu.make_async_remote_copy(
                src_ref=blk, dst_ref=blk, send_sem=send_sems.at[3 + j], recv_sem=recv_sems.at[3 + j],
                device_id=sibling, device_id_type=MESH)

        firsts = [over_ici(j) for j in range(3)]
        for cp in firsts:
            cp.start()
        passed = []
        for j in range(3):
            landed(j).wait_recv()
            cp = to_sibling(j)
            cp.start()
            passed.append(cp)
        for j in range(3):
            from_sibling(j).wait_recv()
        for cp in firsts + passed:
            cp.wait_send()
        mine.wait()

    return pl.pallas_call(
        body, name=name, in_specs=[any_spec], out_specs=any_spec,
        out_shape=jax.ShapeDtypeStruct((N_CHIPS, R, LANES), wp.dtype),
        scratch_shapes=[pltpu.SemaphoreType.DMA((6,)), pltpu.SemaphoreType.DMA((6,)), pltpu.SemaphoreType.DMA],
    )(wp)


def _swap_halves(p, *, name):
    _, R, _ = p.shape
    Hh = R // 2
    any_spec = pl.BlockSpec(memory_space=pl.ANY)

    def body(p_ref, out_ref, send_sem, recv_sem):
        x, y, c = _place()
        give = pl.ds(pl.multiple_of((1 - c) * Hh, 8), Hh)
        cp = pltpu.make_async_remote_copy(
            src_ref=p_ref.at[:, give], dst_ref=out_ref, send_sem=send_sem, recv_sem=recv_sem,
            device_id=(x, y, 1 - c), device_id_type=MESH)
        cp.start()
        cp.wait()

    return pl.pallas_call(
        body, name=name, in_specs=[any_spec], out_specs=any_spec,
        out_shape=jax.ShapeDtypeStruct((N_CHIPS, Hh, LANES), p.dtype),
        scratch_shapes=[pltpu.SemaphoreType.DMA, pltpu.SemaphoreType.DMA],
    )(p)


def _add_halves(p, got, c_arr, *, name):
    _, R, _ = p.shape
    Hh = R // 2
    tr = _pick(Hh, (1504, 752, 376, 94 * 8, 8))
    nb = Hh // tr

    def body(c_ref, p_ref, g_ref, o_ref):
        o_ref[...] = (p_ref[...] + g_ref[...]).astype(o_ref.dtype)

    return pl.pallas_call(
        body, name=name,
        grid_spec=pltpu.PrefetchScalarGridSpec(
            num_scalar_prefetch=1, grid=(N_CHIPS, nb),
            in_specs=[pl.BlockSpec((None, tr, LANES), lambda j, i, c_ref: (j, c_ref[0] * nb + i, 0)),
                      pl.BlockSpec((None, tr, LANES), lambda j, i, c_ref: (j, i, 0))],
            out_specs=pl.BlockSpec((None, tr, LANES), lambda j, i, c_ref: (j, i, 0))),
        out_shape=jax.ShapeDtypeStruct((N_CHIPS, Hh, LANES), WIRE_DTYPE),
        compiler_params=_cparams(("parallel", "parallel")),
    )(c_arr, p, got)


def _exchange_chips(qv, *, name):
    any_spec = pl.BlockSpec(memory_space=pl.ANY)

    def body(q_ref, out_ref, send_sems, recv_sems, local_sem):
        x, y, c = _place()
        me = 2 * x + y
        mine = pltpu.make_async_copy(q_ref.at[me], out_ref.at[me], local_sem)
        mine.start()
        chips = [(x ^ fx, y ^ fy) for fx, fy in _FLIPS]
        sends = []
        for j, (px, py) in enumerate(chips):
            cp = pltpu.make_async_remote_copy(
                src_ref=q_ref.at[2 * px + py], dst_ref=out_ref.at[me],
                send_sem=send_sems.at[j], recv_sem=recv_sems.at[j], device_id=(px, py, c), device_id_type=MESH)
            cp.start()
            sends.append(cp)
        for j, (px, py) in enumerate(chips):
            blk = out_ref.at[2 * px + py]
            pltpu.make_async_remote_copy(
                src_ref=blk, dst_ref=blk, send_sem=send_sems.at[j], recv_sem=recv_sems.at[j],
                device_id=(px, py, c), device_id_type=MESH).wait_recv()
        for cp in sends:
            cp.wait_send()
        mine.wait()

    return pl.pallas_call(
        body, name=name, in_specs=[any_spec], out_specs=any_spec,
        out_shape=jax.ShapeDtypeStruct(qv.shape, qv.dtype),
        scratch_shapes=[pltpu.SemaphoreType.DMA((3,)), pltpu.SemaphoreType.DMA((3,)), pltpu.SemaphoreType.DMA],
    )(qv)


def _sum_chips(r2, *, name):
    _, Hh, _ = r2.shape
    tr = _pick(Hh, (1504, 752, 376, 8))

    def body(r_ref, o_ref):
        acc = r_ref[0].astype(F32)
        for j in range(1, N_CHIPS):
            acc = acc + r_ref[j].astype(F32)
        o_ref[...] = acc

    return pl.pallas_call(
        body, name=name, grid=(Hh // tr,),
        in_specs=[pl.BlockSpec((N_CHIPS, tr, LANES), lambda i: (0, i, 0))],
        out_specs=pl.BlockSpec((tr, LANES), lambda i: (i, 0)),
        out_shape=jax.ShapeDtypeStruct((Hh, LANES), F32),
        compiler_params=_cparams(("parallel",)),
    )(r2)


def _join_halves(fh, *, name):
    Hh = fh.shape[0]
    any_spec = pl.BlockSpec(memory_space=pl.ANY)

    def body(f_ref, out_ref, send_sem, recv_sem, local_sem):
        x, y, c = _place()
        my_rows = out_ref.at[pl.ds(pl.multiple_of(c * Hh, 8), Hh)]
        its_rows = out_ref.at[pl.ds(pl.multiple_of((1 - c) * Hh, 8), Hh)]
        mine = pltpu.make_async_copy(f_ref, my_rows, local_sem)
        mine.start()
        cp = pltpu.make_async_remote_copy(
            src_ref=f_ref, dst_ref=my_rows, send_sem=send_sem, recv_sem=recv_sem,
            device_id=(x, y, 1 - c), device_id_type=MESH)
        cp.start()
        pltpu.make_async_remote_copy(
            src_ref=its_rows, dst_ref=its_rows, send_sem=send_sem, recv_sem=recv_sem,
            device_id=(x, y, 1 - c), device_id_type=MESH).wait_recv()
        cp.wait_send()
        mine.wait()

    return pl.pallas_call(
        body, name=name, in_specs=[any_spec], out_specs=any_spec,
        out_shape=jax.ShapeDtypeStruct((2 * Hh, LANES), fh.dtype),
        scratch_shapes=[pltpu.SemaphoreType.DMA, pltpu.SemaphoreType.DMA, pltpu.SemaphoreType.DMA],
    )(fh)


def _allreduce_small(buf, *, name):
    R = buf.shape[0]
    vm = pl.BlockSpec(memory_space=pltpu.VMEM)

    def body(b_ref, o_ref, slots, send_sems, recv_sems):
        x, y, c = _place()
        me = 4 * x + 2 * y + c
        slots[me] = b_ref[...]
        sends = []
        for kk in range(1, 8):
            fx, fy, fc = (kk >> 2) & 1, (kk >> 1) & 1, kk & 1
            cp = pltpu.make_async_remote_copy(
                src_ref=b_ref, dst_ref=slots.at[me], send_sem=send_sems.at[kk - 1], recv_sem=recv_sems.at[kk - 1],
                device_id=(x ^ fx, y ^ fy, c ^ fc), device_id_type=MESH)
            cp.start()
            sends.append(cp)
        for kk in range(1, 8):
            fx, fy, fc = (kk >> 2) & 1, (kk >> 1) & 1, kk & 1
            peer = 4 * (x ^ fx) + 2 * (y ^ fy) + (c ^ fc)
            pltpu.make_async_remote_copy(
                src_ref=b_ref, dst_ref=slots.at[peer], send_sem=send_sems.at[kk - 1], recv_sem=recv_sems.at[kk - 1],
                device_id=(x ^ fx, y ^ fy, c ^ fc), device_id_type=MESH).wait_recv()
        for cp in sends:
            cp.wait_send()
        acc = slots[0]
        for d in range(1, 8):
            acc = acc + slots[d]
        o_ref[...] = acc

    return pl.pallas_call(
        body, name=name, in_specs=[vm], out_specs=vm,
        out_shape=jax.ShapeDtypeStruct((R, LANES), F32),
        scratch_shapes=[pltpu.VMEM((8, R, LANES), F32), pltpu.SemaphoreType.DMA((7,)), pltpu.SemaphoreType.DMA((7,))],
    )(buf)


def _pad_rows(v, mult=8 * LANES):
    flat = v.reshape(-1)
    n = flat.shape[0]
    tot = -(-n // mult) * mult
    return jnp.pad(flat, (0, tot - n)).reshape(-1, LANES)


def _pack_small(parts):
    return jnp.concatenate([_pad_rows(p.astype(F32)) for p in parts], axis=0)


def _unpack_small(buf, shapes):
    out, r = [], 0
    for sh in shapes:
        n = math.prod(sh)
        rows = -(-n // (8 * LANES)) * 8
        out.append(buf[r:r + rows].reshape(-1)[:n].reshape(sh))
        r += rows
    return out


def _cols_to_shards(w):
    *lead, K, N = w.shape
    t = w.reshape(*lead, K, N_CHIPS, N // N_CHIPS)
    return jnp.moveaxis(t, -2, 0)


def _shards_to_cols(t):
    t = jnp.moveaxis(t, 0, -2)
    *lead, K, _, n = t.shape
    return t.reshape(*lead, K, N_CHIPS * n)


def _rows_to_shards(w):
    *lead, K, N = w.shape
    t = w.reshape(*lead, N_CHIPS, K // N_CHIPS, N)
    return jnp.moveaxis(t, -3, 0)


def _shards_to_rows(t):
    t = jnp.moveaxis(t, 0, -3)
    *lead, _, k, N = t.shape
    return t.reshape(*lead, N_CHIPS * k, N)


_BIG = (("m_w_in", True), ("m_w_out", False), ("a_w_in", True), ("a_w_out", False), ("f_w_up", True), ("f_w_down", False))


def _heads_major(t, H):
    S = t.shape[0]
    return t.reshape(S, H, -1).transpose(1, 0, 2)


def _heads_minor(t):
    H, S, d = t.shape
    return t.transpose(1, 0, 2).reshape(S, H * d)


def kernel(x, m_w_in, m_gate_bias, m_head_norm, m_w_out, a_w_in, a_b_in, a_sinks, a_w_out, a_b_out, norm_mix_pre, norm_mix_post, norm_ffn_pre, norm_ffn_post, f_w_up, f_conv_w, f_conv_b, f_w_down, loss_target, m_m_w_in, m_m_gate_bias, m_m_head_norm, m_m_w_out, m_a_w_in, m_a_b_in, m_a_sinks, m_a_w_out, m_a_b_out, m_norm_mix_pre, m_norm_mix_post, m_norm_ffn_pre, m_norm_ffn_post, m_f_w_up, m_f_conv_w, m_f_conv_b, m_f_w_down, v_m_w_in, v_m_gate_bias, v_m_head_norm, v_m_w_out, v_a_w_in, v_a_b_in, v_a_sinks, v_a_w_out, v_a_b_out, v_norm_mix_pre, v_norm_mix_post, v_norm_ffn_pre, v_norm_ffn_post, v_f_w_up, v_f_conv_w, v_f_conv_b, v_f_w_down):
    params = dict(m_w_in=m_w_in, m_gate_bias=m_gate_bias, m_head_norm=m_head_norm, m_w_out=m_w_out, a_w_in=a_w_in,
                  a_b_in=a_b_in, a_sinks=a_sinks, a_w_out=a_w_out, a_b_out=a_b_out, norm_mix_pre=norm_mix_pre,
                  norm_mix_post=norm_mix_post, norm_ffn_pre=norm_ffn_pre, norm_ffn_post=norm_ffn_post, f_w_up=f_w_up,
                  f_conv_w=f_conv_w, f_conv_b=f_conv_b, f_w_down=f_w_down)
    mom1 = dict(m_w_in=m_m_w_in, m_gate_bias=m_m_gate_bias, m_head_norm=m_m_head_norm, m_w_out=m_m_w_out,
                a_w_in=m_a_w_in, a_b_in=m_a_b_in, a_sinks=m_a_sinks, a_w_out=m_a_w_out, a_b_out=m_a_b_out,
                norm_mix_pre=m_norm_mix_pre, norm_mix_post=m_norm_mix_post, norm_ffn_pre=m_norm_ffn_pre,
                norm_ffn_post=m_norm_ffn_post, f_w_up=m_f_w_up, f_conv_w=m_f_conv_w, f_conv_b=m_f_conv_b,
                f_w_down=m_f_w_down)
    mom2 = dict(m_w_in=v_m_w_in, m_gate_bias=v_m_gate_bias, m_head_norm=v_m_head_norm, m_w_out=v_m_w_out,
                a_w_in=v_a_w_in, a_b_in=v_a_b_in, a_sinks=v_a_sinks, a_w_out=v_a_w_out, a_b_out=v_a_b_out,
                norm_mix_pre=v_norm_mix_pre, norm_mix_post=v_norm_mix_post, norm_ffn_pre=v_norm_ffn_pre,
                norm_ffn_post=v_norm_ffn_post, f_w_up=v_f_w_up, f_conv_w=v_f_conv_w, f_conv_b=v_f_conv_b,
                f_w_down=v_f_w_down)
    order = list(params)

    mx, my, mc = _place()
    chip = 2 * mx + my
    h0 = x[0]
    target = loss_target[0]
    S = h0.shape[0]
    NC = S // M_CHUNK

    piece_rows = [math.prod(params[n].shape) // LANES for n, _ in _BIG]
    wp = jnp.concatenate([params[n].astype(MXU_DTYPE).reshape(-1, LANES) for n, _ in _BIG], axis=0)
    wg = _gather_shards(wp, name="gather_weights")
    full, r = {}, 0
    for (n, by_cols), rows in zip(_BIG, piece_rows):
        t = wg[:, r:r + rows].reshape((N_CHIPS,) + params[n].shape)
        full[n] = _shards_to_cols(t) if by_cols else _shards_to_rows(t)
        r += rows

    def in_place(shard, axis):
        width = shard.shape[axis]
        z = jnp.zeros(shard.shape[:axis] + (N_CHIPS * width,) + shard.shape[axis + 1:], F32)
        contrib = jnp.where(mc == 0, shard, 0.0)
        return lax.dynamic_update_slice_in_dim(z, contrib, chip * width, axis)

    sm_in = [in_place(a_b_in, 1), in_place(a_b_out, 1), in_place(f_conv_w, 2)]
    sm_full = _unpack_small(_allreduce_small(_pack_small(sm_in), name="gather_small"), [t.shape for t in sm_in])
    b_in_full, b_out_full, conv_w_full = sm_full

    W_in = full["m_w_in"][0]
    nqk = M_HEADS * M_QK
    n_main = 2 * nqk + M_HEADS * M_V + D_MODEL

    def pad_heads(w):
        return jnp.pad(w.reshape(D_MODEL, M_HEADS, M_QK), ((0, 0), (0, 0), (0, LANES - M_QK))).reshape(D_MODEL, -1)

    def unpad_heads(w):
        return w.reshape(D_MODEL, M_HEADS, LANES)[:, :, :M_QK].reshape(D_MODEL, nqk)

    W_all = jnp.concatenate([pad_heads(W_in[:, :nqk]), pad_heads(W_in[:, nqk:2 * nqk]), W_in[:, 2 * nqk:n_main],
                             jnp.pad(W_in[:, n_main:], ((0, 0), (0, LANES - 2 * M_HEADS)))], axis=1)
    gbias = jnp.pad(m_gate_bias[0].reshape(1, 2 * M_HEADS), ((0, 0), (0, LANES - 2 * M_HEADS)))
    W_mout = full["m_w_out"][0]
    A_in, A_out = full["a_w_in"][0], full["a_w_out"][0]
    W_up, W_down = full["f_w_up"], full["f_w_down"]

    grads = {}

    def ffn_fwd(i, h1):
        z1 = _rms_fwd(h1, norm_ffn_pre[i:i + 1], out_dtype=MXU_DTYPE, name=f"ffn_pre_norm{i}")
        u = _mm(z1, W_up[i], name=f"ffn_up{i}")
        a = _convgate_fwd(u, conv_w_full[i], f_conv_b[i:i + 1], name=f"ffn_act{i}")
        zf = _mm(a, W_down[i], name=f"ffn_down{i}")
        h2 = _rms_fwd(zf, norm_ffn_post[i:i + 1], res=h1, out_dtype=F32, name=f"ffn_post_norm{i}")
        return h2, (z1, u, a, zf)

    z0 = _rms_fwd(h0, norm_mix_pre[0:1], out_dtype=MXU_DTYPE, name="mix_pre_norm0")
    proj = _mm(z0, W_all, name="mlstm_proj")
    gated, C_all, n_all, m_all = _mlstm_fwd(proj, gbias, m_head_norm, name="mlstm_fwd")
    zm0 = _mm(gated, W_mout, name="mlstm_out")
    h1 = _rms_fwd(zm0, norm_mix_post[0:1], res=h0, out_dtype=F32, name="mix_post_norm0")
    h2, ffn0 = ffn_fwd(0, h1)

    y0 = _rms_fwd(h2, norm_mix_pre[1:2], out_dtype=MXU_DTYPE, name="mix_pre_norm1")
    aproj = _mm(y0, A_in, bias=b_in_full, name="attn_proj")
    sinks = a_sinks.reshape(A_KVH, A_G)
    ao = _attn_fwd(aproj, sinks, name="attn_fwd")
    zm1 = _mm(ao, A_out, bias=b_out_full, name="attn_out")
    h3 = _rms_fwd(zm1, norm_mix_post[1:2], res=h2, out_dtype=F32, name="mix_post_norm1")
    h4, ffn1 = ffn_fwd(1, h3)

    loss_part, dh = _loss_head(h4, target, name="loss_head")
    loss = lax.psum(loss_part[0, 0], ("x", "y", "c"))

    g_post, g_fpre, g_mpost, g_mpre = [None, None], [None, None], [None, None], [None, None]
    dW_up, dW_down, dconv_w, dconv_b = [None, None], [None, None], [None, None], [None, None]

    def ffn_bwd(i, dh2, h1_, saved):
        z1, u, a, zf = saved
        dzf, g_post[i], _ = _rms_bwd(zf, norm_ffn_post[i:i + 1], dh2, out_dtype=MXU_DTYPE, name=f"ffn_post_norm_bwd{i}")
        da = _mm(dzf, W_down[i], tb=True, name=f"ffn_down_dx{i}")
        dW_down[i] = _mm(a, dzf, ta=True, name=f"ffn_down_dw{i}")
        du, dconv_w[i], dconv_b[i] = _convgate_bwd(u, da, conv_w_full[i], f_conv_b[i:i + 1], name=f"ffn_act_bwd{i}")
        dz1 = _mm(du, W_up[i], tb=True, name=f"ffn_up_dx{i}")
        dW_up[i] = _mm(z1, du, ta=True, name=f"ffn_up_dw{i}")
        dh1, g_fpre[i], _ = _rms_bwd(h1_, norm_ffn_pre[i:i + 1], dz1, res=dh2, out_dtype=F32, name=f"ffn_pre_norm_bwd{i}")
        return dh1

    dh3 = ffn_bwd(1, dh, h3, ffn1)
    dzm1, g_mpost[1], db_out = _rms_bwd(zm1, norm_mix_post[1:2], dh3, out_dtype=MXU_DTYPE, name="mix_post_norm_bwd1")
    dao = _mm(dzm1, A_out, tb=True, name="attn_out_dx")
    dA_out = _mm(ao, dzm1, ta=True, name="attn_out_dw")
    daq, dakv, dsinks, cs_q, cs_kv = _attn_bwd(aproj, sinks, dao, name="attn_bwd")
    db_in = jnp.concatenate([cs_q, cs_kv], axis=1)
    dy0 = _mm(daq, A_in[:, :A_NQ], tb=True, add=_mm(dakv, A_in[:, A_NQ:], tb=True, name="attn_proj_kv_dx"),
              name="attn_proj_q_dx")
    dA_in = jnp.concatenate([_mm(y0, daq, ta=True, name="attn_proj_q_dw"),
                             _mm(y0, dakv, ta=True, name="attn_proj_kv_dw")], axis=1)
    dh2, g_mpre[1], _ = _rms_bwd(h2, norm_mix_pre[1:2], dy0, res=dh3, out_dtype=F32, name="mix_pre_norm_bwd1")

    dh1 = ffn_bwd(0, dh2, h1, ffn0)
    dzm0, g_mpost[0], _ = _rms_bwd(zm0, norm_mix_post[0:1], dh1, out_dtype=MXU_DTYPE, name="mix_post_norm_bwd0")
    dgated = _mm(dzm0, W_mout, tb=True, name="mlstm_out_dx")
    dW_mout = _mm(gated, dzm0, ta=True, name="mlstm_out_dw")
    dproj, dgbias, dhn = _mlstm_bwd(proj, gbias, m_head_norm, C_all, n_all, m_all, dgated, name="mlstm_bwd")
    dz0 = _mm(dproj, W_all, tb=True, name="mlstm_proj_dx")
    dW_all = _mm(z0, dproj, ta=True, name="mlstm_proj_dw")
    dW_min = jnp.concatenate([unpad_heads(dW_all[:, M_OFF_Q:M_OFF_K]), unpad_heads(dW_all[:, M_OFF_K:M_OFF_V]),
                              dW_all[:, M_OFF_V:M_OFF_G], dW_all[:, M_OFF_G:M_OFF_G + 2 * M_HEADS]], axis=1)
    grad_x, g_mpre[0], _ = _rms_bwd(h0, norm_mix_pre[0:1], dz0, res=dh1, out_dtype=F32, name="mix_pre_norm_bwd0")

    big_g = dict(m_w_in=dW_min[None], m_w_out=dW_mout[None], a_w_in=dA_in[None], a_w_out=dA_out[None],
                 f_w_up=jnp.stack(dW_up), f_w_down=jnp.stack(dW_down))
    packed = jnp.concatenate(
        [(_cols_to_shards(big_g[n]) if by_cols else _rows_to_shards(big_g[n])).reshape(N_CHIPS, -1, LANES)
         for n, by_cols in _BIG], axis=1)
    got = _swap_halves(packed, name="grad_swap_halves")
    c_arr = jnp.reshape(mc, (1,)).astype(jnp.int32)
    chip_part = _add_halves(packed, got, c_arr, name="grad_add_halves")
    from_chips = _exchange_chips(chip_part, name="grad_exchange_chips")
    my_half = _sum_chips(from_chips, name="grad_sum_chips")
    shard_g = _join_halves(my_half, name="grad_join_halves")
    r = 0
    for (n, _), rows in zip(_BIG, piece_rows):
        grads[n] = shard_g[r:r + rows].reshape(params[n].shape)
        r += rows

    small_g = [
        dgbias[:, :2 * M_HEADS].reshape(1, 2, M_HEADS),
        dhn,
        dsinks.reshape(1, A_QH),
        db_in, db_out,
        jnp.concatenate(g_mpre), jnp.concatenate(g_mpost), jnp.concatenate(g_fpre), jnp.concatenate(g_post),
        jnp.stack(dconv_w), jnp.concatenate(dconv_b),
    ]
    small_names = ["m_gate_bias", "m_head_norm", "a_sinks", "a_b_in", "a_b_out", "norm_mix_pre", "norm_mix_post",
                   "norm_ffn_pre", "norm_ffn_post", "f_conv_w", "f_conv_b"]
    red = _unpack_small(_allreduce_small(_pack_small(small_g), name="reduce_small"), [t.shape for t in small_g])
    for n, t in zip(small_names, red):
        if n in ("a_b_in", "a_b_out", "f_conv_w"):
            axis = t.ndim - 1
            width = params[n].shape[axis]
            t = lax.dynamic_slice_in_dim(t, chip * width, width, axis)
        grads[n] = t

    deltas, new_m, new_v = {}, {}, {}
    for n, _ in _BIG:
        sh = params[n].shape
        two = lambda t: t.reshape(-1, sh[-1])
        d, nm, nv = _adamw_big(two(params[n]), two(grads[n]), two(mom1[n]), two(mom2[n]), name=f"adamw_{n}")
        deltas[n], new_m[n], new_v[n] = d.reshape(sh), nm.reshape(sh), nv.reshape(sh)
    two = lambda t: t.reshape(-1, t.shape[-1])
    res = _adamw_small([(two(params[n]), two(grads[n]), two(mom1[n]), two(mom2[n])) for n in small_names],
                       name="adamw_small")
    for n, (d, nm, nv) in zip(small_names, res):
        sh = params[n].shape
        deltas[n], new_m[n], new_v[n] = d.reshape(sh), nm.reshape(sh), nv.reshape(sh)

    return (loss, grad_x[None], *[grads[n] for n in order], *[deltas[n] for n in order],
            *[new_m[n] for n in order], *[new_v[n] for n in order])
```

```python
import functools
import math

import jax
import jax.numpy as jnp
from jax import lax
from jax.experimental import pallas as pl
from jax.experimental.pallas import tpu as pltpu

F32 = jnp.float32
MXU_DTYPE = jnp.bfloat16
WIRE_DTYPE = jnp.bfloat16
MESH = pl.DeviceIdType.MESH

D_MODEL = 1024
EPS = 1e-6
M_HEADS, M_QK, M_V, M_CHUNK = 8, 64, 128, 128
GATE_CAP = 15.0
A_DH, A_QH, A_KVH, A_G, A_BLK = 64, 16, 2, 8, 128
D_FF = 2816
N_CHIPS = 4
LANES = 128
VMEM_LIMIT = 48 * 1024 * 1024

ADAM_LR, ADAM_B1, ADAM_B2, ADAM_EPS, ADAM_WD, ADAM_STEP = 0.001, 0.9, 0.999, 1e-08, 0.01, 10


def _cparams(sem):
    return pltpu.CompilerParams(dimension_semantics=sem, vmem_limit_bytes=VMEM_LIMIT)


def _pick(n, cands):
    for c in cands:
        if n % c == 0:
            return c
    return n


def _mm(a, b, *, ta=False, tb=False, out_dtype=F32, bias=None, add=None, name):
    if ta:
        K, M = a.shape
    else:
        M, K = a.shape
    if tb:
        N, Kb = b.shape
    else:
        Kb, N = b.shape
    assert K == Kb, (a.shape, b.shape)
    tm = _pick(M, (1024, 1408, 512, 256, 128))
    tn = _pick(N, (1024, 1408, 1280, 512, 256, 128))
    tk = K if K <= 2816 else _pick(K, (2816, 2048, 1408, 1024, 512, 256, 128))
    nk = K // tk
    dn = (((0 if ta else 1,), (1 if tb else 0,)), ((), ()))
    has_bias, has_add = bias is not None, add is not None

    def body(*refs):
        a_ref, b_ref = refs[0], refs[1]
        pos = 2
        bias_ref = add_ref = None
        if has_bias:
            bias_ref = refs[pos]
            pos += 1
        if has_add:
            add_ref = refs[pos]
            pos += 1
        o_ref = refs[pos]
        acc_ref = refs[pos + 1] if nk > 1 else None

        def finish(r):
            if has_bias:
                r = r + bias_ref[...]
            if has_add:
                r = r + add_ref[...]
            o_ref[...] = r.astype(out_dtype)

        part = lax.dot_general(a_ref[...], b_ref[...], dn, preferred_element_type=F32)
        if nk == 1:
            finish(part)
        else:
            k = pl.program_id(2)

            @pl.when(k == 0)
            def _():
                acc_ref[...] = part

            @pl.when(k > 0)
            def _():
                acc_ref[...] += part

            @pl.when(k == nk - 1)
            def _():
                finish(acc_ref[...])

    a_spec = pl.BlockSpec((tk, tm), lambda i, j, k: (k, i)) if ta else pl.BlockSpec((tm, tk), lambda i, j, k: (i, k))
    b_spec = pl.BlockSpec((tn, tk), lambda i, j, k: (j, k)) if tb else pl.BlockSpec((tk, tn), lambda i, j, k: (k, j))
    in_specs, args = [a_spec, b_spec], [a, b]
    if has_bias:
        in_specs.append(pl.BlockSpec((1, tn), lambda i, j, k: (0, j)))
        args.append(bias)
    if has_add:
        in_specs.append(pl.BlockSpec((tm, tn), lambda i, j, k: (i, j)))
        args.append(add)
    return pl.pallas_call(
        body, name=name,
        grid=(M // tm, N // tn, nk),
        in_specs=in_specs,
        out_specs=pl.BlockSpec((tm, tn), lambda i, j, k: (i, j)),
        out_shape=jax.ShapeDtypeStruct((M, N), out_dtype),
        scratch_shapes=[pltpu.VMEM((tm, tn), F32)] if nk > 1 else [],
        compiler_params=_cparams(("parallel", "parallel", "arbitrary")),
    )(*args)


ROW_TILE = 512


def _rms_fwd(x, g, *, res=None, out_dtype, name):
    S, D = x.shape
    tm = _pick(S, (ROW_TILE, 256, 128))
    has_res = res is not None

    def body(*refs):
        x_ref, g_ref = refs[0], refs[1]
        res_ref = refs[2] if has_res else None
        o_ref = refs[-1]
        xv = x_ref[...]
        y = xv * lax.rsqrt(jnp.mean(xv * xv, axis=-1, keepdims=True) + EPS) * g_ref[...]
        if has_res:
            y = res_ref[...] + y
        o_ref[...] = y.astype(out_dtype)

    row = pl.BlockSpec((tm, D), lambda i: (i, 0))
    in_specs, args = [row, pl.BlockSpec((1, D), lambda i: (0, 0))], [x, g]
    if has_res:
        in_specs.append(row)
        args.append(res)
    return pl.pallas_call(
        body, name=name, grid=(S // tm,), in_specs=in_specs, out_specs=row,
        out_shape=jax.ShapeDtypeStruct((S, D), out_dtype),
        compiler_params=_cparams(("parallel",)),
    )(*args)


def _rms_bwd(x, g, dy, *, res=None, out_dtype, name):
    S, D = x.shape
    tm = _pick(S, (ROW_TILE, 256, 128))
    has_res = res is not None

    def body(*refs):
        x_ref, g_ref, dy_ref = refs[0], refs[1], refs[2]
        res_ref = refs[3] if has_res else None
        dx_ref, dg_ref, cs_ref = refs[-3], refs[-2], refs[-1]
        i = pl.program_id(0)
        xv = x_ref[...]
        dyv = dy_ref[...].astype(F32)
        r = lax.rsqrt(jnp.mean(xv * xv, axis=-1, keepdims=True) + EPS)
        xh = xv * r
        gd = dyv * g_ref[...]
        dx = r * (gd - xh * jnp.mean(gd * xh, axis=-1, keepdims=True))
        cs = jnp.sum(dx, axis=0, keepdims=True)
        if has_res:
            dx = res_ref[...] + dx
        dx_ref[...] = dx.astype(out_dtype)
        dg = jnp.sum(dyv * xh, axis=0, keepdims=True)

        @pl.when(i == 0)
        def _():
            dg_ref[...] = dg
            cs_ref[...] = cs

        @pl.when(i > 0)
        def _():
            dg_ref[...] += dg
            cs_ref[...] += cs

    row = pl.BlockSpec((tm, D), lambda i: (i, 0))
    vec = pl.BlockSpec((1, D), lambda i: (0, 0))
    in_specs, args = [row, vec, row], [x, g, dy]
    if has_res:
        in_specs.append(row)
        args.append(res)
    return pl.pallas_call(
        body, name=name, grid=(S // tm,), in_specs=in_specs, out_specs=[row, vec, vec],
        out_shape=[jax.ShapeDtypeStruct((S, D), out_dtype), jax.ShapeDtypeStruct((1, D), F32),
                   jax.ShapeDtypeStruct((1, D), F32)],
        compiler_params=_cparams(("arbitrary",)),
    )(*args)


def _loss_head(y, target, *, name):
    S, D = y.shape
    tm = _pick(S, (ROW_TILE, 256, 128))

    def body(y_ref, t_ref, loss_ref, dy_ref):
        i = pl.program_id(0)
        err = y_ref[...] - t_ref[...]
        dy_ref[...] = err * (1.0 / D)
        part = 0.5 * jnp.sum(jnp.mean(err * err, axis=-1, keepdims=True), axis=0, keepdims=True)

        @pl.when(i == 0)
        def _():
            loss_ref[...] = part

        @pl.when(i > 0)
        def _():
            loss_ref[...] += part

    row = pl.BlockSpec((tm, D), lambda i: (i, 0))
    return pl.pallas_call(
        body, name=name, grid=(S // tm,), in_specs=[row, row],
        out_specs=[pl.BlockSpec((1, 1), lambda i: (0, 0)), row],
        out_shape=[jax.ShapeDtypeStruct((1, 1), F32), jax.ShapeDtypeStruct((S, D), F32)],
        compiler_params=_cparams(("arbitrary",)),
    )(y, target)


def _mx(t):
    return t.astype(MXU_DTYPE)


def _mxf(t):
    return t.astype(MXU_DTYPE).astype(F32)


def _rawdot(a, b, ca, cb):
    return lax.dot_general(_mx(a), _mx(b), (((ca,), (cb,)), ((), ())), preferred_element_type=F32)


@functools.partial(jax.custom_vjp, nondiff_argnums=(2, 3))
def _dot(a, b, ca, cb):
    return _rawdot(a, b, ca, cb)


def _dot_fwd(a, b, ca, cb):
    return _rawdot(a, b, ca, cb), (a, b)


def _dot_bwd(ca, cb, res, g):
    a, b = res
    bj = 1 if cb == 0 else 0
    ai = 0 if ca == 1 else 1
    da = _rawdot(g, b, 1, bj) if ca == 1 else _rawdot(b, g, bj, 1)
    db = _rawdot(a, g, ai, 0) if cb == 0 else _rawdot(g, a, 0, ai)
    return da.astype(a.dtype), db.astype(b.dtype)


_dot.defvjp(_dot_fwd, _dot_bwd)


def _softcap(z):
    return GATE_CAP * jnp.tanh(z / GATE_CAP)


def _log_sigmoid(z):
    return jnp.minimum(z, 0.0) - jnp.log(1.0 + jnp.exp(-jnp.abs(z)))


def _mlstm_head(h, q, k, v, o, G, bias, hn, C, n, m):
    L = q.shape[0]
    ti = lax.broadcasted_iota(jnp.int32, (L, L), 0)
    si = lax.broadcasted_iota(jnp.int32, (L, L), 1)
    eye = (ti == si).astype(F32)
    lower = (si <= ti)
    lane = lax.broadcasted_iota(jnp.int32, G.shape, 1)
    Gb = G + bias
    ig = _softcap(jnp.sum(jnp.where(lane == h, Gb, 0.0), axis=1, keepdims=True))
    lf = _log_sigmoid(_softcap(jnp.sum(jnp.where(lane == M_HEADS + h, Gb, 0.0), axis=1, keepdims=True)))
    ig_row = jnp.sum(eye * ig, axis=0, keepdims=True)
    lf_row = jnp.sum(eye * lf, axis=0, keepdims=True)
    b = jnp.sum(jnp.where(lower, lf_row, 0.0), axis=1, keepdims=True)
    b_row = jnp.sum(jnp.where(ti <= si, lf, 0.0), axis=0, keepdims=True)
    inter = b + m
    dlog = jnp.where(lower, b - b_row + ig_row, -jnp.inf)
    m_t = jnp.maximum(inter, jnp.max(dlog, axis=-1, keepdims=True))
    qs = q * (M_QK ** -0.5)
    w = _dot(qs, k, 1, 1) * jnp.exp(dlog - m_t)
    s_inter = jnp.exp(inter - m_t)
    num = _dot(w, v, 1, 0) + s_inter * _dot(qs, C, 1, 0)
    den = jnp.sum(w, axis=-1, keepdims=True) + s_inter * jnp.sum(_mxf(qs) * _mxf(n), axis=-1, keepdims=True)
    hout = num / jnp.maximum(jnp.abs(den), jnp.exp(-m_t))
    bL = jnp.sum(lf, axis=0, keepdims=True)
    tail = bL - b + ig
    m_new = jnp.maximum(bL + m, jnp.max(tail, axis=0, keepdims=True))
    ws = jnp.exp(tail - m_new)
    decay = jnp.exp(bL + m - m_new)
    wk = ws * k
    C_new = decay * C + _dot(wk, v, 0, 0)
    n_new = decay * n + jnp.sum(_mxf(ws) * _mxf(k), axis=0, keepdims=True)
    hs = hout * lax.rsqrt(jnp.mean(hout * hout, axis=-1, keepdims=True) + EPS) * hn
    gated = jax.nn.sigmoid(o) * hs
    return gated, C_new, n_new, m_new


M_OFF_Q, M_OFF_K, M_OFF_V, M_OFF_O, M_OFF_G = 0, M_HEADS * LANES, 2 * M_HEADS * LANES, 3 * M_HEADS * LANES, 4 * M_HEADS * LANES
M_PROJ = M_OFF_G + LANES


def _head_cols(off, h):
    return slice(off + h * LANES, off + (h + 1) * LANES)


def _mlstm_specs(NC, rev):
    H, L = M_HEADS, M_CHUNK
    cc = (lambda c: NC - 1 - c) if rev else (lambda c: c)
    proj = pl.BlockSpec((L, M_PROJ), lambda c: (cc(c), 0))
    vec = pl.BlockSpec((1, LANES), lambda c: (0, 0))
    hn = pl.BlockSpec((1, H * M_V), lambda c: (0, 0))
    hv = pl.BlockSpec((L, H * M_V), lambda c: (cc(c), 0))
    Cs = pl.BlockSpec((None, H, LANES, M_V), lambda c: (cc(c), 0, 0, 0))
    ns = pl.BlockSpec((None, H, 1, LANES), lambda c: (cc(c), 0, 0, 0))
    ms = pl.BlockSpec((None, H, 1, 1), lambda c: (cc(c), 0, 0, 0))
    return proj, vec, hn, hv, Cs, ns, ms


_MLSTM_STATE = [pltpu.VMEM((M_HEADS, LANES, M_V), F32), pltpu.VMEM((M_HEADS, 1, LANES), F32),
                pltpu.VMEM((M_HEADS, 1, 1), F32)]


def _mlstm_fwd(proj, bias, hn, *, name):
    S = proj.shape[0]
    H, NC = M_HEADS, S // M_CHUNK
    ps, vec, hns, hv, Cs, ns, ms = _mlstm_specs(NC, False)

    def body(p_ref, b_ref, hn_ref, gated_ref, C_all, n_all, m_all, C_s, n_s, m_s):
        @pl.when(pl.program_id(0) == 0)
        def _():
            C_s[...] = jnp.zeros_like(C_s)
            n_s[...] = jnp.zeros_like(n_s)
            m_s[...] = jnp.zeros_like(m_s)

        G = p_ref[:, M_OFF_G:M_OFF_G + LANES]
        bias_v = b_ref[...]
        for h in range(H):
            C, n, m = C_s[h], n_s[h], m_s[h]
            C_all[h] = C
            n_all[h] = n
            m_all[h] = m
            gated, Cn, nn, mn = _mlstm_head(
                h, p_ref[:, _head_cols(M_OFF_Q, h)], p_ref[:, _head_cols(M_OFF_K, h)], p_ref[:, _head_cols(M_OFF_V, h)],
                p_ref[:, _head_cols(M_OFF_O, h)], G, bias_v, hn_ref[:, _head_cols(0, h)], C, n, m)
            gated_ref[:, _head_cols(0, h)] = gated.astype(gated_ref.dtype)
            C_s[h] = Cn
            n_s[h] = nn
            m_s[h] = mn

    return pl.pallas_call(
        body, name=name, grid=(NC,),
        in_specs=[ps, vec, hns],
        out_specs=[hv, Cs, ns, ms],
        out_shape=[jax.ShapeDtypeStruct((S, H * M_V), MXU_DTYPE),
                   jax.ShapeDtypeStruct((NC, H, LANES, M_V), F32),
                   jax.ShapeDtypeStruct((NC, H, 1, LANES), F32),
                   jax.ShapeDtypeStruct((NC, H, 1, 1), F32)],
        scratch_shapes=list(_MLSTM_STATE),
        compiler_params=_cparams(("arbitrary",)),
    )(proj, bias, hn)


def _mlstm_bwd(proj, bias, hn, C_all, n_all, m_all, dgated, *, name):
    S = proj.shape[0]
    H, NC = M_HEADS, S // M_CHUNK
    ps, vec, hns, hv, Cs, ns, ms = _mlstm_specs(NC, True)

    def body(p_ref, b_ref, hn_ref, C_ref, n_ref, m_ref, dg_ref, dp_ref, db_ref, dhn_ref, dC_s, dn_s, dm_s):
        @pl.when(pl.program_id(0) == 0)
        def _():
            dC_s[...] = jnp.zeros_like(dC_s)
            dn_s[...] = jnp.zeros_like(dn_s)
            dm_s[...] = jnp.zeros_like(dm_s)
            db_ref[...] = jnp.zeros_like(db_ref)
            dhn_ref[...] = jnp.zeros_like(dhn_ref)

        G = p_ref[:, M_OFF_G:M_OFF_G + LANES]
        bias_v = b_ref[...]
        dG = jnp.zeros_like(G)
        dbias = jnp.zeros_like(bias_v)
        for h in range(H):
            prim = (p_ref[:, _head_cols(M_OFF_Q, h)], p_ref[:, _head_cols(M_OFF_K, h)], p_ref[:, _head_cols(M_OFF_V, h)],
                    p_ref[:, _head_cols(M_OFF_O, h)], G, bias_v, hn_ref[:, _head_cols(0, h)], C_ref[h], n_ref[h], m_ref[h])
            _, vjp = jax.vjp(functools.partial(_mlstm_head, h), *prim)
            dq, dk, dv, do, dGh, dbh, dhnh, dC, dn, dm = vjp(
                (dg_ref[:, _head_cols(0, h)].astype(F32), dC_s[h], dn_s[h], dm_s[h]))
            dp_ref[:, _head_cols(M_OFF_Q, h)] = dq.astype(dp_ref.dtype)
            dp_ref[:, _head_cols(M_OFF_K, h)] = dk.astype(dp_ref.dtype)
            dp_ref[:, _head_cols(M_OFF_V, h)] = dv.astype(dp_ref.dtype)
            dp_ref[:, _head_cols(M_OFF_O, h)] = do.astype(dp_ref.dtype)
            dG = dG + dGh
            dbias = dbias + dbh
            dhn_ref[:, _head_cols(0, h)] += dhnh
            dC_s[h] = dC
            dn_s[h] = dn
            dm_s[h] = dm
        dp_ref[:, M_OFF_G:M_OFF_G + LANES] = dG.astype(dp_ref.dtype)
        db_ref[...] += dbias

    return pl.pallas_call(
        body, name=name, grid=(NC,),
        in_specs=[ps, vec, hns, Cs, ns, ms, hv],
        out_specs=[ps, vec, hns],
        out_shape=[jax.ShapeDtypeStruct((S, M_PROJ), MXU_DTYPE), jax.ShapeDtypeStruct((1, LANES), F32),
                   jax.ShapeDtypeStruct((1, H * M_V), F32)],
        scratch_shapes=list(_MLSTM_STATE),
        compiler_params=_cparams(("arbitrary",)),
    )(proj, bias, hn, C_all, n_all, m_all, dgated)


A_NQ = A_QH * A_DH
A_NKV = 2 * A_KVH * A_DH
A_PAIRS = A_G // 2


def _attn_group(Ps, KLO, KHI, VLO, VHI, sinks, first):
    B = Ps[0].shape[0]
    R = len(Ps) * B
    q2 = jnp.concatenate(Ps, axis=0) * (A_DH ** -0.5)
    s = jnp.concatenate([_dot(q2, KLO, 1, 1), _dot(q2, KHI, 1, 1)], axis=0)
    qi = lax.broadcasted_iota(jnp.int32, (2 * R, 2 * B), 0) & (B - 1)
    ku = lax.broadcasted_iota(jnp.int32, (2 * R, 2 * B), 1)
    diff = qi - (ku - B)
    mask = (diff >= 0) & (diff < B) & ((ku >= B) | jnp.logical_not(first))
    s = jnp.where(mask, s, -jnp.inf)
    ri = lax.broadcasted_iota(jnp.int32, (2 * R, A_G), 0)
    head = 2 * lax.shift_right_logical(ri & (R - 1), B.bit_length() - 1) + lax.shift_right_logical(ri, R.bit_length() - 1)
    onehot = head == lax.broadcasted_iota(jnp.int32, (2 * R, A_G), 1)
    sink = jnp.sum(jnp.where(onehot, sinks, 0.0), axis=1, keepdims=True)
    mx = jnp.maximum(jnp.max(s, axis=-1, keepdims=True), sink)
    p = jnp.exp(s - mx)
    p = p / (jnp.sum(p, axis=-1, keepdims=True) + jnp.exp(sink - mx))
    o = _dot(p[:R], VLO, 1, 0) + _dot(p[R:], VHI, 1, 0)
    return tuple(o[j * B:(j + 1) * B] for j in range(len(Ps)))


def _swap_halves_of_lanes(t):
    return pltpu.roll(t, LANES // 2, 1)


def _kv_operands(kvp_ref, kvc_ref, h):
    kk = jnp.concatenate([kvp_ref[:, :LANES], kvc_ref[:, :LANES]], axis=0)
    vv = jnp.concatenate([kvp_ref[:, LANES:], kvc_ref[:, LANES:]], axis=0)
    low = lax.broadcasted_iota(jnp.int32, kk.shape, 1) < A_DH
    own = low if h == 0 else jnp.logical_not(low)
    k_own = jnp.where(own, kk, 0.0)
    v_own = jnp.where(own, vv, 0.0)
    k_oth, v_oth = _swap_halves_of_lanes(k_own), _swap_halves_of_lanes(v_own)
    if h == 0:
        return own, k_own, k_oth, v_own, v_oth
    return own, k_oth, k_own, v_oth, v_own


def _pair_cols(h, j):
    c = (h * A_PAIRS + j) * LANES
    return slice(c, c + LANES)


def _attn_fwd(proj, sinks, *, name):
    S = proj.shape[0]
    NB = S // A_BLK
    kv_blk = A_NQ // A_NKV
    qs = pl.BlockSpec((A_BLK, A_NQ), lambda n: (n, 0))
    cur = pl.BlockSpec((A_BLK, A_NKV), lambda n: (n, kv_blk))
    prev = pl.BlockSpec((A_BLK, A_NKV), lambda n: (jnp.maximum(n - 1, 0), kv_blk))
    sk = pl.BlockSpec((A_KVH, A_G), lambda n: (0, 0))

    def body(q_ref, kvp_ref, kvc_ref, s_ref, o_ref):
        first = pl.program_id(0) == 0
        for h in range(A_KVH):
            _, KLO, KHI, VLO, VHI = _kv_operands(kvp_ref, kvc_ref, h)
            Ps = tuple(q_ref[:, _pair_cols(h, j)] for j in range(A_PAIRS))
            outs = _attn_group(Ps, KLO, KHI, VLO, VHI, s_ref[h:h + 1, :], first)
            for j in range(A_PAIRS):
                o_ref[:, _pair_cols(h, j)] = outs[j].astype(o_ref.dtype)

    return pl.pallas_call(
        body, name=name, grid=(NB,),
        in_specs=[qs, prev, cur, sk], out_specs=qs,
        out_shape=jax.ShapeDtypeStruct((S, A_NQ), MXU_DTYPE),
        compiler_params=_cparams(("parallel",)),
    )(proj, proj, proj, sinks)


def _attn_bwd(proj, sinks, do, *, name):
    S = proj.shape[0]
    NB = S // A_BLK
    last = NB - 1
    kv_blk = A_NQ // A_NKV
    qs = pl.BlockSpec((A_BLK, A_NQ), lambda n: (jnp.minimum(n, last), 0))
    cur = pl.BlockSpec((A_BLK, A_NKV), lambda n: (jnp.minimum(n, last), kv_blk))
    prev = pl.BlockSpec((A_BLK, A_NKV), lambda n: (jnp.clip(n - 1, 0, last), kv_blk))
    sk = pl.BlockSpec((A_KVH, A_G), lambda n: (0, 0))
    lag = pl.BlockSpec((A_BLK, A_NKV), lambda n: (jnp.maximum(n - 1, 0), 0))
    cq_spec = pl.BlockSpec((1, A_NQ), lambda n: (0, 0))
    ckv_spec = pl.BlockSpec((1, A_NKV), lambda n: (0, 0))

    def body(q_ref, kvp_ref, kvc_ref, s_ref, do_ref, dq_ref, dkv_ref, ds_ref, cq_ref, ckv_ref, keep):
        n = pl.program_id(0)

        @pl.when(n == 0)
        def _():
            keep[...] = jnp.zeros_like(keep)
            ds_ref[...] = jnp.zeros_like(ds_ref)
            cq_ref[...] = jnp.zeros_like(cq_ref)
            ckv_ref[...] = jnp.zeros_like(ckv_ref)

        @pl.when(n < NB)
        def _():
            f = functools.partial(_attn_group, first=n == 0)
            dk = jnp.zeros((2 * A_BLK, LANES), F32)
            dv = jnp.zeros((2 * A_BLK, LANES), F32)
            ds_rows = []
            for h in range(A_KVH):
                own, KLO, KHI, VLO, VHI = _kv_operands(kvp_ref, kvc_ref, h)
                Ps = tuple(q_ref[:, _pair_cols(h, j)] for j in range(A_PAIRS))
                _, vjp = jax.vjp(f, Ps, KLO, KHI, VLO, VHI, s_ref[h:h + 1, :])
                dPs, dKLO, dKHI, dVLO, dVHI, dsk = vjp(
                    tuple(do_ref[:, _pair_cols(h, j)].astype(F32) for j in range(A_PAIRS)))
                for j in range(A_PAIRS):
                    dq_ref[:, _pair_cols(h, j)] = dPs[j].astype(dq_ref.dtype)
                    cq_ref[:, _pair_cols(h, j)] += jnp.sum(dPs[j], axis=0, keepdims=True)
                dk_own, dk_oth = (dKLO, dKHI) if h == 0 else (dKHI, dKLO)
                dv_own, dv_oth = (dVLO, dVHI) if h == 0 else (dVHI, dVLO)
                dk = dk + jnp.where(own, dk_own, 0.0) + _swap_halves_of_lanes(jnp.where(own, 0.0, dk_oth))
                dv = dv + jnp.where(own, dv_own, 0.0) + _swap_halves_of_lanes(jnp.where(own, 0.0, dv_oth))
                ds_rows.append(dsk)
            ds_ref[...] += jnp.concatenate(ds_rows, axis=0)
            dkv = jnp.concatenate([dk, dv], axis=1)
            done = keep[...] + dkv[:A_BLK]
            dkv_ref[...] = done.astype(dkv_ref.dtype)
            ckv_ref[...] += jnp.sum(done, axis=0, keepdims=True)
            keep[...] = dkv[A_BLK:]

        @pl.when(n == NB)
        def _():
            done = keep[...]
            dkv_ref[...] = done.astype(dkv_ref.dtype)
            ckv_ref[...] += jnp.sum(done, axis=0, keepdims=True)

    return pl.pallas_call(
        body, name=name, grid=(NB + 1,),
        in_specs=[qs, prev, cur, sk, qs],
        out_specs=[qs, lag, sk, cq_spec, ckv_spec],
        out_shape=[jax.ShapeDtypeStruct((S, A_NQ), MXU_DTYPE), jax.ShapeDtypeStruct((S, A_NKV), MXU_DTYPE),
                   jax.ShapeDtypeStruct((A_KVH, A_G), F32), jax.ShapeDtypeStruct((1, A_NQ), F32),
                   jax.ShapeDtypeStruct((1, A_NKV), F32)],
        scratch_shapes=[pltpu.VMEM((A_BLK, A_NKV), F32)],
        compiler_params=_cparams(("arbitrary",)),
    )(proj, proj, proj, sinks, do)


HALO = 8


def _shift_rows(t, j):
    if j == 0:
        return t
    return pltpu.roll(t, j % t.shape[0], 0)


def _conv_gate(gate_ext, cw, cb):
    return cb + cw[0:1, :] * _shift_rows(gate_ext, 2) + cw[1:2, :] * _shift_rows(gate_ext, 1) + cw[2:3, :] * gate_ext


def _convgate_fwd(u, cw, cb, *, name):
    S, F2 = u.shape
    F = F2 // 2
    tm = _pick(S, (256, 128))
    hb = tm // HALO
    urow = pl.BlockSpec((tm, F2), lambda i: (i, 0))
    uprev = pl.BlockSpec((HALO, F), lambda i: (jnp.maximum(i * hb - 1, 0), 0))

    def body(u_ref, up_ref, cw_ref, cb_ref, a_ref):
        i = pl.program_id(0)
        gate = u_ref[:, :F]
        val = u_ref[:, F:]
        prev = jnp.where(i > 0, up_ref[...], 0.0)
        gc = _conv_gate(jnp.concatenate([prev, gate], axis=0), cw_ref[...], cb_ref[...])[HALO:]
        a_ref[...] = (gc * jax.nn.sigmoid(gc) * val).astype(a_ref.dtype)

    return pl.pallas_call(
        body, name=name, grid=(S // tm,),
        in_specs=[urow, uprev, pl.BlockSpec((3, F), lambda i: (0, 0)), pl.BlockSpec((1, F), lambda i: (0, 0))],
        out_specs=pl.BlockSpec((tm, F), lambda i: (i, 0)),
        out_shape=jax.ShapeDtypeStruct((S, F), MXU_DTYPE),
        compiler_params=_cparams(("parallel",)),
    )(u, u, cw, cb)


def _convgate_bwd(u, da, cw, cb, *, name):
    S, F2 = u.shape
    F = F2 // 2
    tm = _pick(S, (128,))
    hb = tm // HALO
    nt = S // tm
    nh = S // HALO
    urow = pl.BlockSpec((tm, F2), lambda i: (i, 0))
    uprev = pl.BlockSpec((HALO, F), lambda i: (jnp.maximum(i * hb - 1, 0), 0))
    unext = pl.BlockSpec((HALO, F2), lambda i: (jnp.minimum((i + 1) * hb, nh - 1), 0))
    darow = pl.BlockSpec((tm, F), lambda i: (i, 0))
    danext = pl.BlockSpec((HALO, F), lambda i: (jnp.minimum((i + 1) * hb, nh - 1), 0))

    def body(u_ref, up_ref, un_ref, da_ref, dan_ref, cw_ref, cb_ref, du_ref, dcw_ref, dcb_ref):
        i = pl.program_id(0)
        cwv = cw_ref[...]
        prev = jnp.where(i > 0, up_ref[...], 0.0)
        gate_ext = jnp.concatenate([prev, u_ref[:, :F], un_ref[:, :F]], axis=0)
        val_ext = jnp.concatenate([u_ref[:, F:], un_ref[:, F:]], axis=0)
        da_next = jnp.where(i < nt - 1, dan_ref[...].astype(F32), 0.0)
        da_ext = jnp.concatenate([da_ref[...].astype(F32), da_next], axis=0)
        gc = _conv_gate(gate_ext, cwv, cb_ref[...])[HALO:]
        sg = jax.nn.sigmoid(gc)
        silu = gc * sg
        dval = da_ext * silu
        dgc = da_ext * val_ext * (sg * (1.0 + gc * (1.0 - sg)))
        dgate = cwv[2:3, :] * dgc + cwv[1:2, :] * _shift_rows(dgc, -1) + cwv[0:1, :] * _shift_rows(dgc, -2)
        du_ref[:, :F] = dgate[:tm].astype(du_ref.dtype)
        du_ref[:, F:] = dval[:tm].astype(du_ref.dtype)
        dgc_c = dgc[:tm]
        g0 = gate_ext[HALO:HALO + tm]
        g1 = _shift_rows(gate_ext, 1)[HALO:HALO + tm]
        g2 = _shift_rows(gate_ext, 2)[HALO:HALO + tm]
        dcw = jnp.concatenate([jnp.sum(dgc_c * g2, axis=0, keepdims=True),
                               jnp.sum(dgc_c * g1, axis=0, keepdims=True),
                               jnp.sum(dgc_c * g0, axis=0, keepdims=True)], axis=0)
        dcb = jnp.sum(dgc_c, axis=0, keepdims=True)

        @pl.when(i == 0)
        def _():
            dcw_ref[...] = dcw
            dcb_ref[...] = dcb

        @pl.when(i > 0)
        def _():
            dcw_ref[...] += dcw
            dcb_ref[...] += dcb

    return pl.pallas_call(
        body, name=name, grid=(nt,),
        in_specs=[urow, uprev, unext, darow, danext,
                  pl.BlockSpec((3, F), lambda i: (0, 0)), pl.BlockSpec((1, F), lambda i: (0, 0))],
        out_specs=[urow, pl.BlockSpec((3, F), lambda i: (0, 0)), pl.BlockSpec((1, F), lambda i: (0, 0))],
        out_shape=[jax.ShapeDtypeStruct((S, F2), MXU_DTYPE), jax.ShapeDtypeStruct((3, F), F32),
                   jax.ShapeDtypeStruct((1, F), F32)],
        compiler_params=_cparams(("arbitrary",)),
    )(u, u, u, da, da, cw, cb)


def _adamw_math(w, g, m, v):
    m = ADAM_B1 * m + (1.0 - ADAM_B1) * g
    v = ADAM_B2 * v + (1.0 - ADAM_B2) * (g * g)
    m_hat = m / (1.0 - ADAM_B1 ** ADAM_STEP)
    v_hat = v / (1.0 - ADAM_B2 ** ADAM_STEP)
    delta = -ADAM_LR * (m_hat / (jnp.sqrt(v_hat) + ADAM_EPS) + ADAM_WD * w)
    return delta, m, v


def _adamw_big(w, g, m, v, *, name):
    R, C = w.shape
    tr = _pick(R, (256, 128, 64, 32, 16, 8))

    def body(w_ref, g_ref, m_ref, v_ref, d_ref, nm_ref, nv_ref):
        d, nm, nv = _adamw_math(w_ref[...], g_ref[...], m_ref[...], v_ref[...])
        d_ref[...] = d
        nm_ref[...] = nm
        nv_ref[...] = nv

    blk = pl.BlockSpec((tr, C), lambda i: (i, 0))
    sh = jax.ShapeDtypeStruct((R, C), F32)
    return pl.pallas_call(
        body, name=name, grid=(R // tr,), in_specs=[blk] * 4, out_specs=[blk] * 3, out_shape=[sh] * 3,
        compiler_params=_cparams(("parallel",)),
    )(w, g, m, v)


def _adamw_small(items, *, name):
    n = len(items)

    def body(*refs):
        ins, outs = refs[:4 * n], refs[4 * n:]
        for t in range(n):
            w, g, m, v = (r[...] for r in ins[4 * t:4 * t + 4])
            d, nm, nv = _adamw_math(w, g, m, v)
            outs[3 * t][...] = d
            outs[3 * t + 1][...] = nm
            outs[3 * t + 2][...] = nv

    flat = [a for it in items for a in it]
    out_shape = [jax.ShapeDtypeStruct(it[0].shape, F32) for it in items for _ in range(3)]
    vm = pl.BlockSpec(memory_space=pltpu.VMEM)
    res = pl.pallas_call(body, name=name, in_specs=[vm] * len(flat), out_specs=[vm] * len(out_shape),
                         out_shape=out_shape)(*flat)
    return [tuple(res[3 * t:3 * t + 3]) for t in range(n)]


def _place():
    return lax.axis_index("x"), lax.axis_index("y"), lax.axis_index("c")


_FLIPS = ((1, 0), (0, 1), (1, 1))


def _gather_shards(wp, *, name):
    R = wp.shape[0]
    Hh = R // 2
    any_spec = pl.BlockSpec(memory_space=pl.ANY)

    def body(w_ref, out_ref, send_sems, recv_sems):
        x, y, c = _place()
        me = 2 * x + y
        half = pl.ds(pl.multiple_of(c * Hh, 16), Hh)
        sibling = (x, y, 1 - c)
        chips = [(x ^ fx, y ^ fy) for fx, fy in _FLIPS]

        def over_ici(j):
            px, py = chips[j]
            return pltpu.make_async_remote_copy(
                src_ref=w_ref.at[half], dst_ref=out_ref.at[me, half],
                send_sem=send_sems.at[j], recv_sem=recv_sems.at[j], device_id=(px, py, c), device_id_type=MESH)

        def landed(j):
            px, py = chips[j]
            blk = out_ref.at[2 * px + py, half]
            return pltpu.make_async_remote_copy(
                src_ref=blk, dst_ref=blk, send_sem=send_sems.at[j], recv_sem=recv_sems.at[j],
                device_id=(px, py, c), device_id_type=MESH)

        def to_sibling(j):
            px, py = chips[j]
            blk = out_ref.at[2 * px + py, half]
            return pltpu.make_async_remote_copy(
                src_ref=blk, dst_ref=blk, send_sem=send_sems.at[3 + j], recv_sem=recv_sems.at[3 + j],
                device_id=sibling, device_id_type=MESH)

        def from_sibling(j):
            px, py = chips[j]
            other = pl.ds(pl.multiple_of((1 - c) * Hh, 16), Hh)
            blk = out_ref.at[2 * px + py, other]
            return pltpu.make_async_remote_copy(
                src_ref=blk, dst_ref=blk, send_sem=send_sems.at[3 + j], recv_sem=recv_sems.at[3 + j],
                device_id=sibling, device_id_type=MESH)

        firsts = [over_ici(j) for j in range(3)]
        for cp in firsts:
            cp.start()
        passed = []
        for j in range(3):
            landed(j).wait_recv()
            cp = to_sibling(j)
            cp.start()
            passed.append(cp)
        for j in range(3):
            from_sibling(j).wait_recv()
        for cp in firsts + passed:
            cp.wait_send()

    return pl.pallas_call(
        body, name=name, in_specs=[any_spec], out_specs=any_spec,
        out_shape=jax.ShapeDtypeStruct((N_CHIPS, R, LANES), wp.dtype),
        scratch_shapes=[pltpu.SemaphoreType.DMA((6,)), pltpu.SemaphoreType.DMA((6,))],
    )(wp)


def _swap_halves(p, *, name):
    _, R, _ = p.shape
    Hh = R // 2
    any_spec = pl.BlockSpec(memory_space=pl.ANY)

    def body(p_ref, out_ref, send_sem, recv_sem):
        x, y, c = _place()
        give = pl.ds(pl.multiple_of((1 - c) * Hh, 8), Hh)
        cp = pltpu.make_async_remote_copy(
            src_ref=p_ref.at[:, give], dst_ref=out_ref, send_sem=send_sem, recv_sem=recv_sem,
            device_id=(x, y, 1 - c), device_id_type=MESH)
        cp.start()
        cp.wait()

    return pl.pallas_call(
        body, name=name, in_specs=[any_spec], out_specs=any_spec,
        out_shape=jax.ShapeDtypeStruct((N_CHIPS, Hh, LANES), p.dtype),
        scratch_shapes=[pltpu.SemaphoreType.DMA, pltpu.SemaphoreType.DMA],
    )(p)


def _add_halves(p, got, c_arr, *, name):
    _, R, _ = p.shape
    Hh = R // 2
    tr = _pick(Hh, (1504, 752, 376, 94 * 8, 8))
    nb = Hh // tr

    def body(c_ref, p_ref, g_ref, o_ref):
        o_ref[...] = (p_ref[...] + g_ref[...]).astype(o_ref.dtype)

    return pl.pallas_call(
        body, name=name,
        grid_spec=pltpu.PrefetchScalarGridSpec(
            num_scalar_prefetch=1, grid=(N_CHIPS, nb),
            in_specs=[pl.BlockSpec((None, tr, LANES), lambda j, i, c_ref: (j, c_ref[0] * nb + i, 0)),
                      pl.BlockSpec((None, tr, LANES), lambda j, i, c_ref: (j, i, 0))],
            out_specs=pl.BlockSpec((None, tr, LANES), lambda j, i, c_ref: (j, i, 0))),
        out_shape=jax.ShapeDtypeStruct((N_CHIPS, Hh, LANES), WIRE_DTYPE),
        compiler_params=_cparams(("parallel", "parallel")),
    )(c_arr, p, got)


def _exchange_chips(qv, *, name):
    any_spec = pl.BlockSpec(memory_space=pl.ANY)

    def body(q_ref, out_ref, send_sems, recv_sems, local_sem):
        x, y, c = _place()
        me = 2 * x + y
        mine = pltpu.make_async_copy(q_ref.at[me], out_ref.at[me], local_sem)
        mine.start()
        chips = [(x ^ fx, y ^ fy) for fx, fy in _FLIPS]
        sends = []
        for j, (px, py) in enumerate(chips):
            cp = pltpu.make_async_remote_copy(
                src_ref=q_ref.at[2 * px + py], dst_ref=out_ref.at[me],
                send_sem=send_sems.at[j], recv_sem=recv_sems.at[j], device_id=(px, py, c), device_id_type=MESH)
            cp.start()
            sends.append(cp)
        for j, (px, py) in enumerate(chips):
            blk = out_ref.at[2 * px + py]
            pltpu.make_async_remote_copy(
                src_ref=blk, dst_ref=blk, send_sem=send_sems.at[j], recv_sem=recv_sems.at[j],
                device_id=(px, py, c), device_id_type=MESH).wait_recv()
        for cp in sends:
            cp.wait_send()
        mine.wait()

    return pl.pallas_call(
        body, name=name, in_specs=[any_spec], out_specs=any_spec,
        out_shape=jax.ShapeDtypeStruct(qv.shape, qv.dtype),
        scratch_shapes=[pltpu.SemaphoreType.DMA((3,)), pltpu.SemaphoreType.DMA((3,)), pltpu.SemaphoreType.DMA],
    )(qv)


def _sum_chips(r2, c_arr, *, name):
    _, Hh, _ = r2.shape
    tr = _pick(Hh, (1504, 752, 376, 8))
    nb = Hh // tr

    def body(c_ref, r_ref, o_ref):
        acc = r_ref[0].astype(F32)
        for j in range(1, N_CHIPS):
            acc = acc + r_ref[j].astype(F32)
        o_ref[...] = acc

    return pl.pallas_call(
        body, name=name,
        grid_spec=pltpu.PrefetchScalarGridSpec(
            num_scalar_prefetch=1, grid=(nb,),
            in_specs=[pl.BlockSpec((N_CHIPS, tr, LANES), lambda i, c_ref: (0, i, 0))],
            out_specs=pl.BlockSpec((tr, LANES), lambda i, c_ref: (c_ref[0] * nb + i, 0))),
        out_shape=jax.ShapeDtypeStruct((2 * Hh, LANES), F32),
        compiler_params=_cparams(("parallel",)),
    )(c_arr, r2)


def _join_halves(f, *, name):
    Hh = f.shape[0] // 2
    any_spec = pl.BlockSpec(memory_space=pl.ANY)

    def body(f_ref, out_ref, send_sem, recv_sem):
        del f_ref
        x, y, c = _place()
        my_rows = out_ref.at[pl.ds(pl.multiple_of(c * Hh, 8), Hh)]
        its_rows = out_ref.at[pl.ds(pl.multiple_of((1 - c) * Hh, 8), Hh)]
        cp = pltpu.make_async_remote_copy(
            src_ref=my_rows, dst_ref=my_rows, send_sem=send_sem, recv_sem=recv_sem,
            device_id=(x, y, 1 - c), device_id_type=MESH)
        cp.start()
        pltpu.make_async_remote_copy(
            src_ref=its_rows, dst_ref=its_rows, send_sem=send_sem, recv_sem=recv_sem,
            device_id=(x, y, 1 - c), device_id_type=MESH).wait_recv()
        cp.wait_send()

    return pl.pallas_call(
        body, name=name, in_specs=[any_spec], out_specs=any_spec,
        out_shape=jax.ShapeDtypeStruct(f.shape, f.dtype), input_output_aliases={0: 0},
        scratch_shapes=[pltpu.SemaphoreType.DMA, pltpu.SemaphoreType.DMA],
    )(f)


def _allreduce_small(buf, *, name):
    R = buf.shape[0]
    vm = pl.BlockSpec(memory_space=pltpu.VMEM)

    def body(b_ref, o_ref, slots, send_sems, recv_sems):
        x, y, c = _place()
        me = 4 * x + 2 * y + c
        slots[me] = b_ref[...]
        sends = []
        for kk in range(1, 8):
            fx, fy, fc = (kk >> 2) & 1, (kk >> 1) & 1, kk & 1
            cp = pltpu.make_async_remote_copy(
                src_ref=b_ref, dst_ref=slots.at[me], send_sem=send_sems.at[kk - 1], recv_sem=recv_sems.at[kk - 1],
                device_id=(x ^ fx, y ^ fy, c ^ fc), device_id_type=MESH)
            cp.start()
            sends.append(cp)
        for kk in range(1, 8):
            fx, fy, fc = (kk >> 2) & 1, (kk >> 1) & 1, kk & 1
            peer = 4 * (x ^ fx) + 2 * (y ^ fy) + (c ^ fc)
            pltpu.make_async_remote_copy(
                src_ref=b_ref, dst_ref=slots.at[peer], send_sem=send_sems.at[kk - 1], recv_sem=recv_sems.at[kk - 1],
                device_id=(x ^ fx, y ^ fy, c ^ fc), device_id_type=MESH).wait_recv()
        for cp in sends:
            cp.wait_send()
        acc = slots[0]
        for d in range(1, 8):
            acc = acc + slots[d]
        o_ref[...] = acc

    return pl.pallas_call(
        body, name=name, in_specs=[vm], out_specs=vm,
        out_shape=jax.ShapeDtypeStruct((R, LANES), F32),
        scratch_shapes=[pltpu.VMEM((8, R, LANES), F32), pltpu.SemaphoreType.DMA((7,)), pltpu.SemaphoreType.DMA((7,))],
    )(buf)


def _pad_rows(v, mult=8 * LANES):
    flat = v.reshape(-1)
    n = flat.shape[0]
    tot = -(-n // mult) * mult
    return jnp.pad(flat, (0, tot - n)).reshape(-1, LANES)


def _pack_small(parts):
    return jnp.concatenate([_pad_rows(p.astype(F32)) for p in parts], axis=0)


def _unpack_small(buf, shapes):
    out, r = [], 0
    for sh in shapes:
        n = math.prod(sh)
        rows = -(-n // (8 * LANES)) * 8
        out.append(buf[r:r + rows].reshape(-1)[:n].reshape(sh))
        r += rows
    return out


def _cols_to_shards(w):
    *lead, K, N = w.shape
    t = w.reshape(*lead, K, N_CHIPS, N // N_CHIPS)
    return jnp.moveaxis(t, -2, 0)


def _shards_to_cols(t):
    t = jnp.moveaxis(t, 0, -2)
    *lead, K, _, n = t.shape
    return t.reshape(*lead, K, N_CHIPS * n)


def _rows_to_shards(w):
    *lead, K, N = w.shape
    t = w.reshape(*lead, N_CHIPS, K // N_CHIPS, N)
    return jnp.moveaxis(t, -3, 0)


def _shards_to_rows(t):
    t = jnp.moveaxis(t, 0, -3)
    *lead, _, k, N = t.shape
    return t.reshape(*lead, N_CHIPS * k, N)


_BIG = (("m_w_in", True), ("m_w_out", False), ("a_w_in", True), ("a_w_out", False), ("f_w_up", True), ("f_w_down", False))


def kernel(x, m_w_in, m_gate_bias, m_head_norm, m_w_out, a_w_in, a_b_in, a_sinks, a_w_out, a_b_out, norm_mix_pre, norm_mix_post, norm_ffn_pre, norm_ffn_post, f_w_up, f_conv_w, f_conv_b, f_w_down, loss_target, m_m_w_in, m_m_gate_bias, m_m_head_norm, m_m_w_out, m_a_w_in, m_a_b_in, m_a_sinks, m_a_w_out, m_a_b_out, m_norm_mix_pre, m_norm_mix_post, m_norm_ffn_pre, m_norm_ffn_post, m_f_w_up, m_f_conv_w, m_f_conv_b, m_f_w_down, v_m_w_in, v_m_gate_bias, v_m_head_norm, v_m_w_out, v_a_w_in, v_a_b_in, v_a_sinks, v_a_w_out, v_a_b_out, v_norm_mix_pre, v_norm_mix_post, v_norm_ffn_pre, v_norm_ffn_post, v_f_w_up, v_f_conv_w, v_f_conv_b, v_f_w_down):
    params = dict(m_w_in=m_w_in, m_gate_bias=m_gate_bias, m_head_norm=m_head_norm, m_w_out=m_w_out, a_w_in=a_w_in,
                  a_b_in=a_b_in, a_sinks=a_sinks, a_w_out=a_w_out, a_b_out=a_b_out, norm_mix_pre=norm_mix_pre,
                  norm_mix_post=norm_mix_post, norm_ffn_pre=norm_ffn_pre, norm_ffn_post=norm_ffn_post, f_w_up=f_w_up,
                  f_conv_w=f_conv_w, f_conv_b=f_conv_b, f_w_down=f_w_down)
    mom1 = dict(m_w_in=m_m_w_in, m_gate_bias=m_m_gate_bias, m_head_norm=m_m_head_norm, m_w_out=m_m_w_out,
                a_w_in=m_a_w_in, a_b_in=m_a_b_in, a_sinks=m_a_sinks, a_w_out=m_a_w_out, a_b_out=m_a_b_out,
                norm_mix_pre=m_norm_mix_pre, norm_mix_post=m_norm_mix_post, norm_ffn_pre=m_norm_ffn_pre,
                norm_ffn_post=m_norm_ffn_post, f_w_up=m_f_w_up, f_conv_w=m_f_conv_w, f_conv_b=m_f_conv_b,
                f_w_down=m_f_w_down)
    mom2 = dict(m_w_in=v_m_w_in, m_gate_bias=v_m_gate_bias, m_head_norm=v_m_head_norm, m_w_out=v_m_w_out,
                a_w_in=v_a_w_in, a_b_in=v_a_b_in, a_sinks=v_a_sinks, a_w_out=v_a_w_out, a_b_out=v_a_b_out,
                norm_mix_pre=v_norm_mix_pre, norm_mix_post=v_norm_mix_post, norm_ffn_pre=v_norm_ffn_pre,
                norm_ffn_post=v_norm_ffn_post, f_w_up=v_f_w_up, f_conv_w=v_f_conv_w, f_conv_b=v_f_conv_b,
                f_w_down=v_f_w_down)
    order = list(params)

    mx, my, mc = _place()
    chip = 2 * mx + my
    h0 = x[0]
    target = loss_target[0]

    piece_rows = [math.prod(params[n].shape) // LANES for n, _ in _BIG]
    wp = jnp.concatenate([params[n].astype(MXU_DTYPE).reshape(-1, LANES) for n, _ in _BIG], axis=0)
    wg = lax.dynamic_update_slice(_gather_shards(wp, name="gather_weights"), wp[None], (chip, 0, 0))
    full, r = {}, 0
    for (n, by_cols), rows in zip(_BIG, piece_rows):
        t = wg[:, r:r + rows].reshape((N_CHIPS,) + params[n].shape)
        full[n] = _shards_to_cols(t) if by_cols else _shards_to_rows(t)
        r += rows

    def in_place(shard, axis):
        width = shard.shape[axis]
        z = jnp.zeros(shard.shape[:axis] + (N_CHIPS * width,) + shard.shape[axis + 1:], F32)
        contrib = jnp.where(mc == 0, shard, 0.0)
        return lax.dynamic_update_slice_in_dim(z, contrib, chip * width, axis)

    sm_in = [in_place(a_b_in, 1), in_place(a_b_out, 1), in_place(f_conv_w, 2)]
    sm_full = _unpack_small(_allreduce_small(_pack_small(sm_in), name="gather_small"), [t.shape for t in sm_in])
    b_in_full, b_out_full, conv_w_full = sm_full

    W_in = full["m_w_in"][0]
    nqk = M_HEADS * M_QK
    n_main = 2 * nqk + M_HEADS * M_V + D_MODEL

    def pad_heads(w):
        return jnp.pad(w.reshape(D_MODEL, M_HEADS, M_QK), ((0, 0), (0, 0), (0, LANES - M_QK))).reshape(D_MODEL, -1)

    def unpad_heads(w):
        return w.reshape(D_MODEL, M_HEADS, LANES)[:, :, :M_QK].reshape(D_MODEL, nqk)

    W_all = jnp.concatenate([pad_heads(W_in[:, :nqk]), pad_heads(W_in[:, nqk:2 * nqk]), W_in[:, 2 * nqk:n_main],
                             jnp.pad(W_in[:, n_main:], ((0, 0), (0, LANES - 2 * M_HEADS)))], axis=1)
    gbias = jnp.pad(m_gate_bias[0].reshape(1, 2 * M_HEADS), ((0, 0), (0, LANES - 2 * M_HEADS)))
    W_mout = full["m_w_out"][0]
    A_in, A_out = full["a_w_in"][0], full["a_w_out"][0]
    W_up, W_down = full["f_w_up"], full["f_w_down"]

    grads = {}

    def ffn_fwd(i, h1):
        z1 = _rms_fwd(h1, norm_ffn_pre[i:i + 1], out_dtype=MXU_DTYPE, name=f"ffn_pre_norm{i}")
        u = _mm(z1, W_up[i], name=f"ffn_up{i}")
        a = _convgate_fwd(u, conv_w_full[i], f_conv_b[i:i + 1], name=f"ffn_act{i}")
        zf = _mm(a, W_down[i], name=f"ffn_down{i}")
        h2 = _rms_fwd(zf, norm_ffn_post[i:i + 1], res=h1, out_dtype=F32, name=f"ffn_post_norm{i}")
        return h2, (z1, u, a, zf)

    z0 = _rms_fwd(h0, norm_mix_pre[0:1], out_dtype=MXU_DTYPE, name="mix_pre_norm0")
    proj = _mm(z0, W_all, name="mlstm_proj")
    gated, C_all, n_all, m_all = _mlstm_fwd(proj, gbias, m_head_norm, name="mlstm_fwd")
    zm0 = _mm(gated, W_mout, name="mlstm_out")
    h1 = _rms_fwd(zm0, norm_mix_post[0:1], res=h0, out_dtype=F32, name="mix_post_norm0")
    h2, ffn0 = ffn_fwd(0, h1)

    y0 = _rms_fwd(h2, norm_mix_pre[1:2], out_dtype=MXU_DTYPE, name="mix_pre_norm1")
    aproj = _mm(y0, A_in, bias=b_in_full, name="attn_proj")
    sinks = a_sinks.reshape(A_KVH, A_G)
    ao = _attn_fwd(aproj, sinks, name="attn_fwd")
    zm1 = _mm(ao, A_out, bias=b_out_full, name="attn_out")
    h3 = _rms_fwd(zm1, norm_mix_post[1:2], res=h2, out_dtype=F32, name="mix_post_norm1")
    h4, ffn1 = ffn_fwd(1, h3)

    loss_part, dh = _loss_head(h4, target, name="loss_head")
    loss = lax.psum(loss_part[0, 0], ("x", "y", "c"))

    g_post, g_fpre, g_mpost, g_mpre = [None, None], [None, None], [None, None], [None, None]
    dW_up, dW_down, dconv_w, dconv_b = [None, None], [None, None], [None, None], [None, None]

    def ffn_bwd(i, dh2, h1_, saved):
        z1, u, a, zf = saved
        dzf, g_post[i], _ = _rms_bwd(zf, norm_ffn_post[i:i + 1], dh2, out_dtype=MXU_DTYPE, name=f"ffn_post_norm_bwd{i}")
        da = _mm(dzf, W_down[i], tb=True, name=f"ffn_down_dx{i}")
        dW_down[i] = _mm(a, dzf, ta=True, name=f"ffn_down_dw{i}")
        du, dconv_w[i], dconv_b[i] = _convgate_bwd(u, da, conv_w_full[i], f_conv_b[i:i + 1], name=f"ffn_act_bwd{i}")
        dz1 = _mm(du, W_up[i], tb=True, name=f"ffn_up_dx{i}")
        dW_up[i] = _mm(z1, du, ta=True, name=f"ffn_up_dw{i}")
        dh1, g_fpre[i], _ = _rms_bwd(h1_, norm_ffn_pre[i:i + 1], dz1, res=dh2, out_dtype=F32, name=f"ffn_pre_norm_bwd{i}")
        return dh1

    dh3 = ffn_bwd(1, dh, h3, ffn1)
    dzm1, g_mpost[1], db_out = _rms_bwd(zm1, norm_mix_post[1:2], dh3, out_dtype=MXU_DTYPE, name="mix_post_norm_bwd1")
    dao = _mm(dzm1, A_out, tb=True, name="attn_out_dx")
    dA_out = _mm(ao, dzm1, ta=True, name="attn_out_dw")
    daq, dakv, dsinks, cs_q, cs_kv = _attn_bwd(aproj, sinks, dao, name="attn_bwd")
    db_in = jnp.concatenate([cs_q, cs_kv], axis=1)
    dy0 = _mm(daq, A_in[:, :A_NQ], tb=True, add=_mm(dakv, A_in[:, A_NQ:], tb=True, name="attn_proj_kv_dx"),
              name="attn_proj_q_dx")
    dA_in = jnp.concatenate([_mm(y0, daq, ta=True, name="attn_proj_q_dw"),
                             _mm(y0, dakv, ta=True, name="attn_proj_kv_dw")], axis=1)
    dh2, g_mpre[1], _ = _rms_bwd(h2, norm_mix_pre[1:2], dy0, res=dh3, out_dtype=F32, name="mix_pre_norm_bwd1")

    dh1 = ffn_bwd(0, dh2, h1, ffn0)
    dzm0, g_mpost[0], _ = _rms_bwd(zm0, norm_mix_post[0:1], dh1, out_dtype=MXU_DTYPE, name="mix_post_norm_bwd0")
    dgated = _mm(dzm0, W_mout, tb=True, name="mlstm_out_dx")
    dW_mout = _mm(gated, dzm0, ta=True, name="mlstm_out_dw")
    dproj, dgbias, dhn = _mlstm_bwd(proj, gbias, m_head_norm, C_all, n_all, m_all, dgated, name="mlstm_bwd")
    dz0 = _mm(dproj, W_all, tb=True, name="mlstm_proj_dx")
    dW_all = _mm(z0, dproj, ta=True, name="mlstm_proj_dw")
    dW_min = jnp.concatenate([unpad_heads(dW_all[:, M_OFF_Q:M_OFF_K]), unpad_heads(dW_all[:, M_OFF_K:M_OFF_V]),
                              dW_all[:, M_OFF_V:M_OFF_G], dW_all[:, M_OFF_G:M_OFF_G + 2 * M_HEADS]], axis=1)
    grad_x, g_mpre[0], _ = _rms_bwd(h0, norm_mix_pre[0:1], dz0, res=dh1, out_dtype=F32, name="mix_pre_norm_bwd0")

    big_g = dict(m_w_in=dW_min[None], m_w_out=dW_mout[None], a_w_in=dA_in[None], a_w_out=dA_out[None],
                 f_w_up=jnp.stack(dW_up), f_w_down=jnp.stack(dW_down))
    packed = jnp.concatenate(
        [(_cols_to_shards(big_g[n]) if by_cols else _rows_to_shards(big_g[n])).reshape(N_CHIPS, -1, LANES)
         for n, by_cols in _BIG], axis=1)
    got = _swap_halves(packed, name="grad_swap_halves")
    c_arr = jnp.reshape(mc, (1,)).astype(jnp.int32)
    chip_part = _add_halves(packed, got, c_arr, name="grad_add_halves")
    from_chips = _exchange_chips(chip_part, name="grad_exchange_chips")
    my_half = _sum_chips(from_chips, c_arr, name="grad_sum_chips")
    shard_g = _join_halves(my_half, name="grad_join_halves")
    r = 0
    for (n, _), rows in zip(_BIG, piece_rows):
        grads[n] = shard_g[r:r + rows].reshape(params[n].shape)
        r += rows

    small_g = [
        dgbias[:, :2 * M_HEADS].reshape(1, 2, M_HEADS),
        dhn,
        dsinks.reshape(1, A_QH),
        db_in, db_out,
        jnp.concatenate(g_mpre), jnp.concatenate(g_mpost), jnp.concatenate(g_fpre), jnp.concatenate(g_post),
        jnp.stack(dconv_w), jnp.concatenate(dconv_b),
    ]
    small_names = ["m_gate_bias", "m_head_norm", "a_sinks", "a_b_in", "a_b_out", "norm_mix_pre", "norm_mix_post",
                   "norm_ffn_pre", "norm_ffn_post", "f_conv_w", "f_conv_b"]
    red = _unpack_small(_allreduce_small(_pack_small(small_g), name="reduce_small"), [t.shape for t in small_g])
    for n, t in zip(small_names, red):
        if n in ("a_b_in", "a_b_out", "f_conv_w"):
            axis = t.ndim - 1
            width = params[n].shape[axis]
            t = lax.dynamic_slice_in_dim(t, chip * width, width, axis)
        grads[n] = t

    deltas, new_m, new_v = {}, {}, {}
    for n, _ in _BIG:
        sh = params[n].shape
        two = lambda t: t.reshape(-1, sh[-1])
        d, nm, nv = _adamw_big(two(params[n]), two(grads[n]), two(mom1[n]), two(mom2[n]), name=f"adamw_{n}")
        deltas[n], new_m[n], new_v[n] = d.reshape(sh), nm.reshape(sh), nv.reshape(sh)
    two = lambda t: t.reshape(-1, t.shape[-1])
    res = _adamw_small([(two(params[n]), two(grads[n]), two(mom1[n]), two(mom2[n])) for n in small_names],
                       name="adamw_small")
    for n, (d, nm, nv) in zip(small_names, res):
        sh = params[n].shape
        deltas[n], new_m[n], new_v[n] = d.reshape(sh), nm.reshape(sh), nv.reshape(sh)

    return (loss, grad_x[None], *[grads[n] for n in order], *[deltas[n] for n in order],
            *[new_m[n] for n in order], *[new_v[n] for n in order])
```

```python
import functools
import math

import jax
import jax.numpy as jnp
from jax import lax
from jax.experimental import pallas as pl
from jax.experimental.pallas import tpu as pltpu

F32 = jnp.float32
MXU_DTYPE = jnp.bfloat16
WIRE_DTYPE = jnp.bfloat16
MESH = pl.DeviceIdType.MESH

D_MODEL = 1024
EPS = 1e-6
M_HEADS, M_QK, M_V, M_CHUNK = 8, 64, 128, 128
GATE_CAP = 15.0
A_DH, A_QH, A_KVH, A_G, A_BLK = 64, 16, 2, 8, 128
D_FF = 2816
N_CHIPS = 4
LANES = 128
VMEM_LIMIT = 48 * 1024 * 1024

ADAM_LR, ADAM_B1, ADAM_B2, ADAM_EPS, ADAM_WD, ADAM_STEP = 0.001, 0.9, 0.999, 1e-08, 0.01, 10


def _cparams(sem):
    return pltpu.CompilerParams(dimension_semantics=sem, vmem_limit_bytes=VMEM_LIMIT)


def _pick(n, cands):
    for c in cands:
        if n % c == 0:
            return c
    return n


def _mm(a, b, *, ta=False, tb=False, out_dtype=F32, bias=None, add=None, name):
    if ta:
        K, M = a.shape
    else:
        M, K = a.shape
    if tb:
        N, Kb = b.shape
    else:
        Kb, N = b.shape
    assert K == Kb, (a.shape, b.shape)
    tm = _pick(M, (1024, 1408, 512, 256, 128))
    tn = _pick(N, (1024, 1408, 1280, 512, 256, 128))
    tk = K if K <= 2816 else _pick(K, (2816, 2048, 1408, 1024, 512, 256, 128))
    nk = K // tk
    dn = (((0 if ta else 1,), (1 if tb else 0,)), ((), ()))
    has_bias, has_add = bias is not None, add is not None

    def body(*refs):
        a_ref, b_ref = refs[0], refs[1]
        pos = 2
        bias_ref = add_ref = None
        if has_bias:
            bias_ref = refs[pos]
            pos += 1
        if has_add:
            add_ref = refs[pos]
            pos += 1
        o_ref = refs[pos]
        acc_ref = refs[pos + 1] if nk > 1 else None

        def finish(r):
            if has_bias:
                r = r + bias_ref[...]
            if has_add:
                r = r + add_ref[...]
            o_ref[...] = r.astype(out_dtype)

        part = lax.dot_general(a_ref[...], b_ref[...], dn, preferred_element_type=F32)
        if nk == 1:
            finish(part)
        else:
            k = pl.program_id(2)

            @pl.when(k == 0)
            def _():
                acc_ref[...] = part

            @pl.when(k > 0)
            def _():
                acc_ref[...] += part

            @pl.when(k == nk - 1)
            def _():
                finish(acc_ref[...])

    a_spec = pl.BlockSpec((tk, tm), lambda i, j, k: (k, i)) if ta else pl.BlockSpec((tm, tk), lambda i, j, k: (i, k))
    b_spec = pl.BlockSpec((tn, tk), lambda i, j, k: (j, k)) if tb else pl.BlockSpec((tk, tn), lambda i, j, k: (k, j))
    in_specs, args = [a_spec, b_spec], [a, b]
    if has_bias:
        in_specs.append(pl.BlockSpec((1, tn), lambda i, j, k: (0, j)))
        args.append(bias)
    if has_add:
        in_specs.append(pl.BlockSpec((tm, tn), lambda i, j, k: (i, j)))
        args.append(add)
    return pl.pallas_call(
        body, name=name,
        grid=(M // tm, N // tn, nk),
        in_specs=in_specs,
        out_specs=pl.BlockSpec((tm, tn), lambda i, j, k: (i, j)),
        out_shape=jax.ShapeDtypeStruct((M, N), out_dtype),
        scratch_shapes=[pltpu.VMEM((tm, tn), F32)] if nk > 1 else [],
        compiler_params=_cparams(("parallel", "parallel", "arbitrary")),
    )(*args)


ROW_TILE = 512


def _rms_fwd(x, g, *, res=None, out_dtype, name):
    S, D = x.shape
    tm = _pick(S, (ROW_TILE, 256, 128))
    has_res = res is not None

    def body(*refs):
        x_ref, g_ref = refs[0], refs[1]
        res_ref = refs[2] if has_res else None
        o_ref = refs[-1]
        xv = x_ref[...]
        y = xv * lax.rsqrt(jnp.mean(xv * xv, axis=-1, keepdims=True) + EPS) * g_ref[...]
        if has_res:
            y = res_ref[...] + y
        o_ref[...] = y.astype(out_dtype)

    row = pl.BlockSpec((tm, D), lambda i: (i, 0))
    in_specs, args = [row, pl.BlockSpec((1, D), lambda i: (0, 0))], [x, g]
    if has_res:
        in_specs.append(row)
        args.append(res)
    return pl.pallas_call(
        body, name=name, grid=(S // tm,), in_specs=in_specs, out_specs=row,
        out_shape=jax.ShapeDtypeStruct((S, D), out_dtype),
        compiler_params=_cparams(("parallel",)),
    )(*args)


def _rms_bwd(x, g, dy, *, res=None, out_dtype, name):
    S, D = x.shape
    tm = _pick(S, (ROW_TILE, 256, 128))
    has_res = res is not None

    def body(*refs):
        x_ref, g_ref, dy_ref = refs[0], refs[1], refs[2]
        res_ref = refs[3] if has_res else None
        dx_ref, dg_ref, cs_ref = refs[-3], refs[-2], refs[-1]
        i = pl.program_id(0)
        xv = x_ref[...]
        dyv = dy_ref[...].astype(F32)
        r = lax.rsqrt(jnp.mean(xv * xv, axis=-1, keepdims=True) + EPS)
        xh = xv * r
        gd = dyv * g_ref[...]
        dx = r * (gd - xh * jnp.mean(gd * xh, axis=-1, keepdims=True))
        cs = jnp.sum(dx, axis=0, keepdims=True)
        if has_res:
            dx = res_ref[...] + dx
        dx_ref[...] = dx.astype(out_dtype)
        dg = jnp.sum(dyv * xh, axis=0, keepdims=True)

        @pl.when(i == 0)
        def _():
            dg_ref[...] = dg
            cs_ref[...] = cs

        @pl.when(i > 0)
        def _():
            dg_ref[...] += dg
            cs_ref[...] += cs

    row = pl.BlockSpec((tm, D), lambda i: (i, 0))
    vec = pl.BlockSpec((1, D), lambda i: (0, 0))
    in_specs, args = [row, vec, row], [x, g, dy]
    if has_res:
        in_specs.append(row)
        args.append(res)
    return pl.pallas_call(
        body, name=name, grid=(S // tm,), in_specs=in_specs, out_specs=[row, vec, vec],
        out_shape=[jax.ShapeDtypeStruct((S, D), out_dtype), jax.ShapeDtypeStruct((1, D), F32),
                   jax.ShapeDtypeStruct((1, D), F32)],
        compiler_params=_cparams(("arbitrary",)),
    )(*args)


def _loss_head(y, target, *, name):
    S, D = y.shape
    tm = _pick(S, (ROW_TILE, 256, 128))

    def body(y_ref, t_ref, loss_ref, dy_ref):
        i = pl.program_id(0)
        err = y_ref[...] - t_ref[...]
        dy_ref[...] = err * (1.0 / D)
        part = 0.5 * jnp.sum(jnp.mean(err * err, axis=-1, keepdims=True), axis=0, keepdims=True)

        @pl.when(i == 0)
        def _():
            loss_ref[...] = part

        @pl.when(i > 0)
        def _():
            loss_ref[...] += part

    row = pl.BlockSpec((tm, D), lambda i: (i, 0))
    return pl.pallas_call(
        body, name=name, grid=(S // tm,), in_specs=[row, row],
        out_specs=[pl.BlockSpec((1, 1), lambda i: (0, 0)), row],
        out_shape=[jax.ShapeDtypeStruct((1, 1), F32), jax.ShapeDtypeStruct((S, D), F32)],
        compiler_params=_cparams(("arbitrary",)),
    )(y, target)


def _mx(t):
    return t.astype(MXU_DTYPE)


def _mxf(t):
    return t.astype(MXU_DTYPE).astype(F32)


def _rawdot(a, b, ca, cb):
    return lax.dot_general(_mx(a), _mx(b), (((ca,), (cb,)), ((), ())), preferred_element_type=F32)


@functools.partial(jax.custom_vjp, nondiff_argnums=(2, 3))
def _dot(a, b, ca, cb):
    return _rawdot(a, b, ca, cb)


def _dot_fwd(a, b, ca, cb):
    return _rawdot(a, b, ca, cb), (a, b)


def _dot_bwd(ca, cb, res, g):
    a, b = res
    bj = 1 if cb == 0 else 0
    ai = 0 if ca == 1 else 1
    da = _rawdot(g, b, 1, bj) if ca == 1 else _rawdot(b, g, bj, 1)
    db = _rawdot(a, g, ai, 0) if cb == 0 else _rawdot(g, a, 0, ai)
    return da.astype(a.dtype), db.astype(b.dtype)


_dot.defvjp(_dot_fwd, _dot_bwd)


def _softcap(z):
    return GATE_CAP * jnp.tanh(z / GATE_CAP)


def _log_sigmoid(z):
    return jnp.minimum(z, 0.0) - jnp.log(1.0 + jnp.exp(-jnp.abs(z)))


def _sigmoid(z):
    return 0.5 * jnp.tanh(0.5 * z) + 0.5


def _lane_col(t, lane_index):
    lane = lax.broadcasted_iota(jnp.int32, t.shape, 1)
    return jnp.sum(jnp.where(lane == lane_index, t, 0.0), axis=1, keepdims=True)


def _mlstm_gates(G, bias):
    L = G.shape[0]
    z = _softcap(G + bias)
    ig = z
    lf = _log_sigmoid(z)
    ti = lax.broadcasted_iota(jnp.int32, (L, L), 0)
    si = lax.broadcasted_iota(jnp.int32, (L, L), 1)
    tril = (si <= ti).astype(F32)
    b = lax.dot_general(tril, lf, (((1,), (0,)), ((), ())), precision=lax.Precision.HIGHEST, preferred_element_type=F32)
    bL = jnp.sum(lf, axis=0, keepdims=True)
    return ig, b, ig.T, b.T, bL


def _mlstm_head(h, q, k, v, o, ig_all, b_all, igT, bT, bL_all, hn, C, n, m):
    L = q.shape[0]
    ti = lax.broadcasted_iota(jnp.int32, (L, L), 0)
    si = lax.broadcasted_iota(jnp.int32, (L, L), 1)
    lower = (si <= ti)
    ig = _lane_col(ig_all, h)
    b = _lane_col(b_all, M_HEADS + h)
    ig_row = igT[h:h + 1, :]
    b_row = bT[M_HEADS + h:M_HEADS + h + 1, :]
    bL = _lane_col(bL_all, M_HEADS + h)
    inter = b + m
    dlog = jnp.where(lower, b - b_row + ig_row, -jnp.inf)
    m_t = lax.stop_gradient(jnp.maximum(inter, jnp.max(dlog, axis=-1, keepdims=True)))
    qs = q * (M_QK ** -0.5)
    w = _dot(qs, k, 1, 1) * jnp.exp(dlog - m_t)
    s_inter = jnp.exp(inter - m_t)
    num = _dot(w, v, 1, 0) + s_inter * _dot(qs, C, 1, 0)
    den = jnp.sum(w, axis=-1, keepdims=True) + s_inter * jnp.sum(_mxf(qs) * _mxf(n), axis=-1, keepdims=True)
    hout = num * (1.0 / jnp.maximum(jnp.abs(den), jnp.exp(-m_t)))
    tail = bL - b + ig
    m_new = lax.stop_gradient(jnp.maximum(bL + m, jnp.max(tail, axis=0, keepdims=True)))
    ws = jnp.exp(tail - m_new)
    decay = jnp.exp(bL + m - m_new)
    wk = ws * k
    C_new = decay * C + _dot(wk, v, 0, 0)
    n_new = decay * n + jnp.sum(_mxf(ws) * _mxf(k), axis=0, keepdims=True)
    hs = hout * lax.rsqrt(jnp.mean(hout * hout, axis=-1, keepdims=True) + EPS) * hn
    gated = _sigmoid(o) * hs
    return (gated, C_new, n_new), m_new


M_OFF_Q, M_OFF_K, M_OFF_V, M_OFF_O, M_OFF_G = 0, M_HEADS * LANES, 2 * M_HEADS * LANES, 3 * M_HEADS * LANES, 4 * M_HEADS * LANES
M_PROJ = M_OFF_G + LANES


def _head_cols(off, h):
    return slice(off + h * LANES, off + (h + 1) * LANES)


def _mlstm_specs(NC, rev):
    H, L = M_HEADS, M_CHUNK
    cc = (lambda c: NC - 1 - c) if rev else (lambda c: c)
    proj = pl.BlockSpec((L, M_PROJ), lambda c: (cc(c), 0))
    vec = pl.BlockSpec((1, LANES), lambda c: (0, 0))
    hn = pl.BlockSpec((1, H * M_V), lambda c: (0, 0))
    hv = pl.BlockSpec((L, H * M_V), lambda c: (cc(c), 0))
    Cs = pl.BlockSpec((None, H, LANES, M_V), lambda c: (cc(c), 0, 0, 0))
    ns = pl.BlockSpec((None, H, 1, LANES), lambda c: (cc(c), 0, 0, 0))
    ms = pl.BlockSpec((None, H, 1, 1), lambda c: (cc(c), 0, 0, 0))
    return proj, vec, hn, hv, Cs, ns, ms


_MLSTM_STATE = [pltpu.VMEM((M_HEADS, LANES, M_V), F32), pltpu.VMEM((M_HEADS, 1, LANES), F32),
                pltpu.VMEM((M_HEADS, 1, 1), F32)]


def _mlstm_fwd(proj, bias, hn, *, name):
    S = proj.shape[0]
    H, NC = M_HEADS, S // M_CHUNK
    ps, vec, hns, hv, Cs, ns, ms = _mlstm_specs(NC, False)

    def body(p_ref, b_ref, hn_ref, gated_ref, C_all, n_all, m_all, C_s, n_s, m_s):
        @pl.when(pl.program_id(0) == 0)
        def _():
            C_s[...] = jnp.zeros_like(C_s)
            n_s[...] = jnp.zeros_like(n_s)
            m_s[...] = jnp.zeros_like(m_s)

        gate_terms = _mlstm_gates(p_ref[:, M_OFF_G:M_OFF_G + LANES], b_ref[...])
        for h in range(H):
            C, n, m = C_s[h], n_s[h], m_s[h]
            C_all[h] = C
            n_all[h] = n
            m_all[h] = m
            (gated, Cn, nn), mn = _mlstm_head(
                h, p_ref[:, _head_cols(M_OFF_Q, h)], p_ref[:, _head_cols(M_OFF_K, h)], p_ref[:, _head_cols(M_OFF_V, h)],
                p_ref[:, _head_cols(M_OFF_O, h)], *gate_terms, hn_ref[:, _head_cols(0, h)], C, n, m)
            gated_ref[:, _head_cols(0, h)] = gated.astype(gated_ref.dtype)
            C_s[h] = Cn
            n_s[h] = nn
            m_s[h] = mn

    return pl.pallas_call(
        body, name=name, grid=(NC,),
        in_specs=[ps, vec, hns],
        out_specs=[hv, Cs, ns, ms],
        out_shape=[jax.ShapeDtypeStruct((S, H * M_V), MXU_DTYPE),
                   jax.ShapeDtypeStruct((NC, H, LANES, M_V), F32),
                   jax.ShapeDtypeStruct((NC, H, 1, LANES), F32),
                   jax.ShapeDtypeStruct((NC, H, 1, 1), F32)],
        scratch_shapes=list(_MLSTM_STATE),
        compiler_params=_cparams(("arbitrary",)),
    )(proj, bias, hn)


def _mlstm_bwd(proj, bias, hn, C_all, n_all, m_all, dgated, *, name):
    S = proj.shape[0]
    H, NC = M_HEADS, S // M_CHUNK
    ps, vec, hns, hv, Cs, ns, ms = _mlstm_specs(NC, True)

    def body(p_ref, b_ref, hn_ref, C_ref, n_ref, m_ref, dg_ref, dp_ref, db_ref, dhn_ref, dC_s, dn_s):
        @pl.when(pl.program_id(0) == 0)
        def _():
            dC_s[...] = jnp.zeros_like(dC_s)
            dn_s[...] = jnp.zeros_like(dn_s)
            db_ref[...] = jnp.zeros_like(db_ref)
            dhn_ref[...] = jnp.zeros_like(dhn_ref)

        gate_terms, gates_vjp = jax.vjp(_mlstm_gates, p_ref[:, M_OFF_G:M_OFF_G + LANES], b_ref[...])
        d_terms = [jnp.zeros_like(t) for t in gate_terms]
        for h in range(H):
            def head(q, k, v, o, *rest, h=h):
                return _mlstm_head(h, q, k, v, o, *rest, m_ref[h])

            prim = (p_ref[:, _head_cols(M_OFF_Q, h)], p_ref[:, _head_cols(M_OFF_K, h)], p_ref[:, _head_cols(M_OFF_V, h)],
                    p_ref[:, _head_cols(M_OFF_O, h)], *gate_terms, hn_ref[:, _head_cols(0, h)], C_ref[h], n_ref[h])
            _, vjp, _ = jax.vjp(head, *prim, has_aux=True)
            dq, dk, dv, do, *d_gate, dhnh, dC, dn = vjp((dg_ref[:, _head_cols(0, h)].astype(F32), dC_s[h], dn_s[h]))
            dp_ref[:, _head_cols(M_OFF_Q, h)] = dq.astype(dp_ref.dtype)
            dp_ref[:, _head_cols(M_OFF_K, h)] = dk.astype(dp_ref.dtype)
            dp_ref[:, _head_cols(M_OFF_V, h)] = dv.astype(dp_ref.dtype)
            dp_ref[:, _head_cols(M_OFF_O, h)] = do.astype(dp_ref.dtype)
            d_terms = [a + g for a, g in zip(d_terms, d_gate)]
            dhn_ref[:, _head_cols(0, h)] += dhnh
            dC_s[h] = dC
            dn_s[h] = dn
        dG, dbias = gates_vjp(tuple(d_terms))
        dp_ref[:, M_OFF_G:M_OFF_G + LANES] = dG.astype(dp_ref.dtype)
        db_ref[...] += dbias

    return pl.pallas_call(
        body, name=name, grid=(NC,),
        in_specs=[ps, vec, hns, Cs, ns, ms, hv],
        out_specs=[ps, vec, hns],
        out_shape=[jax.ShapeDtypeStruct((S, M_PROJ), MXU_DTYPE), jax.ShapeDtypeStruct((1, LANES), F32),
                   jax.ShapeDtypeStruct((1, H * M_V), F32)],
        scratch_shapes=list(_MLSTM_STATE[:2]),
        compiler_params=_cparams(("arbitrary",)),
    )(proj, bias, hn, C_all, n_all, m_all, dgated)


A_NQ = A_QH * A_DH
A_NKV = 2 * A_KVH * A_DH
A_PAIRS = A_G // 2


def _attn_group(Ps, KLO, KHI, VLO, VHI, sinks, first):
    B = Ps[0].shape[0]
    R = len(Ps) * B
    q2 = jnp.concatenate(Ps, axis=0) * (A_DH ** -0.5)
    s = jnp.concatenate([_dot(q2, KLO, 1, 1), _dot(q2, KHI, 1, 1)], axis=0)
    qi = lax.broadcasted_iota(jnp.int32, (2 * R, 2 * B), 0) & (B - 1)
    ku = lax.broadcasted_iota(jnp.int32, (2 * R, 2 * B), 1)
    diff = qi - (ku - B)
    mask = (diff >= 0) & (diff < B) & ((ku >= B) | jnp.logical_not(first))
    s = jnp.where(mask, s, -jnp.inf)
    ri = lax.broadcasted_iota(jnp.int32, (2 * R, A_G), 0)
    head = 2 * lax.shift_right_logical(ri & (R - 1), B.bit_length() - 1) + lax.shift_right_logical(ri, R.bit_length() - 1)
    onehot = head == lax.broadcasted_iota(jnp.int32, (2 * R, A_G), 1)
    sink = jnp.sum(jnp.where(onehot, sinks, 0.0), axis=1, keepdims=True)
    mx = lax.stop_gradient(jnp.maximum(jnp.max(s, axis=-1, keepdims=True), sink))
    p = jnp.exp(s - mx)
    p = p * (1.0 / (jnp.sum(p, axis=-1, keepdims=True) + jnp.exp(sink - mx)))
    o = _dot(p[:R], VLO, 1, 0) + _dot(p[R:], VHI, 1, 0)
    return tuple(o[j * B:(j + 1) * B] for j in range(len(Ps)))


def _swap_halves_of_lanes(t):
    return pltpu.roll(t, LANES // 2, 1)


def _kv_operands(kvp_ref, kvc_ref, h):
    kk = jnp.concatenate([kvp_ref[:, :LANES], kvc_ref[:, :LANES]], axis=0)
    vv = jnp.concatenate([kvp_ref[:, LANES:], kvc_ref[:, LANES:]], axis=0)
    low = lax.broadcasted_iota(jnp.int32, kk.shape, 1) < A_DH
    own = low if h == 0 else jnp.logical_not(low)
    k_own = jnp.where(own, kk, 0.0)
    v_own = jnp.where(own, vv, 0.0)
    k_oth, v_oth = _swap_halves_of_lanes(k_own), _swap_halves_of_lanes(v_own)
    if h == 0:
        return own, k_own, k_oth, v_own, v_oth
    return own, k_oth, k_own, v_oth, v_own


def _pair_cols(h, j):
    c = (h * A_PAIRS + j) * LANES
    return slice(c, c + LANES)


def _attn_fwd(proj, sinks, *, name):
    S = proj.shape[0]
    NB = S // A_BLK
    kv_blk = A_NQ // A_NKV
    qs = pl.BlockSpec((A_BLK, A_NQ), lambda n: (n, 0))
    cur = pl.BlockSpec((A_BLK, A_NKV), lambda n: (n, kv_blk))
    prev = pl.BlockSpec((A_BLK, A_NKV), lambda n: (jnp.maximum(n - 1, 0), kv_blk))
    sk = pl.BlockSpec((A_KVH, A_G), lambda n: (0, 0))

    def body(q_ref, kvp_ref, kvc_ref, s_ref, o_ref):
        first = pl.program_id(0) == 0
        for h in range(A_KVH):
            _, KLO, KHI, VLO, VHI = _kv_operands(kvp_ref, kvc_ref, h)
            Ps = tuple(q_ref[:, _pair_cols(h, j)] for j in range(A_PAIRS))
            outs = _attn_group(Ps, KLO, KHI, VLO, VHI, s_ref[h:h + 1, :], first)
            for j in range(A_PAIRS):
                o_ref[:, _pair_cols(h, j)] = outs[j].astype(o_ref.dtype)

    return pl.pallas_call(
        body, name=name, grid=(NB,),
        in_specs=[qs, prev, cur, sk], out_specs=qs,
        out_shape=jax.ShapeDtypeStruct((S, A_NQ), MXU_DTYPE),
        compiler_params=_cparams(("parallel",)),
    )(proj, proj, proj, sinks)


def _attn_bwd(proj, sinks, do, *, name):
    S = proj.shape[0]
    NB = S // A_BLK
    last = NB - 1
    kv_blk = A_NQ // A_NKV
    qs = pl.BlockSpec((A_BLK, A_NQ), lambda n: (jnp.minimum(n, last), 0))
    cur = pl.BlockSpec((A_BLK, A_NKV), lambda n: (jnp.minimum(n, last), kv_blk))
    prev = pl.BlockSpec((A_BLK, A_NKV), lambda n: (jnp.clip(n - 1, 0, last), kv_blk))
    sk = pl.BlockSpec((A_KVH, A_G), lambda n: (0, 0))
    lag = pl.BlockSpec((A_BLK, A_NKV), lambda n: (jnp.maximum(n - 1, 0), 0))
    cq_spec = pl.BlockSpec((1, A_NQ), lambda n: (0, 0))
    ckv_spec = pl.BlockSpec((1, A_NKV), lambda n: (0, 0))

    def body(q_ref, kvp_ref, kvc_ref, s_ref, do_ref, dq_ref, dkv_ref, ds_ref, cq_ref, ckv_ref, keep):
        n = pl.program_id(0)

        @pl.when(n == 0)
        def _():
            keep[...] = jnp.zeros_like(keep)
            ds_ref[...] = jnp.zeros_like(ds_ref)
            cq_ref[...] = jnp.zeros_like(cq_ref)
            ckv_ref[...] = jnp.zeros_like(ckv_ref)

        @pl.when(n < NB)
        def _():
            f = functools.partial(_attn_group, first=n == 0)
            dk = jnp.zeros((2 * A_BLK, LANES), F32)
            dv = jnp.zeros((2 * A_BLK, LANES), F32)
            ds_rows = []
            for h in range(A_KVH):
                own, KLO, KHI, VLO, VHI = _kv_operands(kvp_ref, kvc_ref, h)
                Ps = tuple(q_ref[:, _pair_cols(h, j)] for j in range(A_PAIRS))
                _, vjp = jax.vjp(f, Ps, KLO, KHI, VLO, VHI, s_ref[h:h + 1, :])
                dPs, dKLO, dKHI, dVLO, dVHI, dsk = vjp(
                    tuple(do_ref[:, _pair_cols(h, j)].astype(F32) for j in range(A_PAIRS)))
                for j in range(A_PAIRS):
                    dq_ref[:, _pair_cols(h, j)] = dPs[j].astype(dq_ref.dtype)
                    cq_ref[:, _pair_cols(h, j)] += jnp.sum(dPs[j], axis=0, keepdims=True)
                dk_own, dk_oth = (dKLO, dKHI) if h == 0 else (dKHI, dKLO)
                dv_own, dv_oth = (dVLO, dVHI) if h == 0 else (dVHI, dVLO)
                dk = dk + jnp.where(own, dk_own, 0.0) + _swap_halves_of_lanes(jnp.where(own, 0.0, dk_oth))
                dv = dv + jnp.where(own, dv_own, 0.0) + _swap_halves_of_lanes(jnp.where(own, 0.0, dv_oth))
                ds_rows.append(dsk)
            ds_ref[...] += jnp.concatenate(ds_rows, axis=0)
            dkv = jnp.concatenate([dk, dv], axis=1)
            done = keep[...] + dkv[:A_BLK]
            dkv_ref[...] = done.astype(dkv_ref.dtype)
            ckv_ref[...] += jnp.sum(done, axis=0, keepdims=True)
            keep[...] = dkv[A_BLK:]

        @pl.when(n == NB)
        def _():
            done = keep[...]
            dkv_ref[...] = done.astype(dkv_ref.dtype)
            ckv_ref[...] += jnp.sum(done, axis=0, keepdims=True)

    return pl.pallas_call(
        body, name=name, grid=(NB + 1,),
        in_specs=[qs, prev, cur, sk, qs],
        out_specs=[qs, lag, sk, cq_spec, ckv_spec],
        out_shape=[jax.ShapeDtypeStruct((S, A_NQ), MXU_DTYPE), jax.ShapeDtypeStruct((S, A_NKV), MXU_DTYPE),
                   jax.ShapeDtypeStruct((A_KVH, A_G), F32), jax.ShapeDtypeStruct((1, A_NQ), F32),
                   jax.ShapeDtypeStruct((1, A_NKV), F32)],
        scratch_shapes=[pltpu.VMEM((A_BLK, A_NKV), F32)],
        compiler_params=_cparams(("arbitrary",)),
    )(proj, proj, proj, sinks, do)


HALO = 8


def _shift_rows(t, j):
    if j == 0:
        return t
    return pltpu.roll(t, j % t.shape[0], 0)


def _conv_gate(gate_ext, cw, cb):
    return cb + cw[0:1, :] * _shift_rows(gate_ext, 2) + cw[1:2, :] * _shift_rows(gate_ext, 1) + cw[2:3, :] * gate_ext


def _convgate_fwd(u, cw, cb, *, name):
    S, F2 = u.shape
    F = F2 // 2
    tm = _pick(S, (256, 128))
    hb = tm // HALO
    urow = pl.BlockSpec((tm, F2), lambda i: (i, 0))
    uprev = pl.BlockSpec((HALO, F), lambda i: (jnp.maximum(i * hb - 1, 0), 0))

    def body(u_ref, up_ref, cw_ref, cb_ref, a_ref):
        i = pl.program_id(0)
        gate = u_ref[:, :F]
        val = u_ref[:, F:]
        prev = jnp.where(i > 0, up_ref[...], 0.0)
        gc = _conv_gate(jnp.concatenate([prev, gate], axis=0), cw_ref[...], cb_ref[...])[HALO:]
        a_ref[...] = (gc * _sigmoid(gc) * val).astype(a_ref.dtype)

    return pl.pallas_call(
        body, name=name, grid=(S // tm,),
        in_specs=[urow, uprev, pl.BlockSpec((3, F), lambda i: (0, 0)), pl.BlockSpec((1, F), lambda i: (0, 0))],
        out_specs=pl.BlockSpec((tm, F), lambda i: (i, 0)),
        out_shape=jax.ShapeDtypeStruct((S, F), MXU_DTYPE),
        compiler_params=_cparams(("parallel",)),
    )(u, u, cw, cb)


def _convgate_bwd(u, da, cw, cb, *, name):
    S, F2 = u.shape
    F = F2 // 2
    tm = _pick(S, (128,))
    hb = tm // HALO
    nt = S // tm
    nh = S // HALO
    urow = pl.BlockSpec((tm, F2), lambda i: (i, 0))
    uprev = pl.BlockSpec((HALO, F), lambda i: (jnp.maximum(i * hb - 1, 0), 0))
    unext = pl.BlockSpec((HALO, F2), lambda i: (jnp.minimum((i + 1) * hb, nh - 1), 0))
    darow = pl.BlockSpec((tm, F), lambda i: (i, 0))
    danext = pl.BlockSpec((HALO, F), lambda i: (jnp.minimum((i + 1) * hb, nh - 1), 0))

    def body(u_ref, up_ref, un_ref, da_ref, dan_ref, cw_ref, cb_ref, du_ref, dcw_ref, dcb_ref):
        i = pl.program_id(0)
        cwv = cw_ref[...]
        prev = jnp.where(i > 0, up_ref[...], 0.0)
        gate_ext = jnp.concatenate([prev, u_ref[:, :F], un_ref[:, :F]], axis=0)
        val_ext = jnp.concatenate([u_ref[:, F:], un_ref[:, F:]], axis=0)
        da_next = jnp.where(i < nt - 1, dan_ref[...].astype(F32), 0.0)
        da_ext = jnp.concatenate([da_ref[...].astype(F32), da_next], axis=0)
        gc = _conv_gate(gate_ext, cwv, cb_ref[...])[HALO:]
        sg = _sigmoid(gc)
        silu = gc * sg
        dval = da_ext * silu
        dgc = da_ext * val_ext * (sg * (1.0 + gc * (1.0 - sg)))
        dgate = cwv[2:3, :] * dgc + cwv[1:2, :] * _shift_rows(dgc, -1) + cwv[0:1, :] * _shift_rows(dgc, -2)
        du_ref[:, :F] = dgate[:tm].astype(du_ref.dtype)
        du_ref[:, F:] = dval[:tm].astype(du_ref.dtype)
        dgc_c = dgc[:tm]
        g0 = gate_ext[HALO:HALO + tm]
        g1 = _shift_rows(gate_ext, 1)[HALO:HALO + tm]
        g2 = _shift_rows(gate_ext, 2)[HALO:HALO + tm]
        dcw = jnp.concatenate([jnp.sum(dgc_c * g2, axis=0, keepdims=True),
                               jnp.sum(dgc_c * g1, axis=0, keepdims=True),
                               jnp.sum(dgc_c * g0, axis=0, keepdims=True)], axis=0)
        dcb = jnp.sum(dgc_c, axis=0, keepdims=True)

        @pl.when(i == 0)
        def _():
            dcw_ref[...] = dcw
            dcb_ref[...] = dcb

        @pl.when(i > 0)
        def _():
            dcw_ref[...] += dcw
            dcb_ref[...] += dcb

    return pl.pallas_call(
        body, name=name, grid=(nt,),
        in_specs=[urow, uprev, unext, darow, danext,
                  pl.BlockSpec((3, F), lambda i: (0, 0)), pl.BlockSpec((1, F), lambda i: (0, 0))],
        out_specs=[urow, pl.BlockSpec((3, F), lambda i: (0, 0)), pl.BlockSpec((1, F), lambda i: (0, 0))],
        out_shape=[jax.ShapeDtypeStruct((S, F2), MXU_DTYPE), jax.ShapeDtypeStruct((3, F), F32),
                   jax.ShapeDtypeStruct((1, F), F32)],
        compiler_params=_cparams(("arbitrary",)),
    )(u, u, u, da, da, cw, cb)


def _adamw_math(w, g, m, v):
    m = ADAM_B1 * m + (1.0 - ADAM_B1) * g
    v = ADAM_B2 * v + (1.0 - ADAM_B2) * (g * g)
    m_hat = m / (1.0 - ADAM_B1 ** ADAM_STEP)
    v_hat = v / (1.0 - ADAM_B2 ** ADAM_STEP)
    delta = -ADAM_LR * (m_hat / (jnp.sqrt(v_hat) + ADAM_EPS) + ADAM_WD * w)
    return delta, m, v


def _adamw_big(w, g, m, v, *, name):
    R, C = w.shape
    tr = _pick(R, (256, 128, 64, 32, 16, 8))

    def body(w_ref, g_ref, m_ref, v_ref, d_ref, nm_ref, nv_ref):
        d, nm, nv = _adamw_math(w_ref[...], g_ref[...], m_ref[...], v_ref[...])
        d_ref[...] = d
        nm_ref[...] = nm
        nv_ref[...] = nv

    blk = pl.BlockSpec((tr, C), lambda i: (i, 0))
    sh = jax.ShapeDtypeStruct((R, C), F32)
    return pl.pallas_call(
        body, name=name, grid=(R // tr,), in_specs=[blk] * 4, out_specs=[blk] * 3, out_shape=[sh] * 3,
        compiler_params=_cparams(("parallel",)),
    )(w, g, m, v)


def _adamw_small(items, *, name):
    n = len(items)

    def body(*refs):
        ins, outs = refs[:4 * n], refs[4 * n:]
        for t in range(n):
            w, g, m, v = (r[...] for r in ins[4 * t:4 * t + 4])
            d, nm, nv = _adamw_math(w, g, m, v)
            outs[3 * t][...] = d
            outs[3 * t + 1][...] = nm
            outs[3 * t + 2][...] = nv

    flat = [a for it in items for a in it]
    out_shape = [jax.ShapeDtypeStruct(it[0].shape, F32) for it in items for _ in range(3)]
    vm = pl.BlockSpec(memory_space=pltpu.VMEM)
    res = pl.pallas_call(body, name=name, in_specs=[vm] * len(flat), out_specs=[vm] * len(out_shape),
                         out_shape=out_shape)(*flat)
    return [tuple(res[3 * t:3 * t + 3]) for t in range(n)]


def _place():
    return lax.axis_index("x"), lax.axis_index("y"), lax.axis_index("c")


_FLIPS = ((1, 0), (0, 1), (1, 1))


def _gather_shards(wp, *, name):
    R = wp.shape[0]
    Hh = R // 2
    any_spec = pl.BlockSpec(memory_space=pl.ANY)

    def body(w_ref, out_ref, send_sems, recv_sems):
        x, y, c = _place()
        me = 2 * x + y
        half = pl.ds(pl.multiple_of(c * Hh, 16), Hh)
        sibling = (x, y, 1 - c)
        chips = [(x ^ fx, y ^ fy) for fx, fy in _FLIPS]

        def over_ici(j):
            px, py = chips[j]
            return pltpu.make_async_remote_copy(
                src_ref=w_ref.at[half], dst_ref=out_ref.at[me, half],
                send_sem=send_sems.at[j], recv_sem=recv_sems.at[j], device_id=(px, py, c), device_id_type=MESH)

        def landed(j):
            px, py = chips[j]
            blk = out_ref.at[2 * px + py, half]
            return pltpu.make_async_remote_copy(
                src_ref=blk, dst_ref=blk, send_sem=send_sems.at[j], recv_sem=recv_sems.at[j],
                device_id=(px, py, c), device_id_type=MESH)

        def to_sibling(j):
            px, py = chips[j]
            blk = out_ref.at[2 * px + py, half]
            return pltpu.make_async_remote_copy(
                src_ref=blk, dst_ref=blk, send_sem=send_sems.at[3 + j], recv_sem=recv_sems.at[3 + j],
                device_id=sibling, device_id_type=MESH)

        def from_sibling(j):
            px, py = chips[j]
            other = pl.ds(pl.multiple_of((1 - c) * Hh, 16), Hh)
            blk = out_ref.at[2 * px + py, other]
            return pltpu.make_async_remote_copy(
                src_ref=blk, dst_ref=blk, send_sem=send_sems.at[3 + j], recv_sem=recv_sems.at[3 + j],
                device_id=sibling, device_id_type=MESH)

        firsts = [over_ici(j) for j in range(3)]
        for cp in firsts:
            cp.start()
        passed = []
        for j in range(3):
            landed(j).wait_recv()
            cp = to_sibling(j)
            cp.start()
            passed.append(cp)
        for j in range(3):
            from_sibling(j).wait_recv()
        for cp in firsts + passed:
            cp.wait_send()

    return pl.pallas_call(
        body, name=name, in_specs=[any_spec], out_specs=any_spec,
        out_shape=jax.ShapeDtypeStruct((N_CHIPS, R, LANES), wp.dtype),
        scratch_shapes=[pltpu.SemaphoreType.DMA((6,)), pltpu.SemaphoreType.DMA((6,))],
    )(wp)


def _swap_halves(p, *, name):
    _, R, _ = p.shape
    Hh = R // 2
    any_spec = pl.BlockSpec(memory_space=pl.ANY)

    def body(p_ref, out_ref, send_sem, recv_sem):
        x, y, c = _place()
        give = pl.ds(pl.multiple_of((1 - c) * Hh, 8), Hh)
        cp = pltpu.make_async_remote_copy(
            src_ref=p_ref.at[:, give], dst_ref=out_ref, send_sem=send_sem, recv_sem=recv_sem,
            device_id=(x, y, 1 - c), device_id_type=MESH)
        cp.start()
        cp.wait()

    return pl.pallas_call(
        body, name=name, in_specs=[any_spec], out_specs=any_spec,
        out_shape=jax.ShapeDtypeStruct((N_CHIPS, Hh, LANES), p.dtype),
        scratch_shapes=[pltpu.SemaphoreType.DMA, pltpu.SemaphoreType.DMA],
    )(p)


def _add_halves(p, got, c_arr, *, name):
    _, R, _ = p.shape
    Hh = R // 2
    tr = _pick(Hh, (1504, 752, 376, 94 * 8, 8))
    nb = Hh // tr

    def body(c_ref, p_ref, g_ref, o_ref):
        o_ref[...] = (p_ref[...] + g_ref[...]).astype(o_ref.dtype)

    return pl.pallas_call(
        body, name=name,
        grid_spec=pltpu.PrefetchScalarGridSpec(
            num_scalar_prefetch=1, grid=(N_CHIPS, nb),
            in_specs=[pl.BlockSpec((None, tr, LANES), lambda j, i, c_ref: (j, c_ref[0] * nb + i, 0)),
                      pl.BlockSpec((None, tr, LANES), lambda j, i, c_ref: (j, i, 0))],
            out_specs=pl.BlockSpec((None, tr, LANES), lambda j, i, c_ref: (j, i, 0))),
        out_shape=jax.ShapeDtypeStruct((N_CHIPS, Hh, LANES), WIRE_DTYPE),
        compiler_params=_cparams(("parallel", "parallel")),
    )(c_arr, p, got)


def _exchange_chips(qv, *, name):
    any_spec = pl.BlockSpec(memory_space=pl.ANY)

    def body(q_ref, out_ref, send_sems, recv_sems, local_sem):
        x, y, c = _place()
        me = 2 * x + y
        mine = pltpu.make_async_copy(q_ref.at[me], out_ref.at[me], local_sem)
        mine.start()
        chips = [(x ^ fx, y ^ fy) for fx, fy in _FLIPS]
        sends = []
        for j, (px, py) in enumerate(chips):
            cp = pltpu.make_async_remote_copy(
                src_ref=q_ref.at[2 * px + py], dst_ref=out_ref.at[me],
                send_sem=send_sems.at[j], recv_sem=recv_sems.at[j], device_id=(px, py, c), device_id_type=MESH)
            cp.start()
            sends.append(cp)
        for j, (px, py) in enumerate(chips):
            blk = out_ref.at[2 * px + py]
            pltpu.make_async_remote_copy(
                src_ref=blk, dst_ref=blk, send_sem=send_sems.at[j], recv_sem=recv_sems.at[j],
                device_id=(px, py, c), device_id_type=MESH).wait_recv()
        for cp in sends:
            cp.wait_send()
        mine.wait()

    return pl.pallas_call(
        body, name=name, in_specs=[any_spec], out_specs=any_spec,
        out_shape=jax.ShapeDtypeStruct(qv.shape, qv.dtype),
        scratch_shapes=[pltpu.SemaphoreType.DMA((3,)), pltpu.SemaphoreType.DMA((3,)), pltpu.SemaphoreType.DMA],
    )(qv)


def _sum_chips(r2, c_arr, *, name):
    _, Hh, _ = r2.shape
    tr = _pick(Hh, (1504, 752, 376, 8))
    nb = Hh // tr

    def body(c_ref, r_ref, o_ref):
        acc = r_ref[0].astype(F32)
        for j in range(1, N_CHIPS):
            acc = acc + r_ref[j].astype(F32)
        o_ref[...] = acc

    return pl.pallas_call(
        body, name=name,
        grid_spec=pltpu.PrefetchScalarGridSpec(
            num_scalar_prefetch=1, grid=(nb,),
            in_specs=[pl.BlockSpec((N_CHIPS, tr, LANES), lambda i, c_ref: (0, i, 0))],
            out_specs=pl.BlockSpec((tr, LANES), lambda i, c_ref: (c_ref[0] * nb + i, 0))),
        out_shape=jax.ShapeDtypeStruct((2 * Hh, LANES), F32),
        compiler_params=_cparams(("parallel",)),
    )(c_arr, r2)


def _join_halves(f, *, name):
    Hh = f.shape[0] // 2
    any_spec = pl.BlockSpec(memory_space=pl.ANY)

    def body(f_ref, out_ref, send_sem, recv_sem):
        del f_ref
        x, y, c = _place()
        my_rows = out_ref.at[pl.ds(pl.multiple_of(c * Hh, 8), Hh)]
        its_rows = out_ref.at[pl.ds(pl.multiple_of((1 - c) * Hh, 8), Hh)]
        cp = pltpu.make_async_remote_copy(
            src_ref=my_rows, dst_ref=my_rows, send_sem=send_sem, recv_sem=recv_sem,
            device_id=(x, y, 1 - c), device_id_type=MESH)
        cp.start()
        pltpu.make_async_remote_copy(
            src_ref=its_rows, dst_ref=its_rows, send_sem=send_sem, recv_sem=recv_sem,
            device_id=(x, y, 1 - c), device_id_type=MESH).wait_recv()
        cp.wait_send()

    return pl.pallas_call(
        body, name=name, in_specs=[any_spec], out_specs=any_spec,
        out_shape=jax.ShapeDtypeStruct(f.shape, f.dtype), input_output_aliases={0: 0},
        scratch_shapes=[pltpu.SemaphoreType.DMA, pltpu.SemaphoreType.DMA],
    )(f)


def _allreduce_small(buf, *, name):
    R = buf.shape[0]
    vm = pl.BlockSpec(memory_space=pltpu.VMEM)

    def body(b_ref, o_ref, slots, send_sems, recv_sems):
        x, y, c = _place()
        me = 4 * x + 2 * y + c
        slots[me] = b_ref[...]
        sends = []
        for kk in range(1, 8):
            fx, fy, fc = (kk >> 2) & 1, (kk >> 1) & 1, kk & 1
            cp = pltpu.make_async_remote_copy(
                src_ref=b_ref, dst_ref=slots.at[me], send_sem=send_sems.at[kk - 1], recv_sem=recv_sems.at[kk - 1],
                device_id=(x ^ fx, y ^ fy, c ^ fc), device_id_type=MESH)
            cp.start()
            sends.append(cp)
        for kk in range(1, 8):
            fx, fy, fc = (kk >> 2) & 1, (kk >> 1) & 1, kk & 1
            peer = 4 * (x ^ fx) + 2 * (y ^ fy) + (c ^ fc)
            pltpu.make_async_remote_copy(
                src_ref=b_ref, dst_ref=slots.at[peer], send_sem=send_sems.at[kk - 1], recv_sem=recv_sems.at[kk - 1],
                device_id=(x ^ fx, y ^ fy, c ^ fc), device_id_type=MESH).wait_recv()
        for cp in sends:
            cp.wait_send()
        acc = slots[0]
        for d in range(1, 8):
            acc = acc + slots[d]
        o_ref[...] = acc

    return pl.pallas_call(
        body, name=name, in_specs=[vm], out_specs=vm,
        out_shape=jax.ShapeDtypeStruct((R, LANES), F32),
        scratch_shapes=[pltpu.VMEM((8, R, LANES), F32), pltpu.SemaphoreType.DMA((7,)), pltpu.SemaphoreType.DMA((7,))],
    )(buf)


def _pad_rows(v, mult=8 * LANES):
    flat = v.reshape(-1)
    n = flat.shape[0]
    tot = -(-n // mult) * mult
    return jnp.pad(flat, (0, tot - n)).reshape(-1, LANES)


def _pack_small(parts):
    return jnp.concatenate([_pad_rows(p.astype(F32)) for p in parts], axis=0)


def _unpack_small(buf, shapes):
    out, r = [], 0
    for sh in shapes:
        n = math.prod(sh)
        rows = -(-n // (8 * LANES)) * 8
        out.append(buf[r:r + rows].reshape(-1)[:n].reshape(sh))
        r += rows
    return out


def _cols_to_shards(w):
    *lead, K, N = w.shape
    t = w.reshape(*lead, K, N_CHIPS, N // N_CHIPS)
    return jnp.moveaxis(t, -2, 0)


def _shards_to_cols(t):
    t = jnp.moveaxis(t, 0, -2)
    *lead, K, _, n = t.shape
    return t.reshape(*lead, K, N_CHIPS * n)


def _rows_to_shards(w):
    *lead, K, N = w.shape
    t = w.reshape(*lead, N_CHIPS, K // N_CHIPS, N)
    return jnp.moveaxis(t, -3, 0)


def _shards_to_rows(t):
    t = jnp.moveaxis(t, 0, -3)
    *lead, _, k, N = t.shape
    return t.reshape(*lead, N_CHIPS * k, N)


_BIG = (("m_w_in", True), ("m_w_out", False), ("a_w_in", True), ("a_w_out", False), ("f_w_up", True), ("f_w_down", False))


def kernel(x, m_w_in, m_gate_bias, m_head_norm, m_w_out, a_w_in, a_b_in, a_sinks, a_w_out, a_b_out, norm_mix_pre, norm_mix_post, norm_ffn_pre, norm_ffn_post, f_w_up, f_conv_w, f_conv_b, f_w_down, loss_target, m_m_w_in, m_m_gate_bias, m_m_head_norm, m_m_w_out, m_a_w_in, m_a_b_in, m_a_sinks, m_a_w_out, m_a_b_out, m_norm_mix_pre, m_norm_mix_post, m_norm_ffn_pre, m_norm_ffn_post, m_f_w_up, m_f_conv_w, m_f_conv_b, m_f_w_down, v_m_w_in, v_m_gate_bias, v_m_head_norm, v_m_w_out, v_a_w_in, v_a_b_in, v_a_sinks, v_a_w_out, v_a_b_out, v_norm_mix_pre, v_norm_mix_post, v_norm_ffn_pre, v_norm_ffn_post, v_f_w_up, v_f_conv_w, v_f_conv_b, v_f_w_down):
    params = dict(m_w_in=m_w_in, m_gate_bias=m_gate_bias, m_head_norm=m_head_norm, m_w_out=m_w_out, a_w_in=a_w_in,
                  a_b_in=a_b_in, a_sinks=a_sinks, a_w_out=a_w_out, a_b_out=a_b_out, norm_mix_pre=norm_mix_pre,
                  norm_mix_post=norm_mix_post, norm_ffn_pre=norm_ffn_pre, norm_ffn_post=norm_ffn_post, f_w_up=f_w_up,
                  f_conv_w=f_conv_w, f_conv_b=f_conv_b, f_w_down=f_w_down)
    mom1 = dict(m_w_in=m_m_w_in, m_gate_bias=m_m_gate_bias, m_head_norm=m_m_head_norm, m_w_out=m_m_w_out,
                a_w_in=m_a_w_in, a_b_in=m_a_b_in, a_sinks=m_a_sinks, a_w_out=m_a_w_out, a_b_out=m_a_b_out,
                norm_mix_pre=m_norm_mix_pre, norm_mix_post=m_norm_mix_post, norm_ffn_pre=m_norm_ffn_pre,
                norm_ffn_post=m_norm_ffn_post, f_w_up=m_f_w_up, f_conv_w=m_f_conv_w, f_conv_b=m_f_conv_b,
                f_w_down=m_f_w_down)
    mom2 = dict(m_w_in=v_m_w_in, m_gate_bias=v_m_gate_bias, m_head_norm=v_m_head_norm, m_w_out=v_m_w_out,
                a_w_in=v_a_w_in, a_b_in=v_a_b_in, a_sinks=v_a_sinks, a_w_out=v_a_w_out, a_b_out=v_a_b_out,
                norm_mix_pre=v_norm_mix_pre, norm_mix_post=v_norm_mix_post, norm_ffn_pre=v_norm_ffn_pre,
                norm_ffn_post=v_norm_ffn_post, f_w_up=v_f_w_up, f_conv_w=v_f_conv_w, f_conv_b=v_f_conv_b,
                f_w_down=v_f_w_down)
    order = list(params)

    mx, my, mc = _place()
    chip = 2 * mx + my
    h0 = x[0]
    target = loss_target[0]

    piece_rows = [math.prod(params[n].shape) // LANES for n, _ in _BIG]
    wp = jnp.concatenate([params[n].astype(MXU_DTYPE).reshape(-1, LANES) for n, _ in _BIG], axis=0)
    wg = lax.dynamic_update_slice(_gather_shards(wp, name="gather_weights"), wp[None], (chip, 0, 0))
    full, r = {}, 0
    for (n, by_cols), rows in zip(_BIG, piece_rows):
        t = wg[:, r:r + rows].reshape((N_CHIPS,) + params[n].shape)
        full[n] = _shards_to_cols(t) if by_cols else _shards_to_rows(t)
        r += rows

    def in_place(shard, axis):
        width = shard.shape[axis]
        z = jnp.zeros(shard.shape[:axis] + (N_CHIPS * width,) + shard.shape[axis + 1:], F32)
        contrib = jnp.where(mc == 0, shard, 0.0)
        return lax.dynamic_update_slice_in_dim(z, contrib, chip * width, axis)

    sm_in = [in_place(a_b_in, 1), in_place(a_b_out, 1), in_place(f_conv_w, 2)]
    sm_full = _unpack_small(_allreduce_small(_pack_small(sm_in), name="gather_small"), [t.shape for t in sm_in])
    b_in_full, b_out_full, conv_w_full = sm_full

    W_in = full["m_w_in"][0]
    nqk = M_HEADS * M_QK
    n_main = 2 * nqk + M_HEADS * M_V + D_MODEL

    def pad_heads(w):
        return jnp.pad(w.reshape(D_MODEL, M_HEADS, M_QK), ((0, 0), (0, 0), (0, LANES - M_QK))).reshape(D_MODEL, -1)

    def unpad_heads(w):
        return w.reshape(D_MODEL, M_HEADS, LANES)[:, :, :M_QK].reshape(D_MODEL, nqk)

    W_all = jnp.concatenate([pad_heads(W_in[:, :nqk]), pad_heads(W_in[:, nqk:2 * nqk]), W_in[:, 2 * nqk:n_main],
                             jnp.pad(W_in[:, n_main:], ((0, 0), (0, LANES - 2 * M_HEADS)))], axis=1)
    gbias = jnp.pad(m_gate_bias[0].reshape(1, 2 * M_HEADS), ((0, 0), (0, LANES - 2 * M_HEADS)))
    W_mout = full["m_w_out"][0]
    A_in, A_out = full["a_w_in"][0], full["a_w_out"][0]
    W_up, W_down = full["f_w_up"], full["f_w_down"]

    grads = {}

    def ffn_fwd(i, h1):
        z1 = _rms_fwd(h1, norm_ffn_pre[i:i + 1], out_dtype=MXU_DTYPE, name=f"ffn_pre_norm{i}")
        u = _mm(z1, W_up[i], name=f"ffn_up{i}")
        a = _convgate_fwd(u, conv_w_full[i], f_conv_b[i:i + 1], name=f"ffn_act{i}")
        zf = _mm(a, W_down[i], name=f"ffn_down{i}")
        h2 = _rms_fwd(zf, norm_ffn_post[i:i + 1], res=h1, out_dtype=F32, name=f"ffn_post_norm{i}")
        return h2, (z1, u, a, zf)

    z0 = _rms_fwd(h0, norm_mix_pre[0:1], out_dtype=MXU_DTYPE, name="mix_pre_norm0")
    proj = _mm(z0, W_all, name="mlstm_proj")
    gated, C_all, n_all, m_all = _mlstm_fwd(proj, gbias, m_head_norm, name="mlstm_fwd")
    zm0 = _mm(gated, W_mout, name="mlstm_out")
    h1 = _rms_fwd(zm0, norm_mix_post[0:1], res=h0, out_dtype=F32, name="mix_post_norm0")
    h2, ffn0 = ffn_fwd(0, h1)

    y0 = _rms_fwd(h2, norm_mix_pre[1:2], out_dtype=MXU_DTYPE, name="mix_pre_norm1")
    aproj = _mm(y0, A_in, bias=b_in_full, name="attn_proj")
    sinks = a_sinks.reshape(A_KVH, A_G)
    ao = _attn_fwd(aproj, sinks, name="attn_fwd")
    zm1 = _mm(ao, A_out, bias=b_out_full, name="attn_out")
    h3 = _rms_fwd(zm1, norm_mix_post[1:2], res=h2, out_dtype=F32, name="mix_post_norm1")
    h4, ffn1 = ffn_fwd(1, h3)

    loss_part, dh = _loss_head(h4, target, name="loss_head")
    loss = lax.psum(loss_part[0, 0], ("x", "y", "c"))

    g_post, g_fpre, g_mpost, g_mpre = [None, None], [None, None], [None, None], [None, None]
    dW_up, dW_down, dconv_w, dconv_b = [None, None], [None, None], [None, None], [None, None]

    def ffn_bwd(i, dh2, h1_, saved):
        z1, u, a, zf = saved
        dzf, g_post[i], _ = _rms_bwd(zf, norm_ffn_post[i:i + 1], dh2, out_dtype=MXU_DTYPE, name=f"ffn_post_norm_bwd{i}")
        da = _mm(dzf, W_down[i], tb=True, name=f"ffn_down_dx{i}")
        dW_down[i] = _mm(a, dzf, ta=True, name=f"ffn_down_dw{i}")
        du, dconv_w[i], dconv_b[i] = _convgate_bwd(u, da, conv_w_full[i], f_conv_b[i:i + 1], name=f"ffn_act_bwd{i}")
        dz1 = _mm(du, W_up[i], tb=True, name=f"ffn_up_dx{i}")
        dW_up[i] = _mm(z1, du, ta=True, name=f"ffn_up_dw{i}")
        dh1, g_fpre[i], _ = _rms_bwd(h1_, norm_ffn_pre[i:i + 1], dz1, res=dh2, out_dtype=F32, name=f"ffn_pre_norm_bwd{i}")
        return dh1

    dh3 = ffn_bwd(1, dh, h3, ffn1)
    dzm1, g_mpost[1], db_out = _rms_bwd(zm1, norm_mix_post[1:2], dh3, out_dtype=MXU_DTYPE, name="mix_post_norm_bwd1")
    dao = _mm(dzm1, A_out, tb=True, name="attn_out_dx")
    dA_out = _mm(ao, dzm1, ta=True, name="attn_out_dw")
    daq, dakv, dsinks, cs_q, cs_kv = _attn_bwd(aproj, sinks, dao, name="attn_bwd")
    db_in = jnp.concatenate([cs_q, cs_kv], axis=1)
    dy0 = _mm(daq, A_in[:, :A_NQ], tb=True, add=_mm(dakv, A_in[:, A_NQ:], tb=True, name="attn_proj_kv_dx"),
              name="attn_proj_q_dx")
    dA_in = jnp.concatenate([_mm(y0, daq, ta=True, name="attn_proj_q_dw"),
                             _mm(y0, dakv, ta=True, name="attn_proj_kv_dw")], axis=1)
    dh2, g_mpre[1], _ = _rms_bwd(h2, norm_mix_pre[1:2], dy0, res=dh3, out_dtype=F32, name="mix_pre_norm_bwd1")

    dh1 = ffn_bwd(0, dh2, h1, ffn0)
    dzm0, g_mpost[0], _ = _rms_bwd(zm0, norm_mix_post[0:1], dh1, out_dtype=MXU_DTYPE, name="mix_post_norm_bwd0")
    dgated = _mm(dzm0, W_mout, tb=True, name="mlstm_out_dx")
    dW_mout = _mm(gated, dzm0, ta=True, name="mlstm_out_dw")
    dproj, dgbias, dhn = _mlstm_bwd(proj, gbias, m_head_norm, C_all, n_all, m_all, dgated, name="mlstm_bwd")
    dz0 = _mm(dproj, W_all, tb=True, name="mlstm_proj_dx")
    dW_all = _mm(z0, dproj, ta=True, name="mlstm_proj_dw")
    dW_min = jnp.concatenate([unpad_heads(dW_all[:, M_OFF_Q:M_OFF_K]), unpad_heads(dW_all[:, M_OFF_K:M_OFF_V]),
                              dW_all[:, M_OFF_V:M_OFF_G], dW_all[:, M_OFF_G:M_OFF_G + 2 * M_HEADS]], axis=1)
    grad_x, g_mpre[0], _ = _rms_bwd(h0, norm_mix_pre[0:1], dz0, res=dh1, out_dtype=F32, name="mix_pre_norm_bwd0")

    big_g = dict(m_w_in=dW_min[None], m_w_out=dW_mout[None], a_w_in=dA_in[None], a_w_out=dA_out[None],
                 f_w_up=jnp.stack(dW_up), f_w_down=jnp.stack(dW_down))
    packed = jnp.concatenate(
        [(_cols_to_shards(big_g[n]) if by_cols else _rows_to_shards(big_g[n])).reshape(N_CHIPS, -1, LANES)
         for n, by_cols in _BIG], axis=1)
    got = _swap_halves(packed, name="grad_swap_halves")
    c_arr = jnp.reshape(mc, (1,)).astype(jnp.int32)
    chip_part = _add_halves(packed, got, c_arr, name="grad_add_halves")
    from_chips = _exchange_chips(chip_part, name="grad_exchange_chips")
    my_half = _sum_chips(from_chips, c_arr, name="grad_sum_chips")
    shard_g = _join_halves(my_half, name="grad_join_halves")
    r = 0
    for (n, _), rows in zip(_BIG, piece_rows):
        grads[n] = shard_g[r:r + rows].reshape(params[n].shape)
        r += rows

    small_g = [
        dgbias[:, :2 * M_HEADS].reshape(1, 2, M_HEADS),
        dhn,
        dsinks.reshape(1, A_QH),
        db_in, db_out,
        jnp.concatenate(g_mpre), jnp.concatenate(g_mpost), jnp.concatenate(g_fpre), jnp.concatenate(g_post),
        jnp.stack(dconv_w), jnp.concatenate(dconv_b),
    ]
    small_names = ["m_gate_bias", "m_head_norm", "a_sinks", "a_b_in", "a_b_out", "norm_mix_pre", "norm_mix_post",
                   "norm_ffn_pre", "norm_ffn_post", "f_conv_w", "f_conv_b"]
    red = _unpack_small(_allreduce_small(_pack_small(small_g), name="reduce_small"), [t.shape for t in small_g])
    for n, t in zip(small_names, red):
        if n in ("a_b_in", "a_b_out", "f_conv_w"):
            axis = t.ndim - 1
            width = params[n].shape[axis]
            t = lax.dynamic_slice_in_dim(t, chip * width, width, axis)
        grads[n] = t

    deltas, new_m, new_v = {}, {}, {}
    for n, _ in _BIG:
        sh = params[n].shape
        two = lambda t: t.reshape(-1, sh[-1])
        d, nm, nv = _adamw_big(two(params[n]), two(grads[n]), two(mom1[n]), two(mom2[n]), name=f"adamw_{n}")
        deltas[n], new_m[n], new_v[n] = d.reshape(sh), nm.reshape(sh), nv.reshape(sh)
    two = lambda t: t.reshape(-1, t.shape[-1])
    res = _adamw_small([(two(params[n]), two(grads[n]), two(mom1[n]), two(mom2[n])) for n in small_names],
                       name="adamw_small")
    for n, (d, nm, nv) in zip(small_names, res):
        sh = params[n].shape
        deltas[n], new_m[n], new_v[n] = d.reshape(sh), nm.reshape(sh), nv.reshape(sh)

    return (loss, grad_x[None], *[grads[n] for n in order], *[deltas[n] for n in order],
            *[new_m[n] for n in order], *[new_v[n] for n in order])
```

```python
import functools
import math

import jax
import jax.numpy as jnp
from jax import lax
from jax.experimental import pallas as pl
from jax.experimental.pallas import tpu as pltpu

F32 = jnp.float32
MXU_DTYPE = jnp.bfloat16
WIRE_DTYPE = jnp.bfloat16
MESH = pl.DeviceIdType.MESH

D_MODEL = 1024
EPS = 1e-6
M_HEADS, M_QK, M_V, M_CHUNK = 8, 64, 128, 128
GATE_CAP = 15.0
A_DH, A_QH, A_KVH, A_G, A_BLK = 64, 16, 2, 8, 128
D_FF = 2816
N_CHIPS = 4
LANES = 128
VMEM_LIMIT = 48 * 1024 * 1024

ADAM_LR, ADAM_B1, ADAM_B2, ADAM_EPS, ADAM_WD, ADAM_STEP = 0.001, 0.9, 0.999, 1e-08, 0.01, 10


def _cparams(sem):
    return pltpu.CompilerParams(dimension_semantics=sem, vmem_limit_bytes=VMEM_LIMIT)


def _pick(n, cands):
    for c in cands:
        if n % c == 0:
            return c
    return n


def _mm(a, b, *, ta=False, tb=False, out_dtype=F32, bias=None, add=None, b_layer=None, out_stacked=False, name):
    if ta:
        K, M = a.shape
    else:
        M, K = a.shape
    b_stacked = b_layer is not None
    if b_stacked:
        assert not ta
        n_sh = b.shape[2]
        w_rows, w_cols = D_MODEL, N_CHIPS * n_sh
        N, Kb = (w_rows, w_cols) if tb else (w_cols, w_rows)
    elif tb:
        N, Kb = b.shape
    else:
        Kb, N = b.shape
    assert K == Kb, (a.shape, b.shape)
    tm = _pick(M, (1024, 1408, 512, 256, 128))
    tn = _pick(N, (1024, 1408, 1280, 512, 256, 128))
    tk = K if K <= 2816 else _pick(K, (2816, 2048, 1408, 1024, 512, 256, 128))
    if b_stacked and tb:
        tk = n_sh
    if (b_stacked and not tb) or out_stacked:
        tn = N // N_CHIPS
    nk = K // tk
    dn = (((0 if ta else 1,), (1 if tb else 0,)), ((), ()))
    has_bias, has_add = bias is not None, add is not None

    def body(*refs):
        a_ref, b_ref = refs[0], refs[1]
        pos = 2
        bias_ref = add_ref = None
        if has_bias:
            bias_ref = refs[pos]
            pos += 1
        if has_add:
            add_ref = refs[pos]
            pos += 1
        o_ref = refs[pos]
        acc_ref = refs[pos + 1] if nk > 1 else None

        def finish(r):
            if has_bias:
                r = r + bias_ref[...]
            if has_add:
                r = r + add_ref[...]
            o_ref[...] = r.astype(out_dtype)

        part = lax.dot_general(a_ref[...], b_ref[...], dn, preferred_element_type=F32)
        if nk == 1:
            finish(part)
        else:
            k = pl.program_id(2)

            @pl.when(k == 0)
            def _():
                acc_ref[...] = part

            @pl.when(k > 0)
            def _():
                acc_ref[...] += part

            @pl.when(k == nk - 1)
            def _():
                finish(acc_ref[...])

    a_spec = pl.BlockSpec((tk, tm), lambda i, j, k: (k, i)) if ta else pl.BlockSpec((tm, tk), lambda i, j, k: (i, k))
    if b_stacked and tb:
        off = b_layer * (w_rows // tn)
        b_spec = pl.BlockSpec((None, tn, tk), lambda i, j, k: (k, off + j, 0))
    elif b_stacked:
        off = b_layer * (w_rows // tk)
        b_spec = pl.BlockSpec((None, tk, tn), lambda i, j, k: (j, off + k, 0))
    elif tb:
        b_spec = pl.BlockSpec((tn, tk), lambda i, j, k: (j, k))
    else:
        b_spec = pl.BlockSpec((tk, tn), lambda i, j, k: (k, j))
    if out_stacked:
        out_spec = pl.BlockSpec((None, tm, tn), lambda i, j, k: (j, i, 0))
        out_shape = jax.ShapeDtypeStruct((N_CHIPS, M, tn), out_dtype)
    else:
        out_spec = pl.BlockSpec((tm, tn), lambda i, j, k: (i, j))
        out_shape = jax.ShapeDtypeStruct((M, N), out_dtype)
    in_specs, args = [a_spec, b_spec], [a, b]
    if has_bias:
        in_specs.append(pl.BlockSpec((1, tn), lambda i, j, k: (0, j)))
        args.append(bias)
    if has_add:
        in_specs.append(pl.BlockSpec((tm, tn), lambda i, j, k: (i, j)))
        args.append(add)
    return pl.pallas_call(
        body, name=name,
        grid=(M // tm, N // tn, nk),
        in_specs=in_specs,
        out_specs=out_spec,
        out_shape=out_shape,
        scratch_shapes=[pltpu.VMEM((tm, tn), F32)] if nk > 1 else [],
        compiler_params=_cparams(("parallel", "parallel", "arbitrary")),
    )(*args)


ROW_TILE = 512


def _rms_fwd(x, g, *, res=None, out_dtype, name):
    S, D = x.shape
    tm = _pick(S, (ROW_TILE, 256, 128))
    has_res = res is not None

    def body(*refs):
        x_ref, g_ref = refs[0], refs[1]
        res_ref = refs[2] if has_res else None
        o_ref = refs[-1]
        xv = x_ref[...]
        y = xv * lax.rsqrt(jnp.mean(xv * xv, axis=-1, keepdims=True) + EPS) * g_ref[...]
        if has_res:
            y = res_ref[...] + y
        o_ref[...] = y.astype(out_dtype)

    row = pl.BlockSpec((tm, D), lambda i: (i, 0))
    in_specs, args = [row, pl.BlockSpec((1, D), lambda i: (0, 0))], [x, g]
    if has_res:
        in_specs.append(row)
        args.append(res)
    return pl.pallas_call(
        body, name=name, grid=(S // tm,), in_specs=in_specs, out_specs=row,
        out_shape=jax.ShapeDtypeStruct((S, D), out_dtype),
        compiler_params=_cparams(("parallel",)),
    )(*args)


def _rms_bwd(x, g, dy, *, res=None, out_dtype, name):
    S, D = x.shape
    tm = _pick(S, (ROW_TILE, 256, 128))
    has_res = res is not None

    def body(*refs):
        x_ref, g_ref, dy_ref = refs[0], refs[1], refs[2]
        res_ref = refs[3] if has_res else None
        dx_ref, dg_ref, cs_ref = refs[-3], refs[-2], refs[-1]
        i = pl.program_id(0)
        xv = x_ref[...]
        dyv = dy_ref[...].astype(F32)
        r = lax.rsqrt(jnp.mean(xv * xv, axis=-1, keepdims=True) + EPS)
        xh = xv * r
        gd = dyv * g_ref[...]
        dx = r * (gd - xh * jnp.mean(gd * xh, axis=-1, keepdims=True))
        cs = jnp.sum(dx, axis=0, keepdims=True)
        if has_res:
            dx = res_ref[...] + dx
        dx_ref[...] = dx.astype(out_dtype)
        dg = jnp.sum(dyv * xh, axis=0, keepdims=True)

        @pl.when(i == 0)
        def _():
            dg_ref[...] = dg
            cs_ref[...] = cs

        @pl.when(i > 0)
        def _():
            dg_ref[...] += dg
            cs_ref[...] += cs

    row = pl.BlockSpec((tm, D), lambda i: (i, 0))
    vec = pl.BlockSpec((1, D), lambda i: (0, 0))
    in_specs, args = [row, vec, row], [x, g, dy]
    if has_res:
        in_specs.append(row)
        args.append(res)
    return pl.pallas_call(
        body, name=name, grid=(S // tm,), in_specs=in_specs, out_specs=[row, vec, vec],
        out_shape=[jax.ShapeDtypeStruct((S, D), out_dtype), jax.ShapeDtypeStruct((1, D), F32),
                   jax.ShapeDtypeStruct((1, D), F32)],
        compiler_params=_cparams(("arbitrary",)),
    )(*args)


def _loss_head(y, target, *, name):
    S, D = y.shape
    tm = _pick(S, (ROW_TILE, 256, 128))

    def body(y_ref, t_ref, loss_ref, dy_ref):
        i = pl.program_id(0)
        err = y_ref[...] - t_ref[...]
        dy_ref[...] = err * (1.0 / D)
        part = 0.5 * jnp.sum(jnp.mean(err * err, axis=-1, keepdims=True), axis=0, keepdims=True)

        @pl.when(i == 0)
        def _():
            loss_ref[...] = part

        @pl.when(i > 0)
        def _():
            loss_ref[...] += part

    row = pl.BlockSpec((tm, D), lambda i: (i, 0))
    return pl.pallas_call(
        body, name=name, grid=(S // tm,), in_specs=[row, row],
        out_specs=[pl.BlockSpec((1, 1), lambda i: (0, 0)), row],
        out_shape=[jax.ShapeDtypeStruct((1, 1), F32), jax.ShapeDtypeStruct((S, D), F32)],
        compiler_params=_cparams(("arbitrary",)),
    )(y, target)


def _mx(t):
    return t.astype(MXU_DTYPE)


def _mxf(t):
    return t.astype(MXU_DTYPE).astype(F32)


def _rawdot(a, b, ca, cb):
    return lax.dot_general(_mx(a), _mx(b), (((ca,), (cb,)), ((), ())), preferred_element_type=F32)


@functools.partial(jax.custom_vjp, nondiff_argnums=(2, 3))
def _dot(a, b, ca, cb):
    return _rawdot(a, b, ca, cb)


def _dot_fwd(a, b, ca, cb):
    return _rawdot(a, b, ca, cb), (a, b)


def _dot_bwd(ca, cb, res, g):
    a, b = res
    bj = 1 if cb == 0 else 0
    ai = 0 if ca == 1 else 1
    da = _rawdot(g, b, 1, bj) if ca == 1 else _rawdot(b, g, bj, 1)
    db = _rawdot(a, g, ai, 0) if cb == 0 else _rawdot(g, a, 0, ai)
    return da.astype(a.dtype), db.astype(b.dtype)


_dot.defvjp(_dot_fwd, _dot_bwd)


def _softcap(z):
    return GATE_CAP * jnp.tanh(z / GATE_CAP)


def _log_sigmoid(z):
    return jnp.minimum(z, 0.0) - jnp.log(1.0 + jnp.exp(-jnp.abs(z)))


def _sigmoid(z):
    return 0.5 * jnp.tanh(0.5 * z) + 0.5


def _lane_col(t, lane_index):
    lane = lax.broadcasted_iota(jnp.int32, t.shape, 1)
    return jnp.sum(jnp.where(lane == lane_index, t, 0.0), axis=1, keepdims=True)


def _mlstm_gates(G, bias):
    L = G.shape[0]
    z = _softcap(G + bias)
    ig = z
    lf = _log_sigmoid(z)
    ti = lax.broadcasted_iota(jnp.int32, (L, L), 0)
    si = lax.broadcasted_iota(jnp.int32, (L, L), 1)
    tril = (si <= ti).astype(F32)
    b = lax.dot_general(tril, lf, (((1,), (0,)), ((), ())), precision=lax.Precision.HIGHEST, preferred_element_type=F32)
    bL = jnp.sum(lf, axis=0, keepdims=True)
    return ig, b, ig.T, b.T, bL


def _mlstm_head(h, q, k, v, o, ig_all, b_all, igT, bT, bL_all, hn, C, n, m):
    L = q.shape[0]
    ti = lax.broadcasted_iota(jnp.int32, (L, L), 0)
    si = lax.broadcasted_iota(jnp.int32, (L, L), 1)
    lower = (si <= ti)
    ig = _lane_col(ig_all, h)
    b = _lane_col(b_all, M_HEADS + h)
    ig_row = igT[h:h + 1, :]
    b_row = bT[M_HEADS + h:M_HEADS + h + 1, :]
    bL = _lane_col(bL_all, M_HEADS + h)
    inter = b + m
    dlog = jnp.where(lower, b - b_row + ig_row, -jnp.inf)
    m_t = lax.stop_gradient(jnp.maximum(inter, jnp.max(dlog, axis=-1, keepdims=True)))
    qs = q * (M_QK ** -0.5)
    w = _dot(qs, k, 1, 1) * jnp.exp(dlog - m_t)
    s_inter = jnp.exp(inter - m_t)
    num = _dot(w, v, 1, 0) + s_inter * _dot(qs, C, 1, 0)
    den = jnp.sum(w, axis=-1, keepdims=True) + s_inter * jnp.sum(_mxf(qs) * _mxf(n), axis=-1, keepdims=True)
    hout = num * (1.0 / jnp.maximum(jnp.abs(den), jnp.exp(-m_t)))
    tail = bL - b + ig
    m_new = lax.stop_gradient(jnp.maximum(bL + m, jnp.max(tail, axis=0, keepdims=True)))
    ws = jnp.exp(tail - m_new)
    decay = jnp.exp(bL + m - m_new)
    wk = ws * k
    C_new = decay * C + _dot(wk, v, 0, 0)
    n_new = decay * n + jnp.sum(_mxf(ws) * _mxf(k), axis=0, keepdims=True)
    hs = hout * lax.rsqrt(jnp.mean(hout * hout, axis=-1, keepdims=True) + EPS) * hn
    gated = _sigmoid(o) * hs
    return (gated, C_new, n_new), m_new


M_OFF_Q, M_OFF_K, M_OFF_V, M_OFF_O, M_OFF_G = 0, M_HEADS * LANES, 2 * M_HEADS * LANES, 3 * M_HEADS * LANES, 4 * M_HEADS * LANES
M_PROJ = M_OFF_G + LANES


def _head_cols(off, h):
    return slice(off + h * LANES, off + (h + 1) * LANES)


def _mlstm_specs(NC, rev):
    H, L = M_HEADS, M_CHUNK
    cc = (lambda c: NC - 1 - c) if rev else (lambda c: c)
    proj = pl.BlockSpec((L, M_PROJ), lambda c: (cc(c), 0))
    vec = pl.BlockSpec((1, LANES), lambda c: (0, 0))
    hn = pl.BlockSpec((1, H * M_V), lambda c: (0, 0))
    hv = pl.BlockSpec((L, H * M_V), lambda c: (cc(c), 0))
    Cs = pl.BlockSpec((None, H, LANES, M_V), lambda c: (cc(c), 0, 0, 0))
    ns = pl.BlockSpec((None, H, 1, LANES), lambda c: (cc(c), 0, 0, 0))
    ms = pl.BlockSpec((None, H, 1, 1), lambda c: (cc(c), 0, 0, 0))
    return proj, vec, hn, hv, Cs, ns, ms


_MLSTM_STATE = [pltpu.VMEM((M_HEADS, LANES, M_V), F32), pltpu.VMEM((M_HEADS, 1, LANES), F32),
                pltpu.VMEM((M_HEADS, 1, 1), F32)]


def _mlstm_fwd(proj, bias, hn, *, name):
    S = proj.shape[0]
    H, NC = M_HEADS, S // M_CHUNK
    ps, vec, hns, hv, Cs, ns, ms = _mlstm_specs(NC, False)

    def body(p_ref, b_ref, hn_ref, gated_ref, C_all, n_all, m_all, C_s, n_s, m_s):
        @pl.when(pl.program_id(0) == 0)
        def _():
            C_s[...] = jnp.zeros_like(C_s)
            n_s[...] = jnp.zeros_like(n_s)
            m_s[...] = jnp.zeros_like(m_s)

        gate_terms = _mlstm_gates(p_ref[:, M_OFF_G:M_OFF_G + LANES], b_ref[...])
        for h in range(H):
            C, n, m = C_s[h], n_s[h], m_s[h]
            C_all[h] = C
            n_all[h] = n
            m_all[h] = m
            (gated, Cn, nn), mn = _mlstm_head(
                h, p_ref[:, _head_cols(M_OFF_Q, h)], p_ref[:, _head_cols(M_OFF_K, h)], p_ref[:, _head_cols(M_OFF_V, h)],
                p_ref[:, _head_cols(M_OFF_O, h)], *gate_terms, hn_ref[:, _head_cols(0, h)], C, n, m)
            gated_ref[:, _head_cols(0, h)] = gated.astype(gated_ref.dtype)
            C_s[h] = Cn
            n_s[h] = nn
            m_s[h] = mn

    return pl.pallas_call(
        body, name=name, grid=(NC,),
        in_specs=[ps, vec, hns],
        out_specs=[hv, Cs, ns, ms],
        out_shape=[jax.ShapeDtypeStruct((S, H * M_V), MXU_DTYPE),
                   jax.ShapeDtypeStruct((NC, H, LANES, M_V), F32),
                   jax.ShapeDtypeStruct((NC, H, 1, LANES), F32),
                   jax.ShapeDtypeStruct((NC, H, 1, 1), F32)],
        scratch_shapes=list(_MLSTM_STATE),
        compiler_params=_cparams(("arbitrary",)),
    )(proj, bias, hn)


def _mlstm_bwd(proj, bias, hn, C_all, n_all, m_all, dgated, *, name):
    S = proj.shape[0]
    H, NC = M_HEADS, S // M_CHUNK
    ps, vec, hns, hv, Cs, ns, ms = _mlstm_specs(NC, True)

    def body(p_ref, b_ref, hn_ref, C_ref, n_ref, m_ref, dg_ref, dp_ref, db_ref, dhn_ref, dC_s, dn_s):
        @pl.when(pl.program_id(0) == 0)
        def _():
            dC_s[...] = jnp.zeros_like(dC_s)
            dn_s[...] = jnp.zeros_like(dn_s)
            db_ref[...] = jnp.zeros_like(db_ref)
            dhn_ref[...] = jnp.zeros_like(dhn_ref)

        gate_terms, gates_vjp = jax.vjp(_mlstm_gates, p_ref[:, M_OFF_G:M_OFF_G + LANES], b_ref[...])
        d_terms = [jnp.zeros_like(t) for t in gate_terms]
        for h in range(H):
            def head(q, k, v, o, *rest, h=h):
                return _mlstm_head(h, q, k, v, o, *rest, m_ref[h])

            prim = (p_ref[:, _head_cols(M_OFF_Q, h)], p_ref[:, _head_cols(M_OFF_K, h)], p_ref[:, _head_cols(M_OFF_V, h)],
                    p_ref[:, _head_cols(M_OFF_O, h)], *gate_terms, hn_ref[:, _head_cols(0, h)], C_ref[h], n_ref[h])
            _, vjp, _ = jax.vjp(head, *prim, has_aux=True)
            dq, dk, dv, do, *d_gate, dhnh, dC, dn = vjp((dg_ref[:, _head_cols(0, h)].astype(F32), dC_s[h], dn_s[h]))
            dp_ref[:, _head_cols(M_OFF_Q, h)] = dq.astype(dp_ref.dtype)
            dp_ref[:, _head_cols(M_OFF_K, h)] = dk.astype(dp_ref.dtype)
            dp_ref[:, _head_cols(M_OFF_V, h)] = dv.astype(dp_ref.dtype)
            dp_ref[:, _head_cols(M_OFF_O, h)] = do.astype(dp_ref.dtype)
            d_terms = [a + g for a, g in zip(d_terms, d_gate)]
            dhn_ref[:, _head_cols(0, h)] += dhnh
            dC_s[h] = dC
            dn_s[h] = dn
        dG, dbias = gates_vjp(tuple(d_terms))
        dp_ref[:, M_OFF_G:M_OFF_G + LANES] = dG.astype(dp_ref.dtype)
        db_ref[...] += dbias

    return pl.pallas_call(
        body, name=name, grid=(NC,),
        in_specs=[ps, vec, hns, Cs, ns, ms, hv],
        out_specs=[ps, vec, hns],
        out_shape=[jax.ShapeDtypeStruct((S, M_PROJ), MXU_DTYPE), jax.ShapeDtypeStruct((1, LANES), F32),
                   jax.ShapeDtypeStruct((1, H * M_V), F32)],
        scratch_shapes=list(_MLSTM_STATE[:2]),
        compiler_params=_cparams(("arbitrary",)),
    )(proj, bias, hn, C_all, n_all, m_all, dgated)


A_NQ = A_QH * A_DH
A_NKV = 2 * A_KVH * A_DH
A_PAIRS = A_G // 2


def _attn_group(Ps, KLO, KHI, VLO, VHI, sinks, first):
    B = Ps[0].shape[0]
    R = len(Ps) * B
    q2 = jnp.concatenate(Ps, axis=0) * (A_DH ** -0.5)
    s = jnp.concatenate([_dot(q2, KLO, 1, 1), _dot(q2, KHI, 1, 1)], axis=0)
    qi = lax.broadcasted_iota(jnp.int32, (2 * R, 2 * B), 0) & (B - 1)
    ku = lax.broadcasted_iota(jnp.int32, (2 * R, 2 * B), 1)
    diff = qi - (ku - B)
    mask = (diff >= 0) & (diff < B) & ((ku >= B) | jnp.logical_not(first))
    s = jnp.where(mask, s, -jnp.inf)
    ri = lax.broadcasted_iota(jnp.int32, (2 * R, A_G), 0)
    head = 2 * lax.shift_right_logical(ri & (R - 1), B.bit_length() - 1) + lax.shift_right_logical(ri, R.bit_length() - 1)
    onehot = head == lax.broadcasted_iota(jnp.int32, (2 * R, A_G), 1)
    sink = jnp.sum(jnp.where(onehot, sinks, 0.0), axis=1, keepdims=True)
    mx = lax.stop_gradient(jnp.maximum(jnp.max(s, axis=-1, keepdims=True), sink))
    p = jnp.exp(s - mx)
    p = p * (1.0 / (jnp.sum(p, axis=-1, keepdims=True) + jnp.exp(sink - mx)))
    o = _dot(p[:R], VLO, 1, 0) + _dot(p[R:], VHI, 1, 0)
    return tuple(o[j * B:(j + 1) * B] for j in range(len(Ps)))


def _swap_halves_of_lanes(t):
    return pltpu.roll(t, LANES // 2, 1)


def _kv_operands(kvp_ref, kvc_ref, h):
    kk = jnp.concatenate([kvp_ref[:, :LANES], kvc_ref[:, :LANES]], axis=0)
    vv = jnp.concatenate([kvp_ref[:, LANES:], kvc_ref[:, LANES:]], axis=0)
    low = lax.broadcasted_iota(jnp.int32, kk.shape, 1) < A_DH
    own = low if h == 0 else jnp.logical_not(low)
    k_own = jnp.where(own, kk, 0.0)
    v_own = jnp.where(own, vv, 0.0)
    k_oth, v_oth = _swap_halves_of_lanes(k_own), _swap_halves_of_lanes(v_own)
    if h == 0:
        return own, k_own, k_oth, v_own, v_oth
    return own, k_oth, k_own, v_oth, v_own


def _pair_cols(h, j):
    c = (h * A_PAIRS + j) * LANES
    return slice(c, c + LANES)


def _attn_fwd(proj, sinks, *, name):
    S = proj.shape[0]
    NB = S // A_BLK
    kv_blk = A_NQ // A_NKV
    qs = pl.BlockSpec((A_BLK, A_NQ), lambda n: (n, 0))
    cur = pl.BlockSpec((A_BLK, A_NKV), lambda n: (n, kv_blk))
    prev = pl.BlockSpec((A_BLK, A_NKV), lambda n: (jnp.maximum(n - 1, 0), kv_blk))
    sk = pl.BlockSpec((A_KVH, A_G), lambda n: (0, 0))

    def body(q_ref, kvp_ref, kvc_ref, s_ref, o_ref):
        first = pl.program_id(0) == 0
        for h in range(A_KVH):
            _, KLO, KHI, VLO, VHI = _kv_operands(kvp_ref, kvc_ref, h)
            Ps = tuple(q_ref[:, _pair_cols(h, j)] for j in range(A_PAIRS))
            outs = _attn_group(Ps, KLO, KHI, VLO, VHI, s_ref[h:h + 1, :], first)
            for j in range(A_PAIRS):
                o_ref[:, _pair_cols(h, j)] = outs[j].astype(o_ref.dtype)

    return pl.pallas_call(
        body, name=name, grid=(NB,),
        in_specs=[qs, prev, cur, sk], out_specs=qs,
        out_shape=jax.ShapeDtypeStruct((S, A_NQ), MXU_DTYPE),
        compiler_params=_cparams(("parallel",)),
    )(proj, proj, proj, sinks)


def _attn_bwd(proj, sinks, do, *, name):
    S = proj.shape[0]
    NB = S // A_BLK
    last = NB - 1
    kv_blk = A_NQ // A_NKV
    qs = pl.BlockSpec((A_BLK, A_NQ), lambda n: (jnp.minimum(n, last), 0))
    cur = pl.BlockSpec((A_BLK, A_NKV), lambda n: (jnp.minimum(n, last), kv_blk))
    prev = pl.BlockSpec((A_BLK, A_NKV), lambda n: (jnp.clip(n - 1, 0, last), kv_blk))
    sk = pl.BlockSpec((A_KVH, A_G), lambda n: (0, 0))
    lag = pl.BlockSpec((A_BLK, A_NKV), lambda n: (jnp.maximum(n - 1, 0), 0))
    cq_spec = pl.BlockSpec((1, A_NQ), lambda n: (0, 0))
    ckv_spec = pl.BlockSpec((1, A_NKV), lambda n: (0, 0))

    def body(q_ref, kvp_ref, kvc_ref, s_ref, do_ref, dq_ref, dkv_ref, ds_ref, cq_ref, ckv_ref, keep):
        n = pl.program_id(0)

        @pl.when(n == 0)
        def _():
            keep[...] = jnp.zeros_like(keep)
            ds_ref[...] = jnp.zeros_like(ds_ref)
            cq_ref[...] = jnp.zeros_like(cq_ref)
            ckv_ref[...] = jnp.zeros_like(ckv_ref)

        @pl.when(n < NB)
        def _():
            f = functools.partial(_attn_group, first=n == 0)
            dk = jnp.zeros((2 * A_BLK, LANES), F32)
            dv = jnp.zeros((2 * A_BLK, LANES), F32)
            ds_rows = []
            for h in range(A_KVH):
                own, KLO, KHI, VLO, VHI = _kv_operands(kvp_ref, kvc_ref, h)
                Ps = tuple(q_ref[:, _pair_cols(h, j)] for j in range(A_PAIRS))
                _, vjp = jax.vjp(f, Ps, KLO, KHI, VLO, VHI, s_ref[h:h + 1, :])
                dPs, dKLO, dKHI, dVLO, dVHI, dsk = vjp(
                    tuple(do_ref[:, _pair_cols(h, j)].astype(F32) for j in range(A_PAIRS)))
                for j in range(A_PAIRS):
                    dq_ref[:, _pair_cols(h, j)] = dPs[j].astype(dq_ref.dtype)
                    cq_ref[:, _pair_cols(h, j)] += jnp.sum(dPs[j], axis=0, keepdims=True)
                dk_own, dk_oth = (dKLO, dKHI) if h == 0 else (dKHI, dKLO)
                dv_own, dv_oth = (dVLO, dVHI) if h == 0 else (dVHI, dVLO)
                dk = dk + jnp.where(own, dk_own, 0.0) + _swap_halves_of_lanes(jnp.where(own, 0.0, dk_oth))
                dv = dv + jnp.where(own, dv_own, 0.0) + _swap_halves_of_lanes(jnp.where(own, 0.0, dv_oth))
                ds_rows.append(dsk)
            ds_ref[...] += jnp.concatenate(ds_rows, axis=0)
            dkv = jnp.concatenate([dk, dv], axis=1)
            done = keep[...] + dkv[:A_BLK]
            dkv_ref[...] = done.astype(dkv_ref.dtype)
            ckv_ref[...] += jnp.sum(done, axis=0, keepdims=True)
            keep[...] = dkv[A_BLK:]

        @pl.when(n == NB)
        def _():
            done = keep[...]
            dkv_ref[...] = done.astype(dkv_ref.dtype)
            ckv_ref[...] += jnp.sum(done, axis=0, keepdims=True)

    return pl.pallas_call(
        body, name=name, grid=(NB + 1,),
        in_specs=[qs, prev, cur, sk, qs],
        out_specs=[qs, lag, sk, cq_spec, ckv_spec],
        out_shape=[jax.ShapeDtypeStruct((S, A_NQ), MXU_DTYPE), jax.ShapeDtypeStruct((S, A_NKV), MXU_DTYPE),
                   jax.ShapeDtypeStruct((A_KVH, A_G), F32), jax.ShapeDtypeStruct((1, A_NQ), F32),
                   jax.ShapeDtypeStruct((1, A_NKV), F32)],
        scratch_shapes=[pltpu.VMEM((A_BLK, A_NKV), F32)],
        compiler_params=_cparams(("arbitrary",)),
    )(proj, proj, proj, sinks, do)


HALO = 8


def _shift_rows(t, j):
    if j == 0:
        return t
    return pltpu.roll(t, j % t.shape[0], 0)


def _conv_gate(gate_ext, cw, cb):
    return cb + cw[0:1, :] * _shift_rows(gate_ext, 2) + cw[1:2, :] * _shift_rows(gate_ext, 1) + cw[2:3, :] * gate_ext


def _convgate_fwd(u, cw, cb, *, name):
    S, F2 = u.shape
    F = F2 // 2
    tm = _pick(S, (256, 128))
    hb = tm // HALO
    urow = pl.BlockSpec((tm, F2), lambda i: (i, 0))
    uprev = pl.BlockSpec((HALO, F), lambda i: (jnp.maximum(i * hb - 1, 0), 0))

    def body(u_ref, up_ref, cw_ref, cb_ref, a_ref):
        i = pl.program_id(0)
        gate = u_ref[:, :F]
        val = u_ref[:, F:]
        prev = jnp.where(i > 0, up_ref[...], 0.0)
        gc = _conv_gate(jnp.concatenate([prev, gate], axis=0), cw_ref[...], cb_ref[...])[HALO:]
        a_ref[...] = (gc * _sigmoid(gc) * val).astype(a_ref.dtype)

    return pl.pallas_call(
        body, name=name, grid=(S // tm,),
        in_specs=[urow, uprev, pl.BlockSpec((3, F), lambda i: (0, 0)), pl.BlockSpec((1, F), lambda i: (0, 0))],
        out_specs=pl.BlockSpec((tm, F), lambda i: (i, 0)),
        out_shape=jax.ShapeDtypeStruct((S, F), MXU_DTYPE),
        compiler_params=_cparams(("parallel",)),
    )(u, u, cw, cb)


def _convgate_bwd(u, da, cw, cb, *, name):
    S, F2 = u.shape
    F = F2 // 2
    tm = _pick(S, (128,))
    hb = tm // HALO
    nt = S // tm
    nh = S // HALO
    urow = pl.BlockSpec((tm, F2), lambda i: (i, 0))
    uprev = pl.BlockSpec((HALO, F), lambda i: (jnp.maximum(i * hb - 1, 0), 0))
    unext = pl.BlockSpec((HALO, F2), lambda i: (jnp.minimum((i + 1) * hb, nh - 1), 0))
    darow = pl.BlockSpec((tm, F), lambda i: (i, 0))
    danext = pl.BlockSpec((HALO, F), lambda i: (jnp.minimum((i + 1) * hb, nh - 1), 0))

    def body(u_ref, up_ref, un_ref, da_ref, dan_ref, cw_ref, cb_ref, du_ref, dcw_ref, dcb_ref):
        i = pl.program_id(0)
        cwv = cw_ref[...]
        prev = jnp.where(i > 0, up_ref[...], 0.0)
        gate_ext = jnp.concatenate([prev, u_ref[:, :F], un_ref[:, :F]], axis=0)
        val_ext = jnp.concatenate([u_ref[:, F:], un_ref[:, F:]], axis=0)
        da_next = jnp.where(i < nt - 1, dan_ref[...].astype(F32), 0.0)
        da_ext = jnp.concatenate([da_ref[...].astype(F32), da_next], axis=0)
        gc = _conv_gate(gate_ext, cwv, cb_ref[...])[HALO:]
        sg = _sigmoid(gc)
        silu = gc * sg
        dval = da_ext * silu
        dgc = da_ext * val_ext * (sg * (1.0 + gc * (1.0 - sg)))
        dgate = cwv[2:3, :] * dgc + cwv[1:2, :] * _shift_rows(dgc, -1) + cwv[0:1, :] * _shift_rows(dgc, -2)
        du_ref[:, :F] = dgate[:tm].astype(du_ref.dtype)
        du_ref[:, F:] = dval[:tm].astype(du_ref.dtype)
        dgc_c = dgc[:tm]
        g0 = gate_ext[HALO:HALO + tm]
        g1 = _shift_rows(gate_ext, 1)[HALO:HALO + tm]
        g2 = _shift_rows(gate_ext, 2)[HALO:HALO + tm]
        dcw = jnp.concatenate([jnp.sum(dgc_c * g2, axis=0, keepdims=True),
                               jnp.sum(dgc_c * g1, axis=0, keepdims=True),
                               jnp.sum(dgc_c * g0, axis=0, keepdims=True)], axis=0)
        dcb = jnp.sum(dgc_c, axis=0, keepdims=True)

        @pl.when(i == 0)
        def _():
            dcw_ref[...] = dcw
            dcb_ref[...] = dcb

        @pl.when(i > 0)
        def _():
            dcw_ref[...] += dcw
            dcb_ref[...] += dcb

    return pl.pallas_call(
        body, name=name, grid=(nt,),
        in_specs=[urow, uprev, unext, darow, danext,
                  pl.BlockSpec((3, F), lambda i: (0, 0)), pl.BlockSpec((1, F), lambda i: (0, 0))],
        out_specs=[urow, pl.BlockSpec((3, F), lambda i: (0, 0)), pl.BlockSpec((1, F), lambda i: (0, 0))],
        out_shape=[jax.ShapeDtypeStruct((S, F2), MXU_DTYPE), jax.ShapeDtypeStruct((3, F), F32),
                   jax.ShapeDtypeStruct((1, F), F32)],
        compiler_params=_cparams(("arbitrary",)),
    )(u, u, u, da, da, cw, cb)


def _adamw_math(w, g, m, v):
    m = ADAM_B1 * m + (1.0 - ADAM_B1) * g
    v = ADAM_B2 * v + (1.0 - ADAM_B2) * (g * g)
    m_hat = m / (1.0 - ADAM_B1 ** ADAM_STEP)
    v_hat = v / (1.0 - ADAM_B2 ** ADAM_STEP)
    delta = -ADAM_LR * (m_hat / (jnp.sqrt(v_hat) + ADAM_EPS) + ADAM_WD * w)
    return delta, m, v


def _adamw_layers(w, gs, m, v, *, name):
    Lr, R, C = w.shape
    tr = _pick(R, (256, 128, 64, 32, 16, 8))
    outs = None
    for layer, g in enumerate(gs):
        def body(w_ref, g_ref, m_ref, v_ref, *rest):
            go_ref, d_ref, nm_ref, nv_ref = rest[-4:]
            gv = g_ref[...]
            d, nm, nv = _adamw_math(w_ref[...], gv, m_ref[...], v_ref[...])
            go_ref[...] = gv
            d_ref[...] = d
            nm_ref[...] = nm
            nv_ref[...] = nv

        lay = pl.BlockSpec((None, tr, C), lambda i, layer=layer: (layer, i, 0))
        in_specs = [lay, pl.BlockSpec((tr, C), lambda i: (i, 0)), lay, lay]
        args = [w, g, m, v]
        aliases = {}
        if outs is not None:
            in_specs += [pl.BlockSpec(memory_space=pl.ANY)] * 4
            args += list(outs)
            aliases = {4 + t: t for t in range(4)}
        outs = pl.pallas_call(
            body, name=f"{name}_{layer}", grid=(R // tr,), in_specs=in_specs, out_specs=[lay] * 4,
            out_shape=[jax.ShapeDtypeStruct((Lr, R, C), F32)] * 4, input_output_aliases=aliases,
            compiler_params=_cparams(("parallel",)),
        )(*args)
    return outs


def _adamw_small(items, *, name):
    n = len(items)

    def body(*refs):
        ins, outs = refs[:4 * n], refs[4 * n:]
        for t in range(n):
            w, g, m, v = (r[...] for r in ins[4 * t:4 * t + 4])
            d, nm, nv = _adamw_math(w, g, m, v)
            outs[3 * t][...] = d
            outs[3 * t + 1][...] = nm
            outs[3 * t + 2][...] = nv

    flat = [a for it in items for a in it]
    out_shape = [jax.ShapeDtypeStruct(it[0].shape, F32) for it in items for _ in range(3)]
    vm = pl.BlockSpec(memory_space=pltpu.VMEM)
    res = pl.pallas_call(body, name=name, in_specs=[vm] * len(flat), out_specs=[vm] * len(out_shape),
                         out_shape=out_shape)(*flat)
    return [tuple(res[3 * t:3 * t + 3]) for t in range(n)]


def _place():
    return lax.axis_index("x"), lax.axis_index("y"), lax.axis_index("c")


_FLIPS = ((1, 0), (0, 1), (1, 1))


ROW_ALIGN = 16


def _half(rows, which):
    return pl.ds(pl.multiple_of(which * (rows // 2), ROW_ALIGN), rows // 2)


def _remote(src, dst, send_sems, recv_sems, k, to):
    return pltpu.make_async_remote_copy(src_ref=src, dst_ref=dst, send_sem=send_sems.at[k], recv_sem=recv_sems.at[k],
                                        device_id=to, device_id_type=MESH)


def _gather_shards(shards, *, name):
    n = len(shards)
    rows = [s.shape[0] for s in shards]
    any_spec = pl.BlockSpec(memory_space=pl.ANY)

    def body(*refs):
        w_refs, out_refs, send_sems, recv_sems = refs[:n], refs[n:2 * n], refs[2 * n], refs[2 * n + 1]
        x, y, c = _place()
        me = 2 * x + y
        sibling = (x, y, 1 - c)
        chips = [(x ^ fx, y ^ fy) for fx, fy in _FLIPS]

        def over_ici(i, j):
            px, py = chips[j]
            mine = _half(rows[i], c)
            return _remote(w_refs[i].at[mine], out_refs[i].at[me, mine], send_sems, recv_sems, 6 * i + j, (px, py, c))

        def landed(i, j):
            px, py = chips[j]
            blk = out_refs[i].at[2 * px + py, _half(rows[i], c)]
            return _remote(blk, blk, send_sems, recv_sems, 6 * i + j, (px, py, c))

        def over_d2d(i, j, which):
            px, py = chips[j]
            blk = out_refs[i].at[2 * px + py, _half(rows[i], which)]
            return _remote(blk, blk, send_sems, recv_sems, 6 * i + 3 + j, sibling)

        pairs = [(i, j) for i in range(n) for j in range(3)]
        sent = [over_ici(i, j) for i, j in pairs]
        for cp in sent:
            cp.start()
        for i, j in pairs:
            landed(i, j).wait_recv()
            cp = over_d2d(i, j, c)
            cp.start()
            sent.append(cp)
        for i, j in pairs:
            over_d2d(i, j, 1 - c).wait_recv()
        for cp in sent:
            cp.wait_send()

    return pl.pallas_call(
        body, name=name, in_specs=[any_spec] * n, out_specs=[any_spec] * n,
        out_shape=[jax.ShapeDtypeStruct((N_CHIPS,) + s.shape, s.dtype) for s in shards],
        scratch_shapes=[pltpu.SemaphoreType.DMA((6 * n,)), pltpu.SemaphoreType.DMA((6 * n,))],
    )(*shards)


def _swap_halves(gs, *, name):
    n = len(gs)
    any_spec = pl.BlockSpec(memory_space=pl.ANY)

    def body(*refs):
        g_refs, out_refs, send_sems, recv_sems = refs[:n], refs[n:2 * n], refs[2 * n], refs[2 * n + 1]
        x, y, c = _place()
        cps = [_remote(g_refs[i].at[:, _half(gs[i].shape[1], 1 - c)], out_refs[i], send_sems, recv_sems, i, (x, y, 1 - c))
               for i in range(n)]
        for cp in cps:
            cp.start()
        for cp in cps:
            cp.wait()

    return pl.pallas_call(
        body, name=name, in_specs=[any_spec] * n, out_specs=[any_spec] * n,
        out_shape=[jax.ShapeDtypeStruct((N_CHIPS, g.shape[1] // 2, g.shape[2]), g.dtype) for g in gs],
        scratch_shapes=[pltpu.SemaphoreType.DMA((n,)), pltpu.SemaphoreType.DMA((n,))],
    )(*gs)


def _add_halves(g, got, c_arr, *, name):
    _, rows, cols = g.shape
    blk = (None, rows // 2, cols)

    def body(c_ref, g_ref, got_ref, o_ref):
        o_ref[...] = (g_ref[...] + got_ref[...]).astype(o_ref.dtype)

    return pl.pallas_call(
        body, name=name,
        grid_spec=pltpu.PrefetchScalarGridSpec(
            num_scalar_prefetch=1, grid=(N_CHIPS,),
            in_specs=[pl.BlockSpec(blk, lambda j, c_ref: (j, c_ref[0], 0)), pl.BlockSpec(blk, lambda j, c_ref: (j, 0, 0))],
            out_specs=pl.BlockSpec(blk, lambda j, c_ref: (j, 0, 0))),
        out_shape=jax.ShapeDtypeStruct((N_CHIPS, rows // 2, cols), WIRE_DTYPE),
        compiler_params=_cparams(("parallel",)),
    )(c_arr, g, got)


def _exchange_chips(qs, *, name):
    n = len(qs)
    any_spec = pl.BlockSpec(memory_space=pl.ANY)

    def body(*refs):
        q_refs, out_refs, send_sems, recv_sems = refs[:n], refs[n:2 * n], refs[2 * n], refs[2 * n + 1]
        x, y, c = _place()
        me = 2 * x + y
        chips = [(x ^ fx, y ^ fy) for fx, fy in _FLIPS]
        pairs = [(i, j) for i in range(n) for j in range(3)]
        sent = []
        for i, j in pairs:
            px, py = chips[j]
            cp = _remote(q_refs[i].at[2 * px + py], out_refs[i].at[me], send_sems, recv_sems, 3 * i + j, (px, py, c))
            cp.start()
            sent.append(cp)
        for i, j in pairs:
            px, py = chips[j]
            blk = out_refs[i].at[2 * px + py]
            _remote(blk, blk, send_sems, recv_sems, 3 * i + j, (px, py, c)).wait_recv()
        for cp in sent:
            cp.wait_send()

    return pl.pallas_call(
        body, name=name, in_specs=[any_spec] * n, out_specs=[any_spec] * n,
        out_shape=[jax.ShapeDtypeStruct(q.shape, q.dtype) for q in qs],
        scratch_shapes=[pltpu.SemaphoreType.DMA((3 * n,)), pltpu.SemaphoreType.DMA((3 * n,))],
    )(*qs)


def _sum_chips(q, r, place_arr, *, name):
    _, h, cols = q.shape
    blk = (None, h, cols)

    def body(p_ref, q_ref, r1_ref, r2_ref, r3_ref, o_ref):
        o_ref[...] = ((q_ref[...].astype(F32) + r1_ref[...].astype(F32)) + r2_ref[...].astype(F32)) + r3_ref[...].astype(F32)

    other = [pl.BlockSpec(blk, lambda i, p_ref, f=f: (p_ref[0] ^ f, 0, 0)) for f in (1, 2, 3)]
    return pl.pallas_call(
        body, name=name,
        grid_spec=pltpu.PrefetchScalarGridSpec(
            num_scalar_prefetch=1, grid=(1,),
            in_specs=[pl.BlockSpec(blk, lambda i, p_ref: (p_ref[0], 0, 0))] + other,
            out_specs=pl.BlockSpec((h, cols), lambda i, p_ref: (p_ref[1], 0))),
        out_shape=jax.ShapeDtypeStruct((2 * h, cols), F32),
        compiler_params=_cparams(("arbitrary",)),
    )(place_arr, q, r, r, r)


def _join_halves(fs, *, name):
    n = len(fs)
    any_spec = pl.BlockSpec(memory_space=pl.ANY)

    def body(*refs):
        out_refs, send_sems, recv_sems = refs[n:2 * n], refs[2 * n], refs[2 * n + 1]
        x, y, c = _place()
        sent = []
        for i in range(n):
            mine = out_refs[i].at[_half(fs[i].shape[0], c)]
            cp = _remote(mine, mine, send_sems, recv_sems, i, (x, y, 1 - c))
            cp.start()
            sent.append(cp)
        for i in range(n):
            its = out_refs[i].at[_half(fs[i].shape[0], 1 - c)]
            _remote(its, its, send_sems, recv_sems, i, (x, y, 1 - c)).wait_recv()
        for cp in sent:
            cp.wait_send()

    return pl.pallas_call(
        body, name=name, in_specs=[any_spec] * n, out_specs=[any_spec] * n,
        out_shape=[jax.ShapeDtypeStruct(f.shape, f.dtype) for f in fs], input_output_aliases={i: i for i in range(n)},
        scratch_shapes=[pltpu.SemaphoreType.DMA((n,)), pltpu.SemaphoreType.DMA((n,))],
    )(*fs)


def _allreduce_small(buf, *, name):
    R = buf.shape[0]
    vm = pl.BlockSpec(memory_space=pltpu.VMEM)

    def body(b_ref, o_ref, slots, send_sems, recv_sems):
        x, y, c = _place()
        me = 4 * x + 2 * y + c
        slots[me] = b_ref[...]
        sends = []
        for kk in range(1, 8):
            fx, fy, fc = (kk >> 2) & 1, (kk >> 1) & 1, kk & 1
            cp = pltpu.make_async_remote_copy(
                src_ref=b_ref, dst_ref=slots.at[me], send_sem=send_sems.at[kk - 1], recv_sem=recv_sems.at[kk - 1],
                device_id=(x ^ fx, y ^ fy, c ^ fc), device_id_type=MESH)
            cp.start()
            sends.append(cp)
        for kk in range(1, 8):
            fx, fy, fc = (kk >> 2) & 1, (kk >> 1) & 1, kk & 1
            peer = 4 * (x ^ fx) + 2 * (y ^ fy) + (c ^ fc)
            pltpu.make_async_remote_copy(
                src_ref=b_ref, dst_ref=slots.at[peer], send_sem=send_sems.at[kk - 1], recv_sem=recv_sems.at[kk - 1],
                device_id=(x ^ fx, y ^ fy, c ^ fc), device_id_type=MESH).wait_recv()
        for cp in sends:
            cp.wait_send()
        acc = slots[0]
        for d in range(1, 8):
            acc = acc + slots[d]
        o_ref[...] = acc

    return pl.pallas_call(
        body, name=name, in_specs=[vm], out_specs=vm,
        out_shape=jax.ShapeDtypeStruct((R, LANES), F32),
        scratch_shapes=[pltpu.VMEM((8, R, LANES), F32), pltpu.SemaphoreType.DMA((7,)), pltpu.SemaphoreType.DMA((7,))],
    )(buf)


def _pad_rows(v, mult=8 * LANES):
    flat = v.reshape(-1)
    n = flat.shape[0]
    tot = -(-n // mult) * mult
    return jnp.pad(flat, (0, tot - n)).reshape(-1, LANES)


def _pack_small(parts):
    return jnp.concatenate([_pad_rows(p.astype(F32)) for p in parts], axis=0)


def _unpack_small(buf, shapes):
    out, r = [], 0
    for sh in shapes:
        n = math.prod(sh)
        rows = -(-n // (8 * LANES)) * 8
        out.append(buf[r:r + rows].reshape(-1)[:n].reshape(sh))
        r += rows
    return out


def _cols_to_shards(w):
    *lead, K, N = w.shape
    t = w.reshape(*lead, K, N_CHIPS, N // N_CHIPS)
    return jnp.moveaxis(t, -2, 0)


def _shards_to_cols(t):
    t = jnp.moveaxis(t, 0, -2)
    *lead, K, _, n = t.shape
    return t.reshape(*lead, K, N_CHIPS * n)


_BIG = ("m_w_in", "m_w_out", "a_w_in", "a_w_out", "f_w_up", "f_w_down")


def kernel(x, m_w_in, m_gate_bias, m_head_norm, m_w_out, a_w_in, a_b_in, a_sinks, a_w_out, a_b_out, norm_mix_pre, norm_mix_post, norm_ffn_pre, norm_ffn_post, f_w_up, f_conv_w, f_conv_b, f_w_down, loss_target, m_m_w_in, m_m_gate_bias, m_m_head_norm, m_m_w_out, m_a_w_in, m_a_b_in, m_a_sinks, m_a_w_out, m_a_b_out, m_norm_mix_pre, m_norm_mix_post, m_norm_ffn_pre, m_norm_ffn_post, m_f_w_up, m_f_conv_w, m_f_conv_b, m_f_w_down, v_m_w_in, v_m_gate_bias, v_m_head_norm, v_m_w_out, v_a_w_in, v_a_b_in, v_a_sinks, v_a_w_out, v_a_b_out, v_norm_mix_pre, v_norm_mix_post, v_norm_ffn_pre, v_norm_ffn_post, v_f_w_up, v_f_conv_w, v_f_conv_b, v_f_w_down):
    params = dict(m_w_in=m_w_in, m_gate_bias=m_gate_bias, m_head_norm=m_head_norm, m_w_out=m_w_out, a_w_in=a_w_in,
                  a_b_in=a_b_in, a_sinks=a_sinks, a_w_out=a_w_out, a_b_out=a_b_out, norm_mix_pre=norm_mix_pre,
                  norm_mix_post=norm_mix_post, norm_ffn_pre=norm_ffn_pre, norm_ffn_post=norm_ffn_post, f_w_up=f_w_up,
                  f_conv_w=f_conv_w, f_conv_b=f_conv_b, f_w_down=f_w_down)
    mom1 = dict(m_w_in=m_m_w_in, m_gate_bias=m_m_gate_bias, m_head_norm=m_m_head_norm, m_w_out=m_m_w_out,
                a_w_in=m_a_w_in, a_b_in=m_a_b_in, a_sinks=m_a_sinks, a_w_out=m_a_w_out, a_b_out=m_a_b_out,
                norm_mix_pre=m_norm_mix_pre, norm_mix_post=m_norm_mix_post, norm_ffn_pre=m_norm_ffn_pre,
                norm_ffn_post=m_norm_ffn_post, f_w_up=m_f_w_up, f_conv_w=m_f_conv_w, f_conv_b=m_f_conv_b,
                f_w_down=m_f_w_down)
    mom2 = dict(m_w_in=v_m_w_in, m_gate_bias=v_m_gate_bias, m_head_norm=v_m_head_norm, m_w_out=v_m_w_out,
                a_w_in=v_a_w_in, a_b_in=v_a_b_in, a_sinks=v_a_sinks, a_w_out=v_a_w_out, a_b_out=v_a_b_out,
                norm_mix_pre=v_norm_mix_pre, norm_mix_post=v_norm_mix_post, norm_ffn_pre=v_norm_ffn_pre,
                norm_ffn_post=v_norm_ffn_post, f_w_up=v_f_w_up, f_conv_w=v_f_conv_w, f_conv_b=v_f_conv_b,
                f_w_down=v_f_w_down)
    order = list(params)

    mx, my, mc = _place()
    chip = 2 * mx + my
    h0 = x[0]
    target = loss_target[0]

    def two_d(t):
        return t.reshape(-1, t.shape[-1])

    mine = [two_d(params[n]).astype(MXU_DTYPE) for n in _BIG]
    theirs = _gather_shards(mine, name="gather_weights")
    gathered = {n: lax.dynamic_update_slice(t, s[None], (chip, 0, 0)) for n, s, t in zip(_BIG, mine, theirs)}

    def in_place(shard, axis):
        width = shard.shape[axis]
        z = jnp.zeros(shard.shape[:axis] + (N_CHIPS * width,) + shard.shape[axis + 1:], F32)
        contrib = jnp.where(mc == 0, shard, 0.0)
        return lax.dynamic_update_slice_in_dim(z, contrib, chip * width, axis)

    sm_in = [in_place(a_b_in, 1), in_place(a_b_out, 1), in_place(f_conv_w, 2)]
    sm_full = _unpack_small(_allreduce_small(_pack_small(sm_in), name="gather_small"), [t.shape for t in sm_in])
    b_in_full, b_out_full, conv_w_full = sm_full

    W_in = _shards_to_cols(gathered["m_w_in"])
    nqk = M_HEADS * M_QK
    n_main = 2 * nqk + M_HEADS * M_V + D_MODEL

    def pad_heads(w):
        return jnp.pad(w.reshape(D_MODEL, M_HEADS, M_QK), ((0, 0), (0, 0), (0, LANES - M_QK))).reshape(D_MODEL, -1)

    def unpad_heads(w):
        return w.reshape(D_MODEL, M_HEADS, LANES)[:, :, :M_QK].reshape(D_MODEL, nqk)

    W_all = jnp.concatenate([pad_heads(W_in[:, :nqk]), pad_heads(W_in[:, nqk:2 * nqk]), W_in[:, 2 * nqk:n_main],
                             jnp.pad(W_in[:, n_main:], ((0, 0), (0, LANES - 2 * M_HEADS)))], axis=1)
    gbias = jnp.pad(m_gate_bias[0].reshape(1, 2 * M_HEADS), ((0, 0), (0, LANES - 2 * M_HEADS)))
    W_mout = gathered["m_w_out"].reshape(D_MODEL, D_MODEL)
    A_in, A_out = _shards_to_cols(gathered["a_w_in"]), gathered["a_w_out"].reshape(D_MODEL, D_MODEL)
    W_up = gathered["f_w_up"]
    dsh = D_FF // N_CHIPS
    W_down = [gathered["f_w_down"][:, i * dsh:(i + 1) * dsh].reshape(D_FF, D_MODEL) for i in range(2)]

    grads = {}

    def ffn_fwd(i, h1):
        z1 = _rms_fwd(h1, norm_ffn_pre[i:i + 1], out_dtype=MXU_DTYPE, name=f"ffn_pre_norm{i}")
        u = _mm(z1, W_up, b_layer=i, name=f"ffn_up{i}")
        a = _convgate_fwd(u, conv_w_full[i], f_conv_b[i:i + 1], name=f"ffn_act{i}")
        zf = _mm(a, W_down[i], name=f"ffn_down{i}")
        h2 = _rms_fwd(zf, norm_ffn_post[i:i + 1], res=h1, out_dtype=F32, name=f"ffn_post_norm{i}")
        return h2, (z1, u, a, zf)

    z0 = _rms_fwd(h0, norm_mix_pre[0:1], out_dtype=MXU_DTYPE, name="mix_pre_norm0")
    proj = _mm(z0, W_all, name="mlstm_proj")
    gated, C_all, n_all, m_all = _mlstm_fwd(proj, gbias, m_head_norm, name="mlstm_fwd")
    zm0 = _mm(gated, W_mout, name="mlstm_out")
    h1 = _rms_fwd(zm0, norm_mix_post[0:1], res=h0, out_dtype=F32, name="mix_post_norm0")
    h2, ffn0 = ffn_fwd(0, h1)

    y0 = _rms_fwd(h2, norm_mix_pre[1:2], out_dtype=MXU_DTYPE, name="mix_pre_norm1")
    aproj = _mm(y0, A_in, bias=b_in_full, name="attn_proj")
    sinks = a_sinks.reshape(A_KVH, A_G)
    ao = _attn_fwd(aproj, sinks, name="attn_fwd")
    zm1 = _mm(ao, A_out, bias=b_out_full, name="attn_out")
    h3 = _rms_fwd(zm1, norm_mix_post[1:2], res=h2, out_dtype=F32, name="mix_post_norm1")
    h4, ffn1 = ffn_fwd(1, h3)

    loss_part, dh = _loss_head(h4, target, name="loss_head")
    loss = lax.psum(loss_part[0, 0], ("x", "y", "c"))

    g_post, g_fpre, g_mpost, g_mpre = [None, None], [None, None], [None, None], [None, None]
    dW_up, dW_down, dconv_w, dconv_b = [None, None], [None, None], [None, None], [None, None]

    def ffn_bwd(i, dh2, h1_, saved):
        z1, u, a, zf = saved
        dzf, g_post[i], _ = _rms_bwd(zf, norm_ffn_post[i:i + 1], dh2, out_dtype=MXU_DTYPE, name=f"ffn_post_norm_bwd{i}")
        da = _mm(dzf, W_down[i], tb=True, name=f"ffn_down_dx{i}")
        dW_down[i] = _mm(a, dzf, ta=True, name=f"ffn_down_dw{i}")
        du, dconv_w[i], dconv_b[i] = _convgate_bwd(u, da, conv_w_full[i], f_conv_b[i:i + 1], name=f"ffn_act_bwd{i}")
        dz1 = _mm(du, W_up, tb=True, b_layer=i, name=f"ffn_up_dx{i}")
        dW_up[i] = _mm(z1, du, ta=True, out_stacked=True, name=f"ffn_up_dw{i}")
        dh1, g_fpre[i], _ = _rms_bwd(h1_, norm_ffn_pre[i:i + 1], dz1, res=dh2, out_dtype=F32, name=f"ffn_pre_norm_bwd{i}")
        return dh1

    dh3 = ffn_bwd(1, dh, h3, ffn1)
    dzm1, g_mpost[1], db_out = _rms_bwd(zm1, norm_mix_post[1:2], dh3, out_dtype=MXU_DTYPE, name="mix_post_norm_bwd1")
    dao = _mm(dzm1, A_out, tb=True, name="attn_out_dx")
    dA_out = _mm(ao, dzm1, ta=True, name="attn_out_dw")
    daq, dakv, dsinks, cs_q, cs_kv = _attn_bwd(aproj, sinks, dao, name="attn_bwd")
    db_in = jnp.concatenate([cs_q, cs_kv], axis=1)
    dy0 = _mm(daq, A_in[:, :A_NQ], tb=True, add=_mm(dakv, A_in[:, A_NQ:], tb=True, name="attn_proj_kv_dx"),
              name="attn_proj_q_dx")
    dA_in = jnp.concatenate([_mm(y0, daq, ta=True, name="attn_proj_q_dw"),
                             _mm(y0, dakv, ta=True, name="attn_proj_kv_dw")], axis=1)
    dh2, g_mpre[1], _ = _rms_bwd(h2, norm_mix_pre[1:2], dy0, res=dh3, out_dtype=F32, name="mix_pre_norm_bwd1")

    dh1 = ffn_bwd(0, dh2, h1, ffn0)
    dzm0, g_mpost[0], _ = _rms_bwd(zm0, norm_mix_post[0:1], dh1, out_dtype=MXU_DTYPE, name="mix_post_norm_bwd0")
    dgated = _mm(dzm0, W_mout, tb=True, name="mlstm_out_dx")
    dW_mout = _mm(gated, dzm0, ta=True, name="mlstm_out_dw")
    dproj, dgbias, dhn = _mlstm_bwd(proj, gbias, m_head_norm, C_all, n_all, m_all, dgated, name="mlstm_bwd")
    dz0 = _mm(dproj, W_all, tb=True, name="mlstm_proj_dx")
    dW_all = _mm(z0, dproj, ta=True, name="mlstm_proj_dw")
    dW_min = jnp.concatenate([unpad_heads(dW_all[:, M_OFF_Q:M_OFF_K]), unpad_heads(dW_all[:, M_OFF_K:M_OFF_V]),
                              dW_all[:, M_OFF_V:M_OFF_G], dW_all[:, M_OFF_G:M_OFF_G + 2 * M_HEADS]], axis=1)
    grad_x, g_mpre[0], _ = _rms_bwd(h0, norm_mix_pre[0:1], dz0, res=dh1, out_dtype=F32, name="mix_pre_norm_bwd0")

    tags = ["m_w_in", "m_w_out", "a_w_in", "a_w_out", "f_w_up0", "f_w_up1", "f_w_down0", "f_w_down1"]
    by_chip = [_cols_to_shards(dW_min), dW_mout.reshape(N_CHIPS, -1, D_MODEL), _cols_to_shards(dA_in),
               dA_out.reshape(N_CHIPS, -1, D_MODEL), dW_up[0], dW_up[1],
               dW_down[0].reshape(N_CHIPS, -1, D_MODEL), dW_down[1].reshape(N_CHIPS, -1, D_MODEL)]
    got = _swap_halves(by_chip, name="grad_swap_halves")
    c_arr = jnp.reshape(mc, (1,)).astype(jnp.int32)
    place_arr = jnp.stack([chip, mc]).astype(jnp.int32)
    chip_part = [_add_halves(g, t, c_arr, name=f"grad_add_halves_{tag}") for tag, g, t in zip(tags, by_chip, got)]
    from_chips = _exchange_chips(chip_part, name="grad_exchange_chips")
    halves = [_sum_chips(q, r, place_arr, name=f"grad_sum_chips_{tag}") for tag, q, r in zip(tags, chip_part, from_chips)]
    reduced = dict(zip(tags, _join_halves(halves, name="grad_join_halves")))
    layer_grads = dict(m_w_in=[reduced["m_w_in"]], m_w_out=[reduced["m_w_out"]], a_w_in=[reduced["a_w_in"]],
                       a_w_out=[reduced["a_w_out"]], f_w_up=[reduced["f_w_up0"], reduced["f_w_up1"]],
                       f_w_down=[reduced["f_w_down0"], reduced["f_w_down1"]])

    small_g = [
        dgbias[:, :2 * M_HEADS].reshape(1, 2, M_HEADS),
        dhn,
        dsinks.reshape(1, A_QH),
        db_in, db_out,
        jnp.concatenate(g_mpre), jnp.concatenate(g_mpost), jnp.concatenate(g_fpre), jnp.concatenate(g_post),
        jnp.stack(dconv_w), jnp.concatenate(dconv_b),
    ]
    small_names = ["m_gate_bias", "m_head_norm", "a_sinks", "a_b_in", "a_b_out", "norm_mix_pre", "norm_mix_post",
                   "norm_ffn_pre", "norm_ffn_post", "f_conv_w", "f_conv_b"]
    red = _unpack_small(_allreduce_small(_pack_small(small_g), name="reduce_small"), [t.shape for t in small_g])
    for n, t in zip(small_names, red):
        if n in ("a_b_in", "a_b_out", "f_conv_w"):
            axis = t.ndim - 1
            width = params[n].shape[axis]
            t = lax.dynamic_slice_in_dim(t, chip * width, width, axis)
        grads[n] = t

    deltas, new_m, new_v = {}, {}, {}
    for n in _BIG:
        grads[n], deltas[n], new_m[n], new_v[n] = _adamw_layers(params[n], layer_grads[n], mom1[n], mom2[n],
                                                                name=f"adamw_{n}")
    two = lambda t: t.reshape(-1, t.shape[-1])
    res = _adamw_small([(two(params[n]), two(grads[n]), two(mom1[n]), two(mom2[n])) for n in small_names],
                       name="adamw_small")
    for n, (d, nm, nv) in zip(small_names, res):
        sh = params[n].shape
        deltas[n], new_m[n], new_v[n] = d.reshape(sh), nm.reshape(sh), nv.reshape(sh)

    return (loss, grad_x[None], *[grads[n] for n in order], *[deltas[n] for n in order],
            *[new_m[n] for n in order], *[new_v[n] for n in order])
```

```python
import functools
import math

import jax
import jax.numpy as jnp
from jax import lax
from jax.experimental import pallas as pl
from jax.experimental.pallas import tpu as pltpu

F32 = jnp.float32
MXU_DTYPE = jnp.bfloat16
WIRE_DTYPE = jnp.bfloat16
MESH = pl.DeviceIdType.MESH

D_MODEL = 1024
EPS = 1e-6
M_HEADS, M_QK, M_V, M_CHUNK = 8, 64, 128, 128
GATE_CAP = 15.0
A_DH, A_QH, A_KVH, A_G, A_BLK = 64, 16, 2, 8, 128
D_FF = 2816
N_CHIPS = 4
LANES = 128
VMEM_LIMIT = 48 * 1024 * 1024

ADAM_LR, ADAM_B1, ADAM_B2, ADAM_EPS, ADAM_WD, ADAM_STEP = 0.001, 0.9, 0.999, 1e-08, 0.01, 10


def _cparams(sem):
    return pltpu.CompilerParams(dimension_semantics=sem, vmem_limit_bytes=VMEM_LIMIT)


def _pick(n, cands):
    for c in cands:
        if n % c == 0:
            return c
    return n


def _mm(a, b, *, ta=False, tb=False, out_dtype=F32, bias=None, add=None, b_layer=None, out_stacked=False, name):
    if ta:
        K, M = a.shape
    else:
        M, K = a.shape
    b_stacked = b_layer is not None
    if b_stacked:
        assert not ta
        n_sh = b.shape[2]
        w_rows, w_cols = D_MODEL, N_CHIPS * n_sh
        N, Kb = (w_rows, w_cols) if tb else (w_cols, w_rows)
    elif tb:
        N, Kb = b.shape
    else:
        Kb, N = b.shape
    assert K == Kb, (a.shape, b.shape)
    tm = _pick(M, (1024, 1408, 512, 256, 128))
    tn = _pick(N, (1024, 1408, 1280, 512, 256, 128))
    tk = K if K <= 2816 else _pick(K, (2816, 2048, 1408, 1024, 512, 256, 128))
    if b_stacked and tb:
        tk = n_sh
    if (b_stacked and not tb) or out_stacked:
        tn = N // N_CHIPS
    nk = K // tk
    dn = (((0 if ta else 1,), (1 if tb else 0,)), ((), ()))
    has_bias, has_add = bias is not None, add is not None

    def body(*refs):
        a_ref, b_ref = refs[0], refs[1]
        pos = 2
        bias_ref = add_ref = None
        if has_bias:
            bias_ref = refs[pos]
            pos += 1
        if has_add:
            add_ref = refs[pos]
            pos += 1
        o_ref = refs[pos]
        acc_ref = refs[pos + 1] if nk > 1 else None

        def finish(r):
            if has_bias:
                r = r + bias_ref[...]
            if has_add:
                r = r + add_ref[...]
            o_ref[...] = r.astype(out_dtype)

        part = lax.dot_general(a_ref[...], b_ref[...], dn, preferred_element_type=F32)
        if nk == 1:
            finish(part)
        else:
            k = pl.program_id(2)

            @pl.when(k == 0)
            def _():
                acc_ref[...] = part

            @pl.when(k > 0)
            def _():
                acc_ref[...] += part

            @pl.when(k == nk - 1)
            def _():
                finish(acc_ref[...])

    a_spec = pl.BlockSpec((tk, tm), lambda i, j, k: (k, i)) if ta else pl.BlockSpec((tm, tk), lambda i, j, k: (i, k))
    if b_stacked and tb:
        off = b_layer * (w_rows // tn)
        b_spec = pl.BlockSpec((None, tn, tk), lambda i, j, k: (k, off + j, 0))
    elif b_stacked:
        off = b_layer * (w_rows // tk)
        b_spec = pl.BlockSpec((None, tk, tn), lambda i, j, k: (j, off + k, 0))
    elif tb:
        b_spec = pl.BlockSpec((tn, tk), lambda i, j, k: (j, k))
    else:
        b_spec = pl.BlockSpec((tk, tn), lambda i, j, k: (k, j))
    if out_stacked:
        out_spec = pl.BlockSpec((None, tm, tn), lambda i, j, k: (j, i, 0))
        out_shape = jax.ShapeDtypeStruct((N_CHIPS, M, tn), out_dtype)
    else:
        out_spec = pl.BlockSpec((tm, tn), lambda i, j, k: (i, j))
        out_shape = jax.ShapeDtypeStruct((M, N), out_dtype)
    in_specs, args = [a_spec, b_spec], [a, b]
    if has_bias:
        in_specs.append(pl.BlockSpec((1, tn), lambda i, j, k: (0, j)))
        args.append(bias)
    if has_add:
        in_specs.append(pl.BlockSpec((tm, tn), lambda i, j, k: (i, j)))
        args.append(add)
    return pl.pallas_call(
        body, name=name,
        grid=(M // tm, N // tn, nk),
        in_specs=in_specs,
        out_specs=out_spec,
        out_shape=out_shape,
        scratch_shapes=[pltpu.VMEM((tm, tn), F32)] if nk > 1 else [],
        compiler_params=_cparams(("parallel", "parallel", "arbitrary")),
    )(*args)


ROW_TILE = 512


def _rms_fwd(x, g, *, res=None, out_dtype, name):
    S, D = x.shape
    tm = _pick(S, (ROW_TILE, 256, 128))
    has_res = res is not None

    def body(*refs):
        x_ref, g_ref = refs[0], refs[1]
        res_ref = refs[2] if has_res else None
        o_ref = refs[-1]
        xv = x_ref[...]
        y = xv * lax.rsqrt(jnp.mean(xv * xv, axis=-1, keepdims=True) + EPS) * g_ref[...]
        if has_res:
            y = res_ref[...] + y
        o_ref[...] = y.astype(out_dtype)

    row = pl.BlockSpec((tm, D), lambda i: (i, 0))
    in_specs, args = [row, pl.BlockSpec((1, D), lambda i: (0, 0))], [x, g]
    if has_res:
        in_specs.append(row)
        args.append(res)
    return pl.pallas_call(
        body, name=name, grid=(S // tm,), in_specs=in_specs, out_specs=row,
        out_shape=jax.ShapeDtypeStruct((S, D), out_dtype),
        compiler_params=_cparams(("parallel",)),
    )(*args)


def _rms_bwd(x, g, dy, *, res=None, out_dtype, name):
    S, D = x.shape
    tm = _pick(S, (ROW_TILE, 256, 128))
    has_res = res is not None

    def body(*refs):
        x_ref, g_ref, dy_ref = refs[0], refs[1], refs[2]
        res_ref = refs[3] if has_res else None
        dx_ref, dg_ref, cs_ref = refs[-3], refs[-2], refs[-1]
        i = pl.program_id(0)
        xv = x_ref[...]
        dyv = dy_ref[...].astype(F32)
        r = lax.rsqrt(jnp.mean(xv * xv, axis=-1, keepdims=True) + EPS)
        xh = xv * r
        gd = dyv * g_ref[...]
        dx = r * (gd - xh * jnp.mean(gd * xh, axis=-1, keepdims=True))
        cs = jnp.sum(dx, axis=0, keepdims=True)
        if has_res:
            dx = res_ref[...] + dx
        dx_ref[...] = dx.astype(out_dtype)
        dg = jnp.sum(dyv * xh, axis=0, keepdims=True)

        @pl.when(i == 0)
        def _():
            dg_ref[...] = dg
            cs_ref[...] = cs

        @pl.when(i > 0)
        def _():
            dg_ref[...] += dg
            cs_ref[...] += cs

    row = pl.BlockSpec((tm, D), lambda i: (i, 0))
    vec = pl.BlockSpec((1, D), lambda i: (0, 0))
    in_specs, args = [row, vec, row], [x, g, dy]
    if has_res:
        in_specs.append(row)
        args.append(res)
    return pl.pallas_call(
        body, name=name, grid=(S // tm,), in_specs=in_specs, out_specs=[row, vec, vec],
        out_shape=[jax.ShapeDtypeStruct((S, D), out_dtype), jax.ShapeDtypeStruct((1, D), F32),
                   jax.ShapeDtypeStruct((1, D), F32)],
        compiler_params=_cparams(("arbitrary",)),
    )(*args)


def _loss_head(y, target, *, name):
    S, D = y.shape
    tm = _pick(S, (ROW_TILE, 256, 128))

    def body(y_ref, t_ref, loss_ref, dy_ref):
        i = pl.program_id(0)
        err = y_ref[...] - t_ref[...]
        dy_ref[...] = err * (1.0 / D)
        part = 0.5 * jnp.sum(jnp.mean(err * err, axis=-1, keepdims=True), axis=0, keepdims=True)

        @pl.when(i == 0)
        def _():
            loss_ref[...] = part

        @pl.when(i > 0)
        def _():
            loss_ref[...] += part

    row = pl.BlockSpec((tm, D), lambda i: (i, 0))
    return pl.pallas_call(
        body, name=name, grid=(S // tm,), in_specs=[row, row],
        out_specs=[pl.BlockSpec((1, 1), lambda i: (0, 0)), row],
        out_shape=[jax.ShapeDtypeStruct((1, 1), F32), jax.ShapeDtypeStruct((S, D), F32)],
        compiler_params=_cparams(("arbitrary",)),
    )(y, target)


def _mx(t):
    return t.astype(MXU_DTYPE)


def _mxf(t):
    return t.astype(MXU_DTYPE).astype(F32)


def _rawdot(a, b, ca, cb):
    return lax.dot_general(_mx(a), _mx(b), (((ca,), (cb,)), ((), ())), preferred_element_type=F32)


@functools.partial(jax.custom_vjp, nondiff_argnums=(2, 3))
def _dot(a, b, ca, cb):
    return _rawdot(a, b, ca, cb)


def _dot_fwd(a, b, ca, cb):
    return _rawdot(a, b, ca, cb), (a, b)


def _dot_bwd(ca, cb, res, g):
    a, b = res
    bj = 1 if cb == 0 else 0
    ai = 0 if ca == 1 else 1
    da = _rawdot(g, b, 1, bj) if ca == 1 else _rawdot(b, g, bj, 1)
    db = _rawdot(a, g, ai, 0) if cb == 0 else _rawdot(g, a, 0, ai)
    return da.astype(a.dtype), db.astype(b.dtype)


_dot.defvjp(_dot_fwd, _dot_bwd)


def _softcap(z):
    return GATE_CAP * jnp.tanh(z / GATE_CAP)


def _log_sigmoid(z):
    return jnp.minimum(z, 0.0) - jnp.log(1.0 + jnp.exp(-jnp.abs(z)))


def _sigmoid(z):
    return 0.5 * jnp.tanh(0.5 * z) + 0.5


def _lane_col(t, lane_index):
    lane = lax.broadcasted_iota(jnp.int32, t.shape, 1)
    return jnp.sum(jnp.where(lane == lane_index, t, 0.0), axis=1, keepdims=True)


def _mlstm_gates(G, bias):
    L = G.shape[0]
    z = _softcap(G + bias)
    ig = z
    lf = _log_sigmoid(z)
    ti = lax.broadcasted_iota(jnp.int32, (L, L), 0)
    si = lax.broadcasted_iota(jnp.int32, (L, L), 1)
    tril = (si <= ti).astype(F32)
    b = lax.dot_general(tril, lf, (((1,), (0,)), ((), ())), precision=lax.Precision.HIGHEST, preferred_element_type=F32)
    bL = jnp.sum(lf, axis=0, keepdims=True)
    return ig, b, ig.T, b.T, bL


def _mlstm_head(h, q, k, v, o, ig_all, b_all, igT, bT, bL_all, hn, C, n, m):
    L = q.shape[0]
    ti = lax.broadcasted_iota(jnp.int32, (L, L), 0)
    si = lax.broadcasted_iota(jnp.int32, (L, L), 1)
    lower = (si <= ti)
    ig = _lane_col(ig_all, h)
    b = _lane_col(b_all, M_HEADS + h)
    ig_row = igT[h:h + 1, :]
    b_row = bT[M_HEADS + h:M_HEADS + h + 1, :]
    bL = _lane_col(bL_all, M_HEADS + h)
    inter = b + m
    dlog = jnp.where(lower, b - b_row + ig_row, -jnp.inf)
    m_t = lax.stop_gradient(jnp.maximum(inter, jnp.max(dlog, axis=-1, keepdims=True)))
    qs = q * (M_QK ** -0.5)
    w = _dot(qs, k, 1, 1) * jnp.exp(dlog - m_t)
    s_inter = jnp.exp(inter - m_t)
    num = _dot(w, v, 1, 0) + s_inter * _dot(qs, C, 1, 0)
    den = jnp.sum(w, axis=-1, keepdims=True) + s_inter * jnp.sum(_mxf(qs) * _mxf(n), axis=-1, keepdims=True)
    hout = num * (1.0 / jnp.maximum(jnp.abs(den), jnp.exp(-m_t)))
    tail = bL - b + ig
    m_new = lax.stop_gradient(jnp.maximum(bL + m, jnp.max(tail, axis=0, keepdims=True)))
    ws = jnp.exp(tail - m_new)
    decay = jnp.exp(bL + m - m_new)
    wk = ws * k
    C_new = decay * C + _dot(wk, v, 0, 0)
    n_new = decay * n + jnp.sum(_mxf(ws) * _mxf(k), axis=0, keepdims=True)
    hs = hout * lax.rsqrt(jnp.mean(hout * hout, axis=-1, keepdims=True) + EPS) * hn
    gated = _sigmoid(o) * hs
    return (gated, C_new, n_new), m_new


M_OFF_Q, M_OFF_K, M_OFF_V, M_OFF_O, M_OFF_G = 0, M_HEADS * LANES, 2 * M_HEADS * LANES, 3 * M_HEADS * LANES, 4 * M_HEADS * LANES
M_PROJ = M_OFF_G + LANES


def _head_cols(off, h):
    return slice(off + h * LANES, off + (h + 1) * LANES)


def _mlstm_specs(NC, rev):
    H, L = M_HEADS, M_CHUNK
    cc = (lambda c: NC - 1 - c) if rev else (lambda c: c)
    proj = pl.BlockSpec((L, M_PROJ), lambda c: (cc(c), 0))
    vec = pl.BlockSpec((1, LANES), lambda c: (0, 0))
    hn = pl.BlockSpec((1, H * M_V), lambda c: (0, 0))
    hv = pl.BlockSpec((L, H * M_V), lambda c: (cc(c), 0))
    Cs = pl.BlockSpec((None, H, LANES, M_V), lambda c: (cc(c), 0, 0, 0))
    ns = pl.BlockSpec((None, H, 1, LANES), lambda c: (cc(c), 0, 0, 0))
    ms = pl.BlockSpec((None, H, 1, 1), lambda c: (cc(c), 0, 0, 0))
    return proj, vec, hn, hv, Cs, ns, ms


_MLSTM_STATE = [pltpu.VMEM((M_HEADS, LANES, M_V), F32), pltpu.VMEM((M_HEADS, 1, LANES), F32),
                pltpu.VMEM((M_HEADS, 1, 1), F32)]


def _mlstm_fwd(proj, bias, hn, shards, *, name):
    S = proj.shape[0]
    H, NC = M_HEADS, S // M_CHUNK
    ps, vec, hns, hv, Cs, ns, ms = _mlstm_specs(NC, False)
    nsh = len(shards)
    any_spec = pl.BlockSpec(memory_space=pl.ANY)

    def body(p_ref, b_ref, hn_ref, *rest):
        gated_ref, C_all, n_all, m_all = rest[nsh:nsh + 4]
        C_s, n_s, m_s, send_sems, recv_sems = rest[2 * nsh + 4:]
        gather = _Gather(rest[:nsh], rest[nsh + 4:2 * nsh + 4], send_sems, recv_sems)

        @pl.when(pl.program_id(0) == 0)
        def _():
            C_s[...] = jnp.zeros_like(C_s)
            n_s[...] = jnp.zeros_like(n_s)
            m_s[...] = jnp.zeros_like(m_s)
            gather.issue()

        gate_terms = _mlstm_gates(p_ref[:, M_OFF_G:M_OFF_G + LANES], b_ref[...])
        for h in range(H):
            C, n, m = C_s[h], n_s[h], m_s[h]
            C_all[h] = C
            n_all[h] = n
            m_all[h] = m
            (gated, Cn, nn), mn = _mlstm_head(
                h, p_ref[:, _head_cols(M_OFF_Q, h)], p_ref[:, _head_cols(M_OFF_K, h)], p_ref[:, _head_cols(M_OFF_V, h)],
                p_ref[:, _head_cols(M_OFF_O, h)], *gate_terms, hn_ref[:, _head_cols(0, h)], C, n, m)
            gated_ref[:, _head_cols(0, h)] = gated.astype(gated_ref.dtype)
            C_s[h] = Cn
            n_s[h] = nn
            m_s[h] = mn

        @pl.when(pl.program_id(0) == NC - 1)
        def _():
            gather.finish()

    return pl.pallas_call(
        body, name=name, grid=(NC,),
        in_specs=[ps, vec, hns] + [any_spec] * nsh,
        out_specs=[hv, Cs, ns, ms] + [any_spec] * nsh,
        out_shape=[jax.ShapeDtypeStruct((S, H * M_V), MXU_DTYPE),
                   jax.ShapeDtypeStruct((NC, H, LANES, M_V), F32),
                   jax.ShapeDtypeStruct((NC, H, 1, LANES), F32),
                   jax.ShapeDtypeStruct((NC, H, 1, 1), F32)] + _Gather.out_shape(shards),
        scratch_shapes=list(_MLSTM_STATE) + _Gather.semaphores(nsh),
        compiler_params=_cparams(("arbitrary",)),
    )(proj, bias, hn, *shards)


def _mlstm_bwd(proj, bias, hn, C_all, n_all, m_all, dgated, qs, *, name):
    S = proj.shape[0]
    H, NC = M_HEADS, S // M_CHUNK
    ps, vec, hns, hv, Cs, ns, ms = _mlstm_specs(NC, True)
    nq = len(qs)
    any_spec = pl.BlockSpec(memory_space=pl.ANY)

    def body(p_ref, b_ref, hn_ref, C_ref, n_ref, m_ref, dg_ref, *rest):
        dp_ref, db_ref, dhn_ref = rest[nq:nq + 3]
        dC_s, dn_s, send_sems, recv_sems = rest[2 * nq + 3:]
        exchange = _Exchange(rest[:nq], rest[nq + 3:2 * nq + 3], send_sems, recv_sems)

        @pl.when(pl.program_id(0) == 0)
        def _():
            dC_s[...] = jnp.zeros_like(dC_s)
            dn_s[...] = jnp.zeros_like(dn_s)
            db_ref[...] = jnp.zeros_like(db_ref)
            dhn_ref[...] = jnp.zeros_like(dhn_ref)
            exchange.issue()

        gate_terms, gates_vjp = jax.vjp(_mlstm_gates, p_ref[:, M_OFF_G:M_OFF_G + LANES], b_ref[...])
        d_terms = [jnp.zeros_like(t) for t in gate_terms]
        for h in range(H):
            def head(q, k, v, o, *rest, h=h):
                return _mlstm_head(h, q, k, v, o, *rest, m_ref[h])

            prim = (p_ref[:, _head_cols(M_OFF_Q, h)], p_ref[:, _head_cols(M_OFF_K, h)], p_ref[:, _head_cols(M_OFF_V, h)],
                    p_ref[:, _head_cols(M_OFF_O, h)], *gate_terms, hn_ref[:, _head_cols(0, h)], C_ref[h], n_ref[h])
            _, vjp, _ = jax.vjp(head, *prim, has_aux=True)
            dq, dk, dv, do, *d_gate, dhnh, dC, dn = vjp((dg_ref[:, _head_cols(0, h)].astype(F32), dC_s[h], dn_s[h]))
            dp_ref[:, _head_cols(M_OFF_Q, h)] = dq.astype(dp_ref.dtype)
            dp_ref[:, _head_cols(M_OFF_K, h)] = dk.astype(dp_ref.dtype)
            dp_ref[:, _head_cols(M_OFF_V, h)] = dv.astype(dp_ref.dtype)
            dp_ref[:, _head_cols(M_OFF_O, h)] = do.astype(dp_ref.dtype)
            d_terms = [a + g for a, g in zip(d_terms, d_gate)]
            dhn_ref[:, _head_cols(0, h)] += dhnh
            dC_s[h] = dC
            dn_s[h] = dn
        dG, dbias = gates_vjp(tuple(d_terms))
        dp_ref[:, M_OFF_G:M_OFF_G + LANES] = dG.astype(dp_ref.dtype)
        db_ref[...] += dbias

        @pl.when(pl.program_id(0) == NC - 1)
        def _():
            exchange.finish()

    return pl.pallas_call(
        body, name=name, grid=(NC,),
        in_specs=[ps, vec, hns, Cs, ns, ms, hv] + [any_spec] * nq,
        out_specs=[ps, vec, hns] + [any_spec] * nq,
        out_shape=[jax.ShapeDtypeStruct((S, M_PROJ), MXU_DTYPE), jax.ShapeDtypeStruct((1, LANES), F32),
                   jax.ShapeDtypeStruct((1, H * M_V), F32)] + _Exchange.out_shape(qs),
        scratch_shapes=list(_MLSTM_STATE[:2]) + _Exchange.semaphores(nq),
        compiler_params=_cparams(("arbitrary",)),
    )(proj, bias, hn, C_all, n_all, m_all, dgated, *qs)


A_NQ = A_QH * A_DH
A_NKV = 2 * A_KVH * A_DH
A_PAIRS = A_G // 2


def _attn_group(Ps, KLO, KHI, VLO, VHI, sinks, first):
    B = Ps[0].shape[0]
    R = len(Ps) * B
    q2 = jnp.concatenate(Ps, axis=0) * (A_DH ** -0.5)
    s = jnp.concatenate([_dot(q2, KLO, 1, 1), _dot(q2, KHI, 1, 1)], axis=0)
    qi = lax.broadcasted_iota(jnp.int32, (2 * R, 2 * B), 0) & (B - 1)
    ku = lax.broadcasted_iota(jnp.int32, (2 * R, 2 * B), 1)
    diff = qi - (ku - B)
    mask = (diff >= 0) & (diff < B) & ((ku >= B) | jnp.logical_not(first))
    s = jnp.where(mask, s, -jnp.inf)
    ri = lax.broadcasted_iota(jnp.int32, (2 * R, A_G), 0)
    head = 2 * lax.shift_right_logical(ri & (R - 1), B.bit_length() - 1) + lax.shift_right_logical(ri, R.bit_length() - 1)
    onehot = head == lax.broadcasted_iota(jnp.int32, (2 * R, A_G), 1)
    sink = jnp.sum(jnp.where(onehot, sinks, 0.0), axis=1, keepdims=True)
    mx = lax.stop_gradient(jnp.maximum(jnp.max(s, axis=-1, keepdims=True), sink))
    p = jnp.exp(s - mx)
    p = p * (1.0 / (jnp.sum(p, axis=-1, keepdims=True) + jnp.exp(sink - mx)))
    o = _dot(p[:R], VLO, 1, 0) + _dot(p[R:], VHI, 1, 0)
    return tuple(o[j * B:(j + 1) * B] for j in range(len(Ps)))


def _swap_halves_of_lanes(t):
    return pltpu.roll(t, LANES // 2, 1)


def _kv_operands(kvp_ref, kvc_ref, h):
    kk = jnp.concatenate([kvp_ref[:, :LANES], kvc_ref[:, :LANES]], axis=0)
    vv = jnp.concatenate([kvp_ref[:, LANES:], kvc_ref[:, LANES:]], axis=0)
    low = lax.broadcasted_iota(jnp.int32, kk.shape, 1) < A_DH
    own = low if h == 0 else jnp.logical_not(low)
    k_own = jnp.where(own, kk, 0.0)
    v_own = jnp.where(own, vv, 0.0)
    k_oth, v_oth = _swap_halves_of_lanes(k_own), _swap_halves_of_lanes(v_own)
    if h == 0:
        return own, k_own, k_oth, v_own, v_oth
    return own, k_oth, k_own, v_oth, v_own


def _pair_cols(h, j):
    c = (h * A_PAIRS + j) * LANES
    return slice(c, c + LANES)


def _attn_fwd(proj, sinks, *, name):
    S = proj.shape[0]
    NB = S // A_BLK
    kv_blk = A_NQ // A_NKV
    qs = pl.BlockSpec((A_BLK, A_NQ), lambda n: (n, 0))
    cur = pl.BlockSpec((A_BLK, A_NKV), lambda n: (n, kv_blk))
    prev = pl.BlockSpec((A_BLK, A_NKV), lambda n: (jnp.maximum(n - 1, 0), kv_blk))
    sk = pl.BlockSpec((A_KVH, A_G), lambda n: (0, 0))

    def body(q_ref, kvp_ref, kvc_ref, s_ref, o_ref):
        first = pl.program_id(0) == 0
        for h in range(A_KVH):
            _, KLO, KHI, VLO, VHI = _kv_operands(kvp_ref, kvc_ref, h)
            Ps = tuple(q_ref[:, _pair_cols(h, j)] for j in range(A_PAIRS))
            outs = _attn_group(Ps, KLO, KHI, VLO, VHI, s_ref[h:h + 1, :], first)
            for j in range(A_PAIRS):
                o_ref[:, _pair_cols(h, j)] = outs[j].astype(o_ref.dtype)

    return pl.pallas_call(
        body, name=name, grid=(NB,),
        in_specs=[qs, prev, cur, sk], out_specs=qs,
        out_shape=jax.ShapeDtypeStruct((S, A_NQ), MXU_DTYPE),
        compiler_params=_cparams(("parallel",)),
    )(proj, proj, proj, sinks)


def _attn_bwd(proj, sinks, do, *, name):
    S = proj.shape[0]
    NB = S // A_BLK
    last = NB - 1
    kv_blk = A_NQ // A_NKV
    qs = pl.BlockSpec((A_BLK, A_NQ), lambda n: (jnp.minimum(n, last), 0))
    cur = pl.BlockSpec((A_BLK, A_NKV), lambda n: (jnp.minimum(n, last), kv_blk))
    prev = pl.BlockSpec((A_BLK, A_NKV), lambda n: (jnp.clip(n - 1, 0, last), kv_blk))
    sk = pl.BlockSpec((A_KVH, A_G), lambda n: (0, 0))
    lag = pl.BlockSpec((A_BLK, A_NKV), lambda n: (jnp.maximum(n - 1, 0), 0))
    cq_spec = pl.BlockSpec((1, A_NQ), lambda n: (0, 0))
    ckv_spec = pl.BlockSpec((1, A_NKV), lambda n: (0, 0))

    def body(q_ref, kvp_ref, kvc_ref, s_ref, do_ref, dq_ref, dkv_ref, ds_ref, cq_ref, ckv_ref, keep):
        n = pl.program_id(0)

        @pl.when(n == 0)
        def _():
            keep[...] = jnp.zeros_like(keep)
            ds_ref[...] = jnp.zeros_like(ds_ref)
            cq_ref[...] = jnp.zeros_like(cq_ref)
            ckv_ref[...] = jnp.zeros_like(ckv_ref)

        @pl.when(n < NB)
        def _():
            f = functools.partial(_attn_group, first=n == 0)
            dk = jnp.zeros((2 * A_BLK, LANES), F32)
            dv = jnp.zeros((2 * A_BLK, LANES), F32)
            ds_rows = []
            for h in range(A_KVH):
                own, KLO, KHI, VLO, VHI = _kv_operands(kvp_ref, kvc_ref, h)
                Ps = tuple(q_ref[:, _pair_cols(h, j)] for j in range(A_PAIRS))
                _, vjp = jax.vjp(f, Ps, KLO, KHI, VLO, VHI, s_ref[h:h + 1, :])
                dPs, dKLO, dKHI, dVLO, dVHI, dsk = vjp(
                    tuple(do_ref[:, _pair_cols(h, j)].astype(F32) for j in range(A_PAIRS)))
                for j in range(A_PAIRS):
                    dq_ref[:, _pair_cols(h, j)] = dPs[j].astype(dq_ref.dtype)
                    cq_ref[:, _pair_cols(h, j)] += jnp.sum(dPs[j], axis=0, keepdims=True)
                dk_own, dk_oth = (dKLO, dKHI) if h == 0 else (dKHI, dKLO)
                dv_own, dv_oth = (dVLO, dVHI) if h == 0 else (dVHI, dVLO)
                dk = dk + jnp.where(own, dk_own, 0.0) + _swap_halves_of_lanes(jnp.where(own, 0.0, dk_oth))
                dv = dv + jnp.where(own, dv_own, 0.0) + _swap_halves_of_lanes(jnp.where(own, 0.0, dv_oth))
                ds_rows.append(dsk)
            ds_ref[...] += jnp.concatenate(ds_rows, axis=0)
            dkv = jnp.concatenate([dk, dv], axis=1)
            done = keep[...] + dkv[:A_BLK]
            dkv_ref[...] = done.astype(dkv_ref.dtype)
            ckv_ref[...] += jnp.sum(done, axis=0, keepdims=True)
            keep[...] = dkv[A_BLK:]

        @pl.when(n == NB)
        def _():
            done = keep[...]
            dkv_ref[...] = done.astype(dkv_ref.dtype)
            ckv_ref[...] += jnp.sum(done, axis=0, keepdims=True)

    return pl.pallas_call(
        body, name=name, grid=(NB + 1,),
        in_specs=[qs, prev, cur, sk, qs],
        out_specs=[qs, lag, sk, cq_spec, ckv_spec],
        out_shape=[jax.ShapeDtypeStruct((S, A_NQ), MXU_DTYPE), jax.ShapeDtypeStruct((S, A_NKV), MXU_DTYPE),
                   jax.ShapeDtypeStruct((A_KVH, A_G), F32), jax.ShapeDtypeStruct((1, A_NQ), F32),
                   jax.ShapeDtypeStruct((1, A_NKV), F32)],
        scratch_shapes=[pltpu.VMEM((A_BLK, A_NKV), F32)],
        compiler_params=_cparams(("arbitrary",)),
    )(proj, proj, proj, sinks, do)


HALO = 8


def _shift_rows(t, j):
    if j == 0:
        return t
    return pltpu.roll(t, j % t.shape[0], 0)


def _conv_gate(gate_ext, cw, cb):
    return cb + cw[0:1, :] * _shift_rows(gate_ext, 2) + cw[1:2, :] * _shift_rows(gate_ext, 1) + cw[2:3, :] * gate_ext


def _convgate_fwd(u, cw, cb, *, name):
    S, F2 = u.shape
    F = F2 // 2
    tm = _pick(S, (256, 128))
    hb = tm // HALO
    urow = pl.BlockSpec((tm, F2), lambda i: (i, 0))
    uprev = pl.BlockSpec((HALO, F), lambda i: (jnp.maximum(i * hb - 1, 0), 0))

    def body(u_ref, up_ref, cw_ref, cb_ref, a_ref):
        i = pl.program_id(0)
        gate = u_ref[:, :F]
        val = u_ref[:, F:]
        prev = jnp.where(i > 0, up_ref[...], 0.0)
        gc = _conv_gate(jnp.concatenate([prev, gate], axis=0), cw_ref[...], cb_ref[...])[HALO:]
        a_ref[...] = (gc * _sigmoid(gc) * val).astype(a_ref.dtype)

    return pl.pallas_call(
        body, name=name, grid=(S // tm,),
        in_specs=[urow, uprev, pl.BlockSpec((3, F), lambda i: (0, 0)), pl.BlockSpec((1, F), lambda i: (0, 0))],
        out_specs=pl.BlockSpec((tm, F), lambda i: (i, 0)),
        out_shape=jax.ShapeDtypeStruct((S, F), MXU_DTYPE),
        compiler_params=_cparams(("parallel",)),
    )(u, u, cw, cb)


def _convgate_bwd(u, da, cw, cb, *, name):
    S, F2 = u.shape
    F = F2 // 2
    tm = _pick(S, (128,))
    hb = tm // HALO
    nt = S // tm
    nh = S // HALO
    urow = pl.BlockSpec((tm, F2), lambda i: (i, 0))
    uprev = pl.BlockSpec((HALO, F), lambda i: (jnp.maximum(i * hb - 1, 0), 0))
    unext = pl.BlockSpec((HALO, F2), lambda i: (jnp.minimum((i + 1) * hb, nh - 1), 0))
    darow = pl.BlockSpec((tm, F), lambda i: (i, 0))
    danext = pl.BlockSpec((HALO, F), lambda i: (jnp.minimum((i + 1) * hb, nh - 1), 0))

    def body(u_ref, up_ref, un_ref, da_ref, dan_ref, cw_ref, cb_ref, du_ref, dcw_ref, dcb_ref):
        i = pl.program_id(0)
        cwv = cw_ref[...]
        prev = jnp.where(i > 0, up_ref[...], 0.0)
        gate_ext = jnp.concatenate([prev, u_ref[:, :F], un_ref[:, :F]], axis=0)
        val_ext = jnp.concatenate([u_ref[:, F:], un_ref[:, F:]], axis=0)
        da_next = jnp.where(i < nt - 1, dan_ref[...].astype(F32), 0.0)
        da_ext = jnp.concatenate([da_ref[...].astype(F32), da_next], axis=0)
        gc = _conv_gate(gate_ext, cwv, cb_ref[...])[HALO:]
        sg = _sigmoid(gc)
        silu = gc * sg
        dval = da_ext * silu
        dgc = da_ext * val_ext * (sg * (1.0 + gc * (1.0 - sg)))
        dgate = cwv[2:3, :] * dgc + cwv[1:2, :] * _shift_rows(dgc, -1) + cwv[0:1, :] * _shift_rows(dgc, -2)
        du_ref[:, :F] = dgate[:tm].astype(du_ref.dtype)
        du_ref[:, F:] = dval[:tm].astype(du_ref.dtype)
        dgc_c = dgc[:tm]
        g0 = gate_ext[HALO:HALO + tm]
        g1 = _shift_rows(gate_ext, 1)[HALO:HALO + tm]
        g2 = _shift_rows(gate_ext, 2)[HALO:HALO + tm]
        dcw = jnp.concatenate([jnp.sum(dgc_c * g2, axis=0, keepdims=True),
                               jnp.sum(dgc_c * g1, axis=0, keepdims=True),
                               jnp.sum(dgc_c * g0, axis=0, keepdims=True)], axis=0)
        dcb = jnp.sum(dgc_c, axis=0, keepdims=True)

        @pl.when(i == 0)
        def _():
            dcw_ref[...] = dcw
            dcb_ref[...] = dcb

        @pl.when(i > 0)
        def _():
            dcw_ref[...] += dcw
            dcb_ref[...] += dcb

    return pl.pallas_call(
        body, name=name, grid=(nt,),
        in_specs=[urow, uprev, unext, darow, danext,
                  pl.BlockSpec((3, F), lambda i: (0, 0)), pl.BlockSpec((1, F), lambda i: (0, 0))],
        out_specs=[urow, pl.BlockSpec((3, F), lambda i: (0, 0)), pl.BlockSpec((1, F), lambda i: (0, 0))],
        out_shape=[jax.ShapeDtypeStruct((S, F2), MXU_DTYPE), jax.ShapeDtypeStruct((3, F), F32),
                   jax.ShapeDtypeStruct((1, F), F32)],
        compiler_params=_cparams(("arbitrary",)),
    )(u, u, u, da, da, cw, cb)


def _adamw_math(w, g, m, v):
    m = ADAM_B1 * m + (1.0 - ADAM_B1) * g
    v = ADAM_B2 * v + (1.0 - ADAM_B2) * (g * g)
    m_hat = m / (1.0 - ADAM_B1 ** ADAM_STEP)
    v_hat = v / (1.0 - ADAM_B2 ** ADAM_STEP)
    delta = -ADAM_LR * (m_hat / (jnp.sqrt(v_hat) + ADAM_EPS) + ADAM_WD * w)
    return delta, m, v


def _adamw_layers(w, gs, m, v, *, name):
    Lr, R, C = w.shape
    tr = _pick(R, (256, 128, 64, 32, 16, 8))
    outs = None
    for layer, g in enumerate(gs):
        def body(w_ref, g_ref, m_ref, v_ref, *rest):
            go_ref, d_ref, nm_ref, nv_ref = rest[-4:]
            gv = g_ref[...]
            d, nm, nv = _adamw_math(w_ref[...], gv, m_ref[...], v_ref[...])
            go_ref[...] = gv
            d_ref[...] = d
            nm_ref[...] = nm
            nv_ref[...] = nv

        lay = pl.BlockSpec((None, tr, C), lambda i, layer=layer: (layer, i, 0))
        in_specs = [lay, pl.BlockSpec((tr, C), lambda i: (i, 0)), lay, lay]
        args = [w, g, m, v]
        aliases = {}
        if outs is not None:
            in_specs += [pl.BlockSpec(memory_space=pl.ANY)] * 4
            args += list(outs)
            aliases = {4 + t: t for t in range(4)}
        outs = pl.pallas_call(
            body, name=f"{name}_{layer}", grid=(R // tr,), in_specs=in_specs, out_specs=[lay] * 4,
            out_shape=[jax.ShapeDtypeStruct((Lr, R, C), F32)] * 4, input_output_aliases=aliases,
            compiler_params=_cparams(("parallel",)),
        )(*args)
    return outs


def _adamw_small(items, *, name):
    n = len(items)

    def body(*refs):
        ins, outs = refs[:4 * n], refs[4 * n:]
        for t in range(n):
            w, g, m, v = (r[...] for r in ins[4 * t:4 * t + 4])
            d, nm, nv = _adamw_math(w, g, m, v)
            outs[3 * t][...] = d
            outs[3 * t + 1][...] = nm
            outs[3 * t + 2][...] = nv

    flat = [a for it in items for a in it]
    out_shape = [jax.ShapeDtypeStruct(it[0].shape, F32) for it in items for _ in range(3)]
    vm = pl.BlockSpec(memory_space=pltpu.VMEM)
    res = pl.pallas_call(body, name=name, in_specs=[vm] * len(flat), out_specs=[vm] * len(out_shape),
                         out_shape=out_shape)(*flat)
    return [tuple(res[3 * t:3 * t + 3]) for t in range(n)]


def _place():
    return lax.axis_index("x"), lax.axis_index("y"), lax.axis_index("c")


_FLIPS = ((1, 0), (0, 1), (1, 1))


ROW_ALIGN = 16


def _half(rows, which):
    return pl.ds(pl.multiple_of(which * (rows // 2), ROW_ALIGN), rows // 2)


def _remote(src, dst, send_sems, recv_sems, k, to):
    return pltpu.make_async_remote_copy(src_ref=src, dst_ref=dst, send_sem=send_sems.at[k], recv_sem=recv_sems.at[k],
                                        device_id=to, device_id_type=MESH)


def _gather_shards(shards, *, name):
    n = len(shards)
    any_spec = pl.BlockSpec(memory_space=pl.ANY)

    def body(*refs):
        gather = _Gather(refs[:n], refs[n:2 * n], refs[2 * n], refs[2 * n + 1])
        gather.issue()
        gather.finish()

    return pl.pallas_call(
        body, name=name, in_specs=[any_spec] * n, out_specs=[any_spec] * n,
        out_shape=_Gather.out_shape(shards), scratch_shapes=_Gather.semaphores(n),
    )(*shards)


class _Gather:
    def __init__(self, w_refs, out_refs, send_sems, recv_sems):
        self.w_refs, self.out_refs, self.send_sems, self.recv_sems = w_refs, out_refs, send_sems, recv_sems
        self.pairs = [(i, j) for i in range(len(w_refs)) for j in range(3)]

    @staticmethod
    def out_shape(shards):
        return [jax.ShapeDtypeStruct((N_CHIPS,) + s.shape, s.dtype) for s in shards]

    @staticmethod
    def semaphores(n):
        return [pltpu.SemaphoreType.DMA((6 * n,)), pltpu.SemaphoreType.DMA((6 * n,))]

    def _where(self):
        x, y, c = _place()
        return x, y, c, [(x ^ fx, y ^ fy) for fx, fy in _FLIPS]

    def _over_ici(self, i, j, landed):
        x, y, c, chips = self._where()
        px, py = chips[j]
        mine = _half(self.w_refs[i].shape[0], c)
        if landed:
            src = dst = self.out_refs[i].at[2 * px + py, mine]
        else:
            src, dst = self.w_refs[i].at[mine], self.out_refs[i].at[2 * x + y, mine]
        return _remote(src, dst, self.send_sems, self.recv_sems, 6 * i + j, (px, py, c))

    def _over_d2d(self, i, j, which):
        x, y, c, chips = self._where()
        px, py = chips[j]
        blk = self.out_refs[i].at[2 * px + py, _half(self.w_refs[i].shape[0], which)]
        return _remote(blk, blk, self.send_sems, self.recv_sems, 6 * i + 3 + j, (x, y, 1 - c))

    def issue(self):
        for i, j in self.pairs:
            self._over_ici(i, j, False).start()

    def finish(self):
        c = lax.axis_index("c")
        for i, j in self.pairs:
            self._over_ici(i, j, True).wait_recv()
            self._over_d2d(i, j, c).start()
        for i, j in self.pairs:
            self._over_d2d(i, j, 1 - c).wait_recv()
        for i, j in self.pairs:
            self._over_ici(i, j, False).wait_send()
            self._over_d2d(i, j, c).wait_send()


def _swap_halves(gs, *, name):
    n = len(gs)
    any_spec = pl.BlockSpec(memory_space=pl.ANY)

    def body(*refs):
        g_refs, out_refs, send_sems, recv_sems = refs[:n], refs[n:2 * n], refs[2 * n], refs[2 * n + 1]
        x, y, c = _place()
        cps = [_remote(g_refs[i].at[:, _half(gs[i].shape[1], 1 - c)], out_refs[i], send_sems, recv_sems, i, (x, y, 1 - c))
               for i in range(n)]
        for cp in cps:
            cp.start()
        for cp in cps:
            cp.wait()

    return pl.pallas_call(
        body, name=name, in_specs=[any_spec] * n, out_specs=[any_spec] * n,
        out_shape=[jax.ShapeDtypeStruct((N_CHIPS, g.shape[1] // 2, g.shape[2]), g.dtype) for g in gs],
        scratch_shapes=[pltpu.SemaphoreType.DMA((n,)), pltpu.SemaphoreType.DMA((n,))],
    )(*gs)


def _add_halves(g, got, c_arr, *, name):
    _, rows, cols = g.shape
    blk = (None, rows // 2, cols)

    def body(c_ref, g_ref, got_ref, o_ref):
        o_ref[...] = (g_ref[...] + got_ref[...]).astype(o_ref.dtype)

    return pl.pallas_call(
        body, name=name,
        grid_spec=pltpu.PrefetchScalarGridSpec(
            num_scalar_prefetch=1, grid=(N_CHIPS,),
            in_specs=[pl.BlockSpec(blk, lambda j, c_ref: (j, c_ref[0], 0)), pl.BlockSpec(blk, lambda j, c_ref: (j, 0, 0))],
            out_specs=pl.BlockSpec(blk, lambda j, c_ref: (j, 0, 0))),
        out_shape=jax.ShapeDtypeStruct((N_CHIPS, rows // 2, cols), WIRE_DTYPE),
        compiler_params=_cparams(("parallel",)),
    )(c_arr, g, got)


def _exchange_chips(qs, *, name):
    n = len(qs)
    any_spec = pl.BlockSpec(memory_space=pl.ANY)

    def body(*refs):
        exchange = _Exchange(refs[:n], refs[n:2 * n], refs[2 * n], refs[2 * n + 1])
        exchange.issue()
        exchange.finish()

    return pl.pallas_call(
        body, name=name, in_specs=[any_spec] * n, out_specs=[any_spec] * n,
        out_shape=_Exchange.out_shape(qs), scratch_shapes=_Exchange.semaphores(n),
    )(*qs)


class _Exchange:
    def __init__(self, q_refs, out_refs, send_sems, recv_sems):
        self.q_refs, self.out_refs, self.send_sems, self.recv_sems = q_refs, out_refs, send_sems, recv_sems
        self.pairs = [(i, j) for i in range(len(q_refs)) for j in range(3)]

    @staticmethod
    def out_shape(qs):
        return [jax.ShapeDtypeStruct(q.shape, q.dtype) for q in qs]

    @staticmethod
    def semaphores(n):
        return [pltpu.SemaphoreType.DMA((3 * n,)), pltpu.SemaphoreType.DMA((3 * n,))]

    def _copy(self, i, j, landed):
        x, y, c = _place()
        px, py = [(x ^ fx, y ^ fy) for fx, fy in _FLIPS][j]
        if landed:
            src = dst = self.out_refs[i].at[2 * px + py]
        else:
            src, dst = self.q_refs[i].at[2 * px + py], self.out_refs[i].at[2 * x + y]
        return _remote(src, dst, self.send_sems, self.recv_sems, 3 * i + j, (px, py, c))

    def issue(self):
        for i, j in self.pairs:
            self._copy(i, j, False).start()

    def finish(self):
        for i, j in self.pairs:
            self._copy(i, j, True).wait_recv()
        for i, j in self.pairs:
            self._copy(i, j, False).wait_send()


def _sum_chips(q, r, place_arr, *, name):
    _, h, cols = q.shape
    blk = (None, h, cols)

    def body(p_ref, q_ref, r1_ref, r2_ref, r3_ref, o_ref):
        o_ref[...] = ((q_ref[...].astype(F32) + r1_ref[...].astype(F32)) + r2_ref[...].astype(F32)) + r3_ref[...].astype(F32)

    other = [pl.BlockSpec(blk, lambda i, p_ref, f=f: (p_ref[0] ^ f, 0, 0)) for f in (1, 2, 3)]
    return pl.pallas_call(
        body, name=name,
        grid_spec=pltpu.PrefetchScalarGridSpec(
            num_scalar_prefetch=1, grid=(1,),
            in_specs=[pl.BlockSpec(blk, lambda i, p_ref: (p_ref[0], 0, 0))] + other,
            out_specs=pl.BlockSpec((h, cols), lambda i, p_ref: (p_ref[1], 0))),
        out_shape=jax.ShapeDtypeStruct((2 * h, cols), F32),
        compiler_params=_cparams(("arbitrary",)),
    )(place_arr, q, r, r, r)


def _join_halves(fs, *, name):
    n = len(fs)
    any_spec = pl.BlockSpec(memory_space=pl.ANY)

    def body(*refs):
        out_refs, send_sems, recv_sems = refs[n:2 * n], refs[2 * n], refs[2 * n + 1]
        x, y, c = _place()
        sent = []
        for i in range(n):
            mine = out_refs[i].at[_half(fs[i].shape[0], c)]
            cp = _remote(mine, mine, send_sems, recv_sems, i, (x, y, 1 - c))
            cp.start()
            sent.append(cp)
        for i in range(n):
            its = out_refs[i].at[_half(fs[i].shape[0], 1 - c)]
            _remote(its, its, send_sems, recv_sems, i, (x, y, 1 - c)).wait_recv()
        for cp in sent:
            cp.wait_send()

    return pl.pallas_call(
        body, name=name, in_specs=[any_spec] * n, out_specs=[any_spec] * n,
        out_shape=[jax.ShapeDtypeStruct(f.shape, f.dtype) for f in fs], input_output_aliases={i: i for i in range(n)},
        scratch_shapes=[pltpu.SemaphoreType.DMA((n,)), pltpu.SemaphoreType.DMA((n,))],
    )(*fs)


def _allreduce_small(buf, *, name):
    R = buf.shape[0]
    vm = pl.BlockSpec(memory_space=pltpu.VMEM)

    def body(b_ref, o_ref, slots, send_sems, recv_sems):
        x, y, c = _place()
        me = 4 * x + 2 * y + c
        slots[me] = b_ref[...]
        sends = []
        for kk in range(1, 8):
            fx, fy, fc = (kk >> 2) & 1, (kk >> 1) & 1, kk & 1
            cp = pltpu.make_async_remote_copy(
                src_ref=b_ref, dst_ref=slots.at[me], send_sem=send_sems.at[kk - 1], recv_sem=recv_sems.at[kk - 1],
                device_id=(x ^ fx, y ^ fy, c ^ fc), device_id_type=MESH)
            cp.start()
            sends.append(cp)
        for kk in range(1, 8):
            fx, fy, fc = (kk >> 2) & 1, (kk >> 1) & 1, kk & 1
            peer = 4 * (x ^ fx) + 2 * (y ^ fy) + (c ^ fc)
            pltpu.make_async_remote_copy(
                src_ref=b_ref, dst_ref=slots.at[peer], send_sem=send_sems.at[kk - 1], recv_sem=recv_sems.at[kk - 1],
                device_id=(x ^ fx, y ^ fy, c ^ fc), device_id_type=MESH).wait_recv()
        for cp in sends:
            cp.wait_send()
        acc = slots[0]
        for d in range(1, 8):
            acc = acc + slots[d]
        o_ref[...] = acc

    return pl.pallas_call(
        body, name=name, in_specs=[vm], out_specs=vm,
        out_shape=jax.ShapeDtypeStruct((R, LANES), F32),
        scratch_shapes=[pltpu.VMEM((8, R, LANES), F32), pltpu.SemaphoreType.DMA((7,)), pltpu.SemaphoreType.DMA((7,))],
    )(buf)


def _pad_rows(v, mult=8 * LANES):
    flat = v.reshape(-1)
    n = flat.shape[0]
    tot = -(-n // mult) * mult
    return jnp.pad(flat, (0, tot - n)).reshape(-1, LANES)


def _pack_small(parts):
    return jnp.concatenate([_pad_rows(p.astype(F32)) for p in parts], axis=0)


def _unpack_small(buf, shapes):
    out, r = [], 0
    for sh in shapes:
        n = math.prod(sh)
        rows = -(-n // (8 * LANES)) * 8
        out.append(buf[r:r + rows].reshape(-1)[:n].reshape(sh))
        r += rows
    return out


def _cols_to_shards(w):
    *lead, K, N = w.shape
    t = w.reshape(*lead, K, N_CHIPS, N // N_CHIPS)
    return jnp.moveaxis(t, -2, 0)


def _shards_to_cols(t):
    t = jnp.moveaxis(t, 0, -2)
    *lead, K, _, n = t.shape
    return t.reshape(*lead, K, N_CHIPS * n)


_BIG = ("m_w_in", "m_w_out", "a_w_in", "a_w_out", "f_w_up", "f_w_down")


def kernel(x, m_w_in, m_gate_bias, m_head_norm, m_w_out, a_w_in, a_b_in, a_sinks, a_w_out, a_b_out, norm_mix_pre, norm_mix_post, norm_ffn_pre, norm_ffn_post, f_w_up, f_conv_w, f_conv_b, f_w_down, loss_target, m_m_w_in, m_m_gate_bias, m_m_head_norm, m_m_w_out, m_a_w_in, m_a_b_in, m_a_sinks, m_a_w_out, m_a_b_out, m_norm_mix_pre, m_norm_mix_post, m_norm_ffn_pre, m_norm_ffn_post, m_f_w_up, m_f_conv_w, m_f_conv_b, m_f_w_down, v_m_w_in, v_m_gate_bias, v_m_head_norm, v_m_w_out, v_a_w_in, v_a_b_in, v_a_sinks, v_a_w_out, v_a_b_out, v_norm_mix_pre, v_norm_mix_post, v_norm_ffn_pre, v_norm_ffn_post, v_f_w_up, v_f_conv_w, v_f_conv_b, v_f_w_down):
    params = dict(m_w_in=m_w_in, m_gate_bias=m_gate_bias, m_head_norm=m_head_norm, m_w_out=m_w_out, a_w_in=a_w_in,
                  a_b_in=a_b_in, a_sinks=a_sinks, a_w_out=a_w_out, a_b_out=a_b_out, norm_mix_pre=norm_mix_pre,
                  norm_mix_post=norm_mix_post, norm_ffn_pre=norm_ffn_pre, norm_ffn_post=norm_ffn_post, f_w_up=f_w_up,
                  f_conv_w=f_conv_w, f_conv_b=f_conv_b, f_w_down=f_w_down)
    mom1 = dict(m_w_in=m_m_w_in, m_gate_bias=m_m_gate_bias, m_head_norm=m_m_head_norm, m_w_out=m_m_w_out,
                a_w_in=m_a_w_in, a_b_in=m_a_b_in, a_sinks=m_a_sinks, a_w_out=m_a_w_out, a_b_out=m_a_b_out,
                norm_mix_pre=m_norm_mix_pre, norm_mix_post=m_norm_mix_post, norm_ffn_pre=m_norm_ffn_pre,
                norm_ffn_post=m_norm_ffn_post, f_w_up=m_f_w_up, f_conv_w=m_f_conv_w, f_conv_b=m_f_conv_b,
                f_w_down=m_f_w_down)
    mom2 = dict(m_w_in=v_m_w_in, m_gate_bias=v_m_gate_bias, m_head_norm=v_m_head_norm, m_w_out=v_m_w_out,
                a_w_in=v_a_w_in, a_b_in=v_a_b_in, a_sinks=v_a_sinks, a_w_out=v_a_w_out, a_b_out=v_a_b_out,
                norm_mix_pre=v_norm_mix_pre, norm_mix_post=v_norm_mix_post, norm_ffn_pre=v_norm_ffn_pre,
                norm_ffn_post=v_norm_ffn_post, f_w_up=v_f_w_up, f_conv_w=v_f_conv_w, f_conv_b=v_f_conv_b,
                f_w_down=v_f_w_down)
    order = list(params)

    mx, my, mc = _place()
    chip = 2 * mx + my
    h0 = x[0]
    target = loss_target[0]

    def two_d(t):
        return t.reshape(-1, t.shape[-1])

    early = ("m_w_in", "m_w_out")
    late = tuple(n for n in _BIG if n not in early)
    mine = {n: two_d(params[n]).astype(MXU_DTYPE) for n in _BIG}

    def with_own_slot(names, theirs):
        return {n: lax.dynamic_update_slice(t, mine[n][None], (chip, 0, 0)) for n, t in zip(names, theirs)}

    gathered = with_own_slot(early, _gather_shards([mine[n] for n in early], name="gather_weights_early"))

    def in_place(shard, axis):
        width = shard.shape[axis]
        z = jnp.zeros(shard.shape[:axis] + (N_CHIPS * width,) + shard.shape[axis + 1:], F32)
        contrib = jnp.where(mc == 0, shard, 0.0)
        return lax.dynamic_update_slice_in_dim(z, contrib, chip * width, axis)

    sm_in = [in_place(a_b_in, 1), in_place(a_b_out, 1), in_place(f_conv_w, 2)]
    sm_full = _unpack_small(_allreduce_small(_pack_small(sm_in), name="gather_small"), [t.shape for t in sm_in])
    b_in_full, b_out_full, conv_w_full = sm_full

    W_in = _shards_to_cols(gathered["m_w_in"])
    nqk = M_HEADS * M_QK
    n_main = 2 * nqk + M_HEADS * M_V + D_MODEL

    def pad_heads(w):
        return jnp.pad(w.reshape(D_MODEL, M_HEADS, M_QK), ((0, 0), (0, 0), (0, LANES - M_QK))).reshape(D_MODEL, -1)

    def unpad_heads(w):
        return w.reshape(D_MODEL, M_HEADS, LANES)[:, :, :M_QK].reshape(D_MODEL, nqk)

    W_all = jnp.concatenate([pad_heads(W_in[:, :nqk]), pad_heads(W_in[:, nqk:2 * nqk]), W_in[:, 2 * nqk:n_main],
                             jnp.pad(W_in[:, n_main:], ((0, 0), (0, LANES - 2 * M_HEADS)))], axis=1)
    gbias = jnp.pad(m_gate_bias[0].reshape(1, 2 * M_HEADS), ((0, 0), (0, LANES - 2 * M_HEADS)))
    W_mout = gathered["m_w_out"].reshape(D_MODEL, D_MODEL)

    grads = {}

    def ffn_fwd(i, h1):
        z1 = _rms_fwd(h1, norm_ffn_pre[i:i + 1], out_dtype=MXU_DTYPE, name=f"ffn_pre_norm{i}")
        u = _mm(z1, W_up, b_layer=i, name=f"ffn_up{i}")
        a = _convgate_fwd(u, conv_w_full[i], f_conv_b[i:i + 1], name=f"ffn_act{i}")
        zf = _mm(a, W_down[i], name=f"ffn_down{i}")
        h2 = _rms_fwd(zf, norm_ffn_post[i:i + 1], res=h1, out_dtype=F32, name=f"ffn_post_norm{i}")
        return h2, (z1, u, a, zf)

    z0 = _rms_fwd(h0, norm_mix_pre[0:1], out_dtype=MXU_DTYPE, name="mix_pre_norm0")
    proj = _mm(z0, W_all, name="mlstm_proj")
    gated, C_all, n_all, m_all, *late_theirs = _mlstm_fwd(proj, gbias, m_head_norm, [mine[n] for n in late],
                                                          name="mlstm_fwd")
    gathered.update(with_own_slot(late, late_theirs))
    A_in, A_out = _shards_to_cols(gathered["a_w_in"]), gathered["a_w_out"].reshape(D_MODEL, D_MODEL)
    W_up = gathered["f_w_up"]
    dsh = D_FF // N_CHIPS
    W_down = [gathered["f_w_down"][:, i * dsh:(i + 1) * dsh].reshape(D_FF, D_MODEL) for i in range(2)]
    zm0 = _mm(gated, W_mout, name="mlstm_out")
    h1 = _rms_fwd(zm0, norm_mix_post[0:1], res=h0, out_dtype=F32, name="mix_post_norm0")
    h2, ffn0 = ffn_fwd(0, h1)

    y0 = _rms_fwd(h2, norm_mix_pre[1:2], out_dtype=MXU_DTYPE, name="mix_pre_norm1")
    aproj = _mm(y0, A_in, bias=b_in_full, name="attn_proj")
    sinks = a_sinks.reshape(A_KVH, A_G)
    ao = _attn_fwd(aproj, sinks, name="attn_fwd")
    zm1 = _mm(ao, A_out, bias=b_out_full, name="attn_out")
    h3 = _rms_fwd(zm1, norm_mix_post[1:2], res=h2, out_dtype=F32, name="mix_post_norm1")
    h4, ffn1 = ffn_fwd(1, h3)

    loss_part, dh = _loss_head(h4, target, name="loss_head")
    loss = lax.psum(loss_part[0, 0], ("x", "y", "c"))

    g_post, g_fpre, g_mpost, g_mpre = [None, None], [None, None], [None, None], [None, None]
    dW_up, dW_down, dconv_w, dconv_b = [None, None], [None, None], [None, None], [None, None]

    def ffn_bwd(i, dh2, h1_, saved):
        z1, u, a, zf = saved
        dzf, g_post[i], _ = _rms_bwd(zf, norm_ffn_post[i:i + 1], dh2, out_dtype=MXU_DTYPE, name=f"ffn_post_norm_bwd{i}")
        da = _mm(dzf, W_down[i], tb=True, name=f"ffn_down_dx{i}")
        dW_down[i] = _mm(a, dzf, ta=True, name=f"ffn_down_dw{i}")
        du, dconv_w[i], dconv_b[i] = _convgate_bwd(u, da, conv_w_full[i], f_conv_b[i:i + 1], name=f"ffn_act_bwd{i}")
        dz1 = _mm(du, W_up, tb=True, b_layer=i, name=f"ffn_up_dx{i}")
        dW_up[i] = _mm(z1, du, ta=True, out_stacked=True, name=f"ffn_up_dw{i}")
        dh1, g_fpre[i], _ = _rms_bwd(h1_, norm_ffn_pre[i:i + 1], dz1, res=dh2, out_dtype=F32, name=f"ffn_pre_norm_bwd{i}")
        return dh1

    dh3 = ffn_bwd(1, dh, h3, ffn1)
    dzm1, g_mpost[1], db_out = _rms_bwd(zm1, norm_mix_post[1:2], dh3, out_dtype=MXU_DTYPE, name="mix_post_norm_bwd1")
    dao = _mm(dzm1, A_out, tb=True, name="attn_out_dx")
    dA_out = _mm(ao, dzm1, ta=True, name="attn_out_dw")
    daq, dakv, dsinks, cs_q, cs_kv = _attn_bwd(aproj, sinks, dao, name="attn_bwd")
    db_in = jnp.concatenate([cs_q, cs_kv], axis=1)
    dy0 = _mm(daq, A_in[:, :A_NQ], tb=True, add=_mm(dakv, A_in[:, A_NQ:], tb=True, name="attn_proj_kv_dx"),
              name="attn_proj_q_dx")
    dA_in = jnp.concatenate([_mm(y0, daq, ta=True, name="attn_proj_q_dw"),
                             _mm(y0, dakv, ta=True, name="attn_proj_kv_dw")], axis=1)
    dh2, g_mpre[1], _ = _rms_bwd(h2, norm_mix_pre[1:2], dy0, res=dh3, out_dtype=F32, name="mix_pre_norm_bwd1")

    dh1 = ffn_bwd(0, dh2, h1, ffn0)
    dzm0, g_mpost[0], _ = _rms_bwd(zm0, norm_mix_post[0:1], dh1, out_dtype=MXU_DTYPE, name="mix_post_norm_bwd0")
    dgated = _mm(dzm0, W_mout, tb=True, name="mlstm_out_dx")
    dW_mout = _mm(gated, dzm0, ta=True, name="mlstm_out_dw")
    c_arr = jnp.reshape(mc, (1,)).astype(jnp.int32)
    place_arr = jnp.stack([chip, mc]).astype(jnp.int32)

    def sibling_sums(tags, gs, stage):
        got = _swap_halves(gs, name=f"grad_swap_halves_{stage}")
        return [_add_halves(g, t, c_arr, name=f"grad_add_halves_{tag}") for tag, g, t in zip(tags, gs, got)]

    late_tags = ["a_w_in", "a_w_out", "f_w_up0", "f_w_up1", "f_w_down0", "f_w_down1"]
    late_part = sibling_sums(late_tags, [
        _cols_to_shards(dA_in), dA_out.reshape(N_CHIPS, -1, D_MODEL), dW_up[0], dW_up[1],
        dW_down[0].reshape(N_CHIPS, -1, D_MODEL), dW_down[1].reshape(N_CHIPS, -1, D_MODEL)], "late")
    dproj, dgbias, dhn, *late_from = _mlstm_bwd(proj, gbias, m_head_norm, C_all, n_all, m_all, dgated, late_part,
                                                name="mlstm_bwd")
    dz0 = _mm(dproj, W_all, tb=True, name="mlstm_proj_dx")
    dW_all = _mm(z0, dproj, ta=True, name="mlstm_proj_dw")
    dW_min = jnp.concatenate([unpad_heads(dW_all[:, M_OFF_Q:M_OFF_K]), unpad_heads(dW_all[:, M_OFF_K:M_OFF_V]),
                              dW_all[:, M_OFF_V:M_OFF_G], dW_all[:, M_OFF_G:M_OFF_G + 2 * M_HEADS]], axis=1)
    grad_x, g_mpre[0], _ = _rms_bwd(h0, norm_mix_pre[0:1], dz0, res=dh1, out_dtype=F32, name="mix_pre_norm_bwd0")

    early_tags = ["m_w_in", "m_w_out"]
    early_part = sibling_sums(early_tags, [_cols_to_shards(dW_min), dW_mout.reshape(N_CHIPS, -1, D_MODEL)], "early")
    early_from = _exchange_chips(early_part, name="grad_exchange_chips_early")
    tags = early_tags + late_tags
    halves = [_sum_chips(q, r, place_arr, name=f"grad_sum_chips_{tag}")
              for tag, q, r in zip(tags, early_part + late_part, list(early_from) + list(late_from))]
    reduced = dict(zip(tags, _join_halves(halves, name="grad_join_halves")))
    layer_grads = dict(m_w_in=[reduced["m_w_in"]], m_w_out=[reduced["m_w_out"]], a_w_in=[reduced["a_w_in"]],
                       a_w_out=[reduced["a_w_out"]], f_w_up=[reduced["f_w_up0"], reduced["f_w_up1"]],
                       f_w_down=[reduced["f_w_down0"], reduced["f_w_down1"]])

    small_g = [
        dgbias[:, :2 * M_HEADS].reshape(1, 2, M_HEADS),
        dhn,
        dsinks.reshape(1, A_QH),
        db_in, db_out,
        jnp.concatenate(g_mpre), jnp.concatenate(g_mpost), jnp.concatenate(g_fpre), jnp.concatenate(g_post),
        jnp.stack(dconv_w), jnp.concatenate(dconv_b),
    ]
    small_names = ["m_gate_bias", "m_head_norm", "a_sinks", "a_b_in", "a_b_out", "norm_mix_pre", "norm_mix_post",
                   "norm_ffn_pre", "norm_ffn_post", "f_conv_w", "f_conv_b"]
    red = _unpack_small(_allreduce_small(_pack_small(small_g), name="reduce_small"), [t.shape for t in small_g])
    for n, t in zip(small_names, red):
        if n in ("a_b_in", "a_b_out", "f_conv_w"):
            axis = t.ndim - 1
            width = params[n].shape[axis]
            t = lax.dynamic_slice_in_dim(t, chip * width, width, axis)
        grads[n] = t

    deltas, new_m, new_v = {}, {}, {}
    for n in _BIG:
        grads[n], deltas[n], new_m[n], new_v[n] = _adamw_layers(params[n], layer_grads[n], mom1[n], mom2[n],
                                                                name=f"adamw_{n}")
    two = lambda t: t.reshape(-1, t.shape[-1])
    res = _adamw_small([(two(params[n]), two(grads[n]), two(mom1[n]), two(mom2[n])) for n in small_names],
                       name="adamw_small")
    for n, (d, nm, nv) in zip(small_names, res):
        sh = params[n].shape
        deltas[n], new_m[n], new_v[n] = d.reshape(sh), nm.reshape(sh), nv.reshape(sh)

    return (loss, grad_x[None], *[grads[n] for n in order], *[deltas[n] for n in order],
            *[new_m[n] for n in order], *[new_v[n] for n in order])
```

```python
import functools
import math

import jax
import jax.numpy as jnp
from jax import lax
from jax.experimental import pallas as pl
from jax.experimental.pallas import tpu as pltpu

F32 = jnp.float32
MXU_DTYPE = jnp.bfloat16
WIRE_DTYPE = jnp.bfloat16
MESH = pl.DeviceIdType.MESH

D_MODEL = 1024
EPS = 1e-6
M_HEADS, M_QK, M_V, M_CHUNK = 8, 64, 128, 128
GATE_CAP = 15.0
A_DH, A_QH, A_KVH, A_G, A_BLK = 64, 16, 2, 8, 128
D_FF = 2816
N_CHIPS = 4
LANES = 128
VMEM_LIMIT = 48 * 1024 * 1024

ADAM_LR, ADAM_B1, ADAM_B2, ADAM_EPS, ADAM_WD, ADAM_STEP = 0.001, 0.9, 0.999, 1e-08, 0.01, 10


def _cparams(sem):
    return pltpu.CompilerParams(dimension_semantics=sem, vmem_limit_bytes=VMEM_LIMIT)


def _pick(n, cands):
    for c in cands:
        if n % c == 0:
            return c
    return n


class _Rows:
    def __init__(self, tiles, vecs, outs, fn):
        self.tiles, self.vecs, self.outs, self.fn = list(tiles), list(vecs), list(outs), fn


ROWS_TILE = 512


def _mm(a, b, *, ta=False, tb=False, out_dtype=F32, bias=None, add=None, b_layer=None, out_stacked=False, rows=None,
        name):
    if ta:
        K, M = a.shape
    else:
        M, K = a.shape
    b_stacked = b_layer is not None
    if b_stacked:
        assert not ta
        n_sh = b.shape[2]
        w_rows, w_cols = D_MODEL, N_CHIPS * n_sh
        N, Kb = (w_rows, w_cols) if tb else (w_cols, w_rows)
    elif tb:
        N, Kb = b.shape
    else:
        Kb, N = b.shape
    assert K == Kb, (a.shape, b.shape)
    tm = _pick(M, (1024, 1408, 512, 256, 128))
    tn = _pick(N, (1024, 1408, 1280, 512, 256, 128))
    tk = K if K <= 2816 else _pick(K, (2816, 2048, 1408, 1024, 512, 256, 128))
    shards_per_step = 1
    if b_stacked and tb:
        shards_per_step = 2
        tk = shards_per_step * n_sh
    if (b_stacked and not tb) or out_stacked:
        tn = N // N_CHIPS
    if rows is not None:
        assert tn == N and not out_stacked
        tm = min(tm, ROWS_TILE)
    nk = K // tk
    dn = (((0 if ta else 1,), (1 if tb else 0,)), ((), ()))
    has_bias, has_add = bias is not None, add is not None
    n_tiles, n_vecs, n_outs = (len(rows.tiles), len(rows.vecs), len(rows.outs)) if rows is not None else (0, 0, 1)

    def body(*refs):
        a_ref, b_ref = refs[0], refs[1]
        pos = 2
        bias_ref = add_ref = None
        if has_bias:
            bias_ref = refs[pos]
            pos += 1
        if has_add:
            add_ref = refs[pos]
            pos += 1
        tile_refs, vec_refs = refs[pos:pos + n_tiles], refs[pos + n_tiles:pos + n_tiles + n_vecs]
        pos += n_tiles + n_vecs
        out_refs = refs[pos:pos + n_outs]
        acc_ref = refs[pos + n_outs] if nk > 1 else None

        def finish(r):
            if has_bias:
                r = r + bias_ref[...]
            if has_add:
                r = r + add_ref[...]
            if rows is None:
                out_refs[0][...] = r.astype(out_dtype)
                return
            vals = rows.fn(r, [t[...] for t in tile_refs], [v[...] for v in vec_refs])
            first = pl.program_id(0) == 0
            for (kind, _), o_ref, val in zip(rows.outs, out_refs, vals):
                if kind == "tile":
                    o_ref[...] = val.astype(o_ref.dtype)
                else:
                    @pl.when(first)
                    def _(o_ref=o_ref, val=val):
                        o_ref[...] = val

                    @pl.when(jnp.logical_not(first))
                    def _(o_ref=o_ref, val=val):
                        o_ref[...] += val

        if shards_per_step > 1:
            part = sum(lax.dot_general(a_ref[:, s * n_sh:(s + 1) * n_sh], b_ref[s], dn, preferred_element_type=F32)
                       for s in range(shards_per_step))
        else:
            part = lax.dot_general(a_ref[...], b_ref[...], dn, preferred_element_type=F32)
        if nk == 1:
            finish(part)
        else:
            k = pl.program_id(2)

            @pl.when(k == 0)
            def _():
                acc_ref[...] = part

            @pl.when(k > 0)
            def _():
                acc_ref[...] += part

            @pl.when(k == nk - 1)
            def _():
                finish(acc_ref[...])

    a_spec = pl.BlockSpec((tk, tm), lambda i, j, k: (k, i)) if ta else pl.BlockSpec((tm, tk), lambda i, j, k: (i, k))
    if b_stacked and tb:
        off = b_layer * (w_rows // tn)
        b_spec = pl.BlockSpec((shards_per_step, tn, n_sh), lambda i, j, k: (k, off + j, 0))
    elif b_stacked:
        off = b_layer * (w_rows // tk)
        b_spec = pl.BlockSpec((None, tk, tn), lambda i, j, k: (j, off + k, 0))
    elif tb:
        b_spec = pl.BlockSpec((tn, tk), lambda i, j, k: (j, k))
    else:
        b_spec = pl.BlockSpec((tk, tn), lambda i, j, k: (k, j))
    if out_stacked:
        out_spec = pl.BlockSpec((None, tm, tn), lambda i, j, k: (j, i, 0))
        out_shape = jax.ShapeDtypeStruct((N_CHIPS, M, tn), out_dtype)
    else:
        out_spec = pl.BlockSpec((tm, tn), lambda i, j, k: (i, j))
        out_shape = jax.ShapeDtypeStruct((M, N), out_dtype)
    in_specs, args = [a_spec, b_spec], [a, b]
    if has_bias:
        in_specs.append(pl.BlockSpec((1, tn), lambda i, j, k: (0, j)))
        args.append(bias)
    if has_add:
        in_specs.append(pl.BlockSpec((tm, tn), lambda i, j, k: (i, j)))
        args.append(add)
    semantics = ("parallel", "parallel", "arbitrary")
    if rows is not None:
        tile_spec = pl.BlockSpec((tm, N), lambda i, j, k: (i, 0))
        vec_spec = pl.BlockSpec((1, N), lambda i, j, k: (0, 0))
        in_specs += [tile_spec] * n_tiles + [vec_spec] * n_vecs
        args += rows.tiles + rows.vecs
        out_spec = [tile_spec if kind == "tile" else vec_spec for kind, _ in rows.outs]
        out_shape = [jax.ShapeDtypeStruct((M, N) if kind == "tile" else (1, N), dt) for kind, dt in rows.outs]
        semantics = ("arbitrary", "arbitrary", "arbitrary")
    return pl.pallas_call(
        body, name=name,
        grid=(M // tm, N // tn, nk),
        in_specs=in_specs,
        out_specs=out_spec,
        out_shape=out_shape,
        scratch_shapes=[pltpu.VMEM((tm, tn), F32)] if nk > 1 else [],
        compiler_params=_cparams(semantics),
    )(*args)


ROW_TILE = 512


def _rms_fwd(x, g, *, out_dtype, name):
    S, D = x.shape
    tm = _pick(S, (ROW_TILE, 256, 128))

    def body(x_ref, g_ref, o_ref):
        o_ref[...] = _rms(x_ref[...], g_ref[...]).astype(out_dtype)

    row = pl.BlockSpec((tm, D), lambda i: (i, 0))
    return pl.pallas_call(
        body, name=name, grid=(S // tm,), in_specs=[row, pl.BlockSpec((1, D), lambda i: (0, 0))], out_specs=row,
        out_shape=jax.ShapeDtypeStruct((S, D), out_dtype),
        compiler_params=_cparams(("parallel",)),
    )(x, g)


def _rms(x, g):
    return x * lax.rsqrt(jnp.mean(x * x, axis=-1, keepdims=True) + EPS) * g


def _rms_vjp(x, g, dy):
    r = lax.rsqrt(jnp.mean(x * x, axis=-1, keepdims=True) + EPS)
    xh = x * r
    gd = dy * g
    dx = r * (gd - xh * jnp.mean(gd * xh, axis=-1, keepdims=True))
    return dx, jnp.sum(dy * xh, axis=0, keepdims=True)


def _rows_sublayer_end(res, g_post, g_pre_next):
    def fn(z, tiles, vecs):
        h = tiles[0] + _rms(z, vecs[0])
        return [z, h, _rms(h, vecs[1])]

    return _Rows([res], [g_post, g_pre_next], [("tile", F32), ("tile", F32), ("tile", MXU_DTYPE)], fn)


def _rows_last_sublayer_end(res, g_post, target):
    def fn(z, tiles, vecs):
        y = tiles[0] + _rms(z, vecs[0])
        err = y - tiles[1]
        dy = err * (1.0 / err.shape[-1])
        dz, dg = _rms_vjp(z, vecs[0], dy)
        loss = 0.5 * jnp.sum(jnp.mean(err * err, axis=-1, keepdims=True), axis=0, keepdims=True)
        return [z, dy, dz, dg, jnp.broadcast_to(loss, dg.shape)]

    return _Rows([res, target], [g_post], [("tile", F32), ("tile", F32), ("tile", MXU_DTYPE), ("vec", F32), ("vec", F32)], fn)


def _rows_sublayer_start_bwd(x, res, z_below, g_pre, g_post_below):
    def fn(dz, tiles, vecs):
        dx, dg_pre = _rms_vjp(tiles[0], vecs[0], dz)
        dh = tiles[1] + dx
        dzb, dg_post = _rms_vjp(tiles[2], vecs[1], dh)
        return [dh, dg_pre, dzb, dg_post, jnp.sum(dzb, axis=0, keepdims=True)]

    return _Rows([x, res, z_below], [g_pre, g_post_below],
                 [("tile", F32), ("vec", F32), ("tile", MXU_DTYPE), ("vec", F32), ("vec", F32)], fn)


def _rows_first_sublayer_start_bwd(x, res, g_pre):
    def fn(dz, tiles, vecs):
        dx, dg_pre = _rms_vjp(tiles[0], vecs[0], dz)
        return [tiles[1] + dx, dg_pre]

    return _Rows([x, res], [g_pre], [("tile", F32), ("vec", F32)], fn)


def _mx(t):
    return t.astype(MXU_DTYPE)


def _mxf(t):
    return t.astype(MXU_DTYPE).astype(F32)


def _rawdot(a, b, ca, cb):
    return lax.dot_general(_mx(a), _mx(b), (((ca,), (cb,)), ((), ())), preferred_element_type=F32)


@functools.partial(jax.custom_vjp, nondiff_argnums=(2, 3))
def _dot(a, b, ca, cb):
    return _rawdot(a, b, ca, cb)


def _dot_fwd(a, b, ca, cb):
    return _rawdot(a, b, ca, cb), (a, b)


def _dot_bwd(ca, cb, res, g):
    a, b = res
    bj = 1 if cb == 0 else 0
    ai = 0 if ca == 1 else 1
    da = _rawdot(g, b, 1, bj) if ca == 1 else _rawdot(b, g, bj, 1)
    db = _rawdot(a, g, ai, 0) if cb == 0 else _rawdot(g, a, 0, ai)
    return da.astype(a.dtype), db.astype(b.dtype)


_dot.defvjp(_dot_fwd, _dot_bwd)


def _softcap(z):
    return GATE_CAP * jnp.tanh(z / GATE_CAP)


def _log_sigmoid(z):
    return jnp.minimum(z, 0.0) - jnp.log(1.0 + jnp.exp(-jnp.abs(z)))


def _sigmoid(z):
    return 0.5 * jnp.tanh(0.5 * z) + 0.5


def _lane_col(t, lane_index):
    lane = lax.broadcasted_iota(jnp.int32, t.shape, 1)
    return jnp.sum(jnp.where(lane == lane_index, t, 0.0), axis=1, keepdims=True)


def _mlstm_gates(G, bias):
    L = G.shape[0]
    z = _softcap(G + bias)
    ig = z
    lf = _log_sigmoid(z)
    ti = lax.broadcasted_iota(jnp.int32, (L, L), 0)
    si = lax.broadcasted_iota(jnp.int32, (L, L), 1)
    tril = (si <= ti).astype(F32)
    b = lax.dot_general(tril, lf, (((1,), (0,)), ((), ())), precision=lax.Precision.HIGHEST, preferred_element_type=F32)
    bL = jnp.sum(lf, axis=0, keepdims=True)
    return ig, b, ig.T, b.T, bL


def _mlstm_head(h, q, k, v, o, ig_all, b_all, igT, bT, bL_all, hn, C, n, m):
    L = q.shape[0]
    ti = lax.broadcasted_iota(jnp.int32, (L, L), 0)
    si = lax.broadcasted_iota(jnp.int32, (L, L), 1)
    lower = (si <= ti)
    ig = _lane_col(ig_all, h)
    b = _lane_col(b_all, M_HEADS + h)
    ig_row = igT[h:h + 1, :]
    b_row = bT[M_HEADS + h:M_HEADS + h + 1, :]
    bL = _lane_col(bL_all, M_HEADS + h)
    inter = b + m
    dlog = jnp.where(lower, b - b_row + ig_row, -jnp.inf)
    m_t = lax.stop_gradient(jnp.maximum(inter, jnp.max(dlog, axis=-1, keepdims=True)))
    qs = q * (M_QK ** -0.5)
    w = _dot(qs, k, 1, 1) * jnp.exp(dlog - m_t)
    s_inter = jnp.exp(inter - m_t)
    num = _dot(w, v, 1, 0) + s_inter * _dot(qs, C, 1, 0)
    den = jnp.sum(w, axis=-1, keepdims=True) + s_inter * jnp.sum(_mxf(qs) * _mxf(n), axis=-1, keepdims=True)
    hout = num * (1.0 / jnp.maximum(jnp.abs(den), jnp.exp(-m_t)))
    tail = bL - b + ig
    m_new = lax.stop_gradient(jnp.maximum(bL + m, jnp.max(tail, axis=0, keepdims=True)))
    ws = jnp.exp(tail - m_new)
    decay = jnp.exp(bL + m - m_new)
    wk = ws * k
    C_new = decay * C + _dot(wk, v, 0, 0)
    n_new = decay * n + jnp.sum(_mxf(ws) * _mxf(k), axis=0, keepdims=True)
    hs = hout * lax.rsqrt(jnp.mean(hout * hout, axis=-1, keepdims=True) + EPS) * hn
    gated = _sigmoid(o) * hs
    return (gated, C_new, n_new), m_new


M_OFF_Q, M_OFF_K, M_OFF_V, M_OFF_O, M_OFF_G = 0, M_HEADS * LANES, 2 * M_HEADS * LANES, 3 * M_HEADS * LANES, 4 * M_HEADS * LANES
M_PROJ = M_OFF_G + LANES


def _head_cols(off, h):
    return slice(off + h * LANES, off + (h + 1) * LANES)


def _mlstm_specs(NC, rev):
    H, L = M_HEADS, M_CHUNK
    cc = (lambda c: NC - 1 - c) if rev else (lambda c: c)
    proj = pl.BlockSpec((L, M_PROJ), lambda c: (cc(c), 0))
    vec = pl.BlockSpec((1, LANES), lambda c: (0, 0))
    hn = pl.BlockSpec((1, H * M_V), lambda c: (0, 0))
    hv = pl.BlockSpec((L, H * M_V), lambda c: (cc(c), 0))
    Cs = pl.BlockSpec((None, H, LANES, M_V), lambda c: (cc(c), 0, 0, 0))
    ns = pl.BlockSpec((None, H, 1, LANES), lambda c: (cc(c), 0, 0, 0))
    ms = pl.BlockSpec((None, H, 1, 1), lambda c: (cc(c), 0, 0, 0))
    return proj, vec, hn, hv, Cs, ns, ms


_MLSTM_STATE = [pltpu.VMEM((M_HEADS, LANES, M_V), F32), pltpu.VMEM((M_HEADS, 1, LANES), F32),
                pltpu.VMEM((M_HEADS, 1, 1), F32)]


def _mlstm_fwd(proj, bias, hn, shards, *, name):
    S = proj.shape[0]
    H, NC = M_HEADS, S // M_CHUNK
    ps, vec, hns, hv, Cs, ns, ms = _mlstm_specs(NC, False)
    nsh = len(shards)
    any_spec = pl.BlockSpec(memory_space=pl.ANY)

    def body(p_ref, b_ref, hn_ref, *rest):
        gated_ref, C_all, n_all, m_all = rest[nsh:nsh + 4]
        C_s, n_s, m_s, send_sems, recv_sems = rest[2 * nsh + 4:]
        gather = _Gather(rest[:nsh], rest[nsh + 4:2 * nsh + 4], send_sems, recv_sems)

        @pl.when(pl.program_id(0) == 0)
        def _():
            C_s[...] = jnp.zeros_like(C_s)
            n_s[...] = jnp.zeros_like(n_s)
            m_s[...] = jnp.zeros_like(m_s)
            gather.issue()

        gate_terms = _mlstm_gates(p_ref[:, M_OFF_G:M_OFF_G + LANES], b_ref[...])
        for h in range(H):
            C, n, m = C_s[h], n_s[h], m_s[h]
            C_all[h] = C
            n_all[h] = n
            m_all[h] = m
            (gated, Cn, nn), mn = _mlstm_head(
                h, p_ref[:, _head_cols(M_OFF_Q, h)], p_ref[:, _head_cols(M_OFF_K, h)], p_ref[:, _head_cols(M_OFF_V, h)],
                p_ref[:, _head_cols(M_OFF_O, h)], *gate_terms, hn_ref[:, _head_cols(0, h)], C, n, m)
            gated_ref[:, _head_cols(0, h)] = gated.astype(gated_ref.dtype)
            C_s[h] = Cn
            n_s[h] = nn
            m_s[h] = mn

        @pl.when(pl.program_id(0) == NC - 1)
        def _():
            gather.finish()

    return pl.pallas_call(
        body, name=name, grid=(NC,),
        in_specs=[ps, vec, hns] + [any_spec] * nsh,
        out_specs=[hv, Cs, ns, ms] + [any_spec] * nsh,
        out_shape=[jax.ShapeDtypeStruct((S, H * M_V), MXU_DTYPE),
                   jax.ShapeDtypeStruct((NC, H, LANES, M_V), F32),
                   jax.ShapeDtypeStruct((NC, H, 1, LANES), F32),
                   jax.ShapeDtypeStruct((NC, H, 1, 1), F32)] + _Gather.out_shape(shards),
        scratch_shapes=list(_MLSTM_STATE) + _Gather.semaphores(nsh),
        compiler_params=_cparams(("arbitrary",)),
    )(proj, bias, hn, *shards)


def _mlstm_bwd(proj, bias, hn, C_all, n_all, m_all, dgated, qs, *, name):
    S = proj.shape[0]
    H, NC = M_HEADS, S // M_CHUNK
    ps, vec, hns, hv, Cs, ns, ms = _mlstm_specs(NC, True)
    nq = len(qs)
    any_spec = pl.BlockSpec(memory_space=pl.ANY)

    def body(p_ref, b_ref, hn_ref, C_ref, n_ref, m_ref, dg_ref, *rest):
        dp_ref, db_ref, dhn_ref = rest[nq:nq + 3]
        dC_s, dn_s, send_sems, recv_sems = rest[2 * nq + 3:]
        exchange = _Exchange(rest[:nq], rest[nq + 3:2 * nq + 3], send_sems, recv_sems)

        @pl.when(pl.program_id(0) == 0)
        def _():
            dC_s[...] = jnp.zeros_like(dC_s)
            dn_s[...] = jnp.zeros_like(dn_s)
            db_ref[...] = jnp.zeros_like(db_ref)
            dhn_ref[...] = jnp.zeros_like(dhn_ref)
            exchange.issue()

        gate_terms, gates_vjp = jax.vjp(_mlstm_gates, p_ref[:, M_OFF_G:M_OFF_G + LANES], b_ref[...])
        d_terms = [jnp.zeros_like(t) for t in gate_terms]
        for h in range(H):
            def head(q, k, v, o, *rest, h=h):
                return _mlstm_head(h, q, k, v, o, *rest, m_ref[h])

            prim = (p_ref[:, _head_cols(M_OFF_Q, h)], p_ref[:, _head_cols(M_OFF_K, h)], p_ref[:, _head_cols(M_OFF_V, h)],
                    p_ref[:, _head_cols(M_OFF_O, h)], *gate_terms, hn_ref[:, _head_cols(0, h)], C_ref[h], n_ref[h])
            _, vjp, _ = jax.vjp(head, *prim, has_aux=True)
            dq, dk, dv, do, *d_gate, dhnh, dC, dn = vjp((dg_ref[:, _head_cols(0, h)].astype(F32), dC_s[h], dn_s[h]))
            dp_ref[:, _head_cols(M_OFF_Q, h)] = dq.astype(dp_ref.dtype)
            dp_ref[:, _head_cols(M_OFF_K, h)] = dk.astype(dp_ref.dtype)
            dp_ref[:, _head_cols(M_OFF_V, h)] = dv.astype(dp_ref.dtype)
            dp_ref[:, _head_cols(M_OFF_O, h)] = do.astype(dp_ref.dtype)
            d_terms = [a + g for a, g in zip(d_terms, d_gate)]
            dhn_ref[:, _head_cols(0, h)] += dhnh
            dC_s[h] = dC
            dn_s[h] = dn
        dG, dbias = gates_vjp(tuple(d_terms))
        dp_ref[:, M_OFF_G:M_OFF_G + LANES] = dG.astype(dp_ref.dtype)
        db_ref[...] += dbias

        @pl.when(pl.program_id(0) == NC - 1)
        def _():
            exchange.finish()

    return pl.pallas_call(
        body, name=name, grid=(NC,),
        in_specs=[ps, vec, hns, Cs, ns, ms, hv] + [any_spec] * nq,
        out_specs=[ps, vec, hns] + [any_spec] * nq,
        out_shape=[jax.ShapeDtypeStruct((S, M_PROJ), MXU_DTYPE), jax.ShapeDtypeStruct((1, LANES), F32),
                   jax.ShapeDtypeStruct((1, H * M_V), F32)] + _Exchange.out_shape(qs),
        scratch_shapes=list(_MLSTM_STATE[:2]) + _Exchange.semaphores(nq),
        compiler_params=_cparams(("arbitrary",)),
    )(proj, bias, hn, C_all, n_all, m_all, dgated, *qs)


A_NQ = A_QH * A_DH
A_NKV = 2 * A_KVH * A_DH
A_PAIRS = A_G // 2


def _attn_group(Ps, KLO, KHI, VLO, VHI, sinks, first):
    B = Ps[0].shape[0]
    R = len(Ps) * B
    q2 = jnp.concatenate(Ps, axis=0) * (A_DH ** -0.5)
    s = jnp.concatenate([_dot(q2, KLO, 1, 1), _dot(q2, KHI, 1, 1)], axis=0)
    qi = lax.broadcasted_iota(jnp.int32, (2 * R, 2 * B), 0) & (B - 1)
    ku = lax.broadcasted_iota(jnp.int32, (2 * R, 2 * B), 1)
    diff = qi - (ku - B)
    mask = (diff >= 0) & (diff < B) & ((ku >= B) | jnp.logical_not(first))
    s = jnp.where(mask, s, -jnp.inf)
    ri = lax.broadcasted_iota(jnp.int32, (2 * R, A_G), 0)
    head = 2 * lax.shift_right_logical(ri & (R - 1), B.bit_length() - 1) + lax.shift_right_logical(ri, R.bit_length() - 1)
    onehot = head == lax.broadcasted_iota(jnp.int32, (2 * R, A_G), 1)
    sink = jnp.sum(jnp.where(onehot, sinks, 0.0), axis=1, keepdims=True)
    mx = lax.stop_gradient(jnp.maximum(jnp.max(s, axis=-1, keepdims=True), sink))
    p = jnp.exp(s - mx)
    p = p * (1.0 / (jnp.sum(p, axis=-1, keepdims=True) + jnp.exp(sink - mx)))
    o = _dot(p[:R], VLO, 1, 0) + _dot(p[R:], VHI, 1, 0)
    return tuple(o[j * B:(j + 1) * B] for j in range(len(Ps)))


def _swap_halves_of_lanes(t):
    return pltpu.roll(t, LANES // 2, 1)


def _kv_operands(kvp_ref, kvc_ref, h):
    kk = jnp.concatenate([kvp_ref[:, :LANES], kvc_ref[:, :LANES]], axis=0)
    vv = jnp.concatenate([kvp_ref[:, LANES:], kvc_ref[:, LANES:]], axis=0)
    low = lax.broadcasted_iota(jnp.int32, kk.shape, 1) < A_DH
    own = low if h == 0 else jnp.logical_not(low)
    k_own = jnp.where(own, kk, 0.0)
    v_own = jnp.where(own, vv, 0.0)
    k_oth, v_oth = _swap_halves_of_lanes(k_own), _swap_halves_of_lanes(v_own)
    if h == 0:
        return own, k_own, k_oth, v_own, v_oth
    return own, k_oth, k_own, v_oth, v_own


def _pair_cols(h, j):
    c = (h * A_PAIRS + j) * LANES
    return slice(c, c + LANES)


def _attn_fwd(proj, sinks, *, name):
    S = proj.shape[0]
    NB = S // A_BLK
    kv_blk = A_NQ // A_NKV
    qs = pl.BlockSpec((A_BLK, A_NQ), lambda n: (n, 0))
    cur = pl.BlockSpec((A_BLK, A_NKV), lambda n: (n, kv_blk))
    prev = pl.BlockSpec((A_BLK, A_NKV), lambda n: (jnp.maximum(n - 1, 0), kv_blk))
    sk = pl.BlockSpec((A_KVH, A_G), lambda n: (0, 0))

    def body(q_ref, kvp_ref, kvc_ref, s_ref, o_ref):
        first = pl.program_id(0) == 0
        for h in range(A_KVH):
            _, KLO, KHI, VLO, VHI = _kv_operands(kvp_ref, kvc_ref, h)
            Ps = tuple(q_ref[:, _pair_cols(h, j)] for j in range(A_PAIRS))
            outs = _attn_group(Ps, KLO, KHI, VLO, VHI, s_ref[h:h + 1, :], first)
            for j in range(A_PAIRS):
                o_ref[:, _pair_cols(h, j)] = outs[j].astype(o_ref.dtype)

    return pl.pallas_call(
        body, name=name, grid=(NB,),
        in_specs=[qs, prev, cur, sk], out_specs=qs,
        out_shape=jax.ShapeDtypeStruct((S, A_NQ), MXU_DTYPE),
        compiler_params=_cparams(("parallel",)),
    )(proj, proj, proj, sinks)


def _attn_bwd(proj, sinks, do, *, name):
    S = proj.shape[0]
    NB = S // A_BLK
    last = NB - 1
    kv_blk = A_NQ // A_NKV
    qs = pl.BlockSpec((A_BLK, A_NQ), lambda n: (jnp.minimum(n, last), 0))
    cur = pl.BlockSpec((A_BLK, A_NKV), lambda n: (jnp.minimum(n, last), kv_blk))
    prev = pl.BlockSpec((A_BLK, A_NKV), lambda n: (jnp.clip(n - 1, 0, last), kv_blk))
    sk = pl.BlockSpec((A_KVH, A_G), lambda n: (0, 0))
    lag = pl.BlockSpec((A_BLK, A_NKV), lambda n: (jnp.maximum(n - 1, 0), 0))
    cq_spec = pl.BlockSpec((1, A_NQ), lambda n: (0, 0))
    ckv_spec = pl.BlockSpec((1, A_NKV), lambda n: (0, 0))

    def body(q_ref, kvp_ref, kvc_ref, s_ref, do_ref, dq_ref, dkv_ref, ds_ref, cq_ref, ckv_ref, keep):
        n = pl.program_id(0)

        @pl.when(n == 0)
        def _():
            keep[...] = jnp.zeros_like(keep)
            ds_ref[...] = jnp.zeros_like(ds_ref)
            cq_ref[...] = jnp.zeros_like(cq_ref)
            ckv_ref[...] = jnp.zeros_like(ckv_ref)

        @pl.when(n < NB)
        def _():
            f = functools.partial(_attn_group, first=n == 0)
            dk = jnp.zeros((2 * A_BLK, LANES), F32)
            dv = jnp.zeros((2 * A_BLK, LANES), F32)
            ds_rows = []
            for h in range(A_KVH):
                own, KLO, KHI, VLO, VHI = _kv_operands(kvp_ref, kvc_ref, h)
                Ps = tuple(q_ref[:, _pair_cols(h, j)] for j in range(A_PAIRS))
                _, vjp = jax.vjp(f, Ps, KLO, KHI, VLO, VHI, s_ref[h:h + 1, :])
                dPs, dKLO, dKHI, dVLO, dVHI, dsk = vjp(
                    tuple(do_ref[:, _pair_cols(h, j)].astype(F32) for j in range(A_PAIRS)))
                for j in range(A_PAIRS):
                    dq_ref[:, _pair_cols(h, j)] = dPs[j].astype(dq_ref.dtype)
                    cq_ref[:, _pair_cols(h, j)] += jnp.sum(dPs[j], axis=0, keepdims=True)
                dk_own, dk_oth = (dKLO, dKHI) if h == 0 else (dKHI, dKLO)
                dv_own, dv_oth = (dVLO, dVHI) if h == 0 else (dVHI, dVLO)
                dk = dk + jnp.where(own, dk_own, 0.0) + _swap_halves_of_lanes(jnp.where(own, 0.0, dk_oth))
                dv = dv + jnp.where(own, dv_own, 0.0) + _swap_halves_of_lanes(jnp.where(own, 0.0, dv_oth))
                ds_rows.append(dsk)
            ds_ref[...] += jnp.concatenate(ds_rows, axis=0)
            dkv = jnp.concatenate([dk, dv], axis=1)
            done = keep[...] + dkv[:A_BLK]
            dkv_ref[...] = done.astype(dkv_ref.dtype)
            ckv_ref[...] += jnp.sum(done, axis=0, keepdims=True)
            keep[...] = dkv[A_BLK:]

        @pl.when(n == NB)
        def _():
            done = keep[...]
            dkv_ref[...] = done.astype(dkv_ref.dtype)
            ckv_ref[...] += jnp.sum(done, axis=0, keepdims=True)

    return pl.pallas_call(
        body, name=name, grid=(NB + 1,),
        in_specs=[qs, prev, cur, sk, qs],
        out_specs=[qs, lag, sk, cq_spec, ckv_spec],
        out_shape=[jax.ShapeDtypeStruct((S, A_NQ), MXU_DTYPE), jax.ShapeDtypeStruct((S, A_NKV), MXU_DTYPE),
                   jax.ShapeDtypeStruct((A_KVH, A_G), F32), jax.ShapeDtypeStruct((1, A_NQ), F32),
                   jax.ShapeDtypeStruct((1, A_NKV), F32)],
        scratch_shapes=[pltpu.VMEM((A_BLK, A_NKV), F32)],
        compiler_params=_cparams(("arbitrary",)),
    )(proj, proj, proj, sinks, do)


HALO = 8


def _shift_rows(t, j):
    if j == 0:
        return t
    return pltpu.roll(t, j % t.shape[0], 0)


def _conv_gate(gate_ext, cw, cb):
    return cb + cw[0:1, :] * _shift_rows(gate_ext, 2) + cw[1:2, :] * _shift_rows(gate_ext, 1) + cw[2:3, :] * gate_ext


def _convgate_fwd(u, cw, cb, *, name):
    S, F2 = u.shape
    F = F2 // 2
    tm = _pick(S, (256, 128))
    hb = tm // HALO
    urow = pl.BlockSpec((tm, F2), lambda i: (i, 0))
    uprev = pl.BlockSpec((HALO, F), lambda i: (jnp.maximum(i * hb - 1, 0), 0))

    def body(u_ref, up_ref, cw_ref, cb_ref, a_ref):
        i = pl.program_id(0)
        gate = u_ref[:, :F]
        val = u_ref[:, F:]
        prev = jnp.where(i > 0, up_ref[...], 0.0)
        gc = _conv_gate(jnp.concatenate([prev, gate], axis=0), cw_ref[...], cb_ref[...])[HALO:]
        a_ref[...] = (gc * _sigmoid(gc) * val).astype(a_ref.dtype)

    return pl.pallas_call(
        body, name=name, grid=(S // tm,),
        in_specs=[urow, uprev, pl.BlockSpec((3, F), lambda i: (0, 0)), pl.BlockSpec((1, F), lambda i: (0, 0))],
        out_specs=pl.BlockSpec((tm, F), lambda i: (i, 0)),
        out_shape=jax.ShapeDtypeStruct((S, F), MXU_DTYPE),
        compiler_params=_cparams(("parallel",)),
    )(u, u, cw, cb)


def _convgate_bwd(u, da, cw, cb, *, name):
    S, F2 = u.shape
    F = F2 // 2
    tm = _pick(S, (128,))
    hb = tm // HALO
    nt = S // tm
    nh = S // HALO
    urow = pl.BlockSpec((tm, F2), lambda i: (i, 0))
    uprev = pl.BlockSpec((HALO, F), lambda i: (jnp.maximum(i * hb - 1, 0), 0))
    unext = pl.BlockSpec((HALO, F2), lambda i: (jnp.minimum((i + 1) * hb, nh - 1), 0))
    darow = pl.BlockSpec((tm, F), lambda i: (i, 0))
    danext = pl.BlockSpec((HALO, F), lambda i: (jnp.minimum((i + 1) * hb, nh - 1), 0))

    def body(u_ref, up_ref, un_ref, da_ref, dan_ref, cw_ref, cb_ref, du_ref, dcw_ref, dcb_ref):
        i = pl.program_id(0)
        cwv = cw_ref[...]
        prev = jnp.where(i > 0, up_ref[...], 0.0)
        gate_ext = jnp.concatenate([prev, u_ref[:, :F], un_ref[:, :F]], axis=0)
        val_ext = jnp.concatenate([u_ref[:, F:], un_ref[:, F:]], axis=0)
        da_next = jnp.where(i < nt - 1, dan_ref[...].astype(F32), 0.0)
        da_ext = jnp.concatenate([da_ref[...].astype(F32), da_next], axis=0)
        gc = _conv_gate(gate_ext, cwv, cb_ref[...])[HALO:]
        sg = _sigmoid(gc)
        silu = gc * sg
        dval = da_ext * silu
        dgc = da_ext * val_ext * (sg * (1.0 + gc * (1.0 - sg)))
        dgate = cwv[2:3, :] * dgc + cwv[1:2, :] * _shift_rows(dgc, -1) + cwv[0:1, :] * _shift_rows(dgc, -2)
        du_ref[:, :F] = dgate[:tm].astype(du_ref.dtype)
        du_ref[:, F:] = dval[:tm].astype(du_ref.dtype)
        dgc_c = dgc[:tm]
        g0 = gate_ext[HALO:HALO + tm]
        g1 = _shift_rows(gate_ext, 1)[HALO:HALO + tm]
        g2 = _shift_rows(gate_ext, 2)[HALO:HALO + tm]
        dcw = jnp.concatenate([jnp.sum(dgc_c * g2, axis=0, keepdims=True),
                               jnp.sum(dgc_c * g1, axis=0, keepdims=True),
                               jnp.sum(dgc_c * g0, axis=0, keepdims=True)], axis=0)
        dcb = jnp.sum(dgc_c, axis=0, keepdims=True)

        @pl.when(i == 0)
        def _():
            dcw_ref[...] = dcw
            dcb_ref[...] = dcb

        @pl.when(i > 0)
        def _():
            dcw_ref[...] += dcw
            dcb_ref[...] += dcb

    return pl.pallas_call(
        body, name=name, grid=(nt,),
        in_specs=[urow, uprev, unext, darow, danext,
                  pl.BlockSpec((3, F), lambda i: (0, 0)), pl.BlockSpec((1, F), lambda i: (0, 0))],
        out_specs=[urow, pl.BlockSpec((3, F), lambda i: (0, 0)), pl.BlockSpec((1, F), lambda i: (0, 0))],
        out_shape=[jax.ShapeDtypeStruct((S, F2), MXU_DTYPE), jax.ShapeDtypeStruct((3, F), F32),
                   jax.ShapeDtypeStruct((1, F), F32)],
        compiler_params=_cparams(("arbitrary",)),
    )(u, u, u, da, da, cw, cb)


def _adamw_math(w, g, m, v):
    m = ADAM_B1 * m + (1.0 - ADAM_B1) * g
    v = ADAM_B2 * v + (1.0 - ADAM_B2) * (g * g)
    m_hat = m / (1.0 - ADAM_B1 ** ADAM_STEP)
    v_hat = v / (1.0 - ADAM_B2 ** ADAM_STEP)
    delta = -ADAM_LR * (m_hat / (jnp.sqrt(v_hat) + ADAM_EPS) + ADAM_WD * w)
    return delta, m, v


def _adamw_layers(w, gs, m, v, *, name):
    Lr, R, C = w.shape
    tr = _pick(R, (256, 128, 64, 32, 16, 8))
    outs = None
    for layer, g in enumerate(gs):
        def body(w_ref, g_ref, m_ref, v_ref, *rest):
            go_ref, d_ref, nm_ref, nv_ref = rest[-4:]
            gv = g_ref[...]
            d, nm, nv = _adamw_math(w_ref[...], gv, m_ref[...], v_ref[...])
            go_ref[...] = gv
            d_ref[...] = d
            nm_ref[...] = nm
            nv_ref[...] = nv

        lay = pl.BlockSpec((None, tr, C), lambda i, layer=layer: (layer, i, 0))
        in_specs = [lay, pl.BlockSpec((tr, C), lambda i: (i, 0)), lay, lay]
        args = [w, g, m, v]
        aliases = {}
        if outs is not None:
            in_specs += [pl.BlockSpec(memory_space=pl.ANY)] * 4
            args += list(outs)
            aliases = {4 + t: t for t in range(4)}
        outs = pl.pallas_call(
            body, name=f"{name}_{layer}", grid=(R // tr,), in_specs=in_specs, out_specs=[lay] * 4,
            out_shape=[jax.ShapeDtypeStruct((Lr, R, C), F32)] * 4, input_output_aliases=aliases,
            compiler_params=_cparams(("parallel",)),
        )(*args)
    return outs


def _adamw_small(items, *, name):
    n = len(items)

    def body(*refs):
        ins, outs = refs[:4 * n], refs[4 * n:]
        for t in range(n):
            w, g, m, v = (r[...] for r in ins[4 * t:4 * t + 4])
            d, nm, nv = _adamw_math(w, g, m, v)
            outs[3 * t][...] = d
            outs[3 * t + 1][...] = nm
            outs[3 * t + 2][...] = nv

    flat = [a for it in items for a in it]
    out_shape = [jax.ShapeDtypeStruct(it[0].shape, F32) for it in items for _ in range(3)]
    vm = pl.BlockSpec(memory_space=pltpu.VMEM)
    res = pl.pallas_call(body, name=name, in_specs=[vm] * len(flat), out_specs=[vm] * len(out_shape),
                         out_shape=out_shape)(*flat)
    return [tuple(res[3 * t:3 * t + 3]) for t in range(n)]


def _place():
    return lax.axis_index("x"), lax.axis_index("y"), lax.axis_index("c")


_FLIPS = ((1, 0), (0, 1), (1, 1))


ROW_ALIGN = 16


def _half(rows, which):
    return pl.ds(pl.multiple_of(which * (rows // 2), ROW_ALIGN), rows // 2)


def _remote(src, dst, send_sems, recv_sems, k, to):
    return pltpu.make_async_remote_copy(src_ref=src, dst_ref=dst, send_sem=send_sems.at[k], recv_sem=recv_sems.at[k],
                                        device_id=to, device_id_type=MESH)


def _gather_shards(shards, *, name):
    n = len(shards)
    any_spec = pl.BlockSpec(memory_space=pl.ANY)

    def body(*refs):
        gather = _Gather(refs[:n], refs[n:2 * n], refs[2 * n], refs[2 * n + 1])
        gather.issue()
        gather.finish()

    return pl.pallas_call(
        body, name=name, in_specs=[any_spec] * n, out_specs=[any_spec] * n,
        out_shape=_Gather.out_shape(shards), scratch_shapes=_Gather.semaphores(n),
    )(*shards)


class _Gather:
    def __init__(self, w_refs, out_refs, send_sems, recv_sems):
        self.w_refs, self.out_refs, self.send_sems, self.recv_sems = w_refs, out_refs, send_sems, recv_sems
        self.pairs = [(i, j) for i in range(len(w_refs)) for j in range(3)]

    @staticmethod
    def out_shape(shards):
        return [jax.ShapeDtypeStruct((N_CHIPS,) + s.shape, s.dtype) for s in shards]

    @staticmethod
    def semaphores(n):
        return [pltpu.SemaphoreType.DMA((6 * n,)), pltpu.SemaphoreType.DMA((6 * n,))]

    def _where(self):
        x, y, c = _place()
        return x, y, c, [(x ^ fx, y ^ fy) for fx, fy in _FLIPS]

    def _over_ici(self, i, j, landed):
        x, y, c, chips = self._where()
        px, py = chips[j]
        mine = _half(self.w_refs[i].shape[0], c)
        if landed:
            src = dst = self.out_refs[i].at[2 * px + py, mine]
        else:
            src, dst = self.w_refs[i].at[mine], self.out_refs[i].at[2 * x + y, mine]
        return _remote(src, dst, self.send_sems, self.recv_sems, 6 * i + j, (px, py, c))

    def _over_d2d(self, i, j, which):
        x, y, c, chips = self._where()
        px, py = chips[j]
        blk = self.out_refs[i].at[2 * px + py, _half(self.w_refs[i].shape[0], which)]
        return _remote(blk, blk, self.send_sems, self.recv_sems, 6 * i + 3 + j, (x, y, 1 - c))

    def issue(self):
        for i, j in self.pairs:
            self._over_ici(i, j, False).start()

    def finish(self):
        c = lax.axis_index("c")
        for i, j in self.pairs:
            self._over_ici(i, j, True).wait_recv()
            self._over_d2d(i, j, c).start()
        for i, j in self.pairs:
            self._over_d2d(i, j, 1 - c).wait_recv()
        for i, j in self.pairs:
            self._over_ici(i, j, False).wait_send()
            self._over_d2d(i, j, c).wait_send()


def _swap_halves(gs, *, name):
    n = len(gs)
    any_spec = pl.BlockSpec(memory_space=pl.ANY)

    def body(*refs):
        g_refs, out_refs, send_sems, recv_sems = refs[:n], refs[n:2 * n], refs[2 * n], refs[2 * n + 1]
        x, y, c = _place()
        cps = [_remote(g_refs[i].at[:, _half(gs[i].shape[1], 1 - c)], out_refs[i], send_sems, recv_sems, i, (x, y, 1 - c))
               for i in range(n)]
        for cp in cps:
            cp.start()
        for cp in cps:
            cp.wait()

    return pl.pallas_call(
        body, name=name, in_specs=[any_spec] * n, out_specs=[any_spec] * n,
        out_shape=[jax.ShapeDtypeStruct((N_CHIPS, g.shape[1] // 2, g.shape[2]), g.dtype) for g in gs],
        scratch_shapes=[pltpu.SemaphoreType.DMA((n,)), pltpu.SemaphoreType.DMA((n,))],
    )(*gs)


def _add_halves(g, got, c_arr, *, name):
    _, rows, cols = g.shape
    blk = (None, rows // 2, cols)

    def body(c_ref, g_ref, got_ref, o_ref):
        o_ref[...] = (g_ref[...] + got_ref[...]).astype(o_ref.dtype)

    return pl.pallas_call(
        body, name=name,
        grid_spec=pltpu.PrefetchScalarGridSpec(
            num_scalar_prefetch=1, grid=(N_CHIPS,),
            in_specs=[pl.BlockSpec(blk, lambda j, c_ref: (j, c_ref[0], 0)), pl.BlockSpec(blk, lambda j, c_ref: (j, 0, 0))],
            out_specs=pl.BlockSpec(blk, lambda j, c_ref: (j, 0, 0))),
        out_shape=jax.ShapeDtypeStruct((N_CHIPS, rows // 2, cols), WIRE_DTYPE),
        compiler_params=_cparams(("parallel",)),
    )(c_arr, g, got)


def _exchange_chips(qs, *, name):
    n = len(qs)
    any_spec = pl.BlockSpec(memory_space=pl.ANY)

    def body(*refs):
        exchange = _Exchange(refs[:n], refs[n:2 * n], refs[2 * n], refs[2 * n + 1])
        exchange.issue()
        exchange.finish()

    return pl.pallas_call(
        body, name=name, in_specs=[any_spec] * n, out_specs=[any_spec] * n,
        out_shape=_Exchange.out_shape(qs), scratch_shapes=_Exchange.semaphores(n),
    )(*qs)


class _Exchange:
    def __init__(self, q_refs, out_refs, send_sems, recv_sems):
        self.q_refs, self.out_refs, self.send_sems, self.recv_sems = q_refs, out_refs, send_sems, recv_sems
        self.pairs = [(i, j) for i in range(len(q_refs)) for j in range(3)]

    @staticmethod
    def out_shape(qs):
        return [jax.ShapeDtypeStruct(q.shape, q.dtype) for q in qs]

    @staticmethod
    def semaphores(n):
        return [pltpu.SemaphoreType.DMA((3 * n,)), pltpu.SemaphoreType.DMA((3 * n,))]

    def _copy(self, i, j, landed):
        x, y, c = _place()
        px, py = [(x ^ fx, y ^ fy) for fx, fy in _FLIPS][j]
        if landed:
            src = dst = self.out_refs[i].at[2 * px + py]
        else:
            src, dst = self.q_refs[i].at[2 * px + py], self.out_refs[i].at[2 * x + y]
        return _remote(src, dst, self.send_sems, self.recv_sems, 3 * i + j, (px, py, c))

    def issue(self):
        for i, j in self.pairs:
            self._copy(i, j, False).start()

    def finish(self):
        for i, j in self.pairs:
            self._copy(i, j, True).wait_recv()
        for i, j in self.pairs:
            self._copy(i, j, False).wait_send()


def _sum_chips(q, r, place_arr, *, name):
    _, h, cols = q.shape
    blk = (None, h, cols)

    def body(p_ref, q_ref, r1_ref, r2_ref, r3_ref, o_ref):
        o_ref[...] = ((q_ref[...].astype(F32) + r1_ref[...].astype(F32)) + r2_ref[...].astype(F32)) + r3_ref[...].astype(F32)

    other = [pl.BlockSpec(blk, lambda i, p_ref, f=f: (p_ref[0] ^ f, 0, 0)) for f in (1, 2, 3)]
    return pl.pallas_call(
        body, name=name,
        grid_spec=pltpu.PrefetchScalarGridSpec(
            num_scalar_prefetch=1, grid=(1,),
            in_specs=[pl.BlockSpec(blk, lambda i, p_ref: (p_ref[0], 0, 0))] + other,
            out_specs=pl.BlockSpec((h, cols), lambda i, p_ref: (p_ref[1], 0))),
        out_shape=jax.ShapeDtypeStruct((2 * h, cols), F32),
        compiler_params=_cparams(("arbitrary",)),
    )(place_arr, q, r, r, r)


def _join_halves(fs, *, name):
    n = len(fs)
    any_spec = pl.BlockSpec(memory_space=pl.ANY)

    def body(*refs):
        out_refs, send_sems, recv_sems = refs[n:2 * n], refs[2 * n], refs[2 * n + 1]
        x, y, c = _place()
        sent = []
        for i in range(n):
            mine = out_refs[i].at[_half(fs[i].shape[0], c)]
            cp = _remote(mine, mine, send_sems, recv_sems, i, (x, y, 1 - c))
            cp.start()
            sent.append(cp)
        for i in range(n):
            its = out_refs[i].at[_half(fs[i].shape[0], 1 - c)]
            _remote(its, its, send_sems, recv_sems, i, (x, y, 1 - c)).wait_recv()
        for cp in sent:
            cp.wait_send()

    return pl.pallas_call(
        body, name=name, in_specs=[any_spec] * n, out_specs=[any_spec] * n,
        out_shape=[jax.ShapeDtypeStruct(f.shape, f.dtype) for f in fs], input_output_aliases={i: i for i in range(n)},
        scratch_shapes=[pltpu.SemaphoreType.DMA((n,)), pltpu.SemaphoreType.DMA((n,))],
    )(*fs)


def _allreduce_small(buf, *, name):
    R = buf.shape[0]
    vm = pl.BlockSpec(memory_space=pltpu.VMEM)

    def body(b_ref, o_ref, slots, send_sems, recv_sems):
        x, y, c = _place()
        me = 4 * x + 2 * y + c
        slots[me] = b_ref[...]
        sends = []
        for kk in range(1, 8):
            fx, fy, fc = (kk >> 2) & 1, (kk >> 1) & 1, kk & 1
            cp = pltpu.make_async_remote_copy(
                src_ref=b_ref, dst_ref=slots.at[me], send_sem=send_sems.at[kk - 1], recv_sem=recv_sems.at[kk - 1],
                device_id=(x ^ fx, y ^ fy, c ^ fc), device_id_type=MESH)
            cp.start()
            sends.append(cp)
        for kk in range(1, 8):
            fx, fy, fc = (kk >> 2) & 1, (kk >> 1) & 1, kk & 1
            peer = 4 * (x ^ fx) + 2 * (y ^ fy) + (c ^ fc)
            pltpu.make_async_remote_copy(
                src_ref=b_ref, dst_ref=slots.at[peer], send_sem=send_sems.at[kk - 1], recv_sem=recv_sems.at[kk - 1],
                device_id=(x ^ fx, y ^ fy, c ^ fc), device_id_type=MESH).wait_recv()
        for cp in sends:
            cp.wait_send()
        acc = slots[0]
        for d in range(1, 8):
            acc = acc + slots[d]
        o_ref[...] = acc

    return pl.pallas_call(
        body, name=name, in_specs=[vm], out_specs=vm,
        out_shape=jax.ShapeDtypeStruct((R, LANES), F32),
        scratch_shapes=[pltpu.VMEM((8, R, LANES), F32), pltpu.SemaphoreType.DMA((7,)), pltpu.SemaphoreType.DMA((7,))],
    )(buf)


def _pad_rows(v, mult=8 * LANES):
    flat = v.reshape(-1)
    n = flat.shape[0]
    tot = -(-n // mult) * mult
    return jnp.pad(flat, (0, tot - n)).reshape(-1, LANES)


def _pack_small(parts):
    return jnp.concatenate([_pad_rows(p.astype(F32)) for p in parts], axis=0)


def _unpack_small(buf, shapes):
    out, r = [], 0
    for sh in shapes:
        n = math.prod(sh)
        rows = -(-n // (8 * LANES)) * 8
        out.append(buf[r:r + rows].reshape(-1)[:n].reshape(sh))
        r += rows
    return out


def _cols_to_shards(w):
    *lead, K, N = w.shape
    t = w.reshape(*lead, K, N_CHIPS, N // N_CHIPS)
    return jnp.moveaxis(t, -2, 0)


def _shards_to_cols(t):
    t = jnp.moveaxis(t, 0, -2)
    *lead, K, _, n = t.shape
    return t.reshape(*lead, K, N_CHIPS * n)


_BIG = ("m_w_in", "m_w_out", "a_w_in", "a_w_out", "f_w_up", "f_w_down")


def kernel(x, m_w_in, m_gate_bias, m_head_norm, m_w_out, a_w_in, a_b_in, a_sinks, a_w_out, a_b_out, norm_mix_pre, norm_mix_post, norm_ffn_pre, norm_ffn_post, f_w_up, f_conv_w, f_conv_b, f_w_down, loss_target, m_m_w_in, m_m_gate_bias, m_m_head_norm, m_m_w_out, m_a_w_in, m_a_b_in, m_a_sinks, m_a_w_out, m_a_b_out, m_norm_mix_pre, m_norm_mix_post, m_norm_ffn_pre, m_norm_ffn_post, m_f_w_up, m_f_conv_w, m_f_conv_b, m_f_w_down, v_m_w_in, v_m_gate_bias, v_m_head_norm, v_m_w_out, v_a_w_in, v_a_b_in, v_a_sinks, v_a_w_out, v_a_b_out, v_norm_mix_pre, v_norm_mix_post, v_norm_ffn_pre, v_norm_ffn_post, v_f_w_up, v_f_conv_w, v_f_conv_b, v_f_w_down):
    params = dict(m_w_in=m_w_in, m_gate_bias=m_gate_bias, m_head_norm=m_head_norm, m_w_out=m_w_out, a_w_in=a_w_in,
                  a_b_in=a_b_in, a_sinks=a_sinks, a_w_out=a_w_out, a_b_out=a_b_out, norm_mix_pre=norm_mix_pre,
                  norm_mix_post=norm_mix_post, norm_ffn_pre=norm_ffn_pre, norm_ffn_post=norm_ffn_post, f_w_up=f_w_up,
                  f_conv_w=f_conv_w, f_conv_b=f_conv_b, f_w_down=f_w_down)
    mom1 = dict(m_w_in=m_m_w_in, m_gate_bias=m_m_gate_bias, m_head_norm=m_m_head_norm, m_w_out=m_m_w_out,
                a_w_in=m_a_w_in, a_b_in=m_a_b_in, a_sinks=m_a_sinks, a_w_out=m_a_w_out, a_b_out=m_a_b_out,
                norm_mix_pre=m_norm_mix_pre, norm_mix_post=m_norm_mix_post, norm_ffn_pre=m_norm_ffn_pre,
                norm_ffn_post=m_norm_ffn_post, f_w_up=m_f_w_up, f_conv_w=m_f_conv_w, f_conv_b=m_f_conv_b,
                f_w_down=m_f_w_down)
    mom2 = dict(m_w_in=v_m_w_in, m_gate_bias=v_m_gate_bias, m_head_norm=v_m_head_norm, m_w_out=v_m_w_out,
                a_w_in=v_a_w_in, a_b_in=v_a_b_in, a_sinks=v_a_sinks, a_w_out=v_a_w_out, a_b_out=v_a_b_out,
                norm_mix_pre=v_norm_mix_pre, norm_mix_post=v_norm_mix_post, norm_ffn_pre=v_norm_ffn_pre,
                norm_ffn_post=v_norm_ffn_post, f_w_up=v_f_w_up, f_conv_w=v_f_conv_w, f_conv_b=v_f_conv_b,
                f_w_down=v_f_w_down)
    order = list(params)

    mx, my, mc = _place()
    chip = 2 * mx + my
    h0 = x[0]
    target = loss_target[0]

    def two_d(t):
        return t.reshape(-1, t.shape[-1])

    early = ("m_w_in", "m_w_out")
    late = tuple(n for n in _BIG if n not in early)
    mine = {n: two_d(params[n]).astype(MXU_DTYPE) for n in _BIG}

    def with_own_slot(names, theirs):
        return {n: lax.dynamic_update_slice(t, mine[n][None], (chip, 0, 0)) for n, t in zip(names, theirs)}

    gathered = with_own_slot(early, _gather_shards([mine[n] for n in early], name="gather_weights_early"))

    def in_place(shard, axis):
        width = shard.shape[axis]
        z = jnp.zeros(shard.shape[:axis] + (N_CHIPS * width,) + shard.shape[axis + 1:], F32)
        contrib = jnp.where(mc == 0, shard, 0.0)
        return lax.dynamic_update_slice_in_dim(z, contrib, chip * width, axis)

    sm_in = [in_place(a_b_in, 1), in_place(a_b_out, 1), in_place(f_conv_w, 2)]
    sm_full = _unpack_small(_allreduce_small(_pack_small(sm_in), name="gather_small"), [t.shape for t in sm_in])
    b_in_full, b_out_full, conv_w_full = sm_full

    W_in = _shards_to_cols(gathered["m_w_in"])
    nqk = M_HEADS * M_QK
    n_main = 2 * nqk + M_HEADS * M_V + D_MODEL

    def pad_heads(w):
        return jnp.pad(w.reshape(D_MODEL, M_HEADS, M_QK), ((0, 0), (0, 0), (0, LANES - M_QK))).reshape(D_MODEL, -1)

    def unpad_heads(w):
        return w.reshape(D_MODEL, M_HEADS, LANES)[:, :, :M_QK].reshape(D_MODEL, nqk)

    W_all = jnp.concatenate([pad_heads(W_in[:, :nqk]), pad_heads(W_in[:, nqk:2 * nqk]), W_in[:, 2 * nqk:n_main],
                             jnp.pad(W_in[:, n_main:], ((0, 0), (0, LANES - 2 * M_HEADS)))], axis=1)
    gbias = jnp.pad(m_gate_bias[0].reshape(1, 2 * M_HEADS), ((0, 0), (0, LANES - 2 * M_HEADS)))
    W_mout = gathered["m_w_out"].reshape(D_MODEL, D_MODEL)

    grads = {}

    def ffn_fwd(i, z1, end):
        u = _mm(z1, W_up, b_layer=i, name=f"ffn_up{i}")
        a = _convgate_fwd(u, conv_w_full[i], f_conv_b[i:i + 1], name=f"ffn_act{i}")
        return u, a, _mm(a, W_down[i], rows=end, name=f"ffn_down{i}")

    z0 = _rms_fwd(h0, norm_mix_pre[0:1], out_dtype=MXU_DTYPE, name="mix_pre_norm0")
    proj = _mm(z0, W_all, name="mlstm_proj")
    gated, C_all, n_all, m_all, *late_theirs = _mlstm_fwd(proj, gbias, m_head_norm, [mine[n] for n in late],
                                                          name="mlstm_fwd")
    gathered.update(with_own_slot(late, late_theirs))
    A_in, A_out = _shards_to_cols(gathered["a_w_in"]), gathered["a_w_out"].reshape(D_MODEL, D_MODEL)
    W_up = gathered["f_w_up"]
    dsh = D_FF // N_CHIPS
    W_down = [gathered["f_w_down"][:, i * dsh:(i + 1) * dsh].reshape(D_FF, D_MODEL) for i in range(2)]
    zm0, h1, z1 = _mm(gated, W_mout, rows=_rows_sublayer_end(h0, norm_mix_post[0:1], norm_ffn_pre[0:1]), name="mlstm_out")
    u0, a0, (zf0, h2, y0) = ffn_fwd(0, z1, _rows_sublayer_end(h1, norm_ffn_post[0:1], norm_mix_pre[1:2]))

    aproj = _mm(y0, A_in, bias=b_in_full, name="attn_proj")
    sinks = a_sinks.reshape(A_KVH, A_G)
    ao = _attn_fwd(aproj, sinks, name="attn_fwd")
    zm1, h3, z3 = _mm(ao, A_out, bias=b_out_full, rows=_rows_sublayer_end(h2, norm_mix_post[1:2], norm_ffn_pre[1:2]),
                      name="attn_out")
    u1, a1, (zf1, dh, dzf1, g_post1, loss_lanes) = ffn_fwd(1, z3, _rows_last_sublayer_end(h3, norm_ffn_post[1:2], target))
    loss = lax.psum(loss_lanes[0, 0], ("x", "y", "c"))

    g_post, g_fpre, g_mpost, g_mpre = [None, g_post1], [None, None], [None, None], [None, None]
    dW_up, dW_down, dconv_w, dconv_b = [None, None], [None, None], [None, None], [None, None]

    def ffn_bwd(i, dzf, z1_, u, a, start):
        da = _mm(dzf, W_down[i], tb=True, name=f"ffn_down_dx{i}")
        dW_down[i] = _mm(a, dzf, ta=True, name=f"ffn_down_dw{i}")
        du, dconv_w[i], dconv_b[i] = _convgate_bwd(u, da, conv_w_full[i], f_conv_b[i:i + 1], name=f"ffn_act_bwd{i}")
        dW_up[i] = _mm(z1_, du, ta=True, out_stacked=True, name=f"ffn_up_dw{i}")
        return _mm(du, W_up, tb=True, b_layer=i, rows=start, name=f"ffn_up_dx{i}")

    dh3, g_fpre[1], dzm1, g_mpost[1], db_out = ffn_bwd(
        1, dzf1, z3, u1, a1, _rows_sublayer_start_bwd(h3, dh, zm1, norm_ffn_pre[1:2], norm_mix_post[1:2]))
    dao = _mm(dzm1, A_out, tb=True, name="attn_out_dx")
    dA_out = _mm(ao, dzm1, ta=True, name="attn_out_dw")
    daq, dakv, dsinks, cs_q, cs_kv = _attn_bwd(aproj, sinks, dao, name="attn_bwd")
    db_in = jnp.concatenate([cs_q, cs_kv], axis=1)
    dh2, g_mpre[1], dzf0, g_post[0], _ = _mm(
        daq, A_in[:, :A_NQ], tb=True, add=_mm(dakv, A_in[:, A_NQ:], tb=True, name="attn_proj_kv_dx"),
        rows=_rows_sublayer_start_bwd(h2, dh3, zf0, norm_mix_pre[1:2], norm_ffn_post[0:1]), name="attn_proj_q_dx")
    dA_in = jnp.concatenate([_mm(y0, daq, ta=True, name="attn_proj_q_dw"),
                             _mm(y0, dakv, ta=True, name="attn_proj_kv_dw")], axis=1)

    dh1, g_fpre[0], dzm0, g_mpost[0], _ = ffn_bwd(
        0, dzf0, z1, u0, a0, _rows_sublayer_start_bwd(h1, dh2, zm0, norm_ffn_pre[0:1], norm_mix_post[0:1]))
    dgated = _mm(dzm0, W_mout, tb=True, name="mlstm_out_dx")
    dW_mout = _mm(gated, dzm0, ta=True, name="mlstm_out_dw")
    c_arr = jnp.reshape(mc, (1,)).astype(jnp.int32)
    place_arr = jnp.stack([chip, mc]).astype(jnp.int32)

    def sibling_sums(tags, gs, stage):
        got = _swap_halves(gs, name=f"grad_swap_halves_{stage}")
        return [_add_halves(g, t, c_arr, name=f"grad_add_halves_{tag}") for tag, g, t in zip(tags, gs, got)]

    late_tags = ["a_w_in", "a_w_out", "f_w_up0", "f_w_up1", "f_w_down0", "f_w_down1"]
    late_part = sibling_sums(late_tags, [
        _cols_to_shards(dA_in), dA_out.reshape(N_CHIPS, -1, D_MODEL), dW_up[0], dW_up[1],
        dW_down[0].reshape(N_CHIPS, -1, D_MODEL), dW_down[1].reshape(N_CHIPS, -1, D_MODEL)], "late")
    dproj, dgbias, dhn, *late_from = _mlstm_bwd(proj, gbias, m_head_norm, C_all, n_all, m_all, dgated, late_part,
                                                name="mlstm_bwd")
    grad_x, g_mpre[0] = _mm(dproj, W_all, tb=True, rows=_rows_first_sublayer_start_bwd(h0, dh1, norm_mix_pre[0:1]),
                            name="mlstm_proj_dx")
    dW_all = _mm(z0, dproj, ta=True, name="mlstm_proj_dw")
    dW_min = jnp.concatenate([unpad_heads(dW_all[:, M_OFF_Q:M_OFF_K]), unpad_heads(dW_all[:, M_OFF_K:M_OFF_V]),
                              dW_all[:, M_OFF_V:M_OFF_G], dW_all[:, M_OFF_G:M_OFF_G + 2 * M_HEADS]], axis=1)

    early_tags = ["m_w_in", "m_w_out"]
    early_part = sibling_sums(early_tags, [_cols_to_shards(dW_min), dW_mout.reshape(N_CHIPS, -1, D_MODEL)], "early")
    early_from = _exchange_chips(early_part, name="grad_exchange_chips_early")
    tags = early_tags + late_tags
    halves = [_sum_chips(q, r, place_arr, name=f"grad_sum_chips_{tag}")
              for tag, q, r in zip(tags, early_part + late_part, list(early_from) + list(late_from))]
    reduced = dict(zip(tags, _join_halves(halves, name="grad_join_halves")))
    layer_grads = dict(m_w_in=[reduced["m_w_in"]], m_w_out=[reduced["m_w_out"]], a_w_in=[reduced["a_w_in"]],
                       a_w_out=[reduced["a_w_out"]], f_w_up=[reduced["f_w_up0"], reduced["f_w_up1"]],
                       f_w_down=[reduced["f_w_down0"], reduced["f_w_down1"]])

    small_g = [
        dgbias[:, :2 * M_HEADS].reshape(1, 2, M_HEADS),
        dhn,
        dsinks.reshape(1, A_QH),
        db_in, db_out,
        jnp.concatenate(g_mpre), jnp.concatenate(g_mpost), jnp.concatenate(g_fpre), jnp.concatenate(g_post),
        jnp.stack(dconv_w), jnp.concatenate(dconv_b),
    ]
    small_names = ["m_gate_bias", "m_head_norm", "a_sinks", "a_b_in", "a_b_out", "norm_mix_pre", "norm_mix_post",
                   "norm_ffn_pre", "norm_ffn_post", "f_conv_w", "f_conv_b"]
    red = _unpack_small(_allreduce_small(_pack_small(small_g), name="reduce_small"), [t.shape for t in small_g])
    for n, t in zip(small_names, red):
        if n in ("a_b_in", "a_b_out", "f_conv_w"):
            axis = t.ndim - 1
            width = params[n].shape[axis]
            t = lax.dynamic_slice_in_dim(t, chip * width, width, axis)
        grads[n] = t

    deltas, new_m, new_v = {}, {}, {}
    for n in _BIG:
        grads[n], deltas[n], new_m[n], new_v[n] = _adamw_layers(params[n], layer_grads[n], mom1[n], mom2[n],
                                                                name=f"adamw_{n}")
    two = lambda t: t.reshape(-1, t.shape[-1])
    res = _adamw_small([(two(params[n]), two(grads[n]), two(mom1[n]), two(mom2[n])) for n in small_names],
                       name="adamw_small")
    for n, (d, nm, nv) in zip(small_names, res):
        sh = params[n].shape
        deltas[n], new_m[n], new_v[n] = d.reshape(sh), nm.reshape(sh), nv.reshape(sh)

    return (loss, grad_x[None], *[grads[n] for n in order], *[deltas[n] for n in order],
            *[new_m[n] for n in order], *[new_v[n] for n in order])
```

```python
import functools
import math

import jax
import jax.numpy as jnp
from jax import lax
from jax.experimental import pallas as pl
from jax.experimental.pallas import tpu as pltpu

F32 = jnp.float32
MXU_DTYPE = jnp.bfloat16
WIRE_DTYPE = jnp.bfloat16
MESH = pl.DeviceIdType.MESH

D_MODEL = 1024
EPS = 1e-6
M_HEADS, M_QK, M_V, M_CHUNK = 8, 64, 128, 128
GATE_CAP = 15.0
A_DH, A_QH, A_KVH, A_G, A_BLK = 64, 16, 2, 8, 128
D_FF = 2816
N_CHIPS = 4
LANES = 128
VMEM_LIMIT = 56 * 1024 * 1024

ADAM_LR, ADAM_B1, ADAM_B2, ADAM_EPS, ADAM_WD, ADAM_STEP = 0.001, 0.9, 0.999, 1e-08, 0.01, 10


def _cparams(sem):
    return pltpu.CompilerParams(dimension_semantics=sem, vmem_limit_bytes=VMEM_LIMIT)


def _pick(n, cands):
    for c in cands:
        if n % c == 0:
            return c
    return n


class _Rows:
    def __init__(self, tiles, vecs, outs, fn):
        self.tiles, self.vecs, self.outs, self.fn = list(tiles), list(vecs), list(outs), fn


ROWS_TILE = 512
ROWS_FULL_K = 4224


def _mm(a, b, *, ta=False, tb=False, out_dtype=F32, bias=None, add=None, b_layer=None, out_stacked=False, rows=None,
        name):
    if ta:
        K, M = a.shape
    else:
        M, K = a.shape
    b_stacked = b_layer is not None
    if b_stacked:
        assert not ta
        n_sh = b.shape[2]
        w_rows, w_cols = D_MODEL, N_CHIPS * n_sh
        N, Kb = (w_rows, w_cols) if tb else (w_cols, w_rows)
    elif tb:
        N, Kb = b.shape
    else:
        Kb, N = b.shape
    assert K == Kb, (a.shape, b.shape)
    tm = _pick(M, (1024, 1408, 512, 256, 128))
    tn = _pick(N, (1024, 1408, 1280, 512, 256, 128))
    tk = K if K <= 2816 else _pick(K, (2816, 2048, 1408, 1024, 512, 256, 128))
    shards_per_step = 1
    if b_stacked and tb:
        shards_per_step = 2
        tk = shards_per_step * n_sh
    if (b_stacked and not tb) or out_stacked:
        tn = N // N_CHIPS
    if rows is not None:
        assert tn == N and not out_stacked and not ta
        tm = min(tm, ROWS_TILE)
        if K <= ROWS_FULL_K:
            tk = K
    nk = K // tk
    dn = (((0 if ta else 1,), (1 if tb else 0,)), ((), ()))
    has_bias, has_add = bias is not None, add is not None
    n_tiles, n_vecs, n_outs = (len(rows.tiles), len(rows.vecs), len(rows.outs)) if rows is not None else (0, 0, 1)
    n_row = M // tm
    lag = 1 if rows is not None else 0
    slab = tm // nk
    assert not lag or (tm % nk == 0 and slab % 16 == 0)

    def body(*refs):
        a_ref, b_ref = refs[0], refs[1]
        pos = 2
        bias_ref = add_ref = None
        if has_bias:
            bias_ref = refs[pos]
            pos += 1
        if has_add:
            add_ref = refs[pos]
            pos += 1
        tile_refs, vec_refs = refs[pos:pos + n_tiles], refs[pos + n_tiles:pos + n_tiles + n_vecs]
        pos += n_tiles + n_vecs
        out_refs = refs[pos:pos + n_outs]
        acc_ref = refs[pos + n_outs] if (nk > 1 or lag) else None
        i, k = pl.program_id(0), pl.program_id(2)

        def product():
            if shards_per_step > 1:
                return sum(lax.dot_general(a_ref[:, s * n_sh:(s + 1) * n_sh], b_ref[s], dn, preferred_element_type=F32)
                           for s in range(shards_per_step))
            return lax.dot_general(a_ref[...], b_ref[...], dn, preferred_element_type=F32)

        if not lag:
            def finish(r):
                if has_bias:
                    r = r + bias_ref[...]
                if has_add:
                    r = r + add_ref[...]
                out_refs[0][...] = r.astype(out_dtype)

            part = product()
            if nk == 1:
                finish(part)
            else:
                @pl.when(k == 0)
                def _():
                    acc_ref[...] = part

                @pl.when(k > 0)
                def _():
                    acc_ref[...] += part

                @pl.when(k == nk - 1)
                def _():
                    finish(acc_ref[...])
            return

        cur = i % 2

        def accumulate():
            part = product()
            if nk == 1:
                acc_ref[cur] = part
            else:
                acc_ref[cur] = part + jnp.where(k == 0, 0.0, acc_ref[cur])

        def epilogue_slab():
            rs = pl.ds(pl.multiple_of(k * slab, 16), slab)
            r = acc_ref[1 - cur, rs, :]
            if has_bias:
                r = r + bias_ref[...]
            if has_add:
                r = r + add_ref[rs, :]
            vals = rows.fn(r, [t[rs, :] for t in tile_refs], [v[...] for v in vec_refs])
            for (kind, _), o_ref, val in zip(rows.outs, out_refs, vals):
                if kind == "tile":
                    o_ref[rs, :] = val.astype(o_ref.dtype)
                else:
                    o_ref[...] += val

        @pl.when(i == 0)
        def _():
            @pl.when(k == 0)
            def _():
                acc_ref[...] = jnp.zeros_like(acc_ref)
                for (kind, _), o_ref in zip(rows.outs, out_refs):
                    if kind == "vec":
                        o_ref[...] = jnp.zeros_like(o_ref)

            accumulate()

        @pl.when((i > 0) & (i < n_row))
        def _():
            accumulate()
            epilogue_slab()

        @pl.when(i == n_row)
        def _():
            epilogue_slab()

    row_of_a = (lambda i: jnp.minimum(i, n_row - 1)) if lag else (lambda i: i)
    row_behind = lambda i: jnp.maximum(i - lag, 0)
    if ta:
        a_spec = pl.BlockSpec((tk, tm), lambda i, j, k: (k, i))
    else:
        a_spec = pl.BlockSpec((tm, tk), lambda i, j, k: (row_of_a(i), k))
    if b_stacked and tb:
        off = b_layer * (w_rows // tn)
        b_spec = pl.BlockSpec((shards_per_step, tn, n_sh), lambda i, j, k: (k, off + j, 0))
    elif b_stacked:
        off = b_layer * (w_rows // tk)
        b_spec = pl.BlockSpec((None, tk, tn), lambda i, j, k: (j, off + k, 0))
    elif tb:
        b_spec = pl.BlockSpec((tn, tk), lambda i, j, k: (j, k))
    else:
        b_spec = pl.BlockSpec((tk, tn), lambda i, j, k: (k, j))
    if out_stacked:
        out_spec = pl.BlockSpec((None, tm, tn), lambda i, j, k: (j, i, 0))
        out_shape = jax.ShapeDtypeStruct((N_CHIPS, M, tn), out_dtype)
    else:
        out_spec = pl.BlockSpec((tm, tn), lambda i, j, k: (i, j))
        out_shape = jax.ShapeDtypeStruct((M, N), out_dtype)
    in_specs, args = [a_spec, b_spec], [a, b]
    if has_bias:
        in_specs.append(pl.BlockSpec((1, tn), lambda i, j, k: (0, j)))
        args.append(bias)
    if has_add:
        in_specs.append(pl.BlockSpec((tm, tn), lambda i, j, k: (row_behind(i), j)))
        args.append(add)
    semantics = ("parallel", "parallel", "arbitrary")
    scratch = [pltpu.VMEM((tm, tn), F32)] if nk > 1 else []
    if rows is not None:
        tile_spec = pl.BlockSpec((tm, N), lambda i, j, k: (row_behind(i), 0))
        vec_spec = pl.BlockSpec((1, N), lambda i, j, k: (0, 0))
        in_specs += [tile_spec] * n_tiles + [vec_spec] * n_vecs
        args += rows.tiles + rows.vecs
        out_spec = [tile_spec if kind == "tile" else vec_spec for kind, _ in rows.outs]
        out_shape = [jax.ShapeDtypeStruct((M, N) if kind == "tile" else (1, N), dt) for kind, dt in rows.outs]
        semantics = ("arbitrary", "arbitrary", "arbitrary")
        scratch = [pltpu.VMEM((2, tm, tn), F32)]
    return pl.pallas_call(
        body, name=name,
        grid=(n_row + lag, N // tn, nk),
        in_specs=in_specs,
        out_specs=out_spec,
        out_shape=out_shape,
        scratch_shapes=scratch,
        compiler_params=_cparams(semantics),
    )(*args)


ROW_TILE = 512


def _rms_fwd(x, g, *, out_dtype, name):
    S, D = x.shape
    tm = _pick(S, (ROW_TILE, 256, 128))

    def body(x_ref, g_ref, o_ref):
        o_ref[...] = _rms(x_ref[...], g_ref[...]).astype(out_dtype)

    row = pl.BlockSpec((tm, D), lambda i: (i, 0))
    return pl.pallas_call(
        body, name=name, grid=(S // tm,), in_specs=[row, pl.BlockSpec((1, D), lambda i: (0, 0))], out_specs=row,
        out_shape=jax.ShapeDtypeStruct((S, D), out_dtype),
        compiler_params=_cparams(("parallel",)),
    )(x, g)


def _rms(x, g):
    return x * lax.rsqrt(jnp.mean(x * x, axis=-1, keepdims=True) + EPS) * g


def _rms_vjp(x, g, dy):
    r = lax.rsqrt(jnp.mean(x * x, axis=-1, keepdims=True) + EPS)
    xh = x * r
    gd = dy * g
    dx = r * (gd - xh * jnp.mean(gd * xh, axis=-1, keepdims=True))
    return dx, jnp.sum(dy * xh, axis=0, keepdims=True)


def _rows_sublayer_end(res, g_post, g_pre_next):
    def fn(z, tiles, vecs):
        h = tiles[0] + _rms(z, vecs[0])
        return [z, h, _rms(h, vecs[1])]

    return _Rows([res], [g_post, g_pre_next], [("tile", F32), ("tile", F32), ("tile", MXU_DTYPE)], fn)


def _rows_last_sublayer_end(res, g_post, target):
    def fn(z, tiles, vecs):
        y = tiles[0] + _rms(z, vecs[0])
        err = y - tiles[1]
        dy = err * (1.0 / err.shape[-1])
        dz, dg = _rms_vjp(z, vecs[0], dy)
        loss = 0.5 * jnp.sum(jnp.mean(err * err, axis=-1, keepdims=True), axis=0, keepdims=True)
        return [z, dy, dz, dg, jnp.broadcast_to(loss, dg.shape)]

    return _Rows([res, target], [g_post], [("tile", F32), ("tile", F32), ("tile", MXU_DTYPE), ("vec", F32), ("vec", F32)], fn)


def _rows_sublayer_start_bwd(x, res, z_below, g_pre, g_post_below):
    def fn(dz, tiles, vecs):
        dx, dg_pre = _rms_vjp(tiles[0], vecs[0], dz)
        dh = tiles[1] + dx
        dzb, dg_post = _rms_vjp(tiles[2], vecs[1], dh)
        return [dh, dg_pre, dzb, dg_post, jnp.sum(dzb, axis=0, keepdims=True)]

    return _Rows([x, res, z_below], [g_pre, g_post_below],
                 [("tile", F32), ("vec", F32), ("tile", MXU_DTYPE), ("vec", F32), ("vec", F32)], fn)


def _rows_first_sublayer_start_bwd(x, res, g_pre):
    def fn(dz, tiles, vecs):
        dx, dg_pre = _rms_vjp(tiles[0], vecs[0], dz)
        return [tiles[1] + dx, dg_pre]

    return _Rows([x, res], [g_pre], [("tile", F32), ("vec", F32)], fn)


def _mx(t):
    return t.astype(MXU_DTYPE)


def _mxf(t):
    return t.astype(MXU_DTYPE).astype(F32)


def _rawdot(a, b, ca, cb):
    return lax.dot_general(_mx(a), _mx(b), (((ca,), (cb,)), ((), ())), preferred_element_type=F32)


@functools.partial(jax.custom_vjp, nondiff_argnums=(2, 3))
def _dot(a, b, ca, cb):
    return _rawdot(a, b, ca, cb)


def _dot_fwd(a, b, ca, cb):
    return _rawdot(a, b, ca, cb), (a, b)


def _dot_bwd(ca, cb, res, g):
    a, b = res
    bj = 1 if cb == 0 else 0
    ai = 0 if ca == 1 else 1
    da = _rawdot(g, b, 1, bj) if ca == 1 else _rawdot(b, g, bj, 1)
    db = _rawdot(a, g, ai, 0) if cb == 0 else _rawdot(g, a, 0, ai)
    return da.astype(a.dtype), db.astype(b.dtype)


_dot.defvjp(_dot_fwd, _dot_bwd)


def _softcap(z):
    return GATE_CAP * jnp.tanh(z / GATE_CAP)


def _log_sigmoid(z):
    return jnp.minimum(z, 0.0) - jnp.log(1.0 + jnp.exp(-jnp.abs(z)))


def _sigmoid(z):
    return 0.5 * jnp.tanh(0.5 * z) + 0.5


def _lane_col(t, lane_index):
    lane = lax.broadcasted_iota(jnp.int32, t.shape, 1)
    return jnp.sum(jnp.where(lane == lane_index, t, 0.0), axis=1, keepdims=True)


def _mlstm_gates(G, bias):
    L = G.shape[0]
    z = _softcap(G + bias)
    ig = z
    lf = _log_sigmoid(z)
    ti = lax.broadcasted_iota(jnp.int32, (L, L), 0)
    si = lax.broadcasted_iota(jnp.int32, (L, L), 1)
    tril = (si <= ti).astype(F32)
    b = lax.dot_general(tril, lf, (((1,), (0,)), ((), ())), precision=lax.Precision.HIGHEST, preferred_element_type=F32)
    bL = jnp.sum(lf, axis=0, keepdims=True)
    return ig, b, ig.T, b.T, bL


def _mlstm_head(h, q, k, v, o, ig_all, b_all, igT, bT, bL_all, hn, C, n, m):
    L = q.shape[0]
    ti = lax.broadcasted_iota(jnp.int32, (L, L), 0)
    si = lax.broadcasted_iota(jnp.int32, (L, L), 1)
    lower = (si <= ti)
    ig = _lane_col(ig_all, h)
    b = _lane_col(b_all, M_HEADS + h)
    ig_row = igT[h:h + 1, :]
    b_row = bT[M_HEADS + h:M_HEADS + h + 1, :]
    bL = _lane_col(bL_all, M_HEADS + h)
    inter = b + m
    dlog = jnp.where(lower, b - b_row + ig_row, -jnp.inf)
    m_t = lax.stop_gradient(jnp.maximum(inter, jnp.max(dlog, axis=-1, keepdims=True)))
    qs = q * (M_QK ** -0.5)
    w = _dot(qs, k, 1, 1) * jnp.exp(dlog - m_t)
    s_inter = jnp.exp(inter - m_t)
    num = _dot(w, v, 1, 0) + s_inter * _dot(qs, C, 1, 0)
    den = jnp.sum(w, axis=-1, keepdims=True) + s_inter * jnp.sum(_mxf(qs) * _mxf(n), axis=-1, keepdims=True)
    hout = num * (1.0 / jnp.maximum(jnp.abs(den), jnp.exp(-m_t)))
    tail = bL - b + ig
    m_new = lax.stop_gradient(jnp.maximum(bL + m, jnp.max(tail, axis=0, keepdims=True)))
    ws = jnp.exp(tail - m_new)
    decay = jnp.exp(bL + m - m_new)
    wk = ws * k
    C_new = decay * C + _dot(wk, v, 0, 0)
    n_new = decay * n + jnp.sum(_mxf(ws) * _mxf(k), axis=0, keepdims=True)
    hs = hout * lax.rsqrt(jnp.mean(hout * hout, axis=-1, keepdims=True) + EPS) * hn
    gated = _sigmoid(o) * hs
    return (gated, C_new, n_new), m_new


M_OFF_Q, M_OFF_K, M_OFF_V, M_OFF_O, M_OFF_G = 0, M_HEADS * LANES, 2 * M_HEADS * LANES, 3 * M_HEADS * LANES, 4 * M_HEADS * LANES
M_PROJ = M_OFF_G + LANES


def _head_cols(off, h):
    return slice(off + h * LANES, off + (h + 1) * LANES)


def _mlstm_specs(NC, rev):
    H, L = M_HEADS, M_CHUNK
    cc = (lambda c: NC - 1 - c) if rev else (lambda c: c)
    proj = pl.BlockSpec((L, M_PROJ), lambda c: (cc(c), 0))
    vec = pl.BlockSpec((1, LANES), lambda c: (0, 0))
    hn = pl.BlockSpec((1, H * M_V), lambda c: (0, 0))
    hv = pl.BlockSpec((L, H * M_V), lambda c: (cc(c), 0))
    Cs = pl.BlockSpec((None, H, LANES, M_V), lambda c: (cc(c), 0, 0, 0))
    ns = pl.BlockSpec((None, H, 1, LANES), lambda c: (cc(c), 0, 0, 0))
    ms = pl.BlockSpec((None, H, 1, 1), lambda c: (cc(c), 0, 0, 0))
    return proj, vec, hn, hv, Cs, ns, ms


_MLSTM_STATE = [pltpu.VMEM((M_HEADS, LANES, M_V), F32), pltpu.VMEM((M_HEADS, 1, LANES), F32),
                pltpu.VMEM((M_HEADS, 1, 1), F32)]


def _mlstm_fwd(proj, bias, hn, shards, *, name):
    S = proj.shape[0]
    H, NC = M_HEADS, S // M_CHUNK
    ps, vec, hns, hv, Cs, ns, ms = _mlstm_specs(NC, False)
    nsh = len(shards)
    any_spec = pl.BlockSpec(memory_space=pl.ANY)

    def body(p_ref, b_ref, hn_ref, *rest):
        gated_ref, C_all, n_all, m_all = rest[nsh:nsh + 4]
        C_s, n_s, m_s, send_sems, recv_sems = rest[2 * nsh + 4:]
        gather = _Gather(rest[:nsh], rest[nsh + 4:2 * nsh + 4], send_sems, recv_sems)

        @pl.when(pl.program_id(0) == 0)
        def _():
            C_s[...] = jnp.zeros_like(C_s)
            n_s[...] = jnp.zeros_like(n_s)
            m_s[...] = jnp.zeros_like(m_s)
            gather.issue()

        gate_terms = _mlstm_gates(p_ref[:, M_OFF_G:M_OFF_G + LANES], b_ref[...])
        for h in range(H):
            C, n, m = C_s[h], n_s[h], m_s[h]
            C_all[h] = C
            n_all[h] = n
            m_all[h] = m
            (gated, Cn, nn), mn = _mlstm_head(
                h, p_ref[:, _head_cols(M_OFF_Q, h)], p_ref[:, _head_cols(M_OFF_K, h)], p_ref[:, _head_cols(M_OFF_V, h)],
                p_ref[:, _head_cols(M_OFF_O, h)], *gate_terms, hn_ref[:, _head_cols(0, h)], C, n, m)
            gated_ref[:, _head_cols(0, h)] = gated.astype(gated_ref.dtype)
            C_s[h] = Cn
            n_s[h] = nn
            m_s[h] = mn

        @pl.when(pl.program_id(0) == NC - 1)
        def _():
            gather.finish()

    return pl.pallas_call(
        body, name=name, grid=(NC,),
        in_specs=[ps, vec, hns] + [any_spec] * nsh,
        out_specs=[hv, Cs, ns, ms] + [any_spec] * nsh,
        out_shape=[jax.ShapeDtypeStruct((S, H * M_V), MXU_DTYPE),
                   jax.ShapeDtypeStruct((NC, H, LANES, M_V), F32),
                   jax.ShapeDtypeStruct((NC, H, 1, LANES), F32),
                   jax.ShapeDtypeStruct((NC, H, 1, 1), F32)] + _Gather.out_shape(shards),
        scratch_shapes=list(_MLSTM_STATE) + _Gather.semaphores(nsh),
        compiler_params=_cparams(("arbitrary",)),
    )(proj, bias, hn, *shards)


def _mlstm_bwd(proj, bias, hn, C_all, n_all, m_all, dgated, qs, *, name):
    S = proj.shape[0]
    H, NC = M_HEADS, S // M_CHUNK
    ps, vec, hns, hv, Cs, ns, ms = _mlstm_specs(NC, True)
    nq = len(qs)
    any_spec = pl.BlockSpec(memory_space=pl.ANY)

    def body(p_ref, b_ref, hn_ref, C_ref, n_ref, m_ref, dg_ref, *rest):
        dp_ref, db_ref, dhn_ref = rest[nq:nq + 3]
        dC_s, dn_s, send_sems, recv_sems = rest[2 * nq + 3:]
        exchange = _Exchange(rest[:nq], rest[nq + 3:2 * nq + 3], send_sems, recv_sems)

        @pl.when(pl.program_id(0) == 0)
        def _():
            dC_s[...] = jnp.zeros_like(dC_s)
            dn_s[...] = jnp.zeros_like(dn_s)
            db_ref[...] = jnp.zeros_like(db_ref)
            dhn_ref[...] = jnp.zeros_like(dhn_ref)
            exchange.issue()

        gate_terms, gates_vjp = jax.vjp(_mlstm_gates, p_ref[:, M_OFF_G:M_OFF_G + LANES], b_ref[...])
        d_terms = [jnp.zeros_like(t) for t in gate_terms]
        for h in range(H):
            def head(q, k, v, o, *rest, h=h):
                return _mlstm_head(h, q, k, v, o, *rest, m_ref[h])

            prim = (p_ref[:, _head_cols(M_OFF_Q, h)], p_ref[:, _head_cols(M_OFF_K, h)], p_ref[:, _head_cols(M_OFF_V, h)],
                    p_ref[:, _head_cols(M_OFF_O, h)], *gate_terms, hn_ref[:, _head_cols(0, h)], C_ref[h], n_ref[h])
            _, vjp, _ = jax.vjp(head, *prim, has_aux=True)
            dq, dk, dv, do, *d_gate, dhnh, dC, dn = vjp((dg_ref[:, _head_cols(0, h)].astype(F32), dC_s[h], dn_s[h]))
            dp_ref[:, _head_cols(M_OFF_Q, h)] = dq.astype(dp_ref.dtype)
            dp_ref[:, _head_cols(M_OFF_K, h)] = dk.astype(dp_ref.dtype)
            dp_ref[:, _head_cols(M_OFF_V, h)] = dv.astype(dp_ref.dtype)
            dp_ref[:, _head_cols(M_OFF_O, h)] = do.astype(dp_ref.dtype)
            d_terms = [a + g for a, g in zip(d_terms, d_gate)]
            dhn_ref[:, _head_cols(0, h)] += dhnh
            dC_s[h] = dC
            dn_s[h] = dn
        dG, dbias = gates_vjp(tuple(d_terms))
        dp_ref[:, M_OFF_G:M_OFF_G + LANES] = dG.astype(dp_ref.dtype)
        db_ref[...] += dbias

        @pl.when(pl.program_id(0) == NC - 1)
        def _():
            exchange.finish()

    return pl.pallas_call(
        body, name=name, grid=(NC,),
        in_specs=[ps, vec, hns, Cs, ns, ms, hv] + [any_spec] * nq,
        out_specs=[ps, vec, hns] + [any_spec] * nq,
        out_shape=[jax.ShapeDtypeStruct((S, M_PROJ), MXU_DTYPE), jax.ShapeDtypeStruct((1, LANES), F32),
                   jax.ShapeDtypeStruct((1, H * M_V), F32)] + _Exchange.out_shape(qs),
        scratch_shapes=list(_MLSTM_STATE[:2]) + _Exchange.semaphores(nq),
        compiler_params=_cparams(("arbitrary",)),
    )(proj, bias, hn, C_all, n_all, m_all, dgated, *qs)


A_NQ = A_QH * A_DH
A_NKV = 2 * A_KVH * A_DH
A_PAIRS = A_G // 2


def _attn_group(Ps, KLO, KHI, VLO, VHI, sinks, first):
    B = Ps[0].shape[0]
    R = len(Ps) * B
    q2 = jnp.concatenate(Ps, axis=0) * (A_DH ** -0.5)
    s = jnp.concatenate([_dot(q2, KLO, 1, 1), _dot(q2, KHI, 1, 1)], axis=0)
    qi = lax.broadcasted_iota(jnp.int32, (2 * R, 2 * B), 0) & (B - 1)
    ku = lax.broadcasted_iota(jnp.int32, (2 * R, 2 * B), 1)
    diff = qi - (ku - B)
    mask = (diff >= 0) & (diff < B) & ((ku >= B) | jnp.logical_not(first))
    s = jnp.where(mask, s, -jnp.inf)
    ri = lax.broadcasted_iota(jnp.int32, (2 * R, A_G), 0)
    head = 2 * lax.shift_right_logical(ri & (R - 1), B.bit_length() - 1) + lax.shift_right_logical(ri, R.bit_length() - 1)
    onehot = head == lax.broadcasted_iota(jnp.int32, (2 * R, A_G), 1)
    sink = jnp.sum(jnp.where(onehot, sinks, 0.0), axis=1, keepdims=True)
    mx = lax.stop_gradient(jnp.maximum(jnp.max(s, axis=-1, keepdims=True), sink))
    p = jnp.exp(s - mx)
    p = p * (1.0 / (jnp.sum(p, axis=-1, keepdims=True) + jnp.exp(sink - mx)))
    o = _dot(p[:R], VLO, 1, 0) + _dot(p[R:], VHI, 1, 0)
    return tuple(o[j * B:(j + 1) * B] for j in range(len(Ps)))


def _swap_halves_of_lanes(t):
    return pltpu.roll(t, LANES // 2, 1)


def _kv_operands(kvp_ref, kvc_ref, h):
    kk = jnp.concatenate([kvp_ref[:, :LANES], kvc_ref[:, :LANES]], axis=0)
    vv = jnp.concatenate([kvp_ref[:, LANES:], kvc_ref[:, LANES:]], axis=0)
    low = lax.broadcasted_iota(jnp.int32, kk.shape, 1) < A_DH
    own = low if h == 0 else jnp.logical_not(low)
    k_own = jnp.where(own, kk, 0.0)
    v_own = jnp.where(own, vv, 0.0)
    k_oth, v_oth = _swap_halves_of_lanes(k_own), _swap_halves_of_lanes(v_own)
    if h == 0:
        return own, k_own, k_oth, v_own, v_oth
    return own, k_oth, k_own, v_oth, v_own


def _pair_cols(h, j):
    c = (h * A_PAIRS + j) * LANES
    return slice(c, c + LANES)


def _attn_fwd(proj, sinks, *, name):
    S = proj.shape[0]
    NB = S // A_BLK
    kv_blk = A_NQ // A_NKV
    qs = pl.BlockSpec((A_BLK, A_NQ), lambda n: (n, 0))
    cur = pl.BlockSpec((A_BLK, A_NKV), lambda n: (n, kv_blk))
    prev = pl.BlockSpec((A_BLK, A_NKV), lambda n: (jnp.maximum(n - 1, 0), kv_blk))
    sk = pl.BlockSpec((A_KVH, A_G), lambda n: (0, 0))

    def body(q_ref, kvp_ref, kvc_ref, s_ref, o_ref):
        first = pl.program_id(0) == 0
        for h in range(A_KVH):
            _, KLO, KHI, VLO, VHI = _kv_operands(kvp_ref, kvc_ref, h)
            Ps = tuple(q_ref[:, _pair_cols(h, j)] for j in range(A_PAIRS))
            outs = _attn_group(Ps, KLO, KHI, VLO, VHI, s_ref[h:h + 1, :], first)
            for j in range(A_PAIRS):
                o_ref[:, _pair_cols(h, j)] = outs[j].astype(o_ref.dtype)

    return pl.pallas_call(
        body, name=name, grid=(NB,),
        in_specs=[qs, prev, cur, sk], out_specs=qs,
        out_shape=jax.ShapeDtypeStruct((S, A_NQ), MXU_DTYPE),
        compiler_params=_cparams(("parallel",)),
    )(proj, proj, proj, sinks)


def _attn_bwd(proj, sinks, do, *, name):
    S = proj.shape[0]
    NB = S // A_BLK
    last = NB - 1
    kv_blk = A_NQ // A_NKV
    qs = pl.BlockSpec((A_BLK, A_NQ), lambda n: (jnp.minimum(n, last), 0))
    cur = pl.BlockSpec((A_BLK, A_NKV), lambda n: (jnp.minimum(n, last), kv_blk))
    prev = pl.BlockSpec((A_BLK, A_NKV), lambda n: (jnp.clip(n - 1, 0, last), kv_blk))
    sk = pl.BlockSpec((A_KVH, A_G), lambda n: (0, 0))
    lag = pl.BlockSpec((A_BLK, A_NKV), lambda n: (jnp.maximum(n - 1, 0), 0))
    cq_spec = pl.BlockSpec((1, A_NQ), lambda n: (0, 0))
    ckv_spec = pl.BlockSpec((1, A_NKV), lambda n: (0, 0))

    def body(q_ref, kvp_ref, kvc_ref, s_ref, do_ref, dq_ref, dkv_ref, ds_ref, cq_ref, ckv_ref, keep):
        n = pl.program_id(0)

        @pl.when(n == 0)
        def _():
            keep[...] = jnp.zeros_like(keep)
            ds_ref[...] = jnp.zeros_like(ds_ref)
            cq_ref[...] = jnp.zeros_like(cq_ref)
            ckv_ref[...] = jnp.zeros_like(ckv_ref)

        @pl.when(n < NB)
        def _():
            f = functools.partial(_attn_group, first=n == 0)
            dk = jnp.zeros((2 * A_BLK, LANES), F32)
            dv = jnp.zeros((2 * A_BLK, LANES), F32)
            ds_rows = []
            for h in range(A_KVH):
                own, KLO, KHI, VLO, VHI = _kv_operands(kvp_ref, kvc_ref, h)
                Ps = tuple(q_ref[:, _pair_cols(h, j)] for j in range(A_PAIRS))
                _, vjp = jax.vjp(f, Ps, KLO, KHI, VLO, VHI, s_ref[h:h + 1, :])
                dPs, dKLO, dKHI, dVLO, dVHI, dsk = vjp(
                    tuple(do_ref[:, _pair_cols(h, j)].astype(F32) for j in range(A_PAIRS)))
                for j in range(A_PAIRS):
                    dq_ref[:, _pair_cols(h, j)] = dPs[j].astype(dq_ref.dtype)
                    cq_ref[:, _pair_cols(h, j)] += jnp.sum(dPs[j], axis=0, keepdims=True)
                dk_own, dk_oth = (dKLO, dKHI) if h == 0 else (dKHI, dKLO)
                dv_own, dv_oth = (dVLO, dVHI) if h == 0 else (dVHI, dVLO)
                dk = dk + jnp.where(own, dk_own, 0.0) + _swap_halves_of_lanes(jnp.where(own, 0.0, dk_oth))
                dv = dv + jnp.where(own, dv_own, 0.0) + _swap_halves_of_lanes(jnp.where(own, 0.0, dv_oth))
                ds_rows.append(dsk)
            ds_ref[...] += jnp.concatenate(ds_rows, axis=0)
            dkv = jnp.concatenate([dk, dv], axis=1)
            done = keep[...] + dkv[:A_BLK]
            dkv_ref[...] = done.astype(dkv_ref.dtype)
            ckv_ref[...] += jnp.sum(done, axis=0, keepdims=True)
            keep[...] = dkv[A_BLK:]

        @pl.when(n == NB)
        def _():
            done = keep[...]
            dkv_ref[...] = done.astype(dkv_ref.dtype)
            ckv_ref[...] += jnp.sum(done, axis=0, keepdims=True)

    return pl.pallas_call(
        body, name=name, grid=(NB + 1,),
        in_specs=[qs, prev, cur, sk, qs],
        out_specs=[qs, lag, sk, cq_spec, ckv_spec],
        out_shape=[jax.ShapeDtypeStruct((S, A_NQ), MXU_DTYPE), jax.ShapeDtypeStruct((S, A_NKV), MXU_DTYPE),
                   jax.ShapeDtypeStruct((A_KVH, A_G), F32), jax.ShapeDtypeStruct((1, A_NQ), F32),
                   jax.ShapeDtypeStruct((1, A_NKV), F32)],
        scratch_shapes=[pltpu.VMEM((A_BLK, A_NKV), F32)],
        compiler_params=_cparams(("arbitrary",)),
    )(proj, proj, proj, sinks, do)


HALO = 8


def _shift_rows(t, j):
    if j == 0:
        return t
    return pltpu.roll(t, j % t.shape[0], 0)


def _conv_gate(gate_ext, cw, cb):
    return cb + cw[0:1, :] * _shift_rows(gate_ext, 2) + cw[1:2, :] * _shift_rows(gate_ext, 1) + cw[2:3, :] * gate_ext


def _convgate_fwd(u, cw, cb, *, name):
    S, F2 = u.shape
    F = F2 // 2
    tm = _pick(S, (256, 128))
    hb = tm // HALO
    urow = pl.BlockSpec((tm, F2), lambda i: (i, 0))
    uprev = pl.BlockSpec((HALO, F), lambda i: (jnp.maximum(i * hb - 1, 0), 0))

    def body(u_ref, up_ref, cw_ref, cb_ref, a_ref):
        i = pl.program_id(0)
        gate = u_ref[:, :F]
        val = u_ref[:, F:]
        prev = jnp.where(i > 0, up_ref[...], 0.0)
        gc = _conv_gate(jnp.concatenate([prev, gate], axis=0), cw_ref[...], cb_ref[...])[HALO:]
        a_ref[...] = (gc * _sigmoid(gc) * val).astype(a_ref.dtype)

    return pl.pallas_call(
        body, name=name, grid=(S // tm,),
        in_specs=[urow, uprev, pl.BlockSpec((3, F), lambda i: (0, 0)), pl.BlockSpec((1, F), lambda i: (0, 0))],
        out_specs=pl.BlockSpec((tm, F), lambda i: (i, 0)),
        out_shape=jax.ShapeDtypeStruct((S, F), MXU_DTYPE),
        compiler_params=_cparams(("parallel",)),
    )(u, u, cw, cb)


def _convgate_bwd(u, da, cw, cb, *, name):
    S, F2 = u.shape
    F = F2 // 2
    tm = _pick(S, (128,))
    hb = tm // HALO
    nt = S // tm
    nh = S // HALO
    urow = pl.BlockSpec((tm, F2), lambda i: (i, 0))
    uprev = pl.BlockSpec((HALO, F), lambda i: (jnp.maximum(i * hb - 1, 0), 0))
    unext = pl.BlockSpec((HALO, F2), lambda i: (jnp.minimum((i + 1) * hb, nh - 1), 0))
    darow = pl.BlockSpec((tm, F), lambda i: (i, 0))
    danext = pl.BlockSpec((HALO, F), lambda i: (jnp.minimum((i + 1) * hb, nh - 1), 0))

    def body(u_ref, up_ref, un_ref, da_ref, dan_ref, cw_ref, cb_ref, du_ref, dcw_ref, dcb_ref):
        i = pl.program_id(0)
        cwv = cw_ref[...]
        prev = jnp.where(i > 0, up_ref[...], 0.0)
        gate_ext = jnp.concatenate([prev, u_ref[:, :F], un_ref[:, :F]], axis=0)
        val_ext = jnp.concatenate([u_ref[:, F:], un_ref[:, F:]], axis=0)
        da_next = jnp.where(i < nt - 1, dan_ref[...].astype(F32), 0.0)
        da_ext = jnp.concatenate([da_ref[...].astype(F32), da_next], axis=0)
        gc = _conv_gate(gate_ext, cwv, cb_ref[...])[HALO:]
        sg = _sigmoid(gc)
        silu = gc * sg
        dval = da_ext * silu
        dgc = da_ext * val_ext * (sg * (1.0 + gc * (1.0 - sg)))
        dgate = cwv[2:3, :] * dgc + cwv[1:2, :] * _shift_rows(dgc, -1) + cwv[0:1, :] * _shift_rows(dgc, -2)
        du_ref[:, :F] = dgate[:tm].astype(du_ref.dtype)
        du_ref[:, F:] = dval[:tm].astype(du_ref.dtype)
        dgc_c = dgc[:tm]
        g0 = gate_ext[HALO:HALO + tm]
        g1 = _shift_rows(gate_ext, 1)[HALO:HALO + tm]
        g2 = _shift_rows(gate_ext, 2)[HALO:HALO + tm]
        dcw = jnp.concatenate([jnp.sum(dgc_c * g2, axis=0, keepdims=True),
                               jnp.sum(dgc_c * g1, axis=0, keepdims=True),
                               jnp.sum(dgc_c * g0, axis=0, keepdims=True)], axis=0)
        dcb = jnp.sum(dgc_c, axis=0, keepdims=True)

        @pl.when(i == 0)
        def _():
            dcw_ref[...] = dcw
            dcb_ref[...] = dcb

        @pl.when(i > 0)
        def _():
            dcw_ref[...] += dcw
            dcb_ref[...] += dcb

    return pl.pallas_call(
        body, name=name, grid=(nt,),
        in_specs=[urow, uprev, unext, darow, danext,
                  pl.BlockSpec((3, F), lambda i: (0, 0)), pl.BlockSpec((1, F), lambda i: (0, 0))],
        out_specs=[urow, pl.BlockSpec((3, F), lambda i: (0, 0)), pl.BlockSpec((1, F), lambda i: (0, 0))],
        out_shape=[jax.ShapeDtypeStruct((S, F2), MXU_DTYPE), jax.ShapeDtypeStruct((3, F), F32),
                   jax.ShapeDtypeStruct((1, F), F32)],
        compiler_params=_cparams(("arbitrary",)),
    )(u, u, u, da, da, cw, cb)


def _adamw_math(w, g, m, v):
    m = ADAM_B1 * m + (1.0 - ADAM_B1) * g
    v = ADAM_B2 * v + (1.0 - ADAM_B2) * (g * g)
    m_hat = m / (1.0 - ADAM_B1 ** ADAM_STEP)
    v_hat = v / (1.0 - ADAM_B2 ** ADAM_STEP)
    delta = -ADAM_LR * (m_hat / (jnp.sqrt(v_hat) + ADAM_EPS) + ADAM_WD * w)
    return delta, m, v


def _adamw_layers(w, gs, m, v, *, name):
    Lr, R, C = w.shape
    tr = _pick(R, (256, 128, 64, 32, 16, 8))
    outs = None
    for layer, g in enumerate(gs):
        def body(w_ref, g_ref, m_ref, v_ref, *rest):
            go_ref, d_ref, nm_ref, nv_ref = rest[-4:]
            gv = g_ref[...]
            d, nm, nv = _adamw_math(w_ref[...], gv, m_ref[...], v_ref[...])
            go_ref[...] = gv
            d_ref[...] = d
            nm_ref[...] = nm
            nv_ref[...] = nv

        lay = pl.BlockSpec((None, tr, C), lambda i, layer=layer: (layer, i, 0))
        in_specs = [lay, pl.BlockSpec((tr, C), lambda i: (i, 0)), lay, lay]
        args = [w, g, m, v]
        aliases = {}
        if outs is not None:
            in_specs += [pl.BlockSpec(memory_space=pl.ANY)] * 4
            args += list(outs)
            aliases = {4 + t: t for t in range(4)}
        outs = pl.pallas_call(
            body, name=f"{name}_{layer}", grid=(R // tr,), in_specs=in_specs, out_specs=[lay] * 4,
            out_shape=[jax.ShapeDtypeStruct((Lr, R, C), F32)] * 4, input_output_aliases=aliases,
            compiler_params=_cparams(("parallel",)),
        )(*args)
    return outs


def _adamw_small(items, *, name):
    n = len(items)

    def body(*refs):
        ins, outs = refs[:4 * n], refs[4 * n:]
        for t in range(n):
            w, g, m, v = (r[...] for r in ins[4 * t:4 * t + 4])
            d, nm, nv = _adamw_math(w, g, m, v)
            outs[3 * t][...] = d
            outs[3 * t + 1][...] = nm
            outs[3 * t + 2][...] = nv

    flat = [a for it in items for a in it]
    out_shape = [jax.ShapeDtypeStruct(it[0].shape, F32) for it in items for _ in range(3)]
    vm = pl.BlockSpec(memory_space=pltpu.VMEM)
    res = pl.pallas_call(body, name=name, in_specs=[vm] * len(flat), out_specs=[vm] * len(out_shape),
                         out_shape=out_shape)(*flat)
    return [tuple(res[3 * t:3 * t + 3]) for t in range(n)]


def _place():
    return lax.axis_index("x"), lax.axis_index("y"), lax.axis_index("c")


_FLIPS = ((1, 0), (0, 1), (1, 1))


ROW_ALIGN = 16


def _half(rows, which):
    return pl.ds(pl.multiple_of(which * (rows // 2), ROW_ALIGN), rows // 2)


def _remote(src, dst, send_sems, recv_sems, k, to):
    return pltpu.make_async_remote_copy(src_ref=src, dst_ref=dst, send_sem=send_sems.at[k], recv_sem=recv_sems.at[k],
                                        device_id=to, device_id_type=MESH)


def _gather_shards(shards, *, name):
    n = len(shards)
    any_spec = pl.BlockSpec(memory_space=pl.ANY)

    def body(*refs):
        gather = _Gather(refs[:n], refs[n:2 * n], refs[2 * n], refs[2 * n + 1])
        gather.issue()
        gather.finish()

    return pl.pallas_call(
        body, name=name, in_specs=[any_spec] * n, out_specs=[any_spec] * n,
        out_shape=_Gather.out_shape(shards), scratch_shapes=_Gather.semaphores(n),
    )(*shards)


class _Gather:
    def __init__(self, w_refs, out_refs, send_sems, recv_sems):
        self.w_refs, self.out_refs, self.send_sems, self.recv_sems = w_refs, out_refs, send_sems, recv_sems
        self.pairs = [(i, j) for i in range(len(w_refs)) for j in range(3)]

    @staticmethod
    def out_shape(shards):
        return [jax.ShapeDtypeStruct((N_CHIPS,) + s.shape, s.dtype) for s in shards]

    @staticmethod
    def semaphores(n):
        return [pltpu.SemaphoreType.DMA((6 * n,)), pltpu.SemaphoreType.DMA((6 * n,))]

    def _where(self):
        x, y, c = _place()
        return x, y, c, [(x ^ fx, y ^ fy) for fx, fy in _FLIPS]

    def _over_ici(self, i, j, landed):
        x, y, c, chips = self._where()
        px, py = chips[j]
        mine = _half(self.w_refs[i].shape[0], c)
        if landed:
            src = dst = self.out_refs[i].at[2 * px + py, mine]
        else:
            src, dst = self.w_refs[i].at[mine], self.out_refs[i].at[2 * x + y, mine]
        return _remote(src, dst, self.send_sems, self.recv_sems, 6 * i + j, (px, py, c))

    def _over_d2d(self, i, j, which):
        x, y, c, chips = self._where()
        px, py = chips[j]
        blk = self.out_refs[i].at[2 * px + py, _half(self.w_refs[i].shape[0], which)]
        return _remote(blk, blk, self.send_sems, self.recv_sems, 6 * i + 3 + j, (x, y, 1 - c))

    def issue(self):
        for i, j in self.pairs:
            self._over_ici(i, j, False).start()

    def finish(self):
        c = lax.axis_index("c")
        for i, j in self.pairs:
            self._over_ici(i, j, True).wait_recv()
            self._over_d2d(i, j, c).start()
        for i, j in self.pairs:
            self._over_d2d(i, j, 1 - c).wait_recv()
        for i, j in self.pairs:
            self._over_ici(i, j, False).wait_send()
            self._over_d2d(i, j, c).wait_send()


def _swap_halves(gs, *, name):
    n = len(gs)
    any_spec = pl.BlockSpec(memory_space=pl.ANY)

    def body(*refs):
        g_refs, out_refs, send_sems, recv_sems = refs[:n], refs[n:2 * n], refs[2 * n], refs[2 * n + 1]
        x, y, c = _place()
        cps = [_remote(g_refs[i].at[:, _half(gs[i].shape[1], 1 - c)], out_refs[i], send_sems, recv_sems, i, (x, y, 1 - c))
               for i in range(n)]
        for cp in cps:
            cp.start()
        for cp in cps:
            cp.wait()

    return pl.pallas_call(
        body, name=name, in_specs=[any_spec] * n, out_specs=[any_spec] * n,
        out_shape=[jax.ShapeDtypeStruct((N_CHIPS, g.shape[1] // 2, g.shape[2]), g.dtype) for g in gs],
        scratch_shapes=[pltpu.SemaphoreType.DMA((n,)), pltpu.SemaphoreType.DMA((n,))],
    )(*gs)


def _add_halves(g, got, c_arr, *, name):
    _, rows, cols = g.shape
    blk = (None, rows // 2, cols)

    def body(c_ref, g_ref, got_ref, o_ref):
        o_ref[...] = (g_ref[...] + got_ref[...]).astype(o_ref.dtype)

    return pl.pallas_call(
        body, name=name,
        grid_spec=pltpu.PrefetchScalarGridSpec(
            num_scalar_prefetch=1, grid=(N_CHIPS,),
            in_specs=[pl.BlockSpec(blk, lambda j, c_ref: (j, c_ref[0], 0)), pl.BlockSpec(blk, lambda j, c_ref: (j, 0, 0))],
            out_specs=pl.BlockSpec(blk, lambda j, c_ref: (j, 0, 0))),
        out_shape=jax.ShapeDtypeStruct((N_CHIPS, rows // 2, cols), WIRE_DTYPE),
        compiler_params=_cparams(("parallel",)),
    )(c_arr, g, got)


def _exchange_chips(qs, *, name):
    n = len(qs)
    any_spec = pl.BlockSpec(memory_space=pl.ANY)

    def body(*refs):
        exchange = _Exchange(refs[:n], refs[n:2 * n], refs[2 * n], refs[2 * n + 1])
        exchange.issue()
        exchange.finish()

    return pl.pallas_call(
        body, name=name, in_specs=[any_spec] * n, out_specs=[any_spec] * n,
        out_shape=_Exchange.out_shape(qs), scratch_shapes=_Exchange.semaphores(n),
    )(*qs)


class _Exchange:
    def __init__(self, q_refs, out_refs, send_sems, recv_sems):
        self.q_refs, self.out_refs, self.send_sems, self.recv_sems = q_refs, out_refs, send_sems, recv_sems
        self.pairs = [(i, j) for i in range(len(q_refs)) for j in range(3)]

    @staticmethod
    def out_shape(qs):
        return [jax.ShapeDtypeStruct(q.shape, q.dtype) for q in qs]

    @staticmethod
    def semaphores(n):
        return [pltpu.SemaphoreType.DMA((3 * n,)), pltpu.SemaphoreType.DMA((3 * n,))]

    def _copy(self, i, j, landed):
        x, y, c = _place()
        px, py = [(x ^ fx, y ^ fy) for fx, fy in _FLIPS][j]
        if landed:
            src = dst = self.out_refs[i].at[2 * px + py]
        else:
            src, dst = self.q_refs[i].at[2 * px + py], self.out_refs[i].at[2 * x + y]
        return _remote(src, dst, self.send_sems, self.recv_sems, 3 * i + j, (px, py, c))

    def issue(self):
        for i, j in self.pairs:
            self._copy(i, j, False).start()

    def finish(self):
        for i, j in self.pairs:
            self._copy(i, j, True).wait_recv()
        for i, j in self.pairs:
            self._copy(i, j, False).wait_send()


def _sum_chips(q, r, place_arr, *, name):
    _, h, cols = q.shape
    blk = (None, h, cols)

    def body(p_ref, q_ref, r1_ref, r2_ref, r3_ref, o_ref):
        o_ref[...] = ((q_ref[...].astype(F32) + r1_ref[...].astype(F32)) + r2_ref[...].astype(F32)) + r3_ref[...].astype(F32)

    other = [pl.BlockSpec(blk, lambda i, p_ref, f=f: (p_ref[0] ^ f, 0, 0)) for f in (1, 2, 3)]
    return pl.pallas_call(
        body, name=name,
        grid_spec=pltpu.PrefetchScalarGridSpec(
            num_scalar_prefetch=1, grid=(1,),
            in_specs=[pl.BlockSpec(blk, lambda i, p_ref: (p_ref[0], 0, 0))] + other,
            out_specs=pl.BlockSpec((h, cols), lambda i, p_ref: (p_ref[1], 0))),
        out_shape=jax.ShapeDtypeStruct((2 * h, cols), F32),
        compiler_params=_cparams(("arbitrary",)),
    )(place_arr, q, r, r, r)


def _join_halves(fs, *, name):
    n = len(fs)
    any_spec = pl.BlockSpec(memory_space=pl.ANY)

    def body(*refs):
        out_refs, send_sems, recv_sems = refs[n:2 * n], refs[2 * n], refs[2 * n + 1]
        x, y, c = _place()
        sent = []
        for i in range(n):
            mine = out_refs[i].at[_half(fs[i].shape[0], c)]
            cp = _remote(mine, mine, send_sems, recv_sems, i, (x, y, 1 - c))
            cp.start()
            sent.append(cp)
        for i in range(n):
            its = out_refs[i].at[_half(fs[i].shape[0], 1 - c)]
            _remote(its, its, send_sems, recv_sems, i, (x, y, 1 - c)).wait_recv()
        for cp in sent:
            cp.wait_send()

    return pl.pallas_call(
        body, name=name, in_specs=[any_spec] * n, out_specs=[any_spec] * n,
        out_shape=[jax.ShapeDtypeStruct(f.shape, f.dtype) for f in fs], input_output_aliases={i: i for i in range(n)},
        scratch_shapes=[pltpu.SemaphoreType.DMA((n,)), pltpu.SemaphoreType.DMA((n,))],
    )(*fs)


def _allreduce_small(buf, *, name):
    R = buf.shape[0]
    vm = pl.BlockSpec(memory_space=pltpu.VMEM)

    def body(b_ref, o_ref, slots, send_sems, recv_sems):
        x, y, c = _place()
        me = 4 * x + 2 * y + c
        slots[me] = b_ref[...]
        sends = []
        for kk in range(1, 8):
            fx, fy, fc = (kk >> 2) & 1, (kk >> 1) & 1, kk & 1
            cp = pltpu.make_async_remote_copy(
                src_ref=b_ref, dst_ref=slots.at[me], send_sem=send_sems.at[kk - 1], recv_sem=recv_sems.at[kk - 1],
                device_id=(x ^ fx, y ^ fy, c ^ fc), device_id_type=MESH)
            cp.start()
            sends.append(cp)
        for kk in range(1, 8):
            fx, fy, fc = (kk >> 2) & 1, (kk >> 1) & 1, kk & 1
            peer = 4 * (x ^ fx) + 2 * (y ^ fy) + (c ^ fc)
            pltpu.make_async_remote_copy(
                src_ref=b_ref, dst_ref=slots.at[peer], send_sem=send_sems.at[kk - 1], recv_sem=recv_sems.at[kk - 1],
                device_id=(x ^ fx, y ^ fy, c ^ fc), device_id_type=MESH).wait_recv()
        for cp in sends:
            cp.wait_send()
        acc = slots[0]
        for d in range(1, 8):
            acc = acc + slots[d]
        o_ref[...] = acc

    return pl.pallas_call(
        body, name=name, in_specs=[vm], out_specs=vm,
        out_shape=jax.ShapeDtypeStruct((R, LANES), F32),
        scratch_shapes=[pltpu.VMEM((8, R, LANES), F32), pltpu.SemaphoreType.DMA((7,)), pltpu.SemaphoreType.DMA((7,))],
    )(buf)


def _pad_rows(v, mult=8 * LANES):
    flat = v.reshape(-1)
    n = flat.shape[0]
    tot = -(-n // mult) * mult
    return jnp.pad(flat, (0, tot - n)).reshape(-1, LANES)


def _pack_small(parts):
    return jnp.concatenate([_pad_rows(p.astype(F32)) for p in parts], axis=0)


def _unpack_small(buf, shapes):
    out, r = [], 0
    for sh in shapes:
        n = math.prod(sh)
        rows = -(-n // (8 * LANES)) * 8
        out.append(buf[r:r + rows].reshape(-1)[:n].reshape(sh))
        r += rows
    return out


def _cols_to_shards(w):
    *lead, K, N = w.shape
    t = w.reshape(*lead, K, N_CHIPS, N // N_CHIPS)
    return jnp.moveaxis(t, -2, 0)


def _shards_to_cols(t):
    t = jnp.moveaxis(t, 0, -2)
    *lead, K, _, n = t.shape
    return t.reshape(*lead, K, N_CHIPS * n)


_BIG = ("m_w_in", "m_w_out", "a_w_in", "a_w_out", "f_w_up", "f_w_down")


def kernel(x, m_w_in, m_gate_bias, m_head_norm, m_w_out, a_w_in, a_b_in, a_sinks, a_w_out, a_b_out, norm_mix_pre, norm_mix_post, norm_ffn_pre, norm_ffn_post, f_w_up, f_conv_w, f_conv_b, f_w_down, loss_target, m_m_w_in, m_m_gate_bias, m_m_head_norm, m_m_w_out, m_a_w_in, m_a_b_in, m_a_sinks, m_a_w_out, m_a_b_out, m_norm_mix_pre, m_norm_mix_post, m_norm_ffn_pre, m_norm_ffn_post, m_f_w_up, m_f_conv_w, m_f_conv_b, m_f_w_down, v_m_w_in, v_m_gate_bias, v_m_head_norm, v_m_w_out, v_a_w_in, v_a_b_in, v_a_sinks, v_a_w_out, v_a_b_out, v_norm_mix_pre, v_norm_mix_post, v_norm_ffn_pre, v_norm_ffn_post, v_f_w_up, v_f_conv_w, v_f_conv_b, v_f_w_down):
    params = dict(m_w_in=m_w_in, m_gate_bias=m_gate_bias, m_head_norm=m_head_norm, m_w_out=m_w_out, a_w_in=a_w_in,
                  a_b_in=a_b_in, a_sinks=a_sinks, a_w_out=a_w_out, a_b_out=a_b_out, norm_mix_pre=norm_mix_pre,
                  norm_mix_post=norm_mix_post, norm_ffn_pre=norm_ffn_pre, norm_ffn_post=norm_ffn_post, f_w_up=f_w_up,
                  f_conv_w=f_conv_w, f_conv_b=f_conv_b, f_w_down=f_w_down)
    mom1 = dict(m_w_in=m_m_w_in, m_gate_bias=m_m_gate_bias, m_head_norm=m_m_head_norm, m_w_out=m_m_w_out,
                a_w_in=m_a_w_in, a_b_in=m_a_b_in, a_sinks=m_a_sinks, a_w_out=m_a_w_out, a_b_out=m_a_b_out,
                norm_mix_pre=m_norm_mix_pre, norm_mix_post=m_norm_mix_post, norm_ffn_pre=m_norm_ffn_pre,
                norm_ffn_post=m_norm_ffn_post, f_w_up=m_f_w_up, f_conv_w=m_f_conv_w, f_conv_b=m_f_conv_b,
                f_w_down=m_f_w_down)
    mom2 = dict(m_w_in=v_m_w_in, m_gate_bias=v_m_gate_bias, m_head_norm=v_m_head_norm, m_w_out=v_m_w_out,
                a_w_in=v_a_w_in, a_b_in=v_a_b_in, a_sinks=v_a_sinks, a_w_out=v_a_w_out, a_b_out=v_a_b_out,
                norm_mix_pre=v_norm_mix_pre, norm_mix_post=v_norm_mix_post, norm_ffn_pre=v_norm_ffn_pre,
                norm_ffn_post=v_norm_ffn_post, f_w_up=v_f_w_up, f_conv_w=v_f_conv_w, f_conv_b=v_f_conv_b,
                f_w_down=v_f_w_down)
    order = list(params)

    mx, my, mc = _place()
    chip = 2 * mx + my
    h0 = x[0]
    target = loss_target[0]

    def two_d(t):
        return t.reshape(-1, t.shape[-1])

    early = ("m_w_in", "m_w_out")
    late = tuple(n for n in _BIG if n not in early)
    mine = {n: two_d(params[n]).astype(MXU_DTYPE) for n in _BIG}

    def with_own_slot(names, theirs):
        return {n: lax.dynamic_update_slice(t, mine[n][None], (chip, 0, 0)) for n, t in zip(names, theirs)}

    gathered = with_own_slot(early, _gather_shards([mine[n] for n in early], name="gather_weights_early"))

    def in_place(shard, axis):
        width = shard.shape[axis]
        z = jnp.zeros(shard.shape[:axis] + (N_CHIPS * width,) + shard.shape[axis + 1:], F32)
        contrib = jnp.where(mc == 0, shard, 0.0)
        return lax.dynamic_update_slice_in_dim(z, contrib, chip * width, axis)

    sm_in = [in_place(a_b_in, 1), in_place(a_b_out, 1), in_place(f_conv_w, 2)]
    sm_full = _unpack_small(_allreduce_small(_pack_small(sm_in), name="gather_small"), [t.shape for t in sm_in])
    b_in_full, b_out_full, conv_w_full = sm_full

    W_in = _shards_to_cols(gathered["m_w_in"])
    nqk = M_HEADS * M_QK
    n_main = 2 * nqk + M_HEADS * M_V + D_MODEL

    def pad_heads(w):
        return jnp.pad(w.reshape(D_MODEL, M_HEADS, M_QK), ((0, 0), (0, 0), (0, LANES - M_QK))).reshape(D_MODEL, -1)

    def unpad_heads(w):
        return w.reshape(D_MODEL, M_HEADS, LANES)[:, :, :M_QK].reshape(D_MODEL, nqk)

    W_all = jnp.concatenate([pad_heads(W_in[:, :nqk]), pad_heads(W_in[:, nqk:2 * nqk]), W_in[:, 2 * nqk:n_main],
                             jnp.pad(W_in[:, n_main:], ((0, 0), (0, LANES - 2 * M_HEADS)))], axis=1)
    gbias = jnp.pad(m_gate_bias[0].reshape(1, 2 * M_HEADS), ((0, 0), (0, LANES - 2 * M_HEADS)))
    W_mout = gathered["m_w_out"].reshape(D_MODEL, D_MODEL)

    grads = {}

    def ffn_fwd(i, z1, end):
        u = _mm(z1, W_up, b_layer=i, name=f"ffn_up{i}")
        a = _convgate_fwd(u, conv_w_full[i], f_conv_b[i:i + 1], name=f"ffn_act{i}")
        return u, a, _mm(a, W_down[i], rows=end, name=f"ffn_down{i}")

    z0 = _rms_fwd(h0, norm_mix_pre[0:1], out_dtype=MXU_DTYPE, name="mix_pre_norm0")
    proj = _mm(z0, W_all, name="mlstm_proj")
    gated, C_all, n_all, m_all, *late_theirs = _mlstm_fwd(proj, gbias, m_head_norm, [mine[n] for n in late],
                                                          name="mlstm_fwd")
    gathered.update(with_own_slot(late, late_theirs))
    A_in, A_out = _shards_to_cols(gathered["a_w_in"]), gathered["a_w_out"].reshape(D_MODEL, D_MODEL)
    W_up = gathered["f_w_up"]
    dsh = D_FF // N_CHIPS
    W_down = [gathered["f_w_down"][:, i * dsh:(i + 1) * dsh].reshape(D_FF, D_MODEL) for i in range(2)]
    zm0, h1, z1 = _mm(gated, W_mout, rows=_rows_sublayer_end(h0, norm_mix_post[0:1], norm_ffn_pre[0:1]), name="mlstm_out")
    u0, a0, (zf0, h2, y0) = ffn_fwd(0, z1, _rows_sublayer_end(h1, norm_ffn_post[0:1], norm_mix_pre[1:2]))

    aproj = _mm(y0, A_in, bias=b_in_full, name="attn_proj")
    sinks = a_sinks.reshape(A_KVH, A_G)
    ao = _attn_fwd(aproj, sinks, name="attn_fwd")
    zm1, h3, z3 = _mm(ao, A_out, bias=b_out_full, rows=_rows_sublayer_end(h2, norm_mix_post[1:2], norm_ffn_pre[1:2]),
                      name="attn_out")
    u1, a1, (zf1, dh, dzf1, g_post1, loss_lanes) = ffn_fwd(1, z3, _rows_last_sublayer_end(h3, norm_ffn_post[1:2], target))
    loss = lax.psum(loss_lanes[0, 0], ("x", "y", "c"))

    g_post, g_fpre, g_mpost, g_mpre = [None, g_post1], [None, None], [None, None], [None, None]
    dW_up, dW_down, dconv_w, dconv_b = [None, None], [None, None], [None, None], [None, None]

    def ffn_bwd(i, dzf, z1_, u, a, start):
        da = _mm(dzf, W_down[i], tb=True, name=f"ffn_down_dx{i}")
        dW_down[i] = _mm(a, dzf, ta=True, name=f"ffn_down_dw{i}")
        du, dconv_w[i], dconv_b[i] = _convgate_bwd(u, da, conv_w_full[i], f_conv_b[i:i + 1], name=f"ffn_act_bwd{i}")
        dW_up[i] = _mm(z1_, du, ta=True, out_stacked=True, name=f"ffn_up_dw{i}")
        return _mm(du, W_up, tb=True, b_layer=i, rows=start, name=f"ffn_up_dx{i}")

    dh3, g_fpre[1], dzm1, g_mpost[1], db_out = ffn_bwd(
        1, dzf1, z3, u1, a1, _rows_sublayer_start_bwd(h3, dh, zm1, norm_ffn_pre[1:2], norm_mix_post[1:2]))
    dao = _mm(dzm1, A_out, tb=True, name="attn_out_dx")
    dA_out = _mm(ao, dzm1, ta=True, name="attn_out_dw")
    daq, dakv, dsinks, cs_q, cs_kv = _attn_bwd(aproj, sinks, dao, name="attn_bwd")
    db_in = jnp.concatenate([cs_q, cs_kv], axis=1)
    dh2, g_mpre[1], dzf0, g_post[0], _ = _mm(
        daq, A_in[:, :A_NQ], tb=True, add=_mm(dakv, A_in[:, A_NQ:], tb=True, name="attn_proj_kv_dx"),
        rows=_rows_sublayer_start_bwd(h2, dh3, zf0, norm_mix_pre[1:2], norm_ffn_post[0:1]), name="attn_proj_q_dx")
    dA_in = jnp.concatenate([_mm(y0, daq, ta=True, name="attn_proj_q_dw"),
                             _mm(y0, dakv, ta=True, name="attn_proj_kv_dw")], axis=1)

    dh1, g_fpre[0], dzm0, g_mpost[0], _ = ffn_bwd(
        0, dzf0, z1, u0, a0, _rows_sublayer_start_bwd(h1, dh2, zm0, norm_ffn_pre[0:1], norm_mix_post[0:1]))
    dgated = _mm(dzm0, W_mout, tb=True, name="mlstm_out_dx")
    dW_mout = _mm(gated, dzm0, ta=True, name="mlstm_out_dw")
    c_arr = jnp.reshape(mc, (1,)).astype(jnp.int32)
    place_arr = jnp.stack([chip, mc]).astype(jnp.int32)

    def sibling_sums(tags, gs, stage):
        got = _swap_halves(gs, name=f"grad_swap_halves_{stage}")
        return [_add_halves(g, t, c_arr, name=f"grad_add_halves_{tag}") for tag, g, t in zip(tags, gs, got)]

    late_tags = ["a_w_in", "a_w_out", "f_w_up0", "f_w_up1", "f_w_down0", "f_w_down1"]
    late_part = sibling_sums(late_tags, [
        _cols_to_shards(dA_in), dA_out.reshape(N_CHIPS, -1, D_MODEL), dW_up[0], dW_up[1],
        dW_down[0].reshape(N_CHIPS, -1, D_MODEL), dW_down[1].reshape(N_CHIPS, -1, D_MODEL)], "late")
    dproj, dgbias, dhn, *late_from = _mlstm_bwd(proj, gbias, m_head_norm, C_all, n_all, m_all, dgated, late_part,
                                                name="mlstm_bwd")
    grad_x, g_mpre[0] = _mm(dproj, W_all, tb=True, rows=_rows_first_sublayer_start_bwd(h0, dh1, norm_mix_pre[0:1]),
                            name="mlstm_proj_dx")
    dW_all = _mm(z0, dproj, ta=True, name="mlstm_proj_dw")
    dW_min = jnp.concatenate([unpad_heads(dW_all[:, M_OFF_Q:M_OFF_K]), unpad_heads(dW_all[:, M_OFF_K:M_OFF_V]),
                              dW_all[:, M_OFF_V:M_OFF_G], dW_all[:, M_OFF_G:M_OFF_G + 2 * M_HEADS]], axis=1)

    early_tags = ["m_w_in", "m_w_out"]
    early_part = sibling_sums(early_tags, [_cols_to_shards(dW_min), dW_mout.reshape(N_CHIPS, -1, D_MODEL)], "early")
    early_from = _exchange_chips(early_part, name="grad_exchange_chips_early")
    tags = early_tags + late_tags
    halves = [_sum_chips(q, r, place_arr, name=f"grad_sum_chips_{tag}")
              for tag, q, r in zip(tags, early_part + late_part, list(early_from) + list(late_from))]
    reduced = dict(zip(tags, _join_halves(halves, name="grad_join_halves")))
    layer_grads = dict(m_w_in=[reduced["m_w_in"]], m_w_out=[reduced["m_w_out"]], a_w_in=[reduced["a_w_in"]],
                       a_w_out=[reduced["a_w_out"]], f_w_up=[reduced["f_w_up0"], reduced["f_w_up1"]],
                       f_w_down=[reduced["f_w_down0"], reduced["f_w_down1"]])

    small_g = [
        dgbias[:, :2 * M_HEADS].reshape(1, 2, M_HEADS),
        dhn,
        dsinks.reshape(1, A_QH),
        db_in, db_out,
        jnp.concatenate(g_mpre), jnp.concatenate(g_mpost), jnp.concatenate(g_fpre), jnp.concatenate(g_post),
        jnp.stack(dconv_w), jnp.concatenate(dconv_b),
    ]
    small_names = ["m_gate_bias", "m_head_norm", "a_sinks", "a_b_in", "a_b_out", "norm_mix_pre", "norm_mix_post",
                   "norm_ffn_pre", "norm_ffn_post", "f_conv_w", "f_conv_b"]
    red = _unpack_small(_allreduce_small(_pack_small(small_g), name="reduce_small"), [t.shape for t in small_g])
    for n, t in zip(small_names, red):
        if n in ("a_b_in", "a_b_out", "f_conv_w"):
            axis = t.ndim - 1
            width = params[n].shape[axis]
            t = lax.dynamic_slice_in_dim(t, chip * width, width, axis)
        grads[n] = t

    deltas, new_m, new_v = {}, {}, {}
    for n in _BIG:
        grads[n], deltas[n], new_m[n], new_v[n] = _adamw_layers(params[n], layer_grads[n], mom1[n], mom2[n],
                                                                name=f"adamw_{n}")
    two = lambda t: t.reshape(-1, t.shape[-1])
    res = _adamw_small([(two(params[n]), two(grads[n]), two(mom1[n]), two(mom2[n])) for n in small_names],
                       name="adamw_small")
    for n, (d, nm, nv) in zip(small_names, res):
        sh = params[n].shape
        deltas[n], new_m[n], new_v[n] = d.reshape(sh), nm.reshape(sh), nv.reshape(sh)

    return (loss, grad_x[None], *[grads[n] for n in order], *[deltas[n] for n in order],
            *[new_m[n] for n in order], *[new_v[n] for n in order])
```

```python
import functools
import math

import jax
import jax.numpy as jnp
from jax import lax
from jax.experimental import pallas as pl
from jax.experimental.pallas import tpu as pltpu

F32 = jnp.float32
MXU_DTYPE = jnp.bfloat16
WIRE_DTYPE = jnp.bfloat16
MESH = pl.DeviceIdType.MESH

D_MODEL = 1024
EPS = 1e-6
M_HEADS, M_QK, M_V, M_CHUNK = 8, 64, 128, 128
GATE_CAP = 15.0
A_DH, A_QH, A_KVH, A_G, A_BLK = 64, 16, 2, 8, 128
D_FF = 2816
N_CHIPS = 4
LANES = 128
VMEM_LIMIT = 56 * 1024 * 1024

ADAM_LR, ADAM_B1, ADAM_B2, ADAM_EPS, ADAM_WD, ADAM_STEP = 0.001, 0.9, 0.999, 1e-08, 0.01, 10


def _cparams(sem):
    return pltpu.CompilerParams(dimension_semantics=sem, vmem_limit_bytes=VMEM_LIMIT)


def _pick(n, cands):
    for c in cands:
        if n % c == 0:
            return c
    return n


class _Rows:
    def __init__(self, tiles, vecs, outs, fn):
        self.tiles, self.vecs, self.outs, self.fn = list(tiles), list(vecs), list(outs), fn


ROWS_TILE = 512
ROWS_FULL_K = 4224


def _mm(a, b, *, ta=False, tb=False, out_dtype=F32, bias=None, add=None, b_layer=None, out_stacked=False, rows=None,
        name):
    if ta:
        K, M = a.shape
    else:
        M, K = a.shape
    b_stacked = b_layer is not None
    if b_stacked:
        assert not ta
        n_sh = b.shape[2]
        w_rows, w_cols = D_MODEL, N_CHIPS * n_sh
        N, Kb = (w_rows, w_cols) if tb else (w_cols, w_rows)
    elif tb:
        N, Kb = b.shape
    else:
        Kb, N = b.shape
    assert K == Kb, (a.shape, b.shape)
    tm = _pick(M, (1024, 1408, 512, 256, 128))
    tn = _pick(N, (1024, 1408, 1280, 512, 256, 128))
    tk = K if K <= 2816 else _pick(K, (2816, 2048, 1408, 1024, 512, 256, 128))
    shards_per_step = 1
    if b_stacked and tb:
        shards_per_step = 2
        tk = shards_per_step * n_sh
    if (b_stacked and not tb) or out_stacked:
        tn = N // N_CHIPS
    if rows is not None:
        assert tn == N and not out_stacked and not ta
        tm = min(tm, ROWS_TILE)
        if K <= ROWS_FULL_K:
            tk = K
    nk = K // tk
    dn = (((0 if ta else 1,), (1 if tb else 0,)), ((), ()))
    has_bias, has_add = bias is not None, add is not None
    n_tiles, n_vecs, n_outs = (len(rows.tiles), len(rows.vecs), len(rows.outs)) if rows is not None else (0, 0, 1)

    def body(*refs):
        a_ref, b_ref = refs[0], refs[1]
        pos = 2
        bias_ref = add_ref = None
        if has_bias:
            bias_ref = refs[pos]
            pos += 1
        if has_add:
            add_ref = refs[pos]
            pos += 1
        tile_refs, vec_refs = refs[pos:pos + n_tiles], refs[pos + n_tiles:pos + n_tiles + n_vecs]
        pos += n_tiles + n_vecs
        out_refs = refs[pos:pos + n_outs]
        acc_ref = refs[pos + n_outs] if nk > 1 else None

        def finish(r):
            if has_bias:
                r = r + bias_ref[...]
            if has_add:
                r = r + add_ref[...]
            if rows is None:
                out_refs[0][...] = r.astype(out_dtype)
                return
            vals = rows.fn(r, [t[...] for t in tile_refs], [v[...] for v in vec_refs])
            first = pl.program_id(0) == 0
            for (kind, _), o_ref, val in zip(rows.outs, out_refs, vals):
                if kind == "tile":
                    o_ref[...] = val.astype(o_ref.dtype)
                else:
                    @pl.when(first)
                    def _(o_ref=o_ref, val=val):
                        o_ref[...] = val

                    @pl.when(jnp.logical_not(first))
                    def _(o_ref=o_ref, val=val):
                        o_ref[...] += val

        if shards_per_step > 1:
            part = sum(lax.dot_general(a_ref[:, s * n_sh:(s + 1) * n_sh], b_ref[s], dn, preferred_element_type=F32)
                       for s in range(shards_per_step))
        else:
            part = lax.dot_general(a_ref[...], b_ref[...], dn, preferred_element_type=F32)
        if nk == 1:
            finish(part)
        else:
            k = pl.program_id(2)

            @pl.when(k == 0)
            def _():
                acc_ref[...] = part

            @pl.when(k > 0)
            def _():
                acc_ref[...] += part

            @pl.when(k == nk - 1)
            def _():
                finish(acc_ref[...])

    a_spec = pl.BlockSpec((tk, tm), lambda i, j, k: (k, i)) if ta else pl.BlockSpec((tm, tk), lambda i, j, k: (i, k))
    if b_stacked and tb:
        off = b_layer * (w_rows // tn)
        b_spec = pl.BlockSpec((shards_per_step, tn, n_sh), lambda i, j, k: (k, off + j, 0))
    elif b_stacked:
        off = b_layer * (w_rows // tk)
        b_spec = pl.BlockSpec((None, tk, tn), lambda i, j, k: (j, off + k, 0))
    elif tb:
        b_spec = pl.BlockSpec((tn, tk), lambda i, j, k: (j, k))
    else:
        b_spec = pl.BlockSpec((tk, tn), lambda i, j, k: (k, j))
    if out_stacked:
        out_spec = pl.BlockSpec((None, tm, tn), lambda i, j, k: (j, i, 0))
        out_shape = jax.ShapeDtypeStruct((N_CHIPS, M, tn), out_dtype)
    else:
        out_spec = pl.BlockSpec((tm, tn), lambda i, j, k: (i, j))
        out_shape = jax.ShapeDtypeStruct((M, N), out_dtype)
    in_specs, args = [a_spec, b_spec], [a, b]
    if has_bias:
        in_specs.append(pl.BlockSpec((1, tn), lambda i, j, k: (0, j)))
        args.append(bias)
    if has_add:
        in_specs.append(pl.BlockSpec((tm, tn), lambda i, j, k: (i, j)))
        args.append(add)
    semantics = ("parallel", "parallel", "arbitrary")
    if rows is not None:
        tile_spec = pl.BlockSpec((tm, N), lambda i, j, k: (i, 0))
        vec_spec = pl.BlockSpec((1, N), lambda i, j, k: (0, 0))
        in_specs += [tile_spec] * n_tiles + [vec_spec] * n_vecs
        args += rows.tiles + rows.vecs
        out_spec = [tile_spec if kind == "tile" else vec_spec for kind, _ in rows.outs]
        out_shape = [jax.ShapeDtypeStruct((M, N) if kind == "tile" else (1, N), dt) for kind, dt in rows.outs]
        semantics = ("arbitrary", "arbitrary", "arbitrary")
    return pl.pallas_call(
        body, name=name,
        grid=(M // tm, N // tn, nk),
        in_specs=in_specs,
        out_specs=out_spec,
        out_shape=out_shape,
        scratch_shapes=[pltpu.VMEM((tm, tn), F32)] if nk > 1 else [],
        compiler_params=_cparams(semantics),
    )(*args)


ROW_TILE = 512


def _rms_fwd(x, g, *, out_dtype, name):
    S, D = x.shape
    tm = _pick(S, (ROW_TILE, 256, 128))

    def body(x_ref, g_ref, o_ref):
        o_ref[...] = _rms(x_ref[...], g_ref[...]).astype(out_dtype)

    row = pl.BlockSpec((tm, D), lambda i: (i, 0))
    return pl.pallas_call(
        body, name=name, grid=(S // tm,), in_specs=[row, pl.BlockSpec((1, D), lambda i: (0, 0))], out_specs=row,
        out_shape=jax.ShapeDtypeStruct((S, D), out_dtype),
        compiler_params=_cparams(("parallel",)),
    )(x, g)


def _rms(x, g):
    return x * lax.rsqrt(jnp.mean(x * x, axis=-1, keepdims=True) + EPS) * g


def _rms_vjp(x, g, dy):
    r = lax.rsqrt(jnp.mean(x * x, axis=-1, keepdims=True) + EPS)
    xh = x * r
    gd = dy * g
    dx = r * (gd - xh * jnp.mean(gd * xh, axis=-1, keepdims=True))
    return dx, jnp.sum(dy * xh, axis=0, keepdims=True)


def _rows_sublayer_end(res, g_post, g_pre_next):
    def fn(z, tiles, vecs):
        h = tiles[0] + _rms(z, vecs[0])
        return [z, h, _rms(h, vecs[1])]

    return _Rows([res], [g_post, g_pre_next], [("tile", F32), ("tile", F32), ("tile", MXU_DTYPE)], fn)


def _rows_last_sublayer_end(res, g_post, target):
    def fn(z, tiles, vecs):
        y = tiles[0] + _rms(z, vecs[0])
        err = y - tiles[1]
        dy = err * (1.0 / err.shape[-1])
        dz, dg = _rms_vjp(z, vecs[0], dy)
        loss = 0.5 * jnp.sum(jnp.mean(err * err, axis=-1, keepdims=True), axis=0, keepdims=True)
        return [z, dy, dz, dg, jnp.broadcast_to(loss, dg.shape)]

    return _Rows([res, target], [g_post], [("tile", F32), ("tile", F32), ("tile", MXU_DTYPE), ("vec", F32), ("vec", F32)], fn)


def _rows_sublayer_start_bwd(x, res, z_below, g_pre, g_post_below):
    def fn(dz, tiles, vecs):
        dx, dg_pre = _rms_vjp(tiles[0], vecs[0], dz)
        dh = tiles[1] + dx
        dzb, dg_post = _rms_vjp(tiles[2], vecs[1], dh)
        return [dh, dg_pre, dzb, dg_post, jnp.sum(dzb, axis=0, keepdims=True)]

    return _Rows([x, res, z_below], [g_pre, g_post_below],
                 [("tile", F32), ("vec", F32), ("tile", MXU_DTYPE), ("vec", F32), ("vec", F32)], fn)


def _rows_first_sublayer_start_bwd(x, res, g_pre):
    def fn(dz, tiles, vecs):
        dx, dg_pre = _rms_vjp(tiles[0], vecs[0], dz)
        return [tiles[1] + dx, dg_pre]

    return _Rows([x, res], [g_pre], [("tile", F32), ("vec", F32)], fn)


def _mx(t):
    return t.astype(MXU_DTYPE)


def _mxf(t):
    return t.astype(MXU_DTYPE).astype(F32)


def _rawdot(a, b, ca, cb):
    return lax.dot_general(_mx(a), _mx(b), (((ca,), (cb,)), ((), ())), preferred_element_type=F32)


@functools.partial(jax.custom_vjp, nondiff_argnums=(2, 3))
def _dot(a, b, ca, cb):
    return _rawdot(a, b, ca, cb)


def _dot_fwd(a, b, ca, cb):
    return _rawdot(a, b, ca, cb), (a, b)


def _dot_bwd(ca, cb, res, g):
    a, b = res
    bj = 1 if cb == 0 else 0
    ai = 0 if ca == 1 else 1
    da = _rawdot(g, b, 1, bj) if ca == 1 else _rawdot(b, g, bj, 1)
    db = _rawdot(a, g, ai, 0) if cb == 0 else _rawdot(g, a, 0, ai)
    return da.astype(a.dtype), db.astype(b.dtype)


_dot.defvjp(_dot_fwd, _dot_bwd)


def _softcap(z):
    return GATE_CAP * jnp.tanh(z / GATE_CAP)


def _log_sigmoid(z):
    return jnp.minimum(z, 0.0) - jnp.log(1.0 + jnp.exp(-jnp.abs(z)))


def _sigmoid(z):
    return 0.5 * jnp.tanh(0.5 * z) + 0.5


def _lane_col(t, lane_index):
    lane = lax.broadcasted_iota(jnp.int32, t.shape, 1)
    return jnp.sum(jnp.where(lane == lane_index, t, 0.0), axis=1, keepdims=True)


def _mlstm_gates(G, bias):
    L = G.shape[0]
    z = _softcap(G + bias)
    ig = z
    lf = _log_sigmoid(z)
    ti = lax.broadcasted_iota(jnp.int32, (L, L), 0)
    si = lax.broadcasted_iota(jnp.int32, (L, L), 1)
    tril = (si <= ti).astype(F32)
    b = lax.dot_general(tril, lf, (((1,), (0,)), ((), ())), precision=lax.Precision.HIGHEST, preferred_element_type=F32)
    bL = jnp.sum(lf, axis=0, keepdims=True)
    return ig, b, ig.T, b.T, bL


def _mlstm_head(h, q, k, v, o, ig_all, b_all, igT, bT, bL_all, hn, C, n, m):
    L = q.shape[0]
    ti = lax.broadcasted_iota(jnp.int32, (L, L), 0)
    si = lax.broadcasted_iota(jnp.int32, (L, L), 1)
    lower = (si <= ti)
    ig = _lane_col(ig_all, h)
    b = _lane_col(b_all, M_HEADS + h)
    ig_row = igT[h:h + 1, :]
    b_row = bT[M_HEADS + h:M_HEADS + h + 1, :]
    bL = _lane_col(bL_all, M_HEADS + h)
    inter = b + m
    dlog = jnp.where(lower, b - b_row + ig_row, -jnp.inf)
    m_t = lax.stop_gradient(jnp.maximum(inter, jnp.max(dlog, axis=-1, keepdims=True)))
    qs = q * (M_QK ** -0.5)
    w = _dot(qs, k, 1, 1) * jnp.exp(dlog - m_t)
    s_inter = jnp.exp(inter - m_t)
    num = _dot(w, v, 1, 0) + s_inter * _dot(qs, C, 1, 0)
    den = jnp.sum(w, axis=-1, keepdims=True) + s_inter * jnp.sum(_mxf(qs) * _mxf(n), axis=-1, keepdims=True)
    hout = num * (1.0 / jnp.maximum(jnp.abs(den), jnp.exp(-m_t)))
    tail = bL - b + ig
    m_new = lax.stop_gradient(jnp.maximum(bL + m, jnp.max(tail, axis=0, keepdims=True)))
    ws = jnp.exp(tail - m_new)
    decay = jnp.exp(bL + m - m_new)
    wk = ws * k
    C_new = decay * C + _dot(wk, v, 0, 0)
    n_new = decay * n + jnp.sum(_mxf(ws) * _mxf(k), axis=0, keepdims=True)
    hs = hout * lax.rsqrt(jnp.mean(hout * hout, axis=-1, keepdims=True) + EPS) * hn
    gated = _sigmoid(o) * hs
    return (gated, C_new, n_new), m_new


M_OFF_Q, M_OFF_K, M_OFF_V, M_OFF_O, M_OFF_G = 0, M_HEADS * LANES, 2 * M_HEADS * LANES, 3 * M_HEADS * LANES, 4 * M_HEADS * LANES
M_PROJ = M_OFF_G + LANES


def _head_cols(off, h):
    return slice(off + h * LANES, off + (h + 1) * LANES)


def _mlstm_specs(NC, rev):
    H, L = M_HEADS, M_CHUNK
    cc = (lambda c: NC - 1 - c) if rev else (lambda c: c)
    proj = pl.BlockSpec((L, M_PROJ), lambda c: (cc(c), 0))
    vec = pl.BlockSpec((1, LANES), lambda c: (0, 0))
    hn = pl.BlockSpec((1, H * M_V), lambda c: (0, 0))
    hv = pl.BlockSpec((L, H * M_V), lambda c: (cc(c), 0))
    Cs = pl.BlockSpec((None, H, LANES, M_V), lambda c: (cc(c), 0, 0, 0))
    ns = pl.BlockSpec((None, H, 1, LANES), lambda c: (cc(c), 0, 0, 0))
    ms = pl.BlockSpec((None, H, 1, 1), lambda c: (cc(c), 0, 0, 0))
    return proj, vec, hn, hv, Cs, ns, ms


_MLSTM_STATE = [pltpu.VMEM((M_HEADS, LANES, M_V), F32), pltpu.VMEM((M_HEADS, 1, LANES), F32),
                pltpu.VMEM((M_HEADS, 1, 1), F32)]


def _mlstm_fwd(proj, bias, hn, shards, *, name):
    S = proj.shape[0]
    H, NC = M_HEADS, S // M_CHUNK
    ps, vec, hns, hv, Cs, ns, ms = _mlstm_specs(NC, False)
    nsh = len(shards)
    any_spec = pl.BlockSpec(memory_space=pl.ANY)

    def body(p_ref, b_ref, hn_ref, *rest):
        gated_ref, C_all, n_all, m_all = rest[nsh:nsh + 4]
        C_s, n_s, m_s, send_sems, recv_sems = rest[2 * nsh + 4:]
        gather = _Gather(rest[:nsh], rest[nsh + 4:2 * nsh + 4], send_sems, recv_sems)

        @pl.when(pl.program_id(0) == 0)
        def _():
            C_s[...] = jnp.zeros_like(C_s)
            n_s[...] = jnp.zeros_like(n_s)
            m_s[...] = jnp.zeros_like(m_s)
            gather.issue()

        gate_terms = _mlstm_gates(p_ref[:, M_OFF_G:M_OFF_G + LANES], b_ref[...])
        for h in range(H):
            C, n, m = C_s[h], n_s[h], m_s[h]
            C_all[h] = C
            n_all[h] = n
            m_all[h] = m
            (gated, Cn, nn), mn = _mlstm_head(
                h, p_ref[:, _head_cols(M_OFF_Q, h)], p_ref[:, _head_cols(M_OFF_K, h)], p_ref[:, _head_cols(M_OFF_V, h)],
                p_ref[:, _head_cols(M_OFF_O, h)], *gate_terms, hn_ref[:, _head_cols(0, h)], C, n, m)
            gated_ref[:, _head_cols(0, h)] = gated.astype(gated_ref.dtype)
            C_s[h] = Cn
            n_s[h] = nn
            m_s[h] = mn

        @pl.when(pl.program_id(0) == NC - 1)
        def _():
            gather.finish()

    return pl.pallas_call(
        body, name=name, grid=(NC,),
        in_specs=[ps, vec, hns] + [any_spec] * nsh,
        out_specs=[hv, Cs, ns, ms] + [any_spec] * nsh,
        out_shape=[jax.ShapeDtypeStruct((S, H * M_V), MXU_DTYPE),
                   jax.ShapeDtypeStruct((NC, H, LANES, M_V), F32),
                   jax.ShapeDtypeStruct((NC, H, 1, LANES), F32),
                   jax.ShapeDtypeStruct((NC, H, 1, 1), F32)] + _Gather.out_shape(shards),
        scratch_shapes=list(_MLSTM_STATE) + _Gather.semaphores(nsh),
        compiler_params=_cparams(("arbitrary",)),
    )(proj, bias, hn, *shards)


def _mlstm_bwd(proj, bias, hn, C_all, n_all, m_all, dgated, qs, *, name):
    S = proj.shape[0]
    H, NC = M_HEADS, S // M_CHUNK
    ps, vec, hns, hv, Cs, ns, ms = _mlstm_specs(NC, True)
    nq = len(qs)
    any_spec = pl.BlockSpec(memory_space=pl.ANY)

    def body(p_ref, b_ref, hn_ref, C_ref, n_ref, m_ref, dg_ref, *rest):
        dp_ref, db_ref, dhn_ref = rest[nq:nq + 3]
        dC_s, dn_s, send_sems, recv_sems = rest[2 * nq + 3:]
        exchange = _Exchange(rest[:nq], rest[nq + 3:2 * nq + 3], send_sems, recv_sems)

        @pl.when(pl.program_id(0) == 0)
        def _():
            dC_s[...] = jnp.zeros_like(dC_s)
            dn_s[...] = jnp.zeros_like(dn_s)
            db_ref[...] = jnp.zeros_like(db_ref)
            dhn_ref[...] = jnp.zeros_like(dhn_ref)
            exchange.issue()

        gate_terms, gates_vjp = jax.vjp(_mlstm_gates, p_ref[:, M_OFF_G:M_OFF_G + LANES], b_ref[...])
        d_terms = [jnp.zeros_like(t) for t in gate_terms]
        for h in range(H):
            def head(q, k, v, o, *rest, h=h):
                return _mlstm_head(h, q, k, v, o, *rest, m_ref[h])

            prim = (p_ref[:, _head_cols(M_OFF_Q, h)], p_ref[:, _head_cols(M_OFF_K, h)], p_ref[:, _head_cols(M_OFF_V, h)],
                    p_ref[:, _head_cols(M_OFF_O, h)], *gate_terms, hn_ref[:, _head_cols(0, h)], C_ref[h], n_ref[h])
            _, vjp, _ = jax.vjp(head, *prim, has_aux=True)
            dq, dk, dv, do, *d_gate, dhnh, dC, dn = vjp((dg_ref[:, _head_cols(0, h)].astype(F32), dC_s[h], dn_s[h]))
            dp_ref[:, _head_cols(M_OFF_Q, h)] = dq.astype(dp_ref.dtype)
            dp_ref[:, _head_cols(M_OFF_K, h)] = dk.astype(dp_ref.dtype)
            dp_ref[:, _head_cols(M_OFF_V, h)] = dv.astype(dp_ref.dtype)
            dp_ref[:, _head_cols(M_OFF_O, h)] = do.astype(dp_ref.dtype)
            d_terms = [a + g for a, g in zip(d_terms, d_gate)]
            dhn_ref[:, _head_cols(0, h)] += dhnh
            dC_s[h] = dC
            dn_s[h] = dn
        dG, dbias = gates_vjp(tuple(d_terms))
        dp_ref[:, M_OFF_G:M_OFF_G + LANES] = dG.astype(dp_ref.dtype)
        db_ref[...] += dbias

        @pl.when(pl.program_id(0) == NC - 1)
        def _():
            exchange.finish()

    return pl.pallas_call(
        body, name=name, grid=(NC,),
        in_specs=[ps, vec, hns, Cs, ns, ms, hv] + [any_spec] * nq,
        out_specs=[ps, vec, hns] + [any_spec] * nq,
        out_shape=[jax.ShapeDtypeStruct((S, M_PROJ), MXU_DTYPE), jax.ShapeDtypeStruct((1, LANES), F32),
                   jax.ShapeDtypeStruct((1, H * M_V), F32)] + _Exchange.out_shape(qs),
        scratch_shapes=list(_MLSTM_STATE[:2]) + _Exchange.semaphores(nq),
        compiler_params=_cparams(("arbitrary",)),
    )(proj, bias, hn, C_all, n_all, m_all, dgated, *qs)


A_NQ = A_QH * A_DH
A_NKV = 2 * A_KVH * A_DH
A_PAIRS = A_G // 2


def _sink_softmax_parts(s, sink):
    mx = jnp.maximum(jnp.max(s, axis=-1, keepdims=True), sink)
    e = jnp.exp(s - mx)
    e_sink = jnp.exp(sink - mx)
    r = 1.0 / (jnp.sum(e, axis=-1, keepdims=True) + e_sink)
    return e * r, e_sink * r


@jax.custom_vjp
def _sink_softmax(s, sink):
    return _sink_softmax_parts(s, sink)[0]


def _sink_softmax_fwd(s, sink):
    p, p_sink = _sink_softmax_parts(s, sink)
    return p, (p, p_sink)


def _sink_softmax_bwd(res, dp):
    p, p_sink = res
    t = jnp.sum(dp * p, axis=-1, keepdims=True)
    return p * (dp - t), -p_sink * t


_sink_softmax.defvjp(_sink_softmax_fwd, _sink_softmax_bwd)


def _attn_group(Ps, KLO, KHI, VLO, VHI, sinks, first):
    B = Ps[0].shape[0]
    R = len(Ps) * B
    q2 = jnp.concatenate(Ps, axis=0) * (A_DH ** -0.5)
    s = jnp.concatenate([_dot(q2, KLO, 1, 1), _dot(q2, KHI, 1, 1)], axis=0)
    qi = lax.broadcasted_iota(jnp.int32, (2 * R, 2 * B), 0) & (B - 1)
    ku = lax.broadcasted_iota(jnp.int32, (2 * R, 2 * B), 1)
    diff = qi - (ku - B)
    mask = (diff >= 0) & (diff < B) & ((ku >= B) | jnp.logical_not(first))
    s = jnp.where(mask, s, -jnp.inf)
    ri = lax.broadcasted_iota(jnp.int32, (2 * R, A_G), 0)
    head = 2 * lax.shift_right_logical(ri & (R - 1), B.bit_length() - 1) + lax.shift_right_logical(ri, R.bit_length() - 1)
    onehot = head == lax.broadcasted_iota(jnp.int32, (2 * R, A_G), 1)
    sink = jnp.sum(jnp.where(onehot, sinks, 0.0), axis=1, keepdims=True)
    p = _sink_softmax(s, sink)
    o = _dot(p[:R], VLO, 1, 0) + _dot(p[R:], VHI, 1, 0)
    return tuple(o[j * B:(j + 1) * B] for j in range(len(Ps)))


def _swap_halves_of_lanes(t):
    return pltpu.roll(t, LANES // 2, 1)


def _kv_operands(kvp_ref, kvc_ref, h):
    kk = jnp.concatenate([kvp_ref[:, :LANES], kvc_ref[:, :LANES]], axis=0)
    vv = jnp.concatenate([kvp_ref[:, LANES:], kvc_ref[:, LANES:]], axis=0)
    low = lax.broadcasted_iota(jnp.int32, kk.shape, 1) < A_DH
    own = low if h == 0 else jnp.logical_not(low)
    k_own = jnp.where(own, kk, 0.0)
    v_own = jnp.where(own, vv, 0.0)
    k_oth, v_oth = _swap_halves_of_lanes(k_own), _swap_halves_of_lanes(v_own)
    if h == 0:
        return own, k_own, k_oth, v_own, v_oth
    return own, k_oth, k_own, v_oth, v_own


def _pair_cols(h, j):
    c = (h * A_PAIRS + j) * LANES
    return slice(c, c + LANES)


def _attn_fwd(proj, sinks, *, name):
    S = proj.shape[0]
    NB = S // A_BLK
    kv_blk = A_NQ // A_NKV
    qs = pl.BlockSpec((A_BLK, A_NQ), lambda n: (n, 0))
    cur = pl.BlockSpec((A_BLK, A_NKV), lambda n: (n, kv_blk))
    prev = pl.BlockSpec((A_BLK, A_NKV), lambda n: (jnp.maximum(n - 1, 0), kv_blk))
    sk = pl.BlockSpec((A_KVH, A_G), lambda n: (0, 0))

    def body(q_ref, kvp_ref, kvc_ref, s_ref, o_ref):
        first = pl.program_id(0) == 0
        for h in range(A_KVH):
            _, KLO, KHI, VLO, VHI = _kv_operands(kvp_ref, kvc_ref, h)
            Ps = tuple(q_ref[:, _pair_cols(h, j)] for j in range(A_PAIRS))
            outs = _attn_group(Ps, KLO, KHI, VLO, VHI, s_ref[h:h + 1, :], first)
            for j in range(A_PAIRS):
                o_ref[:, _pair_cols(h, j)] = outs[j].astype(o_ref.dtype)

    return pl.pallas_call(
        body, name=name, grid=(NB,),
        in_specs=[qs, prev, cur, sk], out_specs=qs,
        out_shape=jax.ShapeDtypeStruct((S, A_NQ), MXU_DTYPE),
        compiler_params=_cparams(("parallel",)),
    )(proj, proj, proj, sinks)


def _attn_bwd(proj, sinks, do, *, name):
    S = proj.shape[0]
    NB = S // A_BLK
    last = NB - 1
    kv_blk = A_NQ // A_NKV
    qs = pl.BlockSpec((A_BLK, A_NQ), lambda n: (jnp.minimum(n, last), 0))
    cur = pl.BlockSpec((A_BLK, A_NKV), lambda n: (jnp.minimum(n, last), kv_blk))
    prev = pl.BlockSpec((A_BLK, A_NKV), lambda n: (jnp.clip(n - 1, 0, last), kv_blk))
    sk = pl.BlockSpec((A_KVH, A_G), lambda n: (0, 0))
    lag = pl.BlockSpec((A_BLK, A_NKV), lambda n: (jnp.maximum(n - 1, 0), 0))
    cq_spec = pl.BlockSpec((1, A_NQ), lambda n: (0, 0))
    ckv_spec = pl.BlockSpec((1, A_NKV), lambda n: (0, 0))

    def body(q_ref, kvp_ref, kvc_ref, s_ref, do_ref, dq_ref, dkv_ref, ds_ref, cq_ref, ckv_ref, keep):
        n = pl.program_id(0)

        @pl.when(n == 0)
        def _():
            keep[...] = jnp.zeros_like(keep)
            ds_ref[...] = jnp.zeros_like(ds_ref)
            cq_ref[...] = jnp.zeros_like(cq_ref)
            ckv_ref[...] = jnp.zeros_like(ckv_ref)

        @pl.when(n < NB)
        def _():
            f = functools.partial(_attn_group, first=n == 0)
            dk = jnp.zeros((2 * A_BLK, LANES), F32)
            dv = jnp.zeros((2 * A_BLK, LANES), F32)
            ds_rows = []
            for h in range(A_KVH):
                own, KLO, KHI, VLO, VHI = _kv_operands(kvp_ref, kvc_ref, h)
                Ps = tuple(q_ref[:, _pair_cols(h, j)] for j in range(A_PAIRS))
                _, vjp = jax.vjp(f, Ps, KLO, KHI, VLO, VHI, s_ref[h:h + 1, :])
                dPs, dKLO, dKHI, dVLO, dVHI, dsk = vjp(
                    tuple(do_ref[:, _pair_cols(h, j)].astype(F32) for j in range(A_PAIRS)))
                for j in range(A_PAIRS):
                    dq_ref[:, _pair_cols(h, j)] = dPs[j].astype(dq_ref.dtype)
                    cq_ref[:, _pair_cols(h, j)] += jnp.sum(dPs[j], axis=0, keepdims=True)
                dk_own, dk_oth = (dKLO, dKHI) if h == 0 else (dKHI, dKLO)
                dv_own, dv_oth = (dVLO, dVHI) if h == 0 else (dVHI, dVLO)
                dk = dk + jnp.where(own, dk_own, 0.0) + _swap_halves_of_lanes(jnp.where(own, 0.0, dk_oth))
                dv = dv + jnp.where(own, dv_own, 0.0) + _swap_halves_of_lanes(jnp.where(own, 0.0, dv_oth))
                ds_rows.append(dsk)
            ds_ref[...] += jnp.concatenate(ds_rows, axis=0)
            dkv = jnp.concatenate([dk, dv], axis=1)
            done = keep[...] + dkv[:A_BLK]
            dkv_ref[...] = done.astype(dkv_ref.dtype)
            ckv_ref[...] += jnp.sum(done, axis=0, keepdims=True)
            keep[...] = dkv[A_BLK:]

        @pl.when(n == NB)
        def _():
            done = keep[...]
            dkv_ref[...] = done.astype(dkv_ref.dtype)
            ckv_ref[...] += jnp.sum(done, axis=0, keepdims=True)

    return pl.pallas_call(
        body, name=name, grid=(NB + 1,),
        in_specs=[qs, prev, cur, sk, qs],
        out_specs=[qs, lag, sk, cq_spec, ckv_spec],
        out_shape=[jax.ShapeDtypeStruct((S, A_NQ), MXU_DTYPE), jax.ShapeDtypeStruct((S, A_NKV), MXU_DTYPE),
                   jax.ShapeDtypeStruct((A_KVH, A_G), F32), jax.ShapeDtypeStruct((1, A_NQ), F32),
                   jax.ShapeDtypeStruct((1, A_NKV), F32)],
        scratch_shapes=[pltpu.VMEM((A_BLK, A_NKV), F32)],
        compiler_params=_cparams(("arbitrary",)),
    )(proj, proj, proj, sinks, do)


HALO = 8


def _shift_rows(t, j):
    if j == 0:
        return t
    return pltpu.roll(t, j % t.shape[0], 0)


def _conv_gate(gate_ext, cw, cb):
    return cb + cw[0:1, :] * _shift_rows(gate_ext, 2) + cw[1:2, :] * _shift_rows(gate_ext, 1) + cw[2:3, :] * gate_ext


def _convgate_fwd(u, cw, cb, *, name):
    S, F2 = u.shape
    F = F2 // 2
    tm = _pick(S, (256, 128))
    hb = tm // HALO
    urow = pl.BlockSpec((tm, F2), lambda i: (i, 0))
    uprev = pl.BlockSpec((HALO, F), lambda i: (jnp.maximum(i * hb - 1, 0), 0))

    def body(u_ref, up_ref, cw_ref, cb_ref, a_ref):
        i = pl.program_id(0)
        gate = u_ref[:, :F]
        val = u_ref[:, F:]
        prev = jnp.where(i > 0, up_ref[...], 0.0)
        gc = _conv_gate(jnp.concatenate([prev, gate], axis=0), cw_ref[...], cb_ref[...])[HALO:]
        a_ref[...] = (gc * _sigmoid(gc) * val).astype(a_ref.dtype)

    return pl.pallas_call(
        body, name=name, grid=(S // tm,),
        in_specs=[urow, uprev, pl.BlockSpec((3, F), lambda i: (0, 0)), pl.BlockSpec((1, F), lambda i: (0, 0))],
        out_specs=pl.BlockSpec((tm, F), lambda i: (i, 0)),
        out_shape=jax.ShapeDtypeStruct((S, F), MXU_DTYPE),
        compiler_params=_cparams(("parallel",)),
    )(u, u, cw, cb)


def _convgate_bwd(u, da, cw, cb, gs, *, name):
    S, F2 = u.shape
    ng = len(gs)
    any_spec = pl.BlockSpec(memory_space=pl.ANY)
    F = F2 // 2
    tm = _pick(S, (128,))
    hb = tm // HALO
    nt = S // tm
    nh = S // HALO
    urow = pl.BlockSpec((tm, F2), lambda i: (i, 0))
    uprev = pl.BlockSpec((HALO, F), lambda i: (jnp.maximum(i * hb - 1, 0), 0))
    unext = pl.BlockSpec((HALO, F2), lambda i: (jnp.minimum((i + 1) * hb, nh - 1), 0))
    darow = pl.BlockSpec((tm, F), lambda i: (i, 0))
    danext = pl.BlockSpec((HALO, F), lambda i: (jnp.minimum((i + 1) * hb, nh - 1), 0))

    def body(u_ref, up_ref, un_ref, da_ref, dan_ref, cw_ref, cb_ref, *rest):
        du_ref, dcw_ref, dcb_ref = rest[ng:ng + 3]
        swap = _Swap(rest[:ng], rest[ng + 3:2 * ng + 3], *rest[2 * ng + 3:]) if ng else None
        i = pl.program_id(0)
        cwv = cw_ref[...]
        prev = jnp.where(i > 0, up_ref[...], 0.0)
        gate_ext = jnp.concatenate([prev, u_ref[:, :F], un_ref[:, :F]], axis=0)
        val_ext = jnp.concatenate([u_ref[:, F:], un_ref[:, F:]], axis=0)
        da_next = jnp.where(i < nt - 1, dan_ref[...].astype(F32), 0.0)
        da_ext = jnp.concatenate([da_ref[...].astype(F32), da_next], axis=0)
        gc = _conv_gate(gate_ext, cwv, cb_ref[...])[HALO:]
        sg = _sigmoid(gc)
        silu = gc * sg
        dval = da_ext * silu
        dgc = da_ext * val_ext * (sg * (1.0 + gc * (1.0 - sg)))
        dgate = cwv[2:3, :] * dgc + cwv[1:2, :] * _shift_rows(dgc, -1) + cwv[0:1, :] * _shift_rows(dgc, -2)
        du_ref[:, :F] = dgate[:tm].astype(du_ref.dtype)
        du_ref[:, F:] = dval[:tm].astype(du_ref.dtype)
        dgc_c = dgc[:tm]
        g0 = gate_ext[HALO:HALO + tm]
        g1 = _shift_rows(gate_ext, 1)[HALO:HALO + tm]
        g2 = _shift_rows(gate_ext, 2)[HALO:HALO + tm]
        dcw = jnp.concatenate([jnp.sum(dgc_c * g2, axis=0, keepdims=True),
                               jnp.sum(dgc_c * g1, axis=0, keepdims=True),
                               jnp.sum(dgc_c * g0, axis=0, keepdims=True)], axis=0)
        dcb = jnp.sum(dgc_c, axis=0, keepdims=True)

        @pl.when(i == 0)
        def _():
            dcw_ref[...] = dcw
            dcb_ref[...] = dcb
            if ng:
                swap.issue()

        @pl.when(i > 0)
        def _():
            dcw_ref[...] += dcw
            dcb_ref[...] += dcb

        if ng:
            @pl.when(i == nt - 1)
            def _():
                swap.finish()

    return pl.pallas_call(
        body, name=name, grid=(nt,),
        in_specs=[urow, uprev, unext, darow, danext,
                  pl.BlockSpec((3, F), lambda i: (0, 0)), pl.BlockSpec((1, F), lambda i: (0, 0))] + [any_spec] * ng,
        out_specs=[urow, pl.BlockSpec((3, F), lambda i: (0, 0)), pl.BlockSpec((1, F), lambda i: (0, 0))] + [any_spec] * ng,
        out_shape=[jax.ShapeDtypeStruct((S, F2), MXU_DTYPE), jax.ShapeDtypeStruct((3, F), F32),
                   jax.ShapeDtypeStruct((1, F), F32)] + (_Swap.out_shape(gs) if ng else []),
        scratch_shapes=_Swap.semaphores(ng) if ng else [],
        compiler_params=_cparams(("arbitrary",)),
    )(u, u, u, da, da, cw, cb, *gs)


def _adamw_math(w, g, m, v):
    m = ADAM_B1 * m + (1.0 - ADAM_B1) * g
    v = ADAM_B2 * v + (1.0 - ADAM_B2) * (g * g)
    m_hat = m / (1.0 - ADAM_B1 ** ADAM_STEP)
    v_hat = v / (1.0 - ADAM_B2 ** ADAM_STEP)
    delta = -ADAM_LR * (m_hat / (jnp.sqrt(v_hat) + ADAM_EPS) + ADAM_WD * w)
    return delta, m, v


def _adamw_layers(w, gs, m, v, *, name):
    Lr, R, C = w.shape
    tr = _pick(R, (256, 128, 64, 32, 16, 8))
    outs = None
    for layer, g in enumerate(gs):
        def body(w_ref, g_ref, m_ref, v_ref, *rest):
            go_ref, d_ref, nm_ref, nv_ref = rest[-4:]
            gv = g_ref[...]
            d, nm, nv = _adamw_math(w_ref[...], gv, m_ref[...], v_ref[...])
            go_ref[...] = gv
            d_ref[...] = d
            nm_ref[...] = nm
            nv_ref[...] = nv

        lay = pl.BlockSpec((None, tr, C), lambda i, layer=layer: (layer, i, 0))
        in_specs = [lay, pl.BlockSpec((tr, C), lambda i: (i, 0)), lay, lay]
        args = [w, g, m, v]
        aliases = {}
        if outs is not None:
            in_specs += [pl.BlockSpec(memory_space=pl.ANY)] * 4
            args += list(outs)
            aliases = {4 + t: t for t in range(4)}
        outs = pl.pallas_call(
            body, name=f"{name}_{layer}", grid=(R // tr,), in_specs=in_specs, out_specs=[lay] * 4,
            out_shape=[jax.ShapeDtypeStruct((Lr, R, C), F32)] * 4, input_output_aliases=aliases,
            compiler_params=_cparams(("parallel",)),
        )(*args)
    return outs


def _adamw_small(items, *, name):
    n = len(items)

    def body(*refs):
        ins, outs = refs[:4 * n], refs[4 * n:]
        for t in range(n):
            w, g, m, v = (r[...] for r in ins[4 * t:4 * t + 4])
            d, nm, nv = _adamw_math(w, g, m, v)
            outs[3 * t][...] = d
            outs[3 * t + 1][...] = nm
            outs[3 * t + 2][...] = nv

    flat = [a for it in items for a in it]
    out_shape = [jax.ShapeDtypeStruct(it[0].shape, F32) for it in items for _ in range(3)]
    vm = pl.BlockSpec(memory_space=pltpu.VMEM)
    res = pl.pallas_call(body, name=name, in_specs=[vm] * len(flat), out_specs=[vm] * len(out_shape),
                         out_shape=out_shape)(*flat)
    return [tuple(res[3 * t:3 * t + 3]) for t in range(n)]


def _place():
    return lax.axis_index("x"), lax.axis_index("y"), lax.axis_index("c")


_FLIPS = ((1, 0), (0, 1), (1, 1))


ROW_ALIGN = 16


def _half(rows, which):
    return pl.ds(pl.multiple_of(which * (rows // 2), ROW_ALIGN), rows // 2)


def _remote(src, dst, send_sems, recv_sems, k, to):
    return pltpu.make_async_remote_copy(src_ref=src, dst_ref=dst, send_sem=send_sems.at[k], recv_sem=recv_sems.at[k],
                                        device_id=to, device_id_type=MESH)


def _gather_shards(shards, *, name):
    n = len(shards)
    any_spec = pl.BlockSpec(memory_space=pl.ANY)

    def body(*refs):
        gather = _Gather(refs[:n], refs[n:2 * n], refs[2 * n], refs[2 * n + 1])
        gather.issue()
        gather.finish()

    return pl.pallas_call(
        body, name=name, in_specs=[any_spec] * n, out_specs=[any_spec] * n,
        out_shape=_Gather.out_shape(shards), scratch_shapes=_Gather.semaphores(n),
    )(*shards)


class _Gather:
    def __init__(self, w_refs, out_refs, send_sems, recv_sems):
        self.w_refs, self.out_refs, self.send_sems, self.recv_sems = w_refs, out_refs, send_sems, recv_sems
        self.pairs = [(i, j) for i in range(len(w_refs)) for j in range(3)]

    @staticmethod
    def out_shape(shards):
        return [jax.ShapeDtypeStruct((N_CHIPS,) + s.shape, s.dtype) for s in shards]

    @staticmethod
    def semaphores(n):
        return [pltpu.SemaphoreType.DMA((6 * n,)), pltpu.SemaphoreType.DMA((6 * n,))]

    def _where(self):
        x, y, c = _place()
        return x, y, c, [(x ^ fx, y ^ fy) for fx, fy in _FLIPS]

    def _over_ici(self, i, j, landed):
        x, y, c, chips = self._where()
        px, py = chips[j]
        mine = _half(self.w_refs[i].shape[0], c)
        if landed:
            src = dst = self.out_refs[i].at[2 * px + py, mine]
        else:
            src, dst = self.w_refs[i].at[mine], self.out_refs[i].at[2 * x + y, mine]
        return _remote(src, dst, self.send_sems, self.recv_sems, 6 * i + j, (px, py, c))

    def _over_d2d(self, i, j, which):
        x, y, c, chips = self._where()
        px, py = chips[j]
        blk = self.out_refs[i].at[2 * px + py, _half(self.w_refs[i].shape[0], which)]
        return _remote(blk, blk, self.send_sems, self.recv_sems, 6 * i + 3 + j, (x, y, 1 - c))

    def issue(self):
        for i, j in self.pairs:
            self._over_ici(i, j, False).start()

    def finish(self):
        c = lax.axis_index("c")
        for i, j in self.pairs:
            self._over_ici(i, j, True).wait_recv()
            self._over_d2d(i, j, c).start()
        for i, j in self.pairs:
            self._over_d2d(i, j, 1 - c).wait_recv()
        for i, j in self.pairs:
            self._over_ici(i, j, False).wait_send()
            self._over_d2d(i, j, c).wait_send()


def _swap_halves(gs, *, name):
    n = len(gs)
    any_spec = pl.BlockSpec(memory_space=pl.ANY)

    def body(*refs):
        swap = _Swap(refs[:n], refs[n:2 * n], refs[2 * n], refs[2 * n + 1])
        swap.issue()
        swap.finish()

    return pl.pallas_call(
        body, name=name, in_specs=[any_spec] * n, out_specs=[any_spec] * n,
        out_shape=_Swap.out_shape(gs), scratch_shapes=_Swap.semaphores(n),
    )(*gs)


class _Swap:
    def __init__(self, g_refs, out_refs, send_sems, recv_sems):
        self.g_refs, self.out_refs, self.send_sems, self.recv_sems = g_refs, out_refs, send_sems, recv_sems

    @staticmethod
    def out_shape(gs):
        return [jax.ShapeDtypeStruct((N_CHIPS, g.shape[1] // 2, g.shape[2]), g.dtype) for g in gs]

    @staticmethod
    def semaphores(n):
        return [pltpu.SemaphoreType.DMA((n,)), pltpu.SemaphoreType.DMA((n,))]

    def _copies(self):
        x, y, c = _place()
        return [_remote(g.at[:, _half(g.shape[1], 1 - c)], out, self.send_sems, self.recv_sems, i, (x, y, 1 - c))
                for i, (g, out) in enumerate(zip(self.g_refs, self.out_refs))]

    def issue(self):
        for cp in self._copies():
            cp.start()

    def finish(self):
        for cp in self._copies():
            cp.wait()


def _add_halves(g, got, c_arr, *, name):
    _, rows, cols = g.shape
    blk = (None, rows // 2, cols)

    def body(c_ref, g_ref, got_ref, o_ref):
        o_ref[...] = (g_ref[...] + got_ref[...]).astype(o_ref.dtype)

    return pl.pallas_call(
        body, name=name,
        grid_spec=pltpu.PrefetchScalarGridSpec(
            num_scalar_prefetch=1, grid=(N_CHIPS,),
            in_specs=[pl.BlockSpec(blk, lambda j, c_ref: (j, c_ref[0], 0)), pl.BlockSpec(blk, lambda j, c_ref: (j, 0, 0))],
            out_specs=pl.BlockSpec(blk, lambda j, c_ref: (j, 0, 0))),
        out_shape=jax.ShapeDtypeStruct((N_CHIPS, rows // 2, cols), WIRE_DTYPE),
        compiler_params=_cparams(("parallel",)),
    )(c_arr, g, got)


def _exchange_chips(qs, *, name):
    n = len(qs)
    any_spec = pl.BlockSpec(memory_space=pl.ANY)

    def body(*refs):
        exchange = _Exchange(refs[:n], refs[n:2 * n], refs[2 * n], refs[2 * n + 1])
        exchange.issue()
        exchange.finish()

    return pl.pallas_call(
        body, name=name, in_specs=[any_spec] * n, out_specs=[any_spec] * n,
        out_shape=_Exchange.out_shape(qs), scratch_shapes=_Exchange.semaphores(n),
    )(*qs)


class _Exchange:
    def __init__(self, q_refs, out_refs, send_sems, recv_sems):
        self.q_refs, self.out_refs, self.send_sems, self.recv_sems = q_refs, out_refs, send_sems, recv_sems
        self.pairs = [(i, j) for i in range(len(q_refs)) for j in range(3)]

    @staticmethod
    def out_shape(qs):
        return [jax.ShapeDtypeStruct(q.shape, q.dtype) for q in qs]

    @staticmethod
    def semaphores(n):
        return [pltpu.SemaphoreType.DMA((3 * n,)), pltpu.SemaphoreType.DMA((3 * n,))]

    def _copy(self, i, j, landed):
        x, y, c = _place()
        px, py = [(x ^ fx, y ^ fy) for fx, fy in _FLIPS][j]
        if landed:
            src = dst = self.out_refs[i].at[2 * px + py]
        else:
            src, dst = self.q_refs[i].at[2 * px + py], self.out_refs[i].at[2 * x + y]
        return _remote(src, dst, self.send_sems, self.recv_sems, 3 * i + j, (px, py, c))

    def issue(self):
        for i, j in self.pairs:
            self._copy(i, j, False).start()

    def finish(self):
        for i, j in self.pairs:
            self._copy(i, j, True).wait_recv()
        for i, j in self.pairs:
            self._copy(i, j, False).wait_send()


def _sum_chips(q, r, place_arr, *, name):
    _, h, cols = q.shape
    blk = (None, h, cols)

    def body(p_ref, q_ref, r1_ref, r2_ref, r3_ref, o_ref):
        o_ref[...] = ((q_ref[...].astype(F32) + r1_ref[...].astype(F32)) + r2_ref[...].astype(F32)) + r3_ref[...].astype(F32)

    other = [pl.BlockSpec(blk, lambda i, p_ref, f=f: (p_ref[0] ^ f, 0, 0)) for f in (1, 2, 3)]
    return pl.pallas_call(
        body, name=name,
        grid_spec=pltpu.PrefetchScalarGridSpec(
            num_scalar_prefetch=1, grid=(1,),
            in_specs=[pl.BlockSpec(blk, lambda i, p_ref: (p_ref[0], 0, 0))] + other,
            out_specs=pl.BlockSpec((h, cols), lambda i, p_ref: (p_ref[1], 0))),
        out_shape=jax.ShapeDtypeStruct((2 * h, cols), F32),
        compiler_params=_cparams(("arbitrary",)),
    )(place_arr, q, r, r, r)


def _join_halves(fs, *, name):
    n = len(fs)
    any_spec = pl.BlockSpec(memory_space=pl.ANY)

    def body(*refs):
        out_refs, send_sems, recv_sems = refs[n:2 * n], refs[2 * n], refs[2 * n + 1]
        x, y, c = _place()
        sent = []
        for i in range(n):
            mine = out_refs[i].at[_half(fs[i].shape[0], c)]
            cp = _remote(mine, mine, send_sems, recv_sems, i, (x, y, 1 - c))
            cp.start()
            sent.append(cp)
        for i in range(n):
            its = out_refs[i].at[_half(fs[i].shape[0], 1 - c)]
            _remote(its, its, send_sems, recv_sems, i, (x, y, 1 - c)).wait_recv()
        for cp in sent:
            cp.wait_send()

    return pl.pallas_call(
        body, name=name, in_specs=[any_spec] * n, out_specs=[any_spec] * n,
        out_shape=[jax.ShapeDtypeStruct(f.shape, f.dtype) for f in fs], input_output_aliases={i: i for i in range(n)},
        scratch_shapes=[pltpu.SemaphoreType.DMA((n,)), pltpu.SemaphoreType.DMA((n,))],
    )(*fs)


def _allreduce_small(buf, *, name):
    R = buf.shape[0]
    vm = pl.BlockSpec(memory_space=pltpu.VMEM)

    def body(b_ref, o_ref, slots, send_sems, recv_sems):
        x, y, c = _place()
        me = 4 * x + 2 * y + c
        slots[me] = b_ref[...]
        sends = []
        for kk in range(1, 8):
            fx, fy, fc = (kk >> 2) & 1, (kk >> 1) & 1, kk & 1
            cp = pltpu.make_async_remote_copy(
                src_ref=b_ref, dst_ref=slots.at[me], send_sem=send_sems.at[kk - 1], recv_sem=recv_sems.at[kk - 1],
                device_id=(x ^ fx, y ^ fy, c ^ fc), device_id_type=MESH)
            cp.start()
            sends.append(cp)
        for kk in range(1, 8):
            fx, fy, fc = (kk >> 2) & 1, (kk >> 1) & 1, kk & 1
            peer = 4 * (x ^ fx) + 2 * (y ^ fy) + (c ^ fc)
            pltpu.make_async_remote_copy(
                src_ref=b_ref, dst_ref=slots.at[peer], send_sem=send_sems.at[kk - 1], recv_sem=recv_sems.at[kk - 1],
                device_id=(x ^ fx, y ^ fy, c ^ fc), device_id_type=MESH).wait_recv()
        for cp in sends:
            cp.wait_send()
        acc = slots[0]
        for d in range(1, 8):
            acc = acc + slots[d]
        o_ref[...] = acc

    return pl.pallas_call(
        body, name=name, in_specs=[vm], out_specs=vm,
        out_shape=jax.ShapeDtypeStruct((R, LANES), F32),
        scratch_shapes=[pltpu.VMEM((8, R, LANES), F32), pltpu.SemaphoreType.DMA((7,)), pltpu.SemaphoreType.DMA((7,))],
    )(buf)


def _pad_rows(v, mult=8 * LANES):
    flat = v.reshape(-1)
    n = flat.shape[0]
    tot = -(-n // mult) * mult
    return jnp.pad(flat, (0, tot - n)).reshape(-1, LANES)


def _pack_small(parts):
    return jnp.concatenate([_pad_rows(p.astype(F32)) for p in parts], axis=0)


def _unpack_small(buf, shapes):
    out, r = [], 0
    for sh in shapes:
        n = math.prod(sh)
        rows = -(-n // (8 * LANES)) * 8
        out.append(buf[r:r + rows].reshape(-1)[:n].reshape(sh))
        r += rows
    return out


def _cols_to_shards(w):
    *lead, K, N = w.shape
    t = w.reshape(*lead, K, N_CHIPS, N // N_CHIPS)
    return jnp.moveaxis(t, -2, 0)


def _shards_to_cols(t):
    t = jnp.moveaxis(t, 0, -2)
    *lead, K, _, n = t.shape
    return t.reshape(*lead, K, N_CHIPS * n)


_BIG = ("m_w_in", "m_w_out", "a_w_in", "a_w_out", "f_w_up", "f_w_down")


def kernel(x, m_w_in, m_gate_bias, m_head_norm, m_w_out, a_w_in, a_b_in, a_sinks, a_w_out, a_b_out, norm_mix_pre, norm_mix_post, norm_ffn_pre, norm_ffn_post, f_w_up, f_conv_w, f_conv_b, f_w_down, loss_target, m_m_w_in, m_m_gate_bias, m_m_head_norm, m_m_w_out, m_a_w_in, m_a_b_in, m_a_sinks, m_a_w_out, m_a_b_out, m_norm_mix_pre, m_norm_mix_post, m_norm_ffn_pre, m_norm_ffn_post, m_f_w_up, m_f_conv_w, m_f_conv_b, m_f_w_down, v_m_w_in, v_m_gate_bias, v_m_head_norm, v_m_w_out, v_a_w_in, v_a_b_in, v_a_sinks, v_a_w_out, v_a_b_out, v_norm_mix_pre, v_norm_mix_post, v_norm_ffn_pre, v_norm_ffn_post, v_f_w_up, v_f_conv_w, v_f_conv_b, v_f_w_down):
    params = dict(m_w_in=m_w_in, m_gate_bias=m_gate_bias, m_head_norm=m_head_norm, m_w_out=m_w_out, a_w_in=a_w_in,
                  a_b_in=a_b_in, a_sinks=a_sinks, a_w_out=a_w_out, a_b_out=a_b_out, norm_mix_pre=norm_mix_pre,
                  norm_mix_post=norm_mix_post, norm_ffn_pre=norm_ffn_pre, norm_ffn_post=norm_ffn_post, f_w_up=f_w_up,
                  f_conv_w=f_conv_w, f_conv_b=f_conv_b, f_w_down=f_w_down)
    mom1 = dict(m_w_in=m_m_w_in, m_gate_bias=m_m_gate_bias, m_head_norm=m_m_head_norm, m_w_out=m_m_w_out,
                a_w_in=m_a_w_in, a_b_in=m_a_b_in, a_sinks=m_a_sinks, a_w_out=m_a_w_out, a_b_out=m_a_b_out,
                norm_mix_pre=m_norm_mix_pre, norm_mix_post=m_norm_mix_post, norm_ffn_pre=m_norm_ffn_pre,
                norm_ffn_post=m_norm_ffn_post, f_w_up=m_f_w_up, f_conv_w=m_f_conv_w, f_conv_b=m_f_conv_b,
                f_w_down=m_f_w_down)
    mom2 = dict(m_w_in=v_m_w_in, m_gate_bias=v_m_gate_bias, m_head_norm=v_m_head_norm, m_w_out=v_m_w_out,
                a_w_in=v_a_w_in, a_b_in=v_a_b_in, a_sinks=v_a_sinks, a_w_out=v_a_w_out, a_b_out=v_a_b_out,
                norm_mix_pre=v_norm_mix_pre, norm_mix_post=v_norm_mix_post, norm_ffn_pre=v_norm_ffn_pre,
                norm_ffn_post=v_norm_ffn_post, f_w_up=v_f_w_up, f_conv_w=v_f_conv_w, f_conv_b=v_f_conv_b,
                f_w_down=v_f_w_down)
    order = list(params)

    mx, my, mc = _place()
    chip = 2 * mx + my
    h0 = x[0]
    target = loss_target[0]

    def two_d(t):
        return t.reshape(-1, t.shape[-1])

    early = ("m_w_in", "m_w_out")
    late = tuple(n for n in _BIG if n not in early)
    mine = {n: two_d(params[n]).astype(MXU_DTYPE) for n in _BIG}

    def with_own_slot(names, theirs):
        return {n: lax.dynamic_update_slice(t, mine[n][None], (chip, 0, 0)) for n, t in zip(names, theirs)}

    gathered = with_own_slot(early, _gather_shards([mine[n] for n in early], name="gather_weights_early"))

    def in_place(shard, axis):
        width = shard.shape[axis]
        z = jnp.zeros(shard.shape[:axis] + (N_CHIPS * width,) + shard.shape[axis + 1:], F32)
        contrib = jnp.where(mc == 0, shard, 0.0)
        return lax.dynamic_update_slice_in_dim(z, contrib, chip * width, axis)

    sm_in = [in_place(a_b_in, 1), in_place(a_b_out, 1), in_place(f_conv_w, 2)]
    sm_full = _unpack_small(_allreduce_small(_pack_small(sm_in), name="gather_small"), [t.shape for t in sm_in])
    b_in_full, b_out_full, conv_w_full = sm_full

    W_in = _shards_to_cols(gathered["m_w_in"])
    nqk = M_HEADS * M_QK
    n_main = 2 * nqk + M_HEADS * M_V + D_MODEL

    def pad_heads(w):
        return jnp.pad(w.reshape(D_MODEL, M_HEADS, M_QK), ((0, 0), (0, 0), (0, LANES - M_QK))).reshape(D_MODEL, -1)

    def unpad_heads(w):
        return w.reshape(D_MODEL, M_HEADS, LANES)[:, :, :M_QK].reshape(D_MODEL, nqk)

    W_all = jnp.concatenate([pad_heads(W_in[:, :nqk]), pad_heads(W_in[:, nqk:2 * nqk]), W_in[:, 2 * nqk:n_main],
                             jnp.pad(W_in[:, n_main:], ((0, 0), (0, LANES - 2 * M_HEADS)))], axis=1)
    gbias = jnp.pad(m_gate_bias[0].reshape(1, 2 * M_HEADS), ((0, 0), (0, LANES - 2 * M_HEADS)))
    W_mout = gathered["m_w_out"].reshape(D_MODEL, D_MODEL)

    grads = {}

    def ffn_fwd(i, z1, end):
        u = _mm(z1, W_up, b_layer=i, name=f"ffn_up{i}")
        a = _convgate_fwd(u, conv_w_full[i], f_conv_b[i:i + 1], name=f"ffn_act{i}")
        return u, a, _mm(a, W_down[i], rows=end, name=f"ffn_down{i}")

    z0 = _rms_fwd(h0, norm_mix_pre[0:1], out_dtype=MXU_DTYPE, name="mix_pre_norm0")
    proj = _mm(z0, W_all, name="mlstm_proj")
    gated, C_all, n_all, m_all, *late_theirs = _mlstm_fwd(proj, gbias, m_head_norm, [mine[n] for n in late],
                                                          name="mlstm_fwd")
    gathered.update(with_own_slot(late, late_theirs))
    A_in, A_out = _shards_to_cols(gathered["a_w_in"]), gathered["a_w_out"].reshape(D_MODEL, D_MODEL)
    W_up = gathered["f_w_up"]
    dsh = D_FF // N_CHIPS
    W_down = [gathered["f_w_down"][:, i * dsh:(i + 1) * dsh].reshape(D_FF, D_MODEL) for i in range(2)]
    zm0, h1, z1 = _mm(gated, W_mout, rows=_rows_sublayer_end(h0, norm_mix_post[0:1], norm_ffn_pre[0:1]), name="mlstm_out")
    u0, a0, (zf0, h2, y0) = ffn_fwd(0, z1, _rows_sublayer_end(h1, norm_ffn_post[0:1], norm_mix_pre[1:2]))

    aproj = _mm(y0, A_in, bias=b_in_full, name="attn_proj")
    sinks = a_sinks.reshape(A_KVH, A_G)
    ao = _attn_fwd(aproj, sinks, name="attn_fwd")
    zm1, h3, z3 = _mm(ao, A_out, bias=b_out_full, rows=_rows_sublayer_end(h2, norm_mix_post[1:2], norm_ffn_pre[1:2]),
                      name="attn_out")
    u1, a1, (zf1, dh, dzf1, g_post1, loss_lanes) = ffn_fwd(1, z3, _rows_last_sublayer_end(h3, norm_ffn_post[1:2], target))
    loss = lax.psum(loss_lanes[0, 0], ("x", "y", "c"))

    g_post, g_fpre, g_mpost, g_mpre = [None, g_post1], [None, None], [None, None], [None, None]
    dW_up, dW_down, dconv_w, dconv_b = [None, None], [None, None], [None, None], [None, None]

    def by_rows(g):
        return g.reshape(N_CHIPS, -1, g.shape[-1])

    def ffn_bwd(i, dzf, z1_, u, a, start, ride):
        da = _mm(dzf, W_down[i], tb=True, name=f"ffn_down_dx{i}")
        dW_down[i] = _mm(a, dzf, ta=True, name=f"ffn_down_dw{i}")
        riders = (list(ride) + [by_rows(dW_down[i])]) if ride is not None else []
        du, dconv_w[i], dconv_b[i], *got = _convgate_bwd(u, da, conv_w_full[i], f_conv_b[i:i + 1], riders,
                                                         name=f"ffn_act_bwd{i}")
        dW_up[i] = _mm(z1_, du, ta=True, out_stacked=True, name=f"ffn_up_dw{i}")
        return _mm(du, W_up, tb=True, b_layer=i, rows=start, name=f"ffn_up_dx{i}"), riders, got

    (dh3, g_fpre[1], dzm1, g_mpost[1], db_out), _, _ = ffn_bwd(
        1, dzf1, z3, u1, a1, _rows_sublayer_start_bwd(h3, dh, zm1, norm_ffn_pre[1:2], norm_mix_post[1:2]), None)
    dao = _mm(dzm1, A_out, tb=True, name="attn_out_dx")
    dA_out = _mm(ao, dzm1, ta=True, name="attn_out_dw")
    daq, dakv, dsinks, cs_q, cs_kv = _attn_bwd(aproj, sinks, dao, name="attn_bwd")
    db_in = jnp.concatenate([cs_q, cs_kv], axis=1)
    dh2, g_mpre[1], dzf0, g_post[0], _ = _mm(
        daq, A_in[:, :A_NQ], tb=True, add=_mm(dakv, A_in[:, A_NQ:], tb=True, name="attn_proj_kv_dx"),
        rows=_rows_sublayer_start_bwd(h2, dh3, zf0, norm_mix_pre[1:2], norm_ffn_post[0:1]), name="attn_proj_q_dx")
    dA_in = jnp.concatenate([_mm(y0, daq, ta=True, name="attn_proj_q_dw"),
                             _mm(y0, dakv, ta=True, name="attn_proj_kv_dw")], axis=1)

    c_arr = jnp.reshape(mc, (1,)).astype(jnp.int32)
    place_arr = jnp.stack([chip, mc]).astype(jnp.int32)

    def add_halves(tags, gs, got):
        return [_add_halves(g, t, c_arr, name=f"grad_add_halves_{tag}") for tag, g, t in zip(tags, gs, got)]

    (dh1, g_fpre[0], dzm0, g_mpost[0], _), rode, rode_got = ffn_bwd(
        0, dzf0, z1, u0, a0, _rows_sublayer_start_bwd(h1, dh2, zm0, norm_ffn_pre[0:1], norm_mix_post[0:1]),
        [_cols_to_shards(dA_in), by_rows(dA_out), dW_up[1], by_rows(dW_down[1])])
    rode_tags = ["a_w_in", "a_w_out", "f_w_up1", "f_w_down1", "f_w_down0"]
    dgated = _mm(dzm0, W_mout, tb=True, name="mlstm_out_dx")
    dW_mout = _mm(gated, dzm0, ta=True, name="mlstm_out_dw")
    next_tags, next_gs = ["f_w_up0", "m_w_out"], [dW_up[0], by_rows(dW_mout)]
    late_tags = rode_tags + next_tags
    late_part = (add_halves(rode_tags, rode, rode_got)
                 + add_halves(next_tags, next_gs, _swap_halves(next_gs, name="grad_swap_halves_late")))
    dproj, dgbias, dhn, *late_from = _mlstm_bwd(proj, gbias, m_head_norm, C_all, n_all, m_all, dgated, late_part,
                                                name="mlstm_bwd")
    grad_x, g_mpre[0] = _mm(dproj, W_all, tb=True, rows=_rows_first_sublayer_start_bwd(h0, dh1, norm_mix_pre[0:1]),
                            name="mlstm_proj_dx")
    dW_all = _mm(z0, dproj, ta=True, name="mlstm_proj_dw")
    dW_min = jnp.concatenate([unpad_heads(dW_all[:, M_OFF_Q:M_OFF_K]), unpad_heads(dW_all[:, M_OFF_K:M_OFF_V]),
                              dW_all[:, M_OFF_V:M_OFF_G], dW_all[:, M_OFF_G:M_OFF_G + 2 * M_HEADS]], axis=1)

    early_tags, early_gs = ["m_w_in"], [_cols_to_shards(dW_min)]
    early_part = add_halves(early_tags, early_gs, _swap_halves(early_gs, name="grad_swap_halves_early"))
    early_from = _exchange_chips(early_part, name="grad_exchange_chips_early")
    tags = early_tags + late_tags
    halves = [_sum_chips(q, r, place_arr, name=f"grad_sum_chips_{tag}")
              for tag, q, r in zip(tags, early_part + late_part, list(early_from) + list(late_from))]
    reduced = dict(zip(tags, _join_halves(halves, name="grad_join_halves")))
    layer_grads = dict(m_w_in=[reduced["m_w_in"]], m_w_out=[reduced["m_w_out"]], a_w_in=[reduced["a_w_in"]],
                       a_w_out=[reduced["a_w_out"]], f_w_up=[reduced["f_w_up0"], reduced["f_w_up1"]],
                       f_w_down=[reduced["f_w_down0"], reduced["f_w_down1"]])

    small_g = [
        dgbias[:, :2 * M_HEADS].reshape(1, 2, M_HEADS),
        dhn,
        dsinks.reshape(1, A_QH),
        db_in, db_out,
        jnp.concatenate(g_mpre), jnp.concatenate(g_mpost), jnp.concatenate(g_fpre), jnp.concatenate(g_post),
        jnp.stack(dconv_w), jnp.concatenate(dconv_b),
    ]
    small_names = ["m_gate_bias", "m_head_norm", "a_sinks", "a_b_in", "a_b_out", "norm_mix_pre", "norm_mix_post",
                   "norm_ffn_pre", "norm_ffn_post", "f_conv_w", "f_conv_b"]
    red = _unpack_small(_allreduce_small(_pack_small(small_g), name="reduce_small"), [t.shape for t in small_g])
    for n, t in zip(small_names, red):
        if n in ("a_b_in", "a_b_out", "f_conv_w"):
            axis = t.ndim - 1
            width = params[n].shape[axis]
            t = lax.dynamic_slice_in_dim(t, chip * width, width, axis)
        grads[n] = t

    deltas, new_m, new_v = {}, {}, {}
    for n in _BIG:
        grads[n], deltas[n], new_m[n], new_v[n] = _adamw_layers(params[n], layer_grads[n], mom1[n], mom2[n],
                                                                name=f"adamw_{n}")
    two = lambda t: t.reshape(-1, t.shape[-1])
    res = _adamw_small([(two(params[n]), two(grads[n]), two(mom1[n]), two(mom2[n])) for n in small_names],
                       name="adamw_small")
    for n, (d, nm, nv) in zip(small_names, res):
        sh = params[n].shape
        deltas[n], new_m[n], new_v[n] = d.reshape(sh), nm.reshape(sh), nv.reshape(sh)

    return (loss, grad_x[None], *[grads[n] for n in order], *[deltas[n] for n in order],
            *[new_m[n] for n in order], *[new_v[n] for n in order])
```

```python
import functools
import math

import jax
import jax.numpy as jnp
from jax import lax
from jax.experimental import pallas as pl
from jax.experimental.pallas import tpu as pltpu

F32 = jnp.float32
MXU_DTYPE = jnp.bfloat16
WIRE_DTYPE = jnp.bfloat16
MESH = pl.DeviceIdType.MESH

D_MODEL = 1024
EPS = 1e-6
M_HEADS, M_QK, M_V, M_CHUNK = 8, 64, 128, 128
GATE_CAP = 15.0
A_DH, A_QH, A_KVH, A_G, A_BLK = 64, 16, 2, 8, 128
D_FF = 2816
N_CHIPS = 4
LANES = 128
VMEM_LIMIT = 56 * 1024 * 1024

ADAM_LR, ADAM_B1, ADAM_B2, ADAM_EPS, ADAM_WD, ADAM_STEP = 0.001, 0.9, 0.999, 1e-08, 0.01, 10


def _cparams(sem):
    return pltpu.CompilerParams(dimension_semantics=sem, vmem_limit_bytes=VMEM_LIMIT)


def _pick(n, cands):
    for c in cands:
        if n % c == 0:
            return c
    return n


class _Rows:
    def __init__(self, tiles, vecs, outs, fn):
        self.tiles, self.vecs, self.outs, self.fn = list(tiles), list(vecs), list(outs), fn


ROWS_TILE = 512
ROWS_FULL_K = 4224


def _mm(a, b, *, ta=False, tb=False, out_dtype=F32, bias=None, add=None, b_layer=None, out_stacked=False, rows=None,
        name):
    if ta:
        K, M = a.shape
    else:
        M, K = a.shape
    b_stacked = b_layer is not None
    if b_stacked:
        assert not ta
        n_sh = b.shape[2]
        w_rows, w_cols = D_MODEL, N_CHIPS * n_sh
        N, Kb = (w_rows, w_cols) if tb else (w_cols, w_rows)
    elif tb:
        N, Kb = b.shape
    else:
        Kb, N = b.shape
    assert K == Kb, (a.shape, b.shape)
    tm = _pick(M, (1024, 1408, 512, 256, 128))
    tn = _pick(N, (1024, 1408, 1280, 640, 512, 256, 128))
    tk = K if K <= 2816 else _pick(K, (2816, 2048, 1408, 1024, 512, 256, 128))
    shards_per_step = 1
    if b_stacked and tb:
        shards_per_step = 2
        tk = shards_per_step * n_sh
    if (b_stacked and not tb) or out_stacked:
        tn = N // N_CHIPS
    if rows is not None:
        assert tn == N and not out_stacked and not ta
        tm = min(tm, ROWS_TILE)
        if K <= ROWS_FULL_K:
            tk = K
    nk = K // tk
    dn = (((0 if ta else 1,), (1 if tb else 0,)), ((), ()))
    has_bias, has_add = bias is not None, add is not None
    n_tiles, n_vecs, n_outs = (len(rows.tiles), len(rows.vecs), len(rows.outs)) if rows is not None else (0, 0, 1)

    def body(*refs):
        a_ref, b_ref = refs[0], refs[1]
        pos = 2
        bias_ref = add_ref = None
        if has_bias:
            bias_ref = refs[pos]
            pos += 1
        if has_add:
            add_ref = refs[pos]
            pos += 1
        tile_refs, vec_refs = refs[pos:pos + n_tiles], refs[pos + n_tiles:pos + n_tiles + n_vecs]
        pos += n_tiles + n_vecs
        out_refs = refs[pos:pos + n_outs]
        acc_ref = refs[pos + n_outs] if nk > 1 else None

        def finish(r):
            if has_bias:
                r = r + bias_ref[...]
            if has_add:
                r = r + add_ref[...]
            if rows is None:
                out_refs[0][...] = r.astype(out_dtype)
                return
            vals = rows.fn(r, [t[...] for t in tile_refs], [v[...] for v in vec_refs])
            first = pl.program_id(0) == 0
            for (kind, _), o_ref, val in zip(rows.outs, out_refs, vals):
                if kind == "tile":
                    o_ref[...] = val.astype(o_ref.dtype)
                else:
                    @pl.when(first)
                    def _(o_ref=o_ref, val=val):
                        o_ref[...] = val

                    @pl.when(jnp.logical_not(first))
                    def _(o_ref=o_ref, val=val):
                        o_ref[...] += val

        if shards_per_step > 1:
            part = sum(lax.dot_general(a_ref[:, s * n_sh:(s + 1) * n_sh], b_ref[s], dn, preferred_element_type=F32)
                       for s in range(shards_per_step))
        else:
            part = lax.dot_general(a_ref[...], b_ref[...], dn, preferred_element_type=F32)
        if nk == 1:
            finish(part)
        else:
            k = pl.program_id(2)

            @pl.when(k == 0)
            def _():
                acc_ref[...] = part

            @pl.when(k > 0)
            def _():
                acc_ref[...] += part

            @pl.when(k == nk - 1)
            def _():
                finish(acc_ref[...])

    a_spec = pl.BlockSpec((tk, tm), lambda i, j, k: (k, i)) if ta else pl.BlockSpec((tm, tk), lambda i, j, k: (i, k))
    if b_stacked and tb:
        off = b_layer * (w_rows // tn)
        b_spec = pl.BlockSpec((shards_per_step, tn, n_sh), lambda i, j, k: (k, off + j, 0))
    elif b_stacked:
        off = b_layer * (w_rows // tk)
        b_spec = pl.BlockSpec((None, tk, tn), lambda i, j, k: (j, off + k, 0))
    elif tb:
        b_spec = pl.BlockSpec((tn, tk), lambda i, j, k: (j, k))
    else:
        b_spec = pl.BlockSpec((tk, tn), lambda i, j, k: (k, j))
    if out_stacked:
        out_spec = pl.BlockSpec((None, tm, tn), lambda i, j, k: (j, i, 0))
        out_shape = jax.ShapeDtypeStruct((N_CHIPS, M, tn), out_dtype)
    else:
        out_spec = pl.BlockSpec((tm, tn), lambda i, j, k: (i, j))
        out_shape = jax.ShapeDtypeStruct((M, N), out_dtype)
    in_specs, args = [a_spec, b_spec], [a, b]
    if has_bias:
        in_specs.append(pl.BlockSpec((1, tn), lambda i, j, k: (0, j)))
        args.append(bias)
    if has_add:
        in_specs.append(pl.BlockSpec((tm, tn), lambda i, j, k: (i, j)))
        args.append(add)
    semantics = ("parallel", "parallel", "arbitrary")
    if rows is not None:
        tile_spec = pl.BlockSpec((tm, N), lambda i, j, k: (i, 0))
        vec_spec = pl.BlockSpec((1, N), lambda i, j, k: (0, 0))
        in_specs += [tile_spec] * n_tiles + [vec_spec] * n_vecs
        args += rows.tiles + rows.vecs
        out_spec = [tile_spec if kind == "tile" else vec_spec for kind, _ in rows.outs]
        out_shape = [jax.ShapeDtypeStruct((M, N) if kind == "tile" else (1, N), dt) for kind, dt in rows.outs]
        semantics = ("arbitrary", "arbitrary", "arbitrary")
    return pl.pallas_call(
        body, name=name,
        grid=(M // tm, N // tn, nk),
        in_specs=in_specs,
        out_specs=out_spec,
        out_shape=out_shape,
        scratch_shapes=[pltpu.VMEM((tm, tn), F32)] if nk > 1 else [],
        compiler_params=_cparams(semantics),
    )(*args)


ROW_TILE = 512


def _rms_fwd(x, g, *, out_dtype, name):
    S, D = x.shape
    tm = _pick(S, (ROW_TILE, 256, 128))

    def body(x_ref, g_ref, o_ref):
        o_ref[...] = _rms(x_ref[...], g_ref[...]).astype(out_dtype)

    row = pl.BlockSpec((tm, D), lambda i: (i, 0))
    return pl.pallas_call(
        body, name=name, grid=(S // tm,), in_specs=[row, pl.BlockSpec((1, D), lambda i: (0, 0))], out_specs=row,
        out_shape=jax.ShapeDtypeStruct((S, D), out_dtype),
        compiler_params=_cparams(("parallel",)),
    )(x, g)


def _rms(x, g):
    return x * lax.rsqrt(jnp.mean(x * x, axis=-1, keepdims=True) + EPS) * g


def _rms_vjp(x, g, dy):
    r = lax.rsqrt(jnp.mean(x * x, axis=-1, keepdims=True) + EPS)
    xh = x * r
    gd = dy * g
    dx = r * (gd - xh * jnp.mean(gd * xh, axis=-1, keepdims=True))
    return dx, jnp.sum(dy * xh, axis=0, keepdims=True)


def _rows_sublayer_end(res, g_post, g_pre_next):
    def fn(z, tiles, vecs):
        h = tiles[0] + _rms(z, vecs[0])
        return [z, h, _rms(h, vecs[1])]

    return _Rows([res], [g_post, g_pre_next], [("tile", F32), ("tile", F32), ("tile", MXU_DTYPE)], fn)


def _rows_last_sublayer_end(res, g_post, target):
    def fn(z, tiles, vecs):
        y = tiles[0] + _rms(z, vecs[0])
        err = y - tiles[1]
        dy = err * (1.0 / err.shape[-1])
        dz, dg = _rms_vjp(z, vecs[0], dy)
        loss = 0.5 * jnp.sum(jnp.mean(err * err, axis=-1, keepdims=True), axis=0, keepdims=True)
        return [z, dy, dz, dg, jnp.broadcast_to(loss, dg.shape)]

    return _Rows([res, target], [g_post], [("tile", F32), ("tile", F32), ("tile", MXU_DTYPE), ("vec", F32), ("vec", F32)], fn)


def _rows_sublayer_start_bwd(x, res, z_below, g_pre, g_post_below):
    def fn(dz, tiles, vecs):
        dx, dg_pre = _rms_vjp(tiles[0], vecs[0], dz)
        dh = tiles[1] + dx
        dzb, dg_post = _rms_vjp(tiles[2], vecs[1], dh)
        return [dh, dg_pre, dzb, dg_post, jnp.sum(dzb, axis=0, keepdims=True)]

    return _Rows([x, res, z_below], [g_pre, g_post_below],
                 [("tile", F32), ("vec", F32), ("tile", MXU_DTYPE), ("vec", F32), ("vec", F32)], fn)


def _rows_first_sublayer_start_bwd(x, res, g_pre):
    def fn(dz, tiles, vecs):
        dx, dg_pre = _rms_vjp(tiles[0], vecs[0], dz)
        return [tiles[1] + dx, dg_pre]

    return _Rows([x, res], [g_pre], [("tile", F32), ("vec", F32)], fn)


def _mx(t):
    return t.astype(MXU_DTYPE)


def _mxf(t):
    return t.astype(MXU_DTYPE).astype(F32)


def _rawdot(a, b, ca, cb):
    return lax.dot_general(_mx(a), _mx(b), (((ca,), (cb,)), ((), ())), preferred_element_type=F32)


@functools.partial(jax.custom_vjp, nondiff_argnums=(2, 3))
def _dot(a, b, ca, cb):
    return _rawdot(a, b, ca, cb)


def _dot_fwd(a, b, ca, cb):
    return _rawdot(a, b, ca, cb), (a, b)


def _dot_bwd(ca, cb, res, g):
    a, b = res
    bj = 1 if cb == 0 else 0
    ai = 0 if ca == 1 else 1
    da = _rawdot(g, b, 1, bj) if ca == 1 else _rawdot(b, g, bj, 1)
    db = _rawdot(a, g, ai, 0) if cb == 0 else _rawdot(g, a, 0, ai)
    return da.astype(a.dtype), db.astype(b.dtype)


_dot.defvjp(_dot_fwd, _dot_bwd)


def _softcap(z):
    return GATE_CAP * jnp.tanh(z / GATE_CAP)


def _log_sigmoid(z):
    return jnp.minimum(z, 0.0) - jnp.log(1.0 + jnp.exp(-jnp.abs(z)))


def _sigmoid(z):
    return 0.5 * jnp.tanh(0.5 * z) + 0.5


def _lane_col(t, lane_index):
    lane = lax.broadcasted_iota(jnp.int32, t.shape, 1)
    return jnp.sum(jnp.where(lane == lane_index, t, 0.0), axis=1, keepdims=True)


def _mlstm_gates(G, bias):
    L = G.shape[0]
    z = _softcap(G + bias)
    ig = z
    lf = _log_sigmoid(z)
    ti = lax.broadcasted_iota(jnp.int32, (L, L), 0)
    si = lax.broadcasted_iota(jnp.int32, (L, L), 1)
    tril = (si <= ti).astype(F32)
    b = lax.dot_general(tril, lf, (((1,), (0,)), ((), ())), precision=lax.Precision.HIGHEST, preferred_element_type=F32)
    bL = jnp.sum(lf, axis=0, keepdims=True)
    return ig, b, ig.T, b.T, bL


def _mlstm_head(h, q, k_pair, v, o, ig_all, b_all, igT, bT, bL_all, hn, C, n, m):
    L = q.shape[0]
    k = jnp.where(_own_lanes(h, k_pair.shape), k_pair, 0.0)
    ti = lax.broadcasted_iota(jnp.int32, (L, L), 0)
    si = lax.broadcasted_iota(jnp.int32, (L, L), 1)
    lower = (si <= ti)
    ig = _lane_col(ig_all, h)
    b = _lane_col(b_all, M_HEADS + h)
    ig_row = igT[h:h + 1, :]
    b_row = bT[M_HEADS + h:M_HEADS + h + 1, :]
    bL = _lane_col(bL_all, M_HEADS + h)
    inter = b + m
    dlog = jnp.where(lower, b - b_row + ig_row, -jnp.inf)
    m_t = lax.stop_gradient(jnp.maximum(inter, jnp.max(dlog, axis=-1, keepdims=True)))
    qs = q * (M_QK ** -0.5)
    w = _dot(qs, k, 1, 1) * jnp.exp(dlog - m_t)
    s_inter = jnp.exp(inter - m_t)
    num = _dot(w, v, 1, 0) + s_inter * _dot(qs, C, 1, 0)
    den = jnp.sum(w, axis=-1, keepdims=True) + s_inter * jnp.sum(_mxf(qs) * _mxf(n), axis=-1, keepdims=True)
    hout = num * (1.0 / jnp.maximum(jnp.abs(den), jnp.exp(-m_t)))
    tail = bL - b + ig
    m_new = lax.stop_gradient(jnp.maximum(bL + m, jnp.max(tail, axis=0, keepdims=True)))
    ws = jnp.exp(tail - m_new)
    decay = jnp.exp(bL + m - m_new)
    wk = ws * k
    C_new = decay * C + _dot(wk, v, 0, 0)
    n_new = decay * n + jnp.sum(_mxf(ws) * _mxf(k), axis=0, keepdims=True)
    hs = hout * lax.rsqrt(jnp.mean(hout * hout, axis=-1, keepdims=True) + EPS) * hn
    gated = _sigmoid(o) * hs
    return (gated, C_new, n_new), m_new


M_OFF_Q, M_OFF_K = 0, M_HEADS * M_QK
M_OFF_V = 2 * M_HEADS * M_QK
M_OFF_O = M_OFF_V + M_HEADS * M_V
M_OFF_G = M_OFF_O + M_HEADS * M_V
M_PROJ = M_OFF_G + LANES
M_PAIRS = M_HEADS * M_QK // LANES


def _head_cols(off, h):
    return slice(off + h * LANES, off + (h + 1) * LANES)


def _own_lanes(h, shape):
    low = lax.broadcasted_iota(jnp.int32, shape, 1) < M_QK
    return low if h % 2 == 0 else jnp.logical_not(low)


def _mlstm_specs(NC, rev):
    H, L = M_HEADS, M_CHUNK
    cc = (lambda c: NC - 1 - c) if rev else (lambda c: c)
    proj = pl.BlockSpec((L, M_PROJ), lambda c: (cc(c), 0))
    vec = pl.BlockSpec((1, LANES), lambda c: (0, 0))
    hn = pl.BlockSpec((1, H * M_V), lambda c: (0, 0))
    hv = pl.BlockSpec((L, H * M_V), lambda c: (cc(c), 0))
    Cs = pl.BlockSpec((None, H, LANES, M_V), lambda c: (cc(c), 0, 0, 0))
    ns = pl.BlockSpec((None, H, 1, LANES), lambda c: (cc(c), 0, 0, 0))
    ms = pl.BlockSpec((None, H, 1, 1), lambda c: (cc(c), 0, 0, 0))
    return proj, vec, hn, hv, Cs, ns, ms


_MLSTM_STATE = [pltpu.VMEM((M_HEADS, LANES, M_V), F32), pltpu.VMEM((M_HEADS, 1, LANES), F32),
                pltpu.VMEM((M_HEADS, 1, 1), F32)]


def _mlstm_fwd(proj, bias, hn, shards, *, name):
    S = proj.shape[0]
    H, NC = M_HEADS, S // M_CHUNK
    ps, vec, hns, hv, Cs, ns, ms = _mlstm_specs(NC, False)
    nsh = len(shards)
    any_spec = pl.BlockSpec(memory_space=pl.ANY)

    def body(p_ref, b_ref, hn_ref, *rest):
        gated_ref, C_all, n_all, m_all = rest[nsh:nsh + 4]
        C_s, n_s, m_s, send_sems, recv_sems = rest[2 * nsh + 4:]
        gather = _Gather(rest[:nsh], rest[nsh + 4:2 * nsh + 4], send_sems, recv_sems)

        @pl.when(pl.program_id(0) == 0)
        def _():
            C_s[...] = jnp.zeros_like(C_s)
            n_s[...] = jnp.zeros_like(n_s)
            m_s[...] = jnp.zeros_like(m_s)
            gather.issue()

        gate_terms = _mlstm_gates(p_ref[:, M_OFF_G:M_OFF_G + LANES], b_ref[...])
        for h in range(H):
            C, n, m = C_s[h], n_s[h], m_s[h]
            C_all[h] = C
            n_all[h] = n
            m_all[h] = m
            (gated, Cn, nn), mn = _mlstm_head(
                h, p_ref[:, _head_cols(M_OFF_Q, h // 2)], p_ref[:, _head_cols(M_OFF_K, h // 2)],
                p_ref[:, _head_cols(M_OFF_V, h)], p_ref[:, _head_cols(M_OFF_O, h)], *gate_terms,
                hn_ref[:, _head_cols(0, h)], C, n, m)
            gated_ref[:, _head_cols(0, h)] = gated.astype(gated_ref.dtype)
            C_s[h] = Cn
            n_s[h] = nn
            m_s[h] = mn

        @pl.when(pl.program_id(0) == NC - 1)
        def _():
            gather.finish()

    return pl.pallas_call(
        body, name=name, grid=(NC,),
        in_specs=[ps, vec, hns] + [any_spec] * nsh,
        out_specs=[hv, Cs, ns, ms] + [any_spec] * nsh,
        out_shape=[jax.ShapeDtypeStruct((S, H * M_V), MXU_DTYPE),
                   jax.ShapeDtypeStruct((NC, H, LANES, M_V), F32),
                   jax.ShapeDtypeStruct((NC, H, 1, LANES), F32),
                   jax.ShapeDtypeStruct((NC, H, 1, 1), F32)] + _Gather.out_shape(shards),
        scratch_shapes=list(_MLSTM_STATE) + _Gather.semaphores(nsh),
        compiler_params=_cparams(("arbitrary",)),
    )(proj, bias, hn, *shards)


def _mlstm_bwd(proj, bias, hn, C_all, n_all, m_all, dgated, qs, *, name):
    S = proj.shape[0]
    H, NC = M_HEADS, S // M_CHUNK
    ps, vec, hns, hv, Cs, ns, ms = _mlstm_specs(NC, True)
    nq = len(qs)
    any_spec = pl.BlockSpec(memory_space=pl.ANY)

    def body(p_ref, b_ref, hn_ref, C_ref, n_ref, m_ref, dg_ref, *rest):
        dp_ref, db_ref, dhn_ref = rest[nq:nq + 3]
        dC_s, dn_s, send_sems, recv_sems = rest[2 * nq + 3:]
        exchange = _Exchange(rest[:nq], rest[nq + 3:2 * nq + 3], send_sems, recv_sems)

        @pl.when(pl.program_id(0) == 0)
        def _():
            dC_s[...] = jnp.zeros_like(dC_s)
            dn_s[...] = jnp.zeros_like(dn_s)
            db_ref[...] = jnp.zeros_like(db_ref)
            dhn_ref[...] = jnp.zeros_like(dhn_ref)
            exchange.issue()

        gate_terms, gates_vjp = jax.vjp(_mlstm_gates, p_ref[:, M_OFF_G:M_OFF_G + LANES], b_ref[...])
        d_terms = [jnp.zeros_like(t) for t in gate_terms]
        for h in range(H):
            def head(q, k, v, o, *rest, h=h):
                return _mlstm_head(h, q, k, v, o, *rest, m_ref[h])

            prim = (p_ref[:, _head_cols(M_OFF_Q, h // 2)], p_ref[:, _head_cols(M_OFF_K, h // 2)],
                    p_ref[:, _head_cols(M_OFF_V, h)], p_ref[:, _head_cols(M_OFF_O, h)], *gate_terms,
                    hn_ref[:, _head_cols(0, h)], C_ref[h], n_ref[h])
            _, vjp, _ = jax.vjp(head, *prim, has_aux=True)
            dq, dk, dv, do, *d_gate, dhnh, dC, dn = vjp((dg_ref[:, _head_cols(0, h)].astype(F32), dC_s[h], dn_s[h]))
            if h % 2 == 0:
                dq_pair, dk_pair = dq, dk
            else:
                dp_ref[:, _head_cols(M_OFF_Q, h // 2)] = (dq_pair + dq).astype(dp_ref.dtype)
                dp_ref[:, _head_cols(M_OFF_K, h // 2)] = (dk_pair + dk).astype(dp_ref.dtype)
            dp_ref[:, _head_cols(M_OFF_V, h)] = dv.astype(dp_ref.dtype)
            dp_ref[:, _head_cols(M_OFF_O, h)] = do.astype(dp_ref.dtype)
            d_terms = [a + g for a, g in zip(d_terms, d_gate)]
            dhn_ref[:, _head_cols(0, h)] += dhnh
            dC_s[h] = dC
            dn_s[h] = dn
        dG, dbias = gates_vjp(tuple(d_terms))
        dp_ref[:, M_OFF_G:M_OFF_G + LANES] = dG.astype(dp_ref.dtype)
        db_ref[...] += dbias

        @pl.when(pl.program_id(0) == NC - 1)
        def _():
            exchange.finish()

    return pl.pallas_call(
        body, name=name, grid=(NC,),
        in_specs=[ps, vec, hns, Cs, ns, ms, hv] + [any_spec] * nq,
        out_specs=[ps, vec, hns] + [any_spec] * nq,
        out_shape=[jax.ShapeDtypeStruct((S, M_PROJ), MXU_DTYPE), jax.ShapeDtypeStruct((1, LANES), F32),
                   jax.ShapeDtypeStruct((1, H * M_V), F32)] + _Exchange.out_shape(qs),
        scratch_shapes=list(_MLSTM_STATE[:2]) + _Exchange.semaphores(nq),
        compiler_params=_cparams(("arbitrary",)),
    )(proj, bias, hn, C_all, n_all, m_all, dgated, *qs)


A_NQ = A_QH * A_DH
A_NKV = 2 * A_KVH * A_DH
A_PAIRS = A_G // 2


def _sink_softmax_parts(s, sink):
    mx = jnp.maximum(jnp.max(s, axis=-1, keepdims=True), sink)
    e = jnp.exp(s - mx)
    e_sink = jnp.exp(sink - mx)
    r = 1.0 / (jnp.sum(e, axis=-1, keepdims=True) + e_sink)
    return e * r, e_sink * r


@jax.custom_vjp
def _sink_softmax(s, sink):
    return _sink_softmax_parts(s, sink)[0]


def _sink_softmax_fwd(s, sink):
    p, p_sink = _sink_softmax_parts(s, sink)
    return p, (p, p_sink)


def _sink_softmax_bwd(res, dp):
    p, p_sink = res
    t = jnp.sum(dp * p, axis=-1, keepdims=True)
    return p * (dp - t), -p_sink * t


_sink_softmax.defvjp(_sink_softmax_fwd, _sink_softmax_bwd)


def _attn_group(Ps, KLO, KHI, VLO, VHI, sinks, first):
    B = Ps[0].shape[0]
    R = len(Ps) * B
    q2 = jnp.concatenate(Ps, axis=0) * (A_DH ** -0.5)
    s = jnp.concatenate([_dot(q2, KLO, 1, 1), _dot(q2, KHI, 1, 1)], axis=0)
    qi = lax.broadcasted_iota(jnp.int32, (2 * R, 2 * B), 0) & (B - 1)
    ku = lax.broadcasted_iota(jnp.int32, (2 * R, 2 * B), 1)
    diff = qi - (ku - B)
    mask = (diff >= 0) & (diff < B) & ((ku >= B) | jnp.logical_not(first))
    s = jnp.where(mask, s, -jnp.inf)
    ri = lax.broadcasted_iota(jnp.int32, (2 * R, A_G), 0)
    head = 2 * lax.shift_right_logical(ri & (R - 1), B.bit_length() - 1) + lax.shift_right_logical(ri, R.bit_length() - 1)
    onehot = head == lax.broadcasted_iota(jnp.int32, (2 * R, A_G), 1)
    sink = jnp.sum(jnp.where(onehot, sinks, 0.0), axis=1, keepdims=True)
    p = _sink_softmax(s, sink)
    o = _dot(p[:R], VLO, 1, 0) + _dot(p[R:], VHI, 1, 0)
    return tuple(o[j * B:(j + 1) * B] for j in range(len(Ps)))


def _swap_halves_of_lanes(t):
    return pltpu.roll(t, LANES // 2, 1)


def _kv_operands(kvp_ref, kvc_ref, h):
    kk = jnp.concatenate([kvp_ref[:, :LANES], kvc_ref[:, :LANES]], axis=0)
    vv = jnp.concatenate([kvp_ref[:, LANES:], kvc_ref[:, LANES:]], axis=0)
    low = lax.broadcasted_iota(jnp.int32, kk.shape, 1) < A_DH
    own = low if h == 0 else jnp.logical_not(low)
    k_own = jnp.where(own, kk, 0.0)
    v_own = jnp.where(own, vv, 0.0)
    k_oth, v_oth = _swap_halves_of_lanes(k_own), _swap_halves_of_lanes(v_own)
    if h == 0:
        return own, k_own, k_oth, v_own, v_oth
    return own, k_oth, k_own, v_oth, v_own


def _pair_cols(h, j):
    c = (h * A_PAIRS + j) * LANES
    return slice(c, c + LANES)


def _attn_fwd(proj, sinks, *, name):
    S = proj.shape[0]
    NB = S // A_BLK
    kv_blk = A_NQ // A_NKV
    qs = pl.BlockSpec((A_BLK, A_NQ), lambda n: (n, 0))
    cur = pl.BlockSpec((A_BLK, A_NKV), lambda n: (n, kv_blk))
    prev = pl.BlockSpec((A_BLK, A_NKV), lambda n: (jnp.maximum(n - 1, 0), kv_blk))
    sk = pl.BlockSpec((A_KVH, A_G), lambda n: (0, 0))

    def body(q_ref, kvp_ref, kvc_ref, s_ref, o_ref):
        first = pl.program_id(0) == 0
        for h in range(A_KVH):
            _, KLO, KHI, VLO, VHI = _kv_operands(kvp_ref, kvc_ref, h)
            Ps = tuple(q_ref[:, _pair_cols(h, j)] for j in range(A_PAIRS))
            outs = _attn_group(Ps, KLO, KHI, VLO, VHI, s_ref[h:h + 1, :], first)
            for j in range(A_PAIRS):
                o_ref[:, _pair_cols(h, j)] = outs[j].astype(o_ref.dtype)

    return pl.pallas_call(
        body, name=name, grid=(NB,),
        in_specs=[qs, prev, cur, sk], out_specs=qs,
        out_shape=jax.ShapeDtypeStruct((S, A_NQ), MXU_DTYPE),
        compiler_params=_cparams(("parallel",)),
    )(proj, proj, proj, sinks)


def _attn_bwd(proj, sinks, do, *, name):
    S = proj.shape[0]
    NB = S // A_BLK
    last = NB - 1
    kv_blk = A_NQ // A_NKV
    qs = pl.BlockSpec((A_BLK, A_NQ), lambda n: (jnp.minimum(n, last), 0))
    cur = pl.BlockSpec((A_BLK, A_NKV), lambda n: (jnp.minimum(n, last), kv_blk))
    prev = pl.BlockSpec((A_BLK, A_NKV), lambda n: (jnp.clip(n - 1, 0, last), kv_blk))
    sk = pl.BlockSpec((A_KVH, A_G), lambda n: (0, 0))
    lag = pl.BlockSpec((A_BLK, A_NKV), lambda n: (jnp.maximum(n - 1, 0), 0))
    cq_spec = pl.BlockSpec((1, A_NQ), lambda n: (0, 0))
    ckv_spec = pl.BlockSpec((1, A_NKV), lambda n: (0, 0))

    def body(q_ref, kvp_ref, kvc_ref, s_ref, do_ref, dq_ref, dkv_ref, ds_ref, cq_ref, ckv_ref, keep):
        n = pl.program_id(0)

        @pl.when(n == 0)
        def _():
            keep[...] = jnp.zeros_like(keep)
            ds_ref[...] = jnp.zeros_like(ds_ref)
            cq_ref[...] = jnp.zeros_like(cq_ref)
            ckv_ref[...] = jnp.zeros_like(ckv_ref)

        @pl.when(n < NB)
        def _():
            f = functools.partial(_attn_group, first=n == 0)
            dk = jnp.zeros((2 * A_BLK, LANES), F32)
            dv = jnp.zeros((2 * A_BLK, LANES), F32)
            ds_rows = []
            for h in range(A_KVH):
                own, KLO, KHI, VLO, VHI = _kv_operands(kvp_ref, kvc_ref, h)
                Ps = tuple(q_ref[:, _pair_cols(h, j)] for j in range(A_PAIRS))
                _, vjp = jax.vjp(f, Ps, KLO, KHI, VLO, VHI, s_ref[h:h + 1, :])
                dPs, dKLO, dKHI, dVLO, dVHI, dsk = vjp(
                    tuple(do_ref[:, _pair_cols(h, j)].astype(F32) for j in range(A_PAIRS)))
                for j in range(A_PAIRS):
                    dq_ref[:, _pair_cols(h, j)] = dPs[j].astype(dq_ref.dtype)
                    cq_ref[:, _pair_cols(h, j)] += jnp.sum(dPs[j], axis=0, keepdims=True)
                dk_own, dk_oth = (dKLO, dKHI) if h == 0 else (dKHI, dKLO)
                dv_own, dv_oth = (dVLO, dVHI) if h == 0 else (dVHI, dVLO)
                dk = dk + jnp.where(own, dk_own, 0.0) + _swap_halves_of_lanes(jnp.where(own, 0.0, dk_oth))
                dv = dv + jnp.where(own, dv_own, 0.0) + _swap_halves_of_lanes(jnp.where(own, 0.0, dv_oth))
                ds_rows.append(dsk)
            ds_ref[...] += jnp.concatenate(ds_rows, axis=0)
            dkv = jnp.concatenate([dk, dv], axis=1)
            done = keep[...] + dkv[:A_BLK]
            dkv_ref[...] = done.astype(dkv_ref.dtype)
            ckv_ref[...] += jnp.sum(done, axis=0, keepdims=True)
            keep[...] = dkv[A_BLK:]

        @pl.when(n == NB)
        def _():
            done = keep[...]
            dkv_ref[...] = done.astype(dkv_ref.dtype)
            ckv_ref[...] += jnp.sum(done, axis=0, keepdims=True)

    return pl.pallas_call(
        body, name=name, grid=(NB + 1,),
        in_specs=[qs, prev, cur, sk, qs],
        out_specs=[qs, lag, sk, cq_spec, ckv_spec],
        out_shape=[jax.ShapeDtypeStruct((S, A_NQ), MXU_DTYPE), jax.ShapeDtypeStruct((S, A_NKV), MXU_DTYPE),
                   jax.ShapeDtypeStruct((A_KVH, A_G), F32), jax.ShapeDtypeStruct((1, A_NQ), F32),
                   jax.ShapeDtypeStruct((1, A_NKV), F32)],
        scratch_shapes=[pltpu.VMEM((A_BLK, A_NKV), F32)],
        compiler_params=_cparams(("arbitrary",)),
    )(proj, proj, proj, sinks, do)


HALO = 8


def _shift_rows(t, j):
    if j == 0:
        return t
    return pltpu.roll(t, j % t.shape[0], 0)


def _conv_gate(gate_ext, cw, cb):
    return cb + cw[0:1, :] * _shift_rows(gate_ext, 2) + cw[1:2, :] * _shift_rows(gate_ext, 1) + cw[2:3, :] * gate_ext


def _convgate_fwd(u, cw, cb, *, name):
    S, F2 = u.shape
    F = F2 // 2
    tm = _pick(S, (256, 128))
    hb = tm // HALO
    urow = pl.BlockSpec((tm, F2), lambda i: (i, 0))
    uprev = pl.BlockSpec((HALO, F), lambda i: (jnp.maximum(i * hb - 1, 0), 0))

    def body(u_ref, up_ref, cw_ref, cb_ref, a_ref):
        i = pl.program_id(0)
        gate = u_ref[:, :F]
        val = u_ref[:, F:]
        prev = jnp.where(i > 0, up_ref[...], 0.0)
        gc = _conv_gate(jnp.concatenate([prev, gate], axis=0), cw_ref[...], cb_ref[...])[HALO:]
        a_ref[...] = (gc * _sigmoid(gc) * val).astype(a_ref.dtype)

    return pl.pallas_call(
        body, name=name, grid=(S // tm,),
        in_specs=[urow, uprev, pl.BlockSpec((3, F), lambda i: (0, 0)), pl.BlockSpec((1, F), lambda i: (0, 0))],
        out_specs=pl.BlockSpec((tm, F), lambda i: (i, 0)),
        out_shape=jax.ShapeDtypeStruct((S, F), MXU_DTYPE),
        compiler_params=_cparams(("parallel",)),
    )(u, u, cw, cb)


def _convgate_bwd(u, da, cw, cb, gs, *, name):
    S, F2 = u.shape
    ng = len(gs)
    any_spec = pl.BlockSpec(memory_space=pl.ANY)
    F = F2 // 2
    tm = _pick(S, (128,))
    hb = tm // HALO
    nt = S // tm
    nh = S // HALO
    urow = pl.BlockSpec((tm, F2), lambda i: (i, 0))
    uprev = pl.BlockSpec((HALO, F), lambda i: (jnp.maximum(i * hb - 1, 0), 0))
    unext = pl.BlockSpec((HALO, F2), lambda i: (jnp.minimum((i + 1) * hb, nh - 1), 0))
    darow = pl.BlockSpec((tm, F), lambda i: (i, 0))
    danext = pl.BlockSpec((HALO, F), lambda i: (jnp.minimum((i + 1) * hb, nh - 1), 0))

    def body(u_ref, up_ref, un_ref, da_ref, dan_ref, cw_ref, cb_ref, *rest):
        du_ref, dcw_ref, dcb_ref = rest[ng:ng + 3]
        swap = _Swap(rest[:ng], rest[ng + 3:2 * ng + 3], *rest[2 * ng + 3:]) if ng else None
        i = pl.program_id(0)
        cwv = cw_ref[...]
        prev = jnp.where(i > 0, up_ref[...], 0.0)
        gate_ext = jnp.concatenate([prev, u_ref[:, :F], un_ref[:, :F]], axis=0)
        val_ext = jnp.concatenate([u_ref[:, F:], un_ref[:, F:]], axis=0)
        da_next = jnp.where(i < nt - 1, dan_ref[...].astype(F32), 0.0)
        da_ext = jnp.concatenate([da_ref[...].astype(F32), da_next], axis=0)
        gc = _conv_gate(gate_ext, cwv, cb_ref[...])[HALO:]
        sg = _sigmoid(gc)
        silu = gc * sg
        dval = da_ext * silu
        dgc = da_ext * val_ext * (sg * (1.0 + gc * (1.0 - sg)))
        dgate = cwv[2:3, :] * dgc + cwv[1:2, :] * _shift_rows(dgc, -1) + cwv[0:1, :] * _shift_rows(dgc, -2)
        du_ref[:, :F] = dgate[:tm].astype(du_ref.dtype)
        du_ref[:, F:] = dval[:tm].astype(du_ref.dtype)
        dgc_c = dgc[:tm]
        g0 = gate_ext[HALO:HALO + tm]
        g1 = _shift_rows(gate_ext, 1)[HALO:HALO + tm]
        g2 = _shift_rows(gate_ext, 2)[HALO:HALO + tm]
        dcw = jnp.concatenate([jnp.sum(dgc_c * g2, axis=0, keepdims=True),
                               jnp.sum(dgc_c * g1, axis=0, keepdims=True),
                               jnp.sum(dgc_c * g0, axis=0, keepdims=True)], axis=0)
        dcb = jnp.sum(dgc_c, axis=0, keepdims=True)

        @pl.when(i == 0)
        def _():
            dcw_ref[...] = dcw
            dcb_ref[...] = dcb
            if ng:
                swap.issue()

        @pl.when(i > 0)
        def _():
            dcw_ref[...] += dcw
            dcb_ref[...] += dcb

        if ng:
            @pl.when(i == nt - 1)
            def _():
                swap.finish()

    return pl.pallas_call(
        body, name=name, grid=(nt,),
        in_specs=[urow, uprev, unext, darow, danext,
                  pl.BlockSpec((3, F), lambda i: (0, 0)), pl.BlockSpec((1, F), lambda i: (0, 0))] + [any_spec] * ng,
        out_specs=[urow, pl.BlockSpec((3, F), lambda i: (0, 0)), pl.BlockSpec((1, F), lambda i: (0, 0))] + [any_spec] * ng,
        out_shape=[jax.ShapeDtypeStruct((S, F2), MXU_DTYPE), jax.ShapeDtypeStruct((3, F), F32),
                   jax.ShapeDtypeStruct((1, F), F32)] + (_Swap.out_shape(gs) if ng else []),
        scratch_shapes=_Swap.semaphores(ng) if ng else [],
        compiler_params=_cparams(("arbitrary",)),
    )(u, u, u, da, da, cw, cb, *gs)


def _adamw_math(w, g, m, v):
    m = ADAM_B1 * m + (1.0 - ADAM_B1) * g
    v = ADAM_B2 * v + (1.0 - ADAM_B2) * (g * g)
    m_hat = m / (1.0 - ADAM_B1 ** ADAM_STEP)
    v_hat = v / (1.0 - ADAM_B2 ** ADAM_STEP)
    delta = -ADAM_LR * (m_hat / (jnp.sqrt(v_hat) + ADAM_EPS) + ADAM_WD * w)
    return delta, m, v


def _adamw_layers(w, gs, m, v, *, name):
    Lr, R, C = w.shape
    tr = _pick(R, (256, 128, 64, 32, 16, 8))
    outs = None
    for layer, g in enumerate(gs):
        def body(w_ref, g_ref, m_ref, v_ref, *rest):
            go_ref, d_ref, nm_ref, nv_ref = rest[-4:]
            gv = g_ref[...]
            d, nm, nv = _adamw_math(w_ref[...], gv, m_ref[...], v_ref[...])
            go_ref[...] = gv
            d_ref[...] = d
            nm_ref[...] = nm
            nv_ref[...] = nv

        lay = pl.BlockSpec((None, tr, C), lambda i, layer=layer: (layer, i, 0))
        in_specs = [lay, pl.BlockSpec((tr, C), lambda i: (i, 0)), lay, lay]
        args = [w, g, m, v]
        aliases = {}
        if outs is not None:
            in_specs += [pl.BlockSpec(memory_space=pl.ANY)] * 4
            args += list(outs)
            aliases = {4 + t: t for t in range(4)}
        outs = pl.pallas_call(
            body, name=f"{name}_{layer}", grid=(R // tr,), in_specs=in_specs, out_specs=[lay] * 4,
            out_shape=[jax.ShapeDtypeStruct((Lr, R, C), F32)] * 4, input_output_aliases=aliases,
            compiler_params=_cparams(("parallel",)),
        )(*args)
    return outs


def _adamw_small(items, *, name):
    n = len(items)

    def body(*refs):
        ins, outs = refs[:4 * n], refs[4 * n:]
        for t in range(n):
            w, g, m, v = (r[...] for r in ins[4 * t:4 * t + 4])
            d, nm, nv = _adamw_math(w, g, m, v)
            outs[3 * t][...] = d
            outs[3 * t + 1][...] = nm
            outs[3 * t + 2][...] = nv

    flat = [a for it in items for a in it]
    out_shape = [jax.ShapeDtypeStruct(it[0].shape, F32) for it in items for _ in range(3)]
    vm = pl.BlockSpec(memory_space=pltpu.VMEM)
    res = pl.pallas_call(body, name=name, in_specs=[vm] * len(flat), out_specs=[vm] * len(out_shape),
                         out_shape=out_shape)(*flat)
    return [tuple(res[3 * t:3 * t + 3]) for t in range(n)]


def _place():
    return lax.axis_index("x"), lax.axis_index("y"), lax.axis_index("c")


_FLIPS = ((1, 0), (0, 1), (1, 1))


ROW_ALIGN = 16


def _half(rows, which):
    return pl.ds(pl.multiple_of(which * (rows // 2), ROW_ALIGN), rows // 2)


def _remote(src, dst, send_sems, recv_sems, k, to):
    return pltpu.make_async_remote_copy(src_ref=src, dst_ref=dst, send_sem=send_sems.at[k], recv_sem=recv_sems.at[k],
                                        device_id=to, device_id_type=MESH)


def _gather_shards(shards, *, name):
    n = len(shards)
    any_spec = pl.BlockSpec(memory_space=pl.ANY)

    def body(*refs):
        gather = _Gather(refs[:n], refs[n:2 * n], refs[2 * n], refs[2 * n + 1])
        gather.issue()
        gather.finish()

    return pl.pallas_call(
        body, name=name, in_specs=[any_spec] * n, out_specs=[any_spec] * n,
        out_shape=_Gather.out_shape(shards), scratch_shapes=_Gather.semaphores(n),
    )(*shards)


class _Gather:
    def __init__(self, w_refs, out_refs, send_sems, recv_sems):
        self.w_refs, self.out_refs, self.send_sems, self.recv_sems = w_refs, out_refs, send_sems, recv_sems
        self.pairs = [(i, j) for i in range(len(w_refs)) for j in range(3)]

    @staticmethod
    def out_shape(shards):
        return [jax.ShapeDtypeStruct((N_CHIPS,) + s.shape, s.dtype) for s in shards]

    @staticmethod
    def semaphores(n):
        return [pltpu.SemaphoreType.DMA((6 * n,)), pltpu.SemaphoreType.DMA((6 * n,))]

    def _where(self):
        x, y, c = _place()
        return x, y, c, [(x ^ fx, y ^ fy) for fx, fy in _FLIPS]

    def _over_ici(self, i, j, landed):
        x, y, c, chips = self._where()
        px, py = chips[j]
        mine = _half(self.w_refs[i].shape[0], c)
        if landed:
            src = dst = self.out_refs[i].at[2 * px + py, mine]
        else:
            src, dst = self.w_refs[i].at[mine], self.out_refs[i].at[2 * x + y, mine]
        return _remote(src, dst, self.send_sems, self.recv_sems, 6 * i + j, (px, py, c))

    def _over_d2d(self, i, j, which):
        x, y, c, chips = self._where()
        px, py = chips[j]
        blk = self.out_refs[i].at[2 * px + py, _half(self.w_refs[i].shape[0], which)]
        return _remote(blk, blk, self.send_sems, self.recv_sems, 6 * i + 3 + j, (x, y, 1 - c))

    def issue(self):
        for i, j in self.pairs:
            self._over_ici(i, j, False).start()

    def finish(self):
        c = lax.axis_index("c")
        for i, j in self.pairs:
            self._over_ici(i, j, True).wait_recv()
            self._over_d2d(i, j, c).start()
        for i, j in self.pairs:
            self._over_d2d(i, j, 1 - c).wait_recv()
        for i, j in self.pairs:
            self._over_ici(i, j, False).wait_send()
            self._over_d2d(i, j, c).wait_send()


def _swap_halves(gs, *, name):
    n = len(gs)
    any_spec = pl.BlockSpec(memory_space=pl.ANY)

    def body(*refs):
        swap = _Swap(refs[:n], refs[n:2 * n], refs[2 * n], refs[2 * n + 1])
        swap.issue()
        swap.finish()

    return pl.pallas_call(
        body, name=name, in_specs=[any_spec] * n, out_specs=[any_spec] * n,
        out_shape=_Swap.out_shape(gs), scratch_shapes=_Swap.semaphores(n),
    )(*gs)


class _Swap:
    def __init__(self, g_refs, out_refs, send_sems, recv_sems):
        self.g_refs, self.out_refs, self.send_sems, self.recv_sems = g_refs, out_refs, send_sems, recv_sems

    @staticmethod
    def out_shape(gs):
        return [jax.ShapeDtypeStruct((N_CHIPS, g.shape[1] // 2, g.shape[2]), g.dtype) for g in gs]

    @staticmethod
    def semaphores(n):
        return [pltpu.SemaphoreType.DMA((n,)), pltpu.SemaphoreType.DMA((n,))]

    def _copies(self):
        x, y, c = _place()
        return [_remote(g.at[:, _half(g.shape[1], 1 - c)], out, self.send_sems, self.recv_sems, i, (x, y, 1 - c))
                for i, (g, out) in enumerate(zip(self.g_refs, self.out_refs))]

    def issue(self):
        for cp in self._copies():
            cp.start()

    def finish(self):
        for cp in self._copies():
            cp.wait()


def _add_halves(g, got, c_arr, *, name):
    _, rows, cols = g.shape
    blk = (None, rows // 2, cols)

    def body(c_ref, g_ref, got_ref, o_ref):
        o_ref[...] = (g_ref[...] + got_ref[...]).astype(o_ref.dtype)

    return pl.pallas_call(
        body, name=name,
        grid_spec=pltpu.PrefetchScalarGridSpec(
            num_scalar_prefetch=1, grid=(N_CHIPS,),
            in_specs=[pl.BlockSpec(blk, lambda j, c_ref: (j, c_ref[0], 0)), pl.BlockSpec(blk, lambda j, c_ref: (j, 0, 0))],
            out_specs=pl.BlockSpec(blk, lambda j, c_ref: (j, 0, 0))),
        out_shape=jax.ShapeDtypeStruct((N_CHIPS, rows // 2, cols), WIRE_DTYPE),
        compiler_params=_cparams(("parallel",)),
    )(c_arr, g, got)


def _exchange_chips(qs, *, name):
    n = len(qs)
    any_spec = pl.BlockSpec(memory_space=pl.ANY)

    def body(*refs):
        exchange = _Exchange(refs[:n], refs[n:2 * n], refs[2 * n], refs[2 * n + 1])
        exchange.issue()
        exchange.finish()

    return pl.pallas_call(
        body, name=name, in_specs=[any_spec] * n, out_specs=[any_spec] * n,
        out_shape=_Exchange.out_shape(qs), scratch_shapes=_Exchange.semaphores(n),
    )(*qs)


class _Exchange:
    def __init__(self, q_refs, out_refs, send_sems, recv_sems):
        self.q_refs, self.out_refs, self.send_sems, self.recv_sems = q_refs, out_refs, send_sems, recv_sems
        self.pairs = [(i, j) for i in range(len(q_refs)) for j in range(3)]

    @staticmethod
    def out_shape(qs):
        return [jax.ShapeDtypeStruct(q.shape, q.dtype) for q in qs]

    @staticmethod
    def semaphores(n):
        return [pltpu.SemaphoreType.DMA((3 * n,)), pltpu.SemaphoreType.DMA((3 * n,))]

    def _copy(self, i, j, landed):
        x, y, c = _place()
        px, py = [(x ^ fx, y ^ fy) for fx, fy in _FLIPS][j]
        if landed:
            src = dst = self.out_refs[i].at[2 * px + py]
        else:
            src, dst = self.q_refs[i].at[2 * px + py], self.out_refs[i].at[2 * x + y]
        return _remote(src, dst, self.send_sems, self.recv_sems, 3 * i + j, (px, py, c))

    def issue(self):
        for i, j in self.pairs:
            self._copy(i, j, False).start()

    def finish(self):
        for i, j in self.pairs:
            self._copy(i, j, True).wait_recv()
        for i, j in self.pairs:
            self._copy(i, j, False).wait_send()


def _sum_chips(q, r, place_arr, *, name):
    _, h, cols = q.shape
    blk = (None, h, cols)

    def body(p_ref, q_ref, r1_ref, r2_ref, r3_ref, o_ref):
        o_ref[...] = ((q_ref[...].astype(F32) + r1_ref[...].astype(F32)) + r2_ref[...].astype(F32)) + r3_ref[...].astype(F32)

    other = [pl.BlockSpec(blk, lambda i, p_ref, f=f: (p_ref[0] ^ f, 0, 0)) for f in (1, 2, 3)]
    return pl.pallas_call(
        body, name=name,
        grid_spec=pltpu.PrefetchScalarGridSpec(
            num_scalar_prefetch=1, grid=(1,),
            in_specs=[pl.BlockSpec(blk, lambda i, p_ref: (p_ref[0], 0, 0))] + other,
            out_specs=pl.BlockSpec((h, cols), lambda i, p_ref: (p_ref[1], 0))),
        out_shape=jax.ShapeDtypeStruct((2 * h, cols), F32),
        compiler_params=_cparams(("arbitrary",)),
    )(place_arr, q, r, r, r)


def _join_halves(fs, *, name):
    n = len(fs)
    any_spec = pl.BlockSpec(memory_space=pl.ANY)

    def body(*refs):
        out_refs, send_sems, recv_sems = refs[n:2 * n], refs[2 * n], refs[2 * n + 1]
        x, y, c = _place()
        sent = []
        for i in range(n):
            mine = out_refs[i].at[_half(fs[i].shape[0], c)]
            cp = _remote(mine, mine, send_sems, recv_sems, i, (x, y, 1 - c))
            cp.start()
            sent.append(cp)
        for i in range(n):
            its = out_refs[i].at[_half(fs[i].shape[0], 1 - c)]
            _remote(its, its, send_sems, recv_sems, i, (x, y, 1 - c)).wait_recv()
        for cp in sent:
            cp.wait_send()

    return pl.pallas_call(
        body, name=name, in_specs=[any_spec] * n, out_specs=[any_spec] * n,
        out_shape=[jax.ShapeDtypeStruct(f.shape, f.dtype) for f in fs], input_output_aliases={i: i for i in range(n)},
        scratch_shapes=[pltpu.SemaphoreType.DMA((n,)), pltpu.SemaphoreType.DMA((n,))],
    )(*fs)


def _allreduce_small(buf, *, name):
    R = buf.shape[0]
    vm = pl.BlockSpec(memory_space=pltpu.VMEM)

    def body(b_ref, o_ref, slots, send_sems, recv_sems):
        x, y, c = _place()
        me = 4 * x + 2 * y + c
        slots[me] = b_ref[...]
        sends = []
        for kk in range(1, 8):
            fx, fy, fc = (kk >> 2) & 1, (kk >> 1) & 1, kk & 1
            cp = pltpu.make_async_remote_copy(
                src_ref=b_ref, dst_ref=slots.at[me], send_sem=send_sems.at[kk - 1], recv_sem=recv_sems.at[kk - 1],
                device_id=(x ^ fx, y ^ fy, c ^ fc), device_id_type=MESH)
            cp.start()
            sends.append(cp)
        for kk in range(1, 8):
            fx, fy, fc = (kk >> 2) & 1, (kk >> 1) & 1, kk & 1
            peer = 4 * (x ^ fx) + 2 * (y ^ fy) + (c ^ fc)
            pltpu.make_async_remote_copy(
                src_ref=b_ref, dst_ref=slots.at[peer], send_sem=send_sems.at[kk - 1], recv_sem=recv_sems.at[kk - 1],
                device_id=(x ^ fx, y ^ fy, c ^ fc), device_id_type=MESH).wait_recv()
        for cp in sends:
            cp.wait_send()
        acc = slots[0]
        for d in range(1, 8):
            acc = acc + slots[d]
        o_ref[...] = acc

    return pl.pallas_call(
        body, name=name, in_specs=[vm], out_specs=vm,
        out_shape=jax.ShapeDtypeStruct((R, LANES), F32),
        scratch_shapes=[pltpu.VMEM((8, R, LANES), F32), pltpu.SemaphoreType.DMA((7,)), pltpu.SemaphoreType.DMA((7,))],
    )(buf)


def _pad_rows(v, mult=8 * LANES):
    flat = v.reshape(-1)
    n = flat.shape[0]
    tot = -(-n // mult) * mult
    return jnp.pad(flat, (0, tot - n)).reshape(-1, LANES)


def _pack_small(parts):
    return jnp.concatenate([_pad_rows(p.astype(F32)) for p in parts], axis=0)


def _unpack_small(buf, shapes):
    out, r = [], 0
    for sh in shapes:
        n = math.prod(sh)
        rows = -(-n // (8 * LANES)) * 8
        out.append(buf[r:r + rows].reshape(-1)[:n].reshape(sh))
        r += rows
    return out


def _cols_to_shards(w):
    *lead, K, N = w.shape
    t = w.reshape(*lead, K, N_CHIPS, N // N_CHIPS)
    return jnp.moveaxis(t, -2, 0)


def _shards_to_cols(t):
    t = jnp.moveaxis(t, 0, -2)
    *lead, K, _, n = t.shape
    return t.reshape(*lead, K, N_CHIPS * n)


_BIG = ("m_w_in", "m_w_out", "a_w_in", "a_w_out", "f_w_up", "f_w_down")


def kernel(x, m_w_in, m_gate_bias, m_head_norm, m_w_out, a_w_in, a_b_in, a_sinks, a_w_out, a_b_out, norm_mix_pre, norm_mix_post, norm_ffn_pre, norm_ffn_post, f_w_up, f_conv_w, f_conv_b, f_w_down, loss_target, m_m_w_in, m_m_gate_bias, m_m_head_norm, m_m_w_out, m_a_w_in, m_a_b_in, m_a_sinks, m_a_w_out, m_a_b_out, m_norm_mix_pre, m_norm_mix_post, m_norm_ffn_pre, m_norm_ffn_post, m_f_w_up, m_f_conv_w, m_f_conv_b, m_f_w_down, v_m_w_in, v_m_gate_bias, v_m_head_norm, v_m_w_out, v_a_w_in, v_a_b_in, v_a_sinks, v_a_w_out, v_a_b_out, v_norm_mix_pre, v_norm_mix_post, v_norm_ffn_pre, v_norm_ffn_post, v_f_w_up, v_f_conv_w, v_f_conv_b, v_f_w_down):
    params = dict(m_w_in=m_w_in, m_gate_bias=m_gate_bias, m_head_norm=m_head_norm, m_w_out=m_w_out, a_w_in=a_w_in,
                  a_b_in=a_b_in, a_sinks=a_sinks, a_w_out=a_w_out, a_b_out=a_b_out, norm_mix_pre=norm_mix_pre,
                  norm_mix_post=norm_mix_post, norm_ffn_pre=norm_ffn_pre, norm_ffn_post=norm_ffn_post, f_w_up=f_w_up,
                  f_conv_w=f_conv_w, f_conv_b=f_conv_b, f_w_down=f_w_down)
    mom1 = dict(m_w_in=m_m_w_in, m_gate_bias=m_m_gate_bias, m_head_norm=m_m_head_norm, m_w_out=m_m_w_out,
                a_w_in=m_a_w_in, a_b_in=m_a_b_in, a_sinks=m_a_sinks, a_w_out=m_a_w_out, a_b_out=m_a_b_out,
                norm_mix_pre=m_norm_mix_pre, norm_mix_post=m_norm_mix_post, norm_ffn_pre=m_norm_ffn_pre,
                norm_ffn_post=m_norm_ffn_post, f_w_up=m_f_w_up, f_conv_w=m_f_conv_w, f_conv_b=m_f_conv_b,
                f_w_down=m_f_w_down)
    mom2 = dict(m_w_in=v_m_w_in, m_gate_bias=v_m_gate_bias, m_head_norm=v_m_head_norm, m_w_out=v_m_w_out,
                a_w_in=v_a_w_in, a_b_in=v_a_b_in, a_sinks=v_a_sinks, a_w_out=v_a_w_out, a_b_out=v_a_b_out,
                norm_mix_pre=v_norm_mix_pre, norm_mix_post=v_norm_mix_post, norm_ffn_pre=v_norm_ffn_pre,
                norm_ffn_post=v_norm_ffn_post, f_w_up=v_f_w_up, f_conv_w=v_f_conv_w, f_conv_b=v_f_conv_b,
                f_w_down=v_f_w_down)
    order = list(params)

    mx, my, mc = _place()
    chip = 2 * mx + my
    h0 = x[0]
    target = loss_target[0]

    def two_d(t):
        return t.reshape(-1, t.shape[-1])

    early = ("m_w_in",)
    late = tuple(n for n in _BIG if n not in early)
    mine = {n: two_d(params[n]).astype(MXU_DTYPE) for n in _BIG}

    def with_own_slot(names, theirs):
        return {n: lax.dynamic_update_slice(t, mine[n][None], (chip, 0, 0)) for n, t in zip(names, theirs)}

    gathered = with_own_slot(early, _gather_shards([mine[n] for n in early], name="gather_weights_early"))

    def in_place(shard, axis):
        width = shard.shape[axis]
        z = jnp.zeros(shard.shape[:axis] + (N_CHIPS * width,) + shard.shape[axis + 1:], F32)
        contrib = jnp.where(mc == 0, shard, 0.0)
        return lax.dynamic_update_slice_in_dim(z, contrib, chip * width, axis)

    sm_in = [in_place(a_b_in, 1), in_place(a_b_out, 1), in_place(f_conv_w, 2)]
    sm_full = _unpack_small(_allreduce_small(_pack_small(sm_in), name="gather_small"), [t.shape for t in sm_in])
    b_in_full, b_out_full, conv_w_full = sm_full

    W_in = _shards_to_cols(gathered["m_w_in"])
    W_all = jnp.pad(W_in, ((0, 0), (0, M_PROJ - W_in.shape[1])))
    gbias = jnp.pad(m_gate_bias[0].reshape(1, 2 * M_HEADS), ((0, 0), (0, LANES - 2 * M_HEADS)))

    grads = {}

    def ffn_fwd(i, z1, end):
        u = _mm(z1, W_up, b_layer=i, name=f"ffn_up{i}")
        a = _convgate_fwd(u, conv_w_full[i], f_conv_b[i:i + 1], name=f"ffn_act{i}")
        return u, a, _mm(a, W_down[i], rows=end, name=f"ffn_down{i}")

    z0 = _rms_fwd(h0, norm_mix_pre[0:1], out_dtype=MXU_DTYPE, name="mix_pre_norm0")
    proj = _mm(z0, W_all, name="mlstm_proj")
    gated, C_all, n_all, m_all, *late_theirs = _mlstm_fwd(proj, gbias, m_head_norm, [mine[n] for n in late],
                                                          name="mlstm_fwd")
    gathered.update(with_own_slot(late, late_theirs))
    W_mout = gathered["m_w_out"].reshape(D_MODEL, D_MODEL)
    A_in, A_out = _shards_to_cols(gathered["a_w_in"]), gathered["a_w_out"].reshape(D_MODEL, D_MODEL)
    W_up = gathered["f_w_up"]
    dsh = D_FF // N_CHIPS
    W_down = [gathered["f_w_down"][:, i * dsh:(i + 1) * dsh].reshape(D_FF, D_MODEL) for i in range(2)]
    zm0, h1, z1 = _mm(gated, W_mout, rows=_rows_sublayer_end(h0, norm_mix_post[0:1], norm_ffn_pre[0:1]), name="mlstm_out")
    u0, a0, (zf0, h2, y0) = ffn_fwd(0, z1, _rows_sublayer_end(h1, norm_ffn_post[0:1], norm_mix_pre[1:2]))

    aproj = _mm(y0, A_in, bias=b_in_full, name="attn_proj")
    sinks = a_sinks.reshape(A_KVH, A_G)
    ao = _attn_fwd(aproj, sinks, name="attn_fwd")
    zm1, h3, z3 = _mm(ao, A_out, bias=b_out_full, rows=_rows_sublayer_end(h2, norm_mix_post[1:2], norm_ffn_pre[1:2]),
                      name="attn_out")
    u1, a1, (zf1, dh, dzf1, g_post1, loss_lanes) = ffn_fwd(1, z3, _rows_last_sublayer_end(h3, norm_ffn_post[1:2], target))
    loss = lax.psum(loss_lanes[0, 0], ("x", "y", "c"))

    g_post, g_fpre, g_mpost, g_mpre = [None, g_post1], [None, None], [None, None], [None, None]
    dW_up, dW_down, dconv_w, dconv_b = [None, None], [None, None], [None, None], [None, None]

    def by_rows(g):
        return g.reshape(N_CHIPS, -1, g.shape[-1])

    def ffn_bwd(i, dzf, z1_, u, a, start, ride):
        da = _mm(dzf, W_down[i], tb=True, name=f"ffn_down_dx{i}")
        dW_down[i] = _mm(a, dzf, ta=True, name=f"ffn_down_dw{i}")
        riders = (list(ride) + [by_rows(dW_down[i])]) if ride is not None else []
        du, dconv_w[i], dconv_b[i], *got = _convgate_bwd(u, da, conv_w_full[i], f_conv_b[i:i + 1], riders,
                                                         name=f"ffn_act_bwd{i}")
        dW_up[i] = _mm(z1_, du, ta=True, out_stacked=True, name=f"ffn_up_dw{i}")
        return _mm(du, W_up, tb=True, b_layer=i, rows=start, name=f"ffn_up_dx{i}"), riders, got

    (dh3, g_fpre[1], dzm1, g_mpost[1], db_out), _, _ = ffn_bwd(
        1, dzf1, z3, u1, a1, _rows_sublayer_start_bwd(h3, dh, zm1, norm_ffn_pre[1:2], norm_mix_post[1:2]), None)
    dao = _mm(dzm1, A_out, tb=True, name="attn_out_dx")
    dA_out = _mm(ao, dzm1, ta=True, name="attn_out_dw")
    daq, dakv, dsinks, cs_q, cs_kv = _attn_bwd(aproj, sinks, dao, name="attn_bwd")
    db_in = jnp.concatenate([cs_q, cs_kv], axis=1)
    dh2, g_mpre[1], dzf0, g_post[0], _ = _mm(
        daq, A_in[:, :A_NQ], tb=True, add=_mm(dakv, A_in[:, A_NQ:], tb=True, name="attn_proj_kv_dx"),
        rows=_rows_sublayer_start_bwd(h2, dh3, zf0, norm_mix_pre[1:2], norm_ffn_post[0:1]), name="attn_proj_q_dx")
    dA_in = jnp.concatenate([_mm(y0, daq, ta=True, name="attn_proj_q_dw"),
                             _mm(y0, dakv, ta=True, name="attn_proj_kv_dw")], axis=1)

    c_arr = jnp.reshape(mc, (1,)).astype(jnp.int32)
    place_arr = jnp.stack([chip, mc]).astype(jnp.int32)

    def add_halves(tags, gs, got):
        return [_add_halves(g, t, c_arr, name=f"grad_add_halves_{tag}") for tag, g, t in zip(tags, gs, got)]

    (dh1, g_fpre[0], dzm0, g_mpost[0], _), rode, rode_got = ffn_bwd(
        0, dzf0, z1, u0, a0, _rows_sublayer_start_bwd(h1, dh2, zm0, norm_ffn_pre[0:1], norm_mix_post[0:1]),
        [_cols_to_shards(dA_in), by_rows(dA_out), dW_up[1], by_rows(dW_down[1])])
    rode_tags = ["a_w_in", "a_w_out", "f_w_up1", "f_w_down1", "f_w_down0"]
    dgated = _mm(dzm0, W_mout, tb=True, name="mlstm_out_dx")
    dW_mout = _mm(gated, dzm0, ta=True, name="mlstm_out_dw")
    next_tags, next_gs = ["f_w_up0", "m_w_out"], [dW_up[0], by_rows(dW_mout)]
    late_tags = rode_tags + next_tags
    late_part = (add_halves(rode_tags, rode, rode_got)
                 + add_halves(next_tags, next_gs, _swap_halves(next_gs, name="grad_swap_halves_late")))
    dproj, dgbias, dhn, *late_from = _mlstm_bwd(proj, gbias, m_head_norm, C_all, n_all, m_all, dgated, late_part,
                                                name="mlstm_bwd")
    grad_x, g_mpre[0] = _mm(dproj, W_all, tb=True, rows=_rows_first_sublayer_start_bwd(h0, dh1, norm_mix_pre[0:1]),
                            name="mlstm_proj_dx")
    dW_all = _mm(z0, dproj, ta=True, name="mlstm_proj_dw")
    dW_min = dW_all[:, :m_w_in.shape[-1] * N_CHIPS]

    early_tags, early_gs = ["m_w_in"], [_cols_to_shards(dW_min)]
    early_part = add_halves(early_tags, early_gs, _swap_halves(early_gs, name="grad_swap_halves_early"))
    early_from = _exchange_chips(early_part, name="grad_exchange_chips_early")
    tags = early_tags + late_tags
    halves = [_sum_chips(q, r, place_arr, name=f"grad_sum_chips_{tag}")
              for tag, q, r in zip(tags, early_part + late_part, list(early_from) + list(late_from))]
    reduced = dict(zip(tags, _join_halves(halves, name="grad_join_halves")))
    layer_grads = dict(m_w_in=[reduced["m_w_in"]], m_w_out=[reduced["m_w_out"]], a_w_in=[reduced["a_w_in"]],
                       a_w_out=[reduced["a_w_out"]], f_w_up=[reduced["f_w_up0"], reduced["f_w_up1"]],
                       f_w_down=[reduced["f_w_down0"], reduced["f_w_down1"]])

    small_g = [
        dgbias[:, :2 * M_HEADS].reshape(1, 2, M_HEADS),
        dhn,
        dsinks.reshape(1, A_QH),
        db_in, db_out,
        jnp.concatenate(g_mpre), jnp.concatenate(g_mpost), jnp.concatenate(g_fpre), jnp.concatenate(g_post),
        jnp.stack(dconv_w), jnp.concatenate(dconv_b),
    ]
    small_names = ["m_gate_bias", "m_head_norm", "a_sinks", "a_b_in", "a_b_out", "norm_mix_pre", "norm_mix_post",
                   "norm_ffn_pre", "norm_ffn_post", "f_conv_w", "f_conv_b"]
    red = _unpack_small(_allreduce_small(_pack_small(small_g), name="reduce_small"), [t.shape for t in small_g])
    for n, t in zip(small_names, red):
        if n in ("a_b_in", "a_b_out", "f_conv_w"):
            axis = t.ndim - 1
            width = params[n].shape[axis]
            t = lax.dynamic_slice_in_dim(t, chip * width, width, axis)
        grads[n] = t

    deltas, new_m, new_v = {}, {}, {}
    for n in _BIG:
        grads[n], deltas[n], new_m[n], new_v[n] = _adamw_layers(params[n], layer_grads[n], mom1[n], mom2[n],
                                                                name=f"adamw_{n}")
    two = lambda t: t.reshape(-1, t.shape[-1])
    res = _adamw_small([(two(params[n]), two(grads[n]), two(mom1[n]), two(mom2[n])) for n in small_names],
                       name="adamw_small")
    for n, (d, nm, nv) in zip(small_names, res):
        sh = params[n].shape
        deltas[n], new_m[n], new_v[n] = d.reshape(sh), nm.reshape(sh), nv.reshape(sh)

    return (loss, grad_x[None], *[grads[n] for n in order], *[deltas[n] for n in order],
            *[new_m[n] for n in order], *[new_v[n] for n in order])
```

```python
import functools
import math

import jax
import jax.numpy as jnp
from jax import lax
from jax.experimental import pallas as pl
from jax.experimental.pallas import tpu as pltpu

F32 = jnp.float32
MXU_DTYPE = jnp.bfloat16
WIRE_DTYPE = jnp.bfloat16
MESH = pl.DeviceIdType.MESH

D_MODEL = 1024
EPS = 1e-6
M_HEADS, M_QK, M_V, M_CHUNK = 8, 64, 128, 128
GATE_CAP = 15.0
A_DH, A_QH, A_KVH, A_G, A_BLK = 64, 16, 2, 8, 128
D_FF = 2816
N_CHIPS = 4
LANES = 128
VMEM_LIMIT = 56 * 1024 * 1024

ADAM_LR, ADAM_B1, ADAM_B2, ADAM_EPS, ADAM_WD, ADAM_STEP = 0.001, 0.9, 0.999, 1e-08, 0.01, 10


def _cparams(sem):
    return pltpu.CompilerParams(dimension_semantics=sem, vmem_limit_bytes=VMEM_LIMIT)


def _pick(n, cands):
    for c in cands:
        if n % c == 0:
            return c
    return n


class _Rows:
    def __init__(self, tiles, vecs, outs, fn):
        self.tiles, self.vecs, self.outs, self.fn = list(tiles), list(vecs), list(outs), fn


ROWS_TILE = 512
ROWS_FULL_K = 4224
WIDE_N = 3200


def _mm(a, b, *, ta=False, tb=False, out_dtype=F32, bias=None, add=None, b_layer=None, out_stacked=False, rows=None,
        name):
    if ta:
        K, M = a.shape
    else:
        M, K = a.shape
    b_stacked = b_layer is not None
    if b_stacked:
        assert not ta
        n_sh = b.shape[2]
        w_rows, w_cols = D_MODEL, N_CHIPS * n_sh
        N, Kb = (w_rows, w_cols) if tb else (w_cols, w_rows)
    elif tb:
        N, Kb = b.shape
    else:
        Kb, N = b.shape
    assert K == Kb, (a.shape, b.shape)
    tm = _pick(M, (1024, 1408, 512, 256, 128))
    tn = _pick(N, (1024, 1408, 1280, 640, 512, 256, 128))
    tk = K if K <= 2816 else _pick(K, (2816, 2048, 1408, 1024, 512, 256, 128))
    if tn < 1024 <= N <= WIDE_N:
        tn, tm, tk = N, min(tm, 512), min(tk, 1024)
    shards_per_step = 1
    if b_stacked and tb:
        shards_per_step = 2
        tk = shards_per_step * n_sh
    if (b_stacked and not tb) or out_stacked:
        tn = N // N_CHIPS
    if rows is not None:
        assert tn == N and not out_stacked and not ta
        tm = min(tm, ROWS_TILE)
        if K <= ROWS_FULL_K:
            tk = K
    nk = K // tk
    dn = (((0 if ta else 1,), (1 if tb else 0,)), ((), ()))
    has_bias, has_add = bias is not None, add is not None
    n_tiles, n_vecs, n_outs = (len(rows.tiles), len(rows.vecs), len(rows.outs)) if rows is not None else (0, 0, 1)

    def body(*refs):
        a_ref, b_ref = refs[0], refs[1]
        pos = 2
        bias_ref = add_ref = None
        if has_bias:
            bias_ref = refs[pos]
            pos += 1
        if has_add:
            add_ref = refs[pos]
            pos += 1
        tile_refs, vec_refs = refs[pos:pos + n_tiles], refs[pos + n_tiles:pos + n_tiles + n_vecs]
        pos += n_tiles + n_vecs
        out_refs = refs[pos:pos + n_outs]
        acc_ref = refs[pos + n_outs] if nk > 1 else None

        def finish(r):
            if has_bias:
                r = r + bias_ref[...]
            if has_add:
                r = r + add_ref[...]
            if rows is None:
                out_refs[0][...] = r.astype(out_dtype)
                return
            vals = rows.fn(r, [t[...] for t in tile_refs], [v[...] for v in vec_refs])
            first = pl.program_id(0) == 0
            for (kind, _), o_ref, val in zip(rows.outs, out_refs, vals):
                if kind == "tile":
                    o_ref[...] = val.astype(o_ref.dtype)
                else:
                    @pl.when(first)
                    def _(o_ref=o_ref, val=val):
                        o_ref[...] = val

                    @pl.when(jnp.logical_not(first))
                    def _(o_ref=o_ref, val=val):
                        o_ref[...] += val

        if shards_per_step > 1:
            part = sum(lax.dot_general(a_ref[:, s * n_sh:(s + 1) * n_sh], b_ref[s], dn, preferred_element_type=F32)
                       for s in range(shards_per_step))
        else:
            part = lax.dot_general(a_ref[...], b_ref[...], dn, preferred_element_type=F32)
        if nk == 1:
            finish(part)
        else:
            k = pl.program_id(2)

            @pl.when(k == 0)
            def _():
                acc_ref[...] = part

            @pl.when(k > 0)
            def _():
                acc_ref[...] += part

            @pl.when(k == nk - 1)
            def _():
                finish(acc_ref[...])

    a_spec = pl.BlockSpec((tk, tm), lambda i, j, k: (k, i)) if ta else pl.BlockSpec((tm, tk), lambda i, j, k: (i, k))
    if b_stacked and tb:
        off = b_layer * (w_rows // tn)
        b_spec = pl.BlockSpec((shards_per_step, tn, n_sh), lambda i, j, k: (k, off + j, 0))
    elif b_stacked:
        off = b_layer * (w_rows // tk)
        b_spec = pl.BlockSpec((None, tk, tn), lambda i, j, k: (j, off + k, 0))
    elif tb:
        b_spec = pl.BlockSpec((tn, tk), lambda i, j, k: (j, k))
    else:
        b_spec = pl.BlockSpec((tk, tn), lambda i, j, k: (k, j))
    if out_stacked:
        out_spec = pl.BlockSpec((None, tm, tn), lambda i, j, k: (j, i, 0))
        out_shape = jax.ShapeDtypeStruct((N_CHIPS, M, tn), out_dtype)
    else:
        out_spec = pl.BlockSpec((tm, tn), lambda i, j, k: (i, j))
        out_shape = jax.ShapeDtypeStruct((M, N), out_dtype)
    in_specs, args = [a_spec, b_spec], [a, b]
    if has_bias:
        in_specs.append(pl.BlockSpec((1, tn), lambda i, j, k: (0, j)))
        args.append(bias)
    if has_add:
        in_specs.append(pl.BlockSpec((tm, tn), lambda i, j, k: (i, j)))
        args.append(add)
    semantics = ("parallel", "parallel", "arbitrary")
    if rows is not None:
        tile_spec = pl.BlockSpec((tm, N), lambda i, j, k: (i, 0))
        vec_spec = pl.BlockSpec((1, N), lambda i, j, k: (0, 0))
        in_specs += [tile_spec] * n_tiles + [vec_spec] * n_vecs
        args += rows.tiles + rows.vecs
        out_spec = [tile_spec if kind == "tile" else vec_spec for kind, _ in rows.outs]
        out_shape = [jax.ShapeDtypeStruct((M, N) if kind == "tile" else (1, N), dt) for kind, dt in rows.outs]
        semantics = ("arbitrary", "arbitrary", "arbitrary")
    return pl.pallas_call(
        body, name=name,
        grid=(M // tm, N // tn, nk),
        in_specs=in_specs,
        out_specs=out_spec,
        out_shape=out_shape,
        scratch_shapes=[pltpu.VMEM((tm, tn), F32)] if nk > 1 else [],
        compiler_params=_cparams(semantics),
    )(*args)


ROW_TILE = 512


def _rms_fwd(x, g, *, out_dtype, name):
    S, D = x.shape
    tm = _pick(S, (ROW_TILE, 256, 128))

    def body(x_ref, g_ref, o_ref):
        o_ref[...] = _rms(x_ref[...], g_ref[...]).astype(out_dtype)

    row = pl.BlockSpec((tm, D), lambda i: (i, 0))
    return pl.pallas_call(
        body, name=name, grid=(S // tm,), in_specs=[row, pl.BlockSpec((1, D), lambda i: (0, 0))], out_specs=row,
        out_shape=jax.ShapeDtypeStruct((S, D), out_dtype),
        compiler_params=_cparams(("parallel",)),
    )(x, g)


def _rms(x, g):
    return x * lax.rsqrt(jnp.mean(x * x, axis=-1, keepdims=True) + EPS) * g


def _rms_vjp(x, g, dy):
    r = lax.rsqrt(jnp.mean(x * x, axis=-1, keepdims=True) + EPS)
    xh = x * r
    gd = dy * g
    dx = r * (gd - xh * jnp.mean(gd * xh, axis=-1, keepdims=True))
    return dx, jnp.sum(dy * xh, axis=0, keepdims=True)


def _rows_sublayer_end(res, g_post, g_pre_next):
    def fn(z, tiles, vecs):
        h = tiles[0] + _rms(z, vecs[0])
        return [z, h, _rms(h, vecs[1])]

    return _Rows([res], [g_post, g_pre_next], [("tile", F32), ("tile", F32), ("tile", MXU_DTYPE)], fn)


def _rows_last_sublayer_end(res, g_post, target):
    def fn(z, tiles, vecs):
        y = tiles[0] + _rms(z, vecs[0])
        err = y - tiles[1]
        dy = err * (1.0 / err.shape[-1])
        dz, dg = _rms_vjp(z, vecs[0], dy)
        loss = 0.5 * jnp.sum(jnp.mean(err * err, axis=-1, keepdims=True), axis=0, keepdims=True)
        return [z, dy, dz, dg, jnp.broadcast_to(loss, dg.shape)]

    return _Rows([res, target], [g_post], [("tile", F32), ("tile", F32), ("tile", MXU_DTYPE), ("vec", F32), ("vec", F32)], fn)


def _rows_sublayer_start_bwd(x, res, z_below, g_pre, g_post_below):
    def fn(dz, tiles, vecs):
        dx, dg_pre = _rms_vjp(tiles[0], vecs[0], dz)
        dh = tiles[1] + dx
        dzb, dg_post = _rms_vjp(tiles[2], vecs[1], dh)
        return [dh, dg_pre, dzb, dg_post, jnp.sum(dzb, axis=0, keepdims=True)]

    return _Rows([x, res, z_below], [g_pre, g_post_below],
                 [("tile", F32), ("vec", F32), ("tile", MXU_DTYPE), ("vec", F32), ("vec", F32)], fn)


def _rows_first_sublayer_start_bwd(x, res, g_pre):
    def fn(dz, tiles, vecs):
        dx, dg_pre = _rms_vjp(tiles[0], vecs[0], dz)
        return [tiles[1] + dx, dg_pre]

    return _Rows([x, res], [g_pre], [("tile", F32), ("vec", F32)], fn)


def _mx(t):
    return t.astype(MXU_DTYPE)


def _mxf(t):
    return t.astype(MXU_DTYPE).astype(F32)


def _rawdot(a, b, ca, cb):
    return lax.dot_general(_mx(a), _mx(b), (((ca,), (cb,)), ((), ())), preferred_element_type=F32)


@functools.partial(jax.custom_vjp, nondiff_argnums=(2, 3))
def _dot(a, b, ca, cb):
    return _rawdot(a, b, ca, cb)


def _dot_fwd(a, b, ca, cb):
    return _rawdot(a, b, ca, cb), (a, b)


def _dot_bwd(ca, cb, res, g):
    a, b = res
    bj = 1 if cb == 0 else 0
    ai = 0 if ca == 1 else 1
    da = _rawdot(g, b, 1, bj) if ca == 1 else _rawdot(b, g, bj, 1)
    db = _rawdot(a, g, ai, 0) if cb == 0 else _rawdot(g, a, 0, ai)
    return da.astype(a.dtype), db.astype(b.dtype)


_dot.defvjp(_dot_fwd, _dot_bwd)


def _softcap(z):
    return GATE_CAP * jnp.tanh(z / GATE_CAP)


def _log_sigmoid(z):
    return jnp.minimum(z, 0.0) - jnp.log(1.0 + jnp.exp(-jnp.abs(z)))


def _sigmoid(z):
    return 0.5 * jnp.tanh(0.5 * z) + 0.5


def _lane_col(t, lane_index):
    lane = lax.broadcasted_iota(jnp.int32, t.shape, 1)
    return jnp.sum(jnp.where(lane == lane_index, t, 0.0), axis=1, keepdims=True)


def _mlstm_gates(G, bias):
    L = G.shape[0]
    z = _softcap(G + bias)
    ig = z
    lf = _log_sigmoid(z)
    ti = lax.broadcasted_iota(jnp.int32, (L, L), 0)
    si = lax.broadcasted_iota(jnp.int32, (L, L), 1)
    tril = (si <= ti).astype(F32)
    b = lax.dot_general(tril, lf, (((1,), (0,)), ((), ())), precision=lax.Precision.HIGHEST, preferred_element_type=F32)
    bL = jnp.sum(lf, axis=0, keepdims=True)
    return ig, b, ig.T, b.T, bL


def _mlstm_head(h, q, k_pair, v, o, ig_all, b_all, igT, bT, bL_all, hn, C, n, m):
    L = q.shape[0]
    k = jnp.where(_own_lanes(h, k_pair.shape), k_pair, 0.0)
    ti = lax.broadcasted_iota(jnp.int32, (L, L), 0)
    si = lax.broadcasted_iota(jnp.int32, (L, L), 1)
    lower = (si <= ti)
    ig = _lane_col(ig_all, h)
    b = _lane_col(b_all, M_HEADS + h)
    ig_row = igT[h:h + 1, :]
    b_row = bT[M_HEADS + h:M_HEADS + h + 1, :]
    bL = _lane_col(bL_all, M_HEADS + h)
    inter = b + m
    dlog = jnp.where(lower, b - b_row + ig_row, -jnp.inf)
    m_t = lax.stop_gradient(jnp.maximum(inter, jnp.max(dlog, axis=-1, keepdims=True)))
    qs = q * (M_QK ** -0.5)
    w = _dot(qs, k, 1, 1) * jnp.exp(dlog - m_t)
    s_inter = jnp.exp(inter - m_t)
    num = _dot(w, v, 1, 0) + s_inter * _dot(qs, C, 1, 0)
    den = jnp.sum(w, axis=-1, keepdims=True) + s_inter * jnp.sum(_mxf(qs) * _mxf(n), axis=-1, keepdims=True)
    hout = num * (1.0 / jnp.maximum(jnp.abs(den), jnp.exp(-m_t)))
    tail = bL - b + ig
    m_new = lax.stop_gradient(jnp.maximum(bL + m, jnp.max(tail, axis=0, keepdims=True)))
    ws = jnp.exp(tail - m_new)
    decay = jnp.exp(bL + m - m_new)
    wk = ws * k
    C_new = decay * C + _dot(wk, v, 0, 0)
    n_new = decay * n + jnp.sum(_mxf(ws) * _mxf(k), axis=0, keepdims=True)
    hs = hout * lax.rsqrt(jnp.mean(hout * hout, axis=-1, keepdims=True) + EPS) * hn
    gated = _sigmoid(o) * hs
    return (gated, C_new, n_new), m_new


M_OFF_Q, M_OFF_K = 0, M_HEADS * M_QK
M_OFF_V = 2 * M_HEADS * M_QK
M_OFF_O = M_OFF_V + M_HEADS * M_V
M_OFF_G = M_OFF_O + M_HEADS * M_V
M_PROJ = M_OFF_G + LANES
M_PAIRS = M_HEADS * M_QK // LANES


def _head_cols(off, h):
    return slice(off + h * LANES, off + (h + 1) * LANES)


def _own_lanes(h, shape):
    low = lax.broadcasted_iota(jnp.int32, shape, 1) < M_QK
    return low if h % 2 == 0 else jnp.logical_not(low)


def _mlstm_specs(NC, rev):
    H, L = M_HEADS, M_CHUNK
    cc = (lambda c: NC - 1 - c) if rev else (lambda c: c)
    proj = pl.BlockSpec((L, M_PROJ), lambda c: (cc(c), 0))
    vec = pl.BlockSpec((1, LANES), lambda c: (0, 0))
    hn = pl.BlockSpec((1, H * M_V), lambda c: (0, 0))
    hv = pl.BlockSpec((L, H * M_V), lambda c: (cc(c), 0))
    Cs = pl.BlockSpec((None, H, LANES, M_V), lambda c: (cc(c), 0, 0, 0))
    ns = pl.BlockSpec((None, H, 1, LANES), lambda c: (cc(c), 0, 0, 0))
    ms = pl.BlockSpec((None, H, 1, 1), lambda c: (cc(c), 0, 0, 0))
    return proj, vec, hn, hv, Cs, ns, ms


_MLSTM_STATE = [pltpu.VMEM((M_HEADS, LANES, M_V), F32), pltpu.VMEM((M_HEADS, 1, LANES), F32),
                pltpu.VMEM((M_HEADS, 1, 1), F32)]


def _mlstm_fwd(proj, bias, hn, shards, *, name):
    S = proj.shape[0]
    H, NC = M_HEADS, S // M_CHUNK
    ps, vec, hns, hv, Cs, ns, ms = _mlstm_specs(NC, False)
    nsh = len(shards)
    any_spec = pl.BlockSpec(memory_space=pl.ANY)

    def body(p_ref, b_ref, hn_ref, *rest):
        gated_ref, C_all, n_all, m_all = rest[nsh:nsh + 4]
        C_s, n_s, m_s, send_sems, recv_sems = rest[2 * nsh + 4:]
        gather = _Gather(rest[:nsh], rest[nsh + 4:2 * nsh + 4], send_sems, recv_sems)

        @pl.when(pl.program_id(0) == 0)
        def _():
            C_s[...] = jnp.zeros_like(C_s)
            n_s[...] = jnp.zeros_like(n_s)
            m_s[...] = jnp.zeros_like(m_s)
            gather.issue()

        gate_terms = _mlstm_gates(p_ref[:, M_OFF_G:M_OFF_G + LANES], b_ref[...])
        for h in range(H):
            C, n, m = C_s[h], n_s[h], m_s[h]
            C_all[h] = C
            n_all[h] = n
            m_all[h] = m
            (gated, Cn, nn), mn = _mlstm_head(
                h, p_ref[:, _head_cols(M_OFF_Q, h // 2)], p_ref[:, _head_cols(M_OFF_K, h // 2)],
                p_ref[:, _head_cols(M_OFF_V, h)], p_ref[:, _head_cols(M_OFF_O, h)], *gate_terms,
                hn_ref[:, _head_cols(0, h)], C, n, m)
            gated_ref[:, _head_cols(0, h)] = gated.astype(gated_ref.dtype)
            C_s[h] = Cn
            n_s[h] = nn
            m_s[h] = mn

        @pl.when(pl.program_id(0) == NC - 1)
        def _():
            gather.finish()

    return pl.pallas_call(
        body, name=name, grid=(NC,),
        in_specs=[ps, vec, hns] + [any_spec] * nsh,
        out_specs=[hv, Cs, ns, ms] + [any_spec] * nsh,
        out_shape=[jax.ShapeDtypeStruct((S, H * M_V), MXU_DTYPE),
                   jax.ShapeDtypeStruct((NC, H, LANES, M_V), F32),
                   jax.ShapeDtypeStruct((NC, H, 1, LANES), F32),
                   jax.ShapeDtypeStruct((NC, H, 1, 1), F32)] + _Gather.out_shape(shards),
        scratch_shapes=list(_MLSTM_STATE) + _Gather.semaphores(nsh),
        compiler_params=_cparams(("arbitrary",)),
    )(proj, bias, hn, *shards)


def _mlstm_bwd(proj, bias, hn, C_all, n_all, m_all, dgated, qs, *, name):
    S = proj.shape[0]
    H, NC = M_HEADS, S // M_CHUNK
    ps, vec, hns, hv, Cs, ns, ms = _mlstm_specs(NC, True)
    nq = len(qs)
    any_spec = pl.BlockSpec(memory_space=pl.ANY)

    def body(p_ref, b_ref, hn_ref, C_ref, n_ref, m_ref, dg_ref, *rest):
        dp_ref, db_ref, dhn_ref = rest[nq:nq + 3]
        dC_s, dn_s, send_sems, recv_sems = rest[2 * nq + 3:]
        exchange = _Exchange(rest[:nq], rest[nq + 3:2 * nq + 3], send_sems, recv_sems)

        @pl.when(pl.program_id(0) == 0)
        def _():
            dC_s[...] = jnp.zeros_like(dC_s)
            dn_s[...] = jnp.zeros_like(dn_s)
            db_ref[...] = jnp.zeros_like(db_ref)
            dhn_ref[...] = jnp.zeros_like(dhn_ref)
            exchange.issue()

        gate_terms, gates_vjp = jax.vjp(_mlstm_gates, p_ref[:, M_OFF_G:M_OFF_G + LANES], b_ref[...])
        d_terms = [jnp.zeros_like(t) for t in gate_terms]
        for h in range(H):
            def head(q, k, v, o, *rest, h=h):
                return _mlstm_head(h, q, k, v, o, *rest, m_ref[h])

            prim = (p_ref[:, _head_cols(M_OFF_Q, h // 2)], p_ref[:, _head_cols(M_OFF_K, h // 2)],
                    p_ref[:, _head_cols(M_OFF_V, h)], p_ref[:, _head_cols(M_OFF_O, h)], *gate_terms,
                    hn_ref[:, _head_cols(0, h)], C_ref[h], n_ref[h])
            _, vjp, _ = jax.vjp(head, *prim, has_aux=True)
            dq, dk, dv, do, *d_gate, dhnh, dC, dn = vjp((dg_ref[:, _head_cols(0, h)].astype(F32), dC_s[h], dn_s[h]))
            if h % 2 == 0:
                dq_pair, dk_pair = dq, dk
            else:
                dp_ref[:, _head_cols(M_OFF_Q, h // 2)] = (dq_pair + dq).astype(dp_ref.dtype)
                dp_ref[:, _head_cols(M_OFF_K, h // 2)] = (dk_pair + dk).astype(dp_ref.dtype)
            dp_ref[:, _head_cols(M_OFF_V, h)] = dv.astype(dp_ref.dtype)
            dp_ref[:, _head_cols(M_OFF_O, h)] = do.astype(dp_ref.dtype)
            d_terms = [a + g for a, g in zip(d_terms, d_gate)]
            dhn_ref[:, _head_cols(0, h)] += dhnh
            dC_s[h] = dC
            dn_s[h] = dn
        dG, dbias = gates_vjp(tuple(d_terms))
        dp_ref[:, M_OFF_G:M_OFF_G + LANES] = dG.astype(dp_ref.dtype)
        db_ref[...] += dbias

        @pl.when(pl.program_id(0) == NC - 1)
        def _():
            exchange.finish()

    return pl.pallas_call(
        body, name=name, grid=(NC,),
        in_specs=[ps, vec, hns, Cs, ns, ms, hv] + [any_spec] * nq,
        out_specs=[ps, vec, hns] + [any_spec] * nq,
        out_shape=[jax.ShapeDtypeStruct((S, M_PROJ), MXU_DTYPE), jax.ShapeDtypeStruct((1, LANES), F32),
                   jax.ShapeDtypeStruct((1, H * M_V), F32)] + _Exchange.out_shape(qs),
        scratch_shapes=list(_MLSTM_STATE[:2]) + _Exchange.semaphores(nq),
        compiler_params=_cparams(("arbitrary",)),
    )(proj, bias, hn, C_all, n_all, m_all, dgated, *qs)


A_NQ = A_QH * A_DH
A_NKV = 2 * A_KVH * A_DH
A_PAIRS = A_G // 2


def _sink_softmax_parts(s, sink):
    mx = jnp.maximum(jnp.max(s, axis=-1, keepdims=True), sink)
    e = jnp.exp(s - mx)
    e_sink = jnp.exp(sink - mx)
    r = 1.0 / (jnp.sum(e, axis=-1, keepdims=True) + e_sink)
    return e * r, e_sink * r


@jax.custom_vjp
def _sink_softmax(s, sink):
    return _sink_softmax_parts(s, sink)[0]


def _sink_softmax_fwd(s, sink):
    p, p_sink = _sink_softmax_parts(s, sink)
    return p, (p, p_sink)


def _sink_softmax_bwd(res, dp):
    p, p_sink = res
    t = jnp.sum(dp * p, axis=-1, keepdims=True)
    return p * (dp - t), -p_sink * t


_sink_softmax.defvjp(_sink_softmax_fwd, _sink_softmax_bwd)


def _attn_group(Ps, KLO, KHI, VLO, VHI, sinks, first):
    B = Ps[0].shape[0]
    R = len(Ps) * B
    q2 = jnp.concatenate(Ps, axis=0) * (A_DH ** -0.5)
    s = jnp.concatenate([_dot(q2, KLO, 1, 1), _dot(q2, KHI, 1, 1)], axis=0)
    qi = lax.broadcasted_iota(jnp.int32, (2 * R, 2 * B), 0) & (B - 1)
    ku = lax.broadcasted_iota(jnp.int32, (2 * R, 2 * B), 1)
    diff = qi - (ku - B)
    mask = (diff >= 0) & (diff < B) & ((ku >= B) | jnp.logical_not(first))
    s = jnp.where(mask, s, -jnp.inf)
    ri = lax.broadcasted_iota(jnp.int32, (2 * R, A_G), 0)
    head = 2 * lax.shift_right_logical(ri & (R - 1), B.bit_length() - 1) + lax.shift_right_logical(ri, R.bit_length() - 1)
    onehot = head == lax.broadcasted_iota(jnp.int32, (2 * R, A_G), 1)
    sink = jnp.sum(jnp.where(onehot, sinks, 0.0), axis=1, keepdims=True)
    p = _sink_softmax(s, sink)
    o = _dot(p[:R], VLO, 1, 0) + _dot(p[R:], VHI, 1, 0)
    return tuple(o[j * B:(j + 1) * B] for j in range(len(Ps)))


def _swap_halves_of_lanes(t):
    return pltpu.roll(t, LANES // 2, 1)


def _kv_operands(kvp_ref, kvc_ref, h):
    kk = jnp.concatenate([kvp_ref[:, :LANES], kvc_ref[:, :LANES]], axis=0)
    vv = jnp.concatenate([kvp_ref[:, LANES:], kvc_ref[:, LANES:]], axis=0)
    low = lax.broadcasted_iota(jnp.int32, kk.shape, 1) < A_DH
    own = low if h == 0 else jnp.logical_not(low)
    k_own = jnp.where(own, kk, 0.0)
    v_own = jnp.where(own, vv, 0.0)
    k_oth, v_oth = _swap_halves_of_lanes(k_own), _swap_halves_of_lanes(v_own)
    if h == 0:
        return own, k_own, k_oth, v_own, v_oth
    return own, k_oth, k_own, v_oth, v_own


def _pair_cols(h, j):
    c = (h * A_PAIRS + j) * LANES
    return slice(c, c + LANES)


def _attn_fwd(proj, sinks, *, name):
    S = proj.shape[0]
    NB = S // A_BLK
    kv_blk = A_NQ // A_NKV
    qs = pl.BlockSpec((A_BLK, A_NQ), lambda n: (n, 0))
    cur = pl.BlockSpec((A_BLK, A_NKV), lambda n: (n, kv_blk))
    prev = pl.BlockSpec((A_BLK, A_NKV), lambda n: (jnp.maximum(n - 1, 0), kv_blk))
    sk = pl.BlockSpec((A_KVH, A_G), lambda n: (0, 0))

    def body(q_ref, kvp_ref, kvc_ref, s_ref, o_ref):
        first = pl.program_id(0) == 0
        for h in range(A_KVH):
            _, KLO, KHI, VLO, VHI = _kv_operands(kvp_ref, kvc_ref, h)
            Ps = tuple(q_ref[:, _pair_cols(h, j)] for j in range(A_PAIRS))
            outs = _attn_group(Ps, KLO, KHI, VLO, VHI, s_ref[h:h + 1, :], first)
            for j in range(A_PAIRS):
                o_ref[:, _pair_cols(h, j)] = outs[j].astype(o_ref.dtype)

    return pl.pallas_call(
        body, name=name, grid=(NB,),
        in_specs=[qs, prev, cur, sk], out_specs=qs,
        out_shape=jax.ShapeDtypeStruct((S, A_NQ), MXU_DTYPE),
        compiler_params=_cparams(("parallel",)),
    )(proj, proj, proj, sinks)


def _attn_bwd(proj, sinks, do, *, name):
    S = proj.shape[0]
    NB = S // A_BLK
    last = NB - 1
    kv_blk = A_NQ // A_NKV
    qs = pl.BlockSpec((A_BLK, A_NQ), lambda n: (jnp.minimum(n, last), 0))
    cur = pl.BlockSpec((A_BLK, A_NKV), lambda n: (jnp.minimum(n, last), kv_blk))
    prev = pl.BlockSpec((A_BLK, A_NKV), lambda n: (jnp.clip(n - 1, 0, last), kv_blk))
    sk = pl.BlockSpec((A_KVH, A_G), lambda n: (0, 0))
    lag = pl.BlockSpec((A_BLK, A_NKV), lambda n: (jnp.maximum(n - 1, 0), 0))
    cq_spec = pl.BlockSpec((1, A_NQ), lambda n: (0, 0))
    ckv_spec = pl.BlockSpec((1, A_NKV), lambda n: (0, 0))

    def body(q_ref, kvp_ref, kvc_ref, s_ref, do_ref, dq_ref, dkv_ref, ds_ref, cq_ref, ckv_ref, keep):
        n = pl.program_id(0)

        @pl.when(n == 0)
        def _():
            keep[...] = jnp.zeros_like(keep)
            ds_ref[...] = jnp.zeros_like(ds_ref)
            cq_ref[...] = jnp.zeros_like(cq_ref)
            ckv_ref[...] = jnp.zeros_like(ckv_ref)

        @pl.when(n < NB)
        def _():
            f = functools.partial(_attn_group, first=n == 0)
            dk = jnp.zeros((2 * A_BLK, LANES), F32)
            dv = jnp.zeros((2 * A_BLK, LANES), F32)
            ds_rows = []
            for h in range(A_KVH):
                own, KLO, KHI, VLO, VHI = _kv_operands(kvp_ref, kvc_ref, h)
                Ps = tuple(q_ref[:, _pair_cols(h, j)] for j in range(A_PAIRS))
                _, vjp = jax.vjp(f, Ps, KLO, KHI, VLO, VHI, s_ref[h:h + 1, :])
                dPs, dKLO, dKHI, dVLO, dVHI, dsk = vjp(
                    tuple(do_ref[:, _pair_cols(h, j)].astype(F32) for j in range(A_PAIRS)))
                for j in range(A_PAIRS):
                    dq_ref[:, _pair_cols(h, j)] = dPs[j].astype(dq_ref.dtype)
                    cq_ref[:, _pair_cols(h, j)] += jnp.sum(dPs[j], axis=0, keepdims=True)
                dk_own, dk_oth = (dKLO, dKHI) if h == 0 else (dKHI, dKLO)
                dv_own, dv_oth = (dVLO, dVHI) if h == 0 else (dVHI, dVLO)
                dk = dk + jnp.where(own, dk_own, 0.0) + _swap_halves_of_lanes(jnp.where(own, 0.0, dk_oth))
                dv = dv + jnp.where(own, dv_own, 0.0) + _swap_halves_of_lanes(jnp.where(own, 0.0, dv_oth))
                ds_rows.append(dsk)
            ds_ref[...] += jnp.concatenate(ds_rows, axis=0)
            dkv = jnp.concatenate([dk, dv], axis=1)
            done = keep[...] + dkv[:A_BLK]
            dkv_ref[...] = done.astype(dkv_ref.dtype)
            ckv_ref[...] += jnp.sum(done, axis=0, keepdims=True)
            keep[...] = dkv[A_BLK:]

        @pl.when(n == NB)
        def _():
            done = keep[...]
            dkv_ref[...] = done.astype(dkv_ref.dtype)
            ckv_ref[...] += jnp.sum(done, axis=0, keepdims=True)

    return pl.pallas_call(
        body, name=name, grid=(NB + 1,),
        in_specs=[qs, prev, cur, sk, qs],
        out_specs=[qs, lag, sk, cq_spec, ckv_spec],
        out_shape=[jax.ShapeDtypeStruct((S, A_NQ), MXU_DTYPE), jax.ShapeDtypeStruct((S, A_NKV), MXU_DTYPE),
                   jax.ShapeDtypeStruct((A_KVH, A_G), F32), jax.ShapeDtypeStruct((1, A_NQ), F32),
                   jax.ShapeDtypeStruct((1, A_NKV), F32)],
        scratch_shapes=[pltpu.VMEM((A_BLK, A_NKV), F32)],
        compiler_params=_cparams(("arbitrary",)),
    )(proj, proj, proj, sinks, do)


HALO = 8


def _shift_rows(t, j):
    if j == 0:
        return t
    return pltpu.roll(t, j % t.shape[0], 0)


def _conv_gate(gate_ext, cw, cb):
    return cb + cw[0:1, :] * _shift_rows(gate_ext, 2) + cw[1:2, :] * _shift_rows(gate_ext, 1) + cw[2:3, :] * gate_ext


def _convgate_fwd(u, cw, cb, *, name):
    S, F2 = u.shape
    F = F2 // 2
    tm = _pick(S, (256, 128))
    hb = tm // HALO
    urow = pl.BlockSpec((tm, F2), lambda i: (i, 0))
    uprev = pl.BlockSpec((HALO, F), lambda i: (jnp.maximum(i * hb - 1, 0), 0))

    def body(u_ref, up_ref, cw_ref, cb_ref, a_ref):
        i = pl.program_id(0)
        gate = u_ref[:, :F]
        val = u_ref[:, F:]
        prev = jnp.where(i > 0, up_ref[...], 0.0)
        gc = _conv_gate(jnp.concatenate([prev, gate], axis=0), cw_ref[...], cb_ref[...])[HALO:]
        a_ref[...] = (gc * _sigmoid(gc) * val).astype(a_ref.dtype)

    return pl.pallas_call(
        body, name=name, grid=(S // tm,),
        in_specs=[urow, uprev, pl.BlockSpec((3, F), lambda i: (0, 0)), pl.BlockSpec((1, F), lambda i: (0, 0))],
        out_specs=pl.BlockSpec((tm, F), lambda i: (i, 0)),
        out_shape=jax.ShapeDtypeStruct((S, F), MXU_DTYPE),
        compiler_params=_cparams(("parallel",)),
    )(u, u, cw, cb)


def _convgate_bwd(u, da, cw, cb, gs, *, name):
    S, F2 = u.shape
    ng = len(gs)
    any_spec = pl.BlockSpec(memory_space=pl.ANY)
    F = F2 // 2
    tm = _pick(S, (128,))
    hb = tm // HALO
    nt = S // tm
    nh = S // HALO
    urow = pl.BlockSpec((tm, F2), lambda i: (i, 0))
    uprev = pl.BlockSpec((HALO, F), lambda i: (jnp.maximum(i * hb - 1, 0), 0))
    unext = pl.BlockSpec((HALO, F2), lambda i: (jnp.minimum((i + 1) * hb, nh - 1), 0))
    darow = pl.BlockSpec((tm, F), lambda i: (i, 0))
    danext = pl.BlockSpec((HALO, F), lambda i: (jnp.minimum((i + 1) * hb, nh - 1), 0))

    def body(u_ref, up_ref, un_ref, da_ref, dan_ref, cw_ref, cb_ref, *rest):
        du_ref, dcw_ref, dcb_ref = rest[ng:ng + 3]
        swap = _Swap(rest[:ng], rest[ng + 3:2 * ng + 3], *rest[2 * ng + 3:]) if ng else None
        i = pl.program_id(0)
        cwv = cw_ref[...]
        prev = jnp.where(i > 0, up_ref[...], 0.0)
        gate_ext = jnp.concatenate([prev, u_ref[:, :F], un_ref[:, :F]], axis=0)
        val_ext = jnp.concatenate([u_ref[:, F:], un_ref[:, F:]], axis=0)
        da_next = jnp.where(i < nt - 1, dan_ref[...].astype(F32), 0.0)
        da_ext = jnp.concatenate([da_ref[...].astype(F32), da_next], axis=0)
        gc = _conv_gate(gate_ext, cwv, cb_ref[...])[HALO:]
        sg = _sigmoid(gc)
        silu = gc * sg
        dval = da_ext * silu
        dgc = da_ext * val_ext * (sg * (1.0 + gc * (1.0 - sg)))
        dgate = cwv[2:3, :] * dgc + cwv[1:2, :] * _shift_rows(dgc, -1) + cwv[0:1, :] * _shift_rows(dgc, -2)
        du_ref[:, :F] = dgate[:tm].astype(du_ref.dtype)
        du_ref[:, F:] = dval[:tm].astype(du_ref.dtype)
        dgc_c = dgc[:tm]
        g0 = gate_ext[HALO:HALO + tm]
        g1 = _shift_rows(gate_ext, 1)[HALO:HALO + tm]
        g2 = _shift_rows(gate_ext, 2)[HALO:HALO + tm]
        dcw = jnp.concatenate([jnp.sum(dgc_c * g2, axis=0, keepdims=True),
                               jnp.sum(dgc_c * g1, axis=0, keepdims=True),
                               jnp.sum(dgc_c * g0, axis=0, keepdims=True)], axis=0)
        dcb = jnp.sum(dgc_c, axis=0, keepdims=True)

        @pl.when(i == 0)
        def _():
            dcw_ref[...] = dcw
            dcb_ref[...] = dcb
            if ng:
                swap.issue()

        @pl.when(i > 0)
        def _():
            dcw_ref[...] += dcw
            dcb_ref[...] += dcb

        if ng:
            @pl.when(i == nt - 1)
            def _():
                swap.finish()

    return pl.pallas_call(
        body, name=name, grid=(nt,),
        in_specs=[urow, uprev, unext, darow, danext,
                  pl.BlockSpec((3, F), lambda i: (0, 0)), pl.BlockSpec((1, F), lambda i: (0, 0))] + [any_spec] * ng,
        out_specs=[urow, pl.BlockSpec((3, F), lambda i: (0, 0)), pl.BlockSpec((1, F), lambda i: (0, 0))] + [any_spec] * ng,
        out_shape=[jax.ShapeDtypeStruct((S, F2), MXU_DTYPE), jax.ShapeDtypeStruct((3, F), F32),
                   jax.ShapeDtypeStruct((1, F), F32)] + (_Swap.out_shape(gs) if ng else []),
        scratch_shapes=_Swap.semaphores(ng) if ng else [],
        compiler_params=_cparams(("arbitrary",)),
    )(u, u, u, da, da, cw, cb, *gs)


def _adamw_math(w, g, m, v):
    m = ADAM_B1 * m + (1.0 - ADAM_B1) * g
    v = ADAM_B2 * v + (1.0 - ADAM_B2) * (g * g)
    m_hat = m / (1.0 - ADAM_B1 ** ADAM_STEP)
    v_hat = v / (1.0 - ADAM_B2 ** ADAM_STEP)
    delta = -ADAM_LR * (m_hat / (jnp.sqrt(v_hat) + ADAM_EPS) + ADAM_WD * w)
    return delta, m, v


def _adamw_layers(w, gs, m, v, *, name):
    Lr, R, C = w.shape
    tr = _pick(R, (256, 128, 64, 32, 16, 8))
    outs = None
    for layer, g in enumerate(gs):
        def body(w_ref, g_ref, m_ref, v_ref, *rest):
            go_ref, d_ref, nm_ref, nv_ref = rest[-4:]
            gv = g_ref[...]
            d, nm, nv = _adamw_math(w_ref[...], gv, m_ref[...], v_ref[...])
            go_ref[...] = gv
            d_ref[...] = d
            nm_ref[...] = nm
            nv_ref[...] = nv

        lay = pl.BlockSpec((None, tr, C), lambda i, layer=layer: (layer, i, 0))
        in_specs = [lay, pl.BlockSpec((tr, C), lambda i: (i, 0)), lay, lay]
        args = [w, g, m, v]
        aliases = {}
        if outs is not None:
            in_specs += [pl.BlockSpec(memory_space=pl.ANY)] * 4
            args += list(outs)
            aliases = {4 + t: t for t in range(4)}
        outs = pl.pallas_call(
            body, name=f"{name}_{layer}", grid=(R // tr,), in_specs=in_specs, out_specs=[lay] * 4,
            out_shape=[jax.ShapeDtypeStruct((Lr, R, C), F32)] * 4, input_output_aliases=aliases,
            compiler_params=_cparams(("parallel",)),
        )(*args)
    return outs


def _adamw_small(items, *, name):
    n = len(items)

    def body(*refs):
        ins, outs = refs[:4 * n], refs[4 * n:]
        for t in range(n):
            w, g, m, v = (r[...] for r in ins[4 * t:4 * t + 4])
            d, nm, nv = _adamw_math(w, g, m, v)
            outs[3 * t][...] = d
            outs[3 * t + 1][...] = nm
            outs[3 * t + 2][...] = nv

    flat = [a for it in items for a in it]
    out_shape = [jax.ShapeDtypeStruct(it[0].shape, F32) for it in items for _ in range(3)]
    vm = pl.BlockSpec(memory_space=pltpu.VMEM)
    res = pl.pallas_call(body, name=name, in_specs=[vm] * len(flat), out_specs=[vm] * len(out_shape),
                         out_shape=out_shape)(*flat)
    return [tuple(res[3 * t:3 * t + 3]) for t in range(n)]


def _place():
    return lax.axis_index("x"), lax.axis_index("y"), lax.axis_index("c")


_FLIPS = ((1, 0), (0, 1), (1, 1))


ROW_ALIGN = 16


def _half(rows, which):
    return pl.ds(pl.multiple_of(which * (rows // 2), ROW_ALIGN), rows // 2)


def _remote(src, dst, send_sems, recv_sems, k, to):
    return pltpu.make_async_remote_copy(src_ref=src, dst_ref=dst, send_sem=send_sems.at[k], recv_sem=recv_sems.at[k],
                                        device_id=to, device_id_type=MESH)


def _gather_shards(shards, *, name):
    n = len(shards)
    any_spec = pl.BlockSpec(memory_space=pl.ANY)

    def body(*refs):
        gather = _Gather(refs[:n], refs[n:2 * n], refs[2 * n], refs[2 * n + 1])
        gather.issue()
        gather.finish()

    return pl.pallas_call(
        body, name=name, in_specs=[any_spec] * n, out_specs=[any_spec] * n,
        out_shape=_Gather.out_shape(shards), scratch_shapes=_Gather.semaphores(n),
    )(*shards)


class _Gather:
    def __init__(self, w_refs, out_refs, send_sems, recv_sems):
        self.w_refs, self.out_refs, self.send_sems, self.recv_sems = w_refs, out_refs, send_sems, recv_sems
        self.pairs = [(i, j) for i in range(len(w_refs)) for j in range(3)]

    @staticmethod
    def out_shape(shards):
        return [jax.ShapeDtypeStruct((N_CHIPS,) + s.shape, s.dtype) for s in shards]

    @staticmethod
    def semaphores(n):
        return [pltpu.SemaphoreType.DMA((6 * n,)), pltpu.SemaphoreType.DMA((6 * n,))]

    def _where(self):
        x, y, c = _place()
        return x, y, c, [(x ^ fx, y ^ fy) for fx, fy in _FLIPS]

    def _over_ici(self, i, j, landed):
        x, y, c, chips = self._where()
        px, py = chips[j]
        mine = _half(self.w_refs[i].shape[0], c)
        if landed:
            src = dst = self.out_refs[i].at[2 * px + py, mine]
        else:
            src, dst = self.w_refs[i].at[mine], self.out_refs[i].at[2 * x + y, mine]
        return _remote(src, dst, self.send_sems, self.recv_sems, 6 * i + j, (px, py, c))

    def _over_d2d(self, i, j, which):
        x, y, c, chips = self._where()
        px, py = chips[j]
        blk = self.out_refs[i].at[2 * px + py, _half(self.w_refs[i].shape[0], which)]
        return _remote(blk, blk, self.send_sems, self.recv_sems, 6 * i + 3 + j, (x, y, 1 - c))

    def issue(self):
        for i, j in self.pairs:
            self._over_ici(i, j, False).start()

    def finish(self):
        c = lax.axis_index("c")
        for i, j in self.pairs:
            self._over_ici(i, j, True).wait_recv()
            self._over_d2d(i, j, c).start()
        for i, j in self.pairs:
            self._over_d2d(i, j, 1 - c).wait_recv()
        for i, j in self.pairs:
            self._over_ici(i, j, False).wait_send()
            self._over_d2d(i, j, c).wait_send()


def _swap_halves(gs, *, name):
    n = len(gs)
    any_spec = pl.BlockSpec(memory_space=pl.ANY)

    def body(*refs):
        swap = _Swap(refs[:n], refs[n:2 * n], refs[2 * n], refs[2 * n + 1])
        swap.issue()
        swap.finish()

    return pl.pallas_call(
        body, name=name, in_specs=[any_spec] * n, out_specs=[any_spec] * n,
        out_shape=_Swap.out_shape(gs), scratch_shapes=_Swap.semaphores(n),
    )(*gs)


class _Swap:
    def __init__(self, g_refs, out_refs, send_sems, recv_sems):
        self.g_refs, self.out_refs, self.send_sems, self.recv_sems = g_refs, out_refs, send_sems, recv_sems

    @staticmethod
    def out_shape(gs):
        return [jax.ShapeDtypeStruct((N_CHIPS, g.shape[1] // 2, g.shape[2]), g.dtype) for g in gs]

    @staticmethod
    def semaphores(n):
        return [pltpu.SemaphoreType.DMA((n,)), pltpu.SemaphoreType.DMA((n,))]

    def _copies(self):
        x, y, c = _place()
        return [_remote(g.at[:, _half(g.shape[1], 1 - c)], out, self.send_sems, self.recv_sems, i, (x, y, 1 - c))
                for i, (g, out) in enumerate(zip(self.g_refs, self.out_refs))]

    def issue(self):
        for cp in self._copies():
            cp.start()

    def finish(self):
        for cp in self._copies():
            cp.wait()


def _add_halves(g, got, c_arr, *, name):
    _, rows, cols = g.shape
    blk = (None, rows // 2, cols)

    def body(c_ref, g_ref, got_ref, o_ref):
        o_ref[...] = (g_ref[...] + got_ref[...]).astype(o_ref.dtype)

    return pl.pallas_call(
        body, name=name,
        grid_spec=pltpu.PrefetchScalarGridSpec(
            num_scalar_prefetch=1, grid=(N_CHIPS,),
            in_specs=[pl.BlockSpec(blk, lambda j, c_ref: (j, c_ref[0], 0)), pl.BlockSpec(blk, lambda j, c_ref: (j, 0, 0))],
            out_specs=pl.BlockSpec(blk, lambda j, c_ref: (j, 0, 0))),
        out_shape=jax.ShapeDtypeStruct((N_CHIPS, rows // 2, cols), WIRE_DTYPE),
        compiler_params=_cparams(("parallel",)),
    )(c_arr, g, got)


def _exchange_chips(qs, *, name):
    n = len(qs)
    any_spec = pl.BlockSpec(memory_space=pl.ANY)

    def body(*refs):
        exchange = _Exchange(refs[:n], refs[n:2 * n], refs[2 * n], refs[2 * n + 1])
        exchange.issue()
        exchange.finish()

    return pl.pallas_call(
        body, name=name, in_specs=[any_spec] * n, out_specs=[any_spec] * n,
        out_shape=_Exchange.out_shape(qs), scratch_shapes=_Exchange.semaphores(n),
    )(*qs)


class _Exchange:
    def __init__(self, q_refs, out_refs, send_sems, recv_sems):
        self.q_refs, self.out_refs, self.send_sems, self.recv_sems = q_refs, out_refs, send_sems, recv_sems
        self.pairs = [(i, j) for i in range(len(q_refs)) for j in range(3)]

    @staticmethod
    def out_shape(qs):
        return [jax.ShapeDtypeStruct(q.shape, q.dtype) for q in qs]

    @staticmethod
    def semaphores(n):
        return [pltpu.SemaphoreType.DMA((3 * n,)), pltpu.SemaphoreType.DMA((3 * n,))]

    def _copy(self, i, j, landed):
        x, y, c = _place()
        px, py = [(x ^ fx, y ^ fy) for fx, fy in _FLIPS][j]
        if landed:
            src = dst = self.out_refs[i].at[2 * px + py]
        else:
            src, dst = self.q_refs[i].at[2 * px + py], self.out_refs[i].at[2 * x + y]
        return _remote(src, dst, self.send_sems, self.recv_sems, 3 * i + j, (px, py, c))

    def issue(self):
        for i, j in self.pairs:
            self._copy(i, j, False).start()

    def finish(self):
        for i, j in self.pairs:
            self._copy(i, j, True).wait_recv()
        for i, j in self.pairs:
            self._copy(i, j, False).wait_send()


def _sum_chips(q, r, place_arr, *, name):
    _, h, cols = q.shape
    blk = (None, h, cols)

    def body(p_ref, q_ref, r1_ref, r2_ref, r3_ref, o_ref):
        o_ref[...] = ((q_ref[...].astype(F32) + r1_ref[...].astype(F32)) + r2_ref[...].astype(F32)) + r3_ref[...].astype(F32)

    other = [pl.BlockSpec(blk, lambda i, p_ref, f=f: (p_ref[0] ^ f, 0, 0)) for f in (1, 2, 3)]
    return pl.pallas_call(
        body, name=name,
        grid_spec=pltpu.PrefetchScalarGridSpec(
            num_scalar_prefetch=1, grid=(1,),
            in_specs=[pl.BlockSpec(blk, lambda i, p_ref: (p_ref[0], 0, 0))] + other,
            out_specs=pl.BlockSpec((h, cols), lambda i, p_ref: (p_ref[1], 0))),
        out_shape=jax.ShapeDtypeStruct((2 * h, cols), F32),
        compiler_params=_cparams(("arbitrary",)),
    )(place_arr, q, r, r, r)


def _join_halves(fs, *, name):
    n = len(fs)
    any_spec = pl.BlockSpec(memory_space=pl.ANY)

    def body(*refs):
        out_refs, send_sems, recv_sems = refs[n:2 * n], refs[2 * n], refs[2 * n + 1]
        x, y, c = _place()
        sent = []
        for i in range(n):
            mine = out_refs[i].at[_half(fs[i].shape[0], c)]
            cp = _remote(mine, mine, send_sems, recv_sems, i, (x, y, 1 - c))
            cp.start()
            sent.append(cp)
        for i in range(n):
            its = out_refs[i].at[_half(fs[i].shape[0], 1 - c)]
            _remote(its, its, send_sems, recv_sems, i, (x, y, 1 - c)).wait_recv()
        for cp in sent:
            cp.wait_send()

    return pl.pallas_call(
        body, name=name, in_specs=[any_spec] * n, out_specs=[any_spec] * n,
        out_shape=[jax.ShapeDtypeStruct(f.shape, f.dtype) for f in fs], input_output_aliases={i: i for i in range(n)},
        scratch_shapes=[pltpu.SemaphoreType.DMA((n,)), pltpu.SemaphoreType.DMA((n,))],
    )(*fs)


def _allreduce_small(buf, *, name):
    R = buf.shape[0]
    vm = pl.BlockSpec(memory_space=pltpu.VMEM)

    def body(b_ref, o_ref, slots, send_sems, recv_sems):
        x, y, c = _place()
        me = 4 * x + 2 * y + c
        slots[me] = b_ref[...]
        sends = []
        for kk in range(1, 8):
            fx, fy, fc = (kk >> 2) & 1, (kk >> 1) & 1, kk & 1
            cp = pltpu.make_async_remote_copy(
                src_ref=b_ref, dst_ref=slots.at[me], send_sem=send_sems.at[kk - 1], recv_sem=recv_sems.at[kk - 1],
                device_id=(x ^ fx, y ^ fy, c ^ fc), device_id_type=MESH)
            cp.start()
            sends.append(cp)
        for kk in range(1, 8):
            fx, fy, fc = (kk >> 2) & 1, (kk >> 1) & 1, kk & 1
            peer = 4 * (x ^ fx) + 2 * (y ^ fy) + (c ^ fc)
            pltpu.make_async_remote_copy(
                src_ref=b_ref, dst_ref=slots.at[peer], send_sem=send_sems.at[kk - 1], recv_sem=recv_sems.at[kk - 1],
                device_id=(x ^ fx, y ^ fy, c ^ fc), device_id_type=MESH).wait_recv()
        for cp in sends:
            cp.wait_send()
        acc = slots[0]
        for d in range(1, 8):
            acc = acc + slots[d]
        o_ref[...] = acc

    return pl.pallas_call(
        body, name=name, in_specs=[vm], out_specs=vm,
        out_shape=jax.ShapeDtypeStruct((R, LANES), F32),
        scratch_shapes=[pltpu.VMEM((8, R, LANES), F32), pltpu.SemaphoreType.DMA((7,)), pltpu.SemaphoreType.DMA((7,))],
    )(buf)


def _pad_rows(v, mult=8 * LANES):
    flat = v.reshape(-1)
    n = flat.shape[0]
    tot = -(-n // mult) * mult
    return jnp.pad(flat, (0, tot - n)).reshape(-1, LANES)


def _pack_small(parts):
    return jnp.concatenate([_pad_rows(p.astype(F32)) for p in parts], axis=0)


def _unpack_small(buf, shapes):
    out, r = [], 0
    for sh in shapes:
        n = math.prod(sh)
        rows = -(-n // (8 * LANES)) * 8
        out.append(buf[r:r + rows].reshape(-1)[:n].reshape(sh))
        r += rows
    return out


def _cols_to_shards(w):
    *lead, K, N = w.shape
    t = w.reshape(*lead, K, N_CHIPS, N // N_CHIPS)
    return jnp.moveaxis(t, -2, 0)


def _shards_to_cols(t):
    t = jnp.moveaxis(t, 0, -2)
    *lead, K, _, n = t.shape
    return t.reshape(*lead, K, N_CHIPS * n)


_BIG = ("m_w_in", "m_w_out", "a_w_in", "a_w_out", "f_w_up", "f_w_down")


def kernel(x, m_w_in, m_gate_bias, m_head_norm, m_w_out, a_w_in, a_b_in, a_sinks, a_w_out, a_b_out, norm_mix_pre, norm_mix_post, norm_ffn_pre, norm_ffn_post, f_w_up, f_conv_w, f_conv_b, f_w_down, loss_target, m_m_w_in, m_m_gate_bias, m_m_head_norm, m_m_w_out, m_a_w_in, m_a_b_in, m_a_sinks, m_a_w_out, m_a_b_out, m_norm_mix_pre, m_norm_mix_post, m_norm_ffn_pre, m_norm_ffn_post, m_f_w_up, m_f_conv_w, m_f_conv_b, m_f_w_down, v_m_w_in, v_m_gate_bias, v_m_head_norm, v_m_w_out, v_a_w_in, v_a_b_in, v_a_sinks, v_a_w_out, v_a_b_out, v_norm_mix_pre, v_norm_mix_post, v_norm_ffn_pre, v_norm_ffn_post, v_f_w_up, v_f_conv_w, v_f_conv_b, v_f_w_down):
    params = dict(m_w_in=m_w_in, m_gate_bias=m_gate_bias, m_head_norm=m_head_norm, m_w_out=m_w_out, a_w_in=a_w_in,
                  a_b_in=a_b_in, a_sinks=a_sinks, a_w_out=a_w_out, a_b_out=a_b_out, norm_mix_pre=norm_mix_pre,
                  norm_mix_post=norm_mix_post, norm_ffn_pre=norm_ffn_pre, norm_ffn_post=norm_ffn_post, f_w_up=f_w_up,
                  f_conv_w=f_conv_w, f_conv_b=f_conv_b, f_w_down=f_w_down)
    mom1 = dict(m_w_in=m_m_w_in, m_gate_bias=m_m_gate_bias, m_head_norm=m_m_head_norm, m_w_out=m_m_w_out,
                a_w_in=m_a_w_in, a_b_in=m_a_b_in, a_sinks=m_a_sinks, a_w_out=m_a_w_out, a_b_out=m_a_b_out,
                norm_mix_pre=m_norm_mix_pre, norm_mix_post=m_norm_mix_post, norm_ffn_pre=m_norm_ffn_pre,
                norm_ffn_post=m_norm_ffn_post, f_w_up=m_f_w_up, f_conv_w=m_f_conv_w, f_conv_b=m_f_conv_b,
                f_w_down=m_f_w_down)
    mom2 = dict(m_w_in=v_m_w_in, m_gate_bias=v_m_gate_bias, m_head_norm=v_m_head_norm, m_w_out=v_m_w_out,
                a_w_in=v_a_w_in, a_b_in=v_a_b_in, a_sinks=v_a_sinks, a_w_out=v_a_w_out, a_b_out=v_a_b_out,
                norm_mix_pre=v_norm_mix_pre, norm_mix_post=v_norm_mix_post, norm_ffn_pre=v_norm_ffn_pre,
                norm_ffn_post=v_norm_ffn_post, f_w_up=v_f_w_up, f_conv_w=v_f_conv_w, f_conv_b=v_f_conv_b,
                f_w_down=v_f_w_down)
    order = list(params)

    mx, my, mc = _place()
    chip = 2 * mx + my
    h0 = x[0]
    target = loss_target[0]

    def two_d(t):
        return t.reshape(-1, t.shape[-1])

    early = ("m_w_in",)
    late = tuple(n for n in _BIG if n not in early)
    mine = {n: two_d(params[n]).astype(MXU_DTYPE) for n in _BIG}

    def with_own_slot(names, theirs):
        return {n: lax.dynamic_update_slice(t, mine[n][None], (chip, 0, 0)) for n, t in zip(names, theirs)}

    gathered = with_own_slot(early, _gather_shards([mine[n] for n in early], name="gather_weights_early"))

    def in_place(shard, axis):
        width = shard.shape[axis]
        z = jnp.zeros(shard.shape[:axis] + (N_CHIPS * width,) + shard.shape[axis + 1:], F32)
        contrib = jnp.where(mc == 0, shard, 0.0)
        return lax.dynamic_update_slice_in_dim(z, contrib, chip * width, axis)

    sm_in = [in_place(a_b_in, 1), in_place(a_b_out, 1), in_place(f_conv_w, 2)]
    sm_full = _unpack_small(_allreduce_small(_pack_small(sm_in), name="gather_small"), [t.shape for t in sm_in])
    b_in_full, b_out_full, conv_w_full = sm_full

    W_in = _shards_to_cols(gathered["m_w_in"])
    W_all = jnp.pad(W_in, ((0, 0), (0, M_PROJ - W_in.shape[1])))
    gbias = jnp.pad(m_gate_bias[0].reshape(1, 2 * M_HEADS), ((0, 0), (0, LANES - 2 * M_HEADS)))

    grads = {}

    def ffn_fwd(i, z1, end):
        u = _mm(z1, W_up, b_layer=i, name=f"ffn_up{i}")
        a = _convgate_fwd(u, conv_w_full[i], f_conv_b[i:i + 1], name=f"ffn_act{i}")
        return u, a, _mm(a, W_down[i], rows=end, name=f"ffn_down{i}")

    z0 = _rms_fwd(h0, norm_mix_pre[0:1], out_dtype=MXU_DTYPE, name="mix_pre_norm0")
    proj = _mm(z0, W_all, name="mlstm_proj")
    gated, C_all, n_all, m_all, *late_theirs = _mlstm_fwd(proj, gbias, m_head_norm, [mine[n] for n in late],
                                                          name="mlstm_fwd")
    gathered.update(with_own_slot(late, late_theirs))
    W_mout = gathered["m_w_out"].reshape(D_MODEL, D_MODEL)
    A_in, A_out = _shards_to_cols(gathered["a_w_in"]), gathered["a_w_out"].reshape(D_MODEL, D_MODEL)
    W_up = gathered["f_w_up"]
    dsh = D_FF // N_CHIPS
    W_down = [gathered["f_w_down"][:, i * dsh:(i + 1) * dsh].reshape(D_FF, D_MODEL) for i in range(2)]
    zm0, h1, z1 = _mm(gated, W_mout, rows=_rows_sublayer_end(h0, norm_mix_post[0:1], norm_ffn_pre[0:1]), name="mlstm_out")
    u0, a0, (zf0, h2, y0) = ffn_fwd(0, z1, _rows_sublayer_end(h1, norm_ffn_post[0:1], norm_mix_pre[1:2]))

    aproj = _mm(y0, A_in, bias=b_in_full, name="attn_proj")
    sinks = a_sinks.reshape(A_KVH, A_G)
    ao = _attn_fwd(aproj, sinks, name="attn_fwd")
    zm1, h3, z3 = _mm(ao, A_out, bias=b_out_full, rows=_rows_sublayer_end(h2, norm_mix_post[1:2], norm_ffn_pre[1:2]),
                      name="attn_out")
    u1, a1, (zf1, dh, dzf1, g_post1, loss_lanes) = ffn_fwd(1, z3, _rows_last_sublayer_end(h3, norm_ffn_post[1:2], target))
    loss = lax.psum(loss_lanes[0, 0], ("x", "y", "c"))

    g_post, g_fpre, g_mpost, g_mpre = [None, g_post1], [None, None], [None, None], [None, None]
    dW_up, dW_down, dconv_w, dconv_b = [None, None], [None, None], [None, None], [None, None]

    def by_rows(g):
        return g.reshape(N_CHIPS, -1, g.shape[-1])

    def ffn_bwd(i, dzf, z1_, u, a, start, ride):
        da = _mm(dzf, W_down[i], tb=True, name=f"ffn_down_dx{i}")
        dW_down[i] = _mm(a, dzf, ta=True, name=f"ffn_down_dw{i}")
        riders = (list(ride) + [by_rows(dW_down[i])]) if ride is not None else []
        du, dconv_w[i], dconv_b[i], *got = _convgate_bwd(u, da, conv_w_full[i], f_conv_b[i:i + 1], riders,
                                                         name=f"ffn_act_bwd{i}")
        dW_up[i] = _mm(z1_, du, ta=True, out_stacked=True, name=f"ffn_up_dw{i}")
        return _mm(du, W_up, tb=True, b_layer=i, rows=start, name=f"ffn_up_dx{i}"), riders, got

    (dh3, g_fpre[1], dzm1, g_mpost[1], db_out), _, _ = ffn_bwd(
        1, dzf1, z3, u1, a1, _rows_sublayer_start_bwd(h3, dh, zm1, norm_ffn_pre[1:2], norm_mix_post[1:2]), None)
    dao = _mm(dzm1, A_out, tb=True, name="attn_out_dx")
    dA_out = _mm(ao, dzm1, ta=True, name="attn_out_dw")
    daq, dakv, dsinks, cs_q, cs_kv = _attn_bwd(aproj, sinks, dao, name="attn_bwd")
    db_in = jnp.concatenate([cs_q, cs_kv], axis=1)
    dh2, g_mpre[1], dzf0, g_post[0], _ = _mm(
        daq, A_in[:, :A_NQ], tb=True, add=_mm(dakv, A_in[:, A_NQ:], tb=True, name="attn_proj_kv_dx"),
        rows=_rows_sublayer_start_bwd(h2, dh3, zf0, norm_mix_pre[1:2], norm_ffn_post[0:1]), name="attn_proj_q_dx")
    dA_in = jnp.concatenate([_mm(y0, daq, ta=True, name="attn_proj_q_dw"),
                             _mm(y0, dakv, ta=True, name="attn_proj_kv_dw")], axis=1)

    c_arr = jnp.reshape(mc, (1,)).astype(jnp.int32)
    place_arr = jnp.stack([chip, mc]).astype(jnp.int32)

    def add_halves(tags, gs, got):
        return [_add_halves(g, t, c_arr, name=f"grad_add_halves_{tag}") for tag, g, t in zip(tags, gs, got)]

    (dh1, g_fpre[0], dzm0, g_mpost[0], _), rode, rode_got = ffn_bwd(
        0, dzf0, z1, u0, a0, _rows_sublayer_start_bwd(h1, dh2, zm0, norm_ffn_pre[0:1], norm_mix_post[0:1]),
        [_cols_to_shards(dA_in), by_rows(dA_out), dW_up[1], by_rows(dW_down[1])])
    rode_tags = ["a_w_in", "a_w_out", "f_w_up1", "f_w_down1", "f_w_down0"]
    dgated = _mm(dzm0, W_mout, tb=True, name="mlstm_out_dx")
    dW_mout = _mm(gated, dzm0, ta=True, name="mlstm_out_dw")
    next_tags, next_gs = ["f_w_up0", "m_w_out"], [dW_up[0], by_rows(dW_mout)]
    late_tags = rode_tags + next_tags
    late_part = (add_halves(rode_tags, rode, rode_got)
                 + add_halves(next_tags, next_gs, _swap_halves(next_gs, name="grad_swap_halves_late")))
    dproj, dgbias, dhn, *late_from = _mlstm_bwd(proj, gbias, m_head_norm, C_all, n_all, m_all, dgated, late_part,
                                                name="mlstm_bwd")
    grad_x, g_mpre[0] = _mm(dproj, W_all, tb=True, rows=_rows_first_sublayer_start_bwd(h0, dh1, norm_mix_pre[0:1]),
                            name="mlstm_proj_dx")
    dW_all = _mm(z0, dproj, ta=True, name="mlstm_proj_dw")
    dW_min = dW_all[:, :m_w_in.shape[-1] * N_CHIPS]

    early_tags, early_gs = ["m_w_in"], [_cols_to_shards(dW_min)]
    early_part = add_halves(early_tags, early_gs, _swap_halves(early_gs, name="grad_swap_halves_early"))
    early_from = _exchange_chips(early_part, name="grad_exchange_chips_early")
    tags = early_tags + late_tags
    halves = [_sum_chips(q, r, place_arr, name=f"grad_sum_chips_{tag}")
              for tag, q, r in zip(tags, early_part + late_part, list(early_from) + list(late_from))]
    reduced = dict(zip(tags, _join_halves(halves, name="grad_join_halves")))
    layer_grads = dict(m_w_in=[reduced["m_w_in"]], m_w_out=[reduced["m_w_out"]], a_w_in=[reduced["a_w_in"]],
                       a_w_out=[reduced["a_w_out"]], f_w_up=[reduced["f_w_up0"], reduced["f_w_up1"]],
                       f_w_down=[reduced["f_w_down0"], reduced["f_w_down1"]])

    small_g = [
        dgbias[:, :2 * M_HEADS].reshape(1, 2, M_HEADS),
        dhn,
        dsinks.reshape(1, A_QH),
        db_in, db_out,
        jnp.concatenate(g_mpre), jnp.concatenate(g_mpost), jnp.concatenate(g_fpre), jnp.concatenate(g_post),
        jnp.stack(dconv_w), jnp.concatenate(dconv_b),
    ]
    small_names = ["m_gate_bias", "m_head_norm", "a_sinks", "a_b_in", "a_b_out", "norm_mix_pre", "norm_mix_post",
                   "norm_ffn_pre", "norm_ffn_post", "f_conv_w", "f_conv_b"]
    red = _unpack_small(_allreduce_small(_pack_small(small_g), name="reduce_small"), [t.shape for t in small_g])
    for n, t in zip(small_names, red):
        if n in ("a_b_in", "a_b_out", "f_conv_w"):
            axis = t.ndim - 1
            width = params[n].shape[axis]
            t = lax.dynamic_slice_in_dim(t, chip * width, width, axis)
        grads[n] = t

    deltas, new_m, new_v = {}, {}, {}
    for n in _BIG:
        grads[n], deltas[n], new_m[n], new_v[n] = _adamw_layers(params[n], layer_grads[n], mom1[n], mom2[n],
                                                                name=f"adamw_{n}")
    two = lambda t: t.reshape(-1, t.shape[-1])
    res = _adamw_small([(two(params[n]), two(grads[n]), two(mom1[n]), two(mom2[n])) for n in small_names],
                       name="adamw_small")
    for n, (d, nm, nv) in zip(small_names, res):
        sh = params[n].shape
        deltas[n], new_m[n], new_v[n] = d.reshape(sh), nm.reshape(sh), nv.reshape(sh)

    return (loss, grad_x[None], *[grads[n] for n in order], *[deltas[n] for n in order],
            *[new_m[n] for n in order], *[new_v[n] for n in order])
```

```python
import functools
import math

import numpy as np

import jax
import jax.numpy as jnp
from jax import lax
from jax.experimental import pallas as pl
from jax.experimental.pallas import tpu as pltpu

F32 = jnp.float32
MXU_DTYPE = jnp.bfloat16
WIRE_DTYPE = jnp.bfloat16
MESH = pl.DeviceIdType.MESH

D_MODEL = 1024
EPS = 1e-6
M_HEADS, M_QK, M_V, M_CHUNK = 8, 64, 128, 128
GATE_CAP = 15.0
A_DH, A_QH, A_KVH, A_G, A_BLK = 64, 16, 2, 8, 128
D_FF = 2816
N_CHIPS = 4
LANES = 128
VMEM_LIMIT = 56 * 1024 * 1024

ADAM_LR, ADAM_B1, ADAM_B2, ADAM_EPS, ADAM_WD, ADAM_STEP = 0.001, 0.9, 0.999, 1e-08, 0.01, 10


def _cparams(sem):
    return pltpu.CompilerParams(dimension_semantics=sem, vmem_limit_bytes=VMEM_LIMIT)


def _pick(n, cands):
    for c in cands:
        if n % c == 0:
            return c
    return n


class _Rows:
    def __init__(self, tiles, vecs, outs, fn):
        self.tiles, self.vecs, self.outs, self.fn = list(tiles), list(vecs), list(outs), fn


ROWS_TILE = 512
ROWS_FULL_K = 4224
WIDE_N = 3200


def _mm(a, b, *, ta=False, tb=False, out_dtype=F32, bias=None, add=None, b_layer=None, out_stacked=False, rows=None,
        name):
    if ta:
        K, M = a.shape
    else:
        M, K = a.shape
    b_stacked = b_layer is not None
    if b_stacked:
        assert not ta
        n_sh = b.shape[2]
        w_rows, w_cols = D_MODEL, N_CHIPS * n_sh
        N, Kb = (w_rows, w_cols) if tb else (w_cols, w_rows)
    elif tb:
        N, Kb = b.shape
    else:
        Kb, N = b.shape
    assert K == Kb, (a.shape, b.shape)
    tm = _pick(M, (1024, 1408, 512, 256, 128))
    tn = _pick(N, (1024, 1408, 1280, 640, 512, 256, 128))
    tk = K if K <= 2816 else _pick(K, (2816, 2048, 1408, 1024, 512, 256, 128))
    if tn < 1024 <= N <= WIDE_N:
        tn, tm, tk = N, min(tm, 512), min(tk, 1024)
    shards_per_step = 1
    if b_stacked and tb:
        shards_per_step = 2
        tk = shards_per_step * n_sh
    if (b_stacked and not tb) or out_stacked:
        tn = N // N_CHIPS
    if rows is not None:
        assert tn == N and not out_stacked and not ta
        tm = min(tm, ROWS_TILE)
        if K <= ROWS_FULL_K:
            tk = K
    nk = K // tk
    dn = (((0 if ta else 1,), (1 if tb else 0,)), ((), ()))
    has_bias, has_add = bias is not None, add is not None
    n_tiles, n_vecs, n_outs = (len(rows.tiles), len(rows.vecs), len(rows.outs)) if rows is not None else (0, 0, 1)

    def body(*refs):
        a_ref, b_ref = refs[0], refs[1]
        pos = 2
        bias_ref = add_ref = None
        if has_bias:
            bias_ref = refs[pos]
            pos += 1
        if has_add:
            add_ref = refs[pos]
            pos += 1
        tile_refs, vec_refs = refs[pos:pos + n_tiles], refs[pos + n_tiles:pos + n_tiles + n_vecs]
        pos += n_tiles + n_vecs
        out_refs = refs[pos:pos + n_outs]
        acc_ref = refs[pos + n_outs] if nk > 1 else None

        def finish(r):
            if has_bias:
                r = r + bias_ref[...]
            if has_add:
                r = r + add_ref[...]
            if rows is None:
                out_refs[0][...] = r.astype(out_dtype)
                return
            vals = rows.fn(r, [t[...] for t in tile_refs], [v[...] for v in vec_refs])
            first = pl.program_id(0) == 0
            for (kind, _), o_ref, val in zip(rows.outs, out_refs, vals):
                if kind == "tile":
                    o_ref[...] = val.astype(o_ref.dtype)
                else:
                    @pl.when(first)
                    def _(o_ref=o_ref, val=val):
                        o_ref[...] = val

                    @pl.when(jnp.logical_not(first))
                    def _(o_ref=o_ref, val=val):
                        o_ref[...] += val

        if shards_per_step > 1:
            part = sum(lax.dot_general(a_ref[:, s * n_sh:(s + 1) * n_sh], b_ref[s], dn, preferred_element_type=F32)
                       for s in range(shards_per_step))
        else:
            part = lax.dot_general(a_ref[...], b_ref[...], dn, preferred_element_type=F32)
        if nk == 1:
            finish(part)
        else:
            k = pl.program_id(2)

            @pl.when(k == 0)
            def _():
                acc_ref[...] = part

            @pl.when(k > 0)
            def _():
                acc_ref[...] += part

            @pl.when(k == nk - 1)
            def _():
                finish(acc_ref[...])

    a_spec = pl.BlockSpec((tk, tm), lambda i, j, k: (k, i)) if ta else pl.BlockSpec((tm, tk), lambda i, j, k: (i, k))
    if b_stacked and tb:
        off = b_layer * (w_rows // tn)
        b_spec = pl.BlockSpec((shards_per_step, tn, n_sh), lambda i, j, k: (k, off + j, 0))
    elif b_stacked:
        off = b_layer * (w_rows // tk)
        b_spec = pl.BlockSpec((None, tk, tn), lambda i, j, k: (j, off + k, 0))
    elif tb:
        b_spec = pl.BlockSpec((tn, tk), lambda i, j, k: (j, k))
    else:
        b_spec = pl.BlockSpec((tk, tn), lambda i, j, k: (k, j))
    if out_stacked:
        out_spec = pl.BlockSpec((None, tm, tn), lambda i, j, k: (j, i, 0))
        out_shape = jax.ShapeDtypeStruct((N_CHIPS, M, tn), out_dtype)
    else:
        out_spec = pl.BlockSpec((tm, tn), lambda i, j, k: (i, j))
        out_shape = jax.ShapeDtypeStruct((M, N), out_dtype)
    in_specs, args = [a_spec, b_spec], [a, b]
    if has_bias:
        in_specs.append(pl.BlockSpec((1, tn), lambda i, j, k: (0, j)))
        args.append(bias)
    if has_add:
        in_specs.append(pl.BlockSpec((tm, tn), lambda i, j, k: (i, j)))
        args.append(add)
    semantics = ("parallel", "parallel", "arbitrary")
    if rows is not None:
        tile_spec = pl.BlockSpec((tm, N), lambda i, j, k: (i, 0))
        vec_spec = pl.BlockSpec((1, N), lambda i, j, k: (0, 0))
        in_specs += [tile_spec] * n_tiles + [vec_spec] * n_vecs
        args += rows.tiles + rows.vecs
        out_spec = [tile_spec if kind == "tile" else vec_spec for kind, _ in rows.outs]
        out_shape = [jax.ShapeDtypeStruct((M, N) if kind == "tile" else (1, N), dt) for kind, dt in rows.outs]
        semantics = ("arbitrary", "arbitrary", "arbitrary")
    return pl.pallas_call(
        body, name=name,
        grid=(M // tm, N // tn, nk),
        in_specs=in_specs,
        out_specs=out_spec,
        out_shape=out_shape,
        scratch_shapes=[pltpu.VMEM((tm, tn), F32)] if nk > 1 else [],
        compiler_params=_cparams(semantics),
    )(*args)


ROW_TILE = 512


def _rms_fwd(x, g, *, out_dtype, name):
    S, D = x.shape
    tm = _pick(S, (ROW_TILE, 256, 128))

    def body(x_ref, g_ref, o_ref):
        o_ref[...] = _rms(x_ref[...], g_ref[...]).astype(out_dtype)

    row = pl.BlockSpec((tm, D), lambda i: (i, 0))
    return pl.pallas_call(
        body, name=name, grid=(S // tm,), in_specs=[row, pl.BlockSpec((1, D), lambda i: (0, 0))], out_specs=row,
        out_shape=jax.ShapeDtypeStruct((S, D), out_dtype),
        compiler_params=_cparams(("parallel",)),
    )(x, g)


def _rms(x, g):
    return x * lax.rsqrt(jnp.mean(x * x, axis=-1, keepdims=True) + EPS) * g


def _rms_vjp(x, g, dy):
    r = lax.rsqrt(jnp.mean(x * x, axis=-1, keepdims=True) + EPS)
    xh = x * r
    gd = dy * g
    dx = r * (gd - xh * jnp.mean(gd * xh, axis=-1, keepdims=True))
    return dx, jnp.sum(dy * xh, axis=0, keepdims=True)


def _rows_sublayer_end(res, g_post, g_pre_next):
    def fn(z, tiles, vecs):
        h = tiles[0] + _rms(z, vecs[0])
        return [z, h, _rms(h, vecs[1])]

    return _Rows([res], [g_post, g_pre_next], [("tile", F32), ("tile", F32), ("tile", MXU_DTYPE)], fn)


def _rows_last_sublayer_end(res, g_post, target):
    def fn(z, tiles, vecs):
        y = tiles[0] + _rms(z, vecs[0])
        err = y - tiles[1]
        dy = err * (1.0 / err.shape[-1])
        dz, dg = _rms_vjp(z, vecs[0], dy)
        loss = 0.5 * jnp.sum(jnp.mean(err * err, axis=-1, keepdims=True), axis=0, keepdims=True)
        return [z, dy, dz, dg, jnp.broadcast_to(loss, dg.shape)]

    return _Rows([res, target], [g_post], [("tile", F32), ("tile", F32), ("tile", MXU_DTYPE), ("vec", F32), ("vec", F32)], fn)


def _rows_sublayer_start_bwd(x, res, z_below, g_pre, g_post_below):
    def fn(dz, tiles, vecs):
        dx, dg_pre = _rms_vjp(tiles[0], vecs[0], dz)
        dh = tiles[1] + dx
        dzb, dg_post = _rms_vjp(tiles[2], vecs[1], dh)
        return [dh, dg_pre, dzb, dg_post, jnp.sum(dzb, axis=0, keepdims=True)]

    return _Rows([x, res, z_below], [g_pre, g_post_below],
                 [("tile", F32), ("vec", F32), ("tile", MXU_DTYPE), ("vec", F32), ("vec", F32)], fn)


def _rows_first_sublayer_start_bwd(x, res, g_pre):
    def fn(dz, tiles, vecs):
        dx, dg_pre = _rms_vjp(tiles[0], vecs[0], dz)
        return [tiles[1] + dx, dg_pre]

    return _Rows([x, res], [g_pre], [("tile", F32), ("vec", F32)], fn)


def _mx(t):
    return t.astype(MXU_DTYPE)


def _mxf(t):
    return t.astype(MXU_DTYPE).astype(F32)


def _rawdot(a, b, ca, cb):
    return lax.dot_general(_mx(a), _mx(b), (((ca,), (cb,)), ((), ())), preferred_element_type=F32)


@functools.partial(jax.custom_vjp, nondiff_argnums=(2, 3))
def _dot(a, b, ca, cb):
    return _rawdot(a, b, ca, cb)


def _dot_fwd(a, b, ca, cb):
    return _rawdot(a, b, ca, cb), (a, b)


def _dot_bwd(ca, cb, res, g):
    a, b = res
    bj = 1 if cb == 0 else 0
    ai = 0 if ca == 1 else 1
    da = _rawdot(g, b, 1, bj) if ca == 1 else _rawdot(b, g, bj, 1)
    db = _rawdot(a, g, ai, 0) if cb == 0 else _rawdot(g, a, 0, ai)
    return da.astype(a.dtype), db.astype(b.dtype)


_dot.defvjp(_dot_fwd, _dot_bwd)


def _softcap(z):
    return GATE_CAP * jnp.tanh(z / GATE_CAP)


def _log_sigmoid(z):
    return jnp.minimum(z, 0.0) - jnp.log(1.0 + jnp.exp(-jnp.abs(z)))


def _sigmoid(z):
    return 0.5 * jnp.tanh(0.5 * z) + 0.5


def _lane_col(t, lane_index):
    lane = lax.broadcasted_iota(jnp.int32, t.shape, 1)
    return jnp.sum(jnp.where(lane == lane_index, t, 0.0), axis=1, keepdims=True)


def _mlstm_gates(G, bias):
    L = G.shape[0]
    z = _softcap(G + bias)
    ig = z
    lf = _log_sigmoid(z)
    ti = lax.broadcasted_iota(jnp.int32, (L, L), 0)
    si = lax.broadcasted_iota(jnp.int32, (L, L), 1)
    tril = (si <= ti).astype(F32)
    b = lax.dot_general(tril, lf, (((1,), (0,)), ((), ())), precision=lax.Precision.HIGHEST, preferred_element_type=F32)
    bL = jnp.sum(lf, axis=0, keepdims=True)
    return ig, b, ig.T, b.T, bL


def _mlstm_head(h, q, k_pair, v, o, ig_all, b_all, igT, bT, bL_all, hn, C, n, m):
    L = q.shape[0]
    k = jnp.where(_own_lanes(h, k_pair.shape), k_pair, 0.0)
    ti = lax.broadcasted_iota(jnp.int32, (L, L), 0)
    si = lax.broadcasted_iota(jnp.int32, (L, L), 1)
    lower = (si <= ti)
    ig = _lane_col(ig_all, h)
    b = _lane_col(b_all, M_HEADS + h)
    ig_row = igT[h:h + 1, :]
    b_row = bT[M_HEADS + h:M_HEADS + h + 1, :]
    bL = _lane_col(bL_all, M_HEADS + h)
    inter = b + m
    dlog = jnp.where(lower, b - b_row + ig_row, -jnp.inf)
    m_t = lax.stop_gradient(jnp.maximum(inter, jnp.max(dlog, axis=-1, keepdims=True)))
    qs = q * (M_QK ** -0.5)
    w = _dot(qs, k, 1, 1) * jnp.exp(dlog - m_t)
    s_inter = jnp.exp(inter - m_t)
    num = _dot(w, v, 1, 0) + s_inter * _dot(qs, C, 1, 0)
    den = jnp.sum(w, axis=-1, keepdims=True) + s_inter * jnp.sum(_mxf(qs) * _mxf(n), axis=-1, keepdims=True)
    hout = num * (1.0 / jnp.maximum(jnp.abs(den), jnp.exp(-m_t)))
    tail = bL - b + ig
    m_new = lax.stop_gradient(jnp.maximum(bL + m, jnp.max(tail, axis=0, keepdims=True)))
    ws = jnp.exp(tail - m_new)
    decay = jnp.exp(bL + m - m_new)
    wk = ws * k
    C_new = decay * C + _dot(wk, v, 0, 0)
    n_new = decay * n + jnp.sum(_mxf(ws) * _mxf(k), axis=0, keepdims=True)
    hs = hout * lax.rsqrt(jnp.mean(hout * hout, axis=-1, keepdims=True) + EPS) * hn
    gated = _sigmoid(o) * hs
    return (gated, C_new, n_new), m_new


M_OFF_Q, M_OFF_K = 0, M_HEADS * M_QK
M_OFF_V = 2 * M_HEADS * M_QK
M_OFF_O = M_OFF_V + M_HEADS * M_V
M_OFF_G = M_OFF_O + M_HEADS * M_V
M_PROJ = M_OFF_G + LANES
M_PAIRS = M_HEADS * M_QK // LANES


def _head_cols(off, h):
    return slice(off + h * LANES, off + (h + 1) * LANES)


def _own_lanes(h, shape):
    low = lax.broadcasted_iota(jnp.int32, shape, 1) < M_QK
    return low if h % 2 == 0 else jnp.logical_not(low)


def _mlstm_specs(NC, rev):
    H, L = M_HEADS, M_CHUNK
    cc = (lambda c: NC - 1 - c) if rev else (lambda c: c)
    proj = pl.BlockSpec((L, M_PROJ), lambda c: (cc(c), 0))
    vec = pl.BlockSpec((1, LANES), lambda c: (0, 0))
    hn = pl.BlockSpec((1, H * M_V), lambda c: (0, 0))
    hv = pl.BlockSpec((L, H * M_V), lambda c: (cc(c), 0))
    Cs = pl.BlockSpec((None, H, LANES, M_V), lambda c: (cc(c), 0, 0, 0))
    ns = pl.BlockSpec((None, H, 1, LANES), lambda c: (cc(c), 0, 0, 0))
    ms = pl.BlockSpec((None, H, 1, 1), lambda c: (cc(c), 0, 0, 0))
    return proj, vec, hn, hv, Cs, ns, ms


_MLSTM_STATE = [pltpu.VMEM((M_HEADS, LANES, M_V), F32), pltpu.VMEM((M_HEADS, 1, LANES), F32),
                pltpu.VMEM((M_HEADS, 1, 1), F32)]


def _mlstm_fwd(proj, bias, hn, shards, *, name):
    S = proj.shape[0]
    H, NC = M_HEADS, S // M_CHUNK
    ps, vec, hns, hv, Cs, ns, ms = _mlstm_specs(NC, False)
    nsh = len(shards)
    any_spec = pl.BlockSpec(memory_space=pl.ANY)

    def body(p_ref, b_ref, hn_ref, *rest):
        gated_ref, C_all, n_all, m_all = rest[nsh:nsh + 4]
        C_s, n_s, m_s, send_sems, recv_sems = rest[2 * nsh + 4:]
        gather = _Gather(rest[:nsh], rest[nsh + 4:2 * nsh + 4], send_sems, recv_sems)

        @pl.when(pl.program_id(0) == 0)
        def _():
            C_s[...] = jnp.zeros_like(C_s)
            n_s[...] = jnp.zeros_like(n_s)
            m_s[...] = jnp.zeros_like(m_s)
            gather.issue()

        gate_terms = _mlstm_gates(p_ref[:, M_OFF_G:M_OFF_G + LANES], b_ref[...])
        for h in range(H):
            C, n, m = C_s[h], n_s[h], m_s[h]
            C_all[h] = C
            n_all[h] = n
            m_all[h] = m
            (gated, Cn, nn), mn = _mlstm_head(
                h, p_ref[:, _head_cols(M_OFF_Q, h // 2)], p_ref[:, _head_cols(M_OFF_K, h // 2)],
                p_ref[:, _head_cols(M_OFF_V, h)], p_ref[:, _head_cols(M_OFF_O, h)], *gate_terms,
                hn_ref[:, _head_cols(0, h)], C, n, m)
            gated_ref[:, _head_cols(0, h)] = gated.astype(gated_ref.dtype)
            C_s[h] = Cn
            n_s[h] = nn
            m_s[h] = mn

        @pl.when(pl.program_id(0) == NC - 1)
        def _():
            gather.finish()

    return pl.pallas_call(
        body, name=name, grid=(NC,),
        in_specs=[ps, vec, hns] + [any_spec] * nsh,
        out_specs=[hv, Cs, ns, ms] + [any_spec] * nsh,
        out_shape=[jax.ShapeDtypeStruct((S, H * M_V), MXU_DTYPE),
                   jax.ShapeDtypeStruct((NC, H, LANES, M_V), F32),
                   jax.ShapeDtypeStruct((NC, H, 1, LANES), F32),
                   jax.ShapeDtypeStruct((NC, H, 1, 1), F32)] + _Gather.out_shape(shards),
        scratch_shapes=list(_MLSTM_STATE) + _Gather.semaphores(nsh),
        compiler_params=_cparams(("arbitrary",)),
    )(proj, bias, hn, *shards)


def _mlstm_bwd(proj, bias, hn, C_all, n_all, m_all, dgated, qs, *, name):
    S = proj.shape[0]
    H, NC = M_HEADS, S // M_CHUNK
    ps, vec, hns, hv, Cs, ns, ms = _mlstm_specs(NC, True)
    nq = len(qs)
    any_spec = pl.BlockSpec(memory_space=pl.ANY)

    def body(p_ref, b_ref, hn_ref, C_ref, n_ref, m_ref, dg_ref, *rest):
        dp_ref, db_ref, dhn_ref = rest[nq:nq + 3]
        dC_s, dn_s, send_sems, recv_sems = rest[2 * nq + 3:]
        exchange = _Exchange(rest[:nq], rest[nq + 3:2 * nq + 3], send_sems, recv_sems)

        @pl.when(pl.program_id(0) == 0)
        def _():
            dC_s[...] = jnp.zeros_like(dC_s)
            dn_s[...] = jnp.zeros_like(dn_s)
            db_ref[...] = jnp.zeros_like(db_ref)
            dhn_ref[...] = jnp.zeros_like(dhn_ref)
            exchange.issue()

        gate_terms, gates_vjp = jax.vjp(_mlstm_gates, p_ref[:, M_OFF_G:M_OFF_G + LANES], b_ref[...])
        d_terms = [jnp.zeros_like(t) for t in gate_terms]
        for h in range(H):
            def head(q, k, v, o, *rest, h=h):
                return _mlstm_head(h, q, k, v, o, *rest, m_ref[h])

            prim = (p_ref[:, _head_cols(M_OFF_Q, h // 2)], p_ref[:, _head_cols(M_OFF_K, h // 2)],
                    p_ref[:, _head_cols(M_OFF_V, h)], p_ref[:, _head_cols(M_OFF_O, h)], *gate_terms,
                    hn_ref[:, _head_cols(0, h)], C_ref[h], n_ref[h])
            _, vjp, _ = jax.vjp(head, *prim, has_aux=True)
            dq, dk, dv, do, *d_gate, dhnh, dC, dn = vjp((dg_ref[:, _head_cols(0, h)].astype(F32), dC_s[h], dn_s[h]))
            if h % 2 == 0:
                dq_pair, dk_pair = dq, dk
            else:
                dp_ref[:, _head_cols(M_OFF_Q, h // 2)] = (dq_pair + dq).astype(dp_ref.dtype)
                dp_ref[:, _head_cols(M_OFF_K, h // 2)] = (dk_pair + dk).astype(dp_ref.dtype)
            dp_ref[:, _head_cols(M_OFF_V, h)] = dv.astype(dp_ref.dtype)
            dp_ref[:, _head_cols(M_OFF_O, h)] = do.astype(dp_ref.dtype)
            d_terms = [a + g for a, g in zip(d_terms, d_gate)]
            dhn_ref[:, _head_cols(0, h)] += dhnh
            dC_s[h] = dC
            dn_s[h] = dn
        dG, dbias = gates_vjp(tuple(d_terms))
        dp_ref[:, M_OFF_G:M_OFF_G + LANES] = dG.astype(dp_ref.dtype)
        db_ref[...] += dbias

        @pl.when(pl.program_id(0) == NC - 1)
        def _():
            exchange.finish()

    return pl.pallas_call(
        body, name=name, grid=(NC,),
        in_specs=[ps, vec, hns, Cs, ns, ms, hv] + [any_spec] * nq,
        out_specs=[ps, vec, hns] + [any_spec] * nq,
        out_shape=[jax.ShapeDtypeStruct((S, M_PROJ), MXU_DTYPE), jax.ShapeDtypeStruct((1, LANES), F32),
                   jax.ShapeDtypeStruct((1, H * M_V), F32)] + _Exchange.out_shape(qs),
        scratch_shapes=list(_MLSTM_STATE[:2]) + _Exchange.semaphores(nq),
        compiler_params=_cparams(("arbitrary",)),
    )(proj, bias, hn, C_all, n_all, m_all, dgated, *qs)


A_NQ = A_QH * A_DH
A_NKV = 2 * A_KVH * A_DH
A_PAIRS = A_G // 2


def _sink_softmax_parts(s, sink):
    mx = jnp.maximum(jnp.max(s, axis=-1, keepdims=True), sink)
    e = jnp.exp(s - mx)
    e_sink = jnp.exp(sink - mx)
    r = 1.0 / (jnp.sum(e, axis=-1, keepdims=True) + e_sink)
    return e * r, e_sink * r


@jax.custom_vjp
def _sink_softmax(s, sink):
    return _sink_softmax_parts(s, sink)[0]


def _sink_softmax_fwd(s, sink):
    p, p_sink = _sink_softmax_parts(s, sink)
    return p, (p, p_sink)


def _sink_softmax_bwd(res, dp):
    p, p_sink = res
    t = jnp.sum(dp * p, axis=-1, keepdims=True)
    return p * (dp - t), -p_sink * t


_sink_softmax.defvjp(_sink_softmax_fwd, _sink_softmax_bwd)


def _band_bias():
    qi = np.arange(A_G * A_BLK)[:, None] % A_BLK
    ku = np.arange(2 * A_BLK)[None, :]
    diff = qi - (ku - A_BLK)
    band = (diff >= 0) & (diff < A_BLK)
    seen = np.stack([band, band & (ku >= A_BLK)])
    return jnp.asarray(np.where(seen, 0.0, -np.inf), F32)


_BAND_BIAS_SPEC = pl.BlockSpec((None, A_G * A_BLK, 2 * A_BLK), lambda n: (jnp.where(n == 0, 1, 0), 0, 0))


def _attn_group(Ps, KLO, KHI, VLO, VHI, sinks, bias):
    B = Ps[0].shape[0]
    R = len(Ps) * B
    q2 = jnp.concatenate(Ps, axis=0) * (A_DH ** -0.5)
    s = jnp.concatenate([_dot(q2, KLO, 1, 1), _dot(q2, KHI, 1, 1)], axis=0) + bias
    ri = lax.broadcasted_iota(jnp.int32, (2 * R, A_G), 0)
    head = 2 * lax.shift_right_logical(ri & (R - 1), B.bit_length() - 1) + lax.shift_right_logical(ri, R.bit_length() - 1)
    onehot = head == lax.broadcasted_iota(jnp.int32, (2 * R, A_G), 1)
    sink = jnp.sum(jnp.where(onehot, sinks, 0.0), axis=1, keepdims=True)
    p = _sink_softmax(s, sink)
    o = _dot(p[:R], VLO, 1, 0) + _dot(p[R:], VHI, 1, 0)
    return tuple(o[j * B:(j + 1) * B] for j in range(len(Ps)))


def _swap_halves_of_lanes(t):
    return pltpu.roll(t, LANES // 2, 1)


def _kv_operands(kvp_ref, kvc_ref, h):
    kk = jnp.concatenate([kvp_ref[:, :LANES], kvc_ref[:, :LANES]], axis=0)
    vv = jnp.concatenate([kvp_ref[:, LANES:], kvc_ref[:, LANES:]], axis=0)
    low = lax.broadcasted_iota(jnp.int32, kk.shape, 1) < A_DH
    own = low if h == 0 else jnp.logical_not(low)
    k_own = jnp.where(own, kk, 0.0)
    v_own = jnp.where(own, vv, 0.0)
    k_oth, v_oth = _swap_halves_of_lanes(k_own), _swap_halves_of_lanes(v_own)
    if h == 0:
        return own, k_own, k_oth, v_own, v_oth
    return own, k_oth, k_own, v_oth, v_own


def _pair_cols(h, j):
    c = (h * A_PAIRS + j) * LANES
    return slice(c, c + LANES)


def _attn_fwd(proj, sinks, *, name):
    S = proj.shape[0]
    NB = S // A_BLK
    kv_blk = A_NQ // A_NKV
    qs = pl.BlockSpec((A_BLK, A_NQ), lambda n: (n, 0))
    cur = pl.BlockSpec((A_BLK, A_NKV), lambda n: (n, kv_blk))
    prev = pl.BlockSpec((A_BLK, A_NKV), lambda n: (jnp.maximum(n - 1, 0), kv_blk))
    sk = pl.BlockSpec((A_KVH, A_G), lambda n: (0, 0))

    def body(q_ref, kvp_ref, kvc_ref, s_ref, bias_ref, o_ref):
        for h in range(A_KVH):
            _, KLO, KHI, VLO, VHI = _kv_operands(kvp_ref, kvc_ref, h)
            Ps = tuple(q_ref[:, _pair_cols(h, j)] for j in range(A_PAIRS))
            outs = _attn_group(Ps, KLO, KHI, VLO, VHI, s_ref[h:h + 1, :], bias_ref[...])
            for j in range(A_PAIRS):
                o_ref[:, _pair_cols(h, j)] = outs[j].astype(o_ref.dtype)

    return pl.pallas_call(
        body, name=name, grid=(NB,),
        in_specs=[qs, prev, cur, sk, _BAND_BIAS_SPEC], out_specs=qs,
        out_shape=jax.ShapeDtypeStruct((S, A_NQ), MXU_DTYPE),
        compiler_params=_cparams(("parallel",)),
    )(proj, proj, proj, sinks, _band_bias())


def _attn_bwd(proj, sinks, do, *, name):
    S = proj.shape[0]
    NB = S // A_BLK
    last = NB - 1
    kv_blk = A_NQ // A_NKV
    qs = pl.BlockSpec((A_BLK, A_NQ), lambda n: (jnp.minimum(n, last), 0))
    cur = pl.BlockSpec((A_BLK, A_NKV), lambda n: (jnp.minimum(n, last), kv_blk))
    prev = pl.BlockSpec((A_BLK, A_NKV), lambda n: (jnp.clip(n - 1, 0, last), kv_blk))
    sk = pl.BlockSpec((A_KVH, A_G), lambda n: (0, 0))
    lag = pl.BlockSpec((A_BLK, A_NKV), lambda n: (jnp.maximum(n - 1, 0), 0))
    cq_spec = pl.BlockSpec((1, A_NQ), lambda n: (0, 0))
    ckv_spec = pl.BlockSpec((1, A_NKV), lambda n: (0, 0))

    def body(q_ref, kvp_ref, kvc_ref, s_ref, do_ref, bias_ref, dq_ref, dkv_ref, ds_ref, cq_ref, ckv_ref, keep):
        n = pl.program_id(0)

        @pl.when(n == 0)
        def _():
            keep[...] = jnp.zeros_like(keep)
            ds_ref[...] = jnp.zeros_like(ds_ref)
            cq_ref[...] = jnp.zeros_like(cq_ref)
            ckv_ref[...] = jnp.zeros_like(ckv_ref)

        @pl.when(n < NB)
        def _():
            f = functools.partial(_attn_group, bias=bias_ref[...])
            dk = jnp.zeros((2 * A_BLK, LANES), F32)
            dv = jnp.zeros((2 * A_BLK, LANES), F32)
            ds_rows = []
            for h in range(A_KVH):
                own, KLO, KHI, VLO, VHI = _kv_operands(kvp_ref, kvc_ref, h)
                Ps = tuple(q_ref[:, _pair_cols(h, j)] for j in range(A_PAIRS))
                _, vjp = jax.vjp(f, Ps, KLO, KHI, VLO, VHI, s_ref[h:h + 1, :])
                dPs, dKLO, dKHI, dVLO, dVHI, dsk = vjp(
                    tuple(do_ref[:, _pair_cols(h, j)].astype(F32) for j in range(A_PAIRS)))
                for j in range(A_PAIRS):
                    dq_ref[:, _pair_cols(h, j)] = dPs[j].astype(dq_ref.dtype)
                    cq_ref[:, _pair_cols(h, j)] += jnp.sum(dPs[j], axis=0, keepdims=True)
                dk_own, dk_oth = (dKLO, dKHI) if h == 0 else (dKHI, dKLO)
                dv_own, dv_oth = (dVLO, dVHI) if h == 0 else (dVHI, dVLO)
                dk = dk + jnp.where(own, dk_own, 0.0) + _swap_halves_of_lanes(jnp.where(own, 0.0, dk_oth))
                dv = dv + jnp.where(own, dv_own, 0.0) + _swap_halves_of_lanes(jnp.where(own, 0.0, dv_oth))
                ds_rows.append(dsk)
            ds_ref[...] += jnp.concatenate(ds_rows, axis=0)
            dkv = jnp.concatenate([dk, dv], axis=1)
            done = keep[...] + dkv[:A_BLK]
            dkv_ref[...] = done.astype(dkv_ref.dtype)
            ckv_ref[...] += jnp.sum(done, axis=0, keepdims=True)
            keep[...] = dkv[A_BLK:]

        @pl.when(n == NB)
        def _():
            done = keep[...]
            dkv_ref[...] = done.astype(dkv_ref.dtype)
            ckv_ref[...] += jnp.sum(done, axis=0, keepdims=True)

    return pl.pallas_call(
        body, name=name, grid=(NB + 1,),
        in_specs=[qs, prev, cur, sk, qs, _BAND_BIAS_SPEC],
        out_specs=[qs, lag, sk, cq_spec, ckv_spec],
        out_shape=[jax.ShapeDtypeStruct((S, A_NQ), MXU_DTYPE), jax.ShapeDtypeStruct((S, A_NKV), MXU_DTYPE),
                   jax.ShapeDtypeStruct((A_KVH, A_G), F32), jax.ShapeDtypeStruct((1, A_NQ), F32),
                   jax.ShapeDtypeStruct((1, A_NKV), F32)],
        scratch_shapes=[pltpu.VMEM((A_BLK, A_NKV), F32)],
        compiler_params=_cparams(("arbitrary",)),
    )(proj, proj, proj, sinks, do, _band_bias())


HALO = 8


def _shift_rows(t, j):
    if j == 0:
        return t
    return pltpu.roll(t, j % t.shape[0], 0)


def _conv_gate(gate_ext, cw, cb):
    return cb + cw[0:1, :] * _shift_rows(gate_ext, 2) + cw[1:2, :] * _shift_rows(gate_ext, 1) + cw[2:3, :] * gate_ext


def _convgate_fwd(u, cw, cb, *, name):
    S, F2 = u.shape
    F = F2 // 2
    tm = _pick(S, (256, 128))
    hb = tm // HALO
    urow = pl.BlockSpec((tm, F2), lambda i: (i, 0))
    uprev = pl.BlockSpec((HALO, F), lambda i: (jnp.maximum(i * hb - 1, 0), 0))

    def body(u_ref, up_ref, cw_ref, cb_ref, a_ref):
        i = pl.program_id(0)
        gate = u_ref[:, :F]
        val = u_ref[:, F:]
        prev = jnp.where(i > 0, up_ref[...], 0.0)
        gc = _conv_gate(jnp.concatenate([prev, gate], axis=0), cw_ref[...], cb_ref[...])[HALO:]
        a_ref[...] = (gc * _sigmoid(gc) * val).astype(a_ref.dtype)

    return pl.pallas_call(
        body, name=name, grid=(S // tm,),
        in_specs=[urow, uprev, pl.BlockSpec((3, F), lambda i: (0, 0)), pl.BlockSpec((1, F), lambda i: (0, 0))],
        out_specs=pl.BlockSpec((tm, F), lambda i: (i, 0)),
        out_shape=jax.ShapeDtypeStruct((S, F), MXU_DTYPE),
        compiler_params=_cparams(("parallel",)),
    )(u, u, cw, cb)


def _convgate_bwd(u, da, cw, cb, gs, *, name):
    S, F2 = u.shape
    ng = len(gs)
    any_spec = pl.BlockSpec(memory_space=pl.ANY)
    F = F2 // 2
    tm = _pick(S, (128,))
    hb = tm // HALO
    nt = S // tm
    nh = S // HALO
    urow = pl.BlockSpec((tm, F2), lambda i: (i, 0))
    uprev = pl.BlockSpec((HALO, F), lambda i: (jnp.maximum(i * hb - 1, 0), 0))
    unext = pl.BlockSpec((HALO, F2), lambda i: (jnp.minimum((i + 1) * hb, nh - 1), 0))
    darow = pl.BlockSpec((tm, F), lambda i: (i, 0))
    danext = pl.BlockSpec((HALO, F), lambda i: (jnp.minimum((i + 1) * hb, nh - 1), 0))

    def body(u_ref, up_ref, un_ref, da_ref, dan_ref, cw_ref, cb_ref, *rest):
        du_ref, dcw_ref, dcb_ref = rest[ng:ng + 3]
        swap = _Swap(rest[:ng], rest[ng + 3:2 * ng + 3], *rest[2 * ng + 3:]) if ng else None
        i = pl.program_id(0)
        cwv = cw_ref[...]
        prev = jnp.where(i > 0, up_ref[...], 0.0)
        gate_ext = jnp.concatenate([prev, u_ref[:, :F], un_ref[:, :F]], axis=0)
        val_ext = jnp.concatenate([u_ref[:, F:], un_ref[:, F:]], axis=0)
        da_next = jnp.where(i < nt - 1, dan_ref[...].astype(F32), 0.0)
        da_ext = jnp.concatenate([da_ref[...].astype(F32), da_next], axis=0)
        gc = _conv_gate(gate_ext, cwv, cb_ref[...])[HALO:]
        sg = _sigmoid(gc)
        silu = gc * sg
        dval = da_ext * silu
        dgc = da_ext * val_ext * (sg * (1.0 + gc * (1.0 - sg)))
        dgate = cwv[2:3, :] * dgc + cwv[1:2, :] * _shift_rows(dgc, -1) + cwv[0:1, :] * _shift_rows(dgc, -2)
        du_ref[:, :F] = dgate[:tm].astype(du_ref.dtype)
        du_ref[:, F:] = dval[:tm].astype(du_ref.dtype)
        dgc_c = dgc[:tm]
        g0 = gate_ext[HALO:HALO + tm]
        g1 = _shift_rows(gate_ext, 1)[HALO:HALO + tm]
        g2 = _shift_rows(gate_ext, 2)[HALO:HALO + tm]
        dcw = jnp.concatenate([jnp.sum(dgc_c * g2, axis=0, keepdims=True),
                               jnp.sum(dgc_c * g1, axis=0, keepdims=True),
                               jnp.sum(dgc_c * g0, axis=0, keepdims=True)], axis=0)
        dcb = jnp.sum(dgc_c, axis=0, keepdims=True)

        @pl.when(i == 0)
        def _():
            dcw_ref[...] = dcw
            dcb_ref[...] = dcb
            if ng:
                swap.issue()

        @pl.when(i > 0)
        def _():
            dcw_ref[...] += dcw
            dcb_ref[...] += dcb

        if ng:
            @pl.when(i == nt - 1)
            def _():
                swap.finish()

    return pl.pallas_call(
        body, name=name, grid=(nt,),
        in_specs=[urow, uprev, unext, darow, danext,
                  pl.BlockSpec((3, F), lambda i: (0, 0)), pl.BlockSpec((1, F), lambda i: (0, 0))] + [any_spec] * ng,
        out_specs=[urow, pl.BlockSpec((3, F), lambda i: (0, 0)), pl.BlockSpec((1, F), lambda i: (0, 0))] + [any_spec] * ng,
        out_shape=[jax.ShapeDtypeStruct((S, F2), MXU_DTYPE), jax.ShapeDtypeStruct((3, F), F32),
                   jax.ShapeDtypeStruct((1, F), F32)] + (_Swap.out_shape(gs) if ng else []),
        scratch_shapes=_Swap.semaphores(ng) if ng else [],
        compiler_params=_cparams(("arbitrary",)),
    )(u, u, u, da, da, cw, cb, *gs)


def _adamw_math(w, g, m, v):
    m = ADAM_B1 * m + (1.0 - ADAM_B1) * g
    v = ADAM_B2 * v + (1.0 - ADAM_B2) * (g * g)
    m_hat = m / (1.0 - ADAM_B1 ** ADAM_STEP)
    v_hat = v / (1.0 - ADAM_B2 ** ADAM_STEP)
    delta = -ADAM_LR * (m_hat / (jnp.sqrt(v_hat) + ADAM_EPS) + ADAM_WD * w)
    return delta, m, v


def _adamw_layers(w, gs, m, v, *, name):
    Lr, R, C = w.shape
    tr = _pick(R, (256, 128, 64, 32, 16, 8))
    outs = None
    for layer, g in enumerate(gs):
        def body(w_ref, g_ref, m_ref, v_ref, *rest):
            go_ref, d_ref, nm_ref, nv_ref = rest[-4:]
            gv = g_ref[...]
            d, nm, nv = _adamw_math(w_ref[...], gv, m_ref[...], v_ref[...])
            go_ref[...] = gv
            d_ref[...] = d
            nm_ref[...] = nm
            nv_ref[...] = nv

        lay = pl.BlockSpec((None, tr, C), lambda i, layer=layer: (layer, i, 0))
        in_specs = [lay, pl.BlockSpec((tr, C), lambda i: (i, 0)), lay, lay]
        args = [w, g, m, v]
        aliases = {}
        if outs is not None:
            in_specs += [pl.BlockSpec(memory_space=pl.ANY)] * 4
            args += list(outs)
            aliases = {4 + t: t for t in range(4)}
        outs = pl.pallas_call(
            body, name=f"{name}_{layer}", grid=(R // tr,), in_specs=in_specs, out_specs=[lay] * 4,
            out_shape=[jax.ShapeDtypeStruct((Lr, R, C), F32)] * 4, input_output_aliases=aliases,
            compiler_params=_cparams(("parallel",)),
        )(*args)
    return outs


def _adamw_small(items, *, name):
    n = len(items)

    def body(*refs):
        ins, outs = refs[:4 * n], refs[4 * n:]
        for t in range(n):
            w, g, m, v = (r[...] for r in ins[4 * t:4 * t + 4])
            d, nm, nv = _adamw_math(w, g, m, v)
            outs[3 * t][...] = d
            outs[3 * t + 1][...] = nm
            outs[3 * t + 2][...] = nv

    flat = [a for it in items for a in it]
    out_shape = [jax.ShapeDtypeStruct(it[0].shape, F32) for it in items for _ in range(3)]
    vm = pl.BlockSpec(memory_space=pltpu.VMEM)
    res = pl.pallas_call(body, name=name, in_specs=[vm] * len(flat), out_specs=[vm] * len(out_shape),
                         out_shape=out_shape)(*flat)
    return [tuple(res[3 * t:3 * t + 3]) for t in range(n)]


def _place():
    return lax.axis_index("x"), lax.axis_index("y"), lax.axis_index("c")


_FLIPS = ((1, 0), (0, 1), (1, 1))


ROW_ALIGN = 16


def _half(rows, which):
    return pl.ds(pl.multiple_of(which * (rows // 2), ROW_ALIGN), rows // 2)


def _remote(src, dst, send_sems, recv_sems, k, to):
    return pltpu.make_async_remote_copy(src_ref=src, dst_ref=dst, send_sem=send_sems.at[k], recv_sem=recv_sems.at[k],
                                        device_id=to, device_id_type=MESH)


def _gather_shards(shards, *, name):
    n = len(shards)
    any_spec = pl.BlockSpec(memory_space=pl.ANY)

    def body(*refs):
        gather = _Gather(refs[:n], refs[n:2 * n], refs[2 * n], refs[2 * n + 1])
        gather.issue()
        gather.finish()

    return pl.pallas_call(
        body, name=name, in_specs=[any_spec] * n, out_specs=[any_spec] * n,
        out_shape=_Gather.out_shape(shards), scratch_shapes=_Gather.semaphores(n),
    )(*shards)


class _Gather:
    def __init__(self, w_refs, out_refs, send_sems, recv_sems):
        self.w_refs, self.out_refs, self.send_sems, self.recv_sems = w_refs, out_refs, send_sems, recv_sems
        self.pairs = [(i, j) for i in range(len(w_refs)) for j in range(3)]

    @staticmethod
    def out_shape(shards):
        return [jax.ShapeDtypeStruct((N_CHIPS,) + s.shape, s.dtype) for s in shards]

    @staticmethod
    def semaphores(n):
        return [pltpu.SemaphoreType.DMA((6 * n,)), pltpu.SemaphoreType.DMA((6 * n,))]

    def _where(self):
        x, y, c = _place()
        return x, y, c, [(x ^ fx, y ^ fy) for fx, fy in _FLIPS]

    def _over_ici(self, i, j, landed):
        x, y, c, chips = self._where()
        px, py = chips[j]
        mine = _half(self.w_refs[i].shape[0], c)
        if landed:
            src = dst = self.out_refs[i].at[2 * px + py, mine]
        else:
            src, dst = self.w_refs[i].at[mine], self.out_refs[i].at[2 * x + y, mine]
        return _remote(src, dst, self.send_sems, self.recv_sems, 6 * i + j, (px, py, c))

    def _over_d2d(self, i, j, which):
        x, y, c, chips = self._where()
        px, py = chips[j]
        blk = self.out_refs[i].at[2 * px + py, _half(self.w_refs[i].shape[0], which)]
        return _remote(blk, blk, self.send_sems, self.recv_sems, 6 * i + 3 + j, (x, y, 1 - c))

    def issue(self):
        for i, j in self.pairs:
            self._over_ici(i, j, False).start()

    def finish(self):
        c = lax.axis_index("c")
        for i, j in self.pairs:
            self._over_ici(i, j, True).wait_recv()
            self._over_d2d(i, j, c).start()
        for i, j in self.pairs:
            self._over_d2d(i, j, 1 - c).wait_recv()
        for i, j in self.pairs:
            self._over_ici(i, j, False).wait_send()
            self._over_d2d(i, j, c).wait_send()


def _swap_halves(gs, *, name):
    n = len(gs)
    any_spec = pl.BlockSpec(memory_space=pl.ANY)

    def body(*refs):
        swap = _Swap(refs[:n], refs[n:2 * n], refs[2 * n], refs[2 * n + 1])
        swap.issue()
        swap.finish()

    return pl.pallas_call(
        body, name=name, in_specs=[any_spec] * n, out_specs=[any_spec] * n,
        out_shape=_Swap.out_shape(gs), scratch_shapes=_Swap.semaphores(n),
    )(*gs)


class _Swap:
    def __init__(self, g_refs, out_refs, send_sems, recv_sems):
        self.g_refs, self.out_refs, self.send_sems, self.recv_sems = g_refs, out_refs, send_sems, recv_sems

    @staticmethod
    def out_shape(gs):
        return [jax.ShapeDtypeStruct((N_CHIPS, g.shape[1] // 2, g.shape[2]), g.dtype) for g in gs]

    @staticmethod
    def semaphores(n):
        return [pltpu.SemaphoreType.DMA((n,)), pltpu.SemaphoreType.DMA((n,))]

    def _copies(self):
        x, y, c = _place()
        return [_remote(g.at[:, _half(g.shape[1], 1 - c)], out, self.send_sems, self.recv_sems, i, (x, y, 1 - c))
                for i, (g, out) in enumerate(zip(self.g_refs, self.out_refs))]

    def issue(self):
        for cp in self._copies():
            cp.start()

    def finish(self):
        for cp in self._copies():
            cp.wait()


def _add_halves(g, got, c_arr, *, name):
    _, rows, cols = g.shape
    blk = (None, rows // 2, cols)

    def body(c_ref, g_ref, got_ref, o_ref):
        o_ref[...] = (g_ref[...] + got_ref[...]).astype(o_ref.dtype)

    return pl.pallas_call(
        body, name=name,
        grid_spec=pltpu.PrefetchScalarGridSpec(
            num_scalar_prefetch=1, grid=(N_CHIPS,),
            in_specs=[pl.BlockSpec(blk, lambda j, c_ref: (j, c_ref[0], 0)), pl.BlockSpec(blk, lambda j, c_ref: (j, 0, 0))],
            out_specs=pl.BlockSpec(blk, lambda j, c_ref: (j, 0, 0))),
        out_shape=jax.ShapeDtypeStruct((N_CHIPS, rows // 2, cols), WIRE_DTYPE),
        compiler_params=_cparams(("parallel",)),
    )(c_arr, g, got)


def _exchange_chips(qs, *, name):
    n = len(qs)
    any_spec = pl.BlockSpec(memory_space=pl.ANY)

    def body(*refs):
        exchange = _Exchange(refs[:n], refs[n:2 * n], refs[2 * n], refs[2 * n + 1])
        exchange.issue()
        exchange.finish()

    return pl.pallas_call(
        body, name=name, in_specs=[any_spec] * n, out_specs=[any_spec] * n,
        out_shape=_Exchange.out_shape(qs), scratch_shapes=_Exchange.semaphores(n),
    )(*qs)


class _Exchange:
    def __init__(self, q_refs, out_refs, send_sems, recv_sems):
        self.q_refs, self.out_refs, self.send_sems, self.recv_sems = q_refs, out_refs, send_sems, recv_sems
        self.pairs = [(i, j) for i in range(len(q_refs)) for j in range(3)]

    @staticmethod
    def out_shape(qs):
        return [jax.ShapeDtypeStruct(q.shape, q.dtype) for q in qs]

    @staticmethod
    def semaphores(n):
        return [pltpu.SemaphoreType.DMA((3 * n,)), pltpu.SemaphoreType.DMA((3 * n,))]

    def _copy(self, i, j, landed):
        x, y, c = _place()
        px, py = [(x ^ fx, y ^ fy) for fx, fy in _FLIPS][j]
        if landed:
            src = dst = self.out_refs[i].at[2 * px + py]
        else:
            src, dst = self.q_refs[i].at[2 * px + py], self.out_refs[i].at[2 * x + y]
        return _remote(src, dst, self.send_sems, self.recv_sems, 3 * i + j, (px, py, c))

    def issue(self):
        for i, j in self.pairs:
            self._copy(i, j, False).start()

    def finish(self):
        for i, j in self.pairs:
            self._copy(i, j, True).wait_recv()
        for i, j in self.pairs:
            self._copy(i, j, False).wait_send()


def _sum_chips(q, r, place_arr, *, name):
    _, h, cols = q.shape
    blk = (None, h, cols)

    def body(p_ref, q_ref, r1_ref, r2_ref, r3_ref, o_ref):
        o_ref[...] = ((q_ref[...].astype(F32) + r1_ref[...].astype(F32)) + r2_ref[...].astype(F32)) + r3_ref[...].astype(F32)

    other = [pl.BlockSpec(blk, lambda i, p_ref, f=f: (p_ref[0] ^ f, 0, 0)) for f in (1, 2, 3)]
    return pl.pallas_call(
        body, name=name,
        grid_spec=pltpu.PrefetchScalarGridSpec(
            num_scalar_prefetch=1, grid=(1,),
            in_specs=[pl.BlockSpec(blk, lambda i, p_ref: (p_ref[0], 0, 0))] + other,
            out_specs=pl.BlockSpec((h, cols), lambda i, p_ref: (p_ref[1], 0))),
        out_shape=jax.ShapeDtypeStruct((2 * h, cols), F32),
        compiler_params=_cparams(("arbitrary",)),
    )(place_arr, q, r, r, r)


def _join_halves(fs, *, name):
    n = len(fs)
    any_spec = pl.BlockSpec(memory_space=pl.ANY)

    def body(*refs):
        out_refs, send_sems, recv_sems = refs[n:2 * n], refs[2 * n], refs[2 * n + 1]
        x, y, c = _place()
        sent = []
        for i in range(n):
            mine = out_refs[i].at[_half(fs[i].shape[0], c)]
            cp = _remote(mine, mine, send_sems, recv_sems, i, (x, y, 1 - c))
            cp.start()
            sent.append(cp)
        for i in range(n):
            its = out_refs[i].at[_half(fs[i].shape[0], 1 - c)]
            _remote(its, its, send_sems, recv_sems, i, (x, y, 1 - c)).wait_recv()
        for cp in sent:
            cp.wait_send()

    return pl.pallas_call(
        body, name=name, in_specs=[any_spec] * n, out_specs=[any_spec] * n,
        out_shape=[jax.ShapeDtypeStruct(f.shape, f.dtype) for f in fs], input_output_aliases={i: i for i in range(n)},
        scratch_shapes=[pltpu.SemaphoreType.DMA((n,)), pltpu.SemaphoreType.DMA((n,))],
    )(*fs)


def _allreduce_small(buf, *, name):
    R = buf.shape[0]
    vm = pl.BlockSpec(memory_space=pltpu.VMEM)

    def body(b_ref, o_ref, slots, send_sems, recv_sems):
        x, y, c = _place()
        me = 4 * x + 2 * y + c
        slots[me] = b_ref[...]
        sends = []
        for kk in range(1, 8):
            fx, fy, fc = (kk >> 2) & 1, (kk >> 1) & 1, kk & 1
            cp = pltpu.make_async_remote_copy(
                src_ref=b_ref, dst_ref=slots.at[me], send_sem=send_sems.at[kk - 1], recv_sem=recv_sems.at[kk - 1],
                device_id=(x ^ fx, y ^ fy, c ^ fc), device_id_type=MESH)
            cp.start()
            sends.append(cp)
        for kk in range(1, 8):
            fx, fy, fc = (kk >> 2) & 1, (kk >> 1) & 1, kk & 1
            peer = 4 * (x ^ fx) + 2 * (y ^ fy) + (c ^ fc)
            pltpu.make_async_remote_copy(
                src_ref=b_ref, dst_ref=slots.at[peer], send_sem=send_sems.at[kk - 1], recv_sem=recv_sems.at[kk - 1],
                device_id=(x ^ fx, y ^ fy, c ^ fc), device_id_type=MESH).wait_recv()
        for cp in sends:
            cp.wait_send()
        acc = slots[0]
        for d in range(1, 8):
            acc = acc + slots[d]
        o_ref[...] = acc

    return pl.pallas_call(
        body, name=name, in_specs=[vm], out_specs=vm,
        out_shape=jax.ShapeDtypeStruct((R, LANES), F32),
        scratch_shapes=[pltpu.VMEM((8, R, LANES), F32), pltpu.SemaphoreType.DMA((7,)), pltpu.SemaphoreType.DMA((7,))],
    )(buf)


def _pad_rows(v, mult=8 * LANES):
    flat = v.reshape(-1)
    n = flat.shape[0]
    tot = -(-n // mult) * mult
    return jnp.pad(flat, (0, tot - n)).reshape(-1, LANES)


def _pack_small(parts):
    return jnp.concatenate([_pad_rows(p.astype(F32)) for p in parts], axis=0)


def _unpack_small(buf, shapes):
    out, r = [], 0
    for sh in shapes:
        n = math.prod(sh)
        rows = -(-n // (8 * LANES)) * 8
        out.append(buf[r:r + rows].reshape(-1)[:n].reshape(sh))
        r += rows
    return out


def _cols_to_shards(w):
    *lead, K, N = w.shape
    t = w.reshape(*lead, K, N_CHIPS, N // N_CHIPS)
    return jnp.moveaxis(t, -2, 0)


def _shards_to_cols(t):
    t = jnp.moveaxis(t, 0, -2)
    *lead, K, _, n = t.shape
    return t.reshape(*lead, K, N_CHIPS * n)


_BIG = ("m_w_in", "m_w_out", "a_w_in", "a_w_out", "f_w_up", "f_w_down")


def kernel(x, m_w_in, m_gate_bias, m_head_norm, m_w_out, a_w_in, a_b_in, a_sinks, a_w_out, a_b_out, norm_mix_pre, norm_mix_post, norm_ffn_pre, norm_ffn_post, f_w_up, f_conv_w, f_conv_b, f_w_down, loss_target, m_m_w_in, m_m_gate_bias, m_m_head_norm, m_m_w_out, m_a_w_in, m_a_b_in, m_a_sinks, m_a_w_out, m_a_b_out, m_norm_mix_pre, m_norm_mix_post, m_norm_ffn_pre, m_norm_ffn_post, m_f_w_up, m_f_conv_w, m_f_conv_b, m_f_w_down, v_m_w_in, v_m_gate_bias, v_m_head_norm, v_m_w_out, v_a_w_in, v_a_b_in, v_a_sinks, v_a_w_out, v_a_b_out, v_norm_mix_pre, v_norm_mix_post, v_norm_ffn_pre, v_norm_ffn_post, v_f_w_up, v_f_conv_w, v_f_conv_b, v_f_w_down):
    params = dict(m_w_in=m_w_in, m_gate_bias=m_gate_bias, m_head_norm=m_head_norm, m_w_out=m_w_out, a_w_in=a_w_in,
                  a_b_in=a_b_in, a_sinks=a_sinks, a_w_out=a_w_out, a_b_out=a_b_out, norm_mix_pre=norm_mix_pre,
                  norm_mix_post=norm_mix_post, norm_ffn_pre=norm_ffn_pre, norm_ffn_post=norm_ffn_post, f_w_up=f_w_up,
                  f_conv_w=f_conv_w, f_conv_b=f_conv_b, f_w_down=f_w_down)
    mom1 = dict(m_w_in=m_m_w_in, m_gate_bias=m_m_gate_bias, m_head_norm=m_m_head_norm, m_w_out=m_m_w_out,
                a_w_in=m_a_w_in, a_b_in=m_a_b_in, a_sinks=m_a_sinks, a_w_out=m_a_w_out, a_b_out=m_a_b_out,
                norm_mix_pre=m_norm_mix_pre, norm_mix_post=m_norm_mix_post, norm_ffn_pre=m_norm_ffn_pre,
                norm_ffn_post=m_norm_ffn_post, f_w_up=m_f_w_up, f_conv_w=m_f_conv_w, f_conv_b=m_f_conv_b,
                f_w_down=m_f_w_down)
    mom2 = dict(m_w_in=v_m_w_in, m_gate_bias=v_m_gate_bias, m_head_norm=v_m_head_norm, m_w_out=v_m_w_out,
                a_w_in=v_a_w_in, a_b_in=v_a_b_in, a_sinks=v_a_sinks, a_w_out=v_a_w_out, a_b_out=v_a_b_out,
                norm_mix_pre=v_norm_mix_pre, norm_mix_post=v_norm_mix_post, norm_ffn_pre=v_norm_ffn_pre,
                norm_ffn_post=v_norm_ffn_post, f_w_up=v_f_w_up, f_conv_w=v_f_conv_w, f_conv_b=v_f_conv_b,
                f_w_down=v_f_w_down)
    order = list(params)

    mx, my, mc = _place()
    chip = 2 * mx + my
    h0 = x[0]
    target = loss_target[0]

    def two_d(t):
        return t.reshape(-1, t.shape[-1])

    early = ("m_w_in",)
    late = tuple(n for n in _BIG if n not in early)
    mine = {n: two_d(params[n]).astype(MXU_DTYPE) for n in _BIG}

    def with_own_slot(names, theirs):
        return {n: lax.dynamic_update_slice(t, mine[n][None], (chip, 0, 0)) for n, t in zip(names, theirs)}

    gathered = with_own_slot(early, _gather_shards([mine[n] for n in early], name="gather_weights_early"))

    def in_place(shard, axis):
        width = shard.shape[axis]
        z = jnp.zeros(shard.shape[:axis] + (N_CHIPS * width,) + shard.shape[axis + 1:], F32)
        contrib = jnp.where(mc == 0, shard, 0.0)
        return lax.dynamic_update_slice_in_dim(z, contrib, chip * width, axis)

    sm_in = [in_place(a_b_in, 1), in_place(a_b_out, 1), in_place(f_conv_w, 2)]
    sm_full = _unpack_small(_allreduce_small(_pack_small(sm_in), name="gather_small"), [t.shape for t in sm_in])
    b_in_full, b_out_full, conv_w_full = sm_full

    W_in = _shards_to_cols(gathered["m_w_in"])
    W_all = jnp.pad(W_in, ((0, 0), (0, M_PROJ - W_in.shape[1])))
    gbias = jnp.pad(m_gate_bias[0].reshape(1, 2 * M_HEADS), ((0, 0), (0, LANES - 2 * M_HEADS)))

    grads = {}

    def ffn_fwd(i, z1, end):
        u = _mm(z1, W_up, b_layer=i, name=f"ffn_up{i}")
        a = _convgate_fwd(u, conv_w_full[i], f_conv_b[i:i + 1], name=f"ffn_act{i}")
        return u, a, _mm(a, W_down[i], rows=end, name=f"ffn_down{i}")

    z0 = _rms_fwd(h0, norm_mix_pre[0:1], out_dtype=MXU_DTYPE, name="mix_pre_norm0")
    proj = _mm(z0, W_all, name="mlstm_proj")
    gated, C_all, n_all, m_all, *late_theirs = _mlstm_fwd(proj, gbias, m_head_norm, [mine[n] for n in late],
                                                          name="mlstm_fwd")
    gathered.update(with_own_slot(late, late_theirs))
    W_mout = gathered["m_w_out"].reshape(D_MODEL, D_MODEL)
    A_in, A_out = _shards_to_cols(gathered["a_w_in"]), gathered["a_w_out"].reshape(D_MODEL, D_MODEL)
    W_up = gathered["f_w_up"]
    dsh = D_FF // N_CHIPS
    W_down = [gathered["f_w_down"][:, i * dsh:(i + 1) * dsh].reshape(D_FF, D_MODEL) for i in range(2)]
    zm0, h1, z1 = _mm(gated, W_mout, rows=_rows_sublayer_end(h0, norm_mix_post[0:1], norm_ffn_pre[0:1]), name="mlstm_out")
    u0, a0, (zf0, h2, y0) = ffn_fwd(0, z1, _rows_sublayer_end(h1, norm_ffn_post[0:1], norm_mix_pre[1:2]))

    aproj = _mm(y0, A_in, bias=b_in_full, name="attn_proj")
    sinks = a_sinks.reshape(A_KVH, A_G)
    ao = _attn_fwd(aproj, sinks, name="attn_fwd")
    zm1, h3, z3 = _mm(ao, A_out, bias=b_out_full, rows=_rows_sublayer_end(h2, norm_mix_post[1:2], norm_ffn_pre[1:2]),
                      name="attn_out")
    u1, a1, (zf1, dh, dzf1, g_post1, loss_lanes) = ffn_fwd(1, z3, _rows_last_sublayer_end(h3, norm_ffn_post[1:2], target))
    loss = lax.psum(loss_lanes[0, 0], ("x", "y", "c"))

    g_post, g_fpre, g_mpost, g_mpre = [None, g_post1], [None, None], [None, None], [None, None]
    dW_up, dW_down, dconv_w, dconv_b = [None, None], [None, None], [None, None], [None, None]

    def by_rows(g):
        return g.reshape(N_CHIPS, -1, g.shape[-1])

    def ffn_bwd(i, dzf, z1_, u, a, start, ride):
        da = _mm(dzf, W_down[i], tb=True, name=f"ffn_down_dx{i}")
        dW_down[i] = _mm(a, dzf, ta=True, name=f"ffn_down_dw{i}")
        riders = (list(ride) + [by_rows(dW_down[i])]) if ride is not None else []
        du, dconv_w[i], dconv_b[i], *got = _convgate_bwd(u, da, conv_w_full[i], f_conv_b[i:i + 1], riders,
                                                         name=f"ffn_act_bwd{i}")
        dW_up[i] = _mm(z1_, du, ta=True, out_stacked=True, name=f"ffn_up_dw{i}")
        return _mm(du, W_up, tb=True, b_layer=i, rows=start, name=f"ffn_up_dx{i}"), riders, got

    (dh3, g_fpre[1], dzm1, g_mpost[1], db_out), _, _ = ffn_bwd(
        1, dzf1, z3, u1, a1, _rows_sublayer_start_bwd(h3, dh, zm1, norm_ffn_pre[1:2], norm_mix_post[1:2]), None)
    dao = _mm(dzm1, A_out, tb=True, name="attn_out_dx")
    dA_out = _mm(ao, dzm1, ta=True, name="attn_out_dw")
    daq, dakv, dsinks, cs_q, cs_kv = _attn_bwd(aproj, sinks, dao, name="attn_bwd")
    db_in = jnp.concatenate([cs_q, cs_kv], axis=1)
    dh2, g_mpre[1], dzf0, g_post[0], _ = _mm(
        daq, A_in[:, :A_NQ], tb=True, add=_mm(dakv, A_in[:, A_NQ:], tb=True, name="attn_proj_kv_dx"),
        rows=_rows_sublayer_start_bwd(h2, dh3, zf0, norm_mix_pre[1:2], norm_ffn_post[0:1]), name="attn_proj_q_dx")
    dA_in = jnp.concatenate([_mm(y0, daq, ta=True, name="attn_proj_q_dw"),
                             _mm(y0, dakv, ta=True, name="attn_proj_kv_dw")], axis=1)

    c_arr = jnp.reshape(mc, (1,)).astype(jnp.int32)
    place_arr = jnp.stack([chip, mc]).astype(jnp.int32)

    def add_halves(tags, gs, got):
        return [_add_halves(g, t, c_arr, name=f"grad_add_halves_{tag}") for tag, g, t in zip(tags, gs, got)]

    (dh1, g_fpre[0], dzm0, g_mpost[0], _), rode, rode_got = ffn_bwd(
        0, dzf0, z1, u0, a0, _rows_sublayer_start_bwd(h1, dh2, zm0, norm_ffn_pre[0:1], norm_mix_post[0:1]),
        [_cols_to_shards(dA_in), by_rows(dA_out), dW_up[1], by_rows(dW_down[1])])
    rode_tags = ["a_w_in", "a_w_out", "f_w_up1", "f_w_down1", "f_w_down0"]
    dgated = _mm(dzm0, W_mout, tb=True, name="mlstm_out_dx")
    dW_mout = _mm(gated, dzm0, ta=True, name="mlstm_out_dw")
    next_tags, next_gs = ["f_w_up0", "m_w_out"], [dW_up[0], by_rows(dW_mout)]
    late_tags = rode_tags + next_tags
    late_part = (add_halves(rode_tags, rode, rode_got)
                 + add_halves(next_tags, next_gs, _swap_halves(next_gs, name="grad_swap_halves_late")))
    dproj, dgbias, dhn, *late_from = _mlstm_bwd(proj, gbias, m_head_norm, C_all, n_all, m_all, dgated, late_part,
                                                name="mlstm_bwd")
    grad_x, g_mpre[0] = _mm(dproj, W_all, tb=True, rows=_rows_first_sublayer_start_bwd(h0, dh1, norm_mix_pre[0:1]),
                            name="mlstm_proj_dx")
    dW_all = _mm(z0, dproj, ta=True, name="mlstm_proj_dw")
    dW_min = dW_all[:, :m_w_in.shape[-1] * N_CHIPS]

    early_tags, early_gs = ["m_w_in"], [_cols_to_shards(dW_min)]
    early_part = add_halves(early_tags, early_gs, _swap_halves(early_gs, name="grad_swap_halves_early"))
    early_from = _exchange_chips(early_part, name="grad_exchange_chips_early")
    tags = early_tags + late_tags
    halves = [_sum_chips(q, r, place_arr, name=f"grad_sum_chips_{tag}")
              for tag, q, r in zip(tags, early_part + late_part, list(early_from) + list(late_from))]
    reduced = dict(zip(tags, _join_halves(halves, name="grad_join_halves")))
    layer_grads = dict(m_w_in=[reduced["m_w_in"]], m_w_out=[reduced["m_w_out"]], a_w_in=[reduced["a_w_in"]],
                       a_w_out=[reduced["a_w_out"]], f_w_up=[reduced["f_w_up0"], reduced["f_w_up1"]],
                       f_w_down=[reduced["f_w_down0"], reduced["f_w_down1"]])

    small_g = [
        dgbias[:, :2 * M_HEADS].reshape(1, 2, M_HEADS),
        dhn,
        dsinks.reshape(1, A_QH),
        db_in, db_out,
        jnp.concatenate(g_mpre), jnp.concatenate(g_mpost), jnp.concatenate(g_fpre), jnp.concatenate(g_post),
        jnp.stack(dconv_w), jnp.concatenate(dconv_b),
    ]
    small_names = ["m_gate_bias", "m_head_norm", "a_sinks", "a_b_in", "a_b_out", "norm_mix_pre", "norm_mix_post",
                   "norm_ffn_pre", "norm_ffn_post", "f_conv_w", "f_conv_b"]
    red = _unpack_small(_allreduce_small(_pack_small(small_g), name="reduce_small"), [t.shape for t in small_g])
    for n, t in zip(small_names, red):
        if n in ("a_b_in", "a_b_out", "f_conv_w"):
            axis = t.ndim - 1
            width = params[n].shape[axis]
            t = lax.dynamic_slice_in_dim(t, chip * width, width, axis)
        grads[n] = t

    deltas, new_m, new_v = {}, {}, {}
    for n in _BIG:
        grads[n], deltas[n], new_m[n], new_v[n] = _adamw_layers(params[n], layer_grads[n], mom1[n], mom2[n],
                                                                name=f"adamw_{n}")
    two = lambda t: t.reshape(-1, t.shape[-1])
    res = _adamw_small([(two(params[n]), two(grads[n]), two(mom1[n]), two(mom2[n])) for n in small_names],
                       name="adamw_small")
    for n, (d, nm, nv) in zip(small_names, res):
        sh = params[n].shape
        deltas[n], new_m[n], new_v[n] = d.reshape(sh), nm.reshape(sh), nv.reshape(sh)

    return (loss, grad_x[None], *[grads[n] for n in order], *[deltas[n] for n in order],
            *[new_m[n] for n in order], *[new_v[n] for n in order])
```

```python
import functools
import math

import numpy as np

import jax
import jax.numpy as jnp
from jax import lax
from jax.experimental import pallas as pl
from jax.experimental.pallas import tpu as pltpu

F32 = jnp.float32
MXU_DTYPE = jnp.bfloat16
WIRE_DTYPE = jnp.bfloat16
MESH = pl.DeviceIdType.MESH

D_MODEL = 1024
EPS = 1e-6
M_HEADS, M_QK, M_V, M_CHUNK = 8, 64, 128, 128
GATE_CAP = 15.0
A_DH, A_QH, A_KVH, A_G, A_BLK = 64, 16, 2, 8, 128
D_FF = 2816
N_CHIPS = 4
LANES = 128
VMEM_LIMIT = 56 * 1024 * 1024

ADAM_LR, ADAM_B1, ADAM_B2, ADAM_EPS, ADAM_WD, ADAM_STEP = 0.001, 0.9, 0.999, 1e-08, 0.01, 10


def _cparams(sem):
    return pltpu.CompilerParams(dimension_semantics=sem, vmem_limit_bytes=VMEM_LIMIT)


def _pick(n, cands):
    for c in cands:
        if n % c == 0:
            return c
    return n


class _Rows:
    def __init__(self, tiles, vecs, outs, fn):
        self.tiles, self.vecs, self.outs, self.fn = list(tiles), list(vecs), list(outs), fn


ROWS_TILE = 512
ROWS_FULL_K = 4224
WIDE_N = 3200


def _mm(a, b, *, ta=False, tb=False, out_dtype=F32, bias=None, add=None, b_layer=None, out_stacked=False, rows=None,
        exchange=(), name):
    if ta:
        K, M = a.shape
    else:
        M, K = a.shape
    b_stacked = b_layer is not None
    if b_stacked:
        assert not ta
        n_sh = b.shape[2]
        w_rows, w_cols = D_MODEL, N_CHIPS * n_sh
        N, Kb = (w_rows, w_cols) if tb else (w_cols, w_rows)
    elif tb:
        N, Kb = b.shape
    else:
        Kb, N = b.shape
    assert K == Kb, (a.shape, b.shape)
    tm = _pick(M, (1024, 1408, 512, 256, 128))
    tn = _pick(N, (1024, 1408, 1280, 640, 512, 256, 128))
    tk = K if K <= 2816 else _pick(K, (2816, 2048, 1408, 1024, 512, 256, 128))
    if tn < 1024 <= N <= WIDE_N:
        tn, tm, tk = N, min(tm, 512), min(tk, 1024)
    shards_per_step = 1
    if b_stacked and tb:
        shards_per_step = 2
        tk = shards_per_step * n_sh
    if (b_stacked and not tb) or out_stacked:
        tn = N // N_CHIPS
    if rows is not None:
        assert tn == N and not out_stacked and not ta
        tm = min(tm, ROWS_TILE)
        if K <= ROWS_FULL_K:
            tk = K
    nk = K // tk
    dn = (((0 if ta else 1,), (1 if tb else 0,)), ((), ()))
    has_bias, has_add = bias is not None, add is not None
    n_tiles, n_vecs, n_outs = (len(rows.tiles), len(rows.vecs), len(rows.outs)) if rows is not None else (0, 0, 1)
    n_ex = len(exchange)
    assert not n_ex or rows is not None
    grid = (M // tm, N // tn, nk)

    def body(*refs):
        a_ref, b_ref = refs[0], refs[1]
        pos = 2
        bias_ref = add_ref = None
        if has_bias:
            bias_ref = refs[pos]
            pos += 1
        if has_add:
            add_ref = refs[pos]
            pos += 1
        tile_refs, vec_refs = refs[pos:pos + n_tiles], refs[pos + n_tiles:pos + n_tiles + n_vecs]
        pos += n_tiles + n_vecs
        ex_in = refs[pos:pos + n_ex]
        pos += n_ex
        out_refs, ex_out = refs[pos:pos + n_outs], refs[pos + n_outs:pos + n_outs + n_ex]
        pos += n_outs + n_ex
        acc_ref = refs[pos] if nk > 1 else None
        if n_ex:
            ex = _Exchange(ex_in, ex_out, *refs[pos + (nk > 1):])
            step = (pl.program_id(0) * grid[1] + pl.program_id(1)) * grid[2] + pl.program_id(2)

            @pl.when(step == 0)
            def _():
                ex.issue()

        def finish(r):
            if has_bias:
                r = r + bias_ref[...]
            if has_add:
                r = r + add_ref[...]
            if rows is None:
                out_refs[0][...] = r.astype(out_dtype)
                return
            vals = rows.fn(r, [t[...] for t in tile_refs], [v[...] for v in vec_refs])
            first = pl.program_id(0) == 0
            for (kind, _), o_ref, val in zip(rows.outs, out_refs, vals):
                if kind == "tile":
                    o_ref[...] = val.astype(o_ref.dtype)
                else:
                    @pl.when(first)
                    def _(o_ref=o_ref, val=val):
                        o_ref[...] = val

                    @pl.when(jnp.logical_not(first))
                    def _(o_ref=o_ref, val=val):
                        o_ref[...] += val

        if shards_per_step > 1:
            part = sum(lax.dot_general(a_ref[:, s * n_sh:(s + 1) * n_sh], b_ref[s], dn, preferred_element_type=F32)
                       for s in range(shards_per_step))
        else:
            part = lax.dot_general(a_ref[...], b_ref[...], dn, preferred_element_type=F32)
        if nk == 1:
            finish(part)
        else:
            k = pl.program_id(2)

            @pl.when(k == 0)
            def _():
                acc_ref[...] = part

            @pl.when(k > 0)
            def _():
                acc_ref[...] += part

            @pl.when(k == nk - 1)
            def _():
                finish(acc_ref[...])

        if n_ex:
            @pl.when(step == grid[0] * grid[1] * grid[2] - 1)
            def _():
                ex.finish()

    a_spec = pl.BlockSpec((tk, tm), lambda i, j, k: (k, i)) if ta else pl.BlockSpec((tm, tk), lambda i, j, k: (i, k))
    if b_stacked and tb:
        off = b_layer * (w_rows // tn)
        b_spec = pl.BlockSpec((shards_per_step, tn, n_sh), lambda i, j, k: (k, off + j, 0))
    elif b_stacked:
        off = b_layer * (w_rows // tk)
        b_spec = pl.BlockSpec((None, tk, tn), lambda i, j, k: (j, off + k, 0))
    elif tb:
        b_spec = pl.BlockSpec((tn, tk), lambda i, j, k: (j, k))
    else:
        b_spec = pl.BlockSpec((tk, tn), lambda i, j, k: (k, j))
    if out_stacked:
        out_spec = pl.BlockSpec((None, tm, tn), lambda i, j, k: (j, i, 0))
        out_shape = jax.ShapeDtypeStruct((N_CHIPS, M, tn), out_dtype)
    else:
        out_spec = pl.BlockSpec((tm, tn), lambda i, j, k: (i, j))
        out_shape = jax.ShapeDtypeStruct((M, N), out_dtype)
    in_specs, args = [a_spec, b_spec], [a, b]
    if has_bias:
        in_specs.append(pl.BlockSpec((1, tn), lambda i, j, k: (0, j)))
        args.append(bias)
    if has_add:
        in_specs.append(pl.BlockSpec((tm, tn), lambda i, j, k: (i, j)))
        args.append(add)
    semantics = ("parallel", "parallel", "arbitrary")
    if rows is not None:
        tile_spec = pl.BlockSpec((tm, N), lambda i, j, k: (i, 0))
        vec_spec = pl.BlockSpec((1, N), lambda i, j, k: (0, 0))
        in_specs += [tile_spec] * n_tiles + [vec_spec] * n_vecs
        args += rows.tiles + rows.vecs
        out_spec = [tile_spec if kind == "tile" else vec_spec for kind, _ in rows.outs]
        out_shape = [jax.ShapeDtypeStruct((M, N) if kind == "tile" else (1, N), dt) for kind, dt in rows.outs]
        semantics = ("arbitrary", "arbitrary", "arbitrary")
    scratch = [pltpu.VMEM((tm, tn), F32)] if nk > 1 else []
    if n_ex:
        any_spec = pl.BlockSpec(memory_space=pl.ANY)
        in_specs += [any_spec] * n_ex
        args += list(exchange)
        out_spec = out_spec + [any_spec] * n_ex
        out_shape = out_shape + _Exchange.out_shape(exchange)
        scratch += _Exchange.semaphores(n_ex)
    return pl.pallas_call(
        body, name=name,
        grid=(M // tm, N // tn, nk),
        in_specs=in_specs,
        out_specs=out_spec,
        out_shape=out_shape,
        scratch_shapes=scratch,
        compiler_params=_cparams(semantics),
    )(*args)


ROW_TILE = 512


def _rms_fwd(x, g, shards, *, out_dtype, name):
    S, D = x.shape
    tm = _pick(S, (ROW_TILE, 256, 128))
    nsh, nt = len(shards), S // tm
    any_spec = pl.BlockSpec(memory_space=pl.ANY)

    def body(x_ref, g_ref, *rest):
        o_ref = rest[nsh]
        gather = _Gather(rest[:nsh], rest[nsh + 1:2 * nsh + 1], *rest[2 * nsh + 1:])

        @pl.when(pl.program_id(0) == 0)
        def _():
            gather.issue()

        o_ref[...] = _rms(x_ref[...], g_ref[...]).astype(out_dtype)

        @pl.when(pl.program_id(0) == nt - 1)
        def _():
            gather.finish()

    row = pl.BlockSpec((tm, D), lambda i: (i, 0))
    return pl.pallas_call(
        body, name=name, grid=(nt,),
        in_specs=[row, pl.BlockSpec((1, D), lambda i: (0, 0))] + [any_spec] * nsh, out_specs=[row] + [any_spec] * nsh,
        out_shape=[jax.ShapeDtypeStruct((S, D), out_dtype)] + _Gather.out_shape(shards),
        scratch_shapes=_Gather.semaphores(nsh),
        compiler_params=_cparams(("arbitrary",)),
    )(x, g, *shards)


def _rms(x, g):
    return x * lax.rsqrt(jnp.mean(x * x, axis=-1, keepdims=True) + EPS) * g


def _rms_vjp(x, g, dy):
    r = lax.rsqrt(jnp.mean(x * x, axis=-1, keepdims=True) + EPS)
    xh = x * r
    gd = dy * g
    dx = r * (gd - xh * jnp.mean(gd * xh, axis=-1, keepdims=True))
    return dx, jnp.sum(dy * xh, axis=0, keepdims=True)


def _rows_sublayer_end(res, g_post, g_pre_next):
    def fn(z, tiles, vecs):
        h = tiles[0] + _rms(z, vecs[0])
        return [z, h, _rms(h, vecs[1])]

    return _Rows([res], [g_post, g_pre_next], [("tile", F32), ("tile", F32), ("tile", MXU_DTYPE)], fn)


def _rows_last_sublayer_end(res, g_post, target):
    def fn(z, tiles, vecs):
        y = tiles[0] + _rms(z, vecs[0])
        err = y - tiles[1]
        dy = err * (1.0 / err.shape[-1])
        dz, dg = _rms_vjp(z, vecs[0], dy)
        loss = 0.5 * jnp.sum(jnp.mean(err * err, axis=-1, keepdims=True), axis=0, keepdims=True)
        return [z, dy, dz, dg, jnp.broadcast_to(loss, dg.shape)]

    return _Rows([res, target], [g_post], [("tile", F32), ("tile", F32), ("tile", MXU_DTYPE), ("vec", F32), ("vec", F32)], fn)


def _rows_sublayer_start_bwd(x, res, z_below, g_pre, g_post_below):
    def fn(dz, tiles, vecs):
        dx, dg_pre = _rms_vjp(tiles[0], vecs[0], dz)
        dh = tiles[1] + dx
        dzb, dg_post = _rms_vjp(tiles[2], vecs[1], dh)
        return [dh, dg_pre, dzb, dg_post, jnp.sum(dzb, axis=0, keepdims=True)]

    return _Rows([x, res, z_below], [g_pre, g_post_below],
                 [("tile", F32), ("vec", F32), ("tile", MXU_DTYPE), ("vec", F32), ("vec", F32)], fn)


def _rows_first_sublayer_start_bwd(x, res, g_pre):
    def fn(dz, tiles, vecs):
        dx, dg_pre = _rms_vjp(tiles[0], vecs[0], dz)
        return [tiles[1] + dx, dg_pre]

    return _Rows([x, res], [g_pre], [("tile", F32), ("vec", F32)], fn)


def _mx(t):
    return t.astype(MXU_DTYPE)


def _mxf(t):
    return t.astype(MXU_DTYPE).astype(F32)


def _rawdot(a, b, ca, cb):
    return lax.dot_general(_mx(a), _mx(b), (((ca,), (cb,)), ((), ())), preferred_element_type=F32)


@functools.partial(jax.custom_vjp, nondiff_argnums=(2, 3))
def _dot(a, b, ca, cb):
    return _rawdot(a, b, ca, cb)


def _dot_fwd(a, b, ca, cb):
    return _rawdot(a, b, ca, cb), (a, b)


def _dot_bwd(ca, cb, res, g):
    a, b = res
    bj = 1 if cb == 0 else 0
    ai = 0 if ca == 1 else 1
    da = _rawdot(g, b, 1, bj) if ca == 1 else _rawdot(b, g, bj, 1)
    db = _rawdot(a, g, ai, 0) if cb == 0 else _rawdot(g, a, 0, ai)
    return da.astype(a.dtype), db.astype(b.dtype)


_dot.defvjp(_dot_fwd, _dot_bwd)


def _softcap(z):
    return GATE_CAP * jnp.tanh(z / GATE_CAP)


def _log_sigmoid(z):
    return jnp.minimum(z, 0.0) - jnp.log(1.0 + jnp.exp(-jnp.abs(z)))


def _sigmoid(z):
    return 0.5 * jnp.tanh(0.5 * z) + 0.5


def _lane_col(t, lane_index):
    lane = lax.broadcasted_iota(jnp.int32, t.shape, 1)
    return jnp.sum(jnp.where(lane == lane_index, t, 0.0), axis=1, keepdims=True)


def _mlstm_gates(G, bias):
    L = G.shape[0]
    z = _softcap(G + bias)
    ig = z
    lf = _log_sigmoid(z)
    ti = lax.broadcasted_iota(jnp.int32, (L, L), 0)
    si = lax.broadcasted_iota(jnp.int32, (L, L), 1)
    tril = (si <= ti).astype(F32)
    b = lax.dot_general(tril, lf, (((1,), (0,)), ((), ())), precision=lax.Precision.HIGHEST, preferred_element_type=F32)
    bL = jnp.sum(lf, axis=0, keepdims=True)
    return ig, b, ig.T, b.T, bL


def _mlstm_head(h, q, k_pair, v, o, ig_all, b_all, igT, bT, bL_all, hn, C, n, m):
    L = q.shape[0]
    k = jnp.where(_own_lanes(h, k_pair.shape), k_pair, 0.0)
    ti = lax.broadcasted_iota(jnp.int32, (L, L), 0)
    si = lax.broadcasted_iota(jnp.int32, (L, L), 1)
    lower = (si <= ti)
    ig = _lane_col(ig_all, h)
    b = _lane_col(b_all, M_HEADS + h)
    ig_row = igT[h:h + 1, :]
    b_row = bT[M_HEADS + h:M_HEADS + h + 1, :]
    bL = _lane_col(bL_all, M_HEADS + h)
    inter = b + m
    dlog = jnp.where(lower, b - b_row + ig_row, -jnp.inf)
    m_t = lax.stop_gradient(jnp.maximum(inter, jnp.max(dlog, axis=-1, keepdims=True)))
    qs = q * (M_QK ** -0.5)
    w = _dot(qs, k, 1, 1) * jnp.exp(dlog - m_t)
    s_inter = jnp.exp(inter - m_t)
    num = _dot(w, v, 1, 0) + s_inter * _dot(qs, C, 1, 0)
    den = jnp.sum(w, axis=-1, keepdims=True) + s_inter * jnp.sum(_mxf(qs) * _mxf(n), axis=-1, keepdims=True)
    hout = num * (1.0 / jnp.maximum(jnp.abs(den), jnp.exp(-m_t)))
    tail = bL - b + ig
    m_new = lax.stop_gradient(jnp.maximum(bL + m, jnp.max(tail, axis=0, keepdims=True)))
    ws = jnp.exp(tail - m_new)
    decay = jnp.exp(bL + m - m_new)
    wk = ws * k
    C_new = decay * C + _dot(wk, v, 0, 0)
    n_new = decay * n + jnp.sum(_mxf(ws) * _mxf(k), axis=0, keepdims=True)
    hs = hout * lax.rsqrt(jnp.mean(hout * hout, axis=-1, keepdims=True) + EPS) * hn
    gated = _sigmoid(o) * hs
    return (gated, C_new, n_new), m_new


M_OFF_Q, M_OFF_K = 0, M_HEADS * M_QK
M_OFF_V = 2 * M_HEADS * M_QK
M_OFF_O = M_OFF_V + M_HEADS * M_V
M_OFF_G = M_OFF_O + M_HEADS * M_V
M_PROJ = M_OFF_G + LANES
M_PAIRS = M_HEADS * M_QK // LANES


def _head_cols(off, h):
    return slice(off + h * LANES, off + (h + 1) * LANES)


def _own_lanes(h, shape):
    low = lax.broadcasted_iota(jnp.int32, shape, 1) < M_QK
    return low if h % 2 == 0 else jnp.logical_not(low)


def _mlstm_specs(NC, rev):
    H, L = M_HEADS, M_CHUNK
    cc = (lambda c: NC - 1 - c) if rev else (lambda c: c)
    proj = pl.BlockSpec((L, M_PROJ), lambda c: (cc(c), 0))
    vec = pl.BlockSpec((1, LANES), lambda c: (0, 0))
    hn = pl.BlockSpec((1, H * M_V), lambda c: (0, 0))
    hv = pl.BlockSpec((L, H * M_V), lambda c: (cc(c), 0))
    Cs = pl.BlockSpec((None, H, LANES, M_V), lambda c: (cc(c), 0, 0, 0))
    ns = pl.BlockSpec((None, H, 1, LANES), lambda c: (cc(c), 0, 0, 0))
    ms = pl.BlockSpec((None, H, 1, 1), lambda c: (cc(c), 0, 0, 0))
    return proj, vec, hn, hv, Cs, ns, ms


_MLSTM_STATE = [pltpu.VMEM((M_HEADS, LANES, M_V), F32), pltpu.VMEM((M_HEADS, 1, LANES), F32),
                pltpu.VMEM((M_HEADS, 1, 1), F32)]


def _mlstm_fwd(proj, bias, hn, shards, *, name):
    S = proj.shape[0]
    H, NC = M_HEADS, S // M_CHUNK
    ps, vec, hns, hv, Cs, ns, ms = _mlstm_specs(NC, False)
    nsh = len(shards)
    any_spec = pl.BlockSpec(memory_space=pl.ANY)

    def body(p_ref, b_ref, hn_ref, *rest):
        gated_ref, C_all, n_all, m_all = rest[nsh:nsh + 4]
        C_s, n_s, m_s, send_sems, recv_sems = rest[2 * nsh + 4:]
        gather = _Gather(rest[:nsh], rest[nsh + 4:2 * nsh + 4], send_sems, recv_sems)

        @pl.when(pl.program_id(0) == 0)
        def _():
            C_s[...] = jnp.zeros_like(C_s)
            n_s[...] = jnp.zeros_like(n_s)
            m_s[...] = jnp.zeros_like(m_s)
            gather.issue()

        gate_terms = _mlstm_gates(p_ref[:, M_OFF_G:M_OFF_G + LANES], b_ref[...])
        for h in range(H):
            C, n, m = C_s[h], n_s[h], m_s[h]
            C_all[h] = C
            n_all[h] = n
            m_all[h] = m
            (gated, Cn, nn), mn = _mlstm_head(
                h, p_ref[:, _head_cols(M_OFF_Q, h // 2)], p_ref[:, _head_cols(M_OFF_K, h // 2)],
                p_ref[:, _head_cols(M_OFF_V, h)], p_ref[:, _head_cols(M_OFF_O, h)], *gate_terms,
                hn_ref[:, _head_cols(0, h)], C, n, m)
            gated_ref[:, _head_cols(0, h)] = gated.astype(gated_ref.dtype)
            C_s[h] = Cn
            n_s[h] = nn
            m_s[h] = mn

        @pl.when(pl.program_id(0) == NC - 1)
        def _():
            gather.finish()

    return pl.pallas_call(
        body, name=name, grid=(NC,),
        in_specs=[ps, vec, hns] + [any_spec] * nsh,
        out_specs=[hv, Cs, ns, ms] + [any_spec] * nsh,
        out_shape=[jax.ShapeDtypeStruct((S, H * M_V), MXU_DTYPE),
                   jax.ShapeDtypeStruct((NC, H, LANES, M_V), F32),
                   jax.ShapeDtypeStruct((NC, H, 1, LANES), F32),
                   jax.ShapeDtypeStruct((NC, H, 1, 1), F32)] + _Gather.out_shape(shards),
        scratch_shapes=list(_MLSTM_STATE) + _Gather.semaphores(nsh),
        compiler_params=_cparams(("arbitrary",)),
    )(proj, bias, hn, *shards)


def _mlstm_bwd(proj, bias, hn, C_all, n_all, m_all, dgated, qs, *, name):
    S = proj.shape[0]
    H, NC = M_HEADS, S // M_CHUNK
    ps, vec, hns, hv, Cs, ns, ms = _mlstm_specs(NC, True)
    nq = len(qs)
    any_spec = pl.BlockSpec(memory_space=pl.ANY)

    def body(p_ref, b_ref, hn_ref, C_ref, n_ref, m_ref, dg_ref, *rest):
        dp_ref, db_ref, dhn_ref = rest[nq:nq + 3]
        dC_s, dn_s, send_sems, recv_sems = rest[2 * nq + 3:]
        exchange = _Exchange(rest[:nq], rest[nq + 3:2 * nq + 3], send_sems, recv_sems)

        @pl.when(pl.program_id(0) == 0)
        def _():
            dC_s[...] = jnp.zeros_like(dC_s)
            dn_s[...] = jnp.zeros_like(dn_s)
            db_ref[...] = jnp.zeros_like(db_ref)
            dhn_ref[...] = jnp.zeros_like(dhn_ref)
            exchange.issue()

        gate_terms, gates_vjp = jax.vjp(_mlstm_gates, p_ref[:, M_OFF_G:M_OFF_G + LANES], b_ref[...])
        d_terms = [jnp.zeros_like(t) for t in gate_terms]
        for h in range(H):
            def head(q, k, v, o, *rest, h=h):
                return _mlstm_head(h, q, k, v, o, *rest, m_ref[h])

            prim = (p_ref[:, _head_cols(M_OFF_Q, h // 2)], p_ref[:, _head_cols(M_OFF_K, h // 2)],
                    p_ref[:, _head_cols(M_OFF_V, h)], p_ref[:, _head_cols(M_OFF_O, h)], *gate_terms,
                    hn_ref[:, _head_cols(0, h)], C_ref[h], n_ref[h])
            _, vjp, _ = jax.vjp(head, *prim, has_aux=True)
            dq, dk, dv, do, *d_gate, dhnh, dC, dn = vjp((dg_ref[:, _head_cols(0, h)].astype(F32), dC_s[h], dn_s[h]))
            if h % 2 == 0:
                dq_pair, dk_pair = dq, dk
            else:
                dp_ref[:, _head_cols(M_OFF_Q, h // 2)] = (dq_pair + dq).astype(dp_ref.dtype)
                dp_ref[:, _head_cols(M_OFF_K, h // 2)] = (dk_pair + dk).astype(dp_ref.dtype)
            dp_ref[:, _head_cols(M_OFF_V, h)] = dv.astype(dp_ref.dtype)
            dp_ref[:, _head_cols(M_OFF_O, h)] = do.astype(dp_ref.dtype)
            d_terms = [a + g for a, g in zip(d_terms, d_gate)]
            dhn_ref[:, _head_cols(0, h)] += dhnh
            dC_s[h] = dC
            dn_s[h] = dn
        dG, dbias = gates_vjp(tuple(d_terms))
        dp_ref[:, M_OFF_G:M_OFF_G + LANES] = dG.astype(dp_ref.dtype)
        db_ref[...] += dbias

        @pl.when(pl.program_id(0) == NC - 1)
        def _():
            exchange.finish()

    return pl.pallas_call(
        body, name=name, grid=(NC,),
        in_specs=[ps, vec, hns, Cs, ns, ms, hv] + [any_spec] * nq,
        out_specs=[ps, vec, hns] + [any_spec] * nq,
        out_shape=[jax.ShapeDtypeStruct((S, M_PROJ), MXU_DTYPE), jax.ShapeDtypeStruct((1, LANES), F32),
                   jax.ShapeDtypeStruct((1, H * M_V), F32)] + _Exchange.out_shape(qs),
        scratch_shapes=list(_MLSTM_STATE[:2]) + _Exchange.semaphores(nq),
        compiler_params=_cparams(("arbitrary",)),
    )(proj, bias, hn, C_all, n_all, m_all, dgated, *qs)


A_NQ = A_QH * A_DH
A_NKV = 2 * A_KVH * A_DH
A_PAIRS = A_G // 2


def _sink_softmax_parts(s, sink):
    mx = jnp.maximum(jnp.max(s, axis=-1, keepdims=True), sink)
    e = jnp.exp(s - mx)
    e_sink = jnp.exp(sink - mx)
    r = 1.0 / (jnp.sum(e, axis=-1, keepdims=True) + e_sink)
    return e * r, e_sink * r


@jax.custom_vjp
def _sink_softmax(s, sink):
    return _sink_softmax_parts(s, sink)[0]


def _sink_softmax_fwd(s, sink):
    p, p_sink = _sink_softmax_parts(s, sink)
    return p, (p, p_sink)


def _sink_softmax_bwd(res, dp):
    p, p_sink = res
    t = jnp.sum(dp * p, axis=-1, keepdims=True)
    return p * (dp - t), -p_sink * t


_sink_softmax.defvjp(_sink_softmax_fwd, _sink_softmax_bwd)


def _band_bias():
    qi = np.arange(A_G * A_BLK)[:, None] % A_BLK
    ku = np.arange(2 * A_BLK)[None, :]
    diff = qi - (ku - A_BLK)
    band = (diff >= 0) & (diff < A_BLK)
    seen = np.stack([band, band & (ku >= A_BLK)])
    return jnp.asarray(np.where(seen, 0.0, -np.inf), F32)


_BAND_BIAS_SPEC = pl.BlockSpec((None, A_G * A_BLK, 2 * A_BLK), lambda n: (jnp.where(n == 0, 1, 0), 0, 0))


def _attn_group(Ps, KLO, KHI, VLO, VHI, sinks, bias):
    B = Ps[0].shape[0]
    R = len(Ps) * B
    q2 = jnp.concatenate(Ps, axis=0) * (A_DH ** -0.5)
    s = jnp.concatenate([_dot(q2, KLO, 1, 1), _dot(q2, KHI, 1, 1)], axis=0) + bias
    ri = lax.broadcasted_iota(jnp.int32, (2 * R, A_G), 0)
    head = 2 * lax.shift_right_logical(ri & (R - 1), B.bit_length() - 1) + lax.shift_right_logical(ri, R.bit_length() - 1)
    onehot = head == lax.broadcasted_iota(jnp.int32, (2 * R, A_G), 1)
    sink = jnp.sum(jnp.where(onehot, sinks, 0.0), axis=1, keepdims=True)
    p = _sink_softmax(s, sink)
    o = _dot(p[:R], VLO, 1, 0) + _dot(p[R:], VHI, 1, 0)
    return tuple(o[j * B:(j + 1) * B] for j in range(len(Ps)))


def _swap_halves_of_lanes(t):
    return pltpu.roll(t, LANES // 2, 1)


def _kv_operands(kvp_ref, kvc_ref, h):
    kk = jnp.concatenate([kvp_ref[:, :LANES], kvc_ref[:, :LANES]], axis=0)
    vv = jnp.concatenate([kvp_ref[:, LANES:], kvc_ref[:, LANES:]], axis=0)
    low = lax.broadcasted_iota(jnp.int32, kk.shape, 1) < A_DH
    own = low if h == 0 else jnp.logical_not(low)
    k_own = jnp.where(own, kk, 0.0)
    v_own = jnp.where(own, vv, 0.0)
    k_oth, v_oth = _swap_halves_of_lanes(k_own), _swap_halves_of_lanes(v_own)
    if h == 0:
        return own, k_own, k_oth, v_own, v_oth
    return own, k_oth, k_own, v_oth, v_own


def _pair_cols(h, j):
    c = (h * A_PAIRS + j) * LANES
    return slice(c, c + LANES)


def _attn_fwd(proj, sinks, *, name):
    S = proj.shape[0]
    NB = S // A_BLK
    kv_blk = A_NQ // A_NKV
    qs = pl.BlockSpec((A_BLK, A_NQ), lambda n: (n, 0))
    cur = pl.BlockSpec((A_BLK, A_NKV), lambda n: (n, kv_blk))
    prev = pl.BlockSpec((A_BLK, A_NKV), lambda n: (jnp.maximum(n - 1, 0), kv_blk))
    sk = pl.BlockSpec((A_KVH, A_G), lambda n: (0, 0))

    def body(q_ref, kvp_ref, kvc_ref, s_ref, bias_ref, o_ref):
        for h in range(A_KVH):
            _, KLO, KHI, VLO, VHI = _kv_operands(kvp_ref, kvc_ref, h)
            Ps = tuple(q_ref[:, _pair_cols(h, j)] for j in range(A_PAIRS))
            outs = _attn_group(Ps, KLO, KHI, VLO, VHI, s_ref[h:h + 1, :], bias_ref[...])
            for j in range(A_PAIRS):
                o_ref[:, _pair_cols(h, j)] = outs[j].astype(o_ref.dtype)

    return pl.pallas_call(
        body, name=name, grid=(NB,),
        in_specs=[qs, prev, cur, sk, _BAND_BIAS_SPEC], out_specs=qs,
        out_shape=jax.ShapeDtypeStruct((S, A_NQ), MXU_DTYPE),
        compiler_params=_cparams(("parallel",)),
    )(proj, proj, proj, sinks, _band_bias())


def _attn_bwd(proj, sinks, do, *, name):
    S = proj.shape[0]
    NB = S // A_BLK
    last = NB - 1
    kv_blk = A_NQ // A_NKV
    qs = pl.BlockSpec((A_BLK, A_NQ), lambda n: (jnp.minimum(n, last), 0))
    cur = pl.BlockSpec((A_BLK, A_NKV), lambda n: (jnp.minimum(n, last), kv_blk))
    prev = pl.BlockSpec((A_BLK, A_NKV), lambda n: (jnp.clip(n - 1, 0, last), kv_blk))
    sk = pl.BlockSpec((A_KVH, A_G), lambda n: (0, 0))
    lag = pl.BlockSpec((A_BLK, A_NKV), lambda n: (jnp.maximum(n - 1, 0), 0))
    cq_spec = pl.BlockSpec((1, A_NQ), lambda n: (0, 0))
    ckv_spec = pl.BlockSpec((1, A_NKV), lambda n: (0, 0))

    def body(q_ref, kvp_ref, kvc_ref, s_ref, do_ref, bias_ref, dq_ref, dkv_ref, ds_ref, cq_ref, ckv_ref, keep):
        n = pl.program_id(0)

        @pl.when(n == 0)
        def _():
            keep[...] = jnp.zeros_like(keep)
            ds_ref[...] = jnp.zeros_like(ds_ref)
            cq_ref[...] = jnp.zeros_like(cq_ref)
            ckv_ref[...] = jnp.zeros_like(ckv_ref)

        @pl.when(n < NB)
        def _():
            f = functools.partial(_attn_group, bias=bias_ref[...])
            dk = jnp.zeros((2 * A_BLK, LANES), F32)
            dv = jnp.zeros((2 * A_BLK, LANES), F32)
            ds_rows = []
            for h in range(A_KVH):
                own, KLO, KHI, VLO, VHI = _kv_operands(kvp_ref, kvc_ref, h)
                Ps = tuple(q_ref[:, _pair_cols(h, j)] for j in range(A_PAIRS))
                _, vjp = jax.vjp(f, Ps, KLO, KHI, VLO, VHI, s_ref[h:h + 1, :])
                dPs, dKLO, dKHI, dVLO, dVHI, dsk = vjp(
                    tuple(do_ref[:, _pair_cols(h, j)].astype(F32) for j in range(A_PAIRS)))
                for j in range(A_PAIRS):
                    dq_ref[:, _pair_cols(h, j)] = dPs[j].astype(dq_ref.dtype)
                    cq_ref[:, _pair_cols(h, j)] += jnp.sum(dPs[j], axis=0, keepdims=True)
                dk_own, dk_oth = (dKLO, dKHI) if h == 0 else (dKHI, dKLO)
                dv_own, dv_oth = (dVLO, dVHI) if h == 0 else (dVHI, dVLO)
                dk = dk + jnp.where(own, dk_own, 0.0) + _swap_halves_of_lanes(jnp.where(own, 0.0, dk_oth))
                dv = dv + jnp.where(own, dv_own, 0.0) + _swap_halves_of_lanes(jnp.where(own, 0.0, dv_oth))
                ds_rows.append(dsk)
            ds_ref[...] += jnp.concatenate(ds_rows, axis=0)
            dkv = jnp.concatenate([dk, dv], axis=1)
            done = keep[...] + dkv[:A_BLK]
            dkv_ref[...] = done.astype(dkv_ref.dtype)
            ckv_ref[...] += jnp.sum(done, axis=0, keepdims=True)
            keep[...] = dkv[A_BLK:]

        @pl.when(n == NB)
        def _():
            done = keep[...]
            dkv_ref[...] = done.astype(dkv_ref.dtype)
            ckv_ref[...] += jnp.sum(done, axis=0, keepdims=True)

    return pl.pallas_call(
        body, name=name, grid=(NB + 1,),
        in_specs=[qs, prev, cur, sk, qs, _BAND_BIAS_SPEC],
        out_specs=[qs, lag, sk, cq_spec, ckv_spec],
        out_shape=[jax.ShapeDtypeStruct((S, A_NQ), MXU_DTYPE), jax.ShapeDtypeStruct((S, A_NKV), MXU_DTYPE),
                   jax.ShapeDtypeStruct((A_KVH, A_G), F32), jax.ShapeDtypeStruct((1, A_NQ), F32),
                   jax.ShapeDtypeStruct((1, A_NKV), F32)],
        scratch_shapes=[pltpu.VMEM((A_BLK, A_NKV), F32)],
        compiler_params=_cparams(("arbitrary",)),
    )(proj, proj, proj, sinks, do, _band_bias())


HALO = 8


def _shift_rows(t, j):
    return pltpu.roll(t, j % t.shape[0], 0)


def _conv_taps(gate_ext):
    return _shift_rows(gate_ext, 2), _shift_rows(gate_ext, 1), gate_ext


def _conv_gate(g2, g1, g0, cw, cb):
    return cb + cw[0:1, :] * g2 + cw[1:2, :] * g1 + cw[2:3, :] * g0


def _convgate_fwd(u, cw, cb, *, name):
    S, F2 = u.shape
    F = F2 // 2
    tm = _pick(S, (256, 128))
    hb = tm // HALO
    urow = pl.BlockSpec((tm, F2), lambda i: (i, 0))
    uprev = pl.BlockSpec((HALO, F), lambda i: (jnp.maximum(i * hb - 1, 0), 0))

    def body(u_ref, up_ref, cw_ref, cb_ref, a_ref):
        i = pl.program_id(0)
        prev = jnp.where(i > 0, up_ref[...], 0.0)
        gc = _conv_gate(*_conv_taps(jnp.concatenate([prev, u_ref[:, :F]], axis=0)), cw_ref[...], cb_ref[...])[HALO:]
        a_ref[...] = (gc * _sigmoid(gc) * u_ref[:, F:]).astype(a_ref.dtype)

    return pl.pallas_call(
        body, name=name, grid=(S // tm,),
        in_specs=[urow, uprev, pl.BlockSpec((3, F), lambda i: (0, 0)), pl.BlockSpec((1, F), lambda i: (0, 0))],
        out_specs=pl.BlockSpec((tm, F), lambda i: (i, 0)),
        out_shape=jax.ShapeDtypeStruct((S, F), MXU_DTYPE),
        compiler_params=_cparams(("parallel",)),
    )(u, u, cw, cb)


def _convgate_bwd(u, da, cw, cb, gs, *, name):
    S, F2 = u.shape
    ng = len(gs)
    any_spec = pl.BlockSpec(memory_space=pl.ANY)
    F = F2 // 2
    tm = _pick(S, (128,))
    hb = tm // HALO
    nt = S // tm
    nh = S // HALO
    urow = pl.BlockSpec((tm, F2), lambda i: (i, 0))
    uprev = pl.BlockSpec((HALO, F), lambda i: (jnp.maximum(i * hb - 1, 0), 0))
    unext = pl.BlockSpec((HALO, F2), lambda i: (jnp.minimum((i + 1) * hb, nh - 1), 0))
    darow = pl.BlockSpec((tm, F), lambda i: (i, 0))
    danext = pl.BlockSpec((HALO, F), lambda i: (jnp.minimum((i + 1) * hb, nh - 1), 0))

    def body(u_ref, up_ref, un_ref, da_ref, dan_ref, cw_ref, cb_ref, *rest):
        du_ref, dcw_ref, dcb_ref = rest[ng:ng + 3]
        swap = _Swap(rest[:ng], rest[ng + 3:2 * ng + 3], *rest[2 * ng + 3:]) if ng else None
        i = pl.program_id(0)
        cwv = cw_ref[...]
        prev = jnp.where(i > 0, up_ref[...], 0.0)
        gate_ext = jnp.concatenate([prev, u_ref[:, :F], un_ref[:, :F]], axis=0)
        val_ext = jnp.concatenate([u_ref[:, F:], un_ref[:, F:]], axis=0)
        da_next = jnp.where(i < nt - 1, dan_ref[...].astype(F32), 0.0)
        da_ext = jnp.concatenate([da_ref[...].astype(F32), da_next], axis=0)
        g2, g1, g0 = (t[HALO:] for t in _conv_taps(gate_ext))
        gc = _conv_gate(g2, g1, g0, cwv, cb_ref[...])
        sg = _sigmoid(gc)
        silu = gc * sg
        dval = da_ext * silu
        dgc = da_ext * val_ext * (sg * (1.0 + gc * (1.0 - sg)))
        dgate = cwv[2:3, :] * dgc + cwv[1:2, :] * _shift_rows(dgc, -1) + cwv[0:1, :] * _shift_rows(dgc, -2)
        du_ref[:, :F] = dgate[:tm].astype(du_ref.dtype)
        du_ref[:, F:] = dval[:tm].astype(du_ref.dtype)
        dgc_c = dgc[:tm]
        dcw = jnp.concatenate([jnp.sum(dgc_c * g2[:tm], axis=0, keepdims=True),
                               jnp.sum(dgc_c * g1[:tm], axis=0, keepdims=True),
                               jnp.sum(dgc_c * g0[:tm], axis=0, keepdims=True)], axis=0)
        dcb = jnp.sum(dgc_c, axis=0, keepdims=True)

        @pl.when(i == 0)
        def _():
            dcw_ref[...] = dcw
            dcb_ref[...] = dcb
            if ng:
                swap.issue()

        @pl.when(i > 0)
        def _():
            dcw_ref[...] += dcw
            dcb_ref[...] += dcb

        if ng:
            @pl.when(i == nt - 1)
            def _():
                swap.finish()

    return pl.pallas_call(
        body, name=name, grid=(nt,),
        in_specs=[urow, uprev, unext, darow, danext,
                  pl.BlockSpec((3, F), lambda i: (0, 0)), pl.BlockSpec((1, F), lambda i: (0, 0))] + [any_spec] * ng,
        out_specs=[urow, pl.BlockSpec((3, F), lambda i: (0, 0)), pl.BlockSpec((1, F), lambda i: (0, 0))] + [any_spec] * ng,
        out_shape=[jax.ShapeDtypeStruct((S, F2), MXU_DTYPE), jax.ShapeDtypeStruct((3, F), F32),
                   jax.ShapeDtypeStruct((1, F), F32)] + (_Swap.out_shape(gs) if ng else []),
        scratch_shapes=_Swap.semaphores(ng) if ng else [],
        compiler_params=_cparams(("arbitrary",)),
    )(u, u, u, da, da, cw, cb, *gs)


def _adamw_math(w, g, m, v):
    m = ADAM_B1 * m + (1.0 - ADAM_B1) * g
    v = ADAM_B2 * v + (1.0 - ADAM_B2) * (g * g)
    m_hat = m / (1.0 - ADAM_B1 ** ADAM_STEP)
    v_hat = v / (1.0 - ADAM_B2 ** ADAM_STEP)
    delta = -ADAM_LR * (m_hat / (jnp.sqrt(v_hat) + ADAM_EPS) + ADAM_WD * w)
    return delta, m, v


def _adamw_layers(w, gs, m, v, *, name):
    Lr, R, C = w.shape
    tr = _pick(R, (256, 128, 64, 32, 16, 8))
    outs = None
    for layer, g in enumerate(gs):
        def body(w_ref, g_ref, m_ref, v_ref, *rest):
            go_ref, d_ref, nm_ref, nv_ref = rest[-4:]
            gv = g_ref[...]
            d, nm, nv = _adamw_math(w_ref[...], gv, m_ref[...], v_ref[...])
            go_ref[...] = gv
            d_ref[...] = d
            nm_ref[...] = nm
            nv_ref[...] = nv

        lay = pl.BlockSpec((None, tr, C), lambda i, layer=layer: (layer, i, 0))
        in_specs = [lay, pl.BlockSpec((tr, C), lambda i: (i, 0)), lay, lay]
        args = [w, g, m, v]
        aliases = {}
        if outs is not None:
            in_specs += [pl.BlockSpec(memory_space=pl.ANY)] * 4
            args += list(outs)
            aliases = {4 + t: t for t in range(4)}
        outs = pl.pallas_call(
            body, name=f"{name}_{layer}", grid=(R // tr,), in_specs=in_specs, out_specs=[lay] * 4,
            out_shape=[jax.ShapeDtypeStruct((Lr, R, C), F32)] * 4, input_output_aliases=aliases,
            compiler_params=_cparams(("parallel",)),
        )(*args)
    return outs


def _adamw_small(items, *, name):
    n = len(items)

    def body(*refs):
        ins, outs = refs[:4 * n], refs[4 * n:]
        for t in range(n):
            w, g, m, v = (r[...] for r in ins[4 * t:4 * t + 4])
            d, nm, nv = _adamw_math(w, g, m, v)
            outs[3 * t][...] = d
            outs[3 * t + 1][...] = nm
            outs[3 * t + 2][...] = nv

    flat = [a for it in items for a in it]
    out_shape = [jax.ShapeDtypeStruct(it[0].shape, F32) for it in items for _ in range(3)]
    vm = pl.BlockSpec(memory_space=pltpu.VMEM)
    res = pl.pallas_call(body, name=name, in_specs=[vm] * len(flat), out_specs=[vm] * len(out_shape),
                         out_shape=out_shape)(*flat)
    return [tuple(res[3 * t:3 * t + 3]) for t in range(n)]


def _place():
    return lax.axis_index("x"), lax.axis_index("y"), lax.axis_index("c")


_FLIPS = ((1, 0), (0, 1), (1, 1))


ROW_ALIGN = 16


def _half(rows, which):
    return pl.ds(pl.multiple_of(which * (rows // 2), ROW_ALIGN), rows // 2)


def _remote(src, dst, send_sems, recv_sems, k, to):
    return pltpu.make_async_remote_copy(src_ref=src, dst_ref=dst, send_sem=send_sems.at[k], recv_sem=recv_sems.at[k],
                                        device_id=to, device_id_type=MESH)


class _Gather:
    def __init__(self, w_refs, out_refs, send_sems, recv_sems):
        self.w_refs, self.out_refs, self.send_sems, self.recv_sems = w_refs, out_refs, send_sems, recv_sems
        self.pairs = [(i, j) for i in range(len(w_refs)) for j in range(3)]

    @staticmethod
    def out_shape(shards):
        return [jax.ShapeDtypeStruct((N_CHIPS,) + s.shape, s.dtype) for s in shards]

    @staticmethod
    def semaphores(n):
        return [pltpu.SemaphoreType.DMA((6 * n,)), pltpu.SemaphoreType.DMA((6 * n,))]

    def _where(self):
        x, y, c = _place()
        return x, y, c, [(x ^ fx, y ^ fy) for fx, fy in _FLIPS]

    def _over_ici(self, i, j, landed):
        x, y, c, chips = self._where()
        px, py = chips[j]
        mine = _half(self.w_refs[i].shape[0], c)
        if landed:
            src = dst = self.out_refs[i].at[2 * px + py, mine]
        else:
            src, dst = self.w_refs[i].at[mine], self.out_refs[i].at[2 * x + y, mine]
        return _remote(src, dst, self.send_sems, self.recv_sems, 6 * i + j, (px, py, c))

    def _over_d2d(self, i, j, which):
        x, y, c, chips = self._where()
        px, py = chips[j]
        blk = self.out_refs[i].at[2 * px + py, _half(self.w_refs[i].shape[0], which)]
        return _remote(blk, blk, self.send_sems, self.recv_sems, 6 * i + 3 + j, (x, y, 1 - c))

    def issue(self):
        for i, j in self.pairs:
            self._over_ici(i, j, False).start()

    def finish(self):
        c = lax.axis_index("c")
        for i, j in self.pairs:
            self._over_ici(i, j, True).wait_recv()
            self._over_d2d(i, j, c).start()
        for i, j in self.pairs:
            self._over_d2d(i, j, 1 - c).wait_recv()
        for i, j in self.pairs:
            self._over_ici(i, j, False).wait_send()
            self._over_d2d(i, j, c).wait_send()


def _swap_halves(gs, *, name):
    n = len(gs)
    any_spec = pl.BlockSpec(memory_space=pl.ANY)

    def body(*refs):
        swap = _Swap(refs[:n], refs[n:2 * n], refs[2 * n], refs[2 * n + 1])
        swap.issue()
        swap.finish()

    return pl.pallas_call(
        body, name=name, in_specs=[any_spec] * n, out_specs=[any_spec] * n,
        out_shape=_Swap.out_shape(gs), scratch_shapes=_Swap.semaphores(n),
    )(*gs)


class _Swap:
    def __init__(self, g_refs, out_refs, send_sems, recv_sems):
        self.g_refs, self.out_refs, self.send_sems, self.recv_sems = g_refs, out_refs, send_sems, recv_sems

    @staticmethod
    def out_shape(gs):
        return [jax.ShapeDtypeStruct((N_CHIPS, g.shape[1] // 2, g.shape[2]), g.dtype) for g in gs]

    @staticmethod
    def semaphores(n):
        return [pltpu.SemaphoreType.DMA((n,)), pltpu.SemaphoreType.DMA((n,))]

    def _copies(self):
        x, y, c = _place()
        return [_remote(g.at[:, _half(g.shape[1], 1 - c)], out, self.send_sems, self.recv_sems, i, (x, y, 1 - c))
                for i, (g, out) in enumerate(zip(self.g_refs, self.out_refs))]

    def issue(self):
        for cp in self._copies():
            cp.start()

    def finish(self):
        for cp in self._copies():
            cp.wait()


def _add_halves(g, got, c_arr, *, name):
    _, rows, cols = g.shape
    blk = (None, rows // 2, cols)

    def body(c_ref, g_ref, got_ref, o_ref):
        o_ref[...] = (g_ref[...] + got_ref[...]).astype(o_ref.dtype)

    return pl.pallas_call(
        body, name=name,
        grid_spec=pltpu.PrefetchScalarGridSpec(
            num_scalar_prefetch=1, grid=(N_CHIPS,),
            in_specs=[pl.BlockSpec(blk, lambda j, c_ref: (j, c_ref[0], 0)), pl.BlockSpec(blk, lambda j, c_ref: (j, 0, 0))],
            out_specs=pl.BlockSpec(blk, lambda j, c_ref: (j, 0, 0))),
        out_shape=jax.ShapeDtypeStruct((N_CHIPS, rows // 2, cols), WIRE_DTYPE),
        compiler_params=_cparams(("parallel",)),
    )(c_arr, g, got)


class _Exchange:
    def __init__(self, q_refs, out_refs, send_sems, recv_sems):
        self.q_refs, self.out_refs, self.send_sems, self.recv_sems = q_refs, out_refs, send_sems, recv_sems
        self.pairs = [(i, j) for i in range(len(q_refs)) for j in range(3)]

    @staticmethod
    def out_shape(qs):
        return [jax.ShapeDtypeStruct(q.shape, q.dtype) for q in qs]

    @staticmethod
    def semaphores(n):
        return [pltpu.SemaphoreType.DMA((3 * n,)), pltpu.SemaphoreType.DMA((3 * n,))]

    def _copy(self, i, j, landed):
        x, y, c = _place()
        px, py = [(x ^ fx, y ^ fy) for fx, fy in _FLIPS][j]
        if landed:
            src = dst = self.out_refs[i].at[2 * px + py]
        else:
            src, dst = self.q_refs[i].at[2 * px + py], self.out_refs[i].at[2 * x + y]
        return _remote(src, dst, self.send_sems, self.recv_sems, 3 * i + j, (px, py, c))

    def issue(self):
        for i, j in self.pairs:
            self._copy(i, j, False).start()

    def finish(self):
        for i, j in self.pairs:
            self._copy(i, j, True).wait_recv()
        for i, j in self.pairs:
            self._copy(i, j, False).wait_send()


def _sum_chips(q, r, place_arr, *, name):
    _, h, cols = q.shape
    blk = (None, h, cols)

    def body(p_ref, q_ref, r1_ref, r2_ref, r3_ref, o_ref):
        o_ref[...] = ((q_ref[...].astype(F32) + r1_ref[...].astype(F32)) + r2_ref[...].astype(F32)) + r3_ref[...].astype(F32)

    other = [pl.BlockSpec(blk, lambda i, p_ref, f=f: (p_ref[0] ^ f, 0, 0)) for f in (1, 2, 3)]
    return pl.pallas_call(
        body, name=name,
        grid_spec=pltpu.PrefetchScalarGridSpec(
            num_scalar_prefetch=1, grid=(1,),
            in_specs=[pl.BlockSpec(blk, lambda i, p_ref: (p_ref[0], 0, 0))] + other,
            out_specs=pl.BlockSpec((h, cols), lambda i, p_ref: (p_ref[1], 0))),
        out_shape=jax.ShapeDtypeStruct((2 * h, cols), F32),
        compiler_params=_cparams(("arbitrary",)),
    )(place_arr, q, r, r, r)


def _join_halves(fs, *, name):
    n = len(fs)
    any_spec = pl.BlockSpec(memory_space=pl.ANY)

    def body(*refs):
        out_refs, send_sems, recv_sems = refs[n:2 * n], refs[2 * n], refs[2 * n + 1]
        x, y, c = _place()
        sent = []
        for i in range(n):
            mine = out_refs[i].at[_half(fs[i].shape[0], c)]
            cp = _remote(mine, mine, send_sems, recv_sems, i, (x, y, 1 - c))
            cp.start()
            sent.append(cp)
        for i in range(n):
            its = out_refs[i].at[_half(fs[i].shape[0], 1 - c)]
            _remote(its, its, send_sems, recv_sems, i, (x, y, 1 - c)).wait_recv()
        for cp in sent:
            cp.wait_send()

    return pl.pallas_call(
        body, name=name, in_specs=[any_spec] * n, out_specs=[any_spec] * n,
        out_shape=[jax.ShapeDtypeStruct(f.shape, f.dtype) for f in fs], input_output_aliases={i: i for i in range(n)},
        scratch_shapes=[pltpu.SemaphoreType.DMA((n,)), pltpu.SemaphoreType.DMA((n,))],
    )(*fs)


def _allreduce_small(buf, *, name):
    R = buf.shape[0]
    vm = pl.BlockSpec(memory_space=pltpu.VMEM)

    def body(b_ref, o_ref, slots, send_sems, recv_sems):
        x, y, c = _place()
        me = 4 * x + 2 * y + c
        slots[me] = b_ref[...]
        sends = []
        for kk in range(1, 8):
            fx, fy, fc = (kk >> 2) & 1, (kk >> 1) & 1, kk & 1
            cp = pltpu.make_async_remote_copy(
                src_ref=b_ref, dst_ref=slots.at[me], send_sem=send_sems.at[kk - 1], recv_sem=recv_sems.at[kk - 1],
                device_id=(x ^ fx, y ^ fy, c ^ fc), device_id_type=MESH)
            cp.start()
            sends.append(cp)
        for kk in range(1, 8):
            fx, fy, fc = (kk >> 2) & 1, (kk >> 1) & 1, kk & 1
            peer = 4 * (x ^ fx) + 2 * (y ^ fy) + (c ^ fc)
            pltpu.make_async_remote_copy(
                src_ref=b_ref, dst_ref=slots.at[peer], send_sem=send_sems.at[kk - 1], recv_sem=recv_sems.at[kk - 1],
                device_id=(x ^ fx, y ^ fy, c ^ fc), device_id_type=MESH).wait_recv()
        for cp in sends:
            cp.wait_send()
        acc = slots[0]
        for d in range(1, 8):
            acc = acc + slots[d]
        o_ref[...] = acc

    return pl.pallas_call(
        body, name=name, in_specs=[vm], out_specs=vm,
        out_shape=jax.ShapeDtypeStruct((R, LANES), F32),
        scratch_shapes=[pltpu.VMEM((8, R, LANES), F32), pltpu.SemaphoreType.DMA((7,)), pltpu.SemaphoreType.DMA((7,))],
    )(buf)


def _pad_rows(v, mult=8 * LANES):
    flat = v.reshape(-1)
    n = flat.shape[0]
    tot = -(-n // mult) * mult
    return jnp.pad(flat, (0, tot - n)).reshape(-1, LANES)


def _pack_small(parts):
    return jnp.concatenate([_pad_rows(p.astype(F32)) for p in parts], axis=0)


def _unpack_small(buf, shapes):
    out, r = [], 0
    for sh in shapes:
        n = math.prod(sh)
        rows = -(-n // (8 * LANES)) * 8
        out.append(buf[r:r + rows].reshape(-1)[:n].reshape(sh))
        r += rows
    return out


def _cols_to_shards(w):
    *lead, K, N = w.shape
    t = w.reshape(*lead, K, N_CHIPS, N // N_CHIPS)
    return jnp.moveaxis(t, -2, 0)


def _shards_to_cols(t):
    t = jnp.moveaxis(t, 0, -2)
    *lead, K, _, n = t.shape
    return t.reshape(*lead, K, N_CHIPS * n)


_BIG = ("m_w_in", "m_w_out", "a_w_in", "a_w_out", "f_w_up", "f_w_down")


def kernel(x, m_w_in, m_gate_bias, m_head_norm, m_w_out, a_w_in, a_b_in, a_sinks, a_w_out, a_b_out, norm_mix_pre, norm_mix_post, norm_ffn_pre, norm_ffn_post, f_w_up, f_conv_w, f_conv_b, f_w_down, loss_target, m_m_w_in, m_m_gate_bias, m_m_head_norm, m_m_w_out, m_a_w_in, m_a_b_in, m_a_sinks, m_a_w_out, m_a_b_out, m_norm_mix_pre, m_norm_mix_post, m_norm_ffn_pre, m_norm_ffn_post, m_f_w_up, m_f_conv_w, m_f_conv_b, m_f_w_down, v_m_w_in, v_m_gate_bias, v_m_head_norm, v_m_w_out, v_a_w_in, v_a_b_in, v_a_sinks, v_a_w_out, v_a_b_out, v_norm_mix_pre, v_norm_mix_post, v_norm_ffn_pre, v_norm_ffn_post, v_f_w_up, v_f_conv_w, v_f_conv_b, v_f_w_down):
    params = dict(m_w_in=m_w_in, m_gate_bias=m_gate_bias, m_head_norm=m_head_norm, m_w_out=m_w_out, a_w_in=a_w_in,
                  a_b_in=a_b_in, a_sinks=a_sinks, a_w_out=a_w_out, a_b_out=a_b_out, norm_mix_pre=norm_mix_pre,
                  norm_mix_post=norm_mix_post, norm_ffn_pre=norm_ffn_pre, norm_ffn_post=norm_ffn_post, f_w_up=f_w_up,
                  f_conv_w=f_conv_w, f_conv_b=f_conv_b, f_w_down=f_w_down)
    mom1 = dict(m_w_in=m_m_w_in, m_gate_bias=m_m_gate_bias, m_head_norm=m_m_head_norm, m_w_out=m_m_w_out,
                a_w_in=m_a_w_in, a_b_in=m_a_b_in, a_sinks=m_a_sinks, a_w_out=m_a_w_out, a_b_out=m_a_b_out,
                norm_mix_pre=m_norm_mix_pre, norm_mix_post=m_norm_mix_post, norm_ffn_pre=m_norm_ffn_pre,
                norm_ffn_post=m_norm_ffn_post, f_w_up=m_f_w_up, f_conv_w=m_f_conv_w, f_conv_b=m_f_conv_b,
                f_w_down=m_f_w_down)
    mom2 = dict(m_w_in=v_m_w_in, m_gate_bias=v_m_gate_bias, m_head_norm=v_m_head_norm, m_w_out=v_m_w_out,
                a_w_in=v_a_w_in, a_b_in=v_a_b_in, a_sinks=v_a_sinks, a_w_out=v_a_w_out, a_b_out=v_a_b_out,
                norm_mix_pre=v_norm_mix_pre, norm_mix_post=v_norm_mix_post, norm_ffn_pre=v_norm_ffn_pre,
                norm_ffn_post=v_norm_ffn_post, f_w_up=v_f_w_up, f_conv_w=v_f_conv_w, f_conv_b=v_f_conv_b,
                f_w_down=v_f_w_down)
    order = list(params)

    mx, my, mc = _place()
    chip = 2 * mx + my
    h0 = x[0]
    target = loss_target[0]

    def two_d(t):
        return t.reshape(-1, t.shape[-1])

    early = ("m_w_in",)
    late = tuple(n for n in _BIG if n not in early)
    mine = {n: two_d(params[n]).astype(MXU_DTYPE) for n in _BIG}

    def with_own_slot(names, theirs):
        return {n: lax.dynamic_update_slice(t, mine[n][None], (chip, 0, 0)) for n, t in zip(names, theirs)}

    z0, *early_theirs = _rms_fwd(h0, norm_mix_pre[0:1], [mine[n] for n in early], out_dtype=MXU_DTYPE,
                                 name="mix_pre_norm0")
    gathered = with_own_slot(early, early_theirs)

    def in_place(shard, axis):
        width = shard.shape[axis]
        z = jnp.zeros(shard.shape[:axis] + (N_CHIPS * width,) + shard.shape[axis + 1:], F32)
        contrib = jnp.where(mc == 0, shard, 0.0)
        return lax.dynamic_update_slice_in_dim(z, contrib, chip * width, axis)

    sm_in = [in_place(a_b_in, 1), in_place(a_b_out, 1), in_place(f_conv_w, 2)]
    sm_full = _unpack_small(_allreduce_small(_pack_small(sm_in), name="gather_small"), [t.shape for t in sm_in])
    b_in_full, b_out_full, conv_w_full = sm_full

    W_in = _shards_to_cols(gathered["m_w_in"])
    W_all = jnp.pad(W_in, ((0, 0), (0, M_PROJ - W_in.shape[1])))
    gbias = jnp.pad(m_gate_bias[0].reshape(1, 2 * M_HEADS), ((0, 0), (0, LANES - 2 * M_HEADS)))

    grads = {}

    def ffn_fwd(i, z1, end):
        u = _mm(z1, W_up, b_layer=i, name=f"ffn_up{i}")
        a = _convgate_fwd(u, conv_w_full[i], f_conv_b[i:i + 1], name=f"ffn_act{i}")
        return u, a, _mm(a, W_down[i], rows=end, name=f"ffn_down{i}")

    proj = _mm(z0, W_all, name="mlstm_proj")
    gated, C_all, n_all, m_all, *late_theirs = _mlstm_fwd(proj, gbias, m_head_norm, [mine[n] for n in late],
                                                          name="mlstm_fwd")
    gathered.update(with_own_slot(late, late_theirs))
    W_mout = gathered["m_w_out"].reshape(D_MODEL, D_MODEL)
    A_in, A_out = _shards_to_cols(gathered["a_w_in"]), gathered["a_w_out"].reshape(D_MODEL, D_MODEL)
    W_up = gathered["f_w_up"]
    dsh = D_FF // N_CHIPS
    W_down = [gathered["f_w_down"][:, i * dsh:(i + 1) * dsh].reshape(D_FF, D_MODEL) for i in range(2)]
    zm0, h1, z1 = _mm(gated, W_mout, rows=_rows_sublayer_end(h0, norm_mix_post[0:1], norm_ffn_pre[0:1]), name="mlstm_out")
    u0, a0, (zf0, h2, y0) = ffn_fwd(0, z1, _rows_sublayer_end(h1, norm_ffn_post[0:1], norm_mix_pre[1:2]))

    aproj = _mm(y0, A_in, bias=b_in_full, name="attn_proj")
    sinks = a_sinks.reshape(A_KVH, A_G)
    ao = _attn_fwd(aproj, sinks, name="attn_fwd")
    zm1, h3, z3 = _mm(ao, A_out, bias=b_out_full, rows=_rows_sublayer_end(h2, norm_mix_post[1:2], norm_ffn_pre[1:2]),
                      name="attn_out")
    u1, a1, (zf1, dh, dzf1, g_post1, loss_lanes) = ffn_fwd(1, z3, _rows_last_sublayer_end(h3, norm_ffn_post[1:2], target))
    loss = lax.psum(loss_lanes[0, 0], ("x", "y", "c"))

    g_post, g_fpre, g_mpost, g_mpre = [None, g_post1], [None, None], [None, None], [None, None]
    dW_up, dW_down, dconv_w, dconv_b = [None, None], [None, None], [None, None], [None, None]

    def by_rows(g):
        return g.reshape(N_CHIPS, -1, g.shape[-1])

    def ffn_bwd(i, dzf, z1_, u, a, start, ride):
        da = _mm(dzf, W_down[i], tb=True, name=f"ffn_down_dx{i}")
        dW_down[i] = _mm(a, dzf, ta=True, name=f"ffn_down_dw{i}")
        riders = (list(ride) + [by_rows(dW_down[i])]) if ride is not None else []
        du, dconv_w[i], dconv_b[i], *got = _convgate_bwd(u, da, conv_w_full[i], f_conv_b[i:i + 1], riders,
                                                         name=f"ffn_act_bwd{i}")
        dW_up[i] = _mm(z1_, du, ta=True, out_stacked=True, name=f"ffn_up_dw{i}")
        return _mm(du, W_up, tb=True, b_layer=i, rows=start, name=f"ffn_up_dx{i}"), riders, got

    (dh3, g_fpre[1], dzm1, g_mpost[1], db_out), _, _ = ffn_bwd(
        1, dzf1, z3, u1, a1, _rows_sublayer_start_bwd(h3, dh, zm1, norm_ffn_pre[1:2], norm_mix_post[1:2]), None)
    dao = _mm(dzm1, A_out, tb=True, name="attn_out_dx")
    dA_out = _mm(ao, dzm1, ta=True, name="attn_out_dw")
    daq, dakv, dsinks, cs_q, cs_kv = _attn_bwd(aproj, sinks, dao, name="attn_bwd")
    db_in = jnp.concatenate([cs_q, cs_kv], axis=1)
    dh2, g_mpre[1], dzf0, g_post[0], _ = _mm(
        daq, A_in[:, :A_NQ], tb=True, add=_mm(dakv, A_in[:, A_NQ:], tb=True, name="attn_proj_kv_dx"),
        rows=_rows_sublayer_start_bwd(h2, dh3, zf0, norm_mix_pre[1:2], norm_ffn_post[0:1]), name="attn_proj_q_dx")
    dA_in = jnp.concatenate([_mm(y0, daq, ta=True, name="attn_proj_q_dw"),
                             _mm(y0, dakv, ta=True, name="attn_proj_kv_dw")], axis=1)

    c_arr = jnp.reshape(mc, (1,)).astype(jnp.int32)
    place_arr = jnp.stack([chip, mc]).astype(jnp.int32)

    def add_halves(tags, gs, got):
        return [_add_halves(g, t, c_arr, name=f"grad_add_halves_{tag}") for tag, g, t in zip(tags, gs, got)]

    (dh1, g_fpre[0], dzm0, g_mpost[0], _), rode, rode_got = ffn_bwd(
        0, dzf0, z1, u0, a0, _rows_sublayer_start_bwd(h1, dh2, zm0, norm_ffn_pre[0:1], norm_mix_post[0:1]),
        [_cols_to_shards(dA_in), by_rows(dA_out), dW_up[1], by_rows(dW_down[1])])
    rode_tags = ["a_w_in", "a_w_out", "f_w_up1", "f_w_down1", "f_w_down0"]
    dgated = _mm(dzm0, W_mout, tb=True, name="mlstm_out_dx")
    dW_mout = _mm(gated, dzm0, ta=True, name="mlstm_out_dw")
    next_tags, next_gs = ["f_w_up0", "m_w_out"], [dW_up[0], by_rows(dW_mout)]
    late_tags = rode_tags + next_tags
    late_part = (add_halves(rode_tags, rode, rode_got)
                 + add_halves(next_tags, next_gs, _swap_halves(next_gs, name="grad_swap_halves_late")))
    dproj, dgbias, dhn, *late_from = _mlstm_bwd(proj, gbias, m_head_norm, C_all, n_all, m_all, dgated, late_part,
                                                name="mlstm_bwd")
    dW_all = _mm(z0, dproj, ta=True, name="mlstm_proj_dw")
    dW_min = dW_all[:, :m_w_in.shape[-1] * N_CHIPS]

    early_tags, early_gs = ["m_w_in"], [_cols_to_shards(dW_min)]
    early_part = add_halves(early_tags, early_gs, _swap_halves(early_gs, name="grad_swap_halves_early"))
    grad_x, g_mpre[0], *early_from = _mm(
        dproj, W_all, tb=True, rows=_rows_first_sublayer_start_bwd(h0, dh1, norm_mix_pre[0:1]), exchange=early_part,
        name="mlstm_proj_dx")
    tags = early_tags + late_tags
    halves = [_sum_chips(q, r, place_arr, name=f"grad_sum_chips_{tag}")
              for tag, q, r in zip(tags, early_part + late_part, list(early_from) + list(late_from))]
    reduced = dict(zip(tags, _join_halves(halves, name="grad_join_halves")))
    layer_grads = dict(m_w_in=[reduced["m_w_in"]], m_w_out=[reduced["m_w_out"]], a_w_in=[reduced["a_w_in"]],
                       a_w_out=[reduced["a_w_out"]], f_w_up=[reduced["f_w_up0"], reduced["f_w_up1"]],
                       f_w_down=[reduced["f_w_down0"], reduced["f_w_down1"]])

    small_g = [
        dgbias[:, :2 * M_HEADS].reshape(1, 2, M_HEADS),
        dhn,
        dsinks.reshape(1, A_QH),
        db_in, db_out,
        jnp.concatenate(g_mpre), jnp.concatenate(g_mpost), jnp.concatenate(g_fpre), jnp.concatenate(g_post),
        jnp.stack(dconv_w), jnp.concatenate(dconv_b),
    ]
    small_names = ["m_gate_bias", "m_head_norm", "a_sinks", "a_b_in", "a_b_out", "norm_mix_pre", "norm_mix_post",
                   "norm_ffn_pre", "norm_ffn_post", "f_conv_w", "f_conv_b"]
    red = _unpack_small(_allreduce_small(_pack_small(small_g), name="reduce_small"), [t.shape for t in small_g])
    for n, t in zip(small_names, red):
        if n in ("a_b_in", "a_b_out", "f_conv_w"):
            axis = t.ndim - 1
            width = params[n].shape[axis]
            t = lax.dynamic_slice_in_dim(t, chip * width, width, axis)
        grads[n] = t

    deltas, new_m, new_v = {}, {}, {}
    for n in _BIG:
        grads[n], deltas[n], new_m[n], new_v[n] = _adamw_layers(params[n], layer_grads[n], mom1[n], mom2[n],
                                                                name=f"adamw_{n}")
    two = lambda t: t.reshape(-1, t.shape[-1])
    res = _adamw_small([(two(params[n]), two(grads[n]), two(mom1[n]), two(mom2[n])) for n in small_names],
                       name="adamw_small")
    for n, (d, nm, nv) in zip(small_names, res):
        sh = params[n].shape
        deltas[n], new_m[n], new_v[n] = d.reshape(sh), nm.reshape(sh), nv.reshape(sh)

    return (loss, grad_x[None], *[grads[n] for n in order], *[deltas[n] for n in order],
            *[new_m[n] for n in order], *[new_v[n] for n in order])
```

```python
import functools
import math

import numpy as np

import jax
import jax.numpy as jnp
from jax import lax
from jax.experimental import pallas as pl
from jax.experimental.pallas import tpu as pltpu

F32 = jnp.float32
MXU_DTYPE = jnp.bfloat16
WIRE_DTYPE = jnp.bfloat16
MESH = pl.DeviceIdType.MESH

D_MODEL = 1024
EPS = 1e-6
M_HEADS, M_QK, M_V, M_CHUNK = 8, 64, 128, 128
GATE_CAP = 15.0
A_DH, A_QH, A_KVH, A_G, A_BLK = 64, 16, 2, 8, 128
D_FF = 2816
N_CHIPS = 4
LANES = 128
VMEM_LIMIT = 56 * 1024 * 1024

ADAM_LR, ADAM_B1, ADAM_B2, ADAM_EPS, ADAM_WD, ADAM_STEP = 0.001, 0.9, 0.999, 1e-08, 0.01, 10


def _cparams(sem):
    return pltpu.CompilerParams(dimension_semantics=sem, vmem_limit_bytes=VMEM_LIMIT)


def _pick(n, cands):
    for c in cands:
        if n % c == 0:
            return c
    return n


class _Rows:
    def __init__(self, tiles, vecs, outs, fn):
        self.tiles, self.vecs, self.outs, self.fn = list(tiles), list(vecs), list(outs), fn


ROWS_TILE = 512
ROWS_FULL_K = 4224
WIDE_N = 3200


def _mm(a, b, *, ta=False, tb=False, out_dtype=F32, bias=None, add=None, b_layer=None, out_stacked=False, rows=None,
        exchange=(), name):
    if ta:
        K, M = a.shape
    else:
        M, K = a.shape
    b_stacked = b_layer is not None
    if b_stacked:
        assert not ta
        n_sh = b.shape[2]
        w_rows, w_cols = D_MODEL, N_CHIPS * n_sh
        N, Kb = (w_rows, w_cols) if tb else (w_cols, w_rows)
    elif tb:
        N, Kb = b.shape
    else:
        Kb, N = b.shape
    assert K == Kb, (a.shape, b.shape)
    tm = _pick(M, (2048, 1024, 1408, 512, 256, 128) if not ta and rows is None else (1024, 1408, 512, 256, 128))
    tn = _pick(N, (1024, 1408, 1280, 640, 512, 256, 128))
    tk = K if K <= 2816 else _pick(K, (2816, 2048, 1408, 1024, 512, 256, 128))
    if tn < 1024 <= N <= WIDE_N:
        tn, tm, tk = N, min(tm, 512), min(tk, 1024)
    shards_per_step = 1
    if b_stacked and tb:
        shards_per_step = 2
        tk = shards_per_step * n_sh
    if (b_stacked and not tb) or out_stacked:
        tn = N // N_CHIPS
    if rows is not None:
        assert tn == N and not out_stacked and not ta
        tm = min(tm, ROWS_TILE)
        if K <= ROWS_FULL_K:
            tk = K
    nk = K // tk
    dn = (((0 if ta else 1,), (1 if tb else 0,)), ((), ()))
    has_bias, has_add = bias is not None, add is not None
    n_tiles, n_vecs, n_outs = (len(rows.tiles), len(rows.vecs), len(rows.outs)) if rows is not None else (0, 0, 1)
    n_ex = len(exchange)
    assert not n_ex or rows is not None
    grid = (M // tm, N // tn, nk)

    def body(*refs):
        a_ref, b_ref = refs[0], refs[1]
        pos = 2
        bias_ref = add_ref = None
        if has_bias:
            bias_ref = refs[pos]
            pos += 1
        if has_add:
            add_ref = refs[pos]
            pos += 1
        tile_refs, vec_refs = refs[pos:pos + n_tiles], refs[pos + n_tiles:pos + n_tiles + n_vecs]
        pos += n_tiles + n_vecs
        ex_in = refs[pos:pos + n_ex]
        pos += n_ex
        out_refs, ex_out = refs[pos:pos + n_outs], refs[pos + n_outs:pos + n_outs + n_ex]
        pos += n_outs + n_ex
        acc_ref = refs[pos] if nk > 1 else None
        if n_ex:
            ex = _Exchange(ex_in, ex_out, *refs[pos + (nk > 1):])
            step = (pl.program_id(0) * grid[1] + pl.program_id(1)) * grid[2] + pl.program_id(2)

            @pl.when(step == 0)
            def _():
                ex.issue()

        def finish(r):
            if has_bias:
                r = r + bias_ref[...]
            if has_add:
                r = r + add_ref[...]
            if rows is None:
                out_refs[0][...] = r.astype(out_dtype)
                return
            vals = rows.fn(r, [t[...] for t in tile_refs], [v[...] for v in vec_refs])
            first = pl.program_id(0) == 0
            for (kind, _), o_ref, val in zip(rows.outs, out_refs, vals):
                if kind == "tile":
                    o_ref[...] = val.astype(o_ref.dtype)
                else:
                    @pl.when(first)
                    def _(o_ref=o_ref, val=val):
                        o_ref[...] = val

                    @pl.when(jnp.logical_not(first))
                    def _(o_ref=o_ref, val=val):
                        o_ref[...] += val

        if shards_per_step > 1:
            part = sum(lax.dot_general(a_ref[:, s * n_sh:(s + 1) * n_sh], b_ref[s], dn, preferred_element_type=F32)
                       for s in range(shards_per_step))
        else:
            part = lax.dot_general(a_ref[...], b_ref[...], dn, preferred_element_type=F32)
        if nk == 1:
            finish(part)
        else:
            k = pl.program_id(2)

            @pl.when(k == 0)
            def _():
                acc_ref[...] = part

            @pl.when(k > 0)
            def _():
                acc_ref[...] += part

            @pl.when(k == nk - 1)
            def _():
                finish(acc_ref[...])

        if n_ex:
            @pl.when(step == grid[0] * grid[1] * grid[2] - 1)
            def _():
                ex.finish()

    a_spec = pl.BlockSpec((tk, tm), lambda i, j, k: (k, i)) if ta else pl.BlockSpec((tm, tk), lambda i, j, k: (i, k))
    if b_stacked and tb:
        off = b_layer * (w_rows // tn)
        b_spec = pl.BlockSpec((shards_per_step, tn, n_sh), lambda i, j, k: (k, off + j, 0))
    elif b_stacked:
        off = b_layer * (w_rows // tk)
        b_spec = pl.BlockSpec((None, tk, tn), lambda i, j, k: (j, off + k, 0))
    elif tb:
        b_spec = pl.BlockSpec((tn, tk), lambda i, j, k: (j, k))
    else:
        b_spec = pl.BlockSpec((tk, tn), lambda i, j, k: (k, j))
    if out_stacked:
        out_spec = pl.BlockSpec((None, tm, tn), lambda i, j, k: (j, i, 0))
        out_shape = jax.ShapeDtypeStruct((N_CHIPS, M, tn), out_dtype)
    else:
        out_spec = pl.BlockSpec((tm, tn), lambda i, j, k: (i, j))
        out_shape = jax.ShapeDtypeStruct((M, N), out_dtype)
    in_specs, args = [a_spec, b_spec], [a, b]
    if has_bias:
        in_specs.append(pl.BlockSpec((1, tn), lambda i, j, k: (0, j)))
        args.append(bias)
    if has_add:
        in_specs.append(pl.BlockSpec((tm, tn), lambda i, j, k: (i, j)))
        args.append(add)
    semantics = ("parallel", "parallel", "arbitrary")
    if rows is not None:
        tile_spec = pl.BlockSpec((tm, N), lambda i, j, k: (i, 0))
        vec_spec = pl.BlockSpec((1, N), lambda i, j, k: (0, 0))
        in_specs += [tile_spec] * n_tiles + [vec_spec] * n_vecs
        args += rows.tiles + rows.vecs
        out_spec = [tile_spec if kind == "tile" else vec_spec for kind, _ in rows.outs]
        out_shape = [jax.ShapeDtypeStruct((M, N) if kind == "tile" else (1, N), dt) for kind, dt in rows.outs]
        semantics = ("arbitrary", "arbitrary", "arbitrary")
    scratch = [pltpu.VMEM((tm, tn), F32)] if nk > 1 else []
    if n_ex:
        any_spec = pl.BlockSpec(memory_space=pl.ANY)
        in_specs += [any_spec] * n_ex
        args += list(exchange)
        out_spec = out_spec + [any_spec] * n_ex
        out_shape = out_shape + _Exchange.out_shape(exchange)
        scratch += _Exchange.semaphores(n_ex)
    return pl.pallas_call(
        body, name=name,
        grid=(M // tm, N // tn, nk),
        in_specs=in_specs,
        out_specs=out_spec,
        out_shape=out_shape,
        scratch_shapes=scratch,
        compiler_params=_cparams(semantics),
    )(*args)


ROW_TILE = 512


def _rms_fwd(x, g, shards, *, out_dtype, name):
    S, D = x.shape
    tm = _pick(S, (ROW_TILE, 256, 128))
    nsh, nt = len(shards), S // tm
    any_spec = pl.BlockSpec(memory_space=pl.ANY)

    def body(x_ref, g_ref, *rest):
        o_ref = rest[nsh]
        gather = _Gather(rest[:nsh], rest[nsh + 1:2 * nsh + 1], *rest[2 * nsh + 1:])

        @pl.when(pl.program_id(0) == 0)
        def _():
            gather.issue()

        o_ref[...] = _rms(x_ref[...], g_ref[...]).astype(out_dtype)

        @pl.when(pl.program_id(0) == nt - 1)
        def _():
            gather.finish()

    row = pl.BlockSpec((tm, D), lambda i: (i, 0))
    return pl.pallas_call(
        body, name=name, grid=(nt,),
        in_specs=[row, pl.BlockSpec((1, D), lambda i: (0, 0))] + [any_spec] * nsh, out_specs=[row] + [any_spec] * nsh,
        out_shape=[jax.ShapeDtypeStruct((S, D), out_dtype)] + _Gather.out_shape(shards),
        scratch_shapes=_Gather.semaphores(nsh),
        compiler_params=_cparams(("arbitrary",)),
    )(x, g, *shards)


def _rms(x, g):
    return x * lax.rsqrt(jnp.mean(x * x, axis=-1, keepdims=True) + EPS) * g


def _rms_vjp(x, g, dy):
    r = lax.rsqrt(jnp.mean(x * x, axis=-1, keepdims=True) + EPS)
    xh = x * r
    gd = dy * g
    dx = r * (gd - xh * jnp.mean(gd * xh, axis=-1, keepdims=True))
    return dx, jnp.sum(dy * xh, axis=0, keepdims=True)


def _rows_sublayer_end(res, g_post, g_pre_next):
    def fn(z, tiles, vecs):
        h = tiles[0] + _rms(z, vecs[0])
        return [z, h, _rms(h, vecs[1])]

    return _Rows([res], [g_post, g_pre_next], [("tile", F32), ("tile", F32), ("tile", MXU_DTYPE)], fn)


def _rows_last_sublayer_end(res, g_post, target):
    def fn(z, tiles, vecs):
        y = tiles[0] + _rms(z, vecs[0])
        err = y - tiles[1]
        dy = err * (1.0 / err.shape[-1])
        dz, dg = _rms_vjp(z, vecs[0], dy)
        loss = 0.5 * jnp.sum(jnp.mean(err * err, axis=-1, keepdims=True), axis=0, keepdims=True)
        return [z, dy, dz, dg, jnp.broadcast_to(loss, dg.shape)]

    return _Rows([res, target], [g_post], [("tile", F32), ("tile", F32), ("tile", MXU_DTYPE), ("vec", F32), ("vec", F32)], fn)


def _rows_sublayer_start_bwd(x, res, z_below, g_pre, g_post_below):
    def fn(dz, tiles, vecs):
        dx, dg_pre = _rms_vjp(tiles[0], vecs[0], dz)
        dh = tiles[1] + dx
        dzb, dg_post = _rms_vjp(tiles[2], vecs[1], dh)
        return [dh, dg_pre, dzb, dg_post, jnp.sum(dzb, axis=0, keepdims=True)]

    return _Rows([x, res, z_below], [g_pre, g_post_below],
                 [("tile", F32), ("vec", F32), ("tile", MXU_DTYPE), ("vec", F32), ("vec", F32)], fn)


def _rows_first_sublayer_start_bwd(x, res, g_pre):
    def fn(dz, tiles, vecs):
        dx, dg_pre = _rms_vjp(tiles[0], vecs[0], dz)
        return [tiles[1] + dx, dg_pre]

    return _Rows([x, res], [g_pre], [("tile", F32), ("vec", F32)], fn)


def _mx(t):
    return t.astype(MXU_DTYPE)


def _mxf(t):
    return t.astype(MXU_DTYPE).astype(F32)


def _rawdot(a, b, ca, cb):
    return lax.dot_general(_mx(a), _mx(b), (((ca,), (cb,)), ((), ())), preferred_element_type=F32)


@functools.partial(jax.custom_vjp, nondiff_argnums=(2, 3))
def _dot(a, b, ca, cb):
    return _rawdot(a, b, ca, cb)


def _dot_fwd(a, b, ca, cb):
    return _rawdot(a, b, ca, cb), (a, b)


def _dot_bwd(ca, cb, res, g):
    a, b = res
    bj = 1 if cb == 0 else 0
    ai = 0 if ca == 1 else 1
    da = _rawdot(g, b, 1, bj) if ca == 1 else _rawdot(b, g, bj, 1)
    db = _rawdot(a, g, ai, 0) if cb == 0 else _rawdot(g, a, 0, ai)
    return da.astype(a.dtype), db.astype(b.dtype)


_dot.defvjp(_dot_fwd, _dot_bwd)


def _softcap(z):
    return GATE_CAP * jnp.tanh(z / GATE_CAP)


def _log_sigmoid(z):
    return jnp.minimum(z, 0.0) - jnp.log(1.0 + jnp.exp(-jnp.abs(z)))


def _sigmoid(z):
    return 0.5 * jnp.tanh(0.5 * z) + 0.5


def _lane_col(t, lane_index):
    lane = lax.broadcasted_iota(jnp.int32, t.shape, 1)
    return jnp.sum(jnp.where(lane == lane_index, t, 0.0), axis=1, keepdims=True)


def _mlstm_gates(G, bias):
    L = G.shape[0]
    z = _softcap(G + bias)
    ig = z
    lf = _log_sigmoid(z)
    ti = lax.broadcasted_iota(jnp.int32, (L, L), 0)
    si = lax.broadcasted_iota(jnp.int32, (L, L), 1)
    tril = (si <= ti).astype(F32)
    b = lax.dot_general(tril, lf, (((1,), (0,)), ((), ())), precision=lax.Precision.HIGHEST, preferred_element_type=F32)
    bL = jnp.sum(lf, axis=0, keepdims=True)
    return ig, b, ig.T, b.T, bL


def _mlstm_head(h, q, k_pair, v, o, ig_all, b_all, igT, bT, bL_all, hn, C, n, m):
    L = q.shape[0]
    k = jnp.where(_own_lanes(h, k_pair.shape), k_pair, 0.0)
    ti = lax.broadcasted_iota(jnp.int32, (L, L), 0)
    si = lax.broadcasted_iota(jnp.int32, (L, L), 1)
    lower = (si <= ti)
    ig = _lane_col(ig_all, h)
    b = _lane_col(b_all, M_HEADS + h)
    ig_row = igT[h:h + 1, :]
    b_row = bT[M_HEADS + h:M_HEADS + h + 1, :]
    bL = _lane_col(bL_all, M_HEADS + h)
    inter = b + m
    dlog = jnp.where(lower, b - b_row + ig_row, -jnp.inf)
    m_t = lax.stop_gradient(jnp.maximum(inter, jnp.max(dlog, axis=-1, keepdims=True)))
    qs = q * (M_QK ** -0.5)
    w = _dot(qs, k, 1, 1) * jnp.exp(dlog - m_t)
    s_inter = jnp.exp(inter - m_t)
    num = _dot(w, v, 1, 0) + s_inter * _dot(qs, C, 1, 0)
    den = jnp.sum(w, axis=-1, keepdims=True) + s_inter * jnp.sum(_mxf(qs) * _mxf(n), axis=-1, keepdims=True)
    hout = num * (1.0 / jnp.maximum(jnp.abs(den), jnp.exp(-m_t)))
    tail = bL - b + ig
    m_new = lax.stop_gradient(jnp.maximum(bL + m, jnp.max(tail, axis=0, keepdims=True)))
    ws = jnp.exp(tail - m_new)
    decay = jnp.exp(bL + m - m_new)
    wk = ws * k
    C_new = decay * C + _dot(wk, v, 0, 0)
    n_new = decay * n + jnp.sum(_mxf(ws) * _mxf(k), axis=0, keepdims=True)
    hs = hout * lax.rsqrt(jnp.mean(hout * hout, axis=-1, keepdims=True) + EPS) * hn
    gated = _sigmoid(o) * hs
    return (gated, C_new, n_new), m_new


M_OFF_Q, M_OFF_K = 0, M_HEADS * M_QK
M_OFF_V = 2 * M_HEADS * M_QK
M_OFF_O = M_OFF_V + M_HEADS * M_V
M_OFF_G = M_OFF_O + M_HEADS * M_V
M_PROJ = M_OFF_G + LANES
M_PAIRS = M_HEADS * M_QK // LANES


def _head_cols(off, h):
    return slice(off + h * LANES, off + (h + 1) * LANES)


def _own_lanes(h, shape):
    low = lax.broadcasted_iota(jnp.int32, shape, 1) < M_QK
    return low if h % 2 == 0 else jnp.logical_not(low)


def _mlstm_specs(NC, rev):
    H, L = M_HEADS, M_CHUNK
    cc = (lambda c: NC - 1 - c) if rev else (lambda c: c)
    proj = pl.BlockSpec((L, M_PROJ), lambda c: (cc(c), 0))
    vec = pl.BlockSpec((1, LANES), lambda c: (0, 0))
    hn = pl.BlockSpec((1, H * M_V), lambda c: (0, 0))
    hv = pl.BlockSpec((L, H * M_V), lambda c: (cc(c), 0))
    Cs = pl.BlockSpec((None, H, LANES, M_V), lambda c: (cc(c), 0, 0, 0))
    ns = pl.BlockSpec((None, H, 1, LANES), lambda c: (cc(c), 0, 0, 0))
    ms = pl.BlockSpec((None, H, 1, 1), lambda c: (cc(c), 0, 0, 0))
    return proj, vec, hn, hv, Cs, ns, ms


_MLSTM_STATE = [pltpu.VMEM((M_HEADS, LANES, M_V), F32), pltpu.VMEM((M_HEADS, 1, LANES), F32),
                pltpu.VMEM((M_HEADS, 1, 1), F32)]


def _mlstm_fwd(proj, bias, hn, shards, *, name):
    S = proj.shape[0]
    H, NC = M_HEADS, S // M_CHUNK
    ps, vec, hns, hv, Cs, ns, ms = _mlstm_specs(NC, False)
    nsh = len(shards)
    any_spec = pl.BlockSpec(memory_space=pl.ANY)

    def body(p_ref, b_ref, hn_ref, *rest):
        gated_ref, C_all, n_all, m_all = rest[nsh:nsh + 4]
        C_s, n_s, m_s, send_sems, recv_sems = rest[2 * nsh + 4:]
        gather = _Gather(rest[:nsh], rest[nsh + 4:2 * nsh + 4], send_sems, recv_sems)

        @pl.when(pl.program_id(0) == 0)
        def _():
            C_s[...] = jnp.zeros_like(C_s)
            n_s[...] = jnp.zeros_like(n_s)
            m_s[...] = jnp.zeros_like(m_s)
            gather.issue()

        gate_terms = _mlstm_gates(p_ref[:, M_OFF_G:M_OFF_G + LANES], b_ref[...])
        for h in range(H):
            C, n, m = C_s[h], n_s[h], m_s[h]
            C_all[h] = C
            n_all[h] = n
            m_all[h] = m
            (gated, Cn, nn), mn = _mlstm_head(
                h, p_ref[:, _head_cols(M_OFF_Q, h // 2)], p_ref[:, _head_cols(M_OFF_K, h // 2)],
                p_ref[:, _head_cols(M_OFF_V, h)], p_ref[:, _head_cols(M_OFF_O, h)], *gate_terms,
                hn_ref[:, _head_cols(0, h)], C, n, m)
            gated_ref[:, _head_cols(0, h)] = gated.astype(gated_ref.dtype)
            C_s[h] = Cn
            n_s[h] = nn
            m_s[h] = mn

        @pl.when(pl.program_id(0) == NC - 1)
        def _():
            gather.finish()

    return pl.pallas_call(
        body, name=name, grid=(NC,),
        in_specs=[ps, vec, hns] + [any_spec] * nsh,
        out_specs=[hv, Cs, ns, ms] + [any_spec] * nsh,
        out_shape=[jax.ShapeDtypeStruct((S, H * M_V), MXU_DTYPE),
                   jax.ShapeDtypeStruct((NC, H, LANES, M_V), F32),
                   jax.ShapeDtypeStruct((NC, H, 1, LANES), F32),
                   jax.ShapeDtypeStruct((NC, H, 1, 1), F32)] + _Gather.out_shape(shards),
        scratch_shapes=list(_MLSTM_STATE) + _Gather.semaphores(nsh),
        compiler_params=_cparams(("arbitrary",)),
    )(proj, bias, hn, *shards)


def _mlstm_bwd(proj, bias, hn, C_all, n_all, m_all, dgated, qs, *, name):
    S = proj.shape[0]
    H, NC = M_HEADS, S // M_CHUNK
    ps, vec, hns, hv, Cs, ns, ms = _mlstm_specs(NC, True)
    nq = len(qs)
    any_spec = pl.BlockSpec(memory_space=pl.ANY)

    def body(p_ref, b_ref, hn_ref, C_ref, n_ref, m_ref, dg_ref, *rest):
        dp_ref, db_ref, dhn_ref = rest[nq:nq + 3]
        dC_s, dn_s, send_sems, recv_sems = rest[2 * nq + 3:]
        exchange = _Exchange(rest[:nq], rest[nq + 3:2 * nq + 3], send_sems, recv_sems)

        @pl.when(pl.program_id(0) == 0)
        def _():
            dC_s[...] = jnp.zeros_like(dC_s)
            dn_s[...] = jnp.zeros_like(dn_s)
            db_ref[...] = jnp.zeros_like(db_ref)
            dhn_ref[...] = jnp.zeros_like(dhn_ref)
            exchange.issue()

        gate_terms, gates_vjp = jax.vjp(_mlstm_gates, p_ref[:, M_OFF_G:M_OFF_G + LANES], b_ref[...])
        d_terms = [jnp.zeros_like(t) for t in gate_terms]
        for h in range(H):
            def head(q, k, v, o, *rest, h=h):
                return _mlstm_head(h, q, k, v, o, *rest, m_ref[h])

            prim = (p_ref[:, _head_cols(M_OFF_Q, h // 2)], p_ref[:, _head_cols(M_OFF_K, h // 2)],
                    p_ref[:, _head_cols(M_OFF_V, h)], p_ref[:, _head_cols(M_OFF_O, h)], *gate_terms,
                    hn_ref[:, _head_cols(0, h)], C_ref[h], n_ref[h])
            _, vjp, _ = jax.vjp(head, *prim, has_aux=True)
            dq, dk, dv, do, *d_gate, dhnh, dC, dn = vjp((dg_ref[:, _head_cols(0, h)].astype(F32), dC_s[h], dn_s[h]))
            if h % 2 == 0:
                dq_pair, dk_pair = dq, dk
            else:
                dp_ref[:, _head_cols(M_OFF_Q, h // 2)] = (dq_pair + dq).astype(dp_ref.dtype)
                dp_ref[:, _head_cols(M_OFF_K, h // 2)] = (dk_pair + dk).astype(dp_ref.dtype)
            dp_ref[:, _head_cols(M_OFF_V, h)] = dv.astype(dp_ref.dtype)
            dp_ref[:, _head_cols(M_OFF_O, h)] = do.astype(dp_ref.dtype)
            d_terms = [a + g for a, g in zip(d_terms, d_gate)]
            dhn_ref[:, _head_cols(0, h)] += dhnh
            dC_s[h] = dC
            dn_s[h] = dn
        dG, dbias = gates_vjp(tuple(d_terms))
        dp_ref[:, M_OFF_G:M_OFF_G + LANES] = dG.astype(dp_ref.dtype)
        db_ref[...] += dbias

        @pl.when(pl.program_id(0) == NC - 1)
        def _():
            exchange.finish()

    return pl.pallas_call(
        body, name=name, grid=(NC,),
        in_specs=[ps, vec, hns, Cs, ns, ms, hv] + [any_spec] * nq,
        out_specs=[ps, vec, hns] + [any_spec] * nq,
        out_shape=[jax.ShapeDtypeStruct((S, M_PROJ), MXU_DTYPE), jax.ShapeDtypeStruct((1, LANES), F32),
                   jax.ShapeDtypeStruct((1, H * M_V), F32)] + _Exchange.out_shape(qs),
        scratch_shapes=list(_MLSTM_STATE[:2]) + _Exchange.semaphores(nq),
        compiler_params=_cparams(("arbitrary",)),
    )(proj, bias, hn, C_all, n_all, m_all, dgated, *qs)


A_NQ = A_QH * A_DH
A_NKV = 2 * A_KVH * A_DH
A_PAIRS = A_G // 2


def _sink_softmax_parts(s, sink):
    mx = jnp.maximum(jnp.max(s, axis=-1, keepdims=True), sink)
    e = jnp.exp(s - mx)
    e_sink = jnp.exp(sink - mx)
    r = 1.0 / (jnp.sum(e, axis=-1, keepdims=True) + e_sink)
    return e * r, e_sink * r


@jax.custom_vjp
def _sink_softmax(s, sink):
    return _sink_softmax_parts(s, sink)[0]


def _sink_softmax_fwd(s, sink):
    p, p_sink = _sink_softmax_parts(s, sink)
    return p, (p, p_sink)


def _sink_softmax_bwd(res, dp):
    p, p_sink = res
    t = jnp.sum(dp * p, axis=-1, keepdims=True)
    return p * (dp - t), -p_sink * t


_sink_softmax.defvjp(_sink_softmax_fwd, _sink_softmax_bwd)


def _band_bias():
    qi = np.arange(A_G * A_BLK)[:, None] % A_BLK
    ku = np.arange(2 * A_BLK)[None, :]
    diff = qi - (ku - A_BLK)
    band = (diff >= 0) & (diff < A_BLK)
    seen = np.stack([band, band & (ku >= A_BLK)])
    return jnp.asarray(np.where(seen, 0.0, -np.inf), F32)


_BAND_BIAS_SPEC = pl.BlockSpec((None, A_G * A_BLK, 2 * A_BLK), lambda n: (jnp.where(n == 0, 1, 0), 0, 0))


def _attn_group(Ps, KLO, KHI, VLO, VHI, sinks, bias):
    B = Ps[0].shape[0]
    R = len(Ps) * B
    q2 = jnp.concatenate(Ps, axis=0) * (A_DH ** -0.5)
    s = jnp.concatenate([_dot(q2, KLO, 1, 1), _dot(q2, KHI, 1, 1)], axis=0) + bias
    ri = lax.broadcasted_iota(jnp.int32, (2 * R, A_G), 0)
    head = 2 * lax.shift_right_logical(ri & (R - 1), B.bit_length() - 1) + lax.shift_right_logical(ri, R.bit_length() - 1)
    onehot = head == lax.broadcasted_iota(jnp.int32, (2 * R, A_G), 1)
    sink = jnp.sum(jnp.where(onehot, sinks, 0.0), axis=1, keepdims=True)
    p = _sink_softmax(s, sink)
    o = _dot(p[:R], VLO, 1, 0) + _dot(p[R:], VHI, 1, 0)
    return tuple(o[j * B:(j + 1) * B] for j in range(len(Ps)))


def _swap_halves_of_lanes(t):
    return pltpu.roll(t, LANES // 2, 1)


def _kv_operands(kvp_ref, kvc_ref, h):
    kk = jnp.concatenate([kvp_ref[:, :LANES], kvc_ref[:, :LANES]], axis=0)
    vv = jnp.concatenate([kvp_ref[:, LANES:], kvc_ref[:, LANES:]], axis=0)
    low = lax.broadcasted_iota(jnp.int32, kk.shape, 1) < A_DH
    own = low if h == 0 else jnp.logical_not(low)
    k_own = jnp.where(own, kk, 0.0)
    v_own = jnp.where(own, vv, 0.0)
    k_oth, v_oth = _swap_halves_of_lanes(k_own), _swap_halves_of_lanes(v_own)
    if h == 0:
        return own, k_own, k_oth, v_own, v_oth
    return own, k_oth, k_own, v_oth, v_own


def _pair_cols(h, j):
    c = (h * A_PAIRS + j) * LANES
    return slice(c, c + LANES)


def _attn_fwd(proj, sinks, *, name):
    S = proj.shape[0]
    NB = S // A_BLK
    kv_blk = A_NQ // A_NKV
    qs = pl.BlockSpec((A_BLK, A_NQ), lambda n: (n, 0))
    cur = pl.BlockSpec((A_BLK, A_NKV), lambda n: (n, kv_blk))
    prev = pl.BlockSpec((A_BLK, A_NKV), lambda n: (jnp.maximum(n - 1, 0), kv_blk))
    sk = pl.BlockSpec((A_KVH, A_G), lambda n: (0, 0))

    def body(q_ref, kvp_ref, kvc_ref, s_ref, bias_ref, o_ref):
        for h in range(A_KVH):
            _, KLO, KHI, VLO, VHI = _kv_operands(kvp_ref, kvc_ref, h)
            Ps = tuple(q_ref[:, _pair_cols(h, j)] for j in range(A_PAIRS))
            outs = _attn_group(Ps, KLO, KHI, VLO, VHI, s_ref[h:h + 1, :], bias_ref[...])
            for j in range(A_PAIRS):
                o_ref[:, _pair_cols(h, j)] = outs[j].astype(o_ref.dtype)

    return pl.pallas_call(
        body, name=name, grid=(NB,),
        in_specs=[qs, prev, cur, sk, _BAND_BIAS_SPEC], out_specs=qs,
        out_shape=jax.ShapeDtypeStruct((S, A_NQ), MXU_DTYPE),
        compiler_params=_cparams(("parallel",)),
    )(proj, proj, proj, sinks, _band_bias())


def _attn_bwd(proj, sinks, do, *, name):
    S = proj.shape[0]
    NB = S // A_BLK
    last = NB - 1
    kv_blk = A_NQ // A_NKV
    qs = pl.BlockSpec((A_BLK, A_NQ), lambda n: (jnp.minimum(n, last), 0))
    cur = pl.BlockSpec((A_BLK, A_NKV), lambda n: (jnp.minimum(n, last), kv_blk))
    prev = pl.BlockSpec((A_BLK, A_NKV), lambda n: (jnp.clip(n - 1, 0, last), kv_blk))
    sk = pl.BlockSpec((A_KVH, A_G), lambda n: (0, 0))
    lag = pl.BlockSpec((A_BLK, A_NKV), lambda n: (jnp.maximum(n - 1, 0), 0))
    cq_spec = pl.BlockSpec((1, A_NQ), lambda n: (0, 0))
    ckv_spec = pl.BlockSpec((1, A_NKV), lambda n: (0, 0))

    def body(q_ref, kvp_ref, kvc_ref, s_ref, do_ref, bias_ref, dq_ref, dkv_ref, ds_ref, cq_ref, ckv_ref, keep):
        n = pl.program_id(0)

        @pl.when(n == 0)
        def _():
            keep[...] = jnp.zeros_like(keep)
            ds_ref[...] = jnp.zeros_like(ds_ref)
            cq_ref[...] = jnp.zeros_like(cq_ref)
            ckv_ref[...] = jnp.zeros_like(ckv_ref)

        @pl.when(n < NB)
        def _():
            f = functools.partial(_attn_group, bias=bias_ref[...])
            dk = jnp.zeros((2 * A_BLK, LANES), F32)
            dv = jnp.zeros((2 * A_BLK, LANES), F32)
            ds_rows = []
            for h in range(A_KVH):
                own, KLO, KHI, VLO, VHI = _kv_operands(kvp_ref, kvc_ref, h)
                Ps = tuple(q_ref[:, _pair_cols(h, j)] for j in range(A_PAIRS))
                _, vjp = jax.vjp(f, Ps, KLO, KHI, VLO, VHI, s_ref[h:h + 1, :])
                dPs, dKLO, dKHI, dVLO, dVHI, dsk = vjp(
                    tuple(do_ref[:, _pair_cols(h, j)].astype(F32) for j in range(A_PAIRS)))
                for j in range(A_PAIRS):
                    dq_ref[:, _pair_cols(h, j)] = dPs[j].astype(dq_ref.dtype)
                    cq_ref[:, _pair_cols(h, j)] += jnp.sum(dPs[j], axis=0, keepdims=True)
                dk_own, dk_oth = (dKLO, dKHI) if h == 0 else (dKHI, dKLO)
                dv_own, dv_oth = (dVLO, dVHI) if h == 0 else (dVHI, dVLO)
                dk = dk + jnp.where(own, dk_own, 0.0) + _swap_halves_of_lanes(jnp.where(own, 0.0, dk_oth))
                dv = dv + jnp.where(own, dv_own, 0.0) + _swap_halves_of_lanes(jnp.where(own, 0.0, dv_oth))
                ds_rows.append(dsk)
            ds_ref[...] += jnp.concatenate(ds_rows, axis=0)
            dkv = jnp.concatenate([dk, dv], axis=1)
            done = keep[...] + dkv[:A_BLK]
            dkv_ref[...] = done.astype(dkv_ref.dtype)
            ckv_ref[...] += jnp.sum(done, axis=0, keepdims=True)
            keep[...] = dkv[A_BLK:]

        @pl.when(n == NB)
        def _():
            done = keep[...]
            dkv_ref[...] = done.astype(dkv_ref.dtype)
            ckv_ref[...] += jnp.sum(done, axis=0, keepdims=True)

    return pl.pallas_call(
        body, name=name, grid=(NB + 1,),
        in_specs=[qs, prev, cur, sk, qs, _BAND_BIAS_SPEC],
        out_specs=[qs, lag, sk, cq_spec, ckv_spec],
        out_shape=[jax.ShapeDtypeStruct((S, A_NQ), MXU_DTYPE), jax.ShapeDtypeStruct((S, A_NKV), MXU_DTYPE),
                   jax.ShapeDtypeStruct((A_KVH, A_G), F32), jax.ShapeDtypeStruct((1, A_NQ), F32),
                   jax.ShapeDtypeStruct((1, A_NKV), F32)],
        scratch_shapes=[pltpu.VMEM((A_BLK, A_NKV), F32)],
        compiler_params=_cparams(("arbitrary",)),
    )(proj, proj, proj, sinks, do, _band_bias())


HALO = 8


def _shift_rows(t, j):
    return pltpu.roll(t, j % t.shape[0], 0)


def _conv_taps(gate_ext):
    return _shift_rows(gate_ext, 2), _shift_rows(gate_ext, 1), gate_ext


def _conv_gate(g2, g1, g0, cw, cb):
    return cb + cw[0:1, :] * g2 + cw[1:2, :] * g1 + cw[2:3, :] * g0


def _convgate_fwd(u, cw, cb, *, name):
    S, F2 = u.shape
    F = F2 // 2
    tm = _pick(S, (256, 128))
    hb = tm // HALO
    urow = pl.BlockSpec((tm, F2), lambda i: (i, 0))
    uprev = pl.BlockSpec((HALO, F), lambda i: (jnp.maximum(i * hb - 1, 0), 0))

    def body(u_ref, up_ref, cw_ref, cb_ref, a_ref):
        i = pl.program_id(0)
        prev = jnp.where(i > 0, up_ref[...], 0.0)
        gc = _conv_gate(*_conv_taps(jnp.concatenate([prev, u_ref[:, :F]], axis=0)), cw_ref[...], cb_ref[...])[HALO:]
        a_ref[...] = (gc * _sigmoid(gc) * u_ref[:, F:]).astype(a_ref.dtype)

    return pl.pallas_call(
        body, name=name, grid=(S // tm,),
        in_specs=[urow, uprev, pl.BlockSpec((3, F), lambda i: (0, 0)), pl.BlockSpec((1, F), lambda i: (0, 0))],
        out_specs=pl.BlockSpec((tm, F), lambda i: (i, 0)),
        out_shape=jax.ShapeDtypeStruct((S, F), MXU_DTYPE),
        compiler_params=_cparams(("parallel",)),
    )(u, u, cw, cb)


def _convgate_bwd(u, da, cw, cb, gs, *, name):
    S, F2 = u.shape
    ng = len(gs)
    any_spec = pl.BlockSpec(memory_space=pl.ANY)
    F = F2 // 2
    tm = _pick(S, (256, 128))
    hb = tm // HALO
    nt = S // tm
    nh = S // HALO
    urow = pl.BlockSpec((tm, F2), lambda i: (i, 0))
    uprev = pl.BlockSpec((HALO, F), lambda i: (jnp.maximum(i * hb - 1, 0), 0))
    unext = pl.BlockSpec((HALO, F2), lambda i: (jnp.minimum((i + 1) * hb, nh - 1), 0))
    darow = pl.BlockSpec((tm, F), lambda i: (i, 0))
    danext = pl.BlockSpec((HALO, F), lambda i: (jnp.minimum((i + 1) * hb, nh - 1), 0))

    def body(u_ref, up_ref, un_ref, da_ref, dan_ref, cw_ref, cb_ref, *rest):
        du_ref, dcw_ref, dcb_ref = rest[ng:ng + 3]
        swap = _Swap(rest[:ng], rest[ng + 3:2 * ng + 3], *rest[2 * ng + 3:]) if ng else None
        i = pl.program_id(0)
        cwv = cw_ref[...]
        prev = jnp.where(i > 0, up_ref[...], 0.0)
        gate_ext = jnp.concatenate([prev, u_ref[:, :F], un_ref[:, :F]], axis=0)
        val_ext = jnp.concatenate([u_ref[:, F:], un_ref[:, F:]], axis=0)
        da_next = jnp.where(i < nt - 1, dan_ref[...].astype(F32), 0.0)
        da_ext = jnp.concatenate([da_ref[...].astype(F32), da_next], axis=0)
        g2, g1, g0 = (t[HALO:] for t in _conv_taps(gate_ext))
        gc = _conv_gate(g2, g1, g0, cwv, cb_ref[...])
        sg = _sigmoid(gc)
        silu = gc * sg
        dval = da_ext * silu
        dgc = da_ext * val_ext * (sg * (1.0 + gc * (1.0 - sg)))
        dgate = cwv[2:3, :] * dgc + cwv[1:2, :] * _shift_rows(dgc, -1) + cwv[0:1, :] * _shift_rows(dgc, -2)
        du_ref[:, :F] = dgate[:tm].astype(du_ref.dtype)
        du_ref[:, F:] = dval[:tm].astype(du_ref.dtype)
        dgc_c = dgc[:tm]
        dcw = jnp.concatenate([jnp.sum(dgc_c * g2[:tm], axis=0, keepdims=True),
                               jnp.sum(dgc_c * g1[:tm], axis=0, keepdims=True),
                               jnp.sum(dgc_c * g0[:tm], axis=0, keepdims=True)], axis=0)
        dcb = jnp.sum(dgc_c, axis=0, keepdims=True)

        @pl.when(i == 0)
        def _():
            dcw_ref[...] = dcw
            dcb_ref[...] = dcb
            if ng:
                swap.issue()

        @pl.when(i > 0)
        def _():
            dcw_ref[...] += dcw
            dcb_ref[...] += dcb

        if ng:
            @pl.when(i == nt - 1)
            def _():
                swap.finish()

    return pl.pallas_call(
        body, name=name, grid=(nt,),
        in_specs=[urow, uprev, unext, darow, danext,
                  pl.BlockSpec((3, F), lambda i: (0, 0)), pl.BlockSpec((1, F), lambda i: (0, 0))] + [any_spec] * ng,
        out_specs=[urow, pl.BlockSpec((3, F), lambda i: (0, 0)), pl.BlockSpec((1, F), lambda i: (0, 0))] + [any_spec] * ng,
        out_shape=[jax.ShapeDtypeStruct((S, F2), MXU_DTYPE), jax.ShapeDtypeStruct((3, F), F32),
                   jax.ShapeDtypeStruct((1, F), F32)] + (_Swap.out_shape(gs) if ng else []),
        scratch_shapes=_Swap.semaphores(ng) if ng else [],
        compiler_params=_cparams(("arbitrary",)),
    )(u, u, u, da, da, cw, cb, *gs)


def _adamw_math(w, g, m, v):
    m = ADAM_B1 * m + (1.0 - ADAM_B1) * g
    v = ADAM_B2 * v + (1.0 - ADAM_B2) * (g * g)
    m_hat = m / (1.0 - ADAM_B1 ** ADAM_STEP)
    v_hat = v / (1.0 - ADAM_B2 ** ADAM_STEP)
    delta = -ADAM_LR * (m_hat / (jnp.sqrt(v_hat) + ADAM_EPS) + ADAM_WD * w)
    return delta, m, v


def _adamw_layers(w, gs, m, v, *, name):
    Lr, R, C = w.shape
    tr = _pick(R, (256, 128, 64, 32, 16, 8))
    outs = None
    for layer, g in enumerate(gs):
        def body(w_ref, g_ref, m_ref, v_ref, *rest):
            go_ref, d_ref, nm_ref, nv_ref = rest[-4:]
            gv = g_ref[...]
            d, nm, nv = _adamw_math(w_ref[...], gv, m_ref[...], v_ref[...])
            go_ref[...] = gv
            d_ref[...] = d
            nm_ref[...] = nm
            nv_ref[...] = nv

        lay = pl.BlockSpec((None, tr, C), lambda i, layer=layer: (layer, i, 0))
        in_specs = [lay, pl.BlockSpec((tr, C), lambda i: (i, 0)), lay, lay]
        args = [w, g, m, v]
        aliases = {}
        if outs is not None:
            in_specs += [pl.BlockSpec(memory_space=pl.ANY)] * 4
            args += list(outs)
            aliases = {4 + t: t for t in range(4)}
        outs = pl.pallas_call(
            body, name=f"{name}_{layer}", grid=(R // tr,), in_specs=in_specs, out_specs=[lay] * 4,
            out_shape=[jax.ShapeDtypeStruct((Lr, R, C), F32)] * 4, input_output_aliases=aliases,
            compiler_params=_cparams(("parallel",)),
        )(*args)
    return outs


def _adamw_small(items, *, name):
    n = len(items)

    def body(*refs):
        ins, outs = refs[:4 * n], refs[4 * n:]
        for t in range(n):
            w, g, m, v = (r[...] for r in ins[4 * t:4 * t + 4])
            d, nm, nv = _adamw_math(w, g, m, v)
            outs[3 * t][...] = d
            outs[3 * t + 1][...] = nm
            outs[3 * t + 2][...] = nv

    flat = [a for it in items for a in it]
    out_shape = [jax.ShapeDtypeStruct(it[0].shape, F32) for it in items for _ in range(3)]
    vm = pl.BlockSpec(memory_space=pltpu.VMEM)
    res = pl.pallas_call(body, name=name, in_specs=[vm] * len(flat), out_specs=[vm] * len(out_shape),
                         out_shape=out_shape)(*flat)
    return [tuple(res[3 * t:3 * t + 3]) for t in range(n)]


def _place():
    return lax.axis_index("x"), lax.axis_index("y"), lax.axis_index("c")


_FLIPS = ((1, 0), (0, 1), (1, 1))


ROW_ALIGN = 16


def _half(rows, which):
    return pl.ds(pl.multiple_of(which * (rows // 2), ROW_ALIGN), rows // 2)


def _remote(src, dst, send_sems, recv_sems, k, to):
    return pltpu.make_async_remote_copy(src_ref=src, dst_ref=dst, send_sem=send_sems.at[k], recv_sem=recv_sems.at[k],
                                        device_id=to, device_id_type=MESH)


class _Gather:
    def __init__(self, w_refs, out_refs, send_sems, recv_sems):
        self.w_refs, self.out_refs, self.send_sems, self.recv_sems = w_refs, out_refs, send_sems, recv_sems
        self.pairs = [(i, j) for i in range(len(w_refs)) for j in range(3)]

    @staticmethod
    def out_shape(shards):
        return [jax.ShapeDtypeStruct((N_CHIPS,) + s.shape, s.dtype) for s in shards]

    @staticmethod
    def semaphores(n):
        return [pltpu.SemaphoreType.DMA((6 * n,)), pltpu.SemaphoreType.DMA((6 * n,))]

    def _where(self):
        x, y, c = _place()
        return x, y, c, [(x ^ fx, y ^ fy) for fx, fy in _FLIPS]

    def _over_ici(self, i, j, landed):
        x, y, c, chips = self._where()
        px, py = chips[j]
        mine = _half(self.w_refs[i].shape[0], c)
        if landed:
            src = dst = self.out_refs[i].at[2 * px + py, mine]
        else:
            src, dst = self.w_refs[i].at[mine], self.out_refs[i].at[2 * x + y, mine]
        return _remote(src, dst, self.send_sems, self.recv_sems, 6 * i + j, (px, py, c))

    def _over_d2d(self, i, j, which):
        x, y, c, chips = self._where()
        px, py = chips[j]
        blk = self.out_refs[i].at[2 * px + py, _half(self.w_refs[i].shape[0], which)]
        return _remote(blk, blk, self.send_sems, self.recv_sems, 6 * i + 3 + j, (x, y, 1 - c))

    def issue(self):
        for i, j in self.pairs:
            self._over_ici(i, j, False).start()

    def finish(self):
        c = lax.axis_index("c")
        for i, j in self.pairs:
            self._over_ici(i, j, True).wait_recv()
            self._over_d2d(i, j, c).start()
        for i, j in self.pairs:
            self._over_d2d(i, j, 1 - c).wait_recv()
        for i, j in self.pairs:
            self._over_ici(i, j, False).wait_send()
            self._over_d2d(i, j, c).wait_send()


def _swap_halves(gs, *, name):
    n = len(gs)
    any_spec = pl.BlockSpec(memory_space=pl.ANY)

    def body(*refs):
        swap = _Swap(refs[:n], refs[n:2 * n], refs[2 * n], refs[2 * n + 1])
        swap.issue()
        swap.finish()

    return pl.pallas_call(
        body, name=name, in_specs=[any_spec] * n, out_specs=[any_spec] * n,
        out_shape=_Swap.out_shape(gs), scratch_shapes=_Swap.semaphores(n),
    )(*gs)


class _Swap:
    def __init__(self, g_refs, out_refs, send_sems, recv_sems):
        self.g_refs, self.out_refs, self.send_sems, self.recv_sems = g_refs, out_refs, send_sems, recv_sems

    @staticmethod
    def out_shape(gs):
        return [jax.ShapeDtypeStruct((N_CHIPS, g.shape[1] // 2, g.shape[2]), g.dtype) for g in gs]

    @staticmethod
    def semaphores(n):
        return [pltpu.SemaphoreType.DMA((n,)), pltpu.SemaphoreType.DMA((n,))]

    def _copies(self):
        x, y, c = _place()
        return [_remote(g.at[:, _half(g.shape[1], 1 - c)], out, self.send_sems, self.recv_sems, i, (x, y, 1 - c))
                for i, (g, out) in enumerate(zip(self.g_refs, self.out_refs))]

    def issue(self):
        for cp in self._copies():
            cp.start()

    def finish(self):
        for cp in self._copies():
            cp.wait()


def _add_halves(g, got, c_arr, *, name):
    _, rows, cols = g.shape
    blk = (None, rows // 2, cols)

    def body(c_ref, g_ref, got_ref, o_ref):
        o_ref[...] = (g_ref[...] + got_ref[...]).astype(o_ref.dtype)

    return pl.pallas_call(
        body, name=name,
        grid_spec=pltpu.PrefetchScalarGridSpec(
            num_scalar_prefetch=1, grid=(N_CHIPS,),
            in_specs=[pl.BlockSpec(blk, lambda j, c_ref: (j, c_ref[0], 0)), pl.BlockSpec(blk, lambda j, c_ref: (j, 0, 0))],
            out_specs=pl.BlockSpec(blk, lambda j, c_ref: (j, 0, 0))),
        out_shape=jax.ShapeDtypeStruct((N_CHIPS, rows // 2, cols), WIRE_DTYPE),
        compiler_params=_cparams(("parallel",)),
    )(c_arr, g, got)


class _Exchange:
    def __init__(self, q_refs, out_refs, send_sems, recv_sems):
        self.q_refs, self.out_refs, self.send_sems, self.recv_sems = q_refs, out_refs, send_sems, recv_sems
        self.pairs = [(i, j) for i in range(len(q_refs)) for j in range(3)]

    @staticmethod
    def out_shape(qs):
        return [jax.ShapeDtypeStruct(q.shape, q.dtype) for q in qs]

    @staticmethod
    def semaphores(n):
        return [pltpu.SemaphoreType.DMA((3 * n,)), pltpu.SemaphoreType.DMA((3 * n,))]

    def _copy(self, i, j, landed):
        x, y, c = _place()
        px, py = [(x ^ fx, y ^ fy) for fx, fy in _FLIPS][j]
        if landed:
            src = dst = self.out_refs[i].at[2 * px + py]
        else:
            src, dst = self.q_refs[i].at[2 * px + py], self.out_refs[i].at[2 * x + y]
        return _remote(src, dst, self.send_sems, self.recv_sems, 3 * i + j, (px, py, c))

    def issue(self):
        for i, j in self.pairs:
            self._copy(i, j, False).start()

    def finish(self):
        for i, j in self.pairs:
            self._copy(i, j, True).wait_recv()
        for i, j in self.pairs:
            self._copy(i, j, False).wait_send()


def _sum_chips(q, r, place_arr, *, name):
    _, h, cols = q.shape
    blk = (None, h, cols)

    def body(p_ref, q_ref, r1_ref, r2_ref, r3_ref, o_ref):
        o_ref[...] = ((q_ref[...].astype(F32) + r1_ref[...].astype(F32)) + r2_ref[...].astype(F32)) + r3_ref[...].astype(F32)

    other = [pl.BlockSpec(blk, lambda i, p_ref, f=f: (p_ref[0] ^ f, 0, 0)) for f in (1, 2, 3)]
    return pl.pallas_call(
        body, name=name,
        grid_spec=pltpu.PrefetchScalarGridSpec(
            num_scalar_prefetch=1, grid=(1,),
            in_specs=[pl.BlockSpec(blk, lambda i, p_ref: (p_ref[0], 0, 0))] + other,
            out_specs=pl.BlockSpec((h, cols), lambda i, p_ref: (p_ref[1], 0))),
        out_shape=jax.ShapeDtypeStruct((2 * h, cols), F32),
        compiler_params=_cparams(("arbitrary",)),
    )(place_arr, q, r, r, r)


def _join_halves(fs, *, name):
    n = len(fs)
    any_spec = pl.BlockSpec(memory_space=pl.ANY)

    def body(*refs):
        out_refs, send_sems, recv_sems = refs[n:2 * n], refs[2 * n], refs[2 * n + 1]
        x, y, c = _place()
        sent = []
        for i in range(n):
            mine = out_refs[i].at[_half(fs[i].shape[0], c)]
            cp = _remote(mine, mine, send_sems, recv_sems, i, (x, y, 1 - c))
            cp.start()
            sent.append(cp)
        for i in range(n):
            its = out_refs[i].at[_half(fs[i].shape[0], 1 - c)]
            _remote(its, its, send_sems, recv_sems, i, (x, y, 1 - c)).wait_recv()
        for cp in sent:
            cp.wait_send()

    return pl.pallas_call(
        body, name=name, in_specs=[any_spec] * n, out_specs=[any_spec] * n,
        out_shape=[jax.ShapeDtypeStruct(f.shape, f.dtype) for f in fs], input_output_aliases={i: i for i in range(n)},
        scratch_shapes=[pltpu.SemaphoreType.DMA((n,)), pltpu.SemaphoreType.DMA((n,))],
    )(*fs)


def _allreduce_small(buf, *, name):
    R = buf.shape[0]
    vm = pl.BlockSpec(memory_space=pltpu.VMEM)

    def body(b_ref, o_ref, slots, send_sems, recv_sems):
        x, y, c = _place()
        me = 4 * x + 2 * y + c
        slots[me] = b_ref[...]
        sends = []
        for kk in range(1, 8):
            fx, fy, fc = (kk >> 2) & 1, (kk >> 1) & 1, kk & 1
            cp = pltpu.make_async_remote_copy(
                src_ref=b_ref, dst_ref=slots.at[me], send_sem=send_sems.at[kk - 1], recv_sem=recv_sems.at[kk - 1],
                device_id=(x ^ fx, y ^ fy, c ^ fc), device_id_type=MESH)
            cp.start()
            sends.append(cp)
        for kk in range(1, 8):
            fx, fy, fc = (kk >> 2) & 1, (kk >> 1) & 1, kk & 1
            peer = 4 * (x ^ fx) + 2 * (y ^ fy) + (c ^ fc)
            pltpu.make_async_remote_copy(
                src_ref=b_ref, dst_ref=slots.at[peer], send_sem=send_sems.at[kk - 1], recv_sem=recv_sems.at[kk - 1],
                device_id=(x ^ fx, y ^ fy, c ^ fc), device_id_type=MESH).wait_recv()
        for cp in sends:
            cp.wait_send()
        acc = slots[0]
        for d in range(1, 8):
            acc = acc + slots[d]
        o_ref[...] = acc

    return pl.pallas_call(
        body, name=name, in_specs=[vm], out_specs=vm,
        out_shape=jax.ShapeDtypeStruct((R, LANES), F32),
        scratch_shapes=[pltpu.VMEM((8, R, LANES), F32), pltpu.SemaphoreType.DMA((7,)), pltpu.SemaphoreType.DMA((7,))],
    )(buf)


def _pad_rows(v, mult=8 * LANES):
    flat = v.reshape(-1)
    n = flat.shape[0]
    tot = -(-n // mult) * mult
    return jnp.pad(flat, (0, tot - n)).reshape(-1, LANES)


def _pack_small(parts):
    return jnp.concatenate([_pad_rows(p.astype(F32)) for p in parts], axis=0)


def _unpack_small(buf, shapes):
    out, r = [], 0
    for sh in shapes:
        n = math.prod(sh)
        rows = -(-n // (8 * LANES)) * 8
        out.append(buf[r:r + rows].reshape(-1)[:n].reshape(sh))
        r += rows
    return out


def _cols_to_shards(w):
    *lead, K, N = w.shape
    t = w.reshape(*lead, K, N_CHIPS, N // N_CHIPS)
    return jnp.moveaxis(t, -2, 0)


def _shards_to_cols(t):
    t = jnp.moveaxis(t, 0, -2)
    *lead, K, _, n = t.shape
    return t.reshape(*lead, K, N_CHIPS * n)


_BIG = ("m_w_in", "m_w_out", "a_w_in", "a_w_out", "f_w_up", "f_w_down")


def kernel(x, m_w_in, m_gate_bias, m_head_norm, m_w_out, a_w_in, a_b_in, a_sinks, a_w_out, a_b_out, norm_mix_pre, norm_mix_post, norm_ffn_pre, norm_ffn_post, f_w_up, f_conv_w, f_conv_b, f_w_down, loss_target, m_m_w_in, m_m_gate_bias, m_m_head_norm, m_m_w_out, m_a_w_in, m_a_b_in, m_a_sinks, m_a_w_out, m_a_b_out, m_norm_mix_pre, m_norm_mix_post, m_norm_ffn_pre, m_norm_ffn_post, m_f_w_up, m_f_conv_w, m_f_conv_b, m_f_w_down, v_m_w_in, v_m_gate_bias, v_m_head_norm, v_m_w_out, v_a_w_in, v_a_b_in, v_a_sinks, v_a_w_out, v_a_b_out, v_norm_mix_pre, v_norm_mix_post, v_norm_ffn_pre, v_norm_ffn_post, v_f_w_up, v_f_conv_w, v_f_conv_b, v_f_w_down):
    params = dict(m_w_in=m_w_in, m_gate_bias=m_gate_bias, m_head_norm=m_head_norm, m_w_out=m_w_out, a_w_in=a_w_in,
                  a_b_in=a_b_in, a_sinks=a_sinks, a_w_out=a_w_out, a_b_out=a_b_out, norm_mix_pre=norm_mix_pre,
                  norm_mix_post=norm_mix_post, norm_ffn_pre=norm_ffn_pre, norm_ffn_post=norm_ffn_post, f_w_up=f_w_up,
                  f_conv_w=f_conv_w, f_conv_b=f_conv_b, f_w_down=f_w_down)
    mom1 = dict(m_w_in=m_m_w_in, m_gate_bias=m_m_gate_bias, m_head_norm=m_m_head_norm, m_w_out=m_m_w_out,
                a_w_in=m_a_w_in, a_b_in=m_a_b_in, a_sinks=m_a_sinks, a_w_out=m_a_w_out, a_b_out=m_a_b_out,
                norm_mix_pre=m_norm_mix_pre, norm_mix_post=m_norm_mix_post, norm_ffn_pre=m_norm_ffn_pre,
                norm_ffn_post=m_norm_ffn_post, f_w_up=m_f_w_up, f_conv_w=m_f_conv_w, f_conv_b=m_f_conv_b,
                f_w_down=m_f_w_down)
    mom2 = dict(m_w_in=v_m_w_in, m_gate_bias=v_m_gate_bias, m_head_norm=v_m_head_norm, m_w_out=v_m_w_out,
                a_w_in=v_a_w_in, a_b_in=v_a_b_in, a_sinks=v_a_sinks, a_w_out=v_a_w_out, a_b_out=v_a_b_out,
                norm_mix_pre=v_norm_mix_pre, norm_mix_post=v_norm_mix_post, norm_ffn_pre=v_norm_ffn_pre,
                norm_ffn_post=v_norm_ffn_post, f_w_up=v_f_w_up, f_conv_w=v_f_conv_w, f_conv_b=v_f_conv_b,
                f_w_down=v_f_w_down)
    order = list(params)

    mx, my, mc = _place()
    chip = 2 * mx + my
    h0 = x[0]
    target = loss_target[0]

    def two_d(t):
        return t.reshape(-1, t.shape[-1])

    early = ("m_w_in",)
    late = tuple(n for n in _BIG if n not in early)
    mine = {n: two_d(params[n]).astype(MXU_DTYPE) for n in _BIG}

    def with_own_slot(names, theirs):
        return {n: lax.dynamic_update_slice(t, mine[n][None], (chip, 0, 0)) for n, t in zip(names, theirs)}

    z0, *early_theirs = _rms_fwd(h0, norm_mix_pre[0:1], [mine[n] for n in early], out_dtype=MXU_DTYPE,
                                 name="mix_pre_norm0")
    gathered = with_own_slot(early, early_theirs)

    def in_place(shard, axis):
        width = shard.shape[axis]
        z = jnp.zeros(shard.shape[:axis] + (N_CHIPS * width,) + shard.shape[axis + 1:], F32)
        contrib = jnp.where(mc == 0, shard, 0.0)
        return lax.dynamic_update_slice_in_dim(z, contrib, chip * width, axis)

    sm_in = [in_place(a_b_in, 1), in_place(a_b_out, 1), in_place(f_conv_w, 2)]
    sm_full = _unpack_small(_allreduce_small(_pack_small(sm_in), name="gather_small"), [t.shape for t in sm_in])
    b_in_full, b_out_full, conv_w_full = sm_full

    W_in = _shards_to_cols(gathered["m_w_in"])
    W_all = jnp.pad(W_in, ((0, 0), (0, M_PROJ - W_in.shape[1])))
    gbias = jnp.pad(m_gate_bias[0].reshape(1, 2 * M_HEADS), ((0, 0), (0, LANES - 2 * M_HEADS)))

    grads = {}

    def ffn_fwd(i, z1, end):
        u = _mm(z1, W_up, b_layer=i, name=f"ffn_up{i}")
        a = _convgate_fwd(u, conv_w_full[i], f_conv_b[i:i + 1], name=f"ffn_act{i}")
        return u, a, _mm(a, W_down[i], rows=end, name=f"ffn_down{i}")

    proj = _mm(z0, W_all, name="mlstm_proj")
    gated, C_all, n_all, m_all, *late_theirs = _mlstm_fwd(proj, gbias, m_head_norm, [mine[n] for n in late],
                                                          name="mlstm_fwd")
    gathered.update(with_own_slot(late, late_theirs))
    W_mout = gathered["m_w_out"].reshape(D_MODEL, D_MODEL)
    A_in, A_out = _shards_to_cols(gathered["a_w_in"]), gathered["a_w_out"].reshape(D_MODEL, D_MODEL)
    W_up = gathered["f_w_up"]
    dsh = D_FF // N_CHIPS
    W_down = [gathered["f_w_down"][:, i * dsh:(i + 1) * dsh].reshape(D_FF, D_MODEL) for i in range(2)]
    zm0, h1, z1 = _mm(gated, W_mout, rows=_rows_sublayer_end(h0, norm_mix_post[0:1], norm_ffn_pre[0:1]), name="mlstm_out")
    u0, a0, (zf0, h2, y0) = ffn_fwd(0, z1, _rows_sublayer_end(h1, norm_ffn_post[0:1], norm_mix_pre[1:2]))

    aproj = _mm(y0, A_in, bias=b_in_full, name="attn_proj")
    sinks = a_sinks.reshape(A_KVH, A_G)
    ao = _attn_fwd(aproj, sinks, name="attn_fwd")
    zm1, h3, z3 = _mm(ao, A_out, bias=b_out_full, rows=_rows_sublayer_end(h2, norm_mix_post[1:2], norm_ffn_pre[1:2]),
                      name="attn_out")
    u1, a1, (zf1, dh, dzf1, g_post1, loss_lanes) = ffn_fwd(1, z3, _rows_last_sublayer_end(h3, norm_ffn_post[1:2], target))
    loss = lax.psum(loss_lanes[0, 0], ("x", "y", "c"))

    g_post, g_fpre, g_mpost, g_mpre = [None, g_post1], [None, None], [None, None], [None, None]
    dW_up, dW_down, dconv_w, dconv_b = [None, None], [None, None], [None, None], [None, None]

    def by_rows(g):
        return g.reshape(N_CHIPS, -1, g.shape[-1])

    def ffn_bwd(i, dzf, z1_, u, a, start, ride):
        da = _mm(dzf, W_down[i], tb=True, name=f"ffn_down_dx{i}")
        dW_down[i] = _mm(a, dzf, ta=True, name=f"ffn_down_dw{i}")
        riders = (list(ride) + [by_rows(dW_down[i])]) if ride is not None else []
        du, dconv_w[i], dconv_b[i], *got = _convgate_bwd(u, da, conv_w_full[i], f_conv_b[i:i + 1], riders,
                                                         name=f"ffn_act_bwd{i}")
        dW_up[i] = _mm(z1_, du, ta=True, out_stacked=True, name=f"ffn_up_dw{i}")
        return _mm(du, W_up, tb=True, b_layer=i, rows=start, name=f"ffn_up_dx{i}"), riders, got

    (dh3, g_fpre[1], dzm1, g_mpost[1], db_out), _, _ = ffn_bwd(
        1, dzf1, z3, u1, a1, _rows_sublayer_start_bwd(h3, dh, zm1, norm_ffn_pre[1:2], norm_mix_post[1:2]), None)
    dao = _mm(dzm1, A_out, tb=True, name="attn_out_dx")
    dA_out = _mm(ao, dzm1, ta=True, name="attn_out_dw")
    daq, dakv, dsinks, cs_q, cs_kv = _attn_bwd(aproj, sinks, dao, name="attn_bwd")
    db_in = jnp.concatenate([cs_q, cs_kv], axis=1)
    dh2, g_mpre[1], dzf0, g_post[0], _ = _mm(
        daq, A_in[:, :A_NQ], tb=True, add=_mm(dakv, A_in[:, A_NQ:], tb=True, name="attn_proj_kv_dx"),
        rows=_rows_sublayer_start_bwd(h2, dh3, zf0, norm_mix_pre[1:2], norm_ffn_post[0:1]), name="attn_proj_q_dx")
    dA_in = jnp.concatenate([_mm(y0, daq, ta=True, name="attn_proj_q_dw"),
                             _mm(y0, dakv, ta=True, name="attn_proj_kv_dw")], axis=1)

    c_arr = jnp.reshape(mc, (1,)).astype(jnp.int32)
    place_arr = jnp.stack([chip, mc]).astype(jnp.int32)

    def add_halves(tags, gs, got):
        return [_add_halves(g, t, c_arr, name=f"grad_add_halves_{tag}") for tag, g, t in zip(tags, gs, got)]

    (dh1, g_fpre[0], dzm0, g_mpost[0], _), rode, rode_got = ffn_bwd(
        0, dzf0, z1, u0, a0, _rows_sublayer_start_bwd(h1, dh2, zm0, norm_ffn_pre[0:1], norm_mix_post[0:1]),
        [_cols_to_shards(dA_in), by_rows(dA_out), dW_up[1], by_rows(dW_down[1])])
    rode_tags = ["a_w_in", "a_w_out", "f_w_up1", "f_w_down1", "f_w_down0"]
    dgated = _mm(dzm0, W_mout, tb=True, name="mlstm_out_dx")
    dW_mout = _mm(gated, dzm0, ta=True, name="mlstm_out_dw")
    next_tags, next_gs = ["f_w_up0", "m_w_out"], [dW_up[0], by_rows(dW_mout)]
    late_tags = rode_tags + next_tags
    late_part = (add_halves(rode_tags, rode, rode_got)
                 + add_halves(next_tags, next_gs, _swap_halves(next_gs, name="grad_swap_halves_late")))
    dproj, dgbias, dhn, *late_from = _mlstm_bwd(proj, gbias, m_head_norm, C_all, n_all, m_all, dgated, late_part,
                                                name="mlstm_bwd")
    dW_all = _mm(z0, dproj, ta=True, name="mlstm_proj_dw")
    dW_min = dW_all[:, :m_w_in.shape[-1] * N_CHIPS]

    early_tags, early_gs = ["m_w_in"], [_cols_to_shards(dW_min)]
    early_part = add_halves(early_tags, early_gs, _swap_halves(early_gs, name="grad_swap_halves_early"))
    grad_x, g_mpre[0], *early_from = _mm(
        dproj, W_all, tb=True, rows=_rows_first_sublayer_start_bwd(h0, dh1, norm_mix_pre[0:1]), exchange=early_part,
        name="mlstm_proj_dx")
    tags = early_tags + late_tags
    halves = [_sum_chips(q, r, place_arr, name=f"grad_sum_chips_{tag}")
              for tag, q, r in zip(tags, early_part + late_part, list(early_from) + list(late_from))]
    reduced = dict(zip(tags, _join_halves(halves, name="grad_join_halves")))
    layer_grads = dict(m_w_in=[reduced["m_w_in"]], m_w_out=[reduced["m_w_out"]], a_w_in=[reduced["a_w_in"]],
                       a_w_out=[reduced["a_w_out"]], f_w_up=[reduced["f_w_up0"], reduced["f_w_up1"]],
                       f_w_down=[reduced["f_w_down0"], reduced["f_w_down1"]])

    small_g = [
        dgbias[:, :2 * M_HEADS].reshape(1, 2, M_HEADS),
        dhn,
        dsinks.reshape(1, A_QH),
        db_in, db_out,
        jnp.concatenate(g_mpre), jnp.concatenate(g_mpost), jnp.concatenate(g_fpre), jnp.concatenate(g_post),
        jnp.stack(dconv_w), jnp.concatenate(dconv_b),
    ]
    small_names = ["m_gate_bias", "m_head_norm", "a_sinks", "a_b_in", "a_b_out", "norm_mix_pre", "norm_mix_post",
                   "norm_ffn_pre", "norm_ffn_post", "f_conv_w", "f_conv_b"]
    red = _unpack_small(_allreduce_small(_pack_small(small_g), name="reduce_small"), [t.shape for t in small_g])
    for n, t in zip(small_names, red):
        if n in ("a_b_in", "a_b_out", "f_conv_w"):
            axis = t.ndim - 1
            width = params[n].shape[axis]
            t = lax.dynamic_slice_in_dim(t, chip * width, width, axis)
        grads[n] = t

    deltas, new_m, new_v = {}, {}, {}
    for n in _BIG:
        grads[n], deltas[n], new_m[n], new_v[n] = _adamw_layers(params[n], layer_grads[n], mom1[n], mom2[n],
                                                                name=f"adamw_{n}")
    two = lambda t: t.reshape(-1, t.shape[-1])
    res = _adamw_small([(two(params[n]), two(grads[n]), two(mom1[n]), two(mom2[n])) for n in small_names],
                       name="adamw_small")
    for n, (d, nm, nv) in zip(small_names, res):
        sh = params[n].shape
        deltas[n], new_m[n], new_v[n] = d.reshape(sh), nm.reshape(sh), nv.reshape(sh)

    return (loss, grad_x[None], *[grads[n] for n in order], *[deltas[n] for n in order],
            *[new_m[n] for n in order], *[new_v[n] for n in order])
```

```python
import functools
import math

import numpy as np

import jax
import jax.numpy as jnp
from jax import lax
from jax.experimental import pallas as pl
from jax.experimental.pallas import tpu as pltpu

F32 = jnp.float32
MXU_DTYPE = jnp.bfloat16
WIRE_DTYPE = jnp.bfloat16
MESH = pl.DeviceIdType.MESH

D_MODEL = 1024
EPS = 1e-6
M_HEADS, M_QK, M_V, M_CHUNK = 8, 64, 128, 128
GATE_CAP = 15.0
A_DH, A_QH, A_KVH, A_G, A_BLK = 64, 16, 2, 8, 128
D_FF = 2816
N_CHIPS = 4
LANES = 128
VMEM_LIMIT = 56 * 1024 * 1024

ADAM_LR, ADAM_B1, ADAM_B2, ADAM_EPS, ADAM_WD, ADAM_STEP = 0.001, 0.9, 0.999, 1e-08, 0.01, 10


def _cparams(sem):
    return pltpu.CompilerParams(dimension_semantics=sem, vmem_limit_bytes=VMEM_LIMIT)


def _pick(n, cands):
    for c in cands:
        if n % c == 0:
            return c
    return n


class _Rows:
    def __init__(self, tiles, vecs, outs, fn):
        self.tiles, self.vecs, self.outs, self.fn = list(tiles), list(vecs), list(outs), fn


MM_VMEM_BUDGET = 46 * 1024 * 1024
ROWS_FULL_K = 4224
WIDE_N = 3200


def _mm(a, b, *, ta=False, tb=False, out_dtype=F32, bias=None, add=None, b_layer=None, out_stacked=False, rows=None,
        exchange=(), name):
    if ta:
        K, M = a.shape
    else:
        M, K = a.shape
    b_stacked = b_layer is not None
    if b_stacked:
        assert not ta
        n_sh = b.shape[2]
        w_rows, w_cols = D_MODEL, N_CHIPS * n_sh
        N, Kb = (w_rows, w_cols) if tb else (w_cols, w_rows)
    elif tb:
        N, Kb = b.shape
    else:
        Kb, N = b.shape
    assert K == Kb, (a.shape, b.shape)
    tn = _pick(N, (1024, 1408, 1280, 640, 512, 256, 128))
    tk = K if K <= 2816 else _pick(K, (2816, 2048, 1408, 1024, 512, 256, 128))
    if tn < 1024 <= N <= WIDE_N:
        tn, tk = N, min(tk, 1024)
    shards_per_step = 1
    if b_stacked and tb:
        shards_per_step = 2
        tk = shards_per_step * n_sh
    if (b_stacked and not tb) or out_stacked:
        tn = N // N_CHIPS
    if rows is not None:
        assert tn == N and not out_stacked and not ta
        if K <= ROWS_FULL_K:
            tk = K
    nk = K // tk

    def vmem_bytes(rows_per_tile):
        total = 2 * (rows_per_tile * tk * a.dtype.itemsize + tk * tn * b.dtype.itemsize)
        if rows is None:
            total += 2 * rows_per_tile * tn * jnp.dtype(out_dtype).itemsize
        else:
            total += 2 * rows_per_tile * N * (4 * len(rows.tiles)
                                              + sum(jnp.dtype(dt).itemsize for kind, dt in rows.outs if kind == "tile"))
        if add is not None:
            total += 2 * rows_per_tile * tn * 4
        return total + (rows_per_tile * tn * 4 if nk > 1 else 0)

    tm = next(c for c in (2048, 1024, 1408, 512, 256, 128, M) if M % c == 0 and (vmem_bytes(c) <= MM_VMEM_BUDGET or c <= 128))
    dn = (((0 if ta else 1,), (1 if tb else 0,)), ((), ()))
    has_bias, has_add = bias is not None, add is not None
    n_tiles, n_vecs, n_outs = (len(rows.tiles), len(rows.vecs), len(rows.outs)) if rows is not None else (0, 0, 1)
    n_ex = len(exchange)
    assert not n_ex or rows is not None
    grid = (M // tm, N // tn, nk)

    def body(*refs):
        a_ref, b_ref = refs[0], refs[1]
        pos = 2
        bias_ref = add_ref = None
        if has_bias:
            bias_ref = refs[pos]
            pos += 1
        if has_add:
            add_ref = refs[pos]
            pos += 1
        tile_refs, vec_refs = refs[pos:pos + n_tiles], refs[pos + n_tiles:pos + n_tiles + n_vecs]
        pos += n_tiles + n_vecs
        ex_in = refs[pos:pos + n_ex]
        pos += n_ex
        out_refs, ex_out = refs[pos:pos + n_outs], refs[pos + n_outs:pos + n_outs + n_ex]
        pos += n_outs + n_ex
        acc_ref = refs[pos] if nk > 1 else None
        if n_ex:
            ex = _Exchange(ex_in, ex_out, *refs[pos + (nk > 1):])
            step = (pl.program_id(0) * grid[1] + pl.program_id(1)) * grid[2] + pl.program_id(2)

            @pl.when(step == 0)
            def _():
                ex.issue()

        def finish(r):
            if has_bias:
                r = r + bias_ref[...]
            if has_add:
                r = r + add_ref[...]
            if rows is None:
                out_refs[0][...] = r.astype(out_dtype)
                return
            vals = rows.fn(r, [t[...] for t in tile_refs], [v[...] for v in vec_refs])
            first = pl.program_id(0) == 0
            for (kind, _), o_ref, val in zip(rows.outs, out_refs, vals):
                if kind == "tile":
                    o_ref[...] = val.astype(o_ref.dtype)
                else:
                    @pl.when(first)
                    def _(o_ref=o_ref, val=val):
                        o_ref[...] = val

                    @pl.when(jnp.logical_not(first))
                    def _(o_ref=o_ref, val=val):
                        o_ref[...] += val

        if shards_per_step > 1:
            part = sum(lax.dot_general(a_ref[:, s * n_sh:(s + 1) * n_sh], b_ref[s], dn, preferred_element_type=F32)
                       for s in range(shards_per_step))
        else:
            part = lax.dot_general(a_ref[...], b_ref[...], dn, preferred_element_type=F32)
        if nk == 1:
            finish(part)
        else:
            k = pl.program_id(2)

            @pl.when(k == 0)
            def _():
                acc_ref[...] = part

            @pl.when(k > 0)
            def _():
                acc_ref[...] += part

            @pl.when(k == nk - 1)
            def _():
                finish(acc_ref[...])

        if n_ex:
            @pl.when(step == grid[0] * grid[1] * grid[2] - 1)
            def _():
                ex.finish()

    a_spec = pl.BlockSpec((tk, tm), lambda i, j, k: (k, i)) if ta else pl.BlockSpec((tm, tk), lambda i, j, k: (i, k))
    if b_stacked and tb:
        off = b_layer * (w_rows // tn)
        b_spec = pl.BlockSpec((shards_per_step, tn, n_sh), lambda i, j, k: (k, off + j, 0))
    elif b_stacked:
        off = b_layer * (w_rows // tk)
        b_spec = pl.BlockSpec((None, tk, tn), lambda i, j, k: (j, off + k, 0))
    elif tb:
        b_spec = pl.BlockSpec((tn, tk), lambda i, j, k: (j, k))
    else:
        b_spec = pl.BlockSpec((tk, tn), lambda i, j, k: (k, j))
    if out_stacked:
        out_spec = pl.BlockSpec((None, tm, tn), lambda i, j, k: (j, i, 0))
        out_shape = jax.ShapeDtypeStruct((N_CHIPS, M, tn), out_dtype)
    else:
        out_spec = pl.BlockSpec((tm, tn), lambda i, j, k: (i, j))
        out_shape = jax.ShapeDtypeStruct((M, N), out_dtype)
    in_specs, args = [a_spec, b_spec], [a, b]
    if has_bias:
        in_specs.append(pl.BlockSpec((1, tn), lambda i, j, k: (0, j)))
        args.append(bias)
    if has_add:
        in_specs.append(pl.BlockSpec((tm, tn), lambda i, j, k: (i, j)))
        args.append(add)
    semantics = ("parallel", "parallel", "arbitrary")
    if rows is not None:
        tile_spec = pl.BlockSpec((tm, N), lambda i, j, k: (i, 0))
        vec_spec = pl.BlockSpec((1, N), lambda i, j, k: (0, 0))
        in_specs += [tile_spec] * n_tiles + [vec_spec] * n_vecs
        args += rows.tiles + rows.vecs
        out_spec = [tile_spec if kind == "tile" else vec_spec for kind, _ in rows.outs]
        out_shape = [jax.ShapeDtypeStruct((M, N) if kind == "tile" else (1, N), dt) for kind, dt in rows.outs]
        semantics = ("arbitrary", "arbitrary", "arbitrary")
    scratch = [pltpu.VMEM((tm, tn), F32)] if nk > 1 else []
    if n_ex:
        any_spec = pl.BlockSpec(memory_space=pl.ANY)
        in_specs += [any_spec] * n_ex
        args += list(exchange)
        out_spec = out_spec + [any_spec] * n_ex
        out_shape = out_shape + _Exchange.out_shape(exchange)
        scratch += _Exchange.semaphores(n_ex)
    return pl.pallas_call(
        body, name=name,
        grid=(M // tm, N // tn, nk),
        in_specs=in_specs,
        out_specs=out_spec,
        out_shape=out_shape,
        scratch_shapes=scratch,
        compiler_params=_cparams(semantics),
    )(*args)


ROW_TILE = 512


def _rms_fwd(x, g, shards, *, out_dtype, name):
    S, D = x.shape
    tm = _pick(S, (ROW_TILE, 256, 128))
    nsh, nt = len(shards), S // tm
    any_spec = pl.BlockSpec(memory_space=pl.ANY)

    def body(x_ref, g_ref, *rest):
        o_ref = rest[nsh]
        gather = _Gather(rest[:nsh], rest[nsh + 1:2 * nsh + 1], *rest[2 * nsh + 1:])

        @pl.when(pl.program_id(0) == 0)
        def _():
            gather.issue()

        o_ref[...] = _rms(x_ref[...], g_ref[...]).astype(out_dtype)

        @pl.when(pl.program_id(0) == nt - 1)
        def _():
            gather.finish()

    row = pl.BlockSpec((tm, D), lambda i: (i, 0))
    return pl.pallas_call(
        body, name=name, grid=(nt,),
        in_specs=[row, pl.BlockSpec((1, D), lambda i: (0, 0))] + [any_spec] * nsh, out_specs=[row] + [any_spec] * nsh,
        out_shape=[jax.ShapeDtypeStruct((S, D), out_dtype)] + _Gather.out_shape(shards),
        scratch_shapes=_Gather.semaphores(nsh),
        compiler_params=_cparams(("arbitrary",)),
    )(x, g, *shards)


def _rms(x, g):
    return x * lax.rsqrt(jnp.mean(x * x, axis=-1, keepdims=True) + EPS) * g


def _rms_vjp(x, g, dy):
    r = lax.rsqrt(jnp.mean(x * x, axis=-1, keepdims=True) + EPS)
    xh = x * r
    gd = dy * g
    dx = r * (gd - xh * jnp.mean(gd * xh, axis=-1, keepdims=True))
    return dx, jnp.sum(dy * xh, axis=0, keepdims=True)


def _rows_sublayer_end(res, g_post, g_pre_next):
    def fn(z, tiles, vecs):
        h = tiles[0] + _rms(z, vecs[0])
        return [z, h, _rms(h, vecs[1])]

    return _Rows([res], [g_post, g_pre_next], [("tile", F32), ("tile", F32), ("tile", MXU_DTYPE)], fn)


def _rows_last_sublayer_end(res, g_post, target):
    def fn(z, tiles, vecs):
        y = tiles[0] + _rms(z, vecs[0])
        err = y - tiles[1]
        dy = err * (1.0 / err.shape[-1])
        dz, dg = _rms_vjp(z, vecs[0], dy)
        loss = 0.5 * jnp.sum(jnp.mean(err * err, axis=-1, keepdims=True), axis=0, keepdims=True)
        return [z, dy, dz, dg, jnp.broadcast_to(loss, dg.shape)]

    return _Rows([res, target], [g_post], [("tile", F32), ("tile", F32), ("tile", MXU_DTYPE), ("vec", F32), ("vec", F32)], fn)


def _rows_sublayer_start_bwd(x, res, z_below, g_pre, g_post_below):
    def fn(dz, tiles, vecs):
        dx, dg_pre = _rms_vjp(tiles[0], vecs[0], dz)
        dh = tiles[1] + dx
        dzb, dg_post = _rms_vjp(tiles[2], vecs[1], dh)
        return [dh, dg_pre, dzb, dg_post, jnp.sum(dzb, axis=0, keepdims=True)]

    return _Rows([x, res, z_below], [g_pre, g_post_below],
                 [("tile", F32), ("vec", F32), ("tile", MXU_DTYPE), ("vec", F32), ("vec", F32)], fn)


def _rows_first_sublayer_start_bwd(x, res, g_pre):
    def fn(dz, tiles, vecs):
        dx, dg_pre = _rms_vjp(tiles[0], vecs[0], dz)
        return [tiles[1] + dx, dg_pre]

    return _Rows([x, res], [g_pre], [("tile", F32), ("vec", F32)], fn)


def _mx(t):
    return t.astype(MXU_DTYPE)


def _mxf(t):
    return t.astype(MXU_DTYPE).astype(F32)


def _rawdot(a, b, ca, cb):
    return lax.dot_general(_mx(a), _mx(b), (((ca,), (cb,)), ((), ())), preferred_element_type=F32)


@functools.partial(jax.custom_vjp, nondiff_argnums=(2, 3))
def _dot(a, b, ca, cb):
    return _rawdot(a, b, ca, cb)


def _dot_fwd(a, b, ca, cb):
    return _rawdot(a, b, ca, cb), (a, b)


def _dot_bwd(ca, cb, res, g):
    a, b = res
    bj = 1 if cb == 0 else 0
    ai = 0 if ca == 1 else 1
    da = _rawdot(g, b, 1, bj) if ca == 1 else _rawdot(b, g, bj, 1)
    db = _rawdot(a, g, ai, 0) if cb == 0 else _rawdot(g, a, 0, ai)
    return da.astype(a.dtype), db.astype(b.dtype)


_dot.defvjp(_dot_fwd, _dot_bwd)


def _softcap(z):
    return GATE_CAP * jnp.tanh(z / GATE_CAP)


def _log_sigmoid(z):
    return jnp.minimum(z, 0.0) - jnp.log(1.0 + jnp.exp(-jnp.abs(z)))


def _sigmoid(z):
    return 0.5 * jnp.tanh(0.5 * z) + 0.5


def _lane_col(t, lane_index):
    lane = lax.broadcasted_iota(jnp.int32, t.shape, 1)
    return jnp.sum(jnp.where(lane == lane_index, t, 0.0), axis=1, keepdims=True)


def _mlstm_gates(G, bias):
    L = G.shape[0]
    z = _softcap(G + bias)
    ig = z
    lf = _log_sigmoid(z)
    ti = lax.broadcasted_iota(jnp.int32, (L, L), 0)
    si = lax.broadcasted_iota(jnp.int32, (L, L), 1)
    tril = (si <= ti).astype(F32)
    b = lax.dot_general(tril, lf, (((1,), (0,)), ((), ())), precision=lax.Precision.HIGHEST, preferred_element_type=F32)
    bL = jnp.sum(lf, axis=0, keepdims=True)
    return ig, b, ig.T, b.T, bL


def _mlstm_head(h, q, k_pair, v, o, ig_all, b_all, igT, bT, bL_all, hn, C, n, m):
    L = q.shape[0]
    k = jnp.where(_own_lanes(h, k_pair.shape), k_pair, 0.0)
    ti = lax.broadcasted_iota(jnp.int32, (L, L), 0)
    si = lax.broadcasted_iota(jnp.int32, (L, L), 1)
    lower = (si <= ti)
    ig = _lane_col(ig_all, h)
    b = _lane_col(b_all, M_HEADS + h)
    ig_row = igT[h:h + 1, :]
    b_row = bT[M_HEADS + h:M_HEADS + h + 1, :]
    bL = _lane_col(bL_all, M_HEADS + h)
    inter = b + m
    dlog = jnp.where(lower, b - b_row + ig_row, -jnp.inf)
    m_t = lax.stop_gradient(jnp.maximum(inter, jnp.max(dlog, axis=-1, keepdims=True)))
    qs = q * (M_QK ** -0.5)
    w = _dot(qs, k, 1, 1) * jnp.exp(dlog - m_t)
    s_inter = jnp.exp(inter - m_t)
    num = _dot(w, v, 1, 0) + s_inter * _dot(qs, C, 1, 0)
    den = jnp.sum(w, axis=-1, keepdims=True) + s_inter * jnp.sum(_mxf(qs) * _mxf(n), axis=-1, keepdims=True)
    hout = num * (1.0 / jnp.maximum(jnp.abs(den), jnp.exp(-m_t)))
    tail = bL - b + ig
    m_new = lax.stop_gradient(jnp.maximum(bL + m, jnp.max(tail, axis=0, keepdims=True)))
    ws = jnp.exp(tail - m_new)
    decay = jnp.exp(bL + m - m_new)
    wk = ws * k
    C_new = decay * C + _dot(wk, v, 0, 0)
    n_new = decay * n + jnp.sum(_mxf(ws) * _mxf(k), axis=0, keepdims=True)
    hs = hout * lax.rsqrt(jnp.mean(hout * hout, axis=-1, keepdims=True) + EPS) * hn
    gated = _sigmoid(o) * hs
    return (gated, C_new, n_new), m_new


M_OFF_Q, M_OFF_K = 0, M_HEADS * M_QK
M_OFF_V = 2 * M_HEADS * M_QK
M_OFF_O = M_OFF_V + M_HEADS * M_V
M_OFF_G = M_OFF_O + M_HEADS * M_V
M_PROJ = M_OFF_G + LANES
M_PAIRS = M_HEADS * M_QK // LANES


def _head_cols(off, h):
    return slice(off + h * LANES, off + (h + 1) * LANES)


def _own_lanes(h, shape):
    low = lax.broadcasted_iota(jnp.int32, shape, 1) < M_QK
    return low if h % 2 == 0 else jnp.logical_not(low)


def _mlstm_specs(NC, rev):
    H, L = M_HEADS, M_CHUNK
    cc = (lambda c: NC - 1 - c) if rev else (lambda c: c)
    proj = pl.BlockSpec((L, M_PROJ), lambda c: (cc(c), 0))
    vec = pl.BlockSpec((1, LANES), lambda c: (0, 0))
    hn = pl.BlockSpec((1, H * M_V), lambda c: (0, 0))
    hv = pl.BlockSpec((L, H * M_V), lambda c: (cc(c), 0))
    Cs = pl.BlockSpec((None, H, LANES, M_V), lambda c: (cc(c), 0, 0, 0))
    ns = pl.BlockSpec((None, H, 1, LANES), lambda c: (cc(c), 0, 0, 0))
    ms = pl.BlockSpec((None, H, 1, 1), lambda c: (cc(c), 0, 0, 0))
    return proj, vec, hn, hv, Cs, ns, ms


_MLSTM_STATE = [pltpu.VMEM((M_HEADS, LANES, M_V), F32), pltpu.VMEM((M_HEADS, 1, LANES), F32),
                pltpu.VMEM((M_HEADS, 1, 1), F32)]


def _mlstm_fwd(proj, bias, hn, shards, *, name):
    S = proj.shape[0]
    H, NC = M_HEADS, S // M_CHUNK
    ps, vec, hns, hv, Cs, ns, ms = _mlstm_specs(NC, False)
    nsh = len(shards)
    any_spec = pl.BlockSpec(memory_space=pl.ANY)

    def body(p_ref, b_ref, hn_ref, *rest):
        gated_ref, C_all, n_all, m_all = rest[nsh:nsh + 4]
        C_s, n_s, m_s, send_sems, recv_sems = rest[2 * nsh + 4:]
        gather = _Gather(rest[:nsh], rest[nsh + 4:2 * nsh + 4], send_sems, recv_sems)

        @pl.when(pl.program_id(0) == 0)
        def _():
            C_s[...] = jnp.zeros_like(C_s)
            n_s[...] = jnp.zeros_like(n_s)
            m_s[...] = jnp.zeros_like(m_s)
            gather.issue()

        gate_terms = _mlstm_gates(p_ref[:, M_OFF_G:M_OFF_G + LANES], b_ref[...])
        for h in range(H):
            C, n, m = C_s[h], n_s[h], m_s[h]
            C_all[h] = C
            n_all[h] = n
            m_all[h] = m
            (gated, Cn, nn), mn = _mlstm_head(
                h, p_ref[:, _head_cols(M_OFF_Q, h // 2)], p_ref[:, _head_cols(M_OFF_K, h // 2)],
                p_ref[:, _head_cols(M_OFF_V, h)], p_ref[:, _head_cols(M_OFF_O, h)], *gate_terms,
                hn_ref[:, _head_cols(0, h)], C, n, m)
            gated_ref[:, _head_cols(0, h)] = gated.astype(gated_ref.dtype)
            C_s[h] = Cn
            n_s[h] = nn
            m_s[h] = mn

        @pl.when(pl.program_id(0) == NC - 1)
        def _():
            gather.finish()

    return pl.pallas_call(
        body, name=name, grid=(NC,),
        in_specs=[ps, vec, hns] + [any_spec] * nsh,
        out_specs=[hv, Cs, ns, ms] + [any_spec] * nsh,
        out_shape=[jax.ShapeDtypeStruct((S, H * M_V), MXU_DTYPE),
                   jax.ShapeDtypeStruct((NC, H, LANES, M_V), F32),
                   jax.ShapeDtypeStruct((NC, H, 1, LANES), F32),
                   jax.ShapeDtypeStruct((NC, H, 1, 1), F32)] + _Gather.out_shape(shards),
        scratch_shapes=list(_MLSTM_STATE) + _Gather.semaphores(nsh),
        compiler_params=_cparams(("arbitrary",)),
    )(proj, bias, hn, *shards)


def _mlstm_bwd(proj, bias, hn, C_all, n_all, m_all, dgated, qs, *, name):
    S = proj.shape[0]
    H, NC = M_HEADS, S // M_CHUNK
    ps, vec, hns, hv, Cs, ns, ms = _mlstm_specs(NC, True)
    nq = len(qs)
    any_spec = pl.BlockSpec(memory_space=pl.ANY)

    def body(p_ref, b_ref, hn_ref, C_ref, n_ref, m_ref, dg_ref, *rest):
        dp_ref, db_ref, dhn_ref = rest[nq:nq + 3]
        dC_s, dn_s, send_sems, recv_sems = rest[2 * nq + 3:]
        exchange = _Exchange(rest[:nq], rest[nq + 3:2 * nq + 3], send_sems, recv_sems)

        @pl.when(pl.program_id(0) == 0)
        def _():
            dC_s[...] = jnp.zeros_like(dC_s)
            dn_s[...] = jnp.zeros_like(dn_s)
            db_ref[...] = jnp.zeros_like(db_ref)
            dhn_ref[...] = jnp.zeros_like(dhn_ref)
            exchange.issue()

        gate_terms, gates_vjp = jax.vjp(_mlstm_gates, p_ref[:, M_OFF_G:M_OFF_G + LANES], b_ref[...])
        d_terms = [jnp.zeros_like(t) for t in gate_terms]
        for h in range(H):
            def head(q, k, v, o, *rest, h=h):
                return _mlstm_head(h, q, k, v, o, *rest, m_ref[h])

            prim = (p_ref[:, _head_cols(M_OFF_Q, h // 2)], p_ref[:, _head_cols(M_OFF_K, h // 2)],
                    p_ref[:, _head_cols(M_OFF_V, h)], p_ref[:, _head_cols(M_OFF_O, h)], *gate_terms,
                    hn_ref[:, _head_cols(0, h)], C_ref[h], n_ref[h])
            _, vjp, _ = jax.vjp(head, *prim, has_aux=True)
            dq, dk, dv, do, *d_gate, dhnh, dC, dn = vjp((dg_ref[:, _head_cols(0, h)].astype(F32), dC_s[h], dn_s[h]))
            if h % 2 == 0:
                dq_pair, dk_pair = dq, dk
            else:
                dp_ref[:, _head_cols(M_OFF_Q, h // 2)] = (dq_pair + dq).astype(dp_ref.dtype)
                dp_ref[:, _head_cols(M_OFF_K, h // 2)] = (dk_pair + dk).astype(dp_ref.dtype)
            dp_ref[:, _head_cols(M_OFF_V, h)] = dv.astype(dp_ref.dtype)
            dp_ref[:, _head_cols(M_OFF_O, h)] = do.astype(dp_ref.dtype)
            d_terms = [a + g for a, g in zip(d_terms, d_gate)]
            dhn_ref[:, _head_cols(0, h)] += dhnh
            dC_s[h] = dC
            dn_s[h] = dn
        dG, dbias = gates_vjp(tuple(d_terms))
        dp_ref[:, M_OFF_G:M_OFF_G + LANES] = dG.astype(dp_ref.dtype)
        db_ref[...] += dbias

        @pl.when(pl.program_id(0) == NC - 1)
        def _():
            exchange.finish()

    return pl.pallas_call(
        body, name=name, grid=(NC,),
        in_specs=[ps, vec, hns, Cs, ns, ms, hv] + [any_spec] * nq,
        out_specs=[ps, vec, hns] + [any_spec] * nq,
        out_shape=[jax.ShapeDtypeStruct((S, M_PROJ), MXU_DTYPE), jax.ShapeDtypeStruct((1, LANES), F32),
                   jax.ShapeDtypeStruct((1, H * M_V), F32)] + _Exchange.out_shape(qs),
        scratch_shapes=list(_MLSTM_STATE[:2]) + _Exchange.semaphores(nq),
        compiler_params=_cparams(("arbitrary",)),
    )(proj, bias, hn, C_all, n_all, m_all, dgated, *qs)


A_NQ = A_QH * A_DH
A_NKV = 2 * A_KVH * A_DH
A_PAIRS = A_G // 2


def _sink_softmax_parts(s, sink):
    mx = jnp.maximum(jnp.max(s, axis=-1, keepdims=True), sink)
    e = jnp.exp(s - mx)
    e_sink = jnp.exp(sink - mx)
    r = 1.0 / (jnp.sum(e, axis=-1, keepdims=True) + e_sink)
    return e * r, e_sink * r


@jax.custom_vjp
def _sink_softmax(s, sink):
    return _sink_softmax_parts(s, sink)[0]


def _sink_softmax_fwd(s, sink):
    p, p_sink = _sink_softmax_parts(s, sink)
    return p, (p, p_sink)


def _sink_softmax_bwd(res, dp):
    p, p_sink = res
    t = jnp.sum(dp * p, axis=-1, keepdims=True)
    return p * (dp - t), -p_sink * t


_sink_softmax.defvjp(_sink_softmax_fwd, _sink_softmax_bwd)


def _band_bias():
    qi = np.arange(A_G * A_BLK)[:, None] % A_BLK
    ku = np.arange(2 * A_BLK)[None, :]
    diff = qi - (ku - A_BLK)
    band = (diff >= 0) & (diff < A_BLK)
    seen = np.stack([band, band & (ku >= A_BLK)])
    return jnp.asarray(np.where(seen, 0.0, -np.inf), F32)


_BAND_BIAS_SPEC = pl.BlockSpec((None, A_G * A_BLK, 2 * A_BLK), lambda n: (jnp.where(n == 0, 1, 0), 0, 0))


def _attn_group(Ps, KLO, KHI, VLO, VHI, sinks, bias):
    B = Ps[0].shape[0]
    R = len(Ps) * B
    q2 = jnp.concatenate(Ps, axis=0) * (A_DH ** -0.5)
    s = jnp.concatenate([_dot(q2, KLO, 1, 1), _dot(q2, KHI, 1, 1)], axis=0) + bias
    ri = lax.broadcasted_iota(jnp.int32, (2 * R, A_G), 0)
    head = 2 * lax.shift_right_logical(ri & (R - 1), B.bit_length() - 1) + lax.shift_right_logical(ri, R.bit_length() - 1)
    onehot = head == lax.broadcasted_iota(jnp.int32, (2 * R, A_G), 1)
    sink = jnp.sum(jnp.where(onehot, sinks, 0.0), axis=1, keepdims=True)
    p = _sink_softmax(s, sink)
    o = _dot(p[:R], VLO, 1, 0) + _dot(p[R:], VHI, 1, 0)
    return tuple(o[j * B:(j + 1) * B] for j in range(len(Ps)))


def _swap_halves_of_lanes(t):
    return pltpu.roll(t, LANES // 2, 1)


def _kv_operands(kvp_ref, kvc_ref, h):
    kk = jnp.concatenate([kvp_ref[:, :LANES], kvc_ref[:, :LANES]], axis=0)
    vv = jnp.concatenate([kvp_ref[:, LANES:], kvc_ref[:, LANES:]], axis=0)
    low = lax.broadcasted_iota(jnp.int32, kk.shape, 1) < A_DH
    own = low if h == 0 else jnp.logical_not(low)
    k_own = jnp.where(own, kk, 0.0)
    v_own = jnp.where(own, vv, 0.0)
    k_oth, v_oth = _swap_halves_of_lanes(k_own), _swap_halves_of_lanes(v_own)
    if h == 0:
        return own, k_own, k_oth, v_own, v_oth
    return own, k_oth, k_own, v_oth, v_own


def _pair_cols(h, j):
    c = (h * A_PAIRS + j) * LANES
    return slice(c, c + LANES)


def _attn_fwd(proj, sinks, *, name):
    S = proj.shape[0]
    NB = S // A_BLK
    kv_blk = A_NQ // A_NKV
    qs = pl.BlockSpec((A_BLK, A_NQ), lambda n: (n, 0))
    cur = pl.BlockSpec((A_BLK, A_NKV), lambda n: (n, kv_blk))
    prev = pl.BlockSpec((A_BLK, A_NKV), lambda n: (jnp.maximum(n - 1, 0), kv_blk))
    sk = pl.BlockSpec((A_KVH, A_G), lambda n: (0, 0))

    def body(q_ref, kvp_ref, kvc_ref, s_ref, bias_ref, o_ref):
        for h in range(A_KVH):
            _, KLO, KHI, VLO, VHI = _kv_operands(kvp_ref, kvc_ref, h)
            Ps = tuple(q_ref[:, _pair_cols(h, j)] for j in range(A_PAIRS))
            outs = _attn_group(Ps, KLO, KHI, VLO, VHI, s_ref[h:h + 1, :], bias_ref[...])
            for j in range(A_PAIRS):
                o_ref[:, _pair_cols(h, j)] = outs[j].astype(o_ref.dtype)

    return pl.pallas_call(
        body, name=name, grid=(NB,),
        in_specs=[qs, prev, cur, sk, _BAND_BIAS_SPEC], out_specs=qs,
        out_shape=jax.ShapeDtypeStruct((S, A_NQ), MXU_DTYPE),
        compiler_params=_cparams(("parallel",)),
    )(proj, proj, proj, sinks, _band_bias())


def _attn_bwd(proj, sinks, do, *, name):
    S = proj.shape[0]
    NB = S // A_BLK
    last = NB - 1
    kv_blk = A_NQ // A_NKV
    qs = pl.BlockSpec((A_BLK, A_NQ), lambda n: (jnp.minimum(n, last), 0))
    cur = pl.BlockSpec((A_BLK, A_NKV), lambda n: (jnp.minimum(n, last), kv_blk))
    prev = pl.BlockSpec((A_BLK, A_NKV), lambda n: (jnp.clip(n - 1, 0, last), kv_blk))
    sk = pl.BlockSpec((A_KVH, A_G), lambda n: (0, 0))
    lag = pl.BlockSpec((A_BLK, A_NKV), lambda n: (jnp.maximum(n - 1, 0), 0))
    cq_spec = pl.BlockSpec((1, A_NQ), lambda n: (0, 0))
    ckv_spec = pl.BlockSpec((1, A_NKV), lambda n: (0, 0))

    def body(q_ref, kvp_ref, kvc_ref, s_ref, do_ref, bias_ref, dq_ref, dkv_ref, ds_ref, cq_ref, ckv_ref, keep):
        n = pl.program_id(0)

        @pl.when(n == 0)
        def _():
            keep[...] = jnp.zeros_like(keep)
            ds_ref[...] = jnp.zeros_like(ds_ref)
            cq_ref[...] = jnp.zeros_like(cq_ref)
            ckv_ref[...] = jnp.zeros_like(ckv_ref)

        @pl.when(n < NB)
        def _():
            f = functools.partial(_attn_group, bias=bias_ref[...])
            dk = jnp.zeros((2 * A_BLK, LANES), F32)
            dv = jnp.zeros((2 * A_BLK, LANES), F32)
            ds_rows = []
            for h in range(A_KVH):
                own, KLO, KHI, VLO, VHI = _kv_operands(kvp_ref, kvc_ref, h)
                Ps = tuple(q_ref[:, _pair_cols(h, j)] for j in range(A_PAIRS))
                _, vjp = jax.vjp(f, Ps, KLO, KHI, VLO, VHI, s_ref[h:h + 1, :])
                dPs, dKLO, dKHI, dVLO, dVHI, dsk = vjp(
                    tuple(do_ref[:, _pair_cols(h, j)].astype(F32) for j in range(A_PAIRS)))
                for j in range(A_PAIRS):
                    dq_ref[:, _pair_cols(h, j)] = dPs[j].astype(dq_ref.dtype)
                    cq_ref[:, _pair_cols(h, j)] += jnp.sum(dPs[j], axis=0, keepdims=True)
                dk_own, dk_oth = (dKLO, dKHI) if h == 0 else (dKHI, dKLO)
                dv_own, dv_oth = (dVLO, dVHI) if h == 0 else (dVHI, dVLO)
                dk = dk + jnp.where(own, dk_own, 0.0) + _swap_halves_of_lanes(jnp.where(own, 0.0, dk_oth))
                dv = dv + jnp.where(own, dv_own, 0.0) + _swap_halves_of_lanes(jnp.where(own, 0.0, dv_oth))
                ds_rows.append(dsk)
            ds_ref[...] += jnp.concatenate(ds_rows, axis=0)
            dkv = jnp.concatenate([dk, dv], axis=1)
            done = keep[...] + dkv[:A_BLK]
            dkv_ref[...] = done.astype(dkv_ref.dtype)
            ckv_ref[...] += jnp.sum(done, axis=0, keepdims=True)
            keep[...] = dkv[A_BLK:]

        @pl.when(n == NB)
        def _():
            done = keep[...]
            dkv_ref[...] = done.astype(dkv_ref.dtype)
            ckv_ref[...] += jnp.sum(done, axis=0, keepdims=True)

    return pl.pallas_call(
        body, name=name, grid=(NB + 1,),
        in_specs=[qs, prev, cur, sk, qs, _BAND_BIAS_SPEC],
        out_specs=[qs, lag, sk, cq_spec, ckv_spec],
        out_shape=[jax.ShapeDtypeStruct((S, A_NQ), MXU_DTYPE), jax.ShapeDtypeStruct((S, A_NKV), MXU_DTYPE),
                   jax.ShapeDtypeStruct((A_KVH, A_G), F32), jax.ShapeDtypeStruct((1, A_NQ), F32),
                   jax.ShapeDtypeStruct((1, A_NKV), F32)],
        scratch_shapes=[pltpu.VMEM((A_BLK, A_NKV), F32)],
        compiler_params=_cparams(("arbitrary",)),
    )(proj, proj, proj, sinks, do, _band_bias())


HALO = 8


def _shift_rows(t, j):
    return pltpu.roll(t, j % t.shape[0], 0)


def _conv_taps(gate_ext):
    return _shift_rows(gate_ext, 2), _shift_rows(gate_ext, 1), gate_ext


def _conv_gate(g2, g1, g0, cw, cb):
    return cb + cw[0:1, :] * g2 + cw[1:2, :] * g1 + cw[2:3, :] * g0


def _convgate_fwd(u, cw, cb, *, name):
    S, F2 = u.shape
    F = F2 // 2
    tm = _pick(S, (256, 128))
    hb = tm // HALO
    urow = pl.BlockSpec((tm, F2), lambda i: (i, 0))
    uprev = pl.BlockSpec((HALO, F), lambda i: (jnp.maximum(i * hb - 1, 0), 0))

    def body(u_ref, up_ref, cw_ref, cb_ref, a_ref):
        i = pl.program_id(0)
        prev = jnp.where(i > 0, up_ref[...], 0.0)
        gc = _conv_gate(*_conv_taps(jnp.concatenate([prev, u_ref[:, :F]], axis=0)), cw_ref[...], cb_ref[...])[HALO:]
        a_ref[...] = (gc * _sigmoid(gc) * u_ref[:, F:]).astype(a_ref.dtype)

    return pl.pallas_call(
        body, name=name, grid=(S // tm,),
        in_specs=[urow, uprev, pl.BlockSpec((3, F), lambda i: (0, 0)), pl.BlockSpec((1, F), lambda i: (0, 0))],
        out_specs=pl.BlockSpec((tm, F), lambda i: (i, 0)),
        out_shape=jax.ShapeDtypeStruct((S, F), MXU_DTYPE),
        compiler_params=_cparams(("parallel",)),
    )(u, u, cw, cb)


def _convgate_bwd(u, da, cw, cb, gs, *, name):
    S, F2 = u.shape
    ng = len(gs)
    any_spec = pl.BlockSpec(memory_space=pl.ANY)
    F = F2 // 2
    tm = _pick(S, (256, 128))
    hb = tm // HALO
    nt = S // tm
    nh = S // HALO
    urow = pl.BlockSpec((tm, F2), lambda i: (i, 0))
    uprev = pl.BlockSpec((HALO, F), lambda i: (jnp.maximum(i * hb - 1, 0), 0))
    unext = pl.BlockSpec((HALO, F2), lambda i: (jnp.minimum((i + 1) * hb, nh - 1), 0))
    darow = pl.BlockSpec((tm, F), lambda i: (i, 0))
    danext = pl.BlockSpec((HALO, F), lambda i: (jnp.minimum((i + 1) * hb, nh - 1), 0))

    def body(u_ref, up_ref, un_ref, da_ref, dan_ref, cw_ref, cb_ref, *rest):
        du_ref, dcw_ref, dcb_ref = rest[ng:ng + 3]
        swap = _Swap(rest[:ng], rest[ng + 3:2 * ng + 3], *rest[2 * ng + 3:]) if ng else None
        i = pl.program_id(0)
        cwv = cw_ref[...]
        prev = jnp.where(i > 0, up_ref[...], 0.0)
        gate_ext = jnp.concatenate([prev, u_ref[:, :F], un_ref[:, :F]], axis=0)
        val_ext = jnp.concatenate([u_ref[:, F:], un_ref[:, F:]], axis=0)
        da_next = jnp.where(i < nt - 1, dan_ref[...].astype(F32), 0.0)
        da_ext = jnp.concatenate([da_ref[...].astype(F32), da_next], axis=0)
        g2, g1, g0 = (t[HALO:] for t in _conv_taps(gate_ext))
        gc = _conv_gate(g2, g1, g0, cwv, cb_ref[...])
        sg = _sigmoid(gc)
        silu = gc * sg
        dval = da_ext * silu
        dgc = da_ext * val_ext * (sg * (1.0 + gc * (1.0 - sg)))
        dgate = cwv[2:3, :] * dgc + cwv[1:2, :] * _shift_rows(dgc, -1) + cwv[0:1, :] * _shift_rows(dgc, -2)
        du_ref[:, :F] = dgate[:tm].astype(du_ref.dtype)
        du_ref[:, F:] = dval[:tm].astype(du_ref.dtype)
        dgc_c = dgc[:tm]
        dcw = jnp.concatenate([jnp.sum(dgc_c * g2[:tm], axis=0, keepdims=True),
                               jnp.sum(dgc_c * g1[:tm], axis=0, keepdims=True),
                               jnp.sum(dgc_c * g0[:tm], axis=0, keepdims=True)], axis=0)
        dcb = jnp.sum(dgc_c, axis=0, keepdims=True)

        @pl.when(i == 0)
        def _():
            dcw_ref[...] = dcw
            dcb_ref[...] = dcb
            if ng:
                swap.issue()

        @pl.when(i > 0)
        def _():
            dcw_ref[...] += dcw
            dcb_ref[...] += dcb

        if ng:
            @pl.when(i == nt - 1)
            def _():
                swap.finish()

    return pl.pallas_call(
        body, name=name, grid=(nt,),
        in_specs=[urow, uprev, unext, darow, danext,
                  pl.BlockSpec((3, F), lambda i: (0, 0)), pl.BlockSpec((1, F), lambda i: (0, 0))] + [any_spec] * ng,
        out_specs=[urow, pl.BlockSpec((3, F), lambda i: (0, 0)), pl.BlockSpec((1, F), lambda i: (0, 0))] + [any_spec] * ng,
        out_shape=[jax.ShapeDtypeStruct((S, F2), MXU_DTYPE), jax.ShapeDtypeStruct((3, F), F32),
                   jax.ShapeDtypeStruct((1, F), F32)] + (_Swap.out_shape(gs) if ng else []),
        scratch_shapes=_Swap.semaphores(ng) if ng else [],
        compiler_params=_cparams(("arbitrary",)),
    )(u, u, u, da, da, cw, cb, *gs)


def _adamw_math(w, g, m, v):
    m = ADAM_B1 * m + (1.0 - ADAM_B1) * g
    v = ADAM_B2 * v + (1.0 - ADAM_B2) * (g * g)
    m_hat = m / (1.0 - ADAM_B1 ** ADAM_STEP)
    v_hat = v / (1.0 - ADAM_B2 ** ADAM_STEP)
    delta = -ADAM_LR * (m_hat / (jnp.sqrt(v_hat) + ADAM_EPS) + ADAM_WD * w)
    return delta, m, v


def _adamw_layers(w, gs, m, v, *, name):
    Lr, R, C = w.shape
    tr = _pick(R, (256, 128, 64, 32, 16, 8))
    outs = None
    for layer, g in enumerate(gs):
        def body(w_ref, g_ref, m_ref, v_ref, *rest):
            go_ref, d_ref, nm_ref, nv_ref = rest[-4:]
            gv = g_ref[...]
            d, nm, nv = _adamw_math(w_ref[...], gv, m_ref[...], v_ref[...])
            go_ref[...] = gv
            d_ref[...] = d
            nm_ref[...] = nm
            nv_ref[...] = nv

        lay = pl.BlockSpec((None, tr, C), lambda i, layer=layer: (layer, i, 0))
        in_specs = [lay, pl.BlockSpec((tr, C), lambda i: (i, 0)), lay, lay]
        args = [w, g, m, v]
        aliases = {}
        if outs is not None:
            in_specs += [pl.BlockSpec(memory_space=pl.ANY)] * 4
            args += list(outs)
            aliases = {4 + t: t for t in range(4)}
        outs = pl.pallas_call(
            body, name=f"{name}_{layer}", grid=(R // tr,), in_specs=in_specs, out_specs=[lay] * 4,
            out_shape=[jax.ShapeDtypeStruct((Lr, R, C), F32)] * 4, input_output_aliases=aliases,
            compiler_params=_cparams(("parallel",)),
        )(*args)
    return outs


def _adamw_small(items, *, name):
    n = len(items)

    def body(*refs):
        ins, outs = refs[:4 * n], refs[4 * n:]
        for t in range(n):
            w, g, m, v = (r[...] for r in ins[4 * t:4 * t + 4])
            d, nm, nv = _adamw_math(w, g, m, v)
            outs[3 * t][...] = d
            outs[3 * t + 1][...] = nm
            outs[3 * t + 2][...] = nv

    flat = [a for it in items for a in it]
    out_shape = [jax.ShapeDtypeStruct(it[0].shape, F32) for it in items for _ in range(3)]
    vm = pl.BlockSpec(memory_space=pltpu.VMEM)
    res = pl.pallas_call(body, name=name, in_specs=[vm] * len(flat), out_specs=[vm] * len(out_shape),
                         out_shape=out_shape)(*flat)
    return [tuple(res[3 * t:3 * t + 3]) for t in range(n)]


def _place():
    return lax.axis_index("x"), lax.axis_index("y"), lax.axis_index("c")


_FLIPS = ((1, 0), (0, 1), (1, 1))


ROW_ALIGN = 16


def _half(rows, which):
    return pl.ds(pl.multiple_of(which * (rows // 2), ROW_ALIGN), rows // 2)


def _remote(src, dst, send_sems, recv_sems, k, to):
    return pltpu.make_async_remote_copy(src_ref=src, dst_ref=dst, send_sem=send_sems.at[k], recv_sem=recv_sems.at[k],
                                        device_id=to, device_id_type=MESH)


class _Gather:
    def __init__(self, w_refs, out_refs, send_sems, recv_sems):
        self.w_refs, self.out_refs, self.send_sems, self.recv_sems = w_refs, out_refs, send_sems, recv_sems
        self.pairs = [(i, j) for i in range(len(w_refs)) for j in range(3)]

    @staticmethod
    def out_shape(shards):
        return [jax.ShapeDtypeStruct((N_CHIPS,) + s.shape, s.dtype) for s in shards]

    @staticmethod
    def semaphores(n):
        return [pltpu.SemaphoreType.DMA((6 * n,)), pltpu.SemaphoreType.DMA((6 * n,))]

    def _where(self):
        x, y, c = _place()
        return x, y, c, [(x ^ fx, y ^ fy) for fx, fy in _FLIPS]

    def _over_ici(self, i, j, landed):
        x, y, c, chips = self._where()
        px, py = chips[j]
        mine = _half(self.w_refs[i].shape[0], c)
        if landed:
            src = dst = self.out_refs[i].at[2 * px + py, mine]
        else:
            src, dst = self.w_refs[i].at[mine], self.out_refs[i].at[2 * x + y, mine]
        return _remote(src, dst, self.send_sems, self.recv_sems, 6 * i + j, (px, py, c))

    def _over_d2d(self, i, j, which):
        x, y, c, chips = self._where()
        px, py = chips[j]
        blk = self.out_refs[i].at[2 * px + py, _half(self.w_refs[i].shape[0], which)]
        return _remote(blk, blk, self.send_sems, self.recv_sems, 6 * i + 3 + j, (x, y, 1 - c))

    def issue(self):
        for i, j in self.pairs:
            self._over_ici(i, j, False).start()

    def finish(self):
        c = lax.axis_index("c")
        for i, j in self.pairs:
            self._over_ici(i, j, True).wait_recv()
            self._over_d2d(i, j, c).start()
        for i, j in self.pairs:
            self._over_d2d(i, j, 1 - c).wait_recv()
        for i, j in self.pairs:
            self._over_ici(i, j, False).wait_send()
            self._over_d2d(i, j, c).wait_send()


def _swap_halves(gs, *, name):
    n = len(gs)
    any_spec = pl.BlockSpec(memory_space=pl.ANY)

    def body(*refs):
        swap = _Swap(refs[:n], refs[n:2 * n], refs[2 * n], refs[2 * n + 1])
        swap.issue()
        swap.finish()

    return pl.pallas_call(
        body, name=name, in_specs=[any_spec] * n, out_specs=[any_spec] * n,
        out_shape=_Swap.out_shape(gs), scratch_shapes=_Swap.semaphores(n),
    )(*gs)


class _Swap:
    def __init__(self, g_refs, out_refs, send_sems, recv_sems):
        self.g_refs, self.out_refs, self.send_sems, self.recv_sems = g_refs, out_refs, send_sems, recv_sems

    @staticmethod
    def out_shape(gs):
        return [jax.ShapeDtypeStruct((N_CHIPS, g.shape[1] // 2, g.shape[2]), g.dtype) for g in gs]

    @staticmethod
    def semaphores(n):
        return [pltpu.SemaphoreType.DMA((n,)), pltpu.SemaphoreType.DMA((n,))]

    def _copies(self):
        x, y, c = _place()
        return [_remote(g.at[:, _half(g.shape[1], 1 - c)], out, self.send_sems, self.recv_sems, i, (x, y, 1 - c))
                for i, (g, out) in enumerate(zip(self.g_refs, self.out_refs))]

    def issue(self):
        for cp in self._copies():
            cp.start()

    def finish(self):
        for cp in self._copies():
            cp.wait()


def _add_halves(g, got, c_arr, *, name):
    _, rows, cols = g.shape
    blk = (None, rows // 2, cols)

    def body(c_ref, g_ref, got_ref, o_ref):
        o_ref[...] = (g_ref[...] + got_ref[...]).astype(o_ref.dtype)

    return pl.pallas_call(
        body, name=name,
        grid_spec=pltpu.PrefetchScalarGridSpec(
            num_scalar_prefetch=1, grid=(N_CHIPS,),
            in_specs=[pl.BlockSpec(blk, lambda j, c_ref: (j, c_ref[0], 0)), pl.BlockSpec(blk, lambda j, c_ref: (j, 0, 0))],
            out_specs=pl.BlockSpec(blk, lambda j, c_ref: (j, 0, 0))),
        out_shape=jax.ShapeDtypeStruct((N_CHIPS, rows // 2, cols), WIRE_DTYPE),
        compiler_params=_cparams(("parallel",)),
    )(c_arr, g, got)


class _Exchange:
    def __init__(self, q_refs, out_refs, send_sems, recv_sems):
        self.q_refs, self.out_refs, self.send_sems, self.recv_sems = q_refs, out_refs, send_sems, recv_sems
        self.pairs = [(i, j) for i in range(len(q_refs)) for j in range(3)]

    @staticmethod
    def out_shape(qs):
        return [jax.ShapeDtypeStruct(q.shape, q.dtype) for q in qs]

    @staticmethod
    def semaphores(n):
        return [pltpu.SemaphoreType.DMA((3 * n,)), pltpu.SemaphoreType.DMA((3 * n,))]

    def _copy(self, i, j, landed):
        x, y, c = _place()
        px, py = [(x ^ fx, y ^ fy) for fx, fy in _FLIPS][j]
        if landed:
            src = dst = self.out_refs[i].at[2 * px + py]
        else:
            src, dst = self.q_refs[i].at[2 * px + py], self.out_refs[i].at[2 * x + y]
        return _remote(src, dst, self.send_sems, self.recv_sems, 3 * i + j, (px, py, c))

    def issue(self):
        for i, j in self.pairs:
            self._copy(i, j, False).start()

    def finish(self):
        for i, j in self.pairs:
            self._copy(i, j, True).wait_recv()
        for i, j in self.pairs:
            self._copy(i, j, False).wait_send()


def _sum_chips(q, r, place_arr, *, name):
    _, h, cols = q.shape
    blk = (None, h, cols)

    def body(p_ref, q_ref, r1_ref, r2_ref, r3_ref, o_ref):
        o_ref[...] = ((q_ref[...].astype(F32) + r1_ref[...].astype(F32)) + r2_ref[...].astype(F32)) + r3_ref[...].astype(F32)

    other = [pl.BlockSpec(blk, lambda i, p_ref, f=f: (p_ref[0] ^ f, 0, 0)) for f in (1, 2, 3)]
    return pl.pallas_call(
        body, name=name,
        grid_spec=pltpu.PrefetchScalarGridSpec(
            num_scalar_prefetch=1, grid=(1,),
            in_specs=[pl.BlockSpec(blk, lambda i, p_ref: (p_ref[0], 0, 0))] + other,
            out_specs=pl.BlockSpec((h, cols), lambda i, p_ref: (p_ref[1], 0))),
        out_shape=jax.ShapeDtypeStruct((2 * h, cols), F32),
        compiler_params=_cparams(("arbitrary",)),
    )(place_arr, q, r, r, r)


def _join_halves(fs, *, name):
    n = len(fs)
    any_spec = pl.BlockSpec(memory_space=pl.ANY)

    def body(*refs):
        out_refs, send_sems, recv_sems = refs[n:2 * n], refs[2 * n], refs[2 * n + 1]
        x, y, c = _place()
        sent = []
        for i in range(n):
            mine = out_refs[i].at[_half(fs[i].shape[0], c)]
            cp = _remote(mine, mine, send_sems, recv_sems, i, (x, y, 1 - c))
            cp.start()
            sent.append(cp)
        for i in range(n):
            its = out_refs[i].at[_half(fs[i].shape[0], 1 - c)]
            _remote(its, its, send_sems, recv_sems, i, (x, y, 1 - c)).wait_recv()
        for cp in sent:
            cp.wait_send()

    return pl.pallas_call(
        body, name=name, in_specs=[any_spec] * n, out_specs=[any_spec] * n,
        out_shape=[jax.ShapeDtypeStruct(f.shape, f.dtype) for f in fs], input_output_aliases={i: i for i in range(n)},
        scratch_shapes=[pltpu.SemaphoreType.DMA((n,)), pltpu.SemaphoreType.DMA((n,))],
    )(*fs)


def _allreduce_small(buf, *, name):
    R = buf.shape[0]
    vm = pl.BlockSpec(memory_space=pltpu.VMEM)

    def body(b_ref, o_ref, slots, send_sems, recv_sems):
        x, y, c = _place()
        me = 4 * x + 2 * y + c
        slots[me] = b_ref[...]
        sends = []
        for kk in range(1, 8):
            fx, fy, fc = (kk >> 2) & 1, (kk >> 1) & 1, kk & 1
            cp = pltpu.make_async_remote_copy(
                src_ref=b_ref, dst_ref=slots.at[me], send_sem=send_sems.at[kk - 1], recv_sem=recv_sems.at[kk - 1],
                device_id=(x ^ fx, y ^ fy, c ^ fc), device_id_type=MESH)
            cp.start()
            sends.append(cp)
        for kk in range(1, 8):
            fx, fy, fc = (kk >> 2) & 1, (kk >> 1) & 1, kk & 1
            peer = 4 * (x ^ fx) + 2 * (y ^ fy) + (c ^ fc)
            pltpu.make_async_remote_copy(
                src_ref=b_ref, dst_ref=slots.at[peer], send_sem=send_sems.at[kk - 1], recv_sem=recv_sems.at[kk - 1],
                device_id=(x ^ fx, y ^ fy, c ^ fc), device_id_type=MESH).wait_recv()
        for cp in sends:
            cp.wait_send()
        acc = slots[0]
        for d in range(1, 8):
            acc = acc + slots[d]
        o_ref[...] = acc

    return pl.pallas_call(
        body, name=name, in_specs=[vm], out_specs=vm,
        out_shape=jax.ShapeDtypeStruct((R, LANES), F32),
        scratch_shapes=[pltpu.VMEM((8, R, LANES), F32), pltpu.SemaphoreType.DMA((7,)), pltpu.SemaphoreType.DMA((7,))],
    )(buf)


def _pad_rows(v, mult=8 * LANES):
    flat = v.reshape(-1)
    n = flat.shape[0]
    tot = -(-n // mult) * mult
    return jnp.pad(flat, (0, tot - n)).reshape(-1, LANES)


def _pack_small(parts):
    return jnp.concatenate([_pad_rows(p.astype(F32)) for p in parts], axis=0)


def _unpack_small(buf, shapes):
    out, r = [], 0
    for sh in shapes:
        n = math.prod(sh)
        rows = -(-n // (8 * LANES)) * 8
        out.append(buf[r:r + rows].reshape(-1)[:n].reshape(sh))
        r += rows
    return out


def _cols_to_shards(w):
    *lead, K, N = w.shape
    t = w.reshape(*lead, K, N_CHIPS, N // N_CHIPS)
    return jnp.moveaxis(t, -2, 0)


def _shards_to_cols(t):
    t = jnp.moveaxis(t, 0, -2)
    *lead, K, _, n = t.shape
    return t.reshape(*lead, K, N_CHIPS * n)


_BIG = ("m_w_in", "m_w_out", "a_w_in", "a_w_out", "f_w_up", "f_w_down")


def kernel(x, m_w_in, m_gate_bias, m_head_norm, m_w_out, a_w_in, a_b_in, a_sinks, a_w_out, a_b_out, norm_mix_pre, norm_mix_post, norm_ffn_pre, norm_ffn_post, f_w_up, f_conv_w, f_conv_b, f_w_down, loss_target, m_m_w_in, m_m_gate_bias, m_m_head_norm, m_m_w_out, m_a_w_in, m_a_b_in, m_a_sinks, m_a_w_out, m_a_b_out, m_norm_mix_pre, m_norm_mix_post, m_norm_ffn_pre, m_norm_ffn_post, m_f_w_up, m_f_conv_w, m_f_conv_b, m_f_w_down, v_m_w_in, v_m_gate_bias, v_m_head_norm, v_m_w_out, v_a_w_in, v_a_b_in, v_a_sinks, v_a_w_out, v_a_b_out, v_norm_mix_pre, v_norm_mix_post, v_norm_ffn_pre, v_norm_ffn_post, v_f_w_up, v_f_conv_w, v_f_conv_b, v_f_w_down):
    params = dict(m_w_in=m_w_in, m_gate_bias=m_gate_bias, m_head_norm=m_head_norm, m_w_out=m_w_out, a_w_in=a_w_in,
                  a_b_in=a_b_in, a_sinks=a_sinks, a_w_out=a_w_out, a_b_out=a_b_out, norm_mix_pre=norm_mix_pre,
                  norm_mix_post=norm_mix_post, norm_ffn_pre=norm_ffn_pre, norm_ffn_post=norm_ffn_post, f_w_up=f_w_up,
                  f_conv_w=f_conv_w, f_conv_b=f_conv_b, f_w_down=f_w_down)
    mom1 = dict(m_w_in=m_m_w_in, m_gate_bias=m_m_gate_bias, m_head_norm=m_m_head_norm, m_w_out=m_m_w_out,
                a_w_in=m_a_w_in, a_b_in=m_a_b_in, a_sinks=m_a_sinks, a_w_out=m_a_w_out, a_b_out=m_a_b_out,
                norm_mix_pre=m_norm_mix_pre, norm_mix_post=m_norm_mix_post, norm_ffn_pre=m_norm_ffn_pre,
                norm_ffn_post=m_norm_ffn_post, f_w_up=m_f_w_up, f_conv_w=m_f_conv_w, f_conv_b=m_f_conv_b,
                f_w_down=m_f_w_down)
    mom2 = dict(m_w_in=v_m_w_in, m_gate_bias=v_m_gate_bias, m_head_norm=v_m_head_norm, m_w_out=v_m_w_out,
                a_w_in=v_a_w_in, a_b_in=v_a_b_in, a_sinks=v_a_sinks, a_w_out=v_a_w_out, a_b_out=v_a_b_out,
                norm_mix_pre=v_norm_mix_pre, norm_mix_post=v_norm_mix_post, norm_ffn_pre=v_norm_ffn_pre,
                norm_ffn_post=v_norm_ffn_post, f_w_up=v_f_w_up, f_conv_w=v_f_conv_w, f_conv_b=v_f_conv_b,
                f_w_down=v_f_w_down)
    order = list(params)

    mx, my, mc = _place()
    chip = 2 * mx + my
    h0 = x[0]
    target = loss_target[0]

    def two_d(t):
        return t.reshape(-1, t.shape[-1])

    early = ("m_w_in",)
    late = tuple(n for n in _BIG if n not in early)
    mine = {n: two_d(params[n]).astype(MXU_DTYPE) for n in _BIG}

    def with_own_slot(names, theirs):
        return {n: lax.dynamic_update_slice(t, mine[n][None], (chip, 0, 0)) for n, t in zip(names, theirs)}

    z0, *early_theirs = _rms_fwd(h0, norm_mix_pre[0:1], [mine[n] for n in early], out_dtype=MXU_DTYPE,
                                 name="mix_pre_norm0")
    gathered = with_own_slot(early, early_theirs)

    def in_place(shard, axis):
        width = shard.shape[axis]
        z = jnp.zeros(shard.shape[:axis] + (N_CHIPS * width,) + shard.shape[axis + 1:], F32)
        contrib = jnp.where(mc == 0, shard, 0.0)
        return lax.dynamic_update_slice_in_dim(z, contrib, chip * width, axis)

    sm_in = [in_place(a_b_in, 1), in_place(a_b_out, 1), in_place(f_conv_w, 2)]
    sm_full = _unpack_small(_allreduce_small(_pack_small(sm_in), name="gather_small"), [t.shape for t in sm_in])
    b_in_full, b_out_full, conv_w_full = sm_full

    W_in = _shards_to_cols(gathered["m_w_in"])
    W_all = jnp.pad(W_in, ((0, 0), (0, M_PROJ - W_in.shape[1])))
    gbias = jnp.pad(m_gate_bias[0].reshape(1, 2 * M_HEADS), ((0, 0), (0, LANES - 2 * M_HEADS)))

    grads = {}

    def ffn_fwd(i, z1, end):
        u = _mm(z1, W_up, b_layer=i, name=f"ffn_up{i}")
        a = _convgate_fwd(u, conv_w_full[i], f_conv_b[i:i + 1], name=f"ffn_act{i}")
        return u, a, _mm(a, W_down[i], rows=end, name=f"ffn_down{i}")

    proj = _mm(z0, W_all, name="mlstm_proj")
    gated, C_all, n_all, m_all, *late_theirs = _mlstm_fwd(proj, gbias, m_head_norm, [mine[n] for n in late],
                                                          name="mlstm_fwd")
    gathered.update(with_own_slot(late, late_theirs))
    W_mout = gathered["m_w_out"].reshape(D_MODEL, D_MODEL)
    A_in, A_out = _shards_to_cols(gathered["a_w_in"]), gathered["a_w_out"].reshape(D_MODEL, D_MODEL)
    W_up = gathered["f_w_up"]
    dsh = D_FF // N_CHIPS
    W_down = [gathered["f_w_down"][:, i * dsh:(i + 1) * dsh].reshape(D_FF, D_MODEL) for i in range(2)]
    zm0, h1, z1 = _mm(gated, W_mout, rows=_rows_sublayer_end(h0, norm_mix_post[0:1], norm_ffn_pre[0:1]), name="mlstm_out")
    u0, a0, (zf0, h2, y0) = ffn_fwd(0, z1, _rows_sublayer_end(h1, norm_ffn_post[0:1], norm_mix_pre[1:2]))

    aproj = _mm(y0, A_in, bias=b_in_full, name="attn_proj")
    sinks = a_sinks.reshape(A_KVH, A_G)
    ao = _attn_fwd(aproj, sinks, name="attn_fwd")
    zm1, h3, z3 = _mm(ao, A_out, bias=b_out_full, rows=_rows_sublayer_end(h2, norm_mix_post[1:2], norm_ffn_pre[1:2]),
                      name="attn_out")
    u1, a1, (zf1, dh, dzf1, g_post1, loss_lanes) = ffn_fwd(1, z3, _rows_last_sublayer_end(h3, norm_ffn_post[1:2], target))
    loss = lax.psum(loss_lanes[0, 0], ("x", "y", "c"))

    g_post, g_fpre, g_mpost, g_mpre = [None, g_post1], [None, None], [None, None], [None, None]
    dW_up, dW_down, dconv_w, dconv_b = [None, None], [None, None], [None, None], [None, None]

    def by_rows(g):
        return g.reshape(N_CHIPS, -1, g.shape[-1])

    def ffn_bwd(i, dzf, z1_, u, a, start, ride):
        da = _mm(dzf, W_down[i], tb=True, name=f"ffn_down_dx{i}")
        dW_down[i] = _mm(a, dzf, ta=True, name=f"ffn_down_dw{i}")
        riders = (list(ride) + [by_rows(dW_down[i])]) if ride is not None else []
        du, dconv_w[i], dconv_b[i], *got = _convgate_bwd(u, da, conv_w_full[i], f_conv_b[i:i + 1], riders,
                                                         name=f"ffn_act_bwd{i}")
        dW_up[i] = _mm(z1_, du, ta=True, out_stacked=True, name=f"ffn_up_dw{i}")
        return _mm(du, W_up, tb=True, b_layer=i, rows=start, name=f"ffn_up_dx{i}"), riders, got

    (dh3, g_fpre[1], dzm1, g_mpost[1], db_out), _, _ = ffn_bwd(
        1, dzf1, z3, u1, a1, _rows_sublayer_start_bwd(h3, dh, zm1, norm_ffn_pre[1:2], norm_mix_post[1:2]), None)
    dao = _mm(dzm1, A_out, tb=True, name="attn_out_dx")
    dA_out = _mm(ao, dzm1, ta=True, name="attn_out_dw")
    daq, dakv, dsinks, cs_q, cs_kv = _attn_bwd(aproj, sinks, dao, name="attn_bwd")
    db_in = jnp.concatenate([cs_q, cs_kv], axis=1)
    dh2, g_mpre[1], dzf0, g_post[0], _ = _mm(
        daq, A_in[:, :A_NQ], tb=True, add=_mm(dakv, A_in[:, A_NQ:], tb=True, name="attn_proj_kv_dx"),
        rows=_rows_sublayer_start_bwd(h2, dh3, zf0, norm_mix_pre[1:2], norm_ffn_post[0:1]), name="attn_proj_q_dx")
    dA_in = jnp.concatenate([_mm(y0, daq, ta=True, name="attn_proj_q_dw"),
                             _mm(y0, dakv, ta=True, name="attn_proj_kv_dw")], axis=1)

    c_arr = jnp.reshape(mc, (1,)).astype(jnp.int32)
    place_arr = jnp.stack([chip, mc]).astype(jnp.int32)

    def add_halves(tags, gs, got):
        return [_add_halves(g, t, c_arr, name=f"grad_add_halves_{tag}") for tag, g, t in zip(tags, gs, got)]

    (dh1, g_fpre[0], dzm0, g_mpost[0], _), rode, rode_got = ffn_bwd(
        0, dzf0, z1, u0, a0, _rows_sublayer_start_bwd(h1, dh2, zm0, norm_ffn_pre[0:1], norm_mix_post[0:1]),
        [_cols_to_shards(dA_in), by_rows(dA_out), dW_up[1], by_rows(dW_down[1])])
    rode_tags = ["a_w_in", "a_w_out", "f_w_up1", "f_w_down1", "f_w_down0"]
    dgated = _mm(dzm0, W_mout, tb=True, name="mlstm_out_dx")
    dW_mout = _mm(gated, dzm0, ta=True, name="mlstm_out_dw")
    next_tags, next_gs = ["f_w_up0", "m_w_out"], [dW_up[0], by_rows(dW_mout)]
    late_tags = rode_tags + next_tags
    late_part = (add_halves(rode_tags, rode, rode_got)
                 + add_halves(next_tags, next_gs, _swap_halves(next_gs, name="grad_swap_halves_late")))
    dproj, dgbias, dhn, *late_from = _mlstm_bwd(proj, gbias, m_head_norm, C_all, n_all, m_all, dgated, late_part,
                                                name="mlstm_bwd")
    dW_all = _mm(z0, dproj, ta=True, name="mlstm_proj_dw")
    dW_min = dW_all[:, :m_w_in.shape[-1] * N_CHIPS]

    early_tags, early_gs = ["m_w_in"], [_cols_to_shards(dW_min)]
    early_part = add_halves(early_tags, early_gs, _swap_halves(early_gs, name="grad_swap_halves_early"))
    grad_x, g_mpre[0], *early_from = _mm(
        dproj, W_all, tb=True, rows=_rows_first_sublayer_start_bwd(h0, dh1, norm_mix_pre[0:1]), exchange=early_part,
        name="mlstm_proj_dx")
    tags = early_tags + late_tags
    halves = [_sum_chips(q, r, place_arr, name=f"grad_sum_chips_{tag}")
              for tag, q, r in zip(tags, early_part + late_part, list(early_from) + list(late_from))]
    reduced = dict(zip(tags, _join_halves(halves, name="grad_join_halves")))
    layer_grads = dict(m_w_in=[reduced["m_w_in"]], m_w_out=[reduced["m_w_out"]], a_w_in=[reduced["a_w_in"]],
                       a_w_out=[reduced["a_w_out"]], f_w_up=[reduced["f_w_up0"], reduced["f_w_up1"]],
                       f_w_down=[reduced["f_w_down0"], reduced["f_w_down1"]])

    small_g = [
        dgbias[:, :2 * M_HEADS].reshape(1, 2, M_HEADS),
        dhn,
        dsinks.reshape(1, A_QH),
        db_in, db_out,
        jnp.concatenate(g_mpre), jnp.concatenate(g_mpost), jnp.concatenate(g_fpre), jnp.concatenate(g_post),
        jnp.stack(dconv_w), jnp.concatenate(dconv_b),
    ]
    small_names = ["m_gate_bias", "m_head_norm", "a_sinks", "a_b_in", "a_b_out", "norm_mix_pre", "norm_mix_post",
                   "norm_ffn_pre", "norm_ffn_post", "f_conv_w", "f_conv_b"]
    red = _unpack_small(_allreduce_small(_pack_small(small_g), name="reduce_small"), [t.shape for t in small_g])
    for n, t in zip(small_names, red):
        if n in ("a_b_in", "a_b_out", "f_conv_w"):
            axis = t.ndim - 1
            width = params[n].shape[axis]
            t = lax.dynamic_slice_in_dim(t, chip * width, width, axis)
        grads[n] = t

    deltas, new_m, new_v = {}, {}, {}
    for n in _BIG:
        grads[n], deltas[n], new_m[n], new_v[n] = _adamw_layers(params[n], layer_grads[n], mom1[n], mom2[n],
                                                                name=f"adamw_{n}")
    two = lambda t: t.reshape(-1, t.shape[-1])
    res = _adamw_small([(two(params[n]), two(grads[n]), two(mom1[n]), two(mom2[n])) for n in small_names],
                       name="adamw_small")
    for n, (d, nm, nv) in zip(small_names, res):
        sh = params[n].shape
        deltas[n], new_m[n], new_v[n] = d.reshape(sh), nm.reshape(sh), nv.reshape(sh)

    return (loss, grad_x[None], *[grads[n] for n in order], *[deltas[n] for n in order],
            *[new_m[n] for n in order], *[new_v[n] for n in order])
```

```python
import functools
import math

import numpy as np

import jax
import jax.numpy as jnp
from jax import lax
from jax.experimental import pallas as pl
from jax.experimental.pallas import tpu as pltpu

F32 = jnp.float32
MXU_DTYPE = jnp.bfloat16
WIRE_DTYPE = jnp.bfloat16
MESH = pl.DeviceIdType.MESH

D_MODEL = 1024
EPS = 1e-6
M_HEADS, M_QK, M_V, M_CHUNK = 8, 64, 128, 128
GATE_CAP = 15.0
A_DH, A_QH, A_KVH, A_G, A_BLK = 64, 16, 2, 8, 128
D_FF = 2816
N_CHIPS = 4
LANES = 128
VMEM_LIMIT = 56 * 1024 * 1024

ADAM_LR, ADAM_B1, ADAM_B2, ADAM_EPS, ADAM_WD, ADAM_STEP = 0.001, 0.9, 0.999, 1e-08, 0.01, 10


def _cparams(sem):
    return pltpu.CompilerParams(dimension_semantics=sem, vmem_limit_bytes=VMEM_LIMIT)


def _pick(n, cands):
    for c in cands:
        if n % c == 0:
            return c
    return n


class _Rows:
    def __init__(self, tiles, vecs, outs, fn):
        self.tiles, self.vecs, self.outs, self.fn = list(tiles), list(vecs), list(outs), fn


MM_VMEM_BUDGET = 46 * 1024 * 1024
ROWS_FULL_K = 4224
WIDE_N = 3200


def _mm(a, b, *, ta=False, tb=False, out_dtype=F32, bias=None, add=None, b_layer=None, out_stacked=False, rows=None,
        ride=None, name):
    if ta:
        K, M = a.shape
    else:
        M, K = a.shape
    b_stacked = b_layer is not None
    if b_stacked:
        assert not ta
        n_sh = b.shape[2]
        w_rows, w_cols = D_MODEL, N_CHIPS * n_sh
        N, Kb = (w_rows, w_cols) if tb else (w_cols, w_rows)
    elif tb:
        N, Kb = b.shape
    else:
        Kb, N = b.shape
    assert K == Kb, (a.shape, b.shape)
    tn = _pick(N, (1024, 1408, 1280, 640, 512, 256, 128))
    tk = K if K <= 2816 else _pick(K, (2816, 2048, 1408, 1024, 512, 256, 128))
    if tn < 1024 <= N <= WIDE_N:
        tn, tk = N, min(tk, 1024)
    shards_per_step = 1
    if b_stacked and tb:
        shards_per_step = 2
        tk = shards_per_step * n_sh
    if (b_stacked and not tb) or out_stacked:
        tn = N // N_CHIPS
    if rows is not None:
        assert tn == N and not out_stacked and not ta
        if K <= ROWS_FULL_K:
            tk = K
    nk = K // tk

    def vmem_bytes(rows_per_tile):
        total = 2 * (rows_per_tile * tk * a.dtype.itemsize + tk * tn * b.dtype.itemsize)
        if rows is None:
            total += 2 * rows_per_tile * tn * jnp.dtype(out_dtype).itemsize
        else:
            total += 2 * rows_per_tile * N * (4 * len(rows.tiles)
                                              + sum(jnp.dtype(dt).itemsize for kind, dt in rows.outs if kind == "tile"))
        if add is not None:
            total += 2 * rows_per_tile * tn * 4
        return total + (rows_per_tile * tn * 4 if nk > 1 else 0)

    tm = next(c for c in (2048, 1024, 1408, 512, 256, 128, M) if M % c == 0 and (vmem_bytes(c) <= MM_VMEM_BUDGET or c <= 128))
    dn = (((0 if ta else 1,), (1 if tb else 0,)), ((), ()))
    has_bias, has_add = bias is not None, add is not None
    n_tiles, n_vecs, n_outs = (len(rows.tiles), len(rows.vecs), len(rows.outs)) if rows is not None else (0, 0, 1)
    rider, riding = ride if ride is not None else (None, ())
    n_ex = len(riding)
    assert not n_ex or rows is not None
    grid = (M // tm, N // tn, nk)

    def body(*refs):
        a_ref, b_ref = refs[0], refs[1]
        pos = 2
        bias_ref = add_ref = None
        if has_bias:
            bias_ref = refs[pos]
            pos += 1
        if has_add:
            add_ref = refs[pos]
            pos += 1
        tile_refs, vec_refs = refs[pos:pos + n_tiles], refs[pos + n_tiles:pos + n_tiles + n_vecs]
        pos += n_tiles + n_vecs
        ex_in = refs[pos:pos + n_ex]
        pos += n_ex
        out_refs, ex_out = refs[pos:pos + n_outs], refs[pos + n_outs:pos + n_outs + n_ex]
        pos += n_outs + n_ex
        acc_ref = refs[pos] if nk > 1 else None
        if n_ex:
            ex = rider(ex_in, ex_out, *refs[pos + (nk > 1):])
            step = (pl.program_id(0) * grid[1] + pl.program_id(1)) * grid[2] + pl.program_id(2)

            @pl.when(step == 0)
            def _():
                ex.issue()

        def finish(r):
            if has_bias:
                r = r + bias_ref[...]
            if has_add:
                r = r + add_ref[...]
            if rows is None:
                out_refs[0][...] = r.astype(out_dtype)
                return
            vals = rows.fn(r, [t[...] for t in tile_refs], [v[...] for v in vec_refs])
            first = pl.program_id(0) == 0
            for (kind, _), o_ref, val in zip(rows.outs, out_refs, vals):
                if kind == "tile":
                    o_ref[...] = val.astype(o_ref.dtype)
                else:
                    @pl.when(first)
                    def _(o_ref=o_ref, val=val):
                        o_ref[...] = val

                    @pl.when(jnp.logical_not(first))
                    def _(o_ref=o_ref, val=val):
                        o_ref[...] += val

        if shards_per_step > 1:
            part = sum(lax.dot_general(a_ref[:, s * n_sh:(s + 1) * n_sh], b_ref[s], dn, preferred_element_type=F32)
                       for s in range(shards_per_step))
        else:
            part = lax.dot_general(a_ref[...], b_ref[...], dn, preferred_element_type=F32)
        if nk == 1:
            finish(part)
        else:
            k = pl.program_id(2)

            @pl.when(k == 0)
            def _():
                acc_ref[...] = part

            @pl.when(k > 0)
            def _():
                acc_ref[...] += part

            @pl.when(k == nk - 1)
            def _():
                finish(acc_ref[...])

        if n_ex:
            @pl.when(step == grid[0] * grid[1] * grid[2] - 1)
            def _():
                ex.finish()

    a_spec = pl.BlockSpec((tk, tm), lambda i, j, k: (k, i)) if ta else pl.BlockSpec((tm, tk), lambda i, j, k: (i, k))
    if b_stacked and tb:
        off = b_layer * (w_rows // tn)
        b_spec = pl.BlockSpec((shards_per_step, tn, n_sh), lambda i, j, k: (k, off + j, 0))
    elif b_stacked:
        off = b_layer * (w_rows // tk)
        b_spec = pl.BlockSpec((None, tk, tn), lambda i, j, k: (j, off + k, 0))
    elif tb:
        b_spec = pl.BlockSpec((tn, tk), lambda i, j, k: (j, k))
    else:
        b_spec = pl.BlockSpec((tk, tn), lambda i, j, k: (k, j))
    if out_stacked:
        out_spec = pl.BlockSpec((None, tm, tn), lambda i, j, k: (j, i, 0))
        out_shape = jax.ShapeDtypeStruct((N_CHIPS, M, tn), out_dtype)
    else:
        out_spec = pl.BlockSpec((tm, tn), lambda i, j, k: (i, j))
        out_shape = jax.ShapeDtypeStruct((M, N), out_dtype)
    in_specs, args = [a_spec, b_spec], [a, b]
    if has_bias:
        in_specs.append(pl.BlockSpec((1, tn), lambda i, j, k: (0, j)))
        args.append(bias)
    if has_add:
        in_specs.append(pl.BlockSpec((tm, tn), lambda i, j, k: (i, j)))
        args.append(add)
    semantics = ("parallel", "parallel", "arbitrary")
    if rows is not None:
        tile_spec = pl.BlockSpec((tm, N), lambda i, j, k: (i, 0))
        vec_spec = pl.BlockSpec((1, N), lambda i, j, k: (0, 0))
        in_specs += [tile_spec] * n_tiles + [vec_spec] * n_vecs
        args += rows.tiles + rows.vecs
        out_spec = [tile_spec if kind == "tile" else vec_spec for kind, _ in rows.outs]
        out_shape = [jax.ShapeDtypeStruct((M, N) if kind == "tile" else (1, N), dt) for kind, dt in rows.outs]
        semantics = ("arbitrary", "arbitrary", "arbitrary")
    scratch = [pltpu.VMEM((tm, tn), F32)] if nk > 1 else []
    if n_ex:
        any_spec = pl.BlockSpec(memory_space=pl.ANY)
        in_specs += [any_spec] * n_ex
        args += list(riding)
        out_spec = out_spec + [any_spec] * n_ex
        out_shape = out_shape + rider.out_shape(riding)
        scratch += rider.semaphores(n_ex)
    return pl.pallas_call(
        body, name=name,
        grid=(M // tm, N // tn, nk),
        in_specs=in_specs,
        out_specs=out_spec,
        out_shape=out_shape,
        scratch_shapes=scratch,
        compiler_params=_cparams(semantics),
    )(*args)


ROW_TILE = 512


def _rms_fwd(x, g, shards, *, out_dtype, name):
    S, D = x.shape
    tm = _pick(S, (ROW_TILE, 256, 128))
    nsh, nt = len(shards), S // tm
    any_spec = pl.BlockSpec(memory_space=pl.ANY)

    def body(x_ref, g_ref, *rest):
        o_ref = rest[nsh]
        gather = _Gather(rest[:nsh], rest[nsh + 1:2 * nsh + 1], *rest[2 * nsh + 1:])

        @pl.when(pl.program_id(0) == 0)
        def _():
            gather.issue()

        o_ref[...] = _rms(x_ref[...], g_ref[...]).astype(out_dtype)

        @pl.when(pl.program_id(0) == nt - 1)
        def _():
            gather.finish()

    row = pl.BlockSpec((tm, D), lambda i: (i, 0))
    return pl.pallas_call(
        body, name=name, grid=(nt,),
        in_specs=[row, pl.BlockSpec((1, D), lambda i: (0, 0))] + [any_spec] * nsh, out_specs=[row] + [any_spec] * nsh,
        out_shape=[jax.ShapeDtypeStruct((S, D), out_dtype)] + _Gather.out_shape(shards),
        scratch_shapes=_Gather.semaphores(nsh),
        compiler_params=_cparams(("arbitrary",)),
    )(x, g, *shards)


def _rms(x, g):
    return x * lax.rsqrt(jnp.mean(x * x, axis=-1, keepdims=True) + EPS) * g


def _rms_vjp(x, g, dy):
    r = lax.rsqrt(jnp.mean(x * x, axis=-1, keepdims=True) + EPS)
    xh = x * r
    gd = dy * g
    dx = r * (gd - xh * jnp.mean(gd * xh, axis=-1, keepdims=True))
    return dx, jnp.sum(dy * xh, axis=0, keepdims=True)


def _rows_sublayer_end(res, g_post, g_pre_next):
    def fn(z, tiles, vecs):
        h = tiles[0] + _rms(z, vecs[0])
        return [z, h, _rms(h, vecs[1])]

    return _Rows([res], [g_post, g_pre_next], [("tile", F32), ("tile", F32), ("tile", MXU_DTYPE)], fn)


def _rows_last_sublayer_end(res, g_post, target):
    def fn(z, tiles, vecs):
        y = tiles[0] + _rms(z, vecs[0])
        err = y - tiles[1]
        dy = err * (1.0 / err.shape[-1])
        dz, dg = _rms_vjp(z, vecs[0], dy)
        loss = 0.5 * jnp.sum(jnp.mean(err * err, axis=-1, keepdims=True), axis=0, keepdims=True)
        return [z, dy, dz, dg, jnp.broadcast_to(loss, dg.shape)]

    return _Rows([res, target], [g_post], [("tile", F32), ("tile", F32), ("tile", MXU_DTYPE), ("vec", F32), ("vec", F32)], fn)


def _rows_sublayer_start_bwd(x, res, z_below, g_pre, g_post_below):
    def fn(dz, tiles, vecs):
        dx, dg_pre = _rms_vjp(tiles[0], vecs[0], dz)
        dh = tiles[1] + dx
        dzb, dg_post = _rms_vjp(tiles[2], vecs[1], dh)
        return [dh, dg_pre, dzb, dg_post, jnp.sum(dzb, axis=0, keepdims=True)]

    return _Rows([x, res, z_below], [g_pre, g_post_below],
                 [("tile", F32), ("vec", F32), ("tile", MXU_DTYPE), ("vec", F32), ("vec", F32)], fn)


def _rows_first_sublayer_start_bwd(x, res, g_pre):
    def fn(dz, tiles, vecs):
        dx, dg_pre = _rms_vjp(tiles[0], vecs[0], dz)
        return [tiles[1] + dx, dg_pre]

    return _Rows([x, res], [g_pre], [("tile", F32), ("vec", F32)], fn)


def _mx(t):
    return t.astype(MXU_DTYPE)


def _mxf(t):
    return t.astype(MXU_DTYPE).astype(F32)


def _rawdot(a, b, ca, cb):
    return lax.dot_general(_mx(a), _mx(b), (((ca,), (cb,)), ((), ())), preferred_element_type=F32)


@functools.partial(jax.custom_vjp, nondiff_argnums=(2, 3))
def _dot(a, b, ca, cb):
    return _rawdot(a, b, ca, cb)


def _dot_fwd(a, b, ca, cb):
    return _rawdot(a, b, ca, cb), (a, b)


def _dot_bwd(ca, cb, res, g):
    a, b = res
    bj = 1 if cb == 0 else 0
    ai = 0 if ca == 1 else 1
    da = _rawdot(g, b, 1, bj) if ca == 1 else _rawdot(b, g, bj, 1)
    db = _rawdot(a, g, ai, 0) if cb == 0 else _rawdot(g, a, 0, ai)
    return da.astype(a.dtype), db.astype(b.dtype)


_dot.defvjp(_dot_fwd, _dot_bwd)


def _softcap(z):
    return GATE_CAP * jnp.tanh(z / GATE_CAP)


def _log_sigmoid(z):
    return jnp.minimum(z, 0.0) - jnp.log(1.0 + jnp.exp(-jnp.abs(z)))


def _sigmoid(z):
    return 0.5 * jnp.tanh(0.5 * z) + 0.5


def _lane_col(t, lane_index):
    lane = lax.broadcasted_iota(jnp.int32, t.shape, 1)
    return jnp.sum(jnp.where(lane == lane_index, t, 0.0), axis=1, keepdims=True)


def _mlstm_gates(G, bias):
    L = G.shape[0]
    z = _softcap(G + bias)
    ig = z
    lf = _log_sigmoid(z)
    ti = lax.broadcasted_iota(jnp.int32, (L, L), 0)
    si = lax.broadcasted_iota(jnp.int32, (L, L), 1)
    tril = (si <= ti).astype(F32)
    b = lax.dot_general(tril, lf, (((1,), (0,)), ((), ())), precision=lax.Precision.HIGHEST, preferred_element_type=F32)
    bL = jnp.sum(lf, axis=0, keepdims=True)
    return ig, b, ig.T, b.T, bL


def _mlstm_head(h, q, k_pair, v, o, ig_all, b_all, igT, bT, bL_all, hn, C, n, m):
    L = q.shape[0]
    k = jnp.where(_own_lanes(h, k_pair.shape), k_pair, 0.0)
    ti = lax.broadcasted_iota(jnp.int32, (L, L), 0)
    si = lax.broadcasted_iota(jnp.int32, (L, L), 1)
    lower = (si <= ti)
    ig = _lane_col(ig_all, h)
    b = _lane_col(b_all, M_HEADS + h)
    ig_row = igT[h:h + 1, :]
    b_row = bT[M_HEADS + h:M_HEADS + h + 1, :]
    bL = _lane_col(bL_all, M_HEADS + h)
    inter = b + m
    dlog = jnp.where(lower, b - b_row + ig_row, -jnp.inf)
    m_t = lax.stop_gradient(jnp.maximum(inter, jnp.max(dlog, axis=-1, keepdims=True)))
    qs = q * (M_QK ** -0.5)
    w = _dot(qs, k, 1, 1) * jnp.exp(dlog - m_t)
    s_inter = jnp.exp(inter - m_t)
    num = _dot(w, v, 1, 0) + s_inter * _dot(qs, C, 1, 0)
    den = jnp.sum(w, axis=-1, keepdims=True) + s_inter * jnp.sum(_mxf(qs) * _mxf(n), axis=-1, keepdims=True)
    hout = num * (1.0 / jnp.maximum(jnp.abs(den), jnp.exp(-m_t)))
    tail = bL - b + ig
    m_new = lax.stop_gradient(jnp.maximum(bL + m, jnp.max(tail, axis=0, keepdims=True)))
    ws = jnp.exp(tail - m_new)
    decay = jnp.exp(bL + m - m_new)
    wk = ws * k
    C_new = decay * C + _dot(wk, v, 0, 0)
    n_new = decay * n + jnp.sum(_mxf(ws) * _mxf(k), axis=0, keepdims=True)
    hs = hout * lax.rsqrt(jnp.mean(hout * hout, axis=-1, keepdims=True) + EPS) * hn
    gated = _sigmoid(o) * hs
    return (gated, C_new, n_new), m_new


M_OFF_Q, M_OFF_K = 0, M_HEADS * M_QK
M_OFF_V = 2 * M_HEADS * M_QK
M_OFF_O = M_OFF_V + M_HEADS * M_V
M_OFF_G = M_OFF_O + M_HEADS * M_V
M_PROJ = M_OFF_G + LANES
M_PAIRS = M_HEADS * M_QK // LANES


def _head_cols(off, h):
    return slice(off + h * LANES, off + (h + 1) * LANES)


def _own_lanes(h, shape):
    low = lax.broadcasted_iota(jnp.int32, shape, 1) < M_QK
    return low if h % 2 == 0 else jnp.logical_not(low)


def _mlstm_specs(NC, rev):
    H, L = M_HEADS, M_CHUNK
    cc = (lambda c: NC - 1 - c) if rev else (lambda c: c)
    proj = pl.BlockSpec((L, M_PROJ), lambda c: (cc(c), 0))
    vec = pl.BlockSpec((1, LANES), lambda c: (0, 0))
    hn = pl.BlockSpec((1, H * M_V), lambda c: (0, 0))
    hv = pl.BlockSpec((L, H * M_V), lambda c: (cc(c), 0))
    Cs = pl.BlockSpec((None, H, LANES, M_V), lambda c: (cc(c), 0, 0, 0))
    ns = pl.BlockSpec((None, H, 1, LANES), lambda c: (cc(c), 0, 0, 0))
    ms = pl.BlockSpec((None, H, 1, 1), lambda c: (cc(c), 0, 0, 0))
    return proj, vec, hn, hv, Cs, ns, ms


_MLSTM_STATE = [pltpu.VMEM((M_HEADS, LANES, M_V), F32), pltpu.VMEM((M_HEADS, 1, LANES), F32),
                pltpu.VMEM((M_HEADS, 1, 1), F32)]


def _mlstm_fwd(proj, bias, hn, shards, *, name):
    S = proj.shape[0]
    H, NC = M_HEADS, S // M_CHUNK
    ps, vec, hns, hv, Cs, ns, ms = _mlstm_specs(NC, False)
    nsh = len(shards)
    any_spec = pl.BlockSpec(memory_space=pl.ANY)

    def body(p_ref, b_ref, hn_ref, *rest):
        gated_ref, C_all, n_all, m_all = rest[nsh:nsh + 4]
        C_s, n_s, m_s, send_sems, recv_sems = rest[2 * nsh + 4:]
        gather = _Gather(rest[:nsh], rest[nsh + 4:2 * nsh + 4], send_sems, recv_sems)

        @pl.when(pl.program_id(0) == 0)
        def _():
            C_s[...] = jnp.zeros_like(C_s)
            n_s[...] = jnp.zeros_like(n_s)
            m_s[...] = jnp.zeros_like(m_s)
            gather.issue()

        gate_terms = _mlstm_gates(p_ref[:, M_OFF_G:M_OFF_G + LANES], b_ref[...])
        for h in range(H):
            C, n, m = C_s[h], n_s[h], m_s[h]
            C_all[h] = C
            n_all[h] = n
            m_all[h] = m
            (gated, Cn, nn), mn = _mlstm_head(
                h, p_ref[:, _head_cols(M_OFF_Q, h // 2)], p_ref[:, _head_cols(M_OFF_K, h // 2)],
                p_ref[:, _head_cols(M_OFF_V, h)], p_ref[:, _head_cols(M_OFF_O, h)], *gate_terms,
                hn_ref[:, _head_cols(0, h)], C, n, m)
            gated_ref[:, _head_cols(0, h)] = gated.astype(gated_ref.dtype)
            C_s[h] = Cn
            n_s[h] = nn
            m_s[h] = mn

        @pl.when(pl.program_id(0) == NC - 1)
        def _():
            gather.finish()

    return pl.pallas_call(
        body, name=name, grid=(NC,),
        in_specs=[ps, vec, hns] + [any_spec] * nsh,
        out_specs=[hv, Cs, ns, ms] + [any_spec] * nsh,
        out_shape=[jax.ShapeDtypeStruct((S, H * M_V), MXU_DTYPE),
                   jax.ShapeDtypeStruct((NC, H, LANES, M_V), F32),
                   jax.ShapeDtypeStruct((NC, H, 1, LANES), F32),
                   jax.ShapeDtypeStruct((NC, H, 1, 1), F32)] + _Gather.out_shape(shards),
        scratch_shapes=list(_MLSTM_STATE) + _Gather.semaphores(nsh),
        compiler_params=_cparams(("arbitrary",)),
    )(proj, bias, hn, *shards)


def _mlstm_bwd(proj, bias, hn, C_all, n_all, m_all, dgated, qs, *, name):
    S = proj.shape[0]
    H, NC = M_HEADS, S // M_CHUNK
    ps, vec, hns, hv, Cs, ns, ms = _mlstm_specs(NC, True)
    nq = len(qs)
    any_spec = pl.BlockSpec(memory_space=pl.ANY)

    def body(p_ref, b_ref, hn_ref, C_ref, n_ref, m_ref, dg_ref, *rest):
        dp_ref, db_ref, dhn_ref = rest[nq:nq + 3]
        dC_s, dn_s, send_sems, recv_sems = rest[2 * nq + 3:]
        exchange = _Exchange(rest[:nq], rest[nq + 3:2 * nq + 3], send_sems, recv_sems)

        @pl.when(pl.program_id(0) == 0)
        def _():
            dC_s[...] = jnp.zeros_like(dC_s)
            dn_s[...] = jnp.zeros_like(dn_s)
            db_ref[...] = jnp.zeros_like(db_ref)
            dhn_ref[...] = jnp.zeros_like(dhn_ref)
            exchange.issue()

        gate_terms, gates_vjp = jax.vjp(_mlstm_gates, p_ref[:, M_OFF_G:M_OFF_G + LANES], b_ref[...])
        d_terms = [jnp.zeros_like(t) for t in gate_terms]
        for h in range(H):
            def head(q, k, v, o, *rest, h=h):
                return _mlstm_head(h, q, k, v, o, *rest, m_ref[h])

            prim = (p_ref[:, _head_cols(M_OFF_Q, h // 2)], p_ref[:, _head_cols(M_OFF_K, h // 2)],
                    p_ref[:, _head_cols(M_OFF_V, h)], p_ref[:, _head_cols(M_OFF_O, h)], *gate_terms,
                    hn_ref[:, _head_cols(0, h)], C_ref[h], n_ref[h])
            _, vjp, _ = jax.vjp(head, *prim, has_aux=True)
            dq, dk, dv, do, *d_gate, dhnh, dC, dn = vjp((dg_ref[:, _head_cols(0, h)].astype(F32), dC_s[h], dn_s[h]))
            if h % 2 == 0:
                dq_pair, dk_pair = dq, dk
            else:
                dp_ref[:, _head_cols(M_OFF_Q, h // 2)] = (dq_pair + dq).astype(dp_ref.dtype)
                dp_ref[:, _head_cols(M_OFF_K, h // 2)] = (dk_pair + dk).astype(dp_ref.dtype)
            dp_ref[:, _head_cols(M_OFF_V, h)] = dv.astype(dp_ref.dtype)
            dp_ref[:, _head_cols(M_OFF_O, h)] = do.astype(dp_ref.dtype)
            d_terms = [a + g for a, g in zip(d_terms, d_gate)]
            dhn_ref[:, _head_cols(0, h)] += dhnh
            dC_s[h] = dC
            dn_s[h] = dn
        dG, dbias = gates_vjp(tuple(d_terms))
        dp_ref[:, M_OFF_G:M_OFF_G + LANES] = dG.astype(dp_ref.dtype)
        db_ref[...] += dbias

        @pl.when(pl.program_id(0) == NC - 1)
        def _():
            exchange.finish()

    return pl.pallas_call(
        body, name=name, grid=(NC,),
        in_specs=[ps, vec, hns, Cs, ns, ms, hv] + [any_spec] * nq,
        out_specs=[ps, vec, hns] + [any_spec] * nq,
        out_shape=[jax.ShapeDtypeStruct((S, M_PROJ), MXU_DTYPE), jax.ShapeDtypeStruct((1, LANES), F32),
                   jax.ShapeDtypeStruct((1, H * M_V), F32)] + _Exchange.out_shape(qs),
        scratch_shapes=list(_MLSTM_STATE[:2]) + _Exchange.semaphores(nq),
        compiler_params=_cparams(("arbitrary",)),
    )(proj, bias, hn, C_all, n_all, m_all, dgated, *qs)


A_NQ = A_QH * A_DH
A_NKV = 2 * A_KVH * A_DH
A_PAIRS = A_G // 2


def _sink_softmax_parts(s, sink):
    mx = jnp.maximum(jnp.max(s, axis=-1, keepdims=True), sink)
    e = jnp.exp(s - mx)
    e_sink = jnp.exp(sink - mx)
    r = 1.0 / (jnp.sum(e, axis=-1, keepdims=True) + e_sink)
    return e * r, e_sink * r


@jax.custom_vjp
def _sink_softmax(s, sink):
    return _sink_softmax_parts(s, sink)[0]


def _sink_softmax_fwd(s, sink):
    p, p_sink = _sink_softmax_parts(s, sink)
    return p, (p, p_sink)


def _sink_softmax_bwd(res, dp):
    p, p_sink = res
    t = jnp.sum(dp * p, axis=-1, keepdims=True)
    return p * (dp - t), -p_sink * t


_sink_softmax.defvjp(_sink_softmax_fwd, _sink_softmax_bwd)


def _band_bias():
    qi = np.arange(A_G * A_BLK)[:, None] % A_BLK
    ku = np.arange(2 * A_BLK)[None, :]
    diff = qi - (ku - A_BLK)
    band = (diff >= 0) & (diff < A_BLK)
    seen = np.stack([band, band & (ku >= A_BLK)])
    return jnp.asarray(np.where(seen, 0.0, -np.inf), F32)


_BAND_BIAS_SPEC = pl.BlockSpec((None, A_G * A_BLK, 2 * A_BLK), lambda n: (jnp.where(n == 0, 1, 0), 0, 0))


def _attn_group(Ps, KLO, KHI, VLO, VHI, sinks, bias):
    B = Ps[0].shape[0]
    R = len(Ps) * B
    q2 = jnp.concatenate(Ps, axis=0) * (A_DH ** -0.5)
    s = jnp.concatenate([_dot(q2, KLO, 1, 1), _dot(q2, KHI, 1, 1)], axis=0) + bias
    ri = lax.broadcasted_iota(jnp.int32, (2 * R, A_G), 0)
    head = 2 * lax.shift_right_logical(ri & (R - 1), B.bit_length() - 1) + lax.shift_right_logical(ri, R.bit_length() - 1)
    onehot = head == lax.broadcasted_iota(jnp.int32, (2 * R, A_G), 1)
    sink = jnp.sum(jnp.where(onehot, sinks, 0.0), axis=1, keepdims=True)
    p = _sink_softmax(s, sink)
    o = _dot(p[:R], VLO, 1, 0) + _dot(p[R:], VHI, 1, 0)
    return tuple(o[j * B:(j + 1) * B] for j in range(len(Ps)))


def _swap_halves_of_lanes(t):
    return pltpu.roll(t, LANES // 2, 1)


def _kv_operands(kvp_ref, kvc_ref, h):
    kk = jnp.concatenate([kvp_ref[:, :LANES], kvc_ref[:, :LANES]], axis=0)
    vv = jnp.concatenate([kvp_ref[:, LANES:], kvc_ref[:, LANES:]], axis=0)
    low = lax.broadcasted_iota(jnp.int32, kk.shape, 1) < A_DH
    own = low if h == 0 else jnp.logical_not(low)
    k_own = jnp.where(own, kk, 0.0)
    v_own = jnp.where(own, vv, 0.0)
    k_oth, v_oth = _swap_halves_of_lanes(k_own), _swap_halves_of_lanes(v_own)
    if h == 0:
        return own, k_own, k_oth, v_own, v_oth
    return own, k_oth, k_own, v_oth, v_own


def _pair_cols(h, j):
    c = (h * A_PAIRS + j) * LANES
    return slice(c, c + LANES)


def _attn_fwd(proj, sinks, *, name):
    S = proj.shape[0]
    NB = S // A_BLK
    kv_blk = A_NQ // A_NKV
    qs = pl.BlockSpec((A_BLK, A_NQ), lambda n: (n, 0))
    cur = pl.BlockSpec((A_BLK, A_NKV), lambda n: (n, kv_blk))
    prev = pl.BlockSpec((A_BLK, A_NKV), lambda n: (jnp.maximum(n - 1, 0), kv_blk))
    sk = pl.BlockSpec((A_KVH, A_G), lambda n: (0, 0))

    def body(q_ref, kvp_ref, kvc_ref, s_ref, bias_ref, o_ref):
        for h in range(A_KVH):
            _, KLO, KHI, VLO, VHI = _kv_operands(kvp_ref, kvc_ref, h)
            Ps = tuple(q_ref[:, _pair_cols(h, j)] for j in range(A_PAIRS))
            outs = _attn_group(Ps, KLO, KHI, VLO, VHI, s_ref[h:h + 1, :], bias_ref[...])
            for j in range(A_PAIRS):
                o_ref[:, _pair_cols(h, j)] = outs[j].astype(o_ref.dtype)

    return pl.pallas_call(
        body, name=name, grid=(NB,),
        in_specs=[qs, prev, cur, sk, _BAND_BIAS_SPEC], out_specs=qs,
        out_shape=jax.ShapeDtypeStruct((S, A_NQ), MXU_DTYPE),
        compiler_params=_cparams(("parallel",)),
    )(proj, proj, proj, sinks, _band_bias())


def _attn_bwd(proj, sinks, do, *, name):
    S = proj.shape[0]
    NB = S // A_BLK
    last = NB - 1
    kv_blk = A_NQ // A_NKV
    qs = pl.BlockSpec((A_BLK, A_NQ), lambda n: (jnp.minimum(n, last), 0))
    cur = pl.BlockSpec((A_BLK, A_NKV), lambda n: (jnp.minimum(n, last), kv_blk))
    prev = pl.BlockSpec((A_BLK, A_NKV), lambda n: (jnp.clip(n - 1, 0, last), kv_blk))
    sk = pl.BlockSpec((A_KVH, A_G), lambda n: (0, 0))
    lag = pl.BlockSpec((A_BLK, A_NKV), lambda n: (jnp.maximum(n - 1, 0), 0))
    cq_spec = pl.BlockSpec((1, A_NQ), lambda n: (0, 0))
    ckv_spec = pl.BlockSpec((1, A_NKV), lambda n: (0, 0))

    def body(q_ref, kvp_ref, kvc_ref, s_ref, do_ref, bias_ref, dq_ref, dkv_ref, ds_ref, cq_ref, ckv_ref, keep):
        n = pl.program_id(0)

        @pl.when(n == 0)
        def _():
            keep[...] = jnp.zeros_like(keep)
            ds_ref[...] = jnp.zeros_like(ds_ref)
            cq_ref[...] = jnp.zeros_like(cq_ref)
            ckv_ref[...] = jnp.zeros_like(ckv_ref)

        @pl.when(n < NB)
        def _():
            f = functools.partial(_attn_group, bias=bias_ref[...])
            dk = jnp.zeros((2 * A_BLK, LANES), F32)
            dv = jnp.zeros((2 * A_BLK, LANES), F32)
            ds_rows = []
            for h in range(A_KVH):
                own, KLO, KHI, VLO, VHI = _kv_operands(kvp_ref, kvc_ref, h)
                Ps = tuple(q_ref[:, _pair_cols(h, j)] for j in range(A_PAIRS))
                _, vjp = jax.vjp(f, Ps, KLO, KHI, VLO, VHI, s_ref[h:h + 1, :])
                dPs, dKLO, dKHI, dVLO, dVHI, dsk = vjp(
                    tuple(do_ref[:, _pair_cols(h, j)].astype(F32) for j in range(A_PAIRS)))
                for j in range(A_PAIRS):
                    dq_ref[:, _pair_cols(h, j)] = dPs[j].astype(dq_ref.dtype)
                    cq_ref[:, _pair_cols(h, j)] += jnp.sum(dPs[j], axis=0, keepdims=True)
                dk_own, dk_oth = (dKLO, dKHI) if h == 0 else (dKHI, dKLO)
                dv_own, dv_oth = (dVLO, dVHI) if h == 0 else (dVHI, dVLO)
                dk = dk + jnp.where(own, dk_own, 0.0) + _swap_halves_of_lanes(jnp.where(own, 0.0, dk_oth))
                dv = dv + jnp.where(own, dv_own, 0.0) + _swap_halves_of_lanes(jnp.where(own, 0.0, dv_oth))
                ds_rows.append(dsk)
            ds_ref[...] += jnp.concatenate(ds_rows, axis=0)
            dkv = jnp.concatenate([dk, dv], axis=1)
            done = keep[...] + dkv[:A_BLK]
            dkv_ref[...] = done.astype(dkv_ref.dtype)
            ckv_ref[...] += jnp.sum(done, axis=0, keepdims=True)
            keep[...] = dkv[A_BLK:]

        @pl.when(n == NB)
        def _():
            done = keep[...]
            dkv_ref[...] = done.astype(dkv_ref.dtype)
            ckv_ref[...] += jnp.sum(done, axis=0, keepdims=True)

    return pl.pallas_call(
        body, name=name, grid=(NB + 1,),
        in_specs=[qs, prev, cur, sk, qs, _BAND_BIAS_SPEC],
        out_specs=[qs, lag, sk, cq_spec, ckv_spec],
        out_shape=[jax.ShapeDtypeStruct((S, A_NQ), MXU_DTYPE), jax.ShapeDtypeStruct((S, A_NKV), MXU_DTYPE),
                   jax.ShapeDtypeStruct((A_KVH, A_G), F32), jax.ShapeDtypeStruct((1, A_NQ), F32),
                   jax.ShapeDtypeStruct((1, A_NKV), F32)],
        scratch_shapes=[pltpu.VMEM((A_BLK, A_NKV), F32)],
        compiler_params=_cparams(("arbitrary",)),
    )(proj, proj, proj, sinks, do, _band_bias())


HALO = 8


def _shift_rows(t, j):
    return pltpu.roll(t, j % t.shape[0], 0)


def _conv_taps(gate_ext):
    return _shift_rows(gate_ext, 2), _shift_rows(gate_ext, 1), gate_ext


def _conv_gate(g2, g1, g0, cw, cb):
    return cb + cw[0:1, :] * g2 + cw[1:2, :] * g1 + cw[2:3, :] * g0


def _convgate_fwd(u, cw, cb, *, name):
    S, F2 = u.shape
    F = F2 // 2
    tm = _pick(S, (256, 128))
    hb = tm // HALO
    urow = pl.BlockSpec((tm, F2), lambda i: (i, 0))
    uprev = pl.BlockSpec((HALO, F), lambda i: (jnp.maximum(i * hb - 1, 0), 0))

    def body(u_ref, up_ref, cw_ref, cb_ref, a_ref):
        i = pl.program_id(0)
        prev = jnp.where(i > 0, up_ref[...], 0.0)
        gc = _conv_gate(*_conv_taps(jnp.concatenate([prev, u_ref[:, :F]], axis=0)), cw_ref[...], cb_ref[...])[HALO:]
        a_ref[...] = (gc * _sigmoid(gc) * u_ref[:, F:]).astype(a_ref.dtype)

    return pl.pallas_call(
        body, name=name, grid=(S // tm,),
        in_specs=[urow, uprev, pl.BlockSpec((3, F), lambda i: (0, 0)), pl.BlockSpec((1, F), lambda i: (0, 0))],
        out_specs=pl.BlockSpec((tm, F), lambda i: (i, 0)),
        out_shape=jax.ShapeDtypeStruct((S, F), MXU_DTYPE),
        compiler_params=_cparams(("parallel",)),
    )(u, u, cw, cb)


def _convgate_bwd(u, da, cw, cb, *, name):
    S, F2 = u.shape
    F = F2 // 2
    tm = _pick(S, (256, 128))
    hb = tm // HALO
    nt = S // tm
    nh = S // HALO
    urow = pl.BlockSpec((tm, F2), lambda i: (i, 0))
    uprev = pl.BlockSpec((HALO, F), lambda i: (jnp.maximum(i * hb - 1, 0), 0))
    unext = pl.BlockSpec((HALO, F2), lambda i: (jnp.minimum((i + 1) * hb, nh - 1), 0))
    darow = pl.BlockSpec((tm, F), lambda i: (i, 0))
    danext = pl.BlockSpec((HALO, F), lambda i: (jnp.minimum((i + 1) * hb, nh - 1), 0))

    def body(u_ref, up_ref, un_ref, da_ref, dan_ref, cw_ref, cb_ref, du_ref, dcw_ref, dcb_ref):
        i = pl.program_id(0)
        cwv = cw_ref[...]
        prev = jnp.where(i > 0, up_ref[...], 0.0)
        gate_ext = jnp.concatenate([prev, u_ref[:, :F], un_ref[:, :F]], axis=0)
        val_ext = jnp.concatenate([u_ref[:, F:], un_ref[:, F:]], axis=0)
        da_next = jnp.where(i < nt - 1, dan_ref[...].astype(F32), 0.0)
        da_ext = jnp.concatenate([da_ref[...].astype(F32), da_next], axis=0)
        g2, g1, g0 = (t[HALO:] for t in _conv_taps(gate_ext))
        gc = _conv_gate(g2, g1, g0, cwv, cb_ref[...])
        sg = _sigmoid(gc)
        silu = gc * sg
        dval = da_ext * silu
        dgc = da_ext * val_ext * (sg * (1.0 + gc * (1.0 - sg)))
        dgate = cwv[2:3, :] * dgc + cwv[1:2, :] * _shift_rows(dgc, -1) + cwv[0:1, :] * _shift_rows(dgc, -2)
        du_ref[:, :F] = dgate[:tm].astype(du_ref.dtype)
        du_ref[:, F:] = dval[:tm].astype(du_ref.dtype)
        dgc_c = dgc[:tm]
        dcw = jnp.concatenate([jnp.sum(dgc_c * g2[:tm], axis=0, keepdims=True),
                               jnp.sum(dgc_c * g1[:tm], axis=0, keepdims=True),
                               jnp.sum(dgc_c * g0[:tm], axis=0, keepdims=True)], axis=0)
        dcb = jnp.sum(dgc_c, axis=0, keepdims=True)

        @pl.when(i == 0)
        def _():
            dcw_ref[...] = dcw
            dcb_ref[...] = dcb

        @pl.when(i > 0)
        def _():
            dcw_ref[...] += dcw
            dcb_ref[...] += dcb

    return pl.pallas_call(
        body, name=name, grid=(nt,),
        in_specs=[urow, uprev, unext, darow, danext,
                  pl.BlockSpec((3, F), lambda i: (0, 0)), pl.BlockSpec((1, F), lambda i: (0, 0))],
        out_specs=[urow, pl.BlockSpec((3, F), lambda i: (0, 0)), pl.BlockSpec((1, F), lambda i: (0, 0))],
        out_shape=[jax.ShapeDtypeStruct((S, F2), MXU_DTYPE), jax.ShapeDtypeStruct((3, F), F32),
                   jax.ShapeDtypeStruct((1, F), F32)],
        compiler_params=_cparams(("arbitrary",)),
    )(u, u, u, da, da, cw, cb)


def _adamw_math(w, g, m, v):
    m = ADAM_B1 * m + (1.0 - ADAM_B1) * g
    v = ADAM_B2 * v + (1.0 - ADAM_B2) * (g * g)
    m_hat = m / (1.0 - ADAM_B1 ** ADAM_STEP)
    v_hat = v / (1.0 - ADAM_B2 ** ADAM_STEP)
    delta = -ADAM_LR * (m_hat / (jnp.sqrt(v_hat) + ADAM_EPS) + ADAM_WD * w)
    return delta, m, v


def _adamw_layers(w, gs, m, v, *, name):
    Lr, R, C = w.shape
    tr = _pick(R, (256, 128, 64, 32, 16, 8))
    outs = None
    for layer, g in enumerate(gs):
        def body(w_ref, g_ref, m_ref, v_ref, *rest):
            go_ref, d_ref, nm_ref, nv_ref = rest[-4:]
            gv = g_ref[...]
            d, nm, nv = _adamw_math(w_ref[...], gv, m_ref[...], v_ref[...])
            go_ref[...] = gv
            d_ref[...] = d
            nm_ref[...] = nm
            nv_ref[...] = nv

        lay = pl.BlockSpec((None, tr, C), lambda i, layer=layer: (layer, i, 0))
        in_specs = [lay, pl.BlockSpec((tr, C), lambda i: (i, 0)), lay, lay]
        args = [w, g, m, v]
        aliases = {}
        if outs is not None:
            in_specs += [pl.BlockSpec(memory_space=pl.ANY)] * 4
            args += list(outs)
            aliases = {4 + t: t for t in range(4)}
        outs = pl.pallas_call(
            body, name=f"{name}_{layer}", grid=(R // tr,), in_specs=in_specs, out_specs=[lay] * 4,
            out_shape=[jax.ShapeDtypeStruct((Lr, R, C), F32)] * 4, input_output_aliases=aliases,
            compiler_params=_cparams(("parallel",)),
        )(*args)
    return outs


def _adamw_small(items, *, name):
    n = len(items)

    def body(*refs):
        ins, outs = refs[:4 * n], refs[4 * n:]
        for t in range(n):
            w, g, m, v = (r[...] for r in ins[4 * t:4 * t + 4])
            d, nm, nv = _adamw_math(w, g, m, v)
            outs[3 * t][...] = d
            outs[3 * t + 1][...] = nm
            outs[3 * t + 2][...] = nv

    flat = [a for it in items for a in it]
    out_shape = [jax.ShapeDtypeStruct(it[0].shape, F32) for it in items for _ in range(3)]
    vm = pl.BlockSpec(memory_space=pltpu.VMEM)
    res = pl.pallas_call(body, name=name, in_specs=[vm] * len(flat), out_specs=[vm] * len(out_shape),
                         out_shape=out_shape)(*flat)
    return [tuple(res[3 * t:3 * t + 3]) for t in range(n)]


def _place():
    return lax.axis_index("x"), lax.axis_index("y"), lax.axis_index("c")


_FLIPS = ((1, 0), (0, 1), (1, 1))


ROW_ALIGN = 16


def _half(rows, which):
    return pl.ds(pl.multiple_of(which * (rows // 2), ROW_ALIGN), rows // 2)


def _remote(src, dst, send_sems, recv_sems, k, to):
    return pltpu.make_async_remote_copy(src_ref=src, dst_ref=dst, send_sem=send_sems.at[k], recv_sem=recv_sems.at[k],
                                        device_id=to, device_id_type=MESH)


class _Gather:
    def __init__(self, w_refs, out_refs, send_sems, recv_sems):
        self.w_refs, self.out_refs, self.send_sems, self.recv_sems = w_refs, out_refs, send_sems, recv_sems
        self.pairs = [(i, j) for i in range(len(w_refs)) for j in range(3)]

    @staticmethod
    def out_shape(shards):
        return [jax.ShapeDtypeStruct((N_CHIPS,) + s.shape, s.dtype) for s in shards]

    @staticmethod
    def semaphores(n):
        return [pltpu.SemaphoreType.DMA((6 * n,)), pltpu.SemaphoreType.DMA((6 * n,))]

    def _where(self):
        x, y, c = _place()
        return x, y, c, [(x ^ fx, y ^ fy) for fx, fy in _FLIPS]

    def _over_ici(self, i, j, landed):
        x, y, c, chips = self._where()
        px, py = chips[j]
        mine = _half(self.w_refs[i].shape[0], c)
        if landed:
            src = dst = self.out_refs[i].at[2 * px + py, mine]
        else:
            src, dst = self.w_refs[i].at[mine], self.out_refs[i].at[2 * x + y, mine]
        return _remote(src, dst, self.send_sems, self.recv_sems, 6 * i + j, (px, py, c))

    def _over_d2d(self, i, j, which):
        x, y, c, chips = self._where()
        px, py = chips[j]
        blk = self.out_refs[i].at[2 * px + py, _half(self.w_refs[i].shape[0], which)]
        return _remote(blk, blk, self.send_sems, self.recv_sems, 6 * i + 3 + j, (x, y, 1 - c))

    def issue(self):
        for i, j in self.pairs:
            self._over_ici(i, j, False).start()

    def finish(self):
        c = lax.axis_index("c")
        for i, j in self.pairs:
            self._over_ici(i, j, True).wait_recv()
            self._over_d2d(i, j, c).start()
        for i, j in self.pairs:
            self._over_d2d(i, j, 1 - c).wait_recv()
        for i, j in self.pairs:
            self._over_ici(i, j, False).wait_send()
            self._over_d2d(i, j, c).wait_send()


def _swap_halves(gs, *, name):
    n = len(gs)
    any_spec = pl.BlockSpec(memory_space=pl.ANY)

    def body(*refs):
        swap = _Swap(refs[:n], refs[n:2 * n], refs[2 * n], refs[2 * n + 1])
        swap.issue()
        swap.finish()

    return pl.pallas_call(
        body, name=name, in_specs=[any_spec] * n, out_specs=[any_spec] * n,
        out_shape=_Swap.out_shape(gs), scratch_shapes=_Swap.semaphores(n),
    )(*gs)


class _Swap:
    def __init__(self, g_refs, out_refs, send_sems, recv_sems):
        self.g_refs, self.out_refs, self.send_sems, self.recv_sems = g_refs, out_refs, send_sems, recv_sems

    @staticmethod
    def out_shape(gs):
        return [jax.ShapeDtypeStruct((N_CHIPS, g.shape[1] // 2, g.shape[2]), g.dtype) for g in gs]

    @staticmethod
    def semaphores(n):
        return [pltpu.SemaphoreType.DMA((n,)), pltpu.SemaphoreType.DMA((n,))]

    def _copies(self):
        x, y, c = _place()
        return [_remote(g.at[:, _half(g.shape[1], 1 - c)], out, self.send_sems, self.recv_sems, i, (x, y, 1 - c))
                for i, (g, out) in enumerate(zip(self.g_refs, self.out_refs))]

    def issue(self):
        for cp in self._copies():
            cp.start()

    def finish(self):
        for cp in self._copies():
            cp.wait()


def _add_halves(g, got, c_arr, *, name):
    _, rows, cols = g.shape
    blk = (None, rows // 2, cols)

    def body(c_ref, g_ref, got_ref, o_ref):
        o_ref[...] = (g_ref[...] + got_ref[...]).astype(o_ref.dtype)

    return pl.pallas_call(
        body, name=name,
        grid_spec=pltpu.PrefetchScalarGridSpec(
            num_scalar_prefetch=1, grid=(N_CHIPS,),
            in_specs=[pl.BlockSpec(blk, lambda j, c_ref: (j, c_ref[0], 0)), pl.BlockSpec(blk, lambda j, c_ref: (j, 0, 0))],
            out_specs=pl.BlockSpec(blk, lambda j, c_ref: (j, 0, 0))),
        out_shape=jax.ShapeDtypeStruct((N_CHIPS, rows // 2, cols), WIRE_DTYPE),
        compiler_params=_cparams(("parallel",)),
    )(c_arr, g, got)


class _Exchange:
    def __init__(self, q_refs, out_refs, send_sems, recv_sems):
        self.q_refs, self.out_refs, self.send_sems, self.recv_sems = q_refs, out_refs, send_sems, recv_sems
        self.pairs = [(i, j) for i in range(len(q_refs)) for j in range(3)]

    @staticmethod
    def out_shape(qs):
        return [jax.ShapeDtypeStruct(q.shape, q.dtype) for q in qs]

    @staticmethod
    def semaphores(n):
        return [pltpu.SemaphoreType.DMA((3 * n,)), pltpu.SemaphoreType.DMA((3 * n,))]

    def _copy(self, i, j, landed):
        x, y, c = _place()
        px, py = [(x ^ fx, y ^ fy) for fx, fy in _FLIPS][j]
        if landed:
            src = dst = self.out_refs[i].at[2 * px + py]
        else:
            src, dst = self.q_refs[i].at[2 * px + py], self.out_refs[i].at[2 * x + y]
        return _remote(src, dst, self.send_sems, self.recv_sems, 3 * i + j, (px, py, c))

    def issue(self):
        for i, j in self.pairs:
            self._copy(i, j, False).start()

    def finish(self):
        for i, j in self.pairs:
            self._copy(i, j, True).wait_recv()
        for i, j in self.pairs:
            self._copy(i, j, False).wait_send()


def _sum_chips(q, r, place_arr, *, name):
    _, h, cols = q.shape
    blk = (None, h, cols)

    def body(p_ref, q_ref, r1_ref, r2_ref, r3_ref, o_ref):
        o_ref[...] = ((q_ref[...].astype(F32) + r1_ref[...].astype(F32)) + r2_ref[...].astype(F32)) + r3_ref[...].astype(F32)

    other = [pl.BlockSpec(blk, lambda i, p_ref, f=f: (p_ref[0] ^ f, 0, 0)) for f in (1, 2, 3)]
    return pl.pallas_call(
        body, name=name,
        grid_spec=pltpu.PrefetchScalarGridSpec(
            num_scalar_prefetch=1, grid=(1,),
            in_specs=[pl.BlockSpec(blk, lambda i, p_ref: (p_ref[0], 0, 0))] + other,
            out_specs=pl.BlockSpec((h, cols), lambda i, p_ref: (p_ref[1], 0))),
        out_shape=jax.ShapeDtypeStruct((2 * h, cols), F32),
        compiler_params=_cparams(("arbitrary",)),
    )(place_arr, q, r, r, r)


def _join_halves(fs, *, name):
    n = len(fs)
    any_spec = pl.BlockSpec(memory_space=pl.ANY)

    def body(*refs):
        out_refs, send_sems, recv_sems = refs[n:2 * n], refs[2 * n], refs[2 * n + 1]
        x, y, c = _place()
        sent = []
        for i in range(n):
            mine = out_refs[i].at[_half(fs[i].shape[0], c)]
            cp = _remote(mine, mine, send_sems, recv_sems, i, (x, y, 1 - c))
            cp.start()
            sent.append(cp)
        for i in range(n):
            its = out_refs[i].at[_half(fs[i].shape[0], 1 - c)]
            _remote(its, its, send_sems, recv_sems, i, (x, y, 1 - c)).wait_recv()
        for cp in sent:
            cp.wait_send()

    return pl.pallas_call(
        body, name=name, in_specs=[any_spec] * n, out_specs=[any_spec] * n,
        out_shape=[jax.ShapeDtypeStruct(f.shape, f.dtype) for f in fs], input_output_aliases={i: i for i in range(n)},
        scratch_shapes=[pltpu.SemaphoreType.DMA((n,)), pltpu.SemaphoreType.DMA((n,))],
    )(*fs)


def _allreduce_small(buf, *, name):
    R = buf.shape[0]
    vm = pl.BlockSpec(memory_space=pltpu.VMEM)

    def body(b_ref, o_ref, slots, send_sems, recv_sems):
        x, y, c = _place()
        me = 4 * x + 2 * y + c
        slots[me] = b_ref[...]
        sends = []
        for kk in range(1, 8):
            fx, fy, fc = (kk >> 2) & 1, (kk >> 1) & 1, kk & 1
            cp = pltpu.make_async_remote_copy(
                src_ref=b_ref, dst_ref=slots.at[me], send_sem=send_sems.at[kk - 1], recv_sem=recv_sems.at[kk - 1],
                device_id=(x ^ fx, y ^ fy, c ^ fc), device_id_type=MESH)
            cp.start()
            sends.append(cp)
        for kk in range(1, 8):
            fx, fy, fc = (kk >> 2) & 1, (kk >> 1) & 1, kk & 1
            peer = 4 * (x ^ fx) + 2 * (y ^ fy) + (c ^ fc)
            pltpu.make_async_remote_copy(
                src_ref=b_ref, dst_ref=slots.at[peer], send_sem=send_sems.at[kk - 1], recv_sem=recv_sems.at[kk - 1],
                device_id=(x ^ fx, y ^ fy, c ^ fc), device_id_type=MESH).wait_recv()
        for cp in sends:
            cp.wait_send()
        acc = slots[0]
        for d in range(1, 8):
            acc = acc + slots[d]
        o_ref[...] = acc

    return pl.pallas_call(
        body, name=name, in_specs=[vm], out_specs=vm,
        out_shape=jax.ShapeDtypeStruct((R, LANES), F32),
        scratch_shapes=[pltpu.VMEM((8, R, LANES), F32), pltpu.SemaphoreType.DMA((7,)), pltpu.SemaphoreType.DMA((7,))],
    )(buf)


def _pad_rows(v, mult=8 * LANES):
    flat = v.reshape(-1)
    n = flat.shape[0]
    tot = -(-n // mult) * mult
    return jnp.pad(flat, (0, tot - n)).reshape(-1, LANES)


def _pack_small(parts):
    return jnp.concatenate([_pad_rows(p.astype(F32)) for p in parts], axis=0)


def _unpack_small(buf, shapes):
    out, r = [], 0
    for sh in shapes:
        n = math.prod(sh)
        rows = -(-n // (8 * LANES)) * 8
        out.append(buf[r:r + rows].reshape(-1)[:n].reshape(sh))
        r += rows
    return out


def _cols_to_shards(w):
    *lead, K, N = w.shape
    t = w.reshape(*lead, K, N_CHIPS, N // N_CHIPS)
    return jnp.moveaxis(t, -2, 0)


def _shards_to_cols(t):
    t = jnp.moveaxis(t, 0, -2)
    *lead, K, _, n = t.shape
    return t.reshape(*lead, K, N_CHIPS * n)


_BIG = ("m_w_in", "m_w_out", "a_w_in", "a_w_out", "f_w_up", "f_w_down")


def kernel(x, m_w_in, m_gate_bias, m_head_norm, m_w_out, a_w_in, a_b_in, a_sinks, a_w_out, a_b_out, norm_mix_pre, norm_mix_post, norm_ffn_pre, norm_ffn_post, f_w_up, f_conv_w, f_conv_b, f_w_down, loss_target, m_m_w_in, m_m_gate_bias, m_m_head_norm, m_m_w_out, m_a_w_in, m_a_b_in, m_a_sinks, m_a_w_out, m_a_b_out, m_norm_mix_pre, m_norm_mix_post, m_norm_ffn_pre, m_norm_ffn_post, m_f_w_up, m_f_conv_w, m_f_conv_b, m_f_w_down, v_m_w_in, v_m_gate_bias, v_m_head_norm, v_m_w_out, v_a_w_in, v_a_b_in, v_a_sinks, v_a_w_out, v_a_b_out, v_norm_mix_pre, v_norm_mix_post, v_norm_ffn_pre, v_norm_ffn_post, v_f_w_up, v_f_conv_w, v_f_conv_b, v_f_w_down):
    params = dict(m_w_in=m_w_in, m_gate_bias=m_gate_bias, m_head_norm=m_head_norm, m_w_out=m_w_out, a_w_in=a_w_in,
                  a_b_in=a_b_in, a_sinks=a_sinks, a_w_out=a_w_out, a_b_out=a_b_out, norm_mix_pre=norm_mix_pre,
                  norm_mix_post=norm_mix_post, norm_ffn_pre=norm_ffn_pre, norm_ffn_post=norm_ffn_post, f_w_up=f_w_up,
                  f_conv_w=f_conv_w, f_conv_b=f_conv_b, f_w_down=f_w_down)
    mom1 = dict(m_w_in=m_m_w_in, m_gate_bias=m_m_gate_bias, m_head_norm=m_m_head_norm, m_w_out=m_m_w_out,
                a_w_in=m_a_w_in, a_b_in=m_a_b_in, a_sinks=m_a_sinks, a_w_out=m_a_w_out, a_b_out=m_a_b_out,
                norm_mix_pre=m_norm_mix_pre, norm_mix_post=m_norm_mix_post, norm_ffn_pre=m_norm_ffn_pre,
                norm_ffn_post=m_norm_ffn_post, f_w_up=m_f_w_up, f_conv_w=m_f_conv_w, f_conv_b=m_f_conv_b,
                f_w_down=m_f_w_down)
    mom2 = dict(m_w_in=v_m_w_in, m_gate_bias=v_m_gate_bias, m_head_norm=v_m_head_norm, m_w_out=v_m_w_out,
                a_w_in=v_a_w_in, a_b_in=v_a_b_in, a_sinks=v_a_sinks, a_w_out=v_a_w_out, a_b_out=v_a_b_out,
                norm_mix_pre=v_norm_mix_pre, norm_mix_post=v_norm_mix_post, norm_ffn_pre=v_norm_ffn_pre,
                norm_ffn_post=v_norm_ffn_post, f_w_up=v_f_w_up, f_conv_w=v_f_conv_w, f_conv_b=v_f_conv_b,
                f_w_down=v_f_w_down)
    order = list(params)

    mx, my, mc = _place()
    chip = 2 * mx + my
    h0 = x[0]
    target = loss_target[0]

    def two_d(t):
        return t.reshape(-1, t.shape[-1])

    early = ("m_w_in",)
    late = tuple(n for n in _BIG if n not in early)
    mine = {n: two_d(params[n]).astype(MXU_DTYPE) for n in _BIG}

    def with_own_slot(names, theirs):
        return {n: lax.dynamic_update_slice(t, mine[n][None], (chip, 0, 0)) for n, t in zip(names, theirs)}

    z0, *early_theirs = _rms_fwd(h0, norm_mix_pre[0:1], [mine[n] for n in early], out_dtype=MXU_DTYPE,
                                 name="mix_pre_norm0")
    gathered = with_own_slot(early, early_theirs)

    def in_place(shard, axis):
        width = shard.shape[axis]
        z = jnp.zeros(shard.shape[:axis] + (N_CHIPS * width,) + shard.shape[axis + 1:], F32)
        contrib = jnp.where(mc == 0, shard, 0.0)
        return lax.dynamic_update_slice_in_dim(z, contrib, chip * width, axis)

    sm_in = [in_place(a_b_in, 1), in_place(a_b_out, 1), in_place(f_conv_w, 2)]
    sm_full = _unpack_small(_allreduce_small(_pack_small(sm_in), name="gather_small"), [t.shape for t in sm_in])
    b_in_full, b_out_full, conv_w_full = sm_full

    W_in = _shards_to_cols(gathered["m_w_in"])
    W_all = jnp.pad(W_in, ((0, 0), (0, M_PROJ - W_in.shape[1])))
    gbias = jnp.pad(m_gate_bias[0].reshape(1, 2 * M_HEADS), ((0, 0), (0, LANES - 2 * M_HEADS)))

    grads = {}

    def ffn_fwd(i, z1, end):
        u = _mm(z1, W_up, b_layer=i, name=f"ffn_up{i}")
        a = _convgate_fwd(u, conv_w_full[i], f_conv_b[i:i + 1], name=f"ffn_act{i}")
        return u, a, _mm(a, W_down[i], rows=end, name=f"ffn_down{i}")

    proj = _mm(z0, W_all, name="mlstm_proj")
    gated, C_all, n_all, m_all, *late_theirs = _mlstm_fwd(proj, gbias, m_head_norm, [mine[n] for n in late],
                                                          name="mlstm_fwd")
    gathered.update(with_own_slot(late, late_theirs))
    W_mout = gathered["m_w_out"].reshape(D_MODEL, D_MODEL)
    A_in, A_out = _shards_to_cols(gathered["a_w_in"]), gathered["a_w_out"].reshape(D_MODEL, D_MODEL)
    W_up = gathered["f_w_up"]
    dsh = D_FF // N_CHIPS
    W_down = [gathered["f_w_down"][:, i * dsh:(i + 1) * dsh].reshape(D_FF, D_MODEL) for i in range(2)]
    zm0, h1, z1 = _mm(gated, W_mout, rows=_rows_sublayer_end(h0, norm_mix_post[0:1], norm_ffn_pre[0:1]), name="mlstm_out")
    u0, a0, (zf0, h2, y0) = ffn_fwd(0, z1, _rows_sublayer_end(h1, norm_ffn_post[0:1], norm_mix_pre[1:2]))

    aproj = _mm(y0, A_in, bias=b_in_full, name="attn_proj")
    sinks = a_sinks.reshape(A_KVH, A_G)
    ao = _attn_fwd(aproj, sinks, name="attn_fwd")
    zm1, h3, z3 = _mm(ao, A_out, bias=b_out_full, rows=_rows_sublayer_end(h2, norm_mix_post[1:2], norm_ffn_pre[1:2]),
                      name="attn_out")
    u1, a1, (zf1, dh, dzf1, g_post1, loss_lanes) = ffn_fwd(1, z3, _rows_last_sublayer_end(h3, norm_ffn_post[1:2], target))
    loss = lax.psum(loss_lanes[0, 0], ("x", "y", "c"))

    g_post, g_fpre, g_mpost, g_mpre = [None, g_post1], [None, None], [None, None], [None, None]
    dW_up, dW_down, dconv_w, dconv_b = [None, None], [None, None], [None, None], [None, None]

    def by_rows(g):
        return g.reshape(N_CHIPS, -1, g.shape[-1])

    def ffn_bwd(i, dzf, z1_, u, a, start, ride):
        da = _mm(dzf, W_down[i], tb=True, name=f"ffn_down_dx{i}")
        dW_down[i] = _mm(a, dzf, ta=True, name=f"ffn_down_dw{i}")
        du, dconv_w[i], dconv_b[i] = _convgate_bwd(u, da, conv_w_full[i], f_conv_b[i:i + 1], name=f"ffn_act_bwd{i}")
        dW_up[i] = _mm(z1_, du, ta=True, out_stacked=True, name=f"ffn_up_dw{i}")
        riders = (list(ride) + [by_rows(dW_down[i]), dW_up[i]]) if ride is not None else []
        outs = _mm(du, W_up, tb=True, b_layer=i, rows=start, ride=(_Swap, riders) if riders else None,
                   name=f"ffn_up_dx{i}")
        return outs[:len(start.outs)], riders, outs[len(start.outs):]

    (dh3, g_fpre[1], dzm1, g_mpost[1], db_out), _, _ = ffn_bwd(
        1, dzf1, z3, u1, a1, _rows_sublayer_start_bwd(h3, dh, zm1, norm_ffn_pre[1:2], norm_mix_post[1:2]), None)
    dao = _mm(dzm1, A_out, tb=True, name="attn_out_dx")
    dA_out = _mm(ao, dzm1, ta=True, name="attn_out_dw")
    daq, dakv, dsinks, cs_q, cs_kv = _attn_bwd(aproj, sinks, dao, name="attn_bwd")
    db_in = jnp.concatenate([cs_q, cs_kv], axis=1)
    dh2, g_mpre[1], dzf0, g_post[0], _ = _mm(
        daq, A_in[:, :A_NQ], tb=True, add=_mm(dakv, A_in[:, A_NQ:], tb=True, name="attn_proj_kv_dx"),
        rows=_rows_sublayer_start_bwd(h2, dh3, zf0, norm_mix_pre[1:2], norm_ffn_post[0:1]), name="attn_proj_q_dx")
    dA_in = jnp.concatenate([_mm(y0, daq, ta=True, name="attn_proj_q_dw"),
                             _mm(y0, dakv, ta=True, name="attn_proj_kv_dw")], axis=1)

    c_arr = jnp.reshape(mc, (1,)).astype(jnp.int32)
    place_arr = jnp.stack([chip, mc]).astype(jnp.int32)

    def add_halves(tags, gs, got):
        return [_add_halves(g, t, c_arr, name=f"grad_add_halves_{tag}") for tag, g, t in zip(tags, gs, got)]

    (dh1, g_fpre[0], dzm0, g_mpost[0], _), rode, rode_got = ffn_bwd(
        0, dzf0, z1, u0, a0, _rows_sublayer_start_bwd(h1, dh2, zm0, norm_ffn_pre[0:1], norm_mix_post[0:1]),
        [_cols_to_shards(dA_in), by_rows(dA_out), dW_up[1], by_rows(dW_down[1])])
    rode_tags = ["a_w_in", "a_w_out", "f_w_up1", "f_w_down1", "f_w_down0", "f_w_up0"]
    dgated = _mm(dzm0, W_mout, tb=True, name="mlstm_out_dx")
    dW_mout = _mm(gated, dzm0, ta=True, name="mlstm_out_dw")
    next_tags, next_gs = ["m_w_out"], [by_rows(dW_mout)]
    late_tags = rode_tags + next_tags
    late_part = (add_halves(rode_tags, rode, rode_got)
                 + add_halves(next_tags, next_gs, _swap_halves(next_gs, name="grad_swap_halves_late")))
    dproj, dgbias, dhn, *late_from = _mlstm_bwd(proj, gbias, m_head_norm, C_all, n_all, m_all, dgated, late_part,
                                                name="mlstm_bwd")
    dW_all = _mm(z0, dproj, ta=True, name="mlstm_proj_dw")
    dW_min = dW_all[:, :m_w_in.shape[-1] * N_CHIPS]

    early_tags, early_gs = ["m_w_in"], [_cols_to_shards(dW_min)]
    early_part = add_halves(early_tags, early_gs, _swap_halves(early_gs, name="grad_swap_halves_early"))
    grad_x, g_mpre[0], *early_from = _mm(
        dproj, W_all, tb=True, rows=_rows_first_sublayer_start_bwd(h0, dh1, norm_mix_pre[0:1]),
        ride=(_Exchange, early_part),
        name="mlstm_proj_dx")
    tags = early_tags + late_tags
    halves = [_sum_chips(q, r, place_arr, name=f"grad_sum_chips_{tag}")
              for tag, q, r in zip(tags, early_part + late_part, list(early_from) + list(late_from))]
    reduced = dict(zip(tags, _join_halves(halves, name="grad_join_halves")))
    layer_grads = dict(m_w_in=[reduced["m_w_in"]], m_w_out=[reduced["m_w_out"]], a_w_in=[reduced["a_w_in"]],
                       a_w_out=[reduced["a_w_out"]], f_w_up=[reduced["f_w_up0"], reduced["f_w_up1"]],
                       f_w_down=[reduced["f_w_down0"], reduced["f_w_down1"]])

    small_g = [
        dgbias[:, :2 * M_HEADS].reshape(1, 2, M_HEADS),
        dhn,
        dsinks.reshape(1, A_QH),
        db_in, db_out,
        jnp.concatenate(g_mpre), jnp.concatenate(g_mpost), jnp.concatenate(g_fpre), jnp.concatenate(g_post),
        jnp.stack(dconv_w), jnp.concatenate(dconv_b),
    ]
    small_names = ["m_gate_bias", "m_head_norm", "a_sinks", "a_b_in", "a_b_out", "norm_mix_pre", "norm_mix_post",
                   "norm_ffn_pre", "norm_ffn_post", "f_conv_w", "f_conv_b"]
    red = _unpack_small(_allreduce_small(_pack_small(small_g), name="reduce_small"), [t.shape for t in small_g])
    for n, t in zip(small_names, red):
        if n in ("a_b_in", "a_b_out", "f_conv_w"):
            axis = t.ndim - 1
            width = params[n].shape[axis]
            t = lax.dynamic_slice_in_dim(t, chip * width, width, axis)
        grads[n] = t

    deltas, new_m, new_v = {}, {}, {}
    for n in _BIG:
        grads[n], deltas[n], new_m[n], new_v[n] = _adamw_layers(params[n], layer_grads[n], mom1[n], mom2[n],
                                                                name=f"adamw_{n}")
    two = lambda t: t.reshape(-1, t.shape[-1])
    res = _adamw_small([(two(params[n]), two(grads[n]), two(mom1[n]), two(mom2[n])) for n in small_names],
                       name="adamw_small")
    for n, (d, nm, nv) in zip(small_names, res):
        sh = params[n].shape
        deltas[n], new_m[n], new_v[n] = d.reshape(sh), nm.reshape(sh), nv.reshape(sh)

    return (loss, grad_x[None], *[grads[n] for n in order], *[deltas[n] for n in order],
            *[new_m[n] for n in order], *[new_v[n] for n in order])
```

```python
import functools
import math

import numpy as np

import jax
import jax.numpy as jnp
from jax import lax
from jax.experimental import pallas as pl
from jax.experimental.pallas import tpu as pltpu

F32 = jnp.float32
MXU_DTYPE = jnp.bfloat16
WIRE_DTYPE = jnp.bfloat16
MESH = pl.DeviceIdType.MESH

D_MODEL = 1024
EPS = 1e-6
M_HEADS, M_QK, M_V, M_CHUNK = 8, 64, 128, 128
GATE_CAP = 15.0
A_DH, A_QH, A_KVH, A_G, A_BLK = 64, 16, 2, 8, 128
D_FF = 2816
N_CHIPS = 4
LANES = 128
VMEM_LIMIT = 56 * 1024 * 1024

ADAM_LR, ADAM_B1, ADAM_B2, ADAM_EPS, ADAM_WD, ADAM_STEP = 0.001, 0.9, 0.999, 1e-08, 0.01, 10


def _cparams(sem):
    return pltpu.CompilerParams(dimension_semantics=sem, vmem_limit_bytes=VMEM_LIMIT)


def _pick(n, cands):
    for c in cands:
        if n % c == 0:
            return c
    return n


class _Rows:
    def __init__(self, tiles, vecs, outs, fn):
        self.tiles, self.vecs, self.outs, self.fn = list(tiles), list(vecs), list(outs), fn


MM_VMEM_BUDGET = 46 * 1024 * 1024
ROWS_FULL_K = 4224
WIDE_N = 3200


def _mm(a, b, *, ta=False, tb=False, out_dtype=F32, bias=None, add=None, b_layer=None, out_stacked=False, rows=None,
        ride=None, name):
    if ta:
        K, M = a.shape
    else:
        M, K = a.shape
    b_stacked = b_layer is not None
    if b_stacked:
        assert not ta
        n_sh = b.shape[2]
        w_rows, w_cols = D_MODEL, N_CHIPS * n_sh
        N, Kb = (w_rows, w_cols) if tb else (w_cols, w_rows)
    elif tb:
        N, Kb = b.shape
    else:
        Kb, N = b.shape
    assert K == Kb, (a.shape, b.shape)
    tn = _pick(N, (1024, 1408, 1280, 640, 512, 256, 128))
    tk = K if K <= 2816 else _pick(K, (2816, 2048, 1408, 1024, 512, 256, 128))
    if tn < 1024 <= N <= WIDE_N:
        tn, tk = N, min(tk, 1024)
    shards_per_step = 1
    if b_stacked and tb:
        shards_per_step = 2
        tk = shards_per_step * n_sh
    if (b_stacked and not tb) or out_stacked:
        tn = N // N_CHIPS
    if rows is not None:
        assert tn == N and not out_stacked and not ta
        if K <= ROWS_FULL_K:
            tk = K
    nk = K // tk

    def vmem_bytes(rows_per_tile):
        total = 2 * (rows_per_tile * tk * a.dtype.itemsize + tk * tn * b.dtype.itemsize)
        if rows is None:
            total += 2 * rows_per_tile * tn * jnp.dtype(out_dtype).itemsize
        else:
            total += 2 * rows_per_tile * N * (4 * len(rows.tiles)
                                              + sum(jnp.dtype(dt).itemsize for kind, dt in rows.outs if kind == "tile"))
        if add is not None:
            total += 2 * rows_per_tile * tn * 4
        return total + (rows_per_tile * tn * 4 if nk > 1 else 0)

    tm = next(c for c in (2048, 1024, 1408, 512, 256, 128, M) if M % c == 0 and (vmem_bytes(c) <= MM_VMEM_BUDGET or c <= 128))
    dn = (((0 if ta else 1,), (1 if tb else 0,)), ((), ()))
    has_bias, has_add = bias is not None, add is not None
    n_tiles, n_vecs, n_outs = (len(rows.tiles), len(rows.vecs), len(rows.outs)) if rows is not None else (0, 0, 1)
    rider, riding = ride if ride is not None else (None, ())
    n_ex = len(riding)
    assert not n_ex or rows is not None
    grid = (M // tm, N // tn, nk)

    def body(*refs):
        a_ref, b_ref = refs[0], refs[1]
        pos = 2
        bias_ref = add_ref = None
        if has_bias:
            bias_ref = refs[pos]
            pos += 1
        if has_add:
            add_ref = refs[pos]
            pos += 1
        tile_refs, vec_refs = refs[pos:pos + n_tiles], refs[pos + n_tiles:pos + n_tiles + n_vecs]
        pos += n_tiles + n_vecs
        ex_in = refs[pos:pos + n_ex]
        pos += n_ex
        out_refs, ex_out = refs[pos:pos + n_outs], refs[pos + n_outs:pos + n_outs + n_ex]
        pos += n_outs + n_ex
        acc_ref = refs[pos] if nk > 1 else None
        if n_ex:
            ex = rider(ex_in, ex_out, *refs[pos + (nk > 1):])
            step = (pl.program_id(0) * grid[1] + pl.program_id(1)) * grid[2] + pl.program_id(2)

            @pl.when(step == 0)
            def _():
                ex.issue()

        def finish(r):
            if has_bias:
                r = r + bias_ref[...]
            if has_add:
                r = r + add_ref[...]
            if rows is None:
                out_refs[0][...] = r.astype(out_dtype)
                return
            vals = rows.fn(r, [t[...] for t in tile_refs], [v[...] for v in vec_refs])
            first = pl.program_id(0) == 0
            for (kind, _), o_ref, val in zip(rows.outs, out_refs, vals):
                if kind == "tile":
                    o_ref[...] = val.astype(o_ref.dtype)
                else:
                    @pl.when(first)
                    def _(o_ref=o_ref, val=val):
                        o_ref[...] = val

                    @pl.when(jnp.logical_not(first))
                    def _(o_ref=o_ref, val=val):
                        o_ref[...] += val

        if shards_per_step > 1:
            part = sum(lax.dot_general(a_ref[:, s * n_sh:(s + 1) * n_sh], b_ref[s], dn, preferred_element_type=F32)
                       for s in range(shards_per_step))
        else:
            part = lax.dot_general(a_ref[...], b_ref[...], dn, preferred_element_type=F32)
        if nk == 1:
            finish(part)
        else:
            k = pl.program_id(2)

            @pl.when(k == 0)
            def _():
                acc_ref[...] = part

            @pl.when(k > 0)
            def _():
                acc_ref[...] += part

            @pl.when(k == nk - 1)
            def _():
                finish(acc_ref[...])

        if n_ex:
            @pl.when(step == grid[0] * grid[1] * grid[2] - 1)
            def _():
                ex.finish()

    a_spec = pl.BlockSpec((tk, tm), lambda i, j, k: (k, i)) if ta else pl.BlockSpec((tm, tk), lambda i, j, k: (i, k))
    if b_stacked and tb:
        off = b_layer * (w_rows // tn)
        b_spec = pl.BlockSpec((shards_per_step, tn, n_sh), lambda i, j, k: (k, off + j, 0))
    elif b_stacked:
        off = b_layer * (w_rows // tk)
        b_spec = pl.BlockSpec((None, tk, tn), lambda i, j, k: (j, off + k, 0))
    elif tb:
        b_spec = pl.BlockSpec((tn, tk), lambda i, j, k: (j, k))
    else:
        b_spec = pl.BlockSpec((tk, tn), lambda i, j, k: (k, j))
    if out_stacked:
        out_spec = pl.BlockSpec((None, tm, tn), lambda i, j, k: (j, i, 0))
        out_shape = jax.ShapeDtypeStruct((N_CHIPS, M, tn), out_dtype)
    else:
        out_spec = pl.BlockSpec((tm, tn), lambda i, j, k: (i, j))
        out_shape = jax.ShapeDtypeStruct((M, N), out_dtype)
    in_specs, args = [a_spec, b_spec], [a, b]
    if has_bias:
        in_specs.append(pl.BlockSpec((1, tn), lambda i, j, k: (0, j)))
        args.append(bias)
    if has_add:
        in_specs.append(pl.BlockSpec((tm, tn), lambda i, j, k: (i, j)))
        args.append(add)
    semantics = ("parallel", "parallel", "arbitrary")
    if rows is not None:
        tile_spec = pl.BlockSpec((tm, N), lambda i, j, k: (i, 0))
        vec_spec = pl.BlockSpec((1, N), lambda i, j, k: (0, 0))
        in_specs += [tile_spec] * n_tiles + [vec_spec] * n_vecs
        args += rows.tiles + rows.vecs
        out_spec = [tile_spec if kind == "tile" else vec_spec for kind, _ in rows.outs]
        out_shape = [jax.ShapeDtypeStruct((M, N) if kind == "tile" else (1, N), dt) for kind, dt in rows.outs]
        semantics = ("arbitrary", "arbitrary", "arbitrary")
    scratch = [pltpu.VMEM((tm, tn), F32)] if nk > 1 else []
    if n_ex:
        any_spec = pl.BlockSpec(memory_space=pl.ANY)
        in_specs += [any_spec] * n_ex
        args += list(riding)
        out_spec = out_spec + [any_spec] * n_ex
        out_shape = out_shape + rider.out_shape(riding)
        scratch += rider.semaphores(n_ex)
    return pl.pallas_call(
        body, name=name,
        grid=(M // tm, N // tn, nk),
        in_specs=in_specs,
        out_specs=out_spec,
        out_shape=out_shape,
        scratch_shapes=scratch,
        compiler_params=_cparams(semantics),
    )(*args)


ROW_TILE = 512


def _rms_fwd(x, g, shards, *, out_dtype, name):
    S, D = x.shape
    tm = _pick(S, (ROW_TILE, 256, 128))
    nsh, nt = len(shards), S // tm
    any_spec = pl.BlockSpec(memory_space=pl.ANY)

    def body(x_ref, g_ref, *rest):
        o_ref = rest[nsh]
        gather = _Gather(rest[:nsh], rest[nsh + 1:2 * nsh + 1], *rest[2 * nsh + 1:])

        @pl.when(pl.program_id(0) == 0)
        def _():
            gather.issue()

        o_ref[...] = _rms(x_ref[...], g_ref[...]).astype(out_dtype)

        @pl.when(pl.program_id(0) == nt - 1)
        def _():
            gather.finish()

    row = pl.BlockSpec((tm, D), lambda i: (i, 0))
    return pl.pallas_call(
        body, name=name, grid=(nt,),
        in_specs=[row, pl.BlockSpec((1, D), lambda i: (0, 0))] + [any_spec] * nsh, out_specs=[row] + [any_spec] * nsh,
        out_shape=[jax.ShapeDtypeStruct((S, D), out_dtype)] + _Gather.out_shape(shards),
        scratch_shapes=_Gather.semaphores(nsh),
        compiler_params=_cparams(("arbitrary",)),
    )(x, g, *shards)


def _rms(x, g):
    return x * lax.rsqrt(jnp.mean(x * x, axis=-1, keepdims=True) + EPS) * g


def _rms_vjp(x, g, dy):
    r = lax.rsqrt(jnp.mean(x * x, axis=-1, keepdims=True) + EPS)
    xh = x * r
    gd = dy * g
    dx = r * (gd - xh * jnp.mean(gd * xh, axis=-1, keepdims=True))
    return dx, jnp.sum(dy * xh, axis=0, keepdims=True)


def _rows_sublayer_end(res, g_post, g_pre_next):
    def fn(z, tiles, vecs):
        h = tiles[0] + _rms(z, vecs[0])
        return [z, h, _rms(h, vecs[1])]

    return _Rows([res], [g_post, g_pre_next], [("tile", F32), ("tile", F32), ("tile", MXU_DTYPE)], fn)


def _rows_last_sublayer_end(res, g_post, target):
    def fn(z, tiles, vecs):
        y = tiles[0] + _rms(z, vecs[0])
        err = y - tiles[1]
        dy = err * (1.0 / err.shape[-1])
        dz, dg = _rms_vjp(z, vecs[0], dy)
        loss = 0.5 * jnp.sum(jnp.mean(err * err, axis=-1, keepdims=True), axis=0, keepdims=True)
        return [z, dy, dz, dg, jnp.broadcast_to(loss, dg.shape)]

    return _Rows([res, target], [g_post], [("tile", F32), ("tile", F32), ("tile", MXU_DTYPE), ("vec", F32), ("vec", F32)], fn)


def _rows_sublayer_start_bwd(x, res, z_below, g_pre, g_post_below):
    def fn(dz, tiles, vecs):
        dx, dg_pre = _rms_vjp(tiles[0], vecs[0], dz)
        dh = tiles[1] + dx
        dzb, dg_post = _rms_vjp(tiles[2], vecs[1], dh)
        return [dh, dg_pre, dzb, dg_post, jnp.sum(dzb, axis=0, keepdims=True)]

    return _Rows([x, res, z_below], [g_pre, g_post_below],
                 [("tile", F32), ("vec", F32), ("tile", MXU_DTYPE), ("vec", F32), ("vec", F32)], fn)


def _rows_first_sublayer_start_bwd(x, res, g_pre):
    def fn(dz, tiles, vecs):
        dx, dg_pre = _rms_vjp(tiles[0], vecs[0], dz)
        return [tiles[1] + dx, dg_pre]

    return _Rows([x, res], [g_pre], [("tile", F32), ("vec", F32)], fn)


def _mx(t):
    return t.astype(MXU_DTYPE)


def _mxf(t):
    return t.astype(MXU_DTYPE).astype(F32)


def _rawdot(a, b, ca, cb):
    return lax.dot_general(_mx(a), _mx(b), (((ca,), (cb,)), ((), ())), preferred_element_type=F32)


@functools.partial(jax.custom_vjp, nondiff_argnums=(2, 3))
def _dot(a, b, ca, cb):
    return _rawdot(a, b, ca, cb)


def _dot_fwd(a, b, ca, cb):
    return _rawdot(a, b, ca, cb), (a, b)


def _dot_bwd(ca, cb, res, g):
    a, b = res
    bj = 1 if cb == 0 else 0
    ai = 0 if ca == 1 else 1
    da = _rawdot(g, b, 1, bj) if ca == 1 else _rawdot(b, g, bj, 1)
    db = _rawdot(a, g, ai, 0) if cb == 0 else _rawdot(g, a, 0, ai)
    return da.astype(a.dtype), db.astype(b.dtype)


_dot.defvjp(_dot_fwd, _dot_bwd)


def _softcap(z):
    return GATE_CAP * jnp.tanh(z / GATE_CAP)


def _log_sigmoid(z):
    return jnp.minimum(z, 0.0) - jnp.log(1.0 + jnp.exp(-jnp.abs(z)))


def _sigmoid(z):
    return 0.5 * jnp.tanh(0.5 * z) + 0.5


def _lane_col(t, lane_index):
    lane = lax.broadcasted_iota(jnp.int32, t.shape, 1)
    return jnp.sum(jnp.where(lane == lane_index, t, 0.0), axis=1, keepdims=True)


def _mlstm_gates(G, bias):
    L = G.shape[0]
    z = _softcap(G + bias)
    ig = z
    lf = _log_sigmoid(z)
    ti = lax.broadcasted_iota(jnp.int32, (L, L), 0)
    si = lax.broadcasted_iota(jnp.int32, (L, L), 1)
    tril = (si <= ti).astype(F32)
    b = lax.dot_general(tril, lf, (((1,), (0,)), ((), ())), precision=lax.Precision.HIGHEST, preferred_element_type=F32)
    bL = jnp.sum(lf, axis=0, keepdims=True)
    return ig, b, ig.T, b.T, bL


def _mlstm_head(h, q, k_pair, v, o, ig_all, b_all, igT, bT, bL_all, hn, C, n, m):
    L = q.shape[0]
    k = jnp.where(_own_lanes(h, k_pair.shape), k_pair, 0.0)
    ti = lax.broadcasted_iota(jnp.int32, (L, L), 0)
    si = lax.broadcasted_iota(jnp.int32, (L, L), 1)
    lower = (si <= ti)
    ig = _lane_col(ig_all, h)
    b = _lane_col(b_all, M_HEADS + h)
    ig_row = igT[h:h + 1, :]
    b_row = bT[M_HEADS + h:M_HEADS + h + 1, :]
    bL = _lane_col(bL_all, M_HEADS + h)
    inter = b + m
    dlog = jnp.where(lower, b - b_row + ig_row, -jnp.inf)
    m_t = lax.stop_gradient(jnp.maximum(inter, jnp.max(dlog, axis=-1, keepdims=True)))
    qs = q * (M_QK ** -0.5)
    w = _dot(qs, k, 1, 1) * jnp.exp(dlog - m_t)
    s_inter = jnp.exp(inter - m_t)
    num = _dot(w, v, 1, 0) + s_inter * _dot(qs, C, 1, 0)
    den = jnp.sum(w, axis=-1, keepdims=True) + s_inter * jnp.sum(_mxf(qs) * _mxf(n), axis=-1, keepdims=True)
    hout = num * (1.0 / jnp.maximum(jnp.abs(den), jnp.exp(-m_t)))
    tail = bL - b + ig
    m_new = lax.stop_gradient(jnp.maximum(bL + m, jnp.max(tail, axis=0, keepdims=True)))
    ws = jnp.exp(tail - m_new)
    decay = jnp.exp(bL + m - m_new)
    wk = ws * k
    C_new = decay * C + _dot(wk, v, 0, 0)
    n_new = decay * n + jnp.sum(_mxf(ws) * _mxf(k), axis=0, keepdims=True)
    hs = hout * lax.rsqrt(jnp.mean(hout * hout, axis=-1, keepdims=True) + EPS) * hn
    gated = _sigmoid(o) * hs
    return (gated, C_new, n_new), m_new


M_OFF_Q, M_OFF_K = 0, M_HEADS * M_QK
M_OFF_V = 2 * M_HEADS * M_QK
M_OFF_O = M_OFF_V + M_HEADS * M_V
M_OFF_G = M_OFF_O + M_HEADS * M_V
M_PROJ = M_OFF_G + LANES
M_PAIRS = M_HEADS * M_QK // LANES


def _head_cols(off, h):
    return slice(off + h * LANES, off + (h + 1) * LANES)


def _own_lanes(h, shape):
    low = lax.broadcasted_iota(jnp.int32, shape, 1) < M_QK
    return low if h % 2 == 0 else jnp.logical_not(low)


def _mlstm_specs(NC, rev):
    H, L = M_HEADS, M_CHUNK
    cc = (lambda c: NC - 1 - c) if rev else (lambda c: c)
    proj = pl.BlockSpec((L, M_PROJ), lambda c: (cc(c), 0))
    vec = pl.BlockSpec((1, LANES), lambda c: (0, 0))
    hn = pl.BlockSpec((1, H * M_V), lambda c: (0, 0))
    hv = pl.BlockSpec((L, H * M_V), lambda c: (cc(c), 0))
    Cs = pl.BlockSpec((None, H, LANES, M_V), lambda c: (cc(c), 0, 0, 0))
    ns = pl.BlockSpec((None, H, 1, LANES), lambda c: (cc(c), 0, 0, 0))
    ms = pl.BlockSpec((None, H, 1, 1), lambda c: (cc(c), 0, 0, 0))
    return proj, vec, hn, hv, Cs, ns, ms


_MLSTM_STATE = [pltpu.VMEM((M_HEADS, LANES, M_V), F32), pltpu.VMEM((M_HEADS, 1, LANES), F32),
                pltpu.VMEM((M_HEADS, 1, 1), F32)]


def _mlstm_fwd(proj, bias, hn, shards, *, name):
    S = proj.shape[0]
    H, NC = M_HEADS, S // M_CHUNK
    ps, vec, hns, hv, Cs, ns, ms = _mlstm_specs(NC, False)
    nsh = len(shards)
    any_spec = pl.BlockSpec(memory_space=pl.ANY)

    def body(p_ref, b_ref, hn_ref, *rest):
        gated_ref, C_all, n_all, m_all = rest[nsh:nsh + 4]
        C_s, n_s, m_s, send_sems, recv_sems = rest[2 * nsh + 4:]
        gather = _Gather(rest[:nsh], rest[nsh + 4:2 * nsh + 4], send_sems, recv_sems)

        @pl.when(pl.program_id(0) == 0)
        def _():
            C_s[...] = jnp.zeros_like(C_s)
            n_s[...] = jnp.zeros_like(n_s)
            m_s[...] = jnp.zeros_like(m_s)
            gather.issue()

        gate_terms = _mlstm_gates(p_ref[:, M_OFF_G:M_OFF_G + LANES], b_ref[...])
        for h in range(H):
            C, n, m = C_s[h], n_s[h], m_s[h]
            C_all[h] = C
            n_all[h] = n
            m_all[h] = m
            (gated, Cn, nn), mn = _mlstm_head(
                h, p_ref[:, _head_cols(M_OFF_Q, h // 2)], p_ref[:, _head_cols(M_OFF_K, h // 2)],
                p_ref[:, _head_cols(M_OFF_V, h)], p_ref[:, _head_cols(M_OFF_O, h)], *gate_terms,
                hn_ref[:, _head_cols(0, h)], C, n, m)
            gated_ref[:, _head_cols(0, h)] = gated.astype(gated_ref.dtype)
            C_s[h] = Cn
            n_s[h] = nn
            m_s[h] = mn

        @pl.when(pl.program_id(0) == NC - 1)
        def _():
            gather.finish()

    return pl.pallas_call(
        body, name=name, grid=(NC,),
        in_specs=[ps, vec, hns] + [any_spec] * nsh,
        out_specs=[hv, Cs, ns, ms] + [any_spec] * nsh,
        out_shape=[jax.ShapeDtypeStruct((S, H * M_V), MXU_DTYPE),
                   jax.ShapeDtypeStruct((NC, H, LANES, M_V), F32),
                   jax.ShapeDtypeStruct((NC, H, 1, LANES), F32),
                   jax.ShapeDtypeStruct((NC, H, 1, 1), F32)] + _Gather.out_shape(shards),
        scratch_shapes=list(_MLSTM_STATE) + _Gather.semaphores(nsh),
        compiler_params=_cparams(("arbitrary",)),
    )(proj, bias, hn, *shards)


def _mlstm_bwd(proj, bias, hn, C_all, n_all, m_all, dgated, qs, *, name):
    S = proj.shape[0]
    H, NC = M_HEADS, S // M_CHUNK
    ps, vec, hns, hv, Cs, ns, ms = _mlstm_specs(NC, True)
    nq = len(qs)
    any_spec = pl.BlockSpec(memory_space=pl.ANY)

    def body(p_ref, b_ref, hn_ref, C_ref, n_ref, m_ref, dg_ref, *rest):
        dp_ref, db_ref, dhn_ref = rest[nq:nq + 3]
        dC_s, dn_s, send_sems, recv_sems = rest[2 * nq + 3:]
        exchange = _Exchange(rest[:nq], rest[nq + 3:2 * nq + 3], send_sems, recv_sems)

        @pl.when(pl.program_id(0) == 0)
        def _():
            dC_s[...] = jnp.zeros_like(dC_s)
            dn_s[...] = jnp.zeros_like(dn_s)
            db_ref[...] = jnp.zeros_like(db_ref)
            dhn_ref[...] = jnp.zeros_like(dhn_ref)
            exchange.issue()

        gate_terms, gates_vjp = jax.vjp(_mlstm_gates, p_ref[:, M_OFF_G:M_OFF_G + LANES], b_ref[...])
        d_terms = [jnp.zeros_like(t) for t in gate_terms]
        for h in range(H):
            def head(q, k, v, o, *rest, h=h):
                return _mlstm_head(h, q, k, v, o, *rest, m_ref[h])

            prim = (p_ref[:, _head_cols(M_OFF_Q, h // 2)], p_ref[:, _head_cols(M_OFF_K, h // 2)],
                    p_ref[:, _head_cols(M_OFF_V, h)], p_ref[:, _head_cols(M_OFF_O, h)], *gate_terms,
                    hn_ref[:, _head_cols(0, h)], C_ref[h], n_ref[h])
            _, vjp, _ = jax.vjp(head, *prim, has_aux=True)
            dq, dk, dv, do, *d_gate, dhnh, dC, dn = vjp((dg_ref[:, _head_cols(0, h)].astype(F32), dC_s[h], dn_s[h]))
            if h % 2 == 0:
                dq_pair, dk_pair = dq, dk
            else:
                dp_ref[:, _head_cols(M_OFF_Q, h // 2)] = (dq_pair + dq).astype(dp_ref.dtype)
                dp_ref[:, _head_cols(M_OFF_K, h // 2)] = (dk_pair + dk).astype(dp_ref.dtype)
            dp_ref[:, _head_cols(M_OFF_V, h)] = dv.astype(dp_ref.dtype)
            dp_ref[:, _head_cols(M_OFF_O, h)] = do.astype(dp_ref.dtype)
            d_terms = [a + g for a, g in zip(d_terms, d_gate)]
            dhn_ref[:, _head_cols(0, h)] += dhnh
            dC_s[h] = dC
            dn_s[h] = dn
        dG, dbias = gates_vjp(tuple(d_terms))
        dp_ref[:, M_OFF_G:M_OFF_G + LANES] = dG.astype(dp_ref.dtype)
        db_ref[...] += dbias

        @pl.when(pl.program_id(0) == NC - 1)
        def _():
            exchange.finish()

    return pl.pallas_call(
        body, name=name, grid=(NC,),
        in_specs=[ps, vec, hns, Cs, ns, ms, hv] + [any_spec] * nq,
        out_specs=[ps, vec, hns] + [any_spec] * nq,
        out_shape=[jax.ShapeDtypeStruct((S, M_PROJ), MXU_DTYPE), jax.ShapeDtypeStruct((1, LANES), F32),
                   jax.ShapeDtypeStruct((1, H * M_V), F32)] + _Exchange.out_shape(qs),
        scratch_shapes=list(_MLSTM_STATE[:2]) + _Exchange.semaphores(nq),
        compiler_params=_cparams(("arbitrary",)),
    )(proj, bias, hn, C_all, n_all, m_all, dgated, *qs)


A_NQ = A_QH * A_DH
A_NKV = 2 * A_KVH * A_DH
A_PAIRS = A_G // 2


def _sink_softmax_parts(s, sink):
    mx = jnp.maximum(jnp.max(s, axis=-1, keepdims=True), sink)
    e = jnp.exp(s - mx)
    e_sink = jnp.exp(sink - mx)
    r = 1.0 / (jnp.sum(e, axis=-1, keepdims=True) + e_sink)
    return e * r, e_sink * r


@jax.custom_vjp
def _sink_softmax(s, sink):
    return _sink_softmax_parts(s, sink)[0]


def _sink_softmax_fwd(s, sink):
    p, p_sink = _sink_softmax_parts(s, sink)
    return p, (p, p_sink)


def _sink_softmax_bwd(res, dp):
    p, p_sink = res
    t = jnp.sum(dp * p, axis=-1, keepdims=True)
    return p * (dp - t), -p_sink * t


_sink_softmax.defvjp(_sink_softmax_fwd, _sink_softmax_bwd)


def _band_bias():
    qi = np.arange(A_G * A_BLK)[:, None] % A_BLK
    ku = np.arange(2 * A_BLK)[None, :]
    diff = qi - (ku - A_BLK)
    band = (diff >= 0) & (diff < A_BLK)
    seen = np.stack([band, band & (ku >= A_BLK)])
    return jnp.asarray(np.where(seen, 0.0, -np.inf), F32)


_BAND_BIAS_SPEC = pl.BlockSpec((None, A_G * A_BLK, 2 * A_BLK), lambda n: (jnp.where(n == 0, 1, 0), 0, 0))


def _attn_group(Ps, KLO, KHI, VLO, VHI, sinks, bias):
    B = Ps[0].shape[0]
    R = len(Ps) * B
    q2 = jnp.concatenate(Ps, axis=0) * (A_DH ** -0.5)
    s = jnp.concatenate([_dot(q2, KLO, 1, 1), _dot(q2, KHI, 1, 1)], axis=0) + bias
    ri = lax.broadcasted_iota(jnp.int32, (2 * R, A_G), 0)
    head = 2 * lax.shift_right_logical(ri & (R - 1), B.bit_length() - 1) + lax.shift_right_logical(ri, R.bit_length() - 1)
    onehot = head == lax.broadcasted_iota(jnp.int32, (2 * R, A_G), 1)
    sink = jnp.sum(jnp.where(onehot, sinks, 0.0), axis=1, keepdims=True)
    p = _sink_softmax(s, sink)
    o = _dot(p[:R], VLO, 1, 0) + _dot(p[R:], VHI, 1, 0)
    return tuple(o[j * B:(j + 1) * B] for j in range(len(Ps)))


def _swap_halves_of_lanes(t):
    return pltpu.roll(t, LANES // 2, 1)


def _kv_operands(kvp_ref, kvc_ref, h):
    kk = jnp.concatenate([kvp_ref[:, :LANES], kvc_ref[:, :LANES]], axis=0)
    vv = jnp.concatenate([kvp_ref[:, LANES:], kvc_ref[:, LANES:]], axis=0)
    low = lax.broadcasted_iota(jnp.int32, kk.shape, 1) < A_DH
    own = low if h == 0 else jnp.logical_not(low)
    k_own = jnp.where(own, kk, 0.0)
    v_own = jnp.where(own, vv, 0.0)
    k_oth, v_oth = _swap_halves_of_lanes(k_own), _swap_halves_of_lanes(v_own)
    if h == 0:
        return own, k_own, k_oth, v_own, v_oth
    return own, k_oth, k_own, v_oth, v_own


def _pair_cols(h, j):
    c = (h * A_PAIRS + j) * LANES
    return slice(c, c + LANES)


def _attn_fwd(proj, sinks, *, name):
    S = proj.shape[0]
    NB = S // A_BLK
    kv_blk = A_NQ // A_NKV
    qs = pl.BlockSpec((A_BLK, A_NQ), lambda n: (n, 0))
    cur = pl.BlockSpec((A_BLK, A_NKV), lambda n: (n, kv_blk))
    prev = pl.BlockSpec((A_BLK, A_NKV), lambda n: (jnp.maximum(n - 1, 0), kv_blk))
    sk = pl.BlockSpec((A_KVH, A_G), lambda n: (0, 0))

    def body(q_ref, kvp_ref, kvc_ref, s_ref, bias_ref, o_ref):
        for h in range(A_KVH):
            _, KLO, KHI, VLO, VHI = _kv_operands(kvp_ref, kvc_ref, h)
            Ps = tuple(q_ref[:, _pair_cols(h, j)] for j in range(A_PAIRS))
            outs = _attn_group(Ps, KLO, KHI, VLO, VHI, s_ref[h:h + 1, :], bias_ref[...])
            for j in range(A_PAIRS):
                o_ref[:, _pair_cols(h, j)] = outs[j].astype(o_ref.dtype)

    return pl.pallas_call(
        body, name=name, grid=(NB,),
        in_specs=[qs, prev, cur, sk, _BAND_BIAS_SPEC], out_specs=qs,
        out_shape=jax.ShapeDtypeStruct((S, A_NQ), MXU_DTYPE),
        compiler_params=_cparams(("parallel",)),
    )(proj, proj, proj, sinks, _band_bias())


def _attn_bwd(proj, sinks, do, *, name):
    S = proj.shape[0]
    NB = S // A_BLK
    last = NB - 1
    kv_blk = A_NQ // A_NKV
    qs = pl.BlockSpec((A_BLK, A_NQ), lambda n: (jnp.minimum(n, last), 0))
    cur = pl.BlockSpec((A_BLK, A_NKV), lambda n: (jnp.minimum(n, last), kv_blk))
    prev = pl.BlockSpec((A_BLK, A_NKV), lambda n: (jnp.clip(n - 1, 0, last), kv_blk))
    sk = pl.BlockSpec((A_KVH, A_G), lambda n: (0, 0))
    lag = pl.BlockSpec((A_BLK, A_NKV), lambda n: (jnp.maximum(n - 1, 0), 0))
    cq_spec = pl.BlockSpec((1, A_NQ), lambda n: (0, 0))
    ckv_spec = pl.BlockSpec((1, A_NKV), lambda n: (0, 0))

    def body(q_ref, kvp_ref, kvc_ref, s_ref, do_ref, bias_ref, dq_ref, dkv_ref, ds_ref, cq_ref, ckv_ref, keep):
        n = pl.program_id(0)

        @pl.when(n == 0)
        def _():
            keep[...] = jnp.zeros_like(keep)
            ds_ref[...] = jnp.zeros_like(ds_ref)
            cq_ref[...] = jnp.zeros_like(cq_ref)
            ckv_ref[...] = jnp.zeros_like(ckv_ref)

        @pl.when(n < NB)
        def _():
            f = functools.partial(_attn_group, bias=bias_ref[...])
            dk = jnp.zeros((2 * A_BLK, LANES), F32)
            dv = jnp.zeros((2 * A_BLK, LANES), F32)
            ds_rows = []
            for h in range(A_KVH):
                own, KLO, KHI, VLO, VHI = _kv_operands(kvp_ref, kvc_ref, h)
                Ps = tuple(q_ref[:, _pair_cols(h, j)] for j in range(A_PAIRS))
                _, vjp = jax.vjp(f, Ps, KLO, KHI, VLO, VHI, s_ref[h:h + 1, :])
                dPs, dKLO, dKHI, dVLO, dVHI, dsk = vjp(
                    tuple(do_ref[:, _pair_cols(h, j)].astype(F32) for j in range(A_PAIRS)))
                for j in range(A_PAIRS):
                    dq_ref[:, _pair_cols(h, j)] = dPs[j].astype(dq_ref.dtype)
                    cq_ref[:, _pair_cols(h, j)] += jnp.sum(dPs[j], axis=0, keepdims=True)
                dk_own, dk_oth = (dKLO, dKHI) if h == 0 else (dKHI, dKLO)
                dv_own, dv_oth = (dVLO, dVHI) if h == 0 else (dVHI, dVLO)
                dk = dk + jnp.where(own, dk_own, 0.0) + _swap_halves_of_lanes(jnp.where(own, 0.0, dk_oth))
                dv = dv + jnp.where(own, dv_own, 0.0) + _swap_halves_of_lanes(jnp.where(own, 0.0, dv_oth))
                ds_rows.append(dsk)
            ds_ref[...] += jnp.concatenate(ds_rows, axis=0)
            dkv = jnp.concatenate([dk, dv], axis=1)
            done = keep[...] + dkv[:A_BLK]
            dkv_ref[...] = done.astype(dkv_ref.dtype)
            ckv_ref[...] += jnp.sum(done, axis=0, keepdims=True)
            keep[...] = dkv[A_BLK:]

        @pl.when(n == NB)
        def _():
            done = keep[...]
            dkv_ref[...] = done.astype(dkv_ref.dtype)
            ckv_ref[...] += jnp.sum(done, axis=0, keepdims=True)

    return pl.pallas_call(
        body, name=name, grid=(NB + 1,),
        in_specs=[qs, prev, cur, sk, qs, _BAND_BIAS_SPEC],
        out_specs=[qs, lag, sk, cq_spec, ckv_spec],
        out_shape=[jax.ShapeDtypeStruct((S, A_NQ), MXU_DTYPE), jax.ShapeDtypeStruct((S, A_NKV), MXU_DTYPE),
                   jax.ShapeDtypeStruct((A_KVH, A_G), F32), jax.ShapeDtypeStruct((1, A_NQ), F32),
                   jax.ShapeDtypeStruct((1, A_NKV), F32)],
        scratch_shapes=[pltpu.VMEM((A_BLK, A_NKV), F32)],
        compiler_params=_cparams(("arbitrary",)),
    )(proj, proj, proj, sinks, do, _band_bias())


HALO = 8


def _shift_rows(t, j):
    return pltpu.roll(t, j % t.shape[0], 0)


def _conv_taps(gate_ext):
    return _shift_rows(gate_ext, 2), _shift_rows(gate_ext, 1), gate_ext


def _conv_gate(g2, g1, g0, cw, cb):
    return cb + cw[0:1, :] * g2 + cw[1:2, :] * g1 + cw[2:3, :] * g0


def _convgate_fwd(u, cw, cb, *, name):
    S, F2 = u.shape
    F = F2 // 2
    tm = _pick(S, (512, 256, 128))
    hb = tm // HALO
    urow = pl.BlockSpec((tm, F2), lambda i: (i, 0))
    uprev = pl.BlockSpec((HALO, F), lambda i: (jnp.maximum(i * hb - 1, 0), 0))

    def body(u_ref, up_ref, cw_ref, cb_ref, a_ref):
        i = pl.program_id(0)
        prev = jnp.where(i > 0, up_ref[...], 0.0)
        gc = _conv_gate(*_conv_taps(jnp.concatenate([prev, u_ref[:, :F]], axis=0)), cw_ref[...], cb_ref[...])[HALO:]
        a_ref[...] = (gc * _sigmoid(gc) * u_ref[:, F:]).astype(a_ref.dtype)

    return pl.pallas_call(
        body, name=name, grid=(S // tm,),
        in_specs=[urow, uprev, pl.BlockSpec((3, F), lambda i: (0, 0)), pl.BlockSpec((1, F), lambda i: (0, 0))],
        out_specs=pl.BlockSpec((tm, F), lambda i: (i, 0)),
        out_shape=jax.ShapeDtypeStruct((S, F), MXU_DTYPE),
        compiler_params=_cparams(("parallel",)),
    )(u, u, cw, cb)


def _convgate_bwd(u, da, cw, cb, *, name):
    S, F2 = u.shape
    F = F2 // 2
    tm = _pick(S, (256, 128))
    hb = tm // HALO
    nt = S // tm
    nh = S // HALO
    urow = pl.BlockSpec((tm, F2), lambda i: (i, 0))
    uprev = pl.BlockSpec((HALO, F), lambda i: (jnp.maximum(i * hb - 1, 0), 0))
    unext = pl.BlockSpec((HALO, F2), lambda i: (jnp.minimum((i + 1) * hb, nh - 1), 0))
    darow = pl.BlockSpec((tm, F), lambda i: (i, 0))
    danext = pl.BlockSpec((HALO, F), lambda i: (jnp.minimum((i + 1) * hb, nh - 1), 0))

    def body(u_ref, up_ref, un_ref, da_ref, dan_ref, cw_ref, cb_ref, du_ref, dcw_ref, dcb_ref):
        i = pl.program_id(0)
        cwv = cw_ref[...]
        prev = jnp.where(i > 0, up_ref[...], 0.0)
        gate_ext = jnp.concatenate([prev, u_ref[:, :F], un_ref[:, :F]], axis=0)
        val_ext = jnp.concatenate([u_ref[:, F:], un_ref[:, F:]], axis=0)
        da_next = jnp.where(i < nt - 1, dan_ref[...].astype(F32), 0.0)
        da_ext = jnp.concatenate([da_ref[...].astype(F32), da_next], axis=0)
        g2, g1, g0 = (t[HALO:] for t in _conv_taps(gate_ext))
        gc = _conv_gate(g2, g1, g0, cwv, cb_ref[...])
        sg = _sigmoid(gc)
        silu = gc * sg
        dval = da_ext * silu
        dgc = da_ext * val_ext * (sg * (1.0 + gc * (1.0 - sg)))
        dgate = cwv[2:3, :] * dgc + cwv[1:2, :] * _shift_rows(dgc, -1) + cwv[0:1, :] * _shift_rows(dgc, -2)
        du_ref[:, :F] = dgate[:tm].astype(du_ref.dtype)
        du_ref[:, F:] = dval[:tm].astype(du_ref.dtype)
        dgc_c = dgc[:tm]
        dcw = jnp.concatenate([jnp.sum(dgc_c * g2[:tm], axis=0, keepdims=True),
                               jnp.sum(dgc_c * g1[:tm], axis=0, keepdims=True),
                               jnp.sum(dgc_c * g0[:tm], axis=0, keepdims=True)], axis=0)
        dcb = jnp.sum(dgc_c, axis=0, keepdims=True)

        @pl.when(i == 0)
        def _():
            dcw_ref[...] = dcw
            dcb_ref[...] = dcb

        @pl.when(i > 0)
        def _():
            dcw_ref[...] += dcw
            dcb_ref[...] += dcb

    return pl.pallas_call(
        body, name=name, grid=(nt,),
        in_specs=[urow, uprev, unext, darow, danext,
                  pl.BlockSpec((3, F), lambda i: (0, 0)), pl.BlockSpec((1, F), lambda i: (0, 0))],
        out_specs=[urow, pl.BlockSpec((3, F), lambda i: (0, 0)), pl.BlockSpec((1, F), lambda i: (0, 0))],
        out_shape=[jax.ShapeDtypeStruct((S, F2), MXU_DTYPE), jax.ShapeDtypeStruct((3, F), F32),
                   jax.ShapeDtypeStruct((1, F), F32)],
        compiler_params=_cparams(("arbitrary",)),
    )(u, u, u, da, da, cw, cb)


def _adamw_math(w, g, m, v):
    m = ADAM_B1 * m + (1.0 - ADAM_B1) * g
    v = ADAM_B2 * v + (1.0 - ADAM_B2) * (g * g)
    m_hat = m / (1.0 - ADAM_B1 ** ADAM_STEP)
    v_hat = v / (1.0 - ADAM_B2 ** ADAM_STEP)
    delta = -ADAM_LR * (m_hat / (jnp.sqrt(v_hat) + ADAM_EPS) + ADAM_WD * w)
    return delta, m, v


def _adamw_layers(w, gs, m, v, *, name):
    Lr, R, C = w.shape
    tr = _pick(R, (256, 128, 64, 32, 16, 8))
    outs = None
    for layer, g in enumerate(gs):
        def body(w_ref, g_ref, m_ref, v_ref, *rest):
            go_ref, d_ref, nm_ref, nv_ref = rest[-4:]
            gv = g_ref[...]
            d, nm, nv = _adamw_math(w_ref[...], gv, m_ref[...], v_ref[...])
            go_ref[...] = gv
            d_ref[...] = d
            nm_ref[...] = nm
            nv_ref[...] = nv

        lay = pl.BlockSpec((None, tr, C), lambda i, layer=layer: (layer, i, 0))
        in_specs = [lay, pl.BlockSpec((tr, C), lambda i: (i, 0)), lay, lay]
        args = [w, g, m, v]
        aliases = {}
        if outs is not None:
            in_specs += [pl.BlockSpec(memory_space=pl.ANY)] * 4
            args += list(outs)
            aliases = {4 + t: t for t in range(4)}
        outs = pl.pallas_call(
            body, name=f"{name}_{layer}", grid=(R // tr,), in_specs=in_specs, out_specs=[lay] * 4,
            out_shape=[jax.ShapeDtypeStruct((Lr, R, C), F32)] * 4, input_output_aliases=aliases,
            compiler_params=_cparams(("parallel",)),
        )(*args)
    return outs


def _adamw_small(items, *, name):
    n = len(items)

    def body(*refs):
        ins, outs = refs[:4 * n], refs[4 * n:]
        for t in range(n):
            w, g, m, v = (r[...] for r in ins[4 * t:4 * t + 4])
            d, nm, nv = _adamw_math(w, g, m, v)
            outs[3 * t][...] = d
            outs[3 * t + 1][...] = nm
            outs[3 * t + 2][...] = nv

    flat = [a for it in items for a in it]
    out_shape = [jax.ShapeDtypeStruct(it[0].shape, F32) for it in items for _ in range(3)]
    vm = pl.BlockSpec(memory_space=pltpu.VMEM)
    res = pl.pallas_call(body, name=name, in_specs=[vm] * len(flat), out_specs=[vm] * len(out_shape),
                         out_shape=out_shape)(*flat)
    return [tuple(res[3 * t:3 * t + 3]) for t in range(n)]


def _place():
    return lax.axis_index("x"), lax.axis_index("y"), lax.axis_index("c")


_FLIPS = ((1, 0), (0, 1), (1, 1))


ROW_ALIGN = 16


def _half(rows, which):
    return pl.ds(pl.multiple_of(which * (rows // 2), ROW_ALIGN), rows // 2)


def _remote(src, dst, send_sems, recv_sems, k, to):
    return pltpu.make_async_remote_copy(src_ref=src, dst_ref=dst, send_sem=send_sems.at[k], recv_sem=recv_sems.at[k],
                                        device_id=to, device_id_type=MESH)


class _Gather:
    def __init__(self, w_refs, out_refs, send_sems, recv_sems):
        self.w_refs, self.out_refs, self.send_sems, self.recv_sems = w_refs, out_refs, send_sems, recv_sems
        self.pairs = [(i, j) for i in range(len(w_refs)) for j in range(3)]

    @staticmethod
    def out_shape(shards):
        return [jax.ShapeDtypeStruct((N_CHIPS,) + s.shape, s.dtype) for s in shards]

    @staticmethod
    def semaphores(n):
        return [pltpu.SemaphoreType.DMA((6 * n,)), pltpu.SemaphoreType.DMA((6 * n,))]

    def _where(self):
        x, y, c = _place()
        return x, y, c, [(x ^ fx, y ^ fy) for fx, fy in _FLIPS]

    def _over_ici(self, i, j, landed):
        x, y, c, chips = self._where()
        px, py = chips[j]
        mine = _half(self.w_refs[i].shape[0], c)
        if landed:
            src = dst = self.out_refs[i].at[2 * px + py, mine]
        else:
            src, dst = self.w_refs[i].at[mine], self.out_refs[i].at[2 * x + y, mine]
        return _remote(src, dst, self.send_sems, self.recv_sems, 6 * i + j, (px, py, c))

    def _over_d2d(self, i, j, which):
        x, y, c, chips = self._where()
        px, py = chips[j]
        blk = self.out_refs[i].at[2 * px + py, _half(self.w_refs[i].shape[0], which)]
        return _remote(blk, blk, self.send_sems, self.recv_sems, 6 * i + 3 + j, (x, y, 1 - c))

    def issue(self):
        for i, j in self.pairs:
            self._over_ici(i, j, False).start()

    def finish(self):
        c = lax.axis_index("c")
        for i, j in self.pairs:
            self._over_ici(i, j, True).wait_recv()
            self._over_d2d(i, j, c).start()
        for i, j in self.pairs:
            self._over_d2d(i, j, 1 - c).wait_recv()
        for i, j in self.pairs:
            self._over_ici(i, j, False).wait_send()
            self._over_d2d(i, j, c).wait_send()


def _swap_halves(gs, *, name):
    n = len(gs)
    any_spec = pl.BlockSpec(memory_space=pl.ANY)

    def body(*refs):
        swap = _Swap(refs[:n], refs[n:2 * n], refs[2 * n], refs[2 * n + 1])
        swap.issue()
        swap.finish()

    return pl.pallas_call(
        body, name=name, in_specs=[any_spec] * n, out_specs=[any_spec] * n,
        out_shape=_Swap.out_shape(gs), scratch_shapes=_Swap.semaphores(n),
    )(*gs)


class _Swap:
    def __init__(self, g_refs, out_refs, send_sems, recv_sems):
        self.g_refs, self.out_refs, self.send_sems, self.recv_sems = g_refs, out_refs, send_sems, recv_sems

    @staticmethod
    def out_shape(gs):
        return [jax.ShapeDtypeStruct((N_CHIPS, g.shape[1] // 2, g.shape[2]), g.dtype) for g in gs]

    @staticmethod
    def semaphores(n):
        return [pltpu.SemaphoreType.DMA((n,)), pltpu.SemaphoreType.DMA((n,))]

    def _copies(self):
        x, y, c = _place()
        return [_remote(g.at[:, _half(g.shape[1], 1 - c)], out, self.send_sems, self.recv_sems, i, (x, y, 1 - c))
                for i, (g, out) in enumerate(zip(self.g_refs, self.out_refs))]

    def issue(self):
        for cp in self._copies():
            cp.start()

    def finish(self):
        for cp in self._copies():
            cp.wait()


def _add_halves(g, got, c_arr, *, name):
    _, rows, cols = g.shape
    blk = (None, rows // 2, cols)

    def body(c_ref, g_ref, got_ref, o_ref):
        o_ref[...] = (g_ref[...] + got_ref[...]).astype(o_ref.dtype)

    return pl.pallas_call(
        body, name=name,
        grid_spec=pltpu.PrefetchScalarGridSpec(
            num_scalar_prefetch=1, grid=(N_CHIPS,),
            in_specs=[pl.BlockSpec(blk, lambda j, c_ref: (j, c_ref[0], 0)), pl.BlockSpec(blk, lambda j, c_ref: (j, 0, 0))],
            out_specs=pl.BlockSpec(blk, lambda j, c_ref: (j, 0, 0))),
        out_shape=jax.ShapeDtypeStruct((N_CHIPS, rows // 2, cols), WIRE_DTYPE),
        compiler_params=_cparams(("parallel",)),
    )(c_arr, g, got)


class _Exchange:
    def __init__(self, q_refs, out_refs, send_sems, recv_sems):
        self.q_refs, self.out_refs, self.send_sems, self.recv_sems = q_refs, out_refs, send_sems, recv_sems
        self.pairs = [(i, j) for i in range(len(q_refs)) for j in range(3)]

    @staticmethod
    def out_shape(qs):
        return [jax.ShapeDtypeStruct(q.shape, q.dtype) for q in qs]

    @staticmethod
    def semaphores(n):
        return [pltpu.SemaphoreType.DMA((3 * n,)), pltpu.SemaphoreType.DMA((3 * n,))]

    def _copy(self, i, j, landed):
        x, y, c = _place()
        px, py = [(x ^ fx, y ^ fy) for fx, fy in _FLIPS][j]
        if landed:
            src = dst = self.out_refs[i].at[2 * px + py]
        else:
            src, dst = self.q_refs[i].at[2 * px + py], self.out_refs[i].at[2 * x + y]
        return _remote(src, dst, self.send_sems, self.recv_sems, 3 * i + j, (px, py, c))

    def issue(self):
        for i, j in self.pairs:
            self._copy(i, j, False).start()

    def finish(self):
        for i, j in self.pairs:
            self._copy(i, j, True).wait_recv()
        for i, j in self.pairs:
            self._copy(i, j, False).wait_send()


def _sum_chips(q, r, place_arr, *, name):
    _, h, cols = q.shape
    blk = (None, h, cols)

    def body(p_ref, q_ref, r1_ref, r2_ref, r3_ref, o_ref):
        o_ref[...] = ((q_ref[...].astype(F32) + r1_ref[...].astype(F32)) + r2_ref[...].astype(F32)) + r3_ref[...].astype(F32)

    other = [pl.BlockSpec(blk, lambda i, p_ref, f=f: (p_ref[0] ^ f, 0, 0)) for f in (1, 2, 3)]
    return pl.pallas_call(
        body, name=name,
        grid_spec=pltpu.PrefetchScalarGridSpec(
            num_scalar_prefetch=1, grid=(1,),
            in_specs=[pl.BlockSpec(blk, lambda i, p_ref: (p_ref[0], 0, 0))] + other,
            out_specs=pl.BlockSpec((h, cols), lambda i, p_ref: (p_ref[1], 0))),
        out_shape=jax.ShapeDtypeStruct((2 * h, cols), F32),
        compiler_params=_cparams(("arbitrary",)),
    )(place_arr, q, r, r, r)


def _join_halves(fs, *, name):
    n = len(fs)
    any_spec = pl.BlockSpec(memory_space=pl.ANY)

    def body(*refs):
        out_refs, send_sems, recv_sems = refs[n:2 * n], refs[2 * n], refs[2 * n + 1]
        x, y, c = _place()
        sent = []
        for i in range(n):
            mine = out_refs[i].at[_half(fs[i].shape[0], c)]
            cp = _remote(mine, mine, send_sems, recv_sems, i, (x, y, 1 - c))
            cp.start()
            sent.append(cp)
        for i in range(n):
            its = out_refs[i].at[_half(fs[i].shape[0], 1 - c)]
            _remote(its, its, send_sems, recv_sems, i, (x, y, 1 - c)).wait_recv()
        for cp in sent:
            cp.wait_send()

    return pl.pallas_call(
        body, name=name, in_specs=[any_spec] * n, out_specs=[any_spec] * n,
        out_shape=[jax.ShapeDtypeStruct(f.shape, f.dtype) for f in fs], input_output_aliases={i: i for i in range(n)},
        scratch_shapes=[pltpu.SemaphoreType.DMA((n,)), pltpu.SemaphoreType.DMA((n,))],
    )(*fs)


def _allreduce_small(buf, *, name):
    R = buf.shape[0]
    vm = pl.BlockSpec(memory_space=pltpu.VMEM)

    def body(b_ref, o_ref, slots, send_sems, recv_sems):
        x, y, c = _place()
        me = 4 * x + 2 * y + c
        slots[me] = b_ref[...]
        sends = []
        for kk in range(1, 8):
            fx, fy, fc = (kk >> 2) & 1, (kk >> 1) & 1, kk & 1
            cp = pltpu.make_async_remote_copy(
                src_ref=b_ref, dst_ref=slots.at[me], send_sem=send_sems.at[kk - 1], recv_sem=recv_sems.at[kk - 1],
                device_id=(x ^ fx, y ^ fy, c ^ fc), device_id_type=MESH)
            cp.start()
            sends.append(cp)
        for kk in range(1, 8):
            fx, fy, fc = (kk >> 2) & 1, (kk >> 1) & 1, kk & 1
            peer = 4 * (x ^ fx) + 2 * (y ^ fy) + (c ^ fc)
            pltpu.make_async_remote_copy(
                src_ref=b_ref, dst_ref=slots.at[peer], send_sem=send_sems.at[kk - 1], recv_sem=recv_sems.at[kk - 1],
                device_id=(x ^ fx, y ^ fy, c ^ fc), device_id_type=MESH).wait_recv()
        for cp in sends:
            cp.wait_send()
        acc = slots[0]
        for d in range(1, 8):
            acc = acc + slots[d]
        o_ref[...] = acc

    return pl.pallas_call(
        body, name=name, in_specs=[vm], out_specs=vm,
        out_shape=jax.ShapeDtypeStruct((R, LANES), F32),
        scratch_shapes=[pltpu.VMEM((8, R, LANES), F32), pltpu.SemaphoreType.DMA((7,)), pltpu.SemaphoreType.DMA((7,))],
    )(buf)


def _pad_rows(v, mult=8 * LANES):
    flat = v.reshape(-1)
    n = flat.shape[0]
    tot = -(-n // mult) * mult
    return jnp.pad(flat, (0, tot - n)).reshape(-1, LANES)


def _pack_small(parts):
    return jnp.concatenate([_pad_rows(p.astype(F32)) for p in parts], axis=0)


def _unpack_small(buf, shapes):
    out, r = [], 0
    for sh in shapes:
        n = math.prod(sh)
        rows = -(-n // (8 * LANES)) * 8
        out.append(buf[r:r + rows].reshape(-1)[:n].reshape(sh))
        r += rows
    return out


def _cols_to_shards(w):
    *lead, K, N = w.shape
    t = w.reshape(*lead, K, N_CHIPS, N // N_CHIPS)
    return jnp.moveaxis(t, -2, 0)


def _shards_to_cols(t):
    t = jnp.moveaxis(t, 0, -2)
    *lead, K, _, n = t.shape
    return t.reshape(*lead, K, N_CHIPS * n)


_BIG = ("m_w_in", "m_w_out", "a_w_in", "a_w_out", "f_w_up", "f_w_down")


def kernel(x, m_w_in, m_gate_bias, m_head_norm, m_w_out, a_w_in, a_b_in, a_sinks, a_w_out, a_b_out, norm_mix_pre, norm_mix_post, norm_ffn_pre, norm_ffn_post, f_w_up, f_conv_w, f_conv_b, f_w_down, loss_target, m_m_w_in, m_m_gate_bias, m_m_head_norm, m_m_w_out, m_a_w_in, m_a_b_in, m_a_sinks, m_a_w_out, m_a_b_out, m_norm_mix_pre, m_norm_mix_post, m_norm_ffn_pre, m_norm_ffn_post, m_f_w_up, m_f_conv_w, m_f_conv_b, m_f_w_down, v_m_w_in, v_m_gate_bias, v_m_head_norm, v_m_w_out, v_a_w_in, v_a_b_in, v_a_sinks, v_a_w_out, v_a_b_out, v_norm_mix_pre, v_norm_mix_post, v_norm_ffn_pre, v_norm_ffn_post, v_f_w_up, v_f_conv_w, v_f_conv_b, v_f_w_down):
    params = dict(m_w_in=m_w_in, m_gate_bias=m_gate_bias, m_head_norm=m_head_norm, m_w_out=m_w_out, a_w_in=a_w_in,
                  a_b_in=a_b_in, a_sinks=a_sinks, a_w_out=a_w_out, a_b_out=a_b_out, norm_mix_pre=norm_mix_pre,
                  norm_mix_post=norm_mix_post, norm_ffn_pre=norm_ffn_pre, norm_ffn_post=norm_ffn_post, f_w_up=f_w_up,
                  f_conv_w=f_conv_w, f_conv_b=f_conv_b, f_w_down=f_w_down)
    mom1 = dict(m_w_in=m_m_w_in, m_gate_bias=m_m_gate_bias, m_head_norm=m_m_head_norm, m_w_out=m_m_w_out,
                a_w_in=m_a_w_in, a_b_in=m_a_b_in, a_sinks=m_a_sinks, a_w_out=m_a_w_out, a_b_out=m_a_b_out,
                norm_mix_pre=m_norm_mix_pre, norm_mix_post=m_norm_mix_post, norm_ffn_pre=m_norm_ffn_pre,
                norm_ffn_post=m_norm_ffn_post, f_w_up=m_f_w_up, f_conv_w=m_f_conv_w, f_conv_b=m_f_conv_b,
                f_w_down=m_f_w_down)
    mom2 = dict(m_w_in=v_m_w_in, m_gate_bias=v_m_gate_bias, m_head_norm=v_m_head_norm, m_w_out=v_m_w_out,
                a_w_in=v_a_w_in, a_b_in=v_a_b_in, a_sinks=v_a_sinks, a_w_out=v_a_w_out, a_b_out=v_a_b_out,
                norm_mix_pre=v_norm_mix_pre, norm_mix_post=v_norm_mix_post, norm_ffn_pre=v_norm_ffn_pre,
                norm_ffn_post=v_norm_ffn_post, f_w_up=v_f_w_up, f_conv_w=v_f_conv_w, f_conv_b=v_f_conv_b,
                f_w_down=v_f_w_down)
    order = list(params)

    mx, my, mc = _place()
    chip = 2 * mx + my
    h0 = x[0]
    target = loss_target[0]

    def two_d(t):
        return t.reshape(-1, t.shape[-1])

    early = ("m_w_in",)
    late = tuple(n for n in _BIG if n not in early)
    mine = {n: two_d(params[n]).astype(MXU_DTYPE) for n in _BIG}

    def with_own_slot(names, theirs):
        return {n: lax.dynamic_update_slice(t, mine[n][None], (chip, 0, 0)) for n, t in zip(names, theirs)}

    z0, *early_theirs = _rms_fwd(h0, norm_mix_pre[0:1], [mine[n] for n in early], out_dtype=MXU_DTYPE,
                                 name="mix_pre_norm0")
    gathered = with_own_slot(early, early_theirs)

    def in_place(shard, axis):
        width = shard.shape[axis]
        z = jnp.zeros(shard.shape[:axis] + (N_CHIPS * width,) + shard.shape[axis + 1:], F32)
        contrib = jnp.where(mc == 0, shard, 0.0)
        return lax.dynamic_update_slice_in_dim(z, contrib, chip * width, axis)

    sm_in = [in_place(a_b_in, 1), in_place(a_b_out, 1), in_place(f_conv_w, 2)]
    sm_full = _unpack_small(_allreduce_small(_pack_small(sm_in), name="gather_small"), [t.shape for t in sm_in])
    b_in_full, b_out_full, conv_w_full = sm_full

    W_in = _shards_to_cols(gathered["m_w_in"])
    W_all = jnp.pad(W_in, ((0, 0), (0, M_PROJ - W_in.shape[1])))
    gbias = jnp.pad(m_gate_bias[0].reshape(1, 2 * M_HEADS), ((0, 0), (0, LANES - 2 * M_HEADS)))

    grads = {}

    def ffn_fwd(i, z1, end):
        u = _mm(z1, W_up, b_layer=i, name=f"ffn_up{i}")
        a = _convgate_fwd(u, conv_w_full[i], f_conv_b[i:i + 1], name=f"ffn_act{i}")
        return u, a, _mm(a, W_down[i], rows=end, name=f"ffn_down{i}")

    proj = _mm(z0, W_all, name="mlstm_proj")
    gated, C_all, n_all, m_all, *late_theirs = _mlstm_fwd(proj, gbias, m_head_norm, [mine[n] for n in late],
                                                          name="mlstm_fwd")
    gathered.update(with_own_slot(late, late_theirs))
    W_mout = gathered["m_w_out"].reshape(D_MODEL, D_MODEL)
    A_in, A_out = _shards_to_cols(gathered["a_w_in"]), gathered["a_w_out"].reshape(D_MODEL, D_MODEL)
    W_up = gathered["f_w_up"]
    dsh = D_FF // N_CHIPS
    W_down = [gathered["f_w_down"][:, i * dsh:(i + 1) * dsh].reshape(D_FF, D_MODEL) for i in range(2)]
    zm0, h1, z1 = _mm(gated, W_mout, rows=_rows_sublayer_end(h0, norm_mix_post[0:1], norm_ffn_pre[0:1]), name="mlstm_out")
    u0, a0, (zf0, h2, y0) = ffn_fwd(0, z1, _rows_sublayer_end(h1, norm_ffn_post[0:1], norm_mix_pre[1:2]))

    aproj = _mm(y0, A_in, bias=b_in_full, name="attn_proj")
    sinks = a_sinks.reshape(A_KVH, A_G)
    ao = _attn_fwd(aproj, sinks, name="attn_fwd")
    zm1, h3, z3 = _mm(ao, A_out, bias=b_out_full, rows=_rows_sublayer_end(h2, norm_mix_post[1:2], norm_ffn_pre[1:2]),
                      name="attn_out")
    u1, a1, (zf1, dh, dzf1, g_post1, loss_lanes) = ffn_fwd(1, z3, _rows_last_sublayer_end(h3, norm_ffn_post[1:2], target))
    loss = lax.psum(loss_lanes[0, 0], ("x", "y", "c"))

    g_post, g_fpre, g_mpost, g_mpre = [None, g_post1], [None, None], [None, None], [None, None]
    dW_up, dW_down, dconv_w, dconv_b = [None, None], [None, None], [None, None], [None, None]

    def by_rows(g):
        return g.reshape(N_CHIPS, -1, g.shape[-1])

    def ffn_bwd(i, dzf, z1_, u, a, start, ride):
        da = _mm(dzf, W_down[i], tb=True, name=f"ffn_down_dx{i}")
        dW_down[i] = _mm(a, dzf, ta=True, name=f"ffn_down_dw{i}")
        du, dconv_w[i], dconv_b[i] = _convgate_bwd(u, da, conv_w_full[i], f_conv_b[i:i + 1], name=f"ffn_act_bwd{i}")
        dW_up[i] = _mm(z1_, du, ta=True, out_stacked=True, name=f"ffn_up_dw{i}")
        riders = (list(ride) + [by_rows(dW_down[i]), dW_up[i]]) if ride is not None else []
        outs = _mm(du, W_up, tb=True, b_layer=i, rows=start, ride=(_Swap, riders) if riders else None,
                   name=f"ffn_up_dx{i}")
        return outs[:len(start.outs)], riders, outs[len(start.outs):]

    (dh3, g_fpre[1], dzm1, g_mpost[1], db_out), _, _ = ffn_bwd(
        1, dzf1, z3, u1, a1, _rows_sublayer_start_bwd(h3, dh, zm1, norm_ffn_pre[1:2], norm_mix_post[1:2]), None)
    dao = _mm(dzm1, A_out, tb=True, name="attn_out_dx")
    dA_out = _mm(ao, dzm1, ta=True, name="attn_out_dw")
    daq, dakv, dsinks, cs_q, cs_kv = _attn_bwd(aproj, sinks, dao, name="attn_bwd")
    db_in = jnp.concatenate([cs_q, cs_kv], axis=1)
    dh2, g_mpre[1], dzf0, g_post[0], _ = _mm(
        daq, A_in[:, :A_NQ], tb=True, add=_mm(dakv, A_in[:, A_NQ:], tb=True, name="attn_proj_kv_dx"),
        rows=_rows_sublayer_start_bwd(h2, dh3, zf0, norm_mix_pre[1:2], norm_ffn_post[0:1]), name="attn_proj_q_dx")
    dA_in = jnp.concatenate([_mm(y0, daq, ta=True, name="attn_proj_q_dw"),
                             _mm(y0, dakv, ta=True, name="attn_proj_kv_dw")], axis=1)

    c_arr = jnp.reshape(mc, (1,)).astype(jnp.int32)
    place_arr = jnp.stack([chip, mc]).astype(jnp.int32)

    def add_halves(tags, gs, got):
        return [_add_halves(g, t, c_arr, name=f"grad_add_halves_{tag}") for tag, g, t in zip(tags, gs, got)]

    (dh1, g_fpre[0], dzm0, g_mpost[0], _), rode, rode_got = ffn_bwd(
        0, dzf0, z1, u0, a0, _rows_sublayer_start_bwd(h1, dh2, zm0, norm_ffn_pre[0:1], norm_mix_post[0:1]),
        [_cols_to_shards(dA_in), by_rows(dA_out), dW_up[1], by_rows(dW_down[1])])
    rode_tags = ["a_w_in", "a_w_out", "f_w_up1", "f_w_down1", "f_w_down0", "f_w_up0"]
    dgated = _mm(dzm0, W_mout, tb=True, name="mlstm_out_dx")
    dW_mout = _mm(gated, dzm0, ta=True, name="mlstm_out_dw")
    next_tags, next_gs = ["m_w_out"], [by_rows(dW_mout)]
    late_tags = rode_tags + next_tags
    late_part = (add_halves(rode_tags, rode, rode_got)
                 + add_halves(next_tags, next_gs, _swap_halves(next_gs, name="grad_swap_halves_late")))
    dproj, dgbias, dhn, *late_from = _mlstm_bwd(proj, gbias, m_head_norm, C_all, n_all, m_all, dgated, late_part,
                                                name="mlstm_bwd")
    dW_all = _mm(z0, dproj, ta=True, name="mlstm_proj_dw")
    dW_min = dW_all[:, :m_w_in.shape[-1] * N_CHIPS]

    early_tags, early_gs = ["m_w_in"], [_cols_to_shards(dW_min)]
    early_part = add_halves(early_tags, early_gs, _swap_halves(early_gs, name="grad_swap_halves_early"))
    grad_x, g_mpre[0], *early_from = _mm(
        dproj, W_all, tb=True, rows=_rows_first_sublayer_start_bwd(h0, dh1, norm_mix_pre[0:1]),
        ride=(_Exchange, early_part),
        name="mlstm_proj_dx")
    tags = early_tags + late_tags
    halves = [_sum_chips(q, r, place_arr, name=f"grad_sum_chips_{tag}")
              for tag, q, r in zip(tags, early_part + late_part, list(early_from) + list(late_from))]
    reduced = dict(zip(tags, _join_halves(halves, name="grad_join_halves")))
    layer_grads = dict(m_w_in=[reduced["m_w_in"]], m_w_out=[reduced["m_w_out"]], a_w_in=[reduced["a_w_in"]],
                       a_w_out=[reduced["a_w_out"]], f_w_up=[reduced["f_w_up0"], reduced["f_w_up1"]],
                       f_w_down=[reduced["f_w_down0"], reduced["f_w_down1"]])

    small_g = [
        dgbias[:, :2 * M_HEADS].reshape(1, 2, M_HEADS),
        dhn,
        dsinks.reshape(1, A_QH),
        db_in, db_out,
        jnp.concatenate(g_mpre), jnp.concatenate(g_mpost), jnp.concatenate(g_fpre), jnp.concatenate(g_post),
        jnp.stack(dconv_w), jnp.concatenate(dconv_b),
    ]
    small_names = ["m_gate_bias", "m_head_norm", "a_sinks", "a_b_in", "a_b_out", "norm_mix_pre", "norm_mix_post",
                   "norm_ffn_pre", "norm_ffn_post", "f_conv_w", "f_conv_b"]
    red = _unpack_small(_allreduce_small(_pack_small(small_g), name="reduce_small"), [t.shape for t in small_g])
    for n, t in zip(small_names, red):
        if n in ("a_b_in", "a_b_out", "f_conv_w"):
            axis = t.ndim - 1
            width = params[n].shape[axis]
            t = lax.dynamic_slice_in_dim(t, chip * width, width, axis)
        grads[n] = t

    deltas, new_m, new_v = {}, {}, {}
    for n in _BIG:
        grads[n], deltas[n], new_m[n], new_v[n] = _adamw_layers(params[n], layer_grads[n], mom1[n], mom2[n],
                                                                name=f"adamw_{n}")
    two = lambda t: t.reshape(-1, t.shape[-1])
    res = _adamw_small([(two(params[n]), two(grads[n]), two(mom1[n]), two(mom2[n])) for n in small_names],
                       name="adamw_small")
    for n, (d, nm, nv) in zip(small_names, res):
        sh = params[n].shape
        deltas[n], new_m[n], new_v[n] = d.reshape(sh), nm.reshape(sh), nv.reshape(sh)

    return (loss, grad_x[None], *[grads[n] for n in order], *[deltas[n] for n in order],
            *[new_m[n] for n in order], *[new_v[n] for n in order])
```

```python
import functools
import math

import numpy as np

import jax
import jax.numpy as jnp
from jax import lax
from jax.experimental import pallas as pl
from jax.experimental.pallas import tpu as pltpu

F32 = jnp.float32
MXU_DTYPE = jnp.bfloat16
WIRE_DTYPE = jnp.bfloat16
MESH = pl.DeviceIdType.MESH

D_MODEL = 1024
EPS = 1e-6
M_HEADS, M_QK, M_V, M_CHUNK = 8, 64, 128, 128
GATE_CAP = 15.0
A_DH, A_QH, A_KVH, A_G, A_BLK = 64, 16, 2, 8, 128
D_FF = 2816
N_CHIPS = 4
LANES = 128
VMEM_LIMIT = 56 * 1024 * 1024

ADAM_LR, ADAM_B1, ADAM_B2, ADAM_EPS, ADAM_WD, ADAM_STEP = 0.001, 0.9, 0.999, 1e-08, 0.01, 10


def _cparams(sem):
    return pltpu.CompilerParams(dimension_semantics=sem, vmem_limit_bytes=VMEM_LIMIT)


def _pick(n, cands):
    for c in cands:
        if n % c == 0:
            return c
    return n


class _Rows:
    def __init__(self, tiles, vecs, outs, fn):
        self.tiles, self.vecs, self.outs, self.fn = list(tiles), list(vecs), list(outs), fn


MM_VMEM_BUDGET = 46 * 1024 * 1024
ROWS_FULL_K = 4224
WIDE_N = 3200


def _mm(a, b, *, ta=False, tb=False, out_dtype=F32, bias=None, add=None, b_layer=None, out_stacked=False, rows=None,
        ride=None, name):
    if ta:
        K, M = a.shape
    else:
        M, K = a.shape
    b_stacked = b_layer is not None
    if b_stacked:
        assert not ta
        n_sh = b.shape[2]
        w_rows, w_cols = D_MODEL, N_CHIPS * n_sh
        N, Kb = (w_rows, w_cols) if tb else (w_cols, w_rows)
    elif tb:
        N, Kb = b.shape
    else:
        Kb, N = b.shape
    assert K == Kb, (a.shape, b.shape)
    tn = _pick(N, (1024, 1408, 1280, 640, 512, 256, 128))
    tk = K if K <= 2816 else _pick(K, (2816, 2048, 1408, 1024, 512, 256, 128))
    if tn < 1024 <= N <= WIDE_N:
        tn, tk = N, min(tk, 1024)
    shards_per_step = 1
    if b_stacked and tb:
        shards_per_step = 2
        tk = shards_per_step * n_sh
    if (b_stacked and not tb) or out_stacked:
        tn = N // N_CHIPS
    if rows is not None:
        assert tn == N and not out_stacked and not ta
        if K <= ROWS_FULL_K:
            tk = K
    nk = K // tk

    def vmem_bytes(rows_per_tile):
        total = 2 * (rows_per_tile * tk * a.dtype.itemsize + tk * tn * b.dtype.itemsize)
        if rows is None:
            total += 2 * rows_per_tile * tn * jnp.dtype(out_dtype).itemsize
        else:
            total += 2 * rows_per_tile * N * (4 * len(rows.tiles)
                                              + sum(jnp.dtype(dt).itemsize for kind, dt in rows.outs if kind == "tile"))
        if add is not None:
            total += 2 * rows_per_tile * tn * 4
        return total + (rows_per_tile * tn * 4 if nk > 1 else 0)

    tm = next(c for c in (2048, 1024, 1408, 512, 256, 128, M) if M % c == 0 and (vmem_bytes(c) <= MM_VMEM_BUDGET or c <= 128))
    dn = (((0 if ta else 1,), (1 if tb else 0,)), ((), ()))
    has_bias, has_add = bias is not None, add is not None
    n_tiles, n_vecs, n_outs = (len(rows.tiles), len(rows.vecs), len(rows.outs)) if rows is not None else (0, 0, 1)
    rider, riding = ride if ride is not None else (None, ())
    n_ex = len(riding)
    assert not n_ex or rows is not None
    grid = (M // tm, N // tn, nk)

    def body(*refs):
        a_ref, b_ref = refs[0], refs[1]
        pos = 2
        bias_ref = add_ref = None
        if has_bias:
            bias_ref = refs[pos]
            pos += 1
        if has_add:
            add_ref = refs[pos]
            pos += 1
        tile_refs, vec_refs = refs[pos:pos + n_tiles], refs[pos + n_tiles:pos + n_tiles + n_vecs]
        pos += n_tiles + n_vecs
        ex_in = refs[pos:pos + n_ex]
        pos += n_ex
        out_refs, ex_out = refs[pos:pos + n_outs], refs[pos + n_outs:pos + n_outs + n_ex]
        pos += n_outs + n_ex
        acc_ref = refs[pos] if nk > 1 else None
        if n_ex:
            ex = rider(ex_in, ex_out, *refs[pos + (nk > 1):])
            step = (pl.program_id(0) * grid[1] + pl.program_id(1)) * grid[2] + pl.program_id(2)

            @pl.when(step == 0)
            def _():
                ex.issue()

        def finish(r):
            if has_bias:
                r = r + bias_ref[...]
            if has_add:
                r = r + add_ref[...]
            if rows is None:
                out_refs[0][...] = r.astype(out_dtype)
                return
            vals = rows.fn(r, [t[...] for t in tile_refs], [v[...] for v in vec_refs])
            first = pl.program_id(0) == 0
            for (kind, _), o_ref, val in zip(rows.outs, out_refs, vals):
                if kind == "tile":
                    o_ref[...] = val.astype(o_ref.dtype)
                else:
                    @pl.when(first)
                    def _(o_ref=o_ref, val=val):
                        o_ref[...] = val

                    @pl.when(jnp.logical_not(first))
                    def _(o_ref=o_ref, val=val):
                        o_ref[...] += val

        if shards_per_step > 1:
            part = sum(lax.dot_general(a_ref[:, s * n_sh:(s + 1) * n_sh], b_ref[s], dn, preferred_element_type=F32)
                       for s in range(shards_per_step))
        else:
            part = lax.dot_general(a_ref[...], b_ref[...], dn, preferred_element_type=F32)
        if nk == 1:
            finish(part)
        else:
            k = pl.program_id(2)

            @pl.when(k == 0)
            def _():
                acc_ref[...] = part

            @pl.when(k > 0)
            def _():
                acc_ref[...] += part

            @pl.when(k == nk - 1)
            def _():
                finish(acc_ref[...])

        if n_ex:
            @pl.when(step == grid[0] * grid[1] * grid[2] - 1)
            def _():
                ex.finish()

    a_spec = pl.BlockSpec((tk, tm), lambda i, j, k: (k, i)) if ta else pl.BlockSpec((tm, tk), lambda i, j, k: (i, k))
    if b_stacked and tb:
        off = b_layer * (w_rows // tn)
        b_spec = pl.BlockSpec((shards_per_step, tn, n_sh), lambda i, j, k: (k, off + j, 0))
    elif b_stacked:
        off = b_layer * (w_rows // tk)
        b_spec = pl.BlockSpec((None, tk, tn), lambda i, j, k: (j, off + k, 0))
    elif tb:
        b_spec = pl.BlockSpec((tn, tk), lambda i, j, k: (j, k))
    else:
        b_spec = pl.BlockSpec((tk, tn), lambda i, j, k: (k, j))
    if out_stacked:
        out_spec = pl.BlockSpec((None, tm, tn), lambda i, j, k: (j, i, 0))
        out_shape = jax.ShapeDtypeStruct((N_CHIPS, M, tn), out_dtype)
    else:
        out_spec = pl.BlockSpec((tm, tn), lambda i, j, k: (i, j))
        out_shape = jax.ShapeDtypeStruct((M, N), out_dtype)
    in_specs, args = [a_spec, b_spec], [a, b]
    if has_bias:
        in_specs.append(pl.BlockSpec((1, tn), lambda i, j, k: (0, j)))
        args.append(bias)
    if has_add:
        in_specs.append(pl.BlockSpec((tm, tn), lambda i, j, k: (i, j)))
        args.append(add)
    semantics = ("parallel", "parallel", "arbitrary")
    if rows is not None:
        tile_spec = pl.BlockSpec((tm, N), lambda i, j, k: (i, 0))
        vec_spec = pl.BlockSpec((1, N), lambda i, j, k: (0, 0))
        in_specs += [tile_spec] * n_tiles + [vec_spec] * n_vecs
        args += rows.tiles + rows.vecs
        out_spec = [tile_spec if kind == "tile" else vec_spec for kind, _ in rows.outs]
        out_shape = [jax.ShapeDtypeStruct((M, N) if kind == "tile" else (1, N), dt) for kind, dt in rows.outs]
        semantics = ("arbitrary", "arbitrary", "arbitrary")
    scratch = [pltpu.VMEM((tm, tn), F32)] if nk > 1 else []
    if n_ex:
        any_spec = pl.BlockSpec(memory_space=pl.ANY)
        in_specs += [any_spec] * n_ex
        args += list(riding)
        out_spec = out_spec + [any_spec] * n_ex
        out_shape = out_shape + rider.out_shape(riding)
        scratch += rider.semaphores(n_ex)
    return pl.pallas_call(
        body, name=name,
        grid=(M // tm, N // tn, nk),
        in_specs=in_specs,
        out_specs=out_spec,
        out_shape=out_shape,
        scratch_shapes=scratch,
        compiler_params=_cparams(semantics),
    )(*args)


ROW_TILE = 512


def _rms_fwd(x, g, shards, *, out_dtype, name):
    S, D = x.shape
    tm = _pick(S, (ROW_TILE, 256, 128))
    nsh, nt = len(shards), S // tm
    any_spec = pl.BlockSpec(memory_space=pl.ANY)

    def body(x_ref, g_ref, *rest):
        o_ref = rest[nsh]
        gather = _Gather(rest[:nsh], rest[nsh + 1:2 * nsh + 1], *rest[2 * nsh + 1:])

        @pl.when(pl.program_id(0) == 0)
        def _():
            gather.issue()

        o_ref[...] = _rms(x_ref[...], g_ref[...]).astype(out_dtype)

        @pl.when(pl.program_id(0) == nt - 1)
        def _():
            gather.finish()

    row = pl.BlockSpec((tm, D), lambda i: (i, 0))
    return pl.pallas_call(
        body, name=name, grid=(nt,),
        in_specs=[row, pl.BlockSpec((1, D), lambda i: (0, 0))] + [any_spec] * nsh, out_specs=[row] + [any_spec] * nsh,
        out_shape=[jax.ShapeDtypeStruct((S, D), out_dtype)] + _Gather.out_shape(shards),
        scratch_shapes=_Gather.semaphores(nsh),
        compiler_params=_cparams(("arbitrary",)),
    )(x, g, *shards)


def _rms(x, g):
    return x * lax.rsqrt(jnp.mean(x * x, axis=-1, keepdims=True) + EPS) * g


def _rms_vjp(x, g, dy):
    r = lax.rsqrt(jnp.mean(x * x, axis=-1, keepdims=True) + EPS)
    xh = x * r
    gd = dy * g
    dx = r * (gd - xh * jnp.mean(gd * xh, axis=-1, keepdims=True))
    return dx, jnp.sum(dy * xh, axis=0, keepdims=True)


def _rows_sublayer_end(res, g_post, g_pre_next):
    def fn(z, tiles, vecs):
        h = tiles[0] + _rms(z, vecs[0])
        return [z, h, _rms(h, vecs[1])]

    return _Rows([res], [g_post, g_pre_next], [("tile", F32), ("tile", F32), ("tile", MXU_DTYPE)], fn)


def _rows_last_sublayer_end(res, g_post, target):
    def fn(z, tiles, vecs):
        y = tiles[0] + _rms(z, vecs[0])
        err = y - tiles[1]
        dy = err * (1.0 / err.shape[-1])
        dz, dg = _rms_vjp(z, vecs[0], dy)
        loss = 0.5 * jnp.sum(jnp.mean(err * err, axis=-1, keepdims=True), axis=0, keepdims=True)
        return [z, dy, dz, dg, jnp.broadcast_to(loss, dg.shape)]

    return _Rows([res, target], [g_post], [("tile", F32), ("tile", F32), ("tile", MXU_DTYPE), ("vec", F32), ("vec", F32)], fn)


def _rows_sublayer_start_bwd(x, res, z_below, g_pre, g_post_below):
    def fn(dz, tiles, vecs):
        dx, dg_pre = _rms_vjp(tiles[0], vecs[0], dz)
        dh = tiles[1] + dx
        dzb, dg_post = _rms_vjp(tiles[2], vecs[1], dh)
        return [dh, dg_pre, dzb, dg_post, jnp.sum(dzb, axis=0, keepdims=True)]

    return _Rows([x, res, z_below], [g_pre, g_post_below],
                 [("tile", F32), ("vec", F32), ("tile", MXU_DTYPE), ("vec", F32), ("vec", F32)], fn)


def _rows_first_sublayer_start_bwd(x, res, g_pre):
    def fn(dz, tiles, vecs):
        dx, dg_pre = _rms_vjp(tiles[0], vecs[0], dz)
        return [tiles[1] + dx, dg_pre]

    return _Rows([x, res], [g_pre], [("tile", F32), ("vec", F32)], fn)


def _mx(t):
    return t.astype(MXU_DTYPE)


def _mxf(t):
    return t.astype(MXU_DTYPE).astype(F32)


def _rawdot(a, b, ca, cb):
    return lax.dot_general(_mx(a), _mx(b), (((ca,), (cb,)), ((), ())), preferred_element_type=F32)


@functools.partial(jax.custom_vjp, nondiff_argnums=(2, 3))
def _dot(a, b, ca, cb):
    return _rawdot(a, b, ca, cb)


def _dot_fwd(a, b, ca, cb):
    return _rawdot(a, b, ca, cb), (a, b)


def _dot_bwd(ca, cb, res, g):
    a, b = res
    bj = 1 if cb == 0 else 0
    ai = 0 if ca == 1 else 1
    da = _rawdot(g, b, 1, bj) if ca == 1 else _rawdot(b, g, bj, 1)
    db = _rawdot(a, g, ai, 0) if cb == 0 else _rawdot(g, a, 0, ai)
    return da.astype(a.dtype), db.astype(b.dtype)


_dot.defvjp(_dot_fwd, _dot_bwd)


def _softcap(z):
    return GATE_CAP * jnp.tanh(z / GATE_CAP)


def _log_sigmoid(z):
    return jnp.minimum(z, 0.0) - jnp.log(1.0 + jnp.exp(-jnp.abs(z)))


def _sigmoid(z):
    return 0.5 * jnp.tanh(0.5 * z) + 0.5


def _lane_col(t, lane_index):
    lane = lax.broadcasted_iota(jnp.int32, t.shape, 1)
    return jnp.sum(jnp.where(lane == lane_index, t, 0.0), axis=1, keepdims=True)


def _mlstm_gates(G, bias):
    L = G.shape[0]
    z = _softcap(G + bias)
    ig = z
    lf = _log_sigmoid(z)
    ti = lax.broadcasted_iota(jnp.int32, (L, L), 0)
    si = lax.broadcasted_iota(jnp.int32, (L, L), 1)
    tril = (si <= ti).astype(F32)
    b = lax.dot_general(tril, lf, (((1,), (0,)), ((), ())), precision=lax.Precision.HIGHEST, preferred_element_type=F32)
    bL = jnp.sum(lf, axis=0, keepdims=True)
    return ig, b, ig.T, b.T, bL


def _mlstm_head(h, q, k_pair, v, o, ig_all, b_all, igT, bT, bL_all, hn, C, n, m):
    L = q.shape[0]
    k = jnp.where(_own_lanes(h, k_pair.shape), k_pair, 0.0)
    ti = lax.broadcasted_iota(jnp.int32, (L, L), 0)
    si = lax.broadcasted_iota(jnp.int32, (L, L), 1)
    lower = (si <= ti)
    ig = _lane_col(ig_all, h)
    b = _lane_col(b_all, M_HEADS + h)
    ig_row = igT[h:h + 1, :]
    b_row = bT[M_HEADS + h:M_HEADS + h + 1, :]
    bL = _lane_col(bL_all, M_HEADS + h)
    inter = b + m
    dlog = jnp.where(lower, b - b_row + ig_row, -jnp.inf)
    m_t = lax.stop_gradient(jnp.maximum(inter, jnp.max(dlog, axis=-1, keepdims=True)))
    qs = q * (M_QK ** -0.5)
    w = _dot(qs, k, 1, 1) * jnp.exp(dlog - m_t)
    s_inter = jnp.exp(inter - m_t)
    num = _dot(w, v, 1, 0) + s_inter * _dot(qs, C, 1, 0)
    den = jnp.sum(w, axis=-1, keepdims=True) + s_inter * jnp.sum(_mxf(qs) * _mxf(n), axis=-1, keepdims=True)
    hout = num * (1.0 / jnp.maximum(jnp.abs(den), jnp.exp(-m_t)))
    tail = bL - b + ig
    m_new = lax.stop_gradient(jnp.maximum(bL + m, jnp.max(tail, axis=0, keepdims=True)))
    ws = jnp.exp(tail - m_new)
    decay = jnp.exp(bL + m - m_new)
    wk = ws * k
    C_new = decay * C + _dot(wk, v, 0, 0)
    n_new = decay * n + jnp.sum(_mxf(ws) * _mxf(k), axis=0, keepdims=True)
    hs = hout * lax.rsqrt(jnp.mean(hout * hout, axis=-1, keepdims=True) + EPS) * hn
    gated = _sigmoid(o) * hs
    return (gated, C_new, n_new), m_new


M_OFF_Q, M_OFF_K = 0, M_HEADS * M_QK
M_OFF_V = 2 * M_HEADS * M_QK
M_OFF_O = M_OFF_V + M_HEADS * M_V
M_OFF_G = M_OFF_O + M_HEADS * M_V
M_PROJ = M_OFF_G + LANES
M_PAIRS = M_HEADS * M_QK // LANES


def _head_cols(off, h):
    return slice(off + h * LANES, off + (h + 1) * LANES)


def _own_lanes(h, shape):
    low = lax.broadcasted_iota(jnp.int32, shape, 1) < M_QK
    return low if h % 2 == 0 else jnp.logical_not(low)


def _mlstm_specs(NC, rev):
    H, L = M_HEADS, M_CHUNK
    cc = (lambda c: NC - 1 - c) if rev else (lambda c: c)
    proj = pl.BlockSpec((L, M_PROJ), lambda c: (cc(c), 0))
    vec = pl.BlockSpec((1, LANES), lambda c: (0, 0))
    hn = pl.BlockSpec((1, H * M_V), lambda c: (0, 0))
    hv = pl.BlockSpec((L, H * M_V), lambda c: (cc(c), 0))
    Cs = pl.BlockSpec((None, H, LANES, M_V), lambda c: (cc(c), 0, 0, 0))
    ns = pl.BlockSpec((None, H, 1, LANES), lambda c: (cc(c), 0, 0, 0))
    ms = pl.BlockSpec((None, H, 1, 1), lambda c: (cc(c), 0, 0, 0))
    return proj, vec, hn, hv, Cs, ns, ms


_MLSTM_STATE = [pltpu.VMEM((M_HEADS, LANES, M_V), F32), pltpu.VMEM((M_HEADS, 1, LANES), F32),
                pltpu.VMEM((M_HEADS, 1, 1), F32)]


def _mlstm_fwd(proj, bias, hn, shards, *, name):
    S = proj.shape[0]
    H, NC = M_HEADS, S // M_CHUNK
    ps, vec, hns, hv, Cs, ns, ms = _mlstm_specs(NC, False)
    nsh = len(shards)
    any_spec = pl.BlockSpec(memory_space=pl.ANY)

    def body(p_ref, b_ref, hn_ref, *rest):
        gated_ref, C_all, n_all, m_all = rest[nsh:nsh + 4]
        C_s, n_s, m_s, send_sems, recv_sems = rest[2 * nsh + 4:]
        gather = _Gather(rest[:nsh], rest[nsh + 4:2 * nsh + 4], send_sems, recv_sems)

        @pl.when(pl.program_id(0) == 0)
        def _():
            C_s[...] = jnp.zeros_like(C_s)
            n_s[...] = jnp.zeros_like(n_s)
            m_s[...] = jnp.zeros_like(m_s)
            gather.issue()

        gate_terms = _mlstm_gates(p_ref[:, M_OFF_G:M_OFF_G + LANES], b_ref[...])
        for h in range(H):
            C, n, m = C_s[h], n_s[h], m_s[h]
            C_all[h] = C
            n_all[h] = n
            m_all[h] = m
            (gated, Cn, nn), mn = _mlstm_head(
                h, p_ref[:, _head_cols(M_OFF_Q, h // 2)], p_ref[:, _head_cols(M_OFF_K, h // 2)],
                p_ref[:, _head_cols(M_OFF_V, h)], p_ref[:, _head_cols(M_OFF_O, h)], *gate_terms,
                hn_ref[:, _head_cols(0, h)], C, n, m)
            gated_ref[:, _head_cols(0, h)] = gated.astype(gated_ref.dtype)
            C_s[h] = Cn
            n_s[h] = nn
            m_s[h] = mn

        @pl.when(pl.program_id(0) == NC - 1)
        def _():
            gather.finish()

    return pl.pallas_call(
        body, name=name, grid=(NC,),
        in_specs=[ps, vec, hns] + [any_spec] * nsh,
        out_specs=[hv, Cs, ns, ms] + [any_spec] * nsh,
        out_shape=[jax.ShapeDtypeStruct((S, H * M_V), MXU_DTYPE),
                   jax.ShapeDtypeStruct((NC, H, LANES, M_V), F32),
                   jax.ShapeDtypeStruct((NC, H, 1, LANES), F32),
                   jax.ShapeDtypeStruct((NC, H, 1, 1), F32)] + _Gather.out_shape(shards),
        scratch_shapes=list(_MLSTM_STATE) + _Gather.semaphores(nsh),
        compiler_params=_cparams(("arbitrary",)),
    )(proj, bias, hn, *shards)


def _mlstm_bwd(proj, bias, hn, C_all, n_all, m_all, dgated, qs, *, name):
    S = proj.shape[0]
    H, NC = M_HEADS, S // M_CHUNK
    ps, vec, hns, hv, Cs, ns, ms = _mlstm_specs(NC, True)
    nq = len(qs)
    any_spec = pl.BlockSpec(memory_space=pl.ANY)

    def body(p_ref, b_ref, hn_ref, C_ref, n_ref, m_ref, dg_ref, *rest):
        dp_ref, db_ref, dhn_ref = rest[nq:nq + 3]
        dC_s, dn_s, send_sems, recv_sems = rest[2 * nq + 3:]
        exchange = _Exchange(rest[:nq], rest[nq + 3:2 * nq + 3], send_sems, recv_sems)

        @pl.when(pl.program_id(0) == 0)
        def _():
            dC_s[...] = jnp.zeros_like(dC_s)
            dn_s[...] = jnp.zeros_like(dn_s)
            db_ref[...] = jnp.zeros_like(db_ref)
            dhn_ref[...] = jnp.zeros_like(dhn_ref)
            exchange.issue()

        gate_terms, gates_vjp = jax.vjp(_mlstm_gates, p_ref[:, M_OFF_G:M_OFF_G + LANES], b_ref[...])
        d_terms = [jnp.zeros_like(t) for t in gate_terms]
        for h in range(H):
            def head(q, k, v, o, *rest, h=h):
                return _mlstm_head(h, q, k, v, o, *rest, m_ref[h])

            prim = (p_ref[:, _head_cols(M_OFF_Q, h // 2)], p_ref[:, _head_cols(M_OFF_K, h // 2)],
                    p_ref[:, _head_cols(M_OFF_V, h)], p_ref[:, _head_cols(M_OFF_O, h)], *gate_terms,
                    hn_ref[:, _head_cols(0, h)], C_ref[h], n_ref[h])
            _, vjp, _ = jax.vjp(head, *prim, has_aux=True)
            dq, dk, dv, do, *d_gate, dhnh, dC, dn = vjp((dg_ref[:, _head_cols(0, h)].astype(F32), dC_s[h], dn_s[h]))
            if h % 2 == 0:
                dq_pair, dk_pair = dq, dk
            else:
                dp_ref[:, _head_cols(M_OFF_Q, h // 2)] = (dq_pair + dq).astype(dp_ref.dtype)
                dp_ref[:, _head_cols(M_OFF_K, h // 2)] = (dk_pair + dk).astype(dp_ref.dtype)
            dp_ref[:, _head_cols(M_OFF_V, h)] = dv.astype(dp_ref.dtype)
            dp_ref[:, _head_cols(M_OFF_O, h)] = do.astype(dp_ref.dtype)
            d_terms = [a + g for a, g in zip(d_terms, d_gate)]
            dhn_ref[:, _head_cols(0, h)] += dhnh
            dC_s[h] = dC
            dn_s[h] = dn
        dG, dbias = gates_vjp(tuple(d_terms))
        dp_ref[:, M_OFF_G:M_OFF_G + LANES] = dG.astype(dp_ref.dtype)
        db_ref[...] += dbias

        @pl.when(pl.program_id(0) == NC - 1)
        def _():
            exchange.finish()

    return pl.pallas_call(
        body, name=name, grid=(NC,),
        in_specs=[ps, vec, hns, Cs, ns, ms, hv] + [any_spec] * nq,
        out_specs=[ps, vec, hns] + [any_spec] * nq,
        out_shape=[jax.ShapeDtypeStruct((S, M_PROJ), MXU_DTYPE), jax.ShapeDtypeStruct((1, LANES), F32),
                   jax.ShapeDtypeStruct((1, H * M_V), F32)] + _Exchange.out_shape(qs),
        scratch_shapes=list(_MLSTM_STATE[:2]) + _Exchange.semaphores(nq),
        compiler_params=_cparams(("arbitrary",)),
    )(proj, bias, hn, C_all, n_all, m_all, dgated, *qs)


A_NQ = A_QH * A_DH
A_NKV = 2 * A_KVH * A_DH
A_PAIRS = A_G // 2


def _sink_softmax_parts(s, sink):
    mx = jnp.maximum(jnp.max(s, axis=-1, keepdims=True), sink)
    e = jnp.exp(s - mx)
    e_sink = jnp.exp(sink - mx)
    r = 1.0 / (jnp.sum(e, axis=-1, keepdims=True) + e_sink)
    return e * r, e_sink * r


@jax.custom_vjp
def _sink_softmax(s, sink):
    return _sink_softmax_parts(s, sink)[0]


def _sink_softmax_fwd(s, sink):
    p, p_sink = _sink_softmax_parts(s, sink)
    return p, (p, p_sink)


def _sink_softmax_bwd(res, dp):
    p, p_sink = res
    t = jnp.sum(dp * p, axis=-1, keepdims=True)
    return p * (dp - t), -p_sink * t


_sink_softmax.defvjp(_sink_softmax_fwd, _sink_softmax_bwd)


def _band_bias():
    qi = np.arange(A_G * A_BLK)[:, None] % A_BLK
    ku = np.arange(2 * A_BLK)[None, :]
    diff = qi - (ku - A_BLK)
    band = (diff >= 0) & (diff < A_BLK)
    seen = np.stack([band, band & (ku >= A_BLK)])
    return jnp.asarray(np.where(seen, 0.0, -np.inf), F32)


_BAND_BIAS_SPEC = pl.BlockSpec((None, A_G * A_BLK, 2 * A_BLK), lambda n: (jnp.where(n == 0, 1, 0), 0, 0))


def _attn_group(Ps, KLO, KHI, VLO, VHI, sinks, bias):
    B = Ps[0].shape[0]
    R = len(Ps) * B
    q2 = jnp.concatenate(Ps, axis=0) * (A_DH ** -0.5)
    s = jnp.concatenate([_dot(q2, KLO, 1, 1), _dot(q2, KHI, 1, 1)], axis=0) + bias
    ri = lax.broadcasted_iota(jnp.int32, (2 * R, A_G), 0)
    head = 2 * lax.shift_right_logical(ri & (R - 1), B.bit_length() - 1) + lax.shift_right_logical(ri, R.bit_length() - 1)
    onehot = head == lax.broadcasted_iota(jnp.int32, (2 * R, A_G), 1)
    sink = jnp.sum(jnp.where(onehot, sinks, 0.0), axis=1, keepdims=True)
    p = _sink_softmax(s, sink)
    o = _dot(p[:R], VLO, 1, 0) + _dot(p[R:], VHI, 1, 0)
    return tuple(o[j * B:(j + 1) * B] for j in range(len(Ps)))


def _swap_halves_of_lanes(t):
    return pltpu.roll(t, LANES // 2, 1)


def _kv_operands(kvp_ref, kvc_ref, h):
    kk = jnp.concatenate([kvp_ref[:, :LANES], kvc_ref[:, :LANES]], axis=0)
    vv = jnp.concatenate([kvp_ref[:, LANES:], kvc_ref[:, LANES:]], axis=0)
    low = lax.broadcasted_iota(jnp.int32, kk.shape, 1) < A_DH
    own = low if h == 0 else jnp.logical_not(low)
    k_own = jnp.where(own, kk, 0.0)
    v_own = jnp.where(own, vv, 0.0)
    k_oth, v_oth = _swap_halves_of_lanes(k_own), _swap_halves_of_lanes(v_own)
    if h == 0:
        return own, k_own, k_oth, v_own, v_oth
    return own, k_oth, k_own, v_oth, v_own


def _pair_cols(h, j):
    c = (h * A_PAIRS + j) * LANES
    return slice(c, c + LANES)


def _attn_fwd(proj, sinks, *, name):
    S = proj.shape[0]
    NB = S // A_BLK
    kv_blk = A_NQ // A_NKV
    qs = pl.BlockSpec((A_BLK, A_NQ), lambda n: (n, 0))
    cur = pl.BlockSpec((A_BLK, A_NKV), lambda n: (n, kv_blk))
    prev = pl.BlockSpec((A_BLK, A_NKV), lambda n: (jnp.maximum(n - 1, 0), kv_blk))
    sk = pl.BlockSpec((A_KVH, A_G), lambda n: (0, 0))

    def body(q_ref, kvp_ref, kvc_ref, s_ref, bias_ref, o_ref):
        for h in range(A_KVH):
            _, KLO, KHI, VLO, VHI = _kv_operands(kvp_ref, kvc_ref, h)
            Ps = tuple(q_ref[:, _pair_cols(h, j)] for j in range(A_PAIRS))
            outs = _attn_group(Ps, KLO, KHI, VLO, VHI, s_ref[h:h + 1, :], bias_ref[...])
            for j in range(A_PAIRS):
                o_ref[:, _pair_cols(h, j)] = outs[j].astype(o_ref.dtype)

    return pl.pallas_call(
        body, name=name, grid=(NB,),
        in_specs=[qs, prev, cur, sk, _BAND_BIAS_SPEC], out_specs=qs,
        out_shape=jax.ShapeDtypeStruct((S, A_NQ), MXU_DTYPE),
        compiler_params=_cparams(("parallel",)),
    )(proj, proj, proj, sinks, _band_bias())


def _attn_bwd(proj, sinks, do, *, name):
    S = proj.shape[0]
    NB = S // A_BLK
    last = NB - 1
    kv_blk = A_NQ // A_NKV
    qs = pl.BlockSpec((A_BLK, A_NQ), lambda n: (jnp.minimum(n, last), 0))
    cur = pl.BlockSpec((A_BLK, A_NKV), lambda n: (jnp.minimum(n, last), kv_blk))
    prev = pl.BlockSpec((A_BLK, A_NKV), lambda n: (jnp.clip(n - 1, 0, last), kv_blk))
    sk = pl.BlockSpec((A_KVH, A_G), lambda n: (0, 0))
    lag = pl.BlockSpec((A_BLK, A_NKV), lambda n: (jnp.maximum(n - 1, 0), 0))
    cq_spec = pl.BlockSpec((1, A_NQ), lambda n: (0, 0))
    ckv_spec = pl.BlockSpec((1, A_NKV), lambda n: (0, 0))

    def body(q_ref, kvp_ref, kvc_ref, s_ref, do_ref, bias_ref, dq_ref, dkv_ref, ds_ref, cq_ref, ckv_ref, keep):
        n = pl.program_id(0)

        @pl.when(n == 0)
        def _():
            keep[...] = jnp.zeros_like(keep)
            ds_ref[...] = jnp.zeros_like(ds_ref)
            cq_ref[...] = jnp.zeros_like(cq_ref)
            ckv_ref[...] = jnp.zeros_like(ckv_ref)

        @pl.when(n < NB)
        def _():
            f = functools.partial(_attn_group, bias=bias_ref[...])
            dk = jnp.zeros((2 * A_BLK, LANES), F32)
            dv = jnp.zeros((2 * A_BLK, LANES), F32)
            ds_rows = []
            for h in range(A_KVH):
                own, KLO, KHI, VLO, VHI = _kv_operands(kvp_ref, kvc_ref, h)
                Ps = tuple(q_ref[:, _pair_cols(h, j)] for j in range(A_PAIRS))
                _, vjp = jax.vjp(f, Ps, KLO, KHI, VLO, VHI, s_ref[h:h + 1, :])
                dPs, dKLO, dKHI, dVLO, dVHI, dsk = vjp(
                    tuple(do_ref[:, _pair_cols(h, j)].astype(F32) for j in range(A_PAIRS)))
                for j in range(A_PAIRS):
                    dq_ref[:, _pair_cols(h, j)] = dPs[j].astype(dq_ref.dtype)
                    cq_ref[:, _pair_cols(h, j)] += jnp.sum(dPs[j], axis=0, keepdims=True)
                dk_own, dk_oth = (dKLO, dKHI) if h == 0 else (dKHI, dKLO)
                dv_own, dv_oth = (dVLO, dVHI) if h == 0 else (dVHI, dVLO)
                dk = dk + jnp.where(own, dk_own, 0.0) + _swap_halves_of_lanes(jnp.where(own, 0.0, dk_oth))
                dv = dv + jnp.where(own, dv_own, 0.0) + _swap_halves_of_lanes(jnp.where(own, 0.0, dv_oth))
                ds_rows.append(dsk)
            ds_ref[...] += jnp.concatenate(ds_rows, axis=0)
            dkv = jnp.concatenate([dk, dv], axis=1)
            done = keep[...] + dkv[:A_BLK]
            dkv_ref[...] = done.astype(dkv_ref.dtype)
            ckv_ref[...] += jnp.sum(done, axis=0, keepdims=True)
            keep[...] = dkv[A_BLK:]

        @pl.when(n == NB)
        def _():
            done = keep[...]
            dkv_ref[...] = done.astype(dkv_ref.dtype)
            ckv_ref[...] += jnp.sum(done, axis=0, keepdims=True)

    return pl.pallas_call(
        body, name=name, grid=(NB + 1,),
        in_specs=[qs, prev, cur, sk, qs, _BAND_BIAS_SPEC],
        out_specs=[qs, lag, sk, cq_spec, ckv_spec],
        out_shape=[jax.ShapeDtypeStruct((S, A_NQ), MXU_DTYPE), jax.ShapeDtypeStruct((S, A_NKV), MXU_DTYPE),
                   jax.ShapeDtypeStruct((A_KVH, A_G), F32), jax.ShapeDtypeStruct((1, A_NQ), F32),
                   jax.ShapeDtypeStruct((1, A_NKV), F32)],
        scratch_shapes=[pltpu.VMEM((A_BLK, A_NKV), F32)],
        compiler_params=_cparams(("arbitrary",)),
    )(proj, proj, proj, sinks, do, _band_bias())


HALO = 8


def _shift_rows(t, j):
    return pltpu.roll(t, j % t.shape[0], 0)


def _conv_taps(gate_ext):
    return _shift_rows(gate_ext, 2), _shift_rows(gate_ext, 1), gate_ext


def _conv_gate(g2, g1, g0, cw, cb):
    return cb + cw[0:1, :] * g2 + cw[1:2, :] * g1 + cw[2:3, :] * g0


def _convgate_fwd(u, cw, cb, *, name):
    S, F2 = u.shape
    F = F2 // 2
    tm = _pick(S, (512, 256, 128))
    hb = tm // HALO
    urow = pl.BlockSpec((tm, F2), lambda i: (i, 0))
    uprev = pl.BlockSpec((HALO, F), lambda i: (jnp.maximum(i * hb - 1, 0), 0))

    def body(u_ref, up_ref, cw_ref, cb_ref, a_ref):
        i = pl.program_id(0)
        prev = jnp.where(i > 0, up_ref[...], 0.0)
        gc = _conv_gate(*_conv_taps(jnp.concatenate([prev, u_ref[:, :F]], axis=0)), cw_ref[...], cb_ref[...])[HALO:]
        a_ref[...] = (gc * _sigmoid(gc) * u_ref[:, F:]).astype(a_ref.dtype)

    return pl.pallas_call(
        body, name=name, grid=(S // tm,),
        in_specs=[urow, uprev, pl.BlockSpec((3, F), lambda i: (0, 0)), pl.BlockSpec((1, F), lambda i: (0, 0))],
        out_specs=pl.BlockSpec((tm, F), lambda i: (i, 0)),
        out_shape=jax.ShapeDtypeStruct((S, F), MXU_DTYPE),
        compiler_params=_cparams(("parallel",)),
    )(u, u, cw, cb)


def _convgate_bwd(u, da, cw, cb, *, name):
    S, F2 = u.shape
    F = F2 // 2
    tm = _pick(S, (256, 128))
    hb = tm // HALO
    nt = S // tm
    nh = S // HALO
    urow = pl.BlockSpec((tm, F2), lambda i: (i, 0))
    uprev = pl.BlockSpec((HALO, F), lambda i: (jnp.maximum(i * hb - 1, 0), 0))
    unext = pl.BlockSpec((HALO, F2), lambda i: (jnp.minimum((i + 1) * hb, nh - 1), 0))
    darow = pl.BlockSpec((tm, F), lambda i: (i, 0))
    danext = pl.BlockSpec((HALO, F), lambda i: (jnp.minimum((i + 1) * hb, nh - 1), 0))

    def body(u_ref, up_ref, un_ref, da_ref, dan_ref, cw_ref, cb_ref, du_ref, dcw_ref, dcb_ref):
        i = pl.program_id(0)
        cwv = cw_ref[...]
        prev = jnp.where(i > 0, up_ref[...], 0.0)
        gate_ext = jnp.concatenate([prev, u_ref[:, :F], un_ref[:, :F]], axis=0)
        val_ext = jnp.concatenate([u_ref[:, F:], un_ref[:, F:]], axis=0)
        da_next = jnp.where(i < nt - 1, dan_ref[...].astype(F32), 0.0)
        da_ext = jnp.concatenate([da_ref[...].astype(F32), da_next], axis=0)
        g2, g1, g0 = (t[HALO:] for t in _conv_taps(gate_ext))
        gc = _conv_gate(g2, g1, g0, cwv, cb_ref[...])
        sg = _sigmoid(gc)
        silu = gc * sg
        dval = da_ext * silu
        dgc = da_ext * val_ext * (sg * (1.0 + gc * (1.0 - sg)))
        dgate = cwv[2:3, :] * dgc + cwv[1:2, :] * _shift_rows(dgc, -1) + cwv[0:1, :] * _shift_rows(dgc, -2)
        du_ref[:, :F] = dgate[:tm].astype(du_ref.dtype)
        du_ref[:, F:] = dval[:tm].astype(du_ref.dtype)
        dgc_c = dgc[:tm]
        dcw = jnp.concatenate([jnp.sum(dgc_c * g2[:tm], axis=0, keepdims=True),
                               jnp.sum(dgc_c * g1[:tm], axis=0, keepdims=True),
                               jnp.sum(dgc_c * g0[:tm], axis=0, keepdims=True)], axis=0)
        dcb = jnp.sum(dgc_c, axis=0, keepdims=True)

        @pl.when(i == 0)
        def _():
            dcw_ref[...] = dcw
            dcb_ref[...] = dcb

        @pl.when(i > 0)
        def _():
            dcw_ref[...] += dcw
            dcb_ref[...] += dcb

    return pl.pallas_call(
        body, name=name, grid=(nt,),
        in_specs=[urow, uprev, unext, darow, danext,
                  pl.BlockSpec((3, F), lambda i: (0, 0)), pl.BlockSpec((1, F), lambda i: (0, 0))],
        out_specs=[urow, pl.BlockSpec((3, F), lambda i: (0, 0)), pl.BlockSpec((1, F), lambda i: (0, 0))],
        out_shape=[jax.ShapeDtypeStruct((S, F2), MXU_DTYPE), jax.ShapeDtypeStruct((3, F), F32),
                   jax.ShapeDtypeStruct((1, F), F32)],
        compiler_params=_cparams(("arbitrary",)),
    )(u, u, u, da, da, cw, cb)


def _adamw_math(w, g, m, v):
    m = ADAM_B1 * m + (1.0 - ADAM_B1) * g
    v = ADAM_B2 * v + (1.0 - ADAM_B2) * (g * g)
    m_hat = m / (1.0 - ADAM_B1 ** ADAM_STEP)
    v_hat = v / (1.0 - ADAM_B2 ** ADAM_STEP)
    delta = -ADAM_LR * (m_hat / (jnp.sqrt(v_hat) + ADAM_EPS) + ADAM_WD * w)
    return delta, m, v


ADAMW_BLOCK_BYTES = 2 * 1024 * 1024


def _adamw_layers(w, gs, m, v, *, name):
    Lr, R, C = w.shape
    if C % LANES:
        flip = lambda t: jnp.swapaxes(t, -1, -2)
        return [flip(o) for o in _adamw_layers(flip(w), [flip(g) for g in gs], flip(m), flip(v), name=name)]
    tr = _pick(R, (256, 128, 64, 32, 16, 8))
    tc = C if tr * C * 4 <= ADAMW_BLOCK_BYTES else _pick(C, (256, 128))
    outs = None
    for layer, g in enumerate(gs):
        def body(w_ref, g_ref, m_ref, v_ref, *rest):
            go_ref, d_ref, nm_ref, nv_ref = rest[-4:]
            gv = g_ref[...]
            d, nm, nv = _adamw_math(w_ref[...], gv, m_ref[...], v_ref[...])
            go_ref[...] = gv
            d_ref[...] = d
            nm_ref[...] = nm
            nv_ref[...] = nv

        lay = pl.BlockSpec((None, tr, tc), lambda i, j, layer=layer: (layer, i, j))
        in_specs = [lay, pl.BlockSpec((tr, tc), lambda i, j: (i, j)), lay, lay]
        args = [w, g, m, v]
        aliases = {}
        if outs is not None:
            in_specs += [pl.BlockSpec(memory_space=pl.ANY)] * 4
            args += list(outs)
            aliases = {4 + t: t for t in range(4)}
        outs = pl.pallas_call(
            body, name=f"{name}_{layer}", grid=(R // tr, C // tc), in_specs=in_specs, out_specs=[lay] * 4,
            out_shape=[jax.ShapeDtypeStruct((Lr, R, C), F32)] * 4, input_output_aliases=aliases,
            compiler_params=_cparams(("parallel", "parallel")),
        )(*args)
    return outs


def _adamw_small(items, *, name):
    n = len(items)

    def body(*refs):
        ins, outs = refs[:4 * n], refs[4 * n:]
        for t in range(n):
            w, g, m, v = (r[...] for r in ins[4 * t:4 * t + 4])
            d, nm, nv = _adamw_math(w, g, m, v)
            outs[3 * t][...] = d
            outs[3 * t + 1][...] = nm
            outs[3 * t + 2][...] = nv

    flat = [a for it in items for a in it]
    out_shape = [jax.ShapeDtypeStruct(it[0].shape, F32) for it in items for _ in range(3)]
    vm = pl.BlockSpec(memory_space=pltpu.VMEM)
    res = pl.pallas_call(body, name=name, in_specs=[vm] * len(flat), out_specs=[vm] * len(out_shape),
                         out_shape=out_shape)(*flat)
    return [tuple(res[3 * t:3 * t + 3]) for t in range(n)]


def _place():
    return lax.axis_index("x"), lax.axis_index("y"), lax.axis_index("c")


_FLIPS = ((1, 0), (0, 1), (1, 1))


ROW_ALIGN = 16


def _half(rows, which):
    return pl.ds(pl.multiple_of(which * (rows // 2), ROW_ALIGN), rows // 2)


def _remote(src, dst, send_sems, recv_sems, k, to):
    return pltpu.make_async_remote_copy(src_ref=src, dst_ref=dst, send_sem=send_sems.at[k], recv_sem=recv_sems.at[k],
                                        device_id=to, device_id_type=MESH)


class _Gather:
    def __init__(self, w_refs, out_refs, send_sems, recv_sems):
        self.w_refs, self.out_refs, self.send_sems, self.recv_sems = w_refs, out_refs, send_sems, recv_sems
        self.pairs = [(i, j) for i in range(len(w_refs)) for j in range(3)]

    @staticmethod
    def out_shape(shards):
        return [jax.ShapeDtypeStruct((N_CHIPS,) + s.shape, s.dtype) for s in shards]

    @staticmethod
    def semaphores(n):
        return [pltpu.SemaphoreType.DMA((6 * n,)), pltpu.SemaphoreType.DMA((6 * n,))]

    def _where(self):
        x, y, c = _place()
        return x, y, c, [(x ^ fx, y ^ fy) for fx, fy in _FLIPS]

    def _over_ici(self, i, j, landed):
        x, y, c, chips = self._where()
        px, py = chips[j]
        mine = _half(self.w_refs[i].shape[0], c)
        if landed:
            src = dst = self.out_refs[i].at[2 * px + py, mine]
        else:
            src, dst = self.w_refs[i].at[mine], self.out_refs[i].at[2 * x + y, mine]
        return _remote(src, dst, self.send_sems, self.recv_sems, 6 * i + j, (px, py, c))

    def _over_d2d(self, i, j, which):
        x, y, c, chips = self._where()
        px, py = chips[j]
        blk = self.out_refs[i].at[2 * px + py, _half(self.w_refs[i].shape[0], which)]
        return _remote(blk, blk, self.send_sems, self.recv_sems, 6 * i + 3 + j, (x, y, 1 - c))

    def issue(self):
        for i, j in self.pairs:
            self._over_ici(i, j, False).start()

    def finish(self):
        c = lax.axis_index("c")
        for i, j in self.pairs:
            self._over_ici(i, j, True).wait_recv()
            self._over_d2d(i, j, c).start()
        for i, j in self.pairs:
            self._over_d2d(i, j, 1 - c).wait_recv()
        for i, j in self.pairs:
            self._over_ici(i, j, False).wait_send()
            self._over_d2d(i, j, c).wait_send()


def _swap_halves(gs, *, name):
    n = len(gs)
    any_spec = pl.BlockSpec(memory_space=pl.ANY)

    def body(*refs):
        swap = _Swap(refs[:n], refs[n:2 * n], refs[2 * n], refs[2 * n + 1])
        swap.issue()
        swap.finish()

    return pl.pallas_call(
        body, name=name, in_specs=[any_spec] * n, out_specs=[any_spec] * n,
        out_shape=_Swap.out_shape(gs), scratch_shapes=_Swap.semaphores(n),
    )(*gs)


class _Swap:
    def __init__(self, g_refs, out_refs, send_sems, recv_sems):
        self.g_refs, self.out_refs, self.send_sems, self.recv_sems = g_refs, out_refs, send_sems, recv_sems

    @staticmethod
    def out_shape(gs):
        return [jax.ShapeDtypeStruct((N_CHIPS, g.shape[1] // 2, g.shape[2]), g.dtype) for g in gs]

    @staticmethod
    def semaphores(n):
        return [pltpu.SemaphoreType.DMA((n,)), pltpu.SemaphoreType.DMA((n,))]

    def _copies(self):
        x, y, c = _place()
        return [_remote(g.at[:, _half(g.shape[1], 1 - c)], out, self.send_sems, self.recv_sems, i, (x, y, 1 - c))
                for i, (g, out) in enumerate(zip(self.g_refs, self.out_refs))]

    def issue(self):
        for cp in self._copies():
            cp.start()

    def finish(self):
        for cp in self._copies():
            cp.wait()


def _add_halves(g, got, c_arr, *, name):
    _, rows, cols = g.shape
    blk = (None, rows // 2, cols)

    def body(c_ref, g_ref, got_ref, o_ref):
        o_ref[...] = (g_ref[...] + got_ref[...]).astype(o_ref.dtype)

    return pl.pallas_call(
        body, name=name,
        grid_spec=pltpu.PrefetchScalarGridSpec(
            num_scalar_prefetch=1, grid=(N_CHIPS,),
            in_specs=[pl.BlockSpec(blk, lambda j, c_ref: (j, c_ref[0], 0)), pl.BlockSpec(blk, lambda j, c_ref: (j, 0, 0))],
            out_specs=pl.BlockSpec(blk, lambda j, c_ref: (j, 0, 0))),
        out_shape=jax.ShapeDtypeStruct((N_CHIPS, rows // 2, cols), WIRE_DTYPE),
        compiler_params=_cparams(("parallel",)),
    )(c_arr, g, got)


class _Exchange:
    def __init__(self, q_refs, out_refs, send_sems, recv_sems):
        self.q_refs, self.out_refs, self.send_sems, self.recv_sems = q_refs, out_refs, send_sems, recv_sems
        self.pairs = [(i, j) for i in range(len(q_refs)) for j in range(3)]

    @staticmethod
    def out_shape(qs):
        return [jax.ShapeDtypeStruct(q.shape, q.dtype) for q in qs]

    @staticmethod
    def semaphores(n):
        return [pltpu.SemaphoreType.DMA((3 * n,)), pltpu.SemaphoreType.DMA((3 * n,))]

    def _copy(self, i, j, landed):
        x, y, c = _place()
        px, py = [(x ^ fx, y ^ fy) for fx, fy in _FLIPS][j]
        if landed:
            src = dst = self.out_refs[i].at[2 * px + py]
        else:
            src, dst = self.q_refs[i].at[2 * px + py], self.out_refs[i].at[2 * x + y]
        return _remote(src, dst, self.send_sems, self.recv_sems, 3 * i + j, (px, py, c))

    def issue(self):
        for i, j in self.pairs:
            self._copy(i, j, False).start()

    def finish(self):
        for i, j in self.pairs:
            self._copy(i, j, True).wait_recv()
        for i, j in self.pairs:
            self._copy(i, j, False).wait_send()


def _sum_chips(q, r, place_arr, *, name):
    _, h, cols = q.shape
    blk = (None, h, cols)

    def body(p_ref, q_ref, r1_ref, r2_ref, r3_ref, o_ref):
        o_ref[...] = ((q_ref[...].astype(F32) + r1_ref[...].astype(F32)) + r2_ref[...].astype(F32)) + r3_ref[...].astype(F32)

    other = [pl.BlockSpec(blk, lambda i, p_ref, f=f: (p_ref[0] ^ f, 0, 0)) for f in (1, 2, 3)]
    return pl.pallas_call(
        body, name=name,
        grid_spec=pltpu.PrefetchScalarGridSpec(
            num_scalar_prefetch=1, grid=(1,),
            in_specs=[pl.BlockSpec(blk, lambda i, p_ref: (p_ref[0], 0, 0))] + other,
            out_specs=pl.BlockSpec((h, cols), lambda i, p_ref: (p_ref[1], 0))),
        out_shape=jax.ShapeDtypeStruct((2 * h, cols), F32),
        compiler_params=_cparams(("arbitrary",)),
    )(place_arr, q, r, r, r)


def _join_halves(fs, *, name):
    n = len(fs)
    any_spec = pl.BlockSpec(memory_space=pl.ANY)

    def body(*refs):
        out_refs, send_sems, recv_sems = refs[n:2 * n], refs[2 * n], refs[2 * n + 1]
        x, y, c = _place()
        sent = []
        for i in range(n):
            mine = out_refs[i].at[_half(fs[i].shape[0], c)]
            cp = _remote(mine, mine, send_sems, recv_sems, i, (x, y, 1 - c))
            cp.start()
            sent.append(cp)
        for i in range(n):
            its = out_refs[i].at[_half(fs[i].shape[0], 1 - c)]
            _remote(its, its, send_sems, recv_sems, i, (x, y, 1 - c)).wait_recv()
        for cp in sent:
            cp.wait_send()

    return pl.pallas_call(
        body, name=name, in_specs=[any_spec] * n, out_specs=[any_spec] * n,
        out_shape=[jax.ShapeDtypeStruct(f.shape, f.dtype) for f in fs], input_output_aliases={i: i for i in range(n)},
        scratch_shapes=[pltpu.SemaphoreType.DMA((n,)), pltpu.SemaphoreType.DMA((n,))],
    )(*fs)


def _allreduce_small(buf, *, name):
    R = buf.shape[0]
    vm = pl.BlockSpec(memory_space=pltpu.VMEM)

    def body(b_ref, o_ref, slots, send_sems, recv_sems):
        x, y, c = _place()
        me = 4 * x + 2 * y + c
        slots[me] = b_ref[...]
        sends = []
        for kk in range(1, 8):
            fx, fy, fc = (kk >> 2) & 1, (kk >> 1) & 1, kk & 1
            cp = pltpu.make_async_remote_copy(
                src_ref=b_ref, dst_ref=slots.at[me], send_sem=send_sems.at[kk - 1], recv_sem=recv_sems.at[kk - 1],
                device_id=(x ^ fx, y ^ fy, c ^ fc), device_id_type=MESH)
            cp.start()
            sends.append(cp)
        for kk in range(1, 8):
            fx, fy, fc = (kk >> 2) & 1, (kk >> 1) & 1, kk & 1
            peer = 4 * (x ^ fx) + 2 * (y ^ fy) + (c ^ fc)
            pltpu.make_async_remote_copy(
                src_ref=b_ref, dst_ref=slots.at[peer], send_sem=send_sems.at[kk - 1], recv_sem=recv_sems.at[kk - 1],
                device_id=(x ^ fx, y ^ fy, c ^ fc), device_id_type=MESH).wait_recv()
        for cp in sends:
            cp.wait_send()
        acc = slots[0]
        for d in range(1, 8):
            acc = acc + slots[d]
        o_ref[...] = acc

    return pl.pallas_call(
        body, name=name, in_specs=[vm], out_specs=vm,
        out_shape=jax.ShapeDtypeStruct((R, LANES), F32),
        scratch_shapes=[pltpu.VMEM((8, R, LANES), F32), pltpu.SemaphoreType.DMA((7,)), pltpu.SemaphoreType.DMA((7,))],
    )(buf)


def _pad_rows(v, mult=8 * LANES):
    flat = v.reshape(-1)
    n = flat.shape[0]
    tot = -(-n // mult) * mult
    return jnp.pad(flat, (0, tot - n)).reshape(-1, LANES)


def _pack_small(parts):
    return jnp.concatenate([_pad_rows(p.astype(F32)) for p in parts], axis=0)


def _unpack_small(buf, shapes):
    out, r = [], 0
    for sh in shapes:
        n = math.prod(sh)
        rows = -(-n // (8 * LANES)) * 8
        out.append(buf[r:r + rows].reshape(-1)[:n].reshape(sh))
        r += rows
    return out


def _cols_to_shards(w):
    *lead, K, N = w.shape
    t = w.reshape(*lead, K, N_CHIPS, N // N_CHIPS)
    return jnp.moveaxis(t, -2, 0)


def _shards_to_cols(t):
    t = jnp.moveaxis(t, 0, -2)
    *lead, K, _, n = t.shape
    return t.reshape(*lead, K, N_CHIPS * n)


_BIG = ("m_w_in", "m_w_out", "a_w_in", "a_w_out", "f_w_up", "f_w_down")


def kernel(x, m_w_in, m_gate_bias, m_head_norm, m_w_out, a_w_in, a_b_in, a_sinks, a_w_out, a_b_out, norm_mix_pre, norm_mix_post, norm_ffn_pre, norm_ffn_post, f_w_up, f_conv_w, f_conv_b, f_w_down, loss_target, m_m_w_in, m_m_gate_bias, m_m_head_norm, m_m_w_out, m_a_w_in, m_a_b_in, m_a_sinks, m_a_w_out, m_a_b_out, m_norm_mix_pre, m_norm_mix_post, m_norm_ffn_pre, m_norm_ffn_post, m_f_w_up, m_f_conv_w, m_f_conv_b, m_f_w_down, v_m_w_in, v_m_gate_bias, v_m_head_norm, v_m_w_out, v_a_w_in, v_a_b_in, v_a_sinks, v_a_w_out, v_a_b_out, v_norm_mix_pre, v_norm_mix_post, v_norm_ffn_pre, v_norm_ffn_post, v_f_w_up, v_f_conv_w, v_f_conv_b, v_f_w_down):
    params = dict(m_w_in=m_w_in, m_gate_bias=m_gate_bias, m_head_norm=m_head_norm, m_w_out=m_w_out, a_w_in=a_w_in,
                  a_b_in=a_b_in, a_sinks=a_sinks, a_w_out=a_w_out, a_b_out=a_b_out, norm_mix_pre=norm_mix_pre,
                  norm_mix_post=norm_mix_post, norm_ffn_pre=norm_ffn_pre, norm_ffn_post=norm_ffn_post, f_w_up=f_w_up,
                  f_conv_w=f_conv_w, f_conv_b=f_conv_b, f_w_down=f_w_down)
    mom1 = dict(m_w_in=m_m_w_in, m_gate_bias=m_m_gate_bias, m_head_norm=m_m_head_norm, m_w_out=m_m_w_out,
                a_w_in=m_a_w_in, a_b_in=m_a_b_in, a_sinks=m_a_sinks, a_w_out=m_a_w_out, a_b_out=m_a_b_out,
                norm_mix_pre=m_norm_mix_pre, norm_mix_post=m_norm_mix_post, norm_ffn_pre=m_norm_ffn_pre,
                norm_ffn_post=m_norm_ffn_post, f_w_up=m_f_w_up, f_conv_w=m_f_conv_w, f_conv_b=m_f_conv_b,
                f_w_down=m_f_w_down)
    mom2 = dict(m_w_in=v_m_w_in, m_gate_bias=v_m_gate_bias, m_head_norm=v_m_head_norm, m_w_out=v_m_w_out,
                a_w_in=v_a_w_in, a_b_in=v_a_b_in, a_sinks=v_a_sinks, a_w_out=v_a_w_out, a_b_out=v_a_b_out,
                norm_mix_pre=v_norm_mix_pre, norm_mix_post=v_norm_mix_post, norm_ffn_pre=v_norm_ffn_pre,
                norm_ffn_post=v_norm_ffn_post, f_w_up=v_f_w_up, f_conv_w=v_f_conv_w, f_conv_b=v_f_conv_b,
                f_w_down=v_f_w_down)
    order = list(params)

    mx, my, mc = _place()
    chip = 2 * mx + my
    h0 = x[0]
    target = loss_target[0]

    def two_d(t):
        return t.reshape(-1, t.shape[-1])

    early = ("m_w_in",)
    late = tuple(n for n in _BIG if n not in early)
    mine = {n: two_d(params[n]).astype(MXU_DTYPE) for n in _BIG}

    def with_own_slot(names, theirs):
        return {n: lax.dynamic_update_slice(t, mine[n][None], (chip, 0, 0)) for n, t in zip(names, theirs)}

    z0, *early_theirs = _rms_fwd(h0, norm_mix_pre[0:1], [mine[n] for n in early], out_dtype=MXU_DTYPE,
                                 name="mix_pre_norm0")
    gathered = with_own_slot(early, early_theirs)

    def in_place(shard, axis):
        width = shard.shape[axis]
        z = jnp.zeros(shard.shape[:axis] + (N_CHIPS * width,) + shard.shape[axis + 1:], F32)
        contrib = jnp.where(mc == 0, shard, 0.0)
        return lax.dynamic_update_slice_in_dim(z, contrib, chip * width, axis)

    sm_in = [in_place(a_b_in, 1), in_place(a_b_out, 1), in_place(f_conv_w, 2)]
    sm_full = _unpack_small(_allreduce_small(_pack_small(sm_in), name="gather_small"), [t.shape for t in sm_in])
    b_in_full, b_out_full, conv_w_full = sm_full

    W_in = _shards_to_cols(gathered["m_w_in"])
    W_all = jnp.pad(W_in, ((0, 0), (0, M_PROJ - W_in.shape[1])))
    gbias = jnp.pad(m_gate_bias[0].reshape(1, 2 * M_HEADS), ((0, 0), (0, LANES - 2 * M_HEADS)))

    grads = {}

    def ffn_fwd(i, z1, end):
        u = _mm(z1, W_up, b_layer=i, name=f"ffn_up{i}")
        a = _convgate_fwd(u, conv_w_full[i], f_conv_b[i:i + 1], name=f"ffn_act{i}")
        return u, a, _mm(a, W_down[i], rows=end, name=f"ffn_down{i}")

    proj = _mm(z0, W_all, name="mlstm_proj")
    gated, C_all, n_all, m_all, *late_theirs = _mlstm_fwd(proj, gbias, m_head_norm, [mine[n] for n in late],
                                                          name="mlstm_fwd")
    gathered.update(with_own_slot(late, late_theirs))
    W_mout = gathered["m_w_out"].reshape(D_MODEL, D_MODEL)
    A_in, A_out = _shards_to_cols(gathered["a_w_in"]), gathered["a_w_out"].reshape(D_MODEL, D_MODEL)
    W_up = gathered["f_w_up"]
    dsh = D_FF // N_CHIPS
    W_down = [gathered["f_w_down"][:, i * dsh:(i + 1) * dsh].reshape(D_FF, D_MODEL) for i in range(2)]
    zm0, h1, z1 = _mm(gated, W_mout, rows=_rows_sublayer_end(h0, norm_mix_post[0:1], norm_ffn_pre[0:1]), name="mlstm_out")
    u0, a0, (zf0, h2, y0) = ffn_fwd(0, z1, _rows_sublayer_end(h1, norm_ffn_post[0:1], norm_mix_pre[1:2]))

    aproj = _mm(y0, A_in, bias=b_in_full, name="attn_proj")
    sinks = a_sinks.reshape(A_KVH, A_G)
    ao = _attn_fwd(aproj, sinks, name="attn_fwd")
    zm1, h3, z3 = _mm(ao, A_out, bias=b_out_full, rows=_rows_sublayer_end(h2, norm_mix_post[1:2], norm_ffn_pre[1:2]),
                      name="attn_out")
    u1, a1, (zf1, dh, dzf1, g_post1, loss_lanes) = ffn_fwd(1, z3, _rows_last_sublayer_end(h3, norm_ffn_post[1:2], target))
    loss = lax.psum(loss_lanes[0, 0], ("x", "y", "c"))

    g_post, g_fpre, g_mpost, g_mpre = [None, g_post1], [None, None], [None, None], [None, None]
    dW_up, dW_down, dconv_w, dconv_b = [None, None], [None, None], [None, None], [None, None]

    def by_rows(g):
        return g.reshape(N_CHIPS, -1, g.shape[-1])

    def ffn_bwd(i, dzf, z1_, u, a, start, ride):
        da = _mm(dzf, W_down[i], tb=True, name=f"ffn_down_dx{i}")
        dW_down[i] = _mm(a, dzf, ta=True, name=f"ffn_down_dw{i}")
        du, dconv_w[i], dconv_b[i] = _convgate_bwd(u, da, conv_w_full[i], f_conv_b[i:i + 1], name=f"ffn_act_bwd{i}")
        dW_up[i] = _mm(z1_, du, ta=True, out_stacked=True, name=f"ffn_up_dw{i}")
        riders = (list(ride) + [by_rows(dW_down[i]), dW_up[i]]) if ride is not None else []
        outs = _mm(du, W_up, tb=True, b_layer=i, rows=start, ride=(_Swap, riders) if riders else None,
                   name=f"ffn_up_dx{i}")
        return outs[:len(start.outs)], riders, outs[len(start.outs):]

    (dh3, g_fpre[1], dzm1, g_mpost[1], db_out), _, _ = ffn_bwd(
        1, dzf1, z3, u1, a1, _rows_sublayer_start_bwd(h3, dh, zm1, norm_ffn_pre[1:2], norm_mix_post[1:2]), None)
    dao = _mm(dzm1, A_out, tb=True, name="attn_out_dx")
    dA_out = _mm(ao, dzm1, ta=True, name="attn_out_dw")
    daq, dakv, dsinks, cs_q, cs_kv = _attn_bwd(aproj, sinks, dao, name="attn_bwd")
    db_in = jnp.concatenate([cs_q, cs_kv], axis=1)
    dh2, g_mpre[1], dzf0, g_post[0], _ = _mm(
        daq, A_in[:, :A_NQ], tb=True, add=_mm(dakv, A_in[:, A_NQ:], tb=True, name="attn_proj_kv_dx"),
        rows=_rows_sublayer_start_bwd(h2, dh3, zf0, norm_mix_pre[1:2], norm_ffn_post[0:1]), name="attn_proj_q_dx")
    dA_in = jnp.concatenate([_mm(y0, daq, ta=True, name="attn_proj_q_dw"),
                             _mm(y0, dakv, ta=True, name="attn_proj_kv_dw")], axis=1)

    c_arr = jnp.reshape(mc, (1,)).astype(jnp.int32)
    place_arr = jnp.stack([chip, mc]).astype(jnp.int32)

    def add_halves(tags, gs, got):
        return [_add_halves(g, t, c_arr, name=f"grad_add_halves_{tag}") for tag, g, t in zip(tags, gs, got)]

    (dh1, g_fpre[0], dzm0, g_mpost[0], _), rode, rode_got = ffn_bwd(
        0, dzf0, z1, u0, a0, _rows_sublayer_start_bwd(h1, dh2, zm0, norm_ffn_pre[0:1], norm_mix_post[0:1]),
        [_cols_to_shards(dA_in), by_rows(dA_out), dW_up[1], by_rows(dW_down[1])])
    rode_tags = ["a_w_in", "a_w_out", "f_w_up1", "f_w_down1", "f_w_down0", "f_w_up0"]
    dgated = _mm(dzm0, W_mout, tb=True, name="mlstm_out_dx")
    dW_mout = _mm(gated, dzm0, ta=True, name="mlstm_out_dw")
    next_tags, next_gs = ["m_w_out"], [by_rows(dW_mout)]
    late_tags = rode_tags + next_tags
    late_part = (add_halves(rode_tags, rode, rode_got)
                 + add_halves(next_tags, next_gs, _swap_halves(next_gs, name="grad_swap_halves_late")))
    dproj, dgbias, dhn, *late_from = _mlstm_bwd(proj, gbias, m_head_norm, C_all, n_all, m_all, dgated, late_part,
                                                name="mlstm_bwd")
    dW_all = _mm(z0, dproj, ta=True, name="mlstm_proj_dw")
    dW_min = dW_all[:, :m_w_in.shape[-1] * N_CHIPS]

    early_tags, early_gs = ["m_w_in"], [_cols_to_shards(dW_min)]
    early_part = add_halves(early_tags, early_gs, _swap_halves(early_gs, name="grad_swap_halves_early"))
    grad_x, g_mpre[0], *early_from = _mm(
        dproj, W_all, tb=True, rows=_rows_first_sublayer_start_bwd(h0, dh1, norm_mix_pre[0:1]),
        ride=(_Exchange, early_part),
        name="mlstm_proj_dx")
    tags = early_tags + late_tags
    halves = [_sum_chips(q, r, place_arr, name=f"grad_sum_chips_{tag}")
              for tag, q, r in zip(tags, early_part + late_part, list(early_from) + list(late_from))]
    reduced = dict(zip(tags, _join_halves(halves, name="grad_join_halves")))
    layer_grads = dict(m_w_in=[reduced["m_w_in"]], m_w_out=[reduced["m_w_out"]], a_w_in=[reduced["a_w_in"]],
                       a_w_out=[reduced["a_w_out"]], f_w_up=[reduced["f_w_up0"], reduced["f_w_up1"]],
                       f_w_down=[reduced["f_w_down0"], reduced["f_w_down1"]])

    small_g = [
        dgbias[:, :2 * M_HEADS].reshape(1, 2, M_HEADS),
        dhn,
        dsinks.reshape(1, A_QH),
        db_in, db_out,
        jnp.concatenate(g_mpre), jnp.concatenate(g_mpost), jnp.concatenate(g_fpre), jnp.concatenate(g_post),
        jnp.stack(dconv_w), jnp.concatenate(dconv_b),
    ]
    small_names = ["m_gate_bias", "m_head_norm", "a_sinks", "a_b_in", "a_b_out", "norm_mix_pre", "norm_mix_post",
                   "norm_ffn_pre", "norm_ffn_post", "f_conv_w", "f_conv_b"]
    red = _unpack_small(_allreduce_small(_pack_small(small_g), name="reduce_small"), [t.shape for t in small_g])
    for n, t in zip(small_names, red):
        if n in ("a_b_in", "a_b_out", "f_conv_w"):
            axis = t.ndim - 1
            width = params[n].shape[axis]
            t = lax.dynamic_slice_in_dim(t, chip * width, width, axis)
        grads[n] = t

    deltas, new_m, new_v = {}, {}, {}
    for n in _BIG:
        grads[n], deltas[n], new_m[n], new_v[n] = _adamw_layers(params[n], layer_grads[n], mom1[n], mom2[n],
                                                                name=f"adamw_{n}")
    two = lambda t: t.reshape(-1, t.shape[-1])
    res = _adamw_small([(two(params[n]), two(grads[n]), two(mom1[n]), two(mom2[n])) for n in small_names],
                       name="adamw_small")
    for n, (d, nm, nv) in zip(small_names, res):
        sh = params[n].shape
        deltas[n], new_m[n], new_v[n] = d.reshape(sh), nm.reshape(sh), nv.reshape(sh)

    return (loss, grad_x[None], *[grads[n] for n in order], *[deltas[n] for n in order],
            *[new_m[n] for n in order], *[new_v[n] for n in order])
```

```python
import functools
import math

import numpy as np

import jax
import jax.numpy as jnp
from jax import lax
from jax.experimental import pallas as pl
from jax.experimental.pallas import tpu as pltpu

F32 = jnp.float32
MXU_DTYPE = jnp.bfloat16
WIRE_DTYPE = jnp.bfloat16
MESH = pl.DeviceIdType.MESH

D_MODEL = 1024
EPS = 1e-6
M_HEADS, M_QK, M_V, M_CHUNK = 8, 64, 128, 128
GATE_CAP = 15.0
A_DH, A_QH, A_KVH, A_G, A_BLK = 64, 16, 2, 8, 128
D_FF = 2816
N_CHIPS = 4
LANES = 128
VMEM_LIMIT = 56 * 1024 * 1024

ADAM_LR, ADAM_B1, ADAM_B2, ADAM_EPS, ADAM_WD, ADAM_STEP = 0.001, 0.9, 0.999, 1e-08, 0.01, 10


def _cparams(sem):
    return pltpu.CompilerParams(dimension_semantics=sem, vmem_limit_bytes=VMEM_LIMIT)


def _pick(n, cands):
    for c in cands:
        if n % c == 0:
            return c
    return n


class _Rows:
    def __init__(self, tiles, vecs, outs, fn):
        self.tiles, self.vecs, self.outs, self.fn = list(tiles), list(vecs), list(outs), fn


MM_VMEM_BUDGET = 46 * 1024 * 1024
ROWS_FULL_K = 4224
WIDE_N = 3200


def _mm(a, b, *, ta=False, tb=False, out_dtype=F32, bias=None, add=None, b_layer=None, out_stacked=False, rows=None,
        ride=None, name):
    if ta:
        K, M = a.shape
    else:
        M, K = a.shape
    b_stacked = b_layer is not None
    if b_stacked:
        assert not ta
        n_sh = b.shape[2]
        w_rows, w_cols = D_MODEL, N_CHIPS * n_sh
        N, Kb = (w_rows, w_cols) if tb else (w_cols, w_rows)
    elif tb:
        N, Kb = b.shape
    else:
        Kb, N = b.shape
    assert K == Kb, (a.shape, b.shape)
    tn = _pick(N, (1024, 1408, 1280, 640, 512, 256, 128))
    tk = K if K <= 2816 else _pick(K, (2816, 2048, 1408, 1024, 512, 256, 128))
    if tn < 1024 <= N <= WIDE_N:
        tn, tk = N, min(tk, 1024)
    shards_per_step = 1
    if b_stacked and tb:
        shards_per_step = N_CHIPS if rows is not None else 2
        tk = shards_per_step * n_sh
    if (b_stacked and not tb) or out_stacked:
        tn = N // N_CHIPS
    if rows is not None:
        assert tn == N and not out_stacked and not ta
        if K <= ROWS_FULL_K:
            tk = K
    nk = K // tk

    def vmem_bytes(rows_per_tile):
        total = 2 * (rows_per_tile * tk * a.dtype.itemsize + tk * tn * b.dtype.itemsize)
        if rows is None:
            total += 2 * rows_per_tile * tn * jnp.dtype(out_dtype).itemsize
        else:
            total += 2 * rows_per_tile * N * (4 * len(rows.tiles)
                                              + sum(jnp.dtype(dt).itemsize for kind, dt in rows.outs if kind == "tile"))
        if add is not None:
            total += 2 * rows_per_tile * tn * 4
        return total + (rows_per_tile * tn * 4 if nk > 1 else 0)

    tm = next(c for c in (2048, 1024, 1408, 512, 256, 128, M) if M % c == 0 and (vmem_bytes(c) <= MM_VMEM_BUDGET or c <= 128))
    dn = (((0 if ta else 1,), (1 if tb else 0,)), ((), ()))
    has_bias, has_add = bias is not None, add is not None
    n_tiles, n_vecs, n_outs = (len(rows.tiles), len(rows.vecs), len(rows.outs)) if rows is not None else (0, 0, 1)
    rider, riding = ride if ride is not None else (None, ())
    n_ex = len(riding)
    assert not n_ex or rows is not None
    grid = (M // tm, N // tn, nk)

    def body(*refs):
        a_ref, b_ref = refs[0], refs[1]
        pos = 2
        bias_ref = add_ref = None
        if has_bias:
            bias_ref = refs[pos]
            pos += 1
        if has_add:
            add_ref = refs[pos]
            pos += 1
        tile_refs, vec_refs = refs[pos:pos + n_tiles], refs[pos + n_tiles:pos + n_tiles + n_vecs]
        pos += n_tiles + n_vecs
        ex_in = refs[pos:pos + n_ex]
        pos += n_ex
        out_refs, ex_out = refs[pos:pos + n_outs], refs[pos + n_outs:pos + n_outs + n_ex]
        pos += n_outs + n_ex
        acc_ref = refs[pos] if nk > 1 else None
        if n_ex:
            ex = rider(ex_in, ex_out, *refs[pos + (nk > 1):])
            step = (pl.program_id(0) * grid[1] + pl.program_id(1)) * grid[2] + pl.program_id(2)

            @pl.when(step == 0)
            def _():
                ex.issue()

        def finish(r):
            if has_bias:
                r = r + bias_ref[...]
            if has_add:
                r = r + add_ref[...]
            if rows is None:
                out_refs[0][...] = r.astype(out_dtype)
                return
            vals = rows.fn(r, [t[...] for t in tile_refs], [v[...] for v in vec_refs])
            first = pl.program_id(0) == 0
            for (kind, _), o_ref, val in zip(rows.outs, out_refs, vals):
                if kind == "tile":
                    o_ref[...] = val.astype(o_ref.dtype)
                else:
                    @pl.when(first)
                    def _(o_ref=o_ref, val=val):
                        o_ref[...] = val

                    @pl.when(jnp.logical_not(first))
                    def _(o_ref=o_ref, val=val):
                        o_ref[...] += val

        if shards_per_step > 1:
            part = sum(lax.dot_general(a_ref[:, s * n_sh:(s + 1) * n_sh], b_ref[s], dn, preferred_element_type=F32)
                       for s in range(shards_per_step))
        else:
            part = lax.dot_general(a_ref[...], b_ref[...], dn, preferred_element_type=F32)
        if nk == 1:
            finish(part)
        else:
            k = pl.program_id(2)

            @pl.when(k == 0)
            def _():
                acc_ref[...] = part

            @pl.when(k > 0)
            def _():
                acc_ref[...] += part

            @pl.when(k == nk - 1)
            def _():
                finish(acc_ref[...])

        if n_ex:
            @pl.when(step == grid[0] * grid[1] * grid[2] - 1)
            def _():
                ex.finish()

    a_spec = pl.BlockSpec((tk, tm), lambda i, j, k: (k, i)) if ta else pl.BlockSpec((tm, tk), lambda i, j, k: (i, k))
    if b_stacked and tb:
        off = b_layer * (w_rows // tn)
        b_spec = pl.BlockSpec((shards_per_step, tn, n_sh), lambda i, j, k: (k, off + j, 0))
    elif b_stacked:
        off = b_layer * (w_rows // tk)
        b_spec = pl.BlockSpec((None, tk, tn), lambda i, j, k: (j, off + k, 0))
    elif tb:
        b_spec = pl.BlockSpec((tn, tk), lambda i, j, k: (j, k))
    else:
        b_spec = pl.BlockSpec((tk, tn), lambda i, j, k: (k, j))
    if out_stacked:
        out_spec = pl.BlockSpec((None, tm, tn), lambda i, j, k: (j, i, 0))
        out_shape = jax.ShapeDtypeStruct((N_CHIPS, M, tn), out_dtype)
    else:
        out_spec = pl.BlockSpec((tm, tn), lambda i, j, k: (i, j))
        out_shape = jax.ShapeDtypeStruct((M, N), out_dtype)
    in_specs, args = [a_spec, b_spec], [a, b]
    if has_bias:
        in_specs.append(pl.BlockSpec((1, tn), lambda i, j, k: (0, j)))
        args.append(bias)
    if has_add:
        in_specs.append(pl.BlockSpec((tm, tn), lambda i, j, k: (i, j)))
        args.append(add)
    semantics = ("parallel", "parallel", "arbitrary")
    if rows is not None:
        tile_spec = pl.BlockSpec((tm, N), lambda i, j, k: (i, 0))
        vec_spec = pl.BlockSpec((1, N), lambda i, j, k: (0, 0))
        in_specs += [tile_spec] * n_tiles + [vec_spec] * n_vecs
        args += rows.tiles + rows.vecs
        out_spec = [tile_spec if kind == "tile" else vec_spec for kind, _ in rows.outs]
        out_shape = [jax.ShapeDtypeStruct((M, N) if kind == "tile" else (1, N), dt) for kind, dt in rows.outs]
        semantics = ("arbitrary", "arbitrary", "arbitrary")
    scratch = [pltpu.VMEM((tm, tn), F32)] if nk > 1 else []
    if n_ex:
        any_spec = pl.BlockSpec(memory_space=pl.ANY)
        in_specs += [any_spec] * n_ex
        args += list(riding)
        out_spec = out_spec + [any_spec] * n_ex
        out_shape = out_shape + rider.out_shape(riding)
        scratch += rider.semaphores(n_ex)
    return pl.pallas_call(
        body, name=name,
        grid=(M // tm, N // tn, nk),
        in_specs=in_specs,
        out_specs=out_spec,
        out_shape=out_shape,
        scratch_shapes=scratch,
        compiler_params=_cparams(semantics),
    )(*args)


ROW_TILE = 512


def _rms_fwd(x, g, shards, *, out_dtype, name):
    S, D = x.shape
    tm = _pick(S, (ROW_TILE, 256, 128))
    nsh, nt = len(shards), S // tm
    any_spec = pl.BlockSpec(memory_space=pl.ANY)

    def body(x_ref, g_ref, *rest):
        o_ref = rest[nsh]
        gather = _Gather(rest[:nsh], rest[nsh + 1:2 * nsh + 1], *rest[2 * nsh + 1:])

        @pl.when(pl.program_id(0) == 0)
        def _():
            gather.issue()

        o_ref[...] = _rms(x_ref[...], g_ref[...]).astype(out_dtype)

        @pl.when(pl.program_id(0) == nt - 1)
        def _():
            gather.finish()

    row = pl.BlockSpec((tm, D), lambda i: (i, 0))
    return pl.pallas_call(
        body, name=name, grid=(nt,),
        in_specs=[row, pl.BlockSpec((1, D), lambda i: (0, 0))] + [any_spec] * nsh, out_specs=[row] + [any_spec] * nsh,
        out_shape=[jax.ShapeDtypeStruct((S, D), out_dtype)] + _Gather.out_shape(shards),
        scratch_shapes=_Gather.semaphores(nsh),
        compiler_params=_cparams(("arbitrary",)),
    )(x, g, *shards)


def _rms(x, g):
    return x * lax.rsqrt(jnp.mean(x * x, axis=-1, keepdims=True) + EPS) * g


def _rms_vjp(x, g, dy):
    r = lax.rsqrt(jnp.mean(x * x, axis=-1, keepdims=True) + EPS)
    xh = x * r
    gd = dy * g
    dx = r * (gd - xh * jnp.mean(gd * xh, axis=-1, keepdims=True))
    return dx, jnp.sum(dy * xh, axis=0, keepdims=True)


def _rows_sublayer_end(res, g_post, g_pre_next):
    def fn(z, tiles, vecs):
        h = tiles[0] + _rms(z, vecs[0])
        return [z, h, _rms(h, vecs[1])]

    return _Rows([res], [g_post, g_pre_next], [("tile", F32), ("tile", F32), ("tile", MXU_DTYPE)], fn)


def _rows_last_sublayer_end(res, g_post, target):
    def fn(z, tiles, vecs):
        y = tiles[0] + _rms(z, vecs[0])
        err = y - tiles[1]
        dy = err * (1.0 / err.shape[-1])
        dz, dg = _rms_vjp(z, vecs[0], dy)
        loss = 0.5 * jnp.sum(jnp.mean(err * err, axis=-1, keepdims=True), axis=0, keepdims=True)
        return [z, dy, dz, dg, jnp.broadcast_to(loss, dg.shape)]

    return _Rows([res, target], [g_post], [("tile", F32), ("tile", F32), ("tile", MXU_DTYPE), ("vec", F32), ("vec", F32)], fn)


def _rows_sublayer_start_bwd(x, res, z_below, g_pre, g_post_below):
    def fn(dz, tiles, vecs):
        dx, dg_pre = _rms_vjp(tiles[0], vecs[0], dz)
        dh = tiles[1] + dx
        dzb, dg_post = _rms_vjp(tiles[2], vecs[1], dh)
        return [dh, dg_pre, dzb, dg_post, jnp.sum(dzb, axis=0, keepdims=True)]

    return _Rows([x, res, z_below], [g_pre, g_post_below],
                 [("tile", F32), ("vec", F32), ("tile", MXU_DTYPE), ("vec", F32), ("vec", F32)], fn)


def _rows_first_sublayer_start_bwd(x, res, g_pre):
    def fn(dz, tiles, vecs):
        dx, dg_pre = _rms_vjp(tiles[0], vecs[0], dz)
        return [tiles[1] + dx, dg_pre]

    return _Rows([x, res], [g_pre], [("tile", F32), ("vec", F32)], fn)


def _mx(t):
    return t.astype(MXU_DTYPE)


def _mxf(t):
    return t.astype(MXU_DTYPE).astype(F32)


def _rawdot(a, b, ca, cb):
    return lax.dot_general(_mx(a), _mx(b), (((ca,), (cb,)), ((), ())), preferred_element_type=F32)


@functools.partial(jax.custom_vjp, nondiff_argnums=(2, 3))
def _dot(a, b, ca, cb):
    return _rawdot(a, b, ca, cb)


def _dot_fwd(a, b, ca, cb):
    return _rawdot(a, b, ca, cb), (a, b)


def _dot_bwd(ca, cb, res, g):
    a, b = res
    bj = 1 if cb == 0 else 0
    ai = 0 if ca == 1 else 1
    da = _rawdot(g, b, 1, bj) if ca == 1 else _rawdot(b, g, bj, 1)
    db = _rawdot(a, g, ai, 0) if cb == 0 else _rawdot(g, a, 0, ai)
    return da.astype(a.dtype), db.astype(b.dtype)


_dot.defvjp(_dot_fwd, _dot_bwd)


def _softcap(z):
    return GATE_CAP * jnp.tanh(z / GATE_CAP)


def _log_sigmoid(z):
    return jnp.minimum(z, 0.0) - jnp.log(1.0 + jnp.exp(-jnp.abs(z)))


def _sigmoid(z):
    return 0.5 * jnp.tanh(0.5 * z) + 0.5


def _lane_col(t, lane_index):
    lane = lax.broadcasted_iota(jnp.int32, t.shape, 1)
    return jnp.sum(jnp.where(lane == lane_index, t, 0.0), axis=1, keepdims=True)


def _mlstm_gates(G, bias):
    L = G.shape[0]
    z = _softcap(G + bias)
    ig = z
    lf = _log_sigmoid(z)
    ti = lax.broadcasted_iota(jnp.int32, (L, L), 0)
    si = lax.broadcasted_iota(jnp.int32, (L, L), 1)
    tril = (si <= ti).astype(F32)
    b = lax.dot_general(tril, lf, (((1,), (0,)), ((), ())), precision=lax.Precision.HIGHEST, preferred_element_type=F32)
    bL = jnp.sum(lf, axis=0, keepdims=True)
    return ig, b, ig.T, b.T, bL


def _mlstm_head(h, q, k_pair, v, o, ig_all, b_all, igT, bT, bL_all, hn, C, n, m):
    L = q.shape[0]
    k = jnp.where(_own_lanes(h, k_pair.shape), k_pair, 0.0)
    ti = lax.broadcasted_iota(jnp.int32, (L, L), 0)
    si = lax.broadcasted_iota(jnp.int32, (L, L), 1)
    lower = (si <= ti)
    ig = _lane_col(ig_all, h)
    b = _lane_col(b_all, M_HEADS + h)
    ig_row = igT[h:h + 1, :]
    b_row = bT[M_HEADS + h:M_HEADS + h + 1, :]
    bL = _lane_col(bL_all, M_HEADS + h)
    inter = b + m
    dlog = jnp.where(lower, b - b_row + ig_row, -jnp.inf)
    m_t = lax.stop_gradient(jnp.maximum(inter, jnp.max(dlog, axis=-1, keepdims=True)))
    qs = q * (M_QK ** -0.5)
    w = _dot(qs, k, 1, 1) * jnp.exp(dlog - m_t)
    s_inter = jnp.exp(inter - m_t)
    num = _dot(w, v, 1, 0) + s_inter * _dot(qs, C, 1, 0)
    den = jnp.sum(w, axis=-1, keepdims=True) + s_inter * jnp.sum(_mxf(qs) * _mxf(n), axis=-1, keepdims=True)
    hout = num * (1.0 / jnp.maximum(jnp.abs(den), jnp.exp(-m_t)))
    tail = bL - b + ig
    m_new = lax.stop_gradient(jnp.maximum(bL + m, jnp.max(tail, axis=0, keepdims=True)))
    ws = jnp.exp(tail - m_new)
    decay = jnp.exp(bL + m - m_new)
    wk = ws * k
    C_new = decay * C + _dot(wk, v, 0, 0)
    n_new = decay * n + jnp.sum(_mxf(ws) * _mxf(k), axis=0, keepdims=True)
    hs = hout * lax.rsqrt(jnp.mean(hout * hout, axis=-1, keepdims=True) + EPS) * hn
    gated = _sigmoid(o) * hs
    return (gated, C_new, n_new), m_new


M_OFF_Q, M_OFF_K = 0, M_HEADS * M_QK
M_OFF_V = 2 * M_HEADS * M_QK
M_OFF_O = M_OFF_V + M_HEADS * M_V
M_OFF_G = M_OFF_O + M_HEADS * M_V
M_PROJ = M_OFF_G + LANES
M_PAIRS = M_HEADS * M_QK // LANES


def _head_cols(off, h):
    return slice(off + h * LANES, off + (h + 1) * LANES)


def _own_lanes(h, shape):
    low = lax.broadcasted_iota(jnp.int32, shape, 1) < M_QK
    return low if h % 2 == 0 else jnp.logical_not(low)


def _mlstm_specs(NC, rev):
    H, L = M_HEADS, M_CHUNK
    cc = (lambda c: NC - 1 - c) if rev else (lambda c: c)
    proj = pl.BlockSpec((L, M_PROJ), lambda c: (cc(c), 0))
    vec = pl.BlockSpec((1, LANES), lambda c: (0, 0))
    hn = pl.BlockSpec((1, H * M_V), lambda c: (0, 0))
    hv = pl.BlockSpec((L, H * M_V), lambda c: (cc(c), 0))
    Cs = pl.BlockSpec((None, H, LANES, M_V), lambda c: (cc(c), 0, 0, 0))
    ns = pl.BlockSpec((None, H, 1, LANES), lambda c: (cc(c), 0, 0, 0))
    ms = pl.BlockSpec((None, H, 1, 1), lambda c: (cc(c), 0, 0, 0))
    return proj, vec, hn, hv, Cs, ns, ms


_MLSTM_STATE = [pltpu.VMEM((M_HEADS, LANES, M_V), F32), pltpu.VMEM((M_HEADS, 1, LANES), F32),
                pltpu.VMEM((M_HEADS, 1, 1), F32)]


def _mlstm_fwd(proj, bias, hn, shards, *, name):
    S = proj.shape[0]
    H, NC = M_HEADS, S // M_CHUNK
    ps, vec, hns, hv, Cs, ns, ms = _mlstm_specs(NC, False)
    nsh = len(shards)
    any_spec = pl.BlockSpec(memory_space=pl.ANY)

    def body(p_ref, b_ref, hn_ref, *rest):
        gated_ref, C_all, n_all, m_all = rest[nsh:nsh + 4]
        C_s, n_s, m_s, send_sems, recv_sems = rest[2 * nsh + 4:]
        gather = _Gather(rest[:nsh], rest[nsh + 4:2 * nsh + 4], send_sems, recv_sems)

        @pl.when(pl.program_id(0) == 0)
        def _():
            C_s[...] = jnp.zeros_like(C_s)
            n_s[...] = jnp.zeros_like(n_s)
            m_s[...] = jnp.zeros_like(m_s)
            gather.issue()

        gate_terms = _mlstm_gates(p_ref[:, M_OFF_G:M_OFF_G + LANES], b_ref[...])
        for h in range(H):
            C, n, m = C_s[h], n_s[h], m_s[h]
            C_all[h] = C
            n_all[h] = n
            m_all[h] = m
            (gated, Cn, nn), mn = _mlstm_head(
                h, p_ref[:, _head_cols(M_OFF_Q, h // 2)], p_ref[:, _head_cols(M_OFF_K, h // 2)],
                p_ref[:, _head_cols(M_OFF_V, h)], p_ref[:, _head_cols(M_OFF_O, h)], *gate_terms,
                hn_ref[:, _head_cols(0, h)], C, n, m)
            gated_ref[:, _head_cols(0, h)] = gated.astype(gated_ref.dtype)
            C_s[h] = Cn
            n_s[h] = nn
            m_s[h] = mn

        @pl.when(pl.program_id(0) == NC - 1)
        def _():
            gather.finish()

    return pl.pallas_call(
        body, name=name, grid=(NC,),
        in_specs=[ps, vec, hns] + [any_spec] * nsh,
        out_specs=[hv, Cs, ns, ms] + [any_spec] * nsh,
        out_shape=[jax.ShapeDtypeStruct((S, H * M_V), MXU_DTYPE),
                   jax.ShapeDtypeStruct((NC, H, LANES, M_V), F32),
                   jax.ShapeDtypeStruct((NC, H, 1, LANES), F32),
                   jax.ShapeDtypeStruct((NC, H, 1, 1), F32)] + _Gather.out_shape(shards),
        scratch_shapes=list(_MLSTM_STATE) + _Gather.semaphores(nsh),
        compiler_params=_cparams(("arbitrary",)),
    )(proj, bias, hn, *shards)


def _mlstm_bwd(proj, bias, hn, C_all, n_all, m_all, dgated, qs, *, name):
    S = proj.shape[0]
    H, NC = M_HEADS, S // M_CHUNK
    ps, vec, hns, hv, Cs, ns, ms = _mlstm_specs(NC, True)
    nq = len(qs)
    any_spec = pl.BlockSpec(memory_space=pl.ANY)

    def body(p_ref, b_ref, hn_ref, C_ref, n_ref, m_ref, dg_ref, *rest):
        dp_ref, db_ref, dhn_ref = rest[nq:nq + 3]
        dC_s, dn_s, send_sems, recv_sems = rest[2 * nq + 3:]
        exchange = _Exchange(rest[:nq], rest[nq + 3:2 * nq + 3], send_sems, recv_sems)

        @pl.when(pl.program_id(0) == 0)
        def _():
            dC_s[...] = jnp.zeros_like(dC_s)
            dn_s[...] = jnp.zeros_like(dn_s)
            db_ref[...] = jnp.zeros_like(db_ref)
            dhn_ref[...] = jnp.zeros_like(dhn_ref)
            exchange.issue()

        gate_terms, gates_vjp = jax.vjp(_mlstm_gates, p_ref[:, M_OFF_G:M_OFF_G + LANES], b_ref[...])
        d_terms = [jnp.zeros_like(t) for t in gate_terms]
        for h in range(H):
            def head(q, k, v, o, *rest, h=h):
                return _mlstm_head(h, q, k, v, o, *rest, m_ref[h])

            prim = (p_ref[:, _head_cols(M_OFF_Q, h // 2)], p_ref[:, _head_cols(M_OFF_K, h // 2)],
                    p_ref[:, _head_cols(M_OFF_V, h)], p_ref[:, _head_cols(M_OFF_O, h)], *gate_terms,
                    hn_ref[:, _head_cols(0, h)], C_ref[h], n_ref[h])
            _, vjp, _ = jax.vjp(head, *prim, has_aux=True)
            dq, dk, dv, do, *d_gate, dhnh, dC, dn = vjp((dg_ref[:, _head_cols(0, h)].astype(F32), dC_s[h], dn_s[h]))
            if h % 2 == 0:
                dq_pair, dk_pair = dq, dk
            else:
                dp_ref[:, _head_cols(M_OFF_Q, h // 2)] = (dq_pair + dq).astype(dp_ref.dtype)
                dp_ref[:, _head_cols(M_OFF_K, h // 2)] = (dk_pair + dk).astype(dp_ref.dtype)
            dp_ref[:, _head_cols(M_OFF_V, h)] = dv.astype(dp_ref.dtype)
            dp_ref[:, _head_cols(M_OFF_O, h)] = do.astype(dp_ref.dtype)
            d_terms = [a + g for a, g in zip(d_terms, d_gate)]
            dhn_ref[:, _head_cols(0, h)] += dhnh
            dC_s[h] = dC
            dn_s[h] = dn
        dG, dbias = gates_vjp(tuple(d_terms))
        dp_ref[:, M_OFF_G:M_OFF_G + LANES] = dG.astype(dp_ref.dtype)
        db_ref[...] += dbias

        @pl.when(pl.program_id(0) == NC - 1)
        def _():
            exchange.finish()

    return pl.pallas_call(
        body, name=name, grid=(NC,),
        in_specs=[ps, vec, hns, Cs, ns, ms, hv] + [any_spec] * nq,
        out_specs=[ps, vec, hns] + [any_spec] * nq,
        out_shape=[jax.ShapeDtypeStruct((S, M_PROJ), MXU_DTYPE), jax.ShapeDtypeStruct((1, LANES), F32),
                   jax.ShapeDtypeStruct((1, H * M_V), F32)] + _Exchange.out_shape(qs),
        scratch_shapes=list(_MLSTM_STATE[:2]) + _Exchange.semaphores(nq),
        compiler_params=_cparams(("arbitrary",)),
    )(proj, bias, hn, C_all, n_all, m_all, dgated, *qs)


A_NQ = A_QH * A_DH
A_NKV = 2 * A_KVH * A_DH
A_PAIRS = A_G // 2


def _sink_softmax_parts(s, sink):
    mx = jnp.maximum(jnp.max(s, axis=-1, keepdims=True), sink)
    e = jnp.exp(s - mx)
    e_sink = jnp.exp(sink - mx)
    r = 1.0 / (jnp.sum(e, axis=-1, keepdims=True) + e_sink)
    return e * r, e_sink * r


@jax.custom_vjp
def _sink_softmax(s, sink):
    return _sink_softmax_parts(s, sink)[0]


def _sink_softmax_fwd(s, sink):
    p, p_sink = _sink_softmax_parts(s, sink)
    return p, (p, p_sink)


def _sink_softmax_bwd(res, dp):
    p, p_sink = res
    t = jnp.sum(dp * p, axis=-1, keepdims=True)
    return p * (dp - t), -p_sink * t


_sink_softmax.defvjp(_sink_softmax_fwd, _sink_softmax_bwd)


def _band_bias():
    qi = np.arange(A_G * A_BLK)[:, None] % A_BLK
    ku = np.arange(2 * A_BLK)[None, :]
    diff = qi - (ku - A_BLK)
    band = (diff >= 0) & (diff < A_BLK)
    seen = np.stack([band, band & (ku >= A_BLK)])
    return jnp.asarray(np.where(seen, 0.0, -np.inf), F32)


_BAND_BIAS_SPEC = pl.BlockSpec((None, A_G * A_BLK, 2 * A_BLK), lambda n: (jnp.where(n == 0, 1, 0), 0, 0))


def _attn_group(Ps, KLO, KHI, VLO, VHI, sinks, bias):
    B = Ps[0].shape[0]
    R = len(Ps) * B
    q2 = jnp.concatenate(Ps, axis=0) * (A_DH ** -0.5)
    s = jnp.concatenate([_dot(q2, KLO, 1, 1), _dot(q2, KHI, 1, 1)], axis=0) + bias
    ri = lax.broadcasted_iota(jnp.int32, (2 * R, A_G), 0)
    head = 2 * lax.shift_right_logical(ri & (R - 1), B.bit_length() - 1) + lax.shift_right_logical(ri, R.bit_length() - 1)
    onehot = head == lax.broadcasted_iota(jnp.int32, (2 * R, A_G), 1)
    sink = jnp.sum(jnp.where(onehot, sinks, 0.0), axis=1, keepdims=True)
    p = _sink_softmax(s, sink)
    o = _dot(p[:R], VLO, 1, 0) + _dot(p[R:], VHI, 1, 0)
    return tuple(o[j * B:(j + 1) * B] for j in range(len(Ps)))


def _swap_halves_of_lanes(t):
    return pltpu.roll(t, LANES // 2, 1)


def _kv_operands(kvp_ref, kvc_ref, h):
    kk = jnp.concatenate([kvp_ref[:, :LANES], kvc_ref[:, :LANES]], axis=0)
    vv = jnp.concatenate([kvp_ref[:, LANES:], kvc_ref[:, LANES:]], axis=0)
    low = lax.broadcasted_iota(jnp.int32, kk.shape, 1) < A_DH
    own = low if h == 0 else jnp.logical_not(low)
    k_own = jnp.where(own, kk, 0.0)
    v_own = jnp.where(own, vv, 0.0)
    k_oth, v_oth = _swap_halves_of_lanes(k_own), _swap_halves_of_lanes(v_own)
    if h == 0:
        return own, k_own, k_oth, v_own, v_oth
    return own, k_oth, k_own, v_oth, v_own


def _pair_cols(h, j):
    c = (h * A_PAIRS + j) * LANES
    return slice(c, c + LANES)


def _attn_fwd(proj, sinks, *, name):
    S = proj.shape[0]
    NB = S // A_BLK
    kv_blk = A_NQ // A_NKV
    qs = pl.BlockSpec((A_BLK, A_NQ), lambda n: (n, 0))
    cur = pl.BlockSpec((A_BLK, A_NKV), lambda n: (n, kv_blk))
    prev = pl.BlockSpec((A_BLK, A_NKV), lambda n: (jnp.maximum(n - 1, 0), kv_blk))
    sk = pl.BlockSpec((A_KVH, A_G), lambda n: (0, 0))

    def body(q_ref, kvp_ref, kvc_ref, s_ref, bias_ref, o_ref):
        for h in range(A_KVH):
            _, KLO, KHI, VLO, VHI = _kv_operands(kvp_ref, kvc_ref, h)
            Ps = tuple(q_ref[:, _pair_cols(h, j)] for j in range(A_PAIRS))
            outs = _attn_group(Ps, KLO, KHI, VLO, VHI, s_ref[h:h + 1, :], bias_ref[...])
            for j in range(A_PAIRS):
                o_ref[:, _pair_cols(h, j)] = outs[j].astype(o_ref.dtype)

    return pl.pallas_call(
        body, name=name, grid=(NB,),
        in_specs=[qs, prev, cur, sk, _BAND_BIAS_SPEC], out_specs=qs,
        out_shape=jax.ShapeDtypeStruct((S, A_NQ), MXU_DTYPE),
        compiler_params=_cparams(("parallel",)),
    )(proj, proj, proj, sinks, _band_bias())


def _attn_bwd(proj, sinks, do, *, name):
    S = proj.shape[0]
    NB = S // A_BLK
    last = NB - 1
    kv_blk = A_NQ // A_NKV
    qs = pl.BlockSpec((A_BLK, A_NQ), lambda n: (jnp.minimum(n, last), 0))
    cur = pl.BlockSpec((A_BLK, A_NKV), lambda n: (jnp.minimum(n, last), kv_blk))
    prev = pl.BlockSpec((A_BLK, A_NKV), lambda n: (jnp.clip(n - 1, 0, last), kv_blk))
    sk = pl.BlockSpec((A_KVH, A_G), lambda n: (0, 0))
    lag = pl.BlockSpec((A_BLK, A_NKV), lambda n: (jnp.maximum(n - 1, 0), 0))
    cq_spec = pl.BlockSpec((1, A_NQ), lambda n: (0, 0))
    ckv_spec = pl.BlockSpec((1, A_NKV), lambda n: (0, 0))

    def body(q_ref, kvp_ref, kvc_ref, s_ref, do_ref, bias_ref, dq_ref, dkv_ref, ds_ref, cq_ref, ckv_ref, keep):
        n = pl.program_id(0)

        @pl.when(n == 0)
        def _():
            keep[...] = jnp.zeros_like(keep)
            ds_ref[...] = jnp.zeros_like(ds_ref)
            cq_ref[...] = jnp.zeros_like(cq_ref)
            ckv_ref[...] = jnp.zeros_like(ckv_ref)

        @pl.when(n < NB)
        def _():
            f = functools.partial(_attn_group, bias=bias_ref[...])
            dk = jnp.zeros((2 * A_BLK, LANES), F32)
            dv = jnp.zeros((2 * A_BLK, LANES), F32)
            ds_rows = []
            for h in range(A_KVH):
                own, KLO, KHI, VLO, VHI = _kv_operands(kvp_ref, kvc_ref, h)
                Ps = tuple(q_ref[:, _pair_cols(h, j)] for j in range(A_PAIRS))
                _, vjp = jax.vjp(f, Ps, KLO, KHI, VLO, VHI, s_ref[h:h + 1, :])
                dPs, dKLO, dKHI, dVLO, dVHI, dsk = vjp(
                    tuple(do_ref[:, _pair_cols(h, j)].astype(F32) for j in range(A_PAIRS)))
                for j in range(A_PAIRS):
                    dq_ref[:, _pair_cols(h, j)] = dPs[j].astype(dq_ref.dtype)
                    cq_ref[:, _pair_cols(h, j)] += jnp.sum(dPs[j], axis=0, keepdims=True)
                dk_own, dk_oth = (dKLO, dKHI) if h == 0 else (dKHI, dKLO)
                dv_own, dv_oth = (dVLO, dVHI) if h == 0 else (dVHI, dVLO)
                dk = dk + jnp.where(own, dk_own, 0.0) + _swap_halves_of_lanes(jnp.where(own, 0.0, dk_oth))
                dv = dv + jnp.where(own, dv_own, 0.0) + _swap_halves_of_lanes(jnp.where(own, 0.0, dv_oth))
                ds_rows.append(dsk)
            ds_ref[...] += jnp.concatenate(ds_rows, axis=0)
            dkv = jnp.concatenate([dk, dv], axis=1)
            done = keep[...] + dkv[:A_BLK]
            dkv_ref[...] = done.astype(dkv_ref.dtype)
            ckv_ref[...] += jnp.sum(done, axis=0, keepdims=True)
            keep[...] = dkv[A_BLK:]

        @pl.when(n == NB)
        def _():
            done = keep[...]
            dkv_ref[...] = done.astype(dkv_ref.dtype)
            ckv_ref[...] += jnp.sum(done, axis=0, keepdims=True)

    return pl.pallas_call(
        body, name=name, grid=(NB + 1,),
        in_specs=[qs, prev, cur, sk, qs, _BAND_BIAS_SPEC],
        out_specs=[qs, lag, sk, cq_spec, ckv_spec],
        out_shape=[jax.ShapeDtypeStruct((S, A_NQ), MXU_DTYPE), jax.ShapeDtypeStruct((S, A_NKV), MXU_DTYPE),
                   jax.ShapeDtypeStruct((A_KVH, A_G), F32), jax.ShapeDtypeStruct((1, A_NQ), F32),
                   jax.ShapeDtypeStruct((1, A_NKV), F32)],
        scratch_shapes=[pltpu.VMEM((A_BLK, A_NKV), F32)],
        compiler_params=_cparams(("arbitrary",)),
    )(proj, proj, proj, sinks, do, _band_bias())


HALO = 8


def _shift_rows(t, j):
    return pltpu.roll(t, j % t.shape[0], 0)


def _conv_taps(gate_ext):
    return _shift_rows(gate_ext, 2), _shift_rows(gate_ext, 1), gate_ext


def _conv_gate(g2, g1, g0, cw, cb):
    return cb + cw[0:1, :] * g2 + cw[1:2, :] * g1 + cw[2:3, :] * g0


def _convgate_fwd(u, cw, cb, *, name):
    S, F2 = u.shape
    F = F2 // 2
    tm = _pick(S, (512, 256, 128))
    hb = tm // HALO
    urow = pl.BlockSpec((tm, F2), lambda i: (i, 0))
    uprev = pl.BlockSpec((HALO, F), lambda i: (jnp.maximum(i * hb - 1, 0), 0))

    def body(u_ref, up_ref, cw_ref, cb_ref, a_ref):
        i = pl.program_id(0)
        prev = jnp.where(i > 0, up_ref[...], 0.0)
        gc = _conv_gate(*_conv_taps(jnp.concatenate([prev, u_ref[:, :F]], axis=0)), cw_ref[...], cb_ref[...])[HALO:]
        a_ref[...] = (gc * _sigmoid(gc) * u_ref[:, F:]).astype(a_ref.dtype)

    return pl.pallas_call(
        body, name=name, grid=(S // tm,),
        in_specs=[urow, uprev, pl.BlockSpec((3, F), lambda i: (0, 0)), pl.BlockSpec((1, F), lambda i: (0, 0))],
        out_specs=pl.BlockSpec((tm, F), lambda i: (i, 0)),
        out_shape=jax.ShapeDtypeStruct((S, F), MXU_DTYPE),
        compiler_params=_cparams(("parallel",)),
    )(u, u, cw, cb)


def _convgate_bwd(u, da, cw, cb, *, name):
    S, F2 = u.shape
    F = F2 // 2
    tm = _pick(S, (256, 128))
    hb = tm // HALO
    nt = S // tm
    nh = S // HALO
    urow = pl.BlockSpec((tm, F2), lambda i: (i, 0))
    uprev = pl.BlockSpec((HALO, F), lambda i: (jnp.maximum(i * hb - 1, 0), 0))
    unext = pl.BlockSpec((HALO, F2), lambda i: (jnp.minimum((i + 1) * hb, nh - 1), 0))
    darow = pl.BlockSpec((tm, F), lambda i: (i, 0))
    danext = pl.BlockSpec((HALO, F), lambda i: (jnp.minimum((i + 1) * hb, nh - 1), 0))

    def body(u_ref, up_ref, un_ref, da_ref, dan_ref, cw_ref, cb_ref, du_ref, dcw_ref, dcb_ref):
        i = pl.program_id(0)
        cwv = cw_ref[...]
        prev = jnp.where(i > 0, up_ref[...], 0.0)
        gate_ext = jnp.concatenate([prev, u_ref[:, :F], un_ref[:, :F]], axis=0)
        val_ext = jnp.concatenate([u_ref[:, F:], un_ref[:, F:]], axis=0)
        da_next = jnp.where(i < nt - 1, dan_ref[...].astype(F32), 0.0)
        da_ext = jnp.concatenate([da_ref[...].astype(F32), da_next], axis=0)
        g2, g1, g0 = (t[HALO:] for t in _conv_taps(gate_ext))
        gc = _conv_gate(g2, g1, g0, cwv, cb_ref[...])
        sg = _sigmoid(gc)
        silu = gc * sg
        dval = da_ext * silu
        dgc = da_ext * val_ext * (sg * (1.0 + gc * (1.0 - sg)))
        dgate = cwv[2:3, :] * dgc + cwv[1:2, :] * _shift_rows(dgc, -1) + cwv[0:1, :] * _shift_rows(dgc, -2)
        du_ref[:, :F] = dgate[:tm].astype(du_ref.dtype)
        du_ref[:, F:] = dval[:tm].astype(du_ref.dtype)
        dgc_c = dgc[:tm]
        dcw = jnp.concatenate([jnp.sum(dgc_c * g2[:tm], axis=0, keepdims=True),
                               jnp.sum(dgc_c * g1[:tm], axis=0, keepdims=True),
                               jnp.sum(dgc_c * g0[:tm], axis=0, keepdims=True)], axis=0)
        dcb = jnp.sum(dgc_c, axis=0, keepdims=True)

        @pl.when(i == 0)
        def _():
            dcw_ref[...] = dcw
            dcb_ref[...] = dcb

        @pl.when(i > 0)
        def _():
            dcw_ref[...] += dcw
            dcb_ref[...] += dcb

    return pl.pallas_call(
        body, name=name, grid=(nt,),
        in_specs=[urow, uprev, unext, darow, danext,
                  pl.BlockSpec((3, F), lambda i: (0, 0)), pl.BlockSpec((1, F), lambda i: (0, 0))],
        out_specs=[urow, pl.BlockSpec((3, F), lambda i: (0, 0)), pl.BlockSpec((1, F), lambda i: (0, 0))],
        out_shape=[jax.ShapeDtypeStruct((S, F2), MXU_DTYPE), jax.ShapeDtypeStruct((3, F), F32),
                   jax.ShapeDtypeStruct((1, F), F32)],
        compiler_params=_cparams(("arbitrary",)),
    )(u, u, u, da, da, cw, cb)


def _adamw_math(w, g, m, v):
    m = ADAM_B1 * m + (1.0 - ADAM_B1) * g
    v = ADAM_B2 * v + (1.0 - ADAM_B2) * (g * g)
    m_hat = m / (1.0 - ADAM_B1 ** ADAM_STEP)
    v_hat = v / (1.0 - ADAM_B2 ** ADAM_STEP)
    delta = -ADAM_LR * (m_hat / (jnp.sqrt(v_hat) + ADAM_EPS) + ADAM_WD * w)
    return delta, m, v


ADAMW_BLOCK_BYTES = 2 * 1024 * 1024


def _adamw_layers(w, gs, m, v, *, name):
    Lr, R, C = w.shape
    if C % LANES:
        flip = lambda t: jnp.swapaxes(t, -1, -2)
        return [flip(o) for o in _adamw_layers(flip(w), [flip(g) for g in gs], flip(m), flip(v), name=name)]
    tr = _pick(R, (256, 128, 64, 32, 16, 8))
    tc = C if tr * C * 4 <= ADAMW_BLOCK_BYTES else _pick(C, (256, 128))
    outs = None
    for layer, g in enumerate(gs):
        def body(w_ref, g_ref, m_ref, v_ref, *rest):
            go_ref, d_ref, nm_ref, nv_ref = rest[-4:]
            gv = g_ref[...]
            d, nm, nv = _adamw_math(w_ref[...], gv, m_ref[...], v_ref[...])
            go_ref[...] = gv
            d_ref[...] = d
            nm_ref[...] = nm
            nv_ref[...] = nv

        lay = pl.BlockSpec((None, tr, tc), lambda i, j, layer=layer: (layer, i, j))
        in_specs = [lay, pl.BlockSpec((tr, tc), lambda i, j: (i, j)), lay, lay]
        args = [w, g, m, v]
        aliases = {}
        if outs is not None:
            in_specs += [pl.BlockSpec(memory_space=pl.ANY)] * 4
            args += list(outs)
            aliases = {4 + t: t for t in range(4)}
        outs = pl.pallas_call(
            body, name=f"{name}_{layer}", grid=(R // tr, C // tc), in_specs=in_specs, out_specs=[lay] * 4,
            out_shape=[jax.ShapeDtypeStruct((Lr, R, C), F32)] * 4, input_output_aliases=aliases,
            compiler_params=_cparams(("parallel", "parallel")),
        )(*args)
    return outs


def _adamw_small(items, *, name):
    n = len(items)

    def body(*refs):
        ins, outs = refs[:4 * n], refs[4 * n:]
        for t in range(n):
            w, g, m, v = (r[...] for r in ins[4 * t:4 * t + 4])
            d, nm, nv = _adamw_math(w, g, m, v)
            outs[3 * t][...] = d
            outs[3 * t + 1][...] = nm
            outs[3 * t + 2][...] = nv

    flat = [a for it in items for a in it]
    out_shape = [jax.ShapeDtypeStruct(it[0].shape, F32) for it in items for _ in range(3)]
    vm = pl.BlockSpec(memory_space=pltpu.VMEM)
    res = pl.pallas_call(body, name=name, in_specs=[vm] * len(flat), out_specs=[vm] * len(out_shape),
                         out_shape=out_shape)(*flat)
    return [tuple(res[3 * t:3 * t + 3]) for t in range(n)]


def _place():
    return lax.axis_index("x"), lax.axis_index("y"), lax.axis_index("c")


_FLIPS = ((1, 0), (0, 1), (1, 1))


ROW_ALIGN = 16


def _half(rows, which):
    return pl.ds(pl.multiple_of(which * (rows // 2), ROW_ALIGN), rows // 2)


def _remote(src, dst, send_sems, recv_sems, k, to):
    return pltpu.make_async_remote_copy(src_ref=src, dst_ref=dst, send_sem=send_sems.at[k], recv_sem=recv_sems.at[k],
                                        device_id=to, device_id_type=MESH)


class _Gather:
    def __init__(self, w_refs, out_refs, send_sems, recv_sems):
        self.w_refs, self.out_refs, self.send_sems, self.recv_sems = w_refs, out_refs, send_sems, recv_sems
        self.pairs = [(i, j) for i in range(len(w_refs)) for j in range(3)]

    @staticmethod
    def out_shape(shards):
        return [jax.ShapeDtypeStruct((N_CHIPS,) + s.shape, s.dtype) for s in shards]

    @staticmethod
    def semaphores(n):
        return [pltpu.SemaphoreType.DMA((6 * n,)), pltpu.SemaphoreType.DMA((6 * n,))]

    def _where(self):
        x, y, c = _place()
        return x, y, c, [(x ^ fx, y ^ fy) for fx, fy in _FLIPS]

    def _over_ici(self, i, j, landed):
        x, y, c, chips = self._where()
        px, py = chips[j]
        mine = _half(self.w_refs[i].shape[0], c)
        if landed:
            src = dst = self.out_refs[i].at[2 * px + py, mine]
        else:
            src, dst = self.w_refs[i].at[mine], self.out_refs[i].at[2 * x + y, mine]
        return _remote(src, dst, self.send_sems, self.recv_sems, 6 * i + j, (px, py, c))

    def _over_d2d(self, i, j, which):
        x, y, c, chips = self._where()
        px, py = chips[j]
        blk = self.out_refs[i].at[2 * px + py, _half(self.w_refs[i].shape[0], which)]
        return _remote(blk, blk, self.send_sems, self.recv_sems, 6 * i + 3 + j, (x, y, 1 - c))

    def issue(self):
        for i, j in self.pairs:
            self._over_ici(i, j, False).start()

    def finish(self):
        c = lax.axis_index("c")
        for i, j in self.pairs:
            self._over_ici(i, j, True).wait_recv()
            self._over_d2d(i, j, c).start()
        for i, j in self.pairs:
            self._over_d2d(i, j, 1 - c).wait_recv()
        for i, j in self.pairs:
            self._over_ici(i, j, False).wait_send()
            self._over_d2d(i, j, c).wait_send()


def _swap_halves(gs, *, name):
    n = len(gs)
    any_spec = pl.BlockSpec(memory_space=pl.ANY)

    def body(*refs):
        swap = _Swap(refs[:n], refs[n:2 * n], refs[2 * n], refs[2 * n + 1])
        swap.issue()
        swap.finish()

    return pl.pallas_call(
        body, name=name, in_specs=[any_spec] * n, out_specs=[any_spec] * n,
        out_shape=_Swap.out_shape(gs), scratch_shapes=_Swap.semaphores(n),
    )(*gs)


class _Swap:
    def __init__(self, g_refs, out_refs, send_sems, recv_sems):
        self.g_refs, self.out_refs, self.send_sems, self.recv_sems = g_refs, out_refs, send_sems, recv_sems

    @staticmethod
    def out_shape(gs):
        return [jax.ShapeDtypeStruct((N_CHIPS, g.shape[1] // 2, g.shape[2]), g.dtype) for g in gs]

    @staticmethod
    def semaphores(n):
        return [pltpu.SemaphoreType.DMA((n,)), pltpu.SemaphoreType.DMA((n,))]

    def _copies(self):
        x, y, c = _place()
        return [_remote(g.at[:, _half(g.shape[1], 1 - c)], out, self.send_sems, self.recv_sems, i, (x, y, 1 - c))
                for i, (g, out) in enumerate(zip(self.g_refs, self.out_refs))]

    def issue(self):
        for cp in self._copies():
            cp.start()

    def finish(self):
        for cp in self._copies():
            cp.wait()


def _add_halves(g, got, c_arr, *, name):
    _, rows, cols = g.shape
    blk = (None, rows // 2, cols)

    def body(c_ref, g_ref, got_ref, o_ref):
        o_ref[...] = (g_ref[...] + got_ref[...]).astype(o_ref.dtype)

    return pl.pallas_call(
        body, name=name,
        grid_spec=pltpu.PrefetchScalarGridSpec(
            num_scalar_prefetch=1, grid=(N_CHIPS,),
            in_specs=[pl.BlockSpec(blk, lambda j, c_ref: (j, c_ref[0], 0)), pl.BlockSpec(blk, lambda j, c_ref: (j, 0, 0))],
            out_specs=pl.BlockSpec(blk, lambda j, c_ref: (j, 0, 0))),
        out_shape=jax.ShapeDtypeStruct((N_CHIPS, rows // 2, cols), WIRE_DTYPE),
        compiler_params=_cparams(("parallel",)),
    )(c_arr, g, got)


class _Exchange:
    def __init__(self, q_refs, out_refs, send_sems, recv_sems):
        self.q_refs, self.out_refs, self.send_sems, self.recv_sems = q_refs, out_refs, send_sems, recv_sems
        self.pairs = [(i, j) for i in range(len(q_refs)) for j in range(3)]

    @staticmethod
    def out_shape(qs):
        return [jax.ShapeDtypeStruct(q.shape, q.dtype) for q in qs]

    @staticmethod
    def semaphores(n):
        return [pltpu.SemaphoreType.DMA((3 * n,)), pltpu.SemaphoreType.DMA((3 * n,))]

    def _copy(self, i, j, landed):
        x, y, c = _place()
        px, py = [(x ^ fx, y ^ fy) for fx, fy in _FLIPS][j]
        if landed:
            src = dst = self.out_refs[i].at[2 * px + py]
        else:
            src, dst = self.q_refs[i].at[2 * px + py], self.out_refs[i].at[2 * x + y]
        return _remote(src, dst, self.send_sems, self.recv_sems, 3 * i + j, (px, py, c))

    def issue(self):
        for i, j in self.pairs:
            self._copy(i, j, False).start()

    def finish(self):
        for i, j in self.pairs:
            self._copy(i, j, True).wait_recv()
        for i, j in self.pairs:
            self._copy(i, j, False).wait_send()


def _sum_chips(q, r, place_arr, *, name):
    _, h, cols = q.shape
    blk = (None, h, cols)

    def body(p_ref, q_ref, r1_ref, r2_ref, r3_ref, o_ref):
        o_ref[...] = ((q_ref[...].astype(F32) + r1_ref[...].astype(F32)) + r2_ref[...].astype(F32)) + r3_ref[...].astype(F32)

    other = [pl.BlockSpec(blk, lambda i, p_ref, f=f: (p_ref[0] ^ f, 0, 0)) for f in (1, 2, 3)]
    return pl.pallas_call(
        body, name=name,
        grid_spec=pltpu.PrefetchScalarGridSpec(
            num_scalar_prefetch=1, grid=(1,),
            in_specs=[pl.BlockSpec(blk, lambda i, p_ref: (p_ref[0], 0, 0))] + other,
            out_specs=pl.BlockSpec((h, cols), lambda i, p_ref: (p_ref[1], 0))),
        out_shape=jax.ShapeDtypeStruct((2 * h, cols), F32),
        compiler_params=_cparams(("arbitrary",)),
    )(place_arr, q, r, r, r)


def _join_halves(fs, *, name):
    n = len(fs)
    any_spec = pl.BlockSpec(memory_space=pl.ANY)

    def body(*refs):
        out_refs, send_sems, recv_sems = refs[n:2 * n], refs[2 * n], refs[2 * n + 1]
        x, y, c = _place()
        sent = []
        for i in range(n):
            mine = out_refs[i].at[_half(fs[i].shape[0], c)]
            cp = _remote(mine, mine, send_sems, recv_sems, i, (x, y, 1 - c))
            cp.start()
            sent.append(cp)
        for i in range(n):
            its = out_refs[i].at[_half(fs[i].shape[0], 1 - c)]
            _remote(its, its, send_sems, recv_sems, i, (x, y, 1 - c)).wait_recv()
        for cp in sent:
            cp.wait_send()

    return pl.pallas_call(
        body, name=name, in_specs=[any_spec] * n, out_specs=[any_spec] * n,
        out_shape=[jax.ShapeDtypeStruct(f.shape, f.dtype) for f in fs], input_output_aliases={i: i for i in range(n)},
        scratch_shapes=[pltpu.SemaphoreType.DMA((n,)), pltpu.SemaphoreType.DMA((n,))],
    )(*fs)


def _allreduce_small(buf, *, name):
    R = buf.shape[0]
    vm = pl.BlockSpec(memory_space=pltpu.VMEM)

    def body(b_ref, o_ref, slots, send_sems, recv_sems):
        x, y, c = _place()
        me = 4 * x + 2 * y + c
        slots[me] = b_ref[...]
        sends = []
        for kk in range(1, 8):
            fx, fy, fc = (kk >> 2) & 1, (kk >> 1) & 1, kk & 1
            cp = pltpu.make_async_remote_copy(
                src_ref=b_ref, dst_ref=slots.at[me], send_sem=send_sems.at[kk - 1], recv_sem=recv_sems.at[kk - 1],
                device_id=(x ^ fx, y ^ fy, c ^ fc), device_id_type=MESH)
            cp.start()
            sends.append(cp)
        for kk in range(1, 8):
            fx, fy, fc = (kk >> 2) & 1, (kk >> 1) & 1, kk & 1
            peer = 4 * (x ^ fx) + 2 * (y ^ fy) + (c ^ fc)
            pltpu.make_async_remote_copy(
                src_ref=b_ref, dst_ref=slots.at[peer], send_sem=send_sems.at[kk - 1], recv_sem=recv_sems.at[kk - 1],
                device_id=(x ^ fx, y ^ fy, c ^ fc), device_id_type=MESH).wait_recv()
        for cp in sends:
            cp.wait_send()
        acc = slots[0]
        for d in range(1, 8):
            acc = acc + slots[d]
        o_ref[...] = acc

    return pl.pallas_call(
        body, name=name, in_specs=[vm], out_specs=vm,
        out_shape=jax.ShapeDtypeStruct((R, LANES), F32),
        scratch_shapes=[pltpu.VMEM((8, R, LANES), F32), pltpu.SemaphoreType.DMA((7,)), pltpu.SemaphoreType.DMA((7,))],
    )(buf)


def _pad_rows(v, mult=8 * LANES):
    flat = v.reshape(-1)
    n = flat.shape[0]
    tot = -(-n // mult) * mult
    return jnp.pad(flat, (0, tot - n)).reshape(-1, LANES)


def _pack_small(parts):
    return jnp.concatenate([_pad_rows(p.astype(F32)) for p in parts], axis=0)


def _unpack_small(buf, shapes):
    out, r = [], 0
    for sh in shapes:
        n = math.prod(sh)
        rows = -(-n // (8 * LANES)) * 8
        out.append(buf[r:r + rows].reshape(-1)[:n].reshape(sh))
        r += rows
    return out


def _cols_to_shards(w):
    *lead, K, N = w.shape
    t = w.reshape(*lead, K, N_CHIPS, N // N_CHIPS)
    return jnp.moveaxis(t, -2, 0)


def _shards_to_cols(t):
    t = jnp.moveaxis(t, 0, -2)
    *lead, K, _, n = t.shape
    return t.reshape(*lead, K, N_CHIPS * n)


_BIG = ("m_w_in", "m_w_out", "a_w_in", "a_w_out", "f_w_up", "f_w_down")


def kernel(x, m_w_in, m_gate_bias, m_head_norm, m_w_out, a_w_in, a_b_in, a_sinks, a_w_out, a_b_out, norm_mix_pre, norm_mix_post, norm_ffn_pre, norm_ffn_post, f_w_up, f_conv_w, f_conv_b, f_w_down, loss_target, m_m_w_in, m_m_gate_bias, m_m_head_norm, m_m_w_out, m_a_w_in, m_a_b_in, m_a_sinks, m_a_w_out, m_a_b_out, m_norm_mix_pre, m_norm_mix_post, m_norm_ffn_pre, m_norm_ffn_post, m_f_w_up, m_f_conv_w, m_f_conv_b, m_f_w_down, v_m_w_in, v_m_gate_bias, v_m_head_norm, v_m_w_out, v_a_w_in, v_a_b_in, v_a_sinks, v_a_w_out, v_a_b_out, v_norm_mix_pre, v_norm_mix_post, v_norm_ffn_pre, v_norm_ffn_post, v_f_w_up, v_f_conv_w, v_f_conv_b, v_f_w_down):
    params = dict(m_w_in=m_w_in, m_gate_bias=m_gate_bias, m_head_norm=m_head_norm, m_w_out=m_w_out, a_w_in=a_w_in,
                  a_b_in=a_b_in, a_sinks=a_sinks, a_w_out=a_w_out, a_b_out=a_b_out, norm_mix_pre=norm_mix_pre,
                  norm_mix_post=norm_mix_post, norm_ffn_pre=norm_ffn_pre, norm_ffn_post=norm_ffn_post, f_w_up=f_w_up,
                  f_conv_w=f_conv_w, f_conv_b=f_conv_b, f_w_down=f_w_down)
    mom1 = dict(m_w_in=m_m_w_in, m_gate_bias=m_m_gate_bias, m_head_norm=m_m_head_norm, m_w_out=m_m_w_out,
                a_w_in=m_a_w_in, a_b_in=m_a_b_in, a_sinks=m_a_sinks, a_w_out=m_a_w_out, a_b_out=m_a_b_out,
                norm_mix_pre=m_norm_mix_pre, norm_mix_post=m_norm_mix_post, norm_ffn_pre=m_norm_ffn_pre,
                norm_ffn_post=m_norm_ffn_post, f_w_up=m_f_w_up, f_conv_w=m_f_conv_w, f_conv_b=m_f_conv_b,
                f_w_down=m_f_w_down)
    mom2 = dict(m_w_in=v_m_w_in, m_gate_bias=v_m_gate_bias, m_head_norm=v_m_head_norm, m_w_out=v_m_w_out,
                a_w_in=v_a_w_in, a_b_in=v_a_b_in, a_sinks=v_a_sinks, a_w_out=v_a_w_out, a_b_out=v_a_b_out,
                norm_mix_pre=v_norm_mix_pre, norm_mix_post=v_norm_mix_post, norm_ffn_pre=v_norm_ffn_pre,
                norm_ffn_post=v_norm_ffn_post, f_w_up=v_f_w_up, f_conv_w=v_f_conv_w, f_conv_b=v_f_conv_b,
                f_w_down=v_f_w_down)
    order = list(params)

    mx, my, mc = _place()
    chip = 2 * mx + my
    h0 = x[0]
    target = loss_target[0]

    def two_d(t):
        return t.reshape(-1, t.shape[-1])

    early = ("m_w_in",)
    late = tuple(n for n in _BIG if n not in early)
    mine = {n: two_d(params[n]).astype(MXU_DTYPE) for n in _BIG}

    def with_own_slot(names, theirs):
        return {n: lax.dynamic_update_slice(t, mine[n][None], (chip, 0, 0)) for n, t in zip(names, theirs)}

    z0, *early_theirs = _rms_fwd(h0, norm_mix_pre[0:1], [mine[n] for n in early], out_dtype=MXU_DTYPE,
                                 name="mix_pre_norm0")
    gathered = with_own_slot(early, early_theirs)

    def in_place(shard, axis):
        width = shard.shape[axis]
        z = jnp.zeros(shard.shape[:axis] + (N_CHIPS * width,) + shard.shape[axis + 1:], F32)
        contrib = jnp.where(mc == 0, shard, 0.0)
        return lax.dynamic_update_slice_in_dim(z, contrib, chip * width, axis)

    sm_in = [in_place(a_b_in, 1), in_place(a_b_out, 1), in_place(f_conv_w, 2)]
    sm_full = _unpack_small(_allreduce_small(_pack_small(sm_in), name="gather_small"), [t.shape for t in sm_in])
    b_in_full, b_out_full, conv_w_full = sm_full

    W_in = _shards_to_cols(gathered["m_w_in"])
    W_all = jnp.pad(W_in, ((0, 0), (0, M_PROJ - W_in.shape[1])))
    gbias = jnp.pad(m_gate_bias[0].reshape(1, 2 * M_HEADS), ((0, 0), (0, LANES - 2 * M_HEADS)))

    grads = {}

    def ffn_fwd(i, z1, end):
        u = _mm(z1, W_up, b_layer=i, name=f"ffn_up{i}")
        a = _convgate_fwd(u, conv_w_full[i], f_conv_b[i:i + 1], name=f"ffn_act{i}")
        return u, a, _mm(a, W_down[i], rows=end, name=f"ffn_down{i}")

    proj = _mm(z0, W_all, name="mlstm_proj")
    gated, C_all, n_all, m_all, *late_theirs = _mlstm_fwd(proj, gbias, m_head_norm, [mine[n] for n in late],
                                                          name="mlstm_fwd")
    gathered.update(with_own_slot(late, late_theirs))
    W_mout = gathered["m_w_out"].reshape(D_MODEL, D_MODEL)
    A_in, A_out = _shards_to_cols(gathered["a_w_in"]), gathered["a_w_out"].reshape(D_MODEL, D_MODEL)
    W_up = gathered["f_w_up"]
    dsh = D_FF // N_CHIPS
    W_down = [gathered["f_w_down"][:, i * dsh:(i + 1) * dsh].reshape(D_FF, D_MODEL) for i in range(2)]
    zm0, h1, z1 = _mm(gated, W_mout, rows=_rows_sublayer_end(h0, norm_mix_post[0:1], norm_ffn_pre[0:1]), name="mlstm_out")
    u0, a0, (zf0, h2, y0) = ffn_fwd(0, z1, _rows_sublayer_end(h1, norm_ffn_post[0:1], norm_mix_pre[1:2]))

    aproj = _mm(y0, A_in, bias=b_in_full, name="attn_proj")
    sinks = a_sinks.reshape(A_KVH, A_G)
    ao = _attn_fwd(aproj, sinks, name="attn_fwd")
    zm1, h3, z3 = _mm(ao, A_out, bias=b_out_full, rows=_rows_sublayer_end(h2, norm_mix_post[1:2], norm_ffn_pre[1:2]),
                      name="attn_out")
    u1, a1, (zf1, dh, dzf1, g_post1, loss_lanes) = ffn_fwd(1, z3, _rows_last_sublayer_end(h3, norm_ffn_post[1:2], target))
    loss = lax.psum(loss_lanes[0, 0], ("x", "y", "c"))

    g_post, g_fpre, g_mpost, g_mpre = [None, g_post1], [None, None], [None, None], [None, None]
    dW_up, dW_down, dconv_w, dconv_b = [None, None], [None, None], [None, None], [None, None]

    def by_rows(g):
        return g.reshape(N_CHIPS, -1, g.shape[-1])

    def ffn_bwd(i, dzf, z1_, u, a, start, ride):
        da = _mm(dzf, W_down[i], tb=True, name=f"ffn_down_dx{i}")
        dW_down[i] = _mm(a, dzf, ta=True, name=f"ffn_down_dw{i}")
        du, dconv_w[i], dconv_b[i] = _convgate_bwd(u, da, conv_w_full[i], f_conv_b[i:i + 1], name=f"ffn_act_bwd{i}")
        dW_up[i] = _mm(z1_, du, ta=True, out_stacked=True, name=f"ffn_up_dw{i}")
        riders = (list(ride) + [by_rows(dW_down[i]), dW_up[i]]) if ride is not None else []
        outs = _mm(du, W_up, tb=True, b_layer=i, rows=start, ride=(_Swap, riders) if riders else None,
                   name=f"ffn_up_dx{i}")
        return outs[:len(start.outs)], riders, outs[len(start.outs):]

    (dh3, g_fpre[1], dzm1, g_mpost[1], db_out), _, _ = ffn_bwd(
        1, dzf1, z3, u1, a1, _rows_sublayer_start_bwd(h3, dh, zm1, norm_ffn_pre[1:2], norm_mix_post[1:2]), None)
    dao = _mm(dzm1, A_out, tb=True, name="attn_out_dx")
    dA_out = _mm(ao, dzm1, ta=True, name="attn_out_dw")
    daq, dakv, dsinks, cs_q, cs_kv = _attn_bwd(aproj, sinks, dao, name="attn_bwd")
    db_in = jnp.concatenate([cs_q, cs_kv], axis=1)
    dh2, g_mpre[1], dzf0, g_post[0], _ = _mm(
        daq, A_in[:, :A_NQ], tb=True, add=_mm(dakv, A_in[:, A_NQ:], tb=True, name="attn_proj_kv_dx"),
        rows=_rows_sublayer_start_bwd(h2, dh3, zf0, norm_mix_pre[1:2], norm_ffn_post[0:1]), name="attn_proj_q_dx")
    dA_in = jnp.concatenate([_mm(y0, daq, ta=True, name="attn_proj_q_dw"),
                             _mm(y0, dakv, ta=True, name="attn_proj_kv_dw")], axis=1)

    c_arr = jnp.reshape(mc, (1,)).astype(jnp.int32)
    place_arr = jnp.stack([chip, mc]).astype(jnp.int32)

    def add_halves(tags, gs, got):
        return [_add_halves(g, t, c_arr, name=f"grad_add_halves_{tag}") for tag, g, t in zip(tags, gs, got)]

    (dh1, g_fpre[0], dzm0, g_mpost[0], _), rode, rode_got = ffn_bwd(
        0, dzf0, z1, u0, a0, _rows_sublayer_start_bwd(h1, dh2, zm0, norm_ffn_pre[0:1], norm_mix_post[0:1]),
        [_cols_to_shards(dA_in), by_rows(dA_out), dW_up[1], by_rows(dW_down[1])])
    rode_tags = ["a_w_in", "a_w_out", "f_w_up1", "f_w_down1", "f_w_down0", "f_w_up0"]
    dgated = _mm(dzm0, W_mout, tb=True, name="mlstm_out_dx")
    dW_mout = _mm(gated, dzm0, ta=True, name="mlstm_out_dw")
    next_tags, next_gs = ["m_w_out"], [by_rows(dW_mout)]
    late_tags = rode_tags + next_tags
    late_part = (add_halves(rode_tags, rode, rode_got)
                 + add_halves(next_tags, next_gs, _swap_halves(next_gs, name="grad_swap_halves_late")))
    dproj, dgbias, dhn, *late_from = _mlstm_bwd(proj, gbias, m_head_norm, C_all, n_all, m_all, dgated, late_part,
                                                name="mlstm_bwd")
    dW_all = _mm(z0, dproj, ta=True, name="mlstm_proj_dw")
    dW_min = dW_all[:, :m_w_in.shape[-1] * N_CHIPS]

    early_tags, early_gs = ["m_w_in"], [_cols_to_shards(dW_min)]
    early_part = add_halves(early_tags, early_gs, _swap_halves(early_gs, name="grad_swap_halves_early"))
    grad_x, g_mpre[0], *early_from = _mm(
        dproj, W_all, tb=True, rows=_rows_first_sublayer_start_bwd(h0, dh1, norm_mix_pre[0:1]),
        ride=(_Exchange, early_part),
        name="mlstm_proj_dx")
    tags = early_tags + late_tags
    halves = [_sum_chips(q, r, place_arr, name=f"grad_sum_chips_{tag}")
              for tag, q, r in zip(tags, early_part + late_part, list(early_from) + list(late_from))]
    reduced = dict(zip(tags, _join_halves(halves, name="grad_join_halves")))
    layer_grads = dict(m_w_in=[reduced["m_w_in"]], m_w_out=[reduced["m_w_out"]], a_w_in=[reduced["a_w_in"]],
                       a_w_out=[reduced["a_w_out"]], f_w_up=[reduced["f_w_up0"], reduced["f_w_up1"]],
                       f_w_down=[reduced["f_w_down0"], reduced["f_w_down1"]])

    small_g = [
        dgbias[:, :2 * M_HEADS].reshape(1, 2, M_HEADS),
        dhn,
        dsinks.reshape(1, A_QH),
        db_in, db_out,
        jnp.concatenate(g_mpre), jnp.concatenate(g_mpost), jnp.concatenate(g_fpre), jnp.concatenate(g_post),
        jnp.stack(dconv_w), jnp.concatenate(dconv_b),
    ]
    small_names = ["m_gate_bias", "m_head_norm", "a_sinks", "a_b_in", "a_b_out", "norm_mix_pre", "norm_mix_post",
                   "norm_ffn_pre", "norm_ffn_post", "f_conv_w", "f_conv_b"]
    red = _unpack_small(_allreduce_small(_pack_small(small_g), name="reduce_small"), [t.shape for t in small_g])
    for n, t in zip(small_names, red):
        if n in ("a_b_in", "a_b_out", "f_conv_w"):
            axis = t.ndim - 1
            width = params[n].shape[axis]
            t = lax.dynamic_slice_in_dim(t, chip * width, width, axis)
        grads[n] = t

    deltas, new_m, new_v = {}, {}, {}
    for n in _BIG:
        grads[n], deltas[n], new_m[n], new_v[n] = _adamw_layers(params[n], layer_grads[n], mom1[n], mom2[n],
                                                                name=f"adamw_{n}")
    two = lambda t: t.reshape(-1, t.shape[-1])
    res = _adamw_small([(two(params[n]), two(grads[n]), two(mom1[n]), two(mom2[n])) for n in small_names],
                       name="adamw_small")
    for n, (d, nm, nv) in zip(small_names, res):
        sh = params[n].shape
        deltas[n], new_m[n], new_v[n] = d.reshape(sh), nm.reshape(sh), nv.reshape(sh)

    return (loss, grad_x[None], *[grads[n] for n in order], *[deltas[n] for n in order],
            *[new_m[n] for n in order], *[new_v[n] for n in order])
```

```python
import functools
import math

import numpy as np

import jax
import jax.numpy as jnp
from jax import lax
from jax.experimental import pallas as pl
from jax.experimental.pallas import tpu as pltpu

F32 = jnp.float32
MXU_DTYPE = jnp.bfloat16
WIRE_DTYPE = jnp.bfloat16
MESH = pl.DeviceIdType.MESH

D_MODEL = 1024
EPS = 1e-6
M_HEADS, M_QK, M_V, M_CHUNK = 8, 64, 128, 128
GATE_CAP = 15.0
A_DH, A_QH, A_KVH, A_G, A_BLK = 64, 16, 2, 8, 128
D_FF = 2816
N_CHIPS = 4
LANES = 128
VMEM_LIMIT = 56 * 1024 * 1024

ADAM_LR, ADAM_B1, ADAM_B2, ADAM_EPS, ADAM_WD, ADAM_STEP = 0.001, 0.9, 0.999, 1e-08, 0.01, 10


def _cparams(sem):
    return pltpu.CompilerParams(dimension_semantics=sem, vmem_limit_bytes=VMEM_LIMIT)


def _pick(n, cands):
    for c in cands:
        if n % c == 0:
            return c
    return n


class _Rows:
    def __init__(self, tiles, vecs, outs, fn):
        self.tiles, self.vecs, self.outs, self.fn = list(tiles), list(vecs), list(outs), fn


MM_VMEM_BUDGET = 46 * 1024 * 1024
ROWS_FULL_K = 4224
WIDE_N = 3200


def _mm(a, b, *, ta=False, tb=False, out_dtype=F32, bias=None, add=None, b_layer=None, out_stacked=False, rows=None,
        ride=None, name):
    if ta:
        K, M = a.shape
    else:
        M, K = a.shape
    b_stacked = b_layer is not None
    if b_stacked:
        assert not ta
        n_sh = b.shape[2]
        w_rows, w_cols = D_MODEL, N_CHIPS * n_sh
        N, Kb = (w_rows, w_cols) if tb else (w_cols, w_rows)
    elif tb:
        N, Kb = b.shape
    else:
        Kb, N = b.shape
    assert K == Kb, (a.shape, b.shape)
    tn = _pick(N, (1024, 1408, 1280, 640, 512, 256, 128))
    tk = K if K <= 2816 else _pick(K, (2816, 2048, 1408, 1024, 512, 256, 128))
    if tn < 1024 <= N <= WIDE_N:
        tn, tk = N, min(tk, 1024)
    shards_per_step = 1
    if b_stacked and tb:
        shards_per_step = N_CHIPS if rows is not None else 2
        tk = shards_per_step * n_sh
    if (b_stacked and not tb) or out_stacked:
        tn = N // N_CHIPS
    if rows is not None:
        assert tn == N and not out_stacked and not ta
        if K <= ROWS_FULL_K:
            tk = K
    nk = K // tk

    def vmem_bytes(rows_per_tile):
        total = 2 * (rows_per_tile * tk * a.dtype.itemsize + tk * tn * b.dtype.itemsize)
        if rows is None:
            total += 2 * rows_per_tile * tn * jnp.dtype(out_dtype).itemsize
        else:
            total += 2 * rows_per_tile * N * (4 * len(rows.tiles)
                                              + sum(jnp.dtype(dt).itemsize for kind, dt in rows.outs if kind == "tile"))
        if add is not None:
            total += 2 * rows_per_tile * tn * 4
        return total + (rows_per_tile * tn * 4 if nk > 1 else 0)

    tm = next(c for c in (2048, 1024, 1408, 512, 256, 128, M) if M % c == 0 and (vmem_bytes(c) <= MM_VMEM_BUDGET or c <= 128))
    dn = (((0 if ta else 1,), (1 if tb else 0,)), ((), ()))
    has_bias, has_add = bias is not None, add is not None
    n_tiles, n_vecs, n_outs = (len(rows.tiles), len(rows.vecs), len(rows.outs)) if rows is not None else (0, 0, 1)
    rider, riding = ride if ride is not None else (None, ())
    n_ex = len(riding)
    assert not n_ex or rows is not None
    grid = (M // tm, N // tn, nk)

    def body(*refs):
        a_ref, b_ref = refs[0], refs[1]
        pos = 2
        bias_ref = add_ref = None
        if has_bias:
            bias_ref = refs[pos]
            pos += 1
        if has_add:
            add_ref = refs[pos]
            pos += 1
        tile_refs, vec_refs = refs[pos:pos + n_tiles], refs[pos + n_tiles:pos + n_tiles + n_vecs]
        pos += n_tiles + n_vecs
        ex_in = refs[pos:pos + n_ex]
        pos += n_ex
        out_refs, ex_out = refs[pos:pos + n_outs], refs[pos + n_outs:pos + n_outs + n_ex]
        pos += n_outs + n_ex
        acc_ref = refs[pos] if nk > 1 else None
        if n_ex:
            ex = rider(ex_in, ex_out, *refs[pos + (nk > 1):])
            step = (pl.program_id(0) * grid[1] + pl.program_id(1)) * grid[2] + pl.program_id(2)

            @pl.when(step == 0)
            def _():
                ex.issue()

        def finish(r):
            if has_bias:
                r = r + bias_ref[...]
            if has_add:
                r = r + add_ref[...]
            if rows is None:
                out_refs[0][...] = r.astype(out_dtype)
                return
            vals = rows.fn(r, [t[...] for t in tile_refs], [v[...] for v in vec_refs])
            first = pl.program_id(0) == 0
            for (kind, _), o_ref, val in zip(rows.outs, out_refs, vals):
                if kind == "tile":
                    o_ref[...] = val.astype(o_ref.dtype)
                else:
                    @pl.when(first)
                    def _(o_ref=o_ref, val=val):
                        o_ref[...] = val

                    @pl.when(jnp.logical_not(first))
                    def _(o_ref=o_ref, val=val):
                        o_ref[...] += val

        if shards_per_step > 1:
            part = sum(lax.dot_general(a_ref[:, s * n_sh:(s + 1) * n_sh], b_ref[s], dn, preferred_element_type=F32)
                       for s in range(shards_per_step))
        else:
            part = lax.dot_general(a_ref[...], b_ref[...], dn, preferred_element_type=F32)
        if nk == 1:
            finish(part)
        else:
            k = pl.program_id(2)

            @pl.when((pl.program_id(0) == 0) & (pl.program_id(1) == 0) & (k == 0))
            def _():
                acc_ref[...] = jnp.zeros_like(acc_ref)

            acc_ref[...] = part + jnp.where(k == 0, 0.0, acc_ref[...])

            @pl.when(k == nk - 1)
            def _():
                finish(acc_ref[...])

        if n_ex:
            @pl.when(step == grid[0] * grid[1] * grid[2] - 1)
            def _():
                ex.finish()

    a_spec = pl.BlockSpec((tk, tm), lambda i, j, k: (k, i)) if ta else pl.BlockSpec((tm, tk), lambda i, j, k: (i, k))
    if b_stacked and tb:
        off = b_layer * (w_rows // tn)
        b_spec = pl.BlockSpec((shards_per_step, tn, n_sh), lambda i, j, k: (k, off + j, 0))
    elif b_stacked:
        off = b_layer * (w_rows // tk)
        b_spec = pl.BlockSpec((None, tk, tn), lambda i, j, k: (j, off + k, 0))
    elif tb:
        b_spec = pl.BlockSpec((tn, tk), lambda i, j, k: (j, k))
    else:
        b_spec = pl.BlockSpec((tk, tn), lambda i, j, k: (k, j))
    if out_stacked:
        out_spec = pl.BlockSpec((None, tm, tn), lambda i, j, k: (j, i, 0))
        out_shape = jax.ShapeDtypeStruct((N_CHIPS, M, tn), out_dtype)
    else:
        out_spec = pl.BlockSpec((tm, tn), lambda i, j, k: (i, j))
        out_shape = jax.ShapeDtypeStruct((M, N), out_dtype)
    in_specs, args = [a_spec, b_spec], [a, b]
    if has_bias:
        in_specs.append(pl.BlockSpec((1, tn), lambda i, j, k: (0, j)))
        args.append(bias)
    if has_add:
        in_specs.append(pl.BlockSpec((tm, tn), lambda i, j, k: (i, j)))
        args.append(add)
    semantics = ("parallel", "parallel", "arbitrary")
    if rows is not None:
        tile_spec = pl.BlockSpec((tm, N), lambda i, j, k: (i, 0))
        vec_spec = pl.BlockSpec((1, N), lambda i, j, k: (0, 0))
        in_specs += [tile_spec] * n_tiles + [vec_spec] * n_vecs
        args += rows.tiles + rows.vecs
        out_spec = [tile_spec if kind == "tile" else vec_spec for kind, _ in rows.outs]
        out_shape = [jax.ShapeDtypeStruct((M, N) if kind == "tile" else (1, N), dt) for kind, dt in rows.outs]
        semantics = ("arbitrary", "arbitrary", "arbitrary")
    scratch = [pltpu.VMEM((tm, tn), F32)] if nk > 1 else []
    if n_ex:
        any_spec = pl.BlockSpec(memory_space=pl.ANY)
        in_specs += [any_spec] * n_ex
        args += list(riding)
        out_spec = out_spec + [any_spec] * n_ex
        out_shape = out_shape + rider.out_shape(riding)
        scratch += rider.semaphores(n_ex)
    return pl.pallas_call(
        body, name=name,
        grid=(M // tm, N // tn, nk),
        in_specs=in_specs,
        out_specs=out_spec,
        out_shape=out_shape,
        scratch_shapes=scratch,
        compiler_params=_cparams(semantics),
    )(*args)


ROW_TILE = 512


def _rms_fwd(x, g, shards, *, out_dtype, name):
    S, D = x.shape
    tm = _pick(S, (ROW_TILE, 256, 128))
    nsh, nt = len(shards), S // tm
    any_spec = pl.BlockSpec(memory_space=pl.ANY)

    def body(x_ref, g_ref, *rest):
        o_ref = rest[nsh]
        gather = _Gather(rest[:nsh], rest[nsh + 1:2 * nsh + 1], *rest[2 * nsh + 1:])

        @pl.when(pl.program_id(0) == 0)
        def _():
            gather.issue()

        o_ref[...] = _rms(x_ref[...], g_ref[...]).astype(out_dtype)

        @pl.when(pl.program_id(0) == nt - 1)
        def _():
            gather.finish()

    row = pl.BlockSpec((tm, D), lambda i: (i, 0))
    return pl.pallas_call(
        body, name=name, grid=(nt,),
        in_specs=[row, pl.BlockSpec((1, D), lambda i: (0, 0))] + [any_spec] * nsh, out_specs=[row] + [any_spec] * nsh,
        out_shape=[jax.ShapeDtypeStruct((S, D), out_dtype)] + _Gather.out_shape(shards),
        scratch_shapes=_Gather.semaphores(nsh),
        compiler_params=_cparams(("arbitrary",)),
    )(x, g, *shards)


def _rms(x, g):
    return x * lax.rsqrt(jnp.mean(x * x, axis=-1, keepdims=True) + EPS) * g


def _rms_vjp(x, g, dy):
    r = lax.rsqrt(jnp.mean(x * x, axis=-1, keepdims=True) + EPS)
    xh = x * r
    gd = dy * g
    dx = r * (gd - xh * jnp.mean(gd * xh, axis=-1, keepdims=True))
    return dx, jnp.sum(dy * xh, axis=0, keepdims=True)


def _rows_sublayer_end(res, g_post, g_pre_next):
    def fn(z, tiles, vecs):
        h = tiles[0] + _rms(z, vecs[0])
        return [z, h, _rms(h, vecs[1])]

    return _Rows([res], [g_post, g_pre_next], [("tile", F32), ("tile", F32), ("tile", MXU_DTYPE)], fn)


def _rows_last_sublayer_end(res, g_post, target):
    def fn(z, tiles, vecs):
        y = tiles[0] + _rms(z, vecs[0])
        err = y - tiles[1]
        dy = err * (1.0 / err.shape[-1])
        dz, dg = _rms_vjp(z, vecs[0], dy)
        loss = 0.5 * jnp.sum(jnp.mean(err * err, axis=-1, keepdims=True), axis=0, keepdims=True)
        return [z, dy, dz, dg, jnp.broadcast_to(loss, dg.shape)]

    return _Rows([res, target], [g_post], [("tile", F32), ("tile", F32), ("tile", MXU_DTYPE), ("vec", F32), ("vec", F32)], fn)


def _rows_sublayer_start_bwd(x, res, z_below, g_pre, g_post_below):
    def fn(dz, tiles, vecs):
        dx, dg_pre = _rms_vjp(tiles[0], vecs[0], dz)
        dh = tiles[1] + dx
        dzb, dg_post = _rms_vjp(tiles[2], vecs[1], dh)
        return [dh, dg_pre, dzb, dg_post, jnp.sum(dzb, axis=0, keepdims=True)]

    return _Rows([x, res, z_below], [g_pre, g_post_below],
                 [("tile", F32), ("vec", F32), ("tile", MXU_DTYPE), ("vec", F32), ("vec", F32)], fn)


def _rows_first_sublayer_start_bwd(x, res, g_pre):
    def fn(dz, tiles, vecs):
        dx, dg_pre = _rms_vjp(tiles[0], vecs[0], dz)
        return [tiles[1] + dx, dg_pre]

    return _Rows([x, res], [g_pre], [("tile", F32), ("vec", F32)], fn)


def _mx(t):
    return t.astype(MXU_DTYPE)


def _mxf(t):
    return t.astype(MXU_DTYPE).astype(F32)


def _rawdot(a, b, ca, cb):
    return lax.dot_general(_mx(a), _mx(b), (((ca,), (cb,)), ((), ())), preferred_element_type=F32)


@functools.partial(jax.custom_vjp, nondiff_argnums=(2, 3))
def _dot(a, b, ca, cb):
    return _rawdot(a, b, ca, cb)


def _dot_fwd(a, b, ca, cb):
    return _rawdot(a, b, ca, cb), (a, b)


def _dot_bwd(ca, cb, res, g):
    a, b = res
    bj = 1 if cb == 0 else 0
    ai = 0 if ca == 1 else 1
    da = _rawdot(g, b, 1, bj) if ca == 1 else _rawdot(b, g, bj, 1)
    db = _rawdot(a, g, ai, 0) if cb == 0 else _rawdot(g, a, 0, ai)
    return da.astype(a.dtype), db.astype(b.dtype)


_dot.defvjp(_dot_fwd, _dot_bwd)


def _softcap(z):
    return GATE_CAP * jnp.tanh(z / GATE_CAP)


def _log_sigmoid(z):
    return jnp.minimum(z, 0.0) - jnp.log(1.0 + jnp.exp(-jnp.abs(z)))


def _sigmoid(z):
    return 0.5 * jnp.tanh(0.5 * z) + 0.5


def _lane_col(t, lane_index):
    lane = lax.broadcasted_iota(jnp.int32, t.shape, 1)
    return jnp.sum(jnp.where(lane == lane_index, t, 0.0), axis=1, keepdims=True)


def _mlstm_gates(G, bias):
    L = G.shape[0]
    z = _softcap(G + bias)
    ig = z
    lf = _log_sigmoid(z)
    ti = lax.broadcasted_iota(jnp.int32, (L, L), 0)
    si = lax.broadcasted_iota(jnp.int32, (L, L), 1)
    tril = (si <= ti).astype(F32)
    b = lax.dot_general(tril, lf, (((1,), (0,)), ((), ())), precision=lax.Precision.HIGHEST, preferred_element_type=F32)
    bL = jnp.sum(lf, axis=0, keepdims=True)
    return ig, b, ig.T, b.T, bL


def _mlstm_head(h, q, k_pair, v, o, ig_all, b_all, igT, bT, bL_all, hn, C, n, m):
    L = q.shape[0]
    k = jnp.where(_own_lanes(h, k_pair.shape), k_pair, 0.0)
    ti = lax.broadcasted_iota(jnp.int32, (L, L), 0)
    si = lax.broadcasted_iota(jnp.int32, (L, L), 1)
    lower = (si <= ti)
    ig = _lane_col(ig_all, h)
    b = _lane_col(b_all, M_HEADS + h)
    ig_row = igT[h:h + 1, :]
    b_row = bT[M_HEADS + h:M_HEADS + h + 1, :]
    bL = _lane_col(bL_all, M_HEADS + h)
    inter = b + m
    dlog = jnp.where(lower, b - b_row + ig_row, -jnp.inf)
    m_t = lax.stop_gradient(jnp.maximum(inter, jnp.max(dlog, axis=-1, keepdims=True)))
    qs = q * (M_QK ** -0.5)
    w = _dot(qs, k, 1, 1) * jnp.exp(dlog - m_t)
    s_inter = jnp.exp(inter - m_t)
    num = _dot(w, v, 1, 0) + s_inter * _dot(qs, C, 1, 0)
    den = jnp.sum(w, axis=-1, keepdims=True) + s_inter * jnp.sum(_mxf(qs) * _mxf(n), axis=-1, keepdims=True)
    hout = num * (1.0 / jnp.maximum(jnp.abs(den), jnp.exp(-m_t)))
    tail = bL - b + ig
    m_new = lax.stop_gradient(jnp.maximum(bL + m, jnp.max(tail, axis=0, keepdims=True)))
    ws = jnp.exp(tail - m_new)
    decay = jnp.exp(bL + m - m_new)
    wk = ws * k
    C_new = decay * C + _dot(wk, v, 0, 0)
    n_new = decay * n + jnp.sum(_mxf(ws) * _mxf(k), axis=0, keepdims=True)
    hs = hout * lax.rsqrt(jnp.mean(hout * hout, axis=-1, keepdims=True) + EPS) * hn
    gated = _sigmoid(o) * hs
    return (gated, C_new, n_new), m_new


M_OFF_Q, M_OFF_K = 0, M_HEADS * M_QK
M_OFF_V = 2 * M_HEADS * M_QK
M_OFF_O = M_OFF_V + M_HEADS * M_V
M_OFF_G = M_OFF_O + M_HEADS * M_V
M_PROJ = M_OFF_G + LANES
M_PAIRS = M_HEADS * M_QK // LANES


def _head_cols(off, h):
    return slice(off + h * LANES, off + (h + 1) * LANES)


def _own_lanes(h, shape):
    low = lax.broadcasted_iota(jnp.int32, shape, 1) < M_QK
    return low if h % 2 == 0 else jnp.logical_not(low)


def _mlstm_specs(NC, rev):
    H, L = M_HEADS, M_CHUNK
    cc = (lambda c: NC - 1 - c) if rev else (lambda c: c)
    proj = pl.BlockSpec((L, M_PROJ), lambda c: (cc(c), 0))
    vec = pl.BlockSpec((1, LANES), lambda c: (0, 0))
    hn = pl.BlockSpec((1, H * M_V), lambda c: (0, 0))
    hv = pl.BlockSpec((L, H * M_V), lambda c: (cc(c), 0))
    Cs = pl.BlockSpec((None, H, LANES, M_V), lambda c: (cc(c), 0, 0, 0))
    ns = pl.BlockSpec((None, H, 1, LANES), lambda c: (cc(c), 0, 0, 0))
    ms = pl.BlockSpec((None, H, 1, 1), lambda c: (cc(c), 0, 0, 0))
    return proj, vec, hn, hv, Cs, ns, ms


_MLSTM_STATE = [pltpu.VMEM((M_HEADS, LANES, M_V), F32), pltpu.VMEM((M_HEADS, 1, LANES), F32),
                pltpu.VMEM((M_HEADS, 1, 1), F32)]


def _mlstm_fwd(proj, bias, hn, shards, *, name):
    S = proj.shape[0]
    H, NC = M_HEADS, S // M_CHUNK
    ps, vec, hns, hv, Cs, ns, ms = _mlstm_specs(NC, False)
    nsh = len(shards)
    any_spec = pl.BlockSpec(memory_space=pl.ANY)

    def body(p_ref, b_ref, hn_ref, *rest):
        gated_ref, C_all, n_all, m_all = rest[nsh:nsh + 4]
        C_s, n_s, m_s, send_sems, recv_sems = rest[2 * nsh + 4:]
        gather = _Gather(rest[:nsh], rest[nsh + 4:2 * nsh + 4], send_sems, recv_sems)

        @pl.when(pl.program_id(0) == 0)
        def _():
            C_s[...] = jnp.zeros_like(C_s)
            n_s[...] = jnp.zeros_like(n_s)
            m_s[...] = jnp.zeros_like(m_s)
            gather.issue()

        gate_terms = _mlstm_gates(p_ref[:, M_OFF_G:M_OFF_G + LANES], b_ref[...])
        for h in range(H):
            C, n, m = C_s[h], n_s[h], m_s[h]
            C_all[h] = C
            n_all[h] = n
            m_all[h] = m
            (gated, Cn, nn), mn = _mlstm_head(
                h, p_ref[:, _head_cols(M_OFF_Q, h // 2)], p_ref[:, _head_cols(M_OFF_K, h // 2)],
                p_ref[:, _head_cols(M_OFF_V, h)], p_ref[:, _head_cols(M_OFF_O, h)], *gate_terms,
                hn_ref[:, _head_cols(0, h)], C, n, m)
            gated_ref[:, _head_cols(0, h)] = gated.astype(gated_ref.dtype)
            C_s[h] = Cn
            n_s[h] = nn
            m_s[h] = mn

        @pl.when(pl.program_id(0) == NC - 1)
        def _():
            gather.finish()

    return pl.pallas_call(
        body, name=name, grid=(NC,),
        in_specs=[ps, vec, hns] + [any_spec] * nsh,
        out_specs=[hv, Cs, ns, ms] + [any_spec] * nsh,
        out_shape=[jax.ShapeDtypeStruct((S, H * M_V), MXU_DTYPE),
                   jax.ShapeDtypeStruct((NC, H, LANES, M_V), F32),
                   jax.ShapeDtypeStruct((NC, H, 1, LANES), F32),
                   jax.ShapeDtypeStruct((NC, H, 1, 1), F32)] + _Gather.out_shape(shards),
        scratch_shapes=list(_MLSTM_STATE) + _Gather.semaphores(nsh),
        compiler_params=_cparams(("arbitrary",)),
    )(proj, bias, hn, *shards)


def _mlstm_bwd(proj, bias, hn, C_all, n_all, m_all, dgated, qs, *, name):
    S = proj.shape[0]
    H, NC = M_HEADS, S // M_CHUNK
    ps, vec, hns, hv, Cs, ns, ms = _mlstm_specs(NC, True)
    nq = len(qs)
    any_spec = pl.BlockSpec(memory_space=pl.ANY)

    def body(p_ref, b_ref, hn_ref, C_ref, n_ref, m_ref, dg_ref, *rest):
        dp_ref, db_ref, dhn_ref = rest[nq:nq + 3]
        dC_s, dn_s, send_sems, recv_sems = rest[2 * nq + 3:]
        exchange = _Exchange(rest[:nq], rest[nq + 3:2 * nq + 3], send_sems, recv_sems)

        @pl.when(pl.program_id(0) == 0)
        def _():
            dC_s[...] = jnp.zeros_like(dC_s)
            dn_s[...] = jnp.zeros_like(dn_s)
            db_ref[...] = jnp.zeros_like(db_ref)
            dhn_ref[...] = jnp.zeros_like(dhn_ref)
            exchange.issue()

        gate_terms, gates_vjp = jax.vjp(_mlstm_gates, p_ref[:, M_OFF_G:M_OFF_G + LANES], b_ref[...])
        d_terms = [jnp.zeros_like(t) for t in gate_terms]
        for h in range(H):
            def head(q, k, v, o, *rest, h=h):
                return _mlstm_head(h, q, k, v, o, *rest, m_ref[h])

            prim = (p_ref[:, _head_cols(M_OFF_Q, h // 2)], p_ref[:, _head_cols(M_OFF_K, h // 2)],
                    p_ref[:, _head_cols(M_OFF_V, h)], p_ref[:, _head_cols(M_OFF_O, h)], *gate_terms,
                    hn_ref[:, _head_cols(0, h)], C_ref[h], n_ref[h])
            _, vjp, _ = jax.vjp(head, *prim, has_aux=True)
            dq, dk, dv, do, *d_gate, dhnh, dC, dn = vjp((dg_ref[:, _head_cols(0, h)].astype(F32), dC_s[h], dn_s[h]))
            if h % 2 == 0:
                dq_pair, dk_pair = dq, dk
            else:
                dp_ref[:, _head_cols(M_OFF_Q, h // 2)] = (dq_pair + dq).astype(dp_ref.dtype)
                dp_ref[:, _head_cols(M_OFF_K, h // 2)] = (dk_pair + dk).astype(dp_ref.dtype)
            dp_ref[:, _head_cols(M_OFF_V, h)] = dv.astype(dp_ref.dtype)
            dp_ref[:, _head_cols(M_OFF_O, h)] = do.astype(dp_ref.dtype)
            d_terms = [a + g for a, g in zip(d_terms, d_gate)]
            dhn_ref[:, _head_cols(0, h)] += dhnh
            dC_s[h] = dC
            dn_s[h] = dn
        dG, dbias = gates_vjp(tuple(d_terms))
        dp_ref[:, M_OFF_G:M_OFF_G + LANES] = dG.astype(dp_ref.dtype)
        db_ref[...] += dbias

        @pl.when(pl.program_id(0) == NC - 1)
        def _():
            exchange.finish()

    return pl.pallas_call(
        body, name=name, grid=(NC,),
        in_specs=[ps, vec, hns, Cs, ns, ms, hv] + [any_spec] * nq,
        out_specs=[ps, vec, hns] + [any_spec] * nq,
        out_shape=[jax.ShapeDtypeStruct((S, M_PROJ), MXU_DTYPE), jax.ShapeDtypeStruct((1, LANES), F32),
                   jax.ShapeDtypeStruct((1, H * M_V), F32)] + _Exchange.out_shape(qs),
        scratch_shapes=list(_MLSTM_STATE[:2]) + _Exchange.semaphores(nq),
        compiler_params=_cparams(("arbitrary",)),
    )(proj, bias, hn, C_all, n_all, m_all, dgated, *qs)


A_NQ = A_QH * A_DH
A_NKV = 2 * A_KVH * A_DH
A_PAIRS = A_G // 2


def _sink_softmax_parts(s, sink):
    mx = jnp.maximum(jnp.max(s, axis=-1, keepdims=True), sink)
    e = jnp.exp(s - mx)
    e_sink = jnp.exp(sink - mx)
    r = 1.0 / (jnp.sum(e, axis=-1, keepdims=True) + e_sink)
    return e * r, e_sink * r


@jax.custom_vjp
def _sink_softmax(s, sink):
    return _sink_softmax_parts(s, sink)[0]


def _sink_softmax_fwd(s, sink):
    p, p_sink = _sink_softmax_parts(s, sink)
    return p, (p, p_sink)


def _sink_softmax_bwd(res, dp):
    p, p_sink = res
    t = jnp.sum(dp * p, axis=-1, keepdims=True)
    return p * (dp - t), -p_sink * t


_sink_softmax.defvjp(_sink_softmax_fwd, _sink_softmax_bwd)


def _band_bias():
    qi = np.arange(A_G * A_BLK)[:, None] % A_BLK
    ku = np.arange(2 * A_BLK)[None, :]
    diff = qi - (ku - A_BLK)
    band = (diff >= 0) & (diff < A_BLK)
    seen = np.stack([band, band & (ku >= A_BLK)])
    return jnp.asarray(np.where(seen, 0.0, -np.inf), F32)


_BAND_BIAS_SPEC = pl.BlockSpec((None, A_G * A_BLK, 2 * A_BLK), lambda n: (jnp.where(n == 0, 1, 0), 0, 0))


def _attn_group(Ps, KLO, KHI, VLO, VHI, sinks, bias):
    B = Ps[0].shape[0]
    R = len(Ps) * B
    q2 = jnp.concatenate(Ps, axis=0) * (A_DH ** -0.5)
    s = jnp.concatenate([_dot(q2, KLO, 1, 1), _dot(q2, KHI, 1, 1)], axis=0) + bias
    ri = lax.broadcasted_iota(jnp.int32, (2 * R, A_G), 0)
    head = 2 * lax.shift_right_logical(ri & (R - 1), B.bit_length() - 1) + lax.shift_right_logical(ri, R.bit_length() - 1)
    onehot = head == lax.broadcasted_iota(jnp.int32, (2 * R, A_G), 1)
    sink = jnp.sum(jnp.where(onehot, sinks, 0.0), axis=1, keepdims=True)
    p = _sink_softmax(s, sink)
    o = _dot(p[:R], VLO, 1, 0) + _dot(p[R:], VHI, 1, 0)
    return tuple(o[j * B:(j + 1) * B] for j in range(len(Ps)))


def _swap_halves_of_lanes(t):
    return pltpu.roll(t, LANES // 2, 1)


def _kv_operands(kvp_ref, kvc_ref, h):
    kk = jnp.concatenate([kvp_ref[:, :LANES], kvc_ref[:, :LANES]], axis=0)
    vv = jnp.concatenate([kvp_ref[:, LANES:], kvc_ref[:, LANES:]], axis=0)
    low = lax.broadcasted_iota(jnp.int32, kk.shape, 1) < A_DH
    own = low if h == 0 else jnp.logical_not(low)
    k_own = jnp.where(own, kk, 0.0)
    v_own = jnp.where(own, vv, 0.0)
    k_oth, v_oth = _swap_halves_of_lanes(k_own), _swap_halves_of_lanes(v_own)
    if h == 0:
        return own, k_own, k_oth, v_own, v_oth
    return own, k_oth, k_own, v_oth, v_own


def _pair_cols(h, j):
    c = (h * A_PAIRS + j) * LANES
    return slice(c, c + LANES)


def _attn_fwd(proj, sinks, *, name):
    S = proj.shape[0]
    NB = S // A_BLK
    kv_blk = A_NQ // A_NKV
    qs = pl.BlockSpec((A_BLK, A_NQ), lambda n: (n, 0))
    cur = pl.BlockSpec((A_BLK, A_NKV), lambda n: (n, kv_blk))
    prev = pl.BlockSpec((A_BLK, A_NKV), lambda n: (jnp.maximum(n - 1, 0), kv_blk))
    sk = pl.BlockSpec((A_KVH, A_G), lambda n: (0, 0))

    def body(q_ref, kvp_ref, kvc_ref, s_ref, bias_ref, o_ref):
        for h in range(A_KVH):
            _, KLO, KHI, VLO, VHI = _kv_operands(kvp_ref, kvc_ref, h)
            Ps = tuple(q_ref[:, _pair_cols(h, j)] for j in range(A_PAIRS))
            outs = _attn_group(Ps, KLO, KHI, VLO, VHI, s_ref[h:h + 1, :], bias_ref[...])
            for j in range(A_PAIRS):
                o_ref[:, _pair_cols(h, j)] = outs[j].astype(o_ref.dtype)

    return pl.pallas_call(
        body, name=name, grid=(NB,),
        in_specs=[qs, prev, cur, sk, _BAND_BIAS_SPEC], out_specs=qs,
        out_shape=jax.ShapeDtypeStruct((S, A_NQ), MXU_DTYPE),
        compiler_params=_cparams(("parallel",)),
    )(proj, proj, proj, sinks, _band_bias())


def _attn_bwd(proj, sinks, do, *, name):
    S = proj.shape[0]
    NB = S // A_BLK
    last = NB - 1
    kv_blk = A_NQ // A_NKV
    qs = pl.BlockSpec((A_BLK, A_NQ), lambda n: (jnp.minimum(n, last), 0))
    cur = pl.BlockSpec((A_BLK, A_NKV), lambda n: (jnp.minimum(n, last), kv_blk))
    prev = pl.BlockSpec((A_BLK, A_NKV), lambda n: (jnp.clip(n - 1, 0, last), kv_blk))
    sk = pl.BlockSpec((A_KVH, A_G), lambda n: (0, 0))
    lag = pl.BlockSpec((A_BLK, A_NKV), lambda n: (jnp.maximum(n - 1, 0), 0))
    cq_spec = pl.BlockSpec((1, A_NQ), lambda n: (0, 0))
    ckv_spec = pl.BlockSpec((1, A_NKV), lambda n: (0, 0))

    def body(q_ref, kvp_ref, kvc_ref, s_ref, do_ref, bias_ref, dq_ref, dkv_ref, ds_ref, cq_ref, ckv_ref, keep):
        n = pl.program_id(0)

        @pl.when(n == 0)
        def _():
            keep[...] = jnp.zeros_like(keep)
            ds_ref[...] = jnp.zeros_like(ds_ref)
            cq_ref[...] = jnp.zeros_like(cq_ref)
            ckv_ref[...] = jnp.zeros_like(ckv_ref)

        @pl.when(n < NB)
        def _():
            f = functools.partial(_attn_group, bias=bias_ref[...])
            dk = jnp.zeros((2 * A_BLK, LANES), F32)
            dv = jnp.zeros((2 * A_BLK, LANES), F32)
            ds_rows = []
            for h in range(A_KVH):
                own, KLO, KHI, VLO, VHI = _kv_operands(kvp_ref, kvc_ref, h)
                Ps = tuple(q_ref[:, _pair_cols(h, j)] for j in range(A_PAIRS))
                _, vjp = jax.vjp(f, Ps, KLO, KHI, VLO, VHI, s_ref[h:h + 1, :])
                dPs, dKLO, dKHI, dVLO, dVHI, dsk = vjp(
                    tuple(do_ref[:, _pair_cols(h, j)].astype(F32) for j in range(A_PAIRS)))
                for j in range(A_PAIRS):
                    dq_ref[:, _pair_cols(h, j)] = dPs[j].astype(dq_ref.dtype)
                    cq_ref[:, _pair_cols(h, j)] += jnp.sum(dPs[j], axis=0, keepdims=True)
                dk_own, dk_oth = (dKLO, dKHI) if h == 0 else (dKHI, dKLO)
                dv_own, dv_oth = (dVLO, dVHI) if h == 0 else (dVHI, dVLO)
                dk = dk + jnp.where(own, dk_own, 0.0) + _swap_halves_of_lanes(jnp.where(own, 0.0, dk_oth))
                dv = dv + jnp.where(own, dv_own, 0.0) + _swap_halves_of_lanes(jnp.where(own, 0.0, dv_oth))
                ds_rows.append(dsk)
            ds_ref[...] += jnp.concatenate(ds_rows, axis=0)
            dkv = jnp.concatenate([dk, dv], axis=1)
            done = keep[...] + dkv[:A_BLK]
            dkv_ref[...] = done.astype(dkv_ref.dtype)
            ckv_ref[...] += jnp.sum(done, axis=0, keepdims=True)
            keep[...] = dkv[A_BLK:]

        @pl.when(n == NB)
        def _():
            done = keep[...]
            dkv_ref[...] = done.astype(dkv_ref.dtype)
            ckv_ref[...] += jnp.sum(done, axis=0, keepdims=True)

    return pl.pallas_call(
        body, name=name, grid=(NB + 1,),
        in_specs=[qs, prev, cur, sk, qs, _BAND_BIAS_SPEC],
        out_specs=[qs, lag, sk, cq_spec, ckv_spec],
        out_shape=[jax.ShapeDtypeStruct((S, A_NQ), MXU_DTYPE), jax.ShapeDtypeStruct((S, A_NKV), MXU_DTYPE),
                   jax.ShapeDtypeStruct((A_KVH, A_G), F32), jax.ShapeDtypeStruct((1, A_NQ), F32),
                   jax.ShapeDtypeStruct((1, A_NKV), F32)],
        scratch_shapes=[pltpu.VMEM((A_BLK, A_NKV), F32)],
        compiler_params=_cparams(("arbitrary",)),
    )(proj, proj, proj, sinks, do, _band_bias())


HALO = 8


def _shift_rows(t, j):
    return pltpu.roll(t, j % t.shape[0], 0)


def _conv_taps(gate_ext):
    return _shift_rows(gate_ext, 2), _shift_rows(gate_ext, 1), gate_ext


def _conv_gate(g2, g1, g0, cw, cb):
    return cb + cw[0:1, :] * g2 + cw[1:2, :] * g1 + cw[2:3, :] * g0


def _convgate_fwd(u, cw, cb, *, name):
    S, F2 = u.shape
    F = F2 // 2
    tm = _pick(S, (512, 256, 128))
    hb = tm // HALO
    urow = pl.BlockSpec((tm, F2), lambda i: (i, 0))
    uprev = pl.BlockSpec((HALO, F), lambda i: (jnp.maximum(i * hb - 1, 0), 0))

    def body(u_ref, up_ref, cw_ref, cb_ref, a_ref):
        i = pl.program_id(0)
        prev = jnp.where(i > 0, up_ref[...], 0.0)
        gc = _conv_gate(*_conv_taps(jnp.concatenate([prev, u_ref[:, :F]], axis=0)), cw_ref[...], cb_ref[...])[HALO:]
        a_ref[...] = (gc * _sigmoid(gc) * u_ref[:, F:]).astype(a_ref.dtype)

    return pl.pallas_call(
        body, name=name, grid=(S // tm,),
        in_specs=[urow, uprev, pl.BlockSpec((3, F), lambda i: (0, 0)), pl.BlockSpec((1, F), lambda i: (0, 0))],
        out_specs=pl.BlockSpec((tm, F), lambda i: (i, 0)),
        out_shape=jax.ShapeDtypeStruct((S, F), MXU_DTYPE),
        compiler_params=_cparams(("parallel",)),
    )(u, u, cw, cb)


def _convgate_bwd(u, da, cw, cb, *, name):
    S, F2 = u.shape
    F = F2 // 2
    tm = _pick(S, (256, 128))
    hb = tm // HALO
    nt = S // tm
    nh = S // HALO
    urow = pl.BlockSpec((tm, F2), lambda i: (i, 0))
    uprev = pl.BlockSpec((HALO, F), lambda i: (jnp.maximum(i * hb - 1, 0), 0))
    unext = pl.BlockSpec((HALO, F2), lambda i: (jnp.minimum((i + 1) * hb, nh - 1), 0))
    darow = pl.BlockSpec((tm, F), lambda i: (i, 0))
    danext = pl.BlockSpec((HALO, F), lambda i: (jnp.minimum((i + 1) * hb, nh - 1), 0))

    def body(u_ref, up_ref, un_ref, da_ref, dan_ref, cw_ref, cb_ref, du_ref, dcw_ref, dcb_ref):
        i = pl.program_id(0)
        cwv = cw_ref[...]
        prev = jnp.where(i > 0, up_ref[...], 0.0)
        gate_ext = jnp.concatenate([prev, u_ref[:, :F], un_ref[:, :F]], axis=0)
        val_ext = jnp.concatenate([u_ref[:, F:], un_ref[:, F:]], axis=0)
        da_next = jnp.where(i < nt - 1, dan_ref[...].astype(F32), 0.0)
        da_ext = jnp.concatenate([da_ref[...].astype(F32), da_next], axis=0)
        g2, g1, g0 = (t[HALO:] for t in _conv_taps(gate_ext))
        gc = _conv_gate(g2, g1, g0, cwv, cb_ref[...])
        sg = _sigmoid(gc)
        silu = gc * sg
        dval = da_ext * silu
        dgc = da_ext * val_ext * (sg * (1.0 + gc * (1.0 - sg)))
        dgate = cwv[2:3, :] * dgc + cwv[1:2, :] * _shift_rows(dgc, -1) + cwv[0:1, :] * _shift_rows(dgc, -2)
        du_ref[:, :F] = dgate[:tm].astype(du_ref.dtype)
        du_ref[:, F:] = dval[:tm].astype(du_ref.dtype)
        dgc_c = dgc[:tm]
        dcw = jnp.concatenate([jnp.sum(dgc_c * g2[:tm], axis=0, keepdims=True),
                               jnp.sum(dgc_c * g1[:tm], axis=0, keepdims=True),
                               jnp.sum(dgc_c * g0[:tm], axis=0, keepdims=True)], axis=0)
        dcb = jnp.sum(dgc_c, axis=0, keepdims=True)

        @pl.when(i == 0)
        def _():
            dcw_ref[...] = dcw
            dcb_ref[...] = dcb

        @pl.when(i > 0)
        def _():
            dcw_ref[...] += dcw
            dcb_ref[...] += dcb

    return pl.pallas_call(
        body, name=name, grid=(nt,),
        in_specs=[urow, uprev, unext, darow, danext,
                  pl.BlockSpec((3, F), lambda i: (0, 0)), pl.BlockSpec((1, F), lambda i: (0, 0))],
        out_specs=[urow, pl.BlockSpec((3, F), lambda i: (0, 0)), pl.BlockSpec((1, F), lambda i: (0, 0))],
        out_shape=[jax.ShapeDtypeStruct((S, F2), MXU_DTYPE), jax.ShapeDtypeStruct((3, F), F32),
                   jax.ShapeDtypeStruct((1, F), F32)],
        compiler_params=_cparams(("arbitrary",)),
    )(u, u, u, da, da, cw, cb)


def _adamw_math(w, g, m, v):
    m = ADAM_B1 * m + (1.0 - ADAM_B1) * g
    v = ADAM_B2 * v + (1.0 - ADAM_B2) * (g * g)
    m_hat = m / (1.0 - ADAM_B1 ** ADAM_STEP)
    v_hat = v / (1.0 - ADAM_B2 ** ADAM_STEP)
    delta = -ADAM_LR * (m_hat / (jnp.sqrt(v_hat) + ADAM_EPS) + ADAM_WD * w)
    return delta, m, v


ADAMW_BLOCK_BYTES = 2 * 1024 * 1024


def _adamw_layers(w, gs, m, v, *, name):
    Lr, R, C = w.shape
    if C % LANES:
        flip = lambda t: jnp.swapaxes(t, -1, -2)
        return [flip(o) for o in _adamw_layers(flip(w), [flip(g) for g in gs], flip(m), flip(v), name=name)]
    tr = _pick(R, (256, 128, 64, 32, 16, 8))
    tc = C if tr * C * 4 <= ADAMW_BLOCK_BYTES else _pick(C, (256, 128))
    outs = None
    for layer, g in enumerate(gs):
        def body(w_ref, g_ref, m_ref, v_ref, *rest):
            go_ref, d_ref, nm_ref, nv_ref = rest[-4:]
            gv = g_ref[...]
            d, nm, nv = _adamw_math(w_ref[...], gv, m_ref[...], v_ref[...])
            go_ref[...] = gv
            d_ref[...] = d
            nm_ref[...] = nm
            nv_ref[...] = nv

        lay = pl.BlockSpec((None, tr, tc), lambda i, j, layer=layer: (layer, i, j))
        in_specs = [lay, pl.BlockSpec((tr, tc), lambda i, j: (i, j)), lay, lay]
        args = [w, g, m, v]
        aliases = {}
        if outs is not None:
            in_specs += [pl.BlockSpec(memory_space=pl.ANY)] * 4
            args += list(outs)
            aliases = {4 + t: t for t in range(4)}
        outs = pl.pallas_call(
            body, name=f"{name}_{layer}", grid=(R // tr, C // tc), in_specs=in_specs, out_specs=[lay] * 4,
            out_shape=[jax.ShapeDtypeStruct((Lr, R, C), F32)] * 4, input_output_aliases=aliases,
            compiler_params=_cparams(("parallel", "parallel")),
        )(*args)
    return outs


def _adamw_small(items, *, name):
    n = len(items)

    def body(*refs):
        ins, outs = refs[:4 * n], refs[4 * n:]
        for t in range(n):
            w, g, m, v = (r[...] for r in ins[4 * t:4 * t + 4])
            d, nm, nv = _adamw_math(w, g, m, v)
            outs[3 * t][...] = d
            outs[3 * t + 1][...] = nm
            outs[3 * t + 2][...] = nv

    flat = [a for it in items for a in it]
    out_shape = [jax.ShapeDtypeStruct(it[0].shape, F32) for it in items for _ in range(3)]
    vm = pl.BlockSpec(memory_space=pltpu.VMEM)
    res = pl.pallas_call(body, name=name, in_specs=[vm] * len(flat), out_specs=[vm] * len(out_shape),
                         out_shape=out_shape)(*flat)
    return [tuple(res[3 * t:3 * t + 3]) for t in range(n)]


def _place():
    return lax.axis_index("x"), lax.axis_index("y"), lax.axis_index("c")


_FLIPS = ((1, 0), (0, 1), (1, 1))


ROW_ALIGN = 16


def _half(rows, which):
    return pl.ds(pl.multiple_of(which * (rows // 2), ROW_ALIGN), rows // 2)


def _remote(src, dst, send_sems, recv_sems, k, to):
    return pltpu.make_async_remote_copy(src_ref=src, dst_ref=dst, send_sem=send_sems.at[k], recv_sem=recv_sems.at[k],
                                        device_id=to, device_id_type=MESH)


class _Gather:
    def __init__(self, w_refs, out_refs, send_sems, recv_sems):
        self.w_refs, self.out_refs, self.send_sems, self.recv_sems = w_refs, out_refs, send_sems, recv_sems
        self.pairs = [(i, j) for i in range(len(w_refs)) for j in range(3)]

    @staticmethod
    def out_shape(shards):
        return [jax.ShapeDtypeStruct((N_CHIPS,) + s.shape, s.dtype) for s in shards]

    @staticmethod
    def semaphores(n):
        return [pltpu.SemaphoreType.DMA((6 * n,)), pltpu.SemaphoreType.DMA((6 * n,))]

    def _where(self):
        x, y, c = _place()
        return x, y, c, [(x ^ fx, y ^ fy) for fx, fy in _FLIPS]

    def _over_ici(self, i, j, landed):
        x, y, c, chips = self._where()
        px, py = chips[j]
        mine = _half(self.w_refs[i].shape[0], c)
        if landed:
            src = dst = self.out_refs[i].at[2 * px + py, mine]
        else:
            src, dst = self.w_refs[i].at[mine], self.out_refs[i].at[2 * x + y, mine]
        return _remote(src, dst, self.send_sems, self.recv_sems, 6 * i + j, (px, py, c))

    def _over_d2d(self, i, j, which):
        x, y, c, chips = self._where()
        px, py = chips[j]
        blk = self.out_refs[i].at[2 * px + py, _half(self.w_refs[i].shape[0], which)]
        return _remote(blk, blk, self.send_sems, self.recv_sems, 6 * i + 3 + j, (x, y, 1 - c))

    def issue(self):
        for i, j in self.pairs:
            self._over_ici(i, j, False).start()

    def finish(self):
        c = lax.axis_index("c")
        for i, j in self.pairs:
            self._over_ici(i, j, True).wait_recv()
            self._over_d2d(i, j, c).start()
        for i, j in self.pairs:
            self._over_d2d(i, j, 1 - c).wait_recv()
        for i, j in self.pairs:
            self._over_ici(i, j, False).wait_send()
            self._over_d2d(i, j, c).wait_send()


def _swap_halves(gs, *, name):
    n = len(gs)
    any_spec = pl.BlockSpec(memory_space=pl.ANY)

    def body(*refs):
        swap = _Swap(refs[:n], refs[n:2 * n], refs[2 * n], refs[2 * n + 1])
        swap.issue()
        swap.finish()

    return pl.pallas_call(
        body, name=name, in_specs=[any_spec] * n, out_specs=[any_spec] * n,
        out_shape=_Swap.out_shape(gs), scratch_shapes=_Swap.semaphores(n),
    )(*gs)


class _Swap:
    def __init__(self, g_refs, out_refs, send_sems, recv_sems):
        self.g_refs, self.out_refs, self.send_sems, self.recv_sems = g_refs, out_refs, send_sems, recv_sems

    @staticmethod
    def out_shape(gs):
        return [jax.ShapeDtypeStruct((N_CHIPS, g.shape[1] // 2, g.shape[2]), g.dtype) for g in gs]

    @staticmethod
    def semaphores(n):
        return [pltpu.SemaphoreType.DMA((n,)), pltpu.SemaphoreType.DMA((n,))]

    def _copies(self):
        x, y, c = _place()
        return [_remote(g.at[:, _half(g.shape[1], 1 - c)], out, self.send_sems, self.recv_sems, i, (x, y, 1 - c))
                for i, (g, out) in enumerate(zip(self.g_refs, self.out_refs))]

    def issue(self):
        for cp in self._copies():
            cp.start()

    def finish(self):
        for cp in self._copies():
            cp.wait()


def _add_halves(g, got, c_arr, *, name):
    _, rows, cols = g.shape
    blk = (None, rows // 2, cols)

    def body(c_ref, g_ref, got_ref, o_ref):
        o_ref[...] = (g_ref[...] + got_ref[...]).astype(o_ref.dtype)

    return pl.pallas_call(
        body, name=name,
        grid_spec=pltpu.PrefetchScalarGridSpec(
            num_scalar_prefetch=1, grid=(N_CHIPS,),
            in_specs=[pl.BlockSpec(blk, lambda j, c_ref: (j, c_ref[0], 0)), pl.BlockSpec(blk, lambda j, c_ref: (j, 0, 0))],
            out_specs=pl.BlockSpec(blk, lambda j, c_ref: (j, 0, 0))),
        out_shape=jax.ShapeDtypeStruct((N_CHIPS, rows // 2, cols), WIRE_DTYPE),
        compiler_params=_cparams(("parallel",)),
    )(c_arr, g, got)


class _Exchange:
    def __init__(self, q_refs, out_refs, send_sems, recv_sems):
        self.q_refs, self.out_refs, self.send_sems, self.recv_sems = q_refs, out_refs, send_sems, recv_sems
        self.pairs = [(i, j) for i in range(len(q_refs)) for j in range(3)]

    @staticmethod
    def out_shape(qs):
        return [jax.ShapeDtypeStruct(q.shape, q.dtype) for q in qs]

    @staticmethod
    def semaphores(n):
        return [pltpu.SemaphoreType.DMA((3 * n,)), pltpu.SemaphoreType.DMA((3 * n,))]

    def _copy(self, i, j, landed):
        x, y, c = _place()
        px, py = [(x ^ fx, y ^ fy) for fx, fy in _FLIPS][j]
        if landed:
            src = dst = self.out_refs[i].at[2 * px + py]
        else:
            src, dst = self.q_refs[i].at[2 * px + py], self.out_refs[i].at[2 * x + y]
        return _remote(src, dst, self.send_sems, self.recv_sems, 3 * i + j, (px, py, c))

    def issue(self):
        for i, j in self.pairs:
            self._copy(i, j, False).start()

    def finish(self):
        for i, j in self.pairs:
            self._copy(i, j, True).wait_recv()
        for i, j in self.pairs:
            self._copy(i, j, False).wait_send()


def _sum_chips(q, r, place_arr, *, name):
    _, h, cols = q.shape
    blk = (None, h, cols)

    def body(p_ref, q_ref, r1_ref, r2_ref, r3_ref, o_ref):
        o_ref[...] = ((q_ref[...].astype(F32) + r1_ref[...].astype(F32)) + r2_ref[...].astype(F32)) + r3_ref[...].astype(F32)

    other = [pl.BlockSpec(blk, lambda i, p_ref, f=f: (p_ref[0] ^ f, 0, 0)) for f in (1, 2, 3)]
    return pl.pallas_call(
        body, name=name,
        grid_spec=pltpu.PrefetchScalarGridSpec(
            num_scalar_prefetch=1, grid=(1,),
            in_specs=[pl.BlockSpec(blk, lambda i, p_ref: (p_ref[0], 0, 0))] + other,
            out_specs=pl.BlockSpec((h, cols), lambda i, p_ref: (p_ref[1], 0))),
        out_shape=jax.ShapeDtypeStruct((2 * h, cols), F32),
        compiler_params=_cparams(("arbitrary",)),
    )(place_arr, q, r, r, r)


def _join_halves(fs, *, name):
    n = len(fs)
    any_spec = pl.BlockSpec(memory_space=pl.ANY)

    def body(*refs):
        out_refs, send_sems, recv_sems = refs[n:2 * n], refs[2 * n], refs[2 * n + 1]
        x, y, c = _place()
        sent = []
        for i in range(n):
            mine = out_refs[i].at[_half(fs[i].shape[0], c)]
            cp = _remote(mine, mine, send_sems, recv_sems, i, (x, y, 1 - c))
            cp.start()
            sent.append(cp)
        for i in range(n):
            its = out_refs[i].at[_half(fs[i].shape[0], 1 - c)]
            _remote(its, its, send_sems, recv_sems, i, (x, y, 1 - c)).wait_recv()
        for cp in sent:
            cp.wait_send()

    return pl.pallas_call(
        body, name=name, in_specs=[any_spec] * n, out_specs=[any_spec] * n,
        out_shape=[jax.ShapeDtypeStruct(f.shape, f.dtype) for f in fs], input_output_aliases={i: i for i in range(n)},
        scratch_shapes=[pltpu.SemaphoreType.DMA((n,)), pltpu.SemaphoreType.DMA((n,))],
    )(*fs)


def _allreduce_small(buf, *, name):
    R = buf.shape[0]
    vm = pl.BlockSpec(memory_space=pltpu.VMEM)

    def body(b_ref, o_ref, slots, send_sems, recv_sems):
        x, y, c = _place()
        me = 4 * x + 2 * y + c
        slots[me] = b_ref[...]
        sends = []
        for kk in range(1, 8):
            fx, fy, fc = (kk >> 2) & 1, (kk >> 1) & 1, kk & 1
            cp = pltpu.make_async_remote_copy(
                src_ref=b_ref, dst_ref=slots.at[me], send_sem=send_sems.at[kk - 1], recv_sem=recv_sems.at[kk - 1],
                device_id=(x ^ fx, y ^ fy, c ^ fc), device_id_type=MESH)
            cp.start()
            sends.append(cp)
        for kk in range(1, 8):
            fx, fy, fc = (kk >> 2) & 1, (kk >> 1) & 1, kk & 1
            peer = 4 * (x ^ fx) + 2 * (y ^ fy) + (c ^ fc)
            pltpu.make_async_remote_copy(
                src_ref=b_ref, dst_ref=slots.at[peer], send_sem=send_sems.at[kk - 1], recv_sem=recv_sems.at[kk - 1],
                device_id=(x ^ fx, y ^ fy, c ^ fc), device_id_type=MESH).wait_recv()
        for cp in sends:
            cp.wait_send()
        acc = slots[0]
        for d in range(1, 8):
            acc = acc + slots[d]
        o_ref[...] = acc

    return pl.pallas_call(
        body, name=name, in_specs=[vm], out_specs=vm,
        out_shape=jax.ShapeDtypeStruct((R, LANES), F32),
        scratch_shapes=[pltpu.VMEM((8, R, LANES), F32), pltpu.SemaphoreType.DMA((7,)), pltpu.SemaphoreType.DMA((7,))],
    )(buf)


def _pad_rows(v, mult=8 * LANES):
    flat = v.reshape(-1)
    n = flat.shape[0]
    tot = -(-n // mult) * mult
    return jnp.pad(flat, (0, tot - n)).reshape(-1, LANES)


def _pack_small(parts):
    return jnp.concatenate([_pad_rows(p.astype(F32)) for p in parts], axis=0)


def _unpack_small(buf, shapes):
    out, r = [], 0
    for sh in shapes:
        n = math.prod(sh)
        rows = -(-n // (8 * LANES)) * 8
        out.append(buf[r:r + rows].reshape(-1)[:n].reshape(sh))
        r += rows
    return out


def _cols_to_shards(w):
    *lead, K, N = w.shape
    t = w.reshape(*lead, K, N_CHIPS, N // N_CHIPS)
    return jnp.moveaxis(t, -2, 0)


def _shards_to_cols(t):
    t = jnp.moveaxis(t, 0, -2)
    *lead, K, _, n = t.shape
    return t.reshape(*lead, K, N_CHIPS * n)


_BIG = ("m_w_in", "m_w_out", "a_w_in", "a_w_out", "f_w_up", "f_w_down")


def kernel(x, m_w_in, m_gate_bias, m_head_norm, m_w_out, a_w_in, a_b_in, a_sinks, a_w_out, a_b_out, norm_mix_pre, norm_mix_post, norm_ffn_pre, norm_ffn_post, f_w_up, f_conv_w, f_conv_b, f_w_down, loss_target, m_m_w_in, m_m_gate_bias, m_m_head_norm, m_m_w_out, m_a_w_in, m_a_b_in, m_a_sinks, m_a_w_out, m_a_b_out, m_norm_mix_pre, m_norm_mix_post, m_norm_ffn_pre, m_norm_ffn_post, m_f_w_up, m_f_conv_w, m_f_conv_b, m_f_w_down, v_m_w_in, v_m_gate_bias, v_m_head_norm, v_m_w_out, v_a_w_in, v_a_b_in, v_a_sinks, v_a_w_out, v_a_b_out, v_norm_mix_pre, v_norm_mix_post, v_norm_ffn_pre, v_norm_ffn_post, v_f_w_up, v_f_conv_w, v_f_conv_b, v_f_w_down):
    params = dict(m_w_in=m_w_in, m_gate_bias=m_gate_bias, m_head_norm=m_head_norm, m_w_out=m_w_out, a_w_in=a_w_in,
                  a_b_in=a_b_in, a_sinks=a_sinks, a_w_out=a_w_out, a_b_out=a_b_out, norm_mix_pre=norm_mix_pre,
                  norm_mix_post=norm_mix_post, norm_ffn_pre=norm_ffn_pre, norm_ffn_post=norm_ffn_post, f_w_up=f_w_up,
                  f_conv_w=f_conv_w, f_conv_b=f_conv_b, f_w_down=f_w_down)
    mom1 = dict(m_w_in=m_m_w_in, m_gate_bias=m_m_gate_bias, m_head_norm=m_m_head_norm, m_w_out=m_m_w_out,
                a_w_in=m_a_w_in, a_b_in=m_a_b_in, a_sinks=m_a_sinks, a_w_out=m_a_w_out, a_b_out=m_a_b_out,
                norm_mix_pre=m_norm_mix_pre, norm_mix_post=m_norm_mix_post, norm_ffn_pre=m_norm_ffn_pre,
                norm_ffn_post=m_norm_ffn_post, f_w_up=m_f_w_up, f_conv_w=m_f_conv_w, f_conv_b=m_f_conv_b,
                f_w_down=m_f_w_down)
    mom2 = dict(m_w_in=v_m_w_in, m_gate_bias=v_m_gate_bias, m_head_norm=v_m_head_norm, m_w_out=v_m_w_out,
                a_w_in=v_a_w_in, a_b_in=v_a_b_in, a_sinks=v_a_sinks, a_w_out=v_a_w_out, a_b_out=v_a_b_out,
                norm_mix_pre=v_norm_mix_pre, norm_mix_post=v_norm_mix_post, norm_ffn_pre=v_norm_ffn_pre,
                norm_ffn_post=v_norm_ffn_post, f_w_up=v_f_w_up, f_conv_w=v_f_conv_w, f_conv_b=v_f_conv_b,
                f_w_down=v_f_w_down)
    order = list(params)

    mx, my, mc = _place()
    chip = 2 * mx + my
    h0 = x[0]
    target = loss_target[0]

    def two_d(t):
        return t.reshape(-1, t.shape[-1])

    early = ("m_w_in",)
    late = tuple(n for n in _BIG if n not in early)
    mine = {n: two_d(params[n]).astype(MXU_DTYPE) for n in _BIG}

    def with_own_slot(names, theirs):
        return {n: lax.dynamic_update_slice(t, mine[n][None], (chip, 0, 0)) for n, t in zip(names, theirs)}

    z0, *early_theirs = _rms_fwd(h0, norm_mix_pre[0:1], [mine[n] for n in early], out_dtype=MXU_DTYPE,
                                 name="mix_pre_norm0")
    gathered = with_own_slot(early, early_theirs)

    def in_place(shard, axis):
        width = shard.shape[axis]
        z = jnp.zeros(shard.shape[:axis] + (N_CHIPS * width,) + shard.shape[axis + 1:], F32)
        contrib = jnp.where(mc == 0, shard, 0.0)
        return lax.dynamic_update_slice_in_dim(z, contrib, chip * width, axis)

    sm_in = [in_place(a_b_in, 1), in_place(a_b_out, 1), in_place(f_conv_w, 2)]
    sm_full = _unpack_small(_allreduce_small(_pack_small(sm_in), name="gather_small"), [t.shape for t in sm_in])
    b_in_full, b_out_full, conv_w_full = sm_full

    W_in = _shards_to_cols(gathered["m_w_in"])
    W_all = jnp.pad(W_in, ((0, 0), (0, M_PROJ - W_in.shape[1])))
    gbias = jnp.pad(m_gate_bias[0].reshape(1, 2 * M_HEADS), ((0, 0), (0, LANES - 2 * M_HEADS)))

    grads = {}

    def ffn_fwd(i, z1, end):
        u = _mm(z1, W_up, b_layer=i, name=f"ffn_up{i}")
        a = _convgate_fwd(u, conv_w_full[i], f_conv_b[i:i + 1], name=f"ffn_act{i}")
        return u, a, _mm(a, W_down[i], rows=end, name=f"ffn_down{i}")

    proj = _mm(z0, W_all, name="mlstm_proj")
    gated, C_all, n_all, m_all, *late_theirs = _mlstm_fwd(proj, gbias, m_head_norm, [mine[n] for n in late],
                                                          name="mlstm_fwd")
    gathered.update(with_own_slot(late, late_theirs))
    W_mout = gathered["m_w_out"].reshape(D_MODEL, D_MODEL)
    A_in, A_out = _shards_to_cols(gathered["a_w_in"]), gathered["a_w_out"].reshape(D_MODEL, D_MODEL)
    W_up = gathered["f_w_up"]
    dsh = D_FF // N_CHIPS
    W_down = [gathered["f_w_down"][:, i * dsh:(i + 1) * dsh].reshape(D_FF, D_MODEL) for i in range(2)]
    zm0, h1, z1 = _mm(gated, W_mout, rows=_rows_sublayer_end(h0, norm_mix_post[0:1], norm_ffn_pre[0:1]), name="mlstm_out")
    u0, a0, (zf0, h2, y0) = ffn_fwd(0, z1, _rows_sublayer_end(h1, norm_ffn_post[0:1], norm_mix_pre[1:2]))

    aproj = _mm(y0, A_in, bias=b_in_full, name="attn_proj")
    sinks = a_sinks.reshape(A_KVH, A_G)
    ao = _attn_fwd(aproj, sinks, name="attn_fwd")
    zm1, h3, z3 = _mm(ao, A_out, bias=b_out_full, rows=_rows_sublayer_end(h2, norm_mix_post[1:2], norm_ffn_pre[1:2]),
                      name="attn_out")
    u1, a1, (zf1, dh, dzf1, g_post1, loss_lanes) = ffn_fwd(1, z3, _rows_last_sublayer_end(h3, norm_ffn_post[1:2], target))
    loss = lax.psum(loss_lanes[0, 0], ("x", "y", "c"))

    g_post, g_fpre, g_mpost, g_mpre = [None, g_post1], [None, None], [None, None], [None, None]
    dW_up, dW_down, dconv_w, dconv_b = [None, None], [None, None], [None, None], [None, None]

    def by_rows(g):
        return g.reshape(N_CHIPS, -1, g.shape[-1])

    def ffn_bwd(i, dzf, z1_, u, a, start, ride):
        da = _mm(dzf, W_down[i], tb=True, name=f"ffn_down_dx{i}")
        dW_down[i] = _mm(a, dzf, ta=True, name=f"ffn_down_dw{i}")
        du, dconv_w[i], dconv_b[i] = _convgate_bwd(u, da, conv_w_full[i], f_conv_b[i:i + 1], name=f"ffn_act_bwd{i}")
        dW_up[i] = _mm(z1_, du, ta=True, out_stacked=True, name=f"ffn_up_dw{i}")
        riders = (list(ride) + [by_rows(dW_down[i]), dW_up[i]]) if ride is not None else []
        outs = _mm(du, W_up, tb=True, b_layer=i, rows=start, ride=(_Swap, riders) if riders else None,
                   name=f"ffn_up_dx{i}")
        return outs[:len(start.outs)], riders, outs[len(start.outs):]

    (dh3, g_fpre[1], dzm1, g_mpost[1], db_out), _, _ = ffn_bwd(
        1, dzf1, z3, u1, a1, _rows_sublayer_start_bwd(h3, dh, zm1, norm_ffn_pre[1:2], norm_mix_post[1:2]), None)
    dao = _mm(dzm1, A_out, tb=True, name="attn_out_dx")
    dA_out = _mm(ao, dzm1, ta=True, name="attn_out_dw")
    daq, dakv, dsinks, cs_q, cs_kv = _attn_bwd(aproj, sinks, dao, name="attn_bwd")
    db_in = jnp.concatenate([cs_q, cs_kv], axis=1)
    dh2, g_mpre[1], dzf0, g_post[0], _ = _mm(
        daq, A_in[:, :A_NQ], tb=True, add=_mm(dakv, A_in[:, A_NQ:], tb=True, name="attn_proj_kv_dx"),
        rows=_rows_sublayer_start_bwd(h2, dh3, zf0, norm_mix_pre[1:2], norm_ffn_post[0:1]), name="attn_proj_q_dx")
    dA_in = jnp.concatenate([_mm(y0, daq, ta=True, name="attn_proj_q_dw"),
                             _mm(y0, dakv, ta=True, name="attn_proj_kv_dw")], axis=1)

    c_arr = jnp.reshape(mc, (1,)).astype(jnp.int32)
    place_arr = jnp.stack([chip, mc]).astype(jnp.int32)

    def add_halves(tags, gs, got):
        return [_add_halves(g, t, c_arr, name=f"grad_add_halves_{tag}") for tag, g, t in zip(tags, gs, got)]

    (dh1, g_fpre[0], dzm0, g_mpost[0], _), rode, rode_got = ffn_bwd(
        0, dzf0, z1, u0, a0, _rows_sublayer_start_bwd(h1, dh2, zm0, norm_ffn_pre[0:1], norm_mix_post[0:1]),
        [_cols_to_shards(dA_in), by_rows(dA_out), dW_up[1], by_rows(dW_down[1])])
    rode_tags = ["a_w_in", "a_w_out", "f_w_up1", "f_w_down1", "f_w_down0", "f_w_up0"]
    dgated = _mm(dzm0, W_mout, tb=True, name="mlstm_out_dx")
    dW_mout = _mm(gated, dzm0, ta=True, name="mlstm_out_dw")
    next_tags, next_gs = ["m_w_out"], [by_rows(dW_mout)]
    late_tags = rode_tags + next_tags
    late_part = (add_halves(rode_tags, rode, rode_got)
                 + add_halves(next_tags, next_gs, _swap_halves(next_gs, name="grad_swap_halves_late")))
    dproj, dgbias, dhn, *late_from = _mlstm_bwd(proj, gbias, m_head_norm, C_all, n_all, m_all, dgated, late_part,
                                                name="mlstm_bwd")
    dW_all = _mm(z0, dproj, ta=True, name="mlstm_proj_dw")
    dW_min = dW_all[:, :m_w_in.shape[-1] * N_CHIPS]

    early_tags, early_gs = ["m_w_in"], [_cols_to_shards(dW_min)]
    early_part = add_halves(early_tags, early_gs, _swap_halves(early_gs, name="grad_swap_halves_early"))
    grad_x, g_mpre[0], *early_from = _mm(
        dproj, W_all, tb=True, rows=_rows_first_sublayer_start_bwd(h0, dh1, norm_mix_pre[0:1]),
        ride=(_Exchange, early_part),
        name="mlstm_proj_dx")
    tags = early_tags + late_tags
    halves = [_sum_chips(q, r, place_arr, name=f"grad_sum_chips_{tag}")
              for tag, q, r in zip(tags, early_part + late_part, list(early_from) + list(late_from))]
    reduced = dict(zip(tags, _join_halves(halves, name="grad_join_halves")))
    layer_grads = dict(m_w_in=[reduced["m_w_in"]], m_w_out=[reduced["m_w_out"]], a_w_in=[reduced["a_w_in"]],
                       a_w_out=[reduced["a_w_out"]], f_w_up=[reduced["f_w_up0"], reduced["f_w_up1"]],
                       f_w_down=[reduced["f_w_down0"], reduced["f_w_down1"]])

    small_g = [
        dgbias[:, :2 * M_HEADS].reshape(1, 2, M_HEADS),
        dhn,
        dsinks.reshape(1, A_QH),
        db_in, db_out,
        jnp.concatenate(g_mpre), jnp.concatenate(g_mpost), jnp.concatenate(g_fpre), jnp.concatenate(g_post),
        jnp.stack(dconv_w), jnp.concatenate(dconv_b),
    ]
    small_names = ["m_gate_bias", "m_head_norm", "a_sinks", "a_b_in", "a_b_out", "norm_mix_pre", "norm_mix_post",
                   "norm_ffn_pre", "norm_ffn_post", "f_conv_w", "f_conv_b"]
    red = _unpack_small(_allreduce_small(_pack_small(small_g), name="reduce_small"), [t.shape for t in small_g])
    for n, t in zip(small_names, red):
        if n in ("a_b_in", "a_b_out", "f_conv_w"):
            axis = t.ndim - 1
            width = params[n].shape[axis]
            t = lax.dynamic_slice_in_dim(t, chip * width, width, axis)
        grads[n] = t

    deltas, new_m, new_v = {}, {}, {}
    for n in _BIG:
        grads[n], deltas[n], new_m[n], new_v[n] = _adamw_layers(params[n], layer_grads[n], mom1[n], mom2[n],
                                                                name=f"adamw_{n}")
    two = lambda t: t.reshape(-1, t.shape[-1])
    res = _adamw_small([(two(params[n]), two(grads[n]), two(mom1[n]), two(mom2[n])) for n in small_names],
                       name="adamw_small")
    for n, (d, nm, nv) in zip(small_names, res):
        sh = params[n].shape
        deltas[n], new_m[n], new_v[n] = d.reshape(sh), nm.reshape(sh), nv.reshape(sh)

    return (loss, grad_x[None], *[grads[n] for n in order], *[deltas[n] for n in order],
            *[new_m[n] for n in order], *[new_v[n] for n in order])
```

```python
import functools
import math

import numpy as np

import jax
import jax.numpy as jnp
from jax import lax
from jax.experimental import pallas as pl
from jax.experimental.pallas import tpu as pltpu

F32 = jnp.float32
MXU_DTYPE = jnp.bfloat16
WIRE_DTYPE = jnp.bfloat16
MESH = pl.DeviceIdType.MESH

D_MODEL = 1024
EPS = 1e-6
M_HEADS, M_QK, M_V, M_CHUNK = 8, 64, 128, 128
GATE_CAP = 15.0
A_DH, A_QH, A_KVH, A_G, A_BLK = 64, 16, 2, 8, 128
D_FF = 2816
N_CHIPS = 4
LANES = 128
VMEM_LIMIT = 56 * 1024 * 1024

ADAM_LR, ADAM_B1, ADAM_B2, ADAM_EPS, ADAM_WD, ADAM_STEP = 0.001, 0.9, 0.999, 1e-08, 0.01, 10


def _cparams(sem):
    return pltpu.CompilerParams(dimension_semantics=sem, vmem_limit_bytes=VMEM_LIMIT)


def _pick(n, cands):
    for c in cands:
        if n % c == 0:
            return c
    return n


class _Rows:
    def __init__(self, tiles, vecs, outs, fn):
        self.tiles, self.vecs, self.outs, self.fn = list(tiles), list(vecs), list(outs), fn


MM_VMEM_BUDGET = 46 * 1024 * 1024
ROWS_FULL_K = 4224
WIDE_N = 3200


def _mm(a, b, *, ta=False, tb=False, out_dtype=F32, bias=None, add=None, b_layer=None, out_stacked=False, rows=None,
        ride=None, name):
    if ta:
        K, M = a.shape
    else:
        M, K = a.shape
    b_stacked = b_layer is not None
    if b_stacked:
        assert not ta
        n_sh = b.shape[2]
        w_rows, w_cols = D_MODEL, N_CHIPS * n_sh
        N, Kb = (w_rows, w_cols) if tb else (w_cols, w_rows)
    elif tb:
        N, Kb = b.shape
    else:
        Kb, N = b.shape
    assert K == Kb, (a.shape, b.shape)
    tn = _pick(N, (1024, 1408, 1280, 640, 512, 256, 128))
    tk = K if K <= 2816 else _pick(K, (2816, 2048, 1408, 1024, 512, 256, 128))
    if tn < 1024 <= N <= WIDE_N:
        tn, tk = N, min(tk, 1024)
    shards_per_step = 1
    if b_stacked and tb:
        shards_per_step = N_CHIPS if rows is not None else 2
        tk = shards_per_step * n_sh
    if (b_stacked and not tb) or out_stacked:
        tn = N // N_CHIPS
    if rows is not None:
        assert tn == N and not out_stacked and not ta
        if K <= ROWS_FULL_K:
            tk = K
    nk = K // tk

    def vmem_bytes(rows_per_tile):
        total = 2 * (rows_per_tile * tk * a.dtype.itemsize + tk * tn * b.dtype.itemsize)
        if rows is None:
            total += 2 * rows_per_tile * tn * jnp.dtype(out_dtype).itemsize
        else:
            total += 2 * rows_per_tile * N * (4 * len(rows.tiles)
                                              + sum(jnp.dtype(dt).itemsize for kind, dt in rows.outs if kind == "tile"))
        if add is not None:
            total += 2 * rows_per_tile * tn * 4
        return total + (rows_per_tile * tn * 4 if nk > 1 else 0)

    tm = next(c for c in (2048, 1024, 1408, 512, 256, 128, M) if M % c == 0 and (vmem_bytes(c) <= MM_VMEM_BUDGET or c <= 128))
    dn = (((0 if ta else 1,), (1 if tb else 0,)), ((), ()))
    has_bias, has_add = bias is not None, add is not None
    n_tiles, n_vecs, n_outs = (len(rows.tiles), len(rows.vecs), len(rows.outs)) if rows is not None else (0, 0, 1)
    rider, riding = ride if ride is not None else (None, ())
    n_ex = len(riding)
    assert not n_ex or rows is not None
    grid = (M // tm, N // tn, nk)

    def body(*refs):
        a_ref, b_ref = refs[0], refs[1]
        pos = 2
        bias_ref = add_ref = None
        if has_bias:
            bias_ref = refs[pos]
            pos += 1
        if has_add:
            add_ref = refs[pos]
            pos += 1
        tile_refs, vec_refs = refs[pos:pos + n_tiles], refs[pos + n_tiles:pos + n_tiles + n_vecs]
        pos += n_tiles + n_vecs
        ex_in = refs[pos:pos + n_ex]
        pos += n_ex
        out_refs, ex_out = refs[pos:pos + n_outs], refs[pos + n_outs:pos + n_outs + n_ex]
        pos += n_outs + n_ex
        acc_ref = refs[pos] if nk > 1 else None
        if n_ex:
            ex = rider(ex_in, ex_out, *refs[pos + (nk > 1):])
            step = (pl.program_id(0) * grid[1] + pl.program_id(1)) * grid[2] + pl.program_id(2)

            @pl.when(step == 0)
            def _():
                ex.issue()

        if rows is not None:
            @pl.when((pl.program_id(0) == 0) & (pl.program_id(2) == 0))
            def _():
                for (kind, _), o_ref in zip(rows.outs, out_refs):
                    if kind == "vec":
                        o_ref[...] = jnp.zeros_like(o_ref)

        def finish(r):
            if has_bias:
                r = r + bias_ref[...]
            if has_add:
                r = r + add_ref[...]
            if rows is None:
                out_refs[0][...] = r.astype(out_dtype)
                return
            vals = rows.fn(r, [t[...] for t in tile_refs], [v[...] for v in vec_refs])
            for (kind, _), o_ref, val in zip(rows.outs, out_refs, vals):
                if kind == "tile":
                    o_ref[...] = val.astype(o_ref.dtype)
                else:
                    o_ref[...] += val

        if shards_per_step > 1:
            part = sum(lax.dot_general(a_ref[:, s * n_sh:(s + 1) * n_sh], b_ref[s], dn, preferred_element_type=F32)
                       for s in range(shards_per_step))
        else:
            part = lax.dot_general(a_ref[...], b_ref[...], dn, preferred_element_type=F32)
        if nk == 1:
            finish(part)
        else:
            k = pl.program_id(2)

            @pl.when(k == 0)
            def _():
                acc_ref[...] = part

            @pl.when(k > 0)
            def _():
                acc_ref[...] += part

            @pl.when(k == nk - 1)
            def _():
                finish(acc_ref[...])

        if n_ex:
            @pl.when(step == grid[0] * grid[1] * grid[2] - 1)
            def _():
                ex.finish()

    a_spec = pl.BlockSpec((tk, tm), lambda i, j, k: (k, i)) if ta else pl.BlockSpec((tm, tk), lambda i, j, k: (i, k))
    if b_stacked and tb:
        off = b_layer * (w_rows // tn)
        b_spec = pl.BlockSpec((shards_per_step, tn, n_sh), lambda i, j, k: (k, off + j, 0))
    elif b_stacked:
        off = b_layer * (w_rows // tk)
        b_spec = pl.BlockSpec((None, tk, tn), lambda i, j, k: (j, off + k, 0))
    elif tb:
        b_spec = pl.BlockSpec((tn, tk), lambda i, j, k: (j, k))
    else:
        b_spec = pl.BlockSpec((tk, tn), lambda i, j, k: (k, j))
    if out_stacked:
        out_spec = pl.BlockSpec((None, tm, tn), lambda i, j, k: (j, i, 0))
        out_shape = jax.ShapeDtypeStruct((N_CHIPS, M, tn), out_dtype)
    else:
        out_spec = pl.BlockSpec((tm, tn), lambda i, j, k: (i, j))
        out_shape = jax.ShapeDtypeStruct((M, N), out_dtype)
    in_specs, args = [a_spec, b_spec], [a, b]
    if has_bias:
        in_specs.append(pl.BlockSpec((1, tn), lambda i, j, k: (0, j)))
        args.append(bias)
    if has_add:
        in_specs.append(pl.BlockSpec((tm, tn), lambda i, j, k: (i, j)))
        args.append(add)
    semantics = ("parallel", "parallel", "arbitrary")
    if rows is not None:
        tile_spec = pl.BlockSpec((tm, N), lambda i, j, k: (i, 0))
        vec_spec = pl.BlockSpec((1, N), lambda i, j, k: (0, 0))
        in_specs += [tile_spec] * n_tiles + [vec_spec] * n_vecs
        args += rows.tiles + rows.vecs
        out_spec = [tile_spec if kind == "tile" else vec_spec for kind, _ in rows.outs]
        out_shape = [jax.ShapeDtypeStruct((M, N) if kind == "tile" else (1, N), dt) for kind, dt in rows.outs]
        semantics = ("arbitrary", "arbitrary", "arbitrary")
    scratch = [pltpu.VMEM((tm, tn), F32)] if nk > 1 else []
    if n_ex:
        any_spec = pl.BlockSpec(memory_space=pl.ANY)
        in_specs += [any_spec] * n_ex
        args += list(riding)
        out_spec = out_spec + [any_spec] * n_ex
        out_shape = out_shape + rider.out_shape(riding)
        scratch += rider.semaphores(n_ex)
    return pl.pallas_call(
        body, name=name,
        grid=(M // tm, N // tn, nk),
        in_specs=in_specs,
        out_specs=out_spec,
        out_shape=out_shape,
        scratch_shapes=scratch,
        compiler_params=_cparams(semantics),
    )(*args)


ROW_TILE = 512


def _rms_fwd(x, g, shards, *, out_dtype, name):
    S, D = x.shape
    tm = _pick(S, (ROW_TILE, 256, 128))
    nsh, nt = len(shards), S // tm
    any_spec = pl.BlockSpec(memory_space=pl.ANY)

    def body(x_ref, g_ref, *rest):
        o_ref = rest[nsh]
        gather = _Gather(rest[:nsh], rest[nsh + 1:2 * nsh + 1], *rest[2 * nsh + 1:])

        @pl.when(pl.program_id(0) == 0)
        def _():
            gather.issue()

        o_ref[...] = _rms(x_ref[...], g_ref[...]).astype(out_dtype)

        @pl.when(pl.program_id(0) == nt - 1)
        def _():
            gather.finish()

    row = pl.BlockSpec((tm, D), lambda i: (i, 0))
    return pl.pallas_call(
        body, name=name, grid=(nt,),
        in_specs=[row, pl.BlockSpec((1, D), lambda i: (0, 0))] + [any_spec] * nsh, out_specs=[row] + [any_spec] * nsh,
        out_shape=[jax.ShapeDtypeStruct((S, D), out_dtype)] + _Gather.out_shape(shards),
        scratch_shapes=_Gather.semaphores(nsh),
        compiler_params=_cparams(("arbitrary",)),
    )(x, g, *shards)


def _rms(x, g):
    return x * lax.rsqrt(jnp.mean(x * x, axis=-1, keepdims=True) + EPS) * g


def _rms_vjp(x, g, dy):
    r = lax.rsqrt(jnp.mean(x * x, axis=-1, keepdims=True) + EPS)
    xh = x * r
    gd = dy * g
    dx = r * (gd - xh * jnp.mean(gd * xh, axis=-1, keepdims=True))
    return dx, jnp.sum(dy * xh, axis=0, keepdims=True)


def _rows_sublayer_end(res, g_post, g_pre_next):
    def fn(z, tiles, vecs):
        h = tiles[0] + _rms(z, vecs[0])
        return [z, h, _rms(h, vecs[1])]

    return _Rows([res], [g_post, g_pre_next], [("tile", F32), ("tile", F32), ("tile", MXU_DTYPE)], fn)


def _rows_last_sublayer_end(res, g_post, target):
    def fn(z, tiles, vecs):
        y = tiles[0] + _rms(z, vecs[0])
        err = y - tiles[1]
        dy = err * (1.0 / err.shape[-1])
        dz, dg = _rms_vjp(z, vecs[0], dy)
        loss = 0.5 * jnp.sum(jnp.mean(err * err, axis=-1, keepdims=True), axis=0, keepdims=True)
        return [z, dy, dz, dg, jnp.broadcast_to(loss, dg.shape)]

    return _Rows([res, target], [g_post], [("tile", F32), ("tile", F32), ("tile", MXU_DTYPE), ("vec", F32), ("vec", F32)], fn)


def _rows_sublayer_start_bwd(x, res, z_below, g_pre, g_post_below):
    def fn(dz, tiles, vecs):
        dx, dg_pre = _rms_vjp(tiles[0], vecs[0], dz)
        dh = tiles[1] + dx
        dzb, dg_post = _rms_vjp(tiles[2], vecs[1], dh)
        return [dh, dg_pre, dzb, dg_post, jnp.sum(dzb, axis=0, keepdims=True)]

    return _Rows([x, res, z_below], [g_pre, g_post_below],
                 [("tile", F32), ("vec", F32), ("tile", MXU_DTYPE), ("vec", F32), ("vec", F32)], fn)


def _rows_first_sublayer_start_bwd(x, res, g_pre):
    def fn(dz, tiles, vecs):
        dx, dg_pre = _rms_vjp(tiles[0], vecs[0], dz)
        return [tiles[1] + dx, dg_pre]

    return _Rows([x, res], [g_pre], [("tile", F32), ("vec", F32)], fn)


def _mx(t):
    return t.astype(MXU_DTYPE)


def _mxf(t):
    return t.astype(MXU_DTYPE).astype(F32)


def _rawdot(a, b, ca, cb):
    return lax.dot_general(_mx(a), _mx(b), (((ca,), (cb,)), ((), ())), preferred_element_type=F32)


@functools.partial(jax.custom_vjp, nondiff_argnums=(2, 3))
def _dot(a, b, ca, cb):
    return _rawdot(a, b, ca, cb)


def _dot_fwd(a, b, ca, cb):
    return _rawdot(a, b, ca, cb), (a, b)


def _dot_bwd(ca, cb, res, g):
    a, b = res
    bj = 1 if cb == 0 else 0
    ai = 0 if ca == 1 else 1
    da = _rawdot(g, b, 1, bj) if ca == 1 else _rawdot(b, g, bj, 1)
    db = _rawdot(a, g, ai, 0) if cb == 0 else _rawdot(g, a, 0, ai)
    return da.astype(a.dtype), db.astype(b.dtype)


_dot.defvjp(_dot_fwd, _dot_bwd)


def _softcap(z):
    return GATE_CAP * jnp.tanh(z / GATE_CAP)


def _log_sigmoid(z):
    return jnp.minimum(z, 0.0) - jnp.log(1.0 + jnp.exp(-jnp.abs(z)))


def _sigmoid(z):
    return 0.5 * jnp.tanh(0.5 * z) + 0.5


def _lane_col(t, lane_index):
    lane = lax.broadcasted_iota(jnp.int32, t.shape, 1)
    return jnp.sum(jnp.where(lane == lane_index, t, 0.0), axis=1, keepdims=True)


def _mlstm_gates(G, bias):
    L = G.shape[0]
    z = _softcap(G + bias)
    ig = z
    lf = _log_sigmoid(z)
    ti = lax.broadcasted_iota(jnp.int32, (L, L), 0)
    si = lax.broadcasted_iota(jnp.int32, (L, L), 1)
    tril = (si <= ti).astype(F32)
    b = lax.dot_general(tril, lf, (((1,), (0,)), ((), ())), precision=lax.Precision.HIGHEST, preferred_element_type=F32)
    bL = jnp.sum(lf, axis=0, keepdims=True)
    return ig, b, ig.T, b.T, bL


def _mlstm_head(h, q, k_pair, v, o, ig_all, b_all, igT, bT, bL_all, hn, C, n, m):
    L = q.shape[0]
    k = jnp.where(_own_lanes(h, k_pair.shape), k_pair, 0.0)
    ti = lax.broadcasted_iota(jnp.int32, (L, L), 0)
    si = lax.broadcasted_iota(jnp.int32, (L, L), 1)
    lower = (si <= ti)
    ig = _lane_col(ig_all, h)
    b = _lane_col(b_all, M_HEADS + h)
    ig_row = igT[h:h + 1, :]
    b_row = bT[M_HEADS + h:M_HEADS + h + 1, :]
    bL = _lane_col(bL_all, M_HEADS + h)
    inter = b + m
    dlog = jnp.where(lower, b - b_row + ig_row, -jnp.inf)
    m_t = lax.stop_gradient(jnp.maximum(inter, jnp.max(dlog, axis=-1, keepdims=True)))
    qs = q * (M_QK ** -0.5)
    w = _dot(qs, k, 1, 1) * jnp.exp(dlog - m_t)
    s_inter = jnp.exp(inter - m_t)
    num = _dot(w, v, 1, 0) + s_inter * _dot(qs, C, 1, 0)
    den = jnp.sum(w, axis=-1, keepdims=True) + s_inter * jnp.sum(_mxf(qs) * _mxf(n), axis=-1, keepdims=True)
    hout = num * (1.0 / jnp.maximum(jnp.abs(den), jnp.exp(-m_t)))
    tail = bL - b + ig
    m_new = lax.stop_gradient(jnp.maximum(bL + m, jnp.max(tail, axis=0, keepdims=True)))
    ws = jnp.exp(tail - m_new)
    decay = jnp.exp(bL + m - m_new)
    wk = ws * k
    C_new = decay * C + _dot(wk, v, 0, 0)
    n_new = decay * n + jnp.sum(_mxf(ws) * _mxf(k), axis=0, keepdims=True)
    hs = hout * lax.rsqrt(jnp.mean(hout * hout, axis=-1, keepdims=True) + EPS) * hn
    gated = _sigmoid(o) * hs
    return (gated, C_new, n_new), m_new


M_OFF_Q, M_OFF_K = 0, M_HEADS * M_QK
M_OFF_V = 2 * M_HEADS * M_QK
M_OFF_O = M_OFF_V + M_HEADS * M_V
M_OFF_G = M_OFF_O + M_HEADS * M_V
M_PROJ = M_OFF_G + LANES
M_PAIRS = M_HEADS * M_QK // LANES


def _head_cols(off, h):
    return slice(off + h * LANES, off + (h + 1) * LANES)


def _own_lanes(h, shape):
    low = lax.broadcasted_iota(jnp.int32, shape, 1) < M_QK
    return low if h % 2 == 0 else jnp.logical_not(low)


def _mlstm_specs(NC, rev):
    H, L = M_HEADS, M_CHUNK
    cc = (lambda c: NC - 1 - c) if rev else (lambda c: c)
    proj = pl.BlockSpec((L, M_PROJ), lambda c: (cc(c), 0))
    vec = pl.BlockSpec((1, LANES), lambda c: (0, 0))
    hn = pl.BlockSpec((1, H * M_V), lambda c: (0, 0))
    hv = pl.BlockSpec((L, H * M_V), lambda c: (cc(c), 0))
    Cs = pl.BlockSpec((None, H, LANES, M_V), lambda c: (cc(c), 0, 0, 0))
    ns = pl.BlockSpec((None, H, 1, LANES), lambda c: (cc(c), 0, 0, 0))
    ms = pl.BlockSpec((None, H, 1, 1), lambda c: (cc(c), 0, 0, 0))
    return proj, vec, hn, hv, Cs, ns, ms


_MLSTM_STATE = [pltpu.VMEM((M_HEADS, LANES, M_V), F32), pltpu.VMEM((M_HEADS, 1, LANES), F32),
                pltpu.VMEM((M_HEADS, 1, 1), F32)]


def _mlstm_fwd(proj, bias, hn, shards, *, name):
    S = proj.shape[0]
    H, NC = M_HEADS, S // M_CHUNK
    ps, vec, hns, hv, Cs, ns, ms = _mlstm_specs(NC, False)
    nsh = len(shards)
    any_spec = pl.BlockSpec(memory_space=pl.ANY)

    def body(p_ref, b_ref, hn_ref, *rest):
        gated_ref, C_all, n_all, m_all = rest[nsh:nsh + 4]
        C_s, n_s, m_s, send_sems, recv_sems = rest[2 * nsh + 4:]
        gather = _Gather(rest[:nsh], rest[nsh + 4:2 * nsh + 4], send_sems, recv_sems)

        @pl.when(pl.program_id(0) == 0)
        def _():
            C_s[...] = jnp.zeros_like(C_s)
            n_s[...] = jnp.zeros_like(n_s)
            m_s[...] = jnp.zeros_like(m_s)
            gather.issue()

        gate_terms = _mlstm_gates(p_ref[:, M_OFF_G:M_OFF_G + LANES], b_ref[...])
        for h in range(H):
            C, n, m = C_s[h], n_s[h], m_s[h]
            C_all[h] = C
            n_all[h] = n
            m_all[h] = m
            (gated, Cn, nn), mn = _mlstm_head(
                h, p_ref[:, _head_cols(M_OFF_Q, h // 2)], p_ref[:, _head_cols(M_OFF_K, h // 2)],
                p_ref[:, _head_cols(M_OFF_V, h)], p_ref[:, _head_cols(M_OFF_O, h)], *gate_terms,
                hn_ref[:, _head_cols(0, h)], C, n, m)
            gated_ref[:, _head_cols(0, h)] = gated.astype(gated_ref.dtype)
            C_s[h] = Cn
            n_s[h] = nn
            m_s[h] = mn

        @pl.when(pl.program_id(0) == NC - 1)
        def _():
            gather.finish()

    return pl.pallas_call(
        body, name=name, grid=(NC,),
        in_specs=[ps, vec, hns] + [any_spec] * nsh,
        out_specs=[hv, Cs, ns, ms] + [any_spec] * nsh,
        out_shape=[jax.ShapeDtypeStruct((S, H * M_V), MXU_DTYPE),
                   jax.ShapeDtypeStruct((NC, H, LANES, M_V), F32),
                   jax.ShapeDtypeStruct((NC, H, 1, LANES), F32),
                   jax.ShapeDtypeStruct((NC, H, 1, 1), F32)] + _Gather.out_shape(shards),
        scratch_shapes=list(_MLSTM_STATE) + _Gather.semaphores(nsh),
        compiler_params=_cparams(("arbitrary",)),
    )(proj, bias, hn, *shards)


def _mlstm_bwd(proj, bias, hn, C_all, n_all, m_all, dgated, qs, *, name):
    S = proj.shape[0]
    H, NC = M_HEADS, S // M_CHUNK
    ps, vec, hns, hv, Cs, ns, ms = _mlstm_specs(NC, True)
    nq = len(qs)
    any_spec = pl.BlockSpec(memory_space=pl.ANY)

    def body(p_ref, b_ref, hn_ref, C_ref, n_ref, m_ref, dg_ref, *rest):
        dp_ref, db_ref, dhn_ref = rest[nq:nq + 3]
        dC_s, dn_s, send_sems, recv_sems = rest[2 * nq + 3:]
        exchange = _Exchange(rest[:nq], rest[nq + 3:2 * nq + 3], send_sems, recv_sems)

        @pl.when(pl.program_id(0) == 0)
        def _():
            dC_s[...] = jnp.zeros_like(dC_s)
            dn_s[...] = jnp.zeros_like(dn_s)
            db_ref[...] = jnp.zeros_like(db_ref)
            dhn_ref[...] = jnp.zeros_like(dhn_ref)
            exchange.issue()

        gate_terms, gates_vjp = jax.vjp(_mlstm_gates, p_ref[:, M_OFF_G:M_OFF_G + LANES], b_ref[...])
        d_terms = [jnp.zeros_like(t) for t in gate_terms]
        for h in range(H):
            def head(q, k, v, o, *rest, h=h):
                return _mlstm_head(h, q, k, v, o, *rest, m_ref[h])

            prim = (p_ref[:, _head_cols(M_OFF_Q, h // 2)], p_ref[:, _head_cols(M_OFF_K, h // 2)],
                    p_ref[:, _head_cols(M_OFF_V, h)], p_ref[:, _head_cols(M_OFF_O, h)], *gate_terms,
                    hn_ref[:, _head_cols(0, h)], C_ref[h], n_ref[h])
            _, vjp, _ = jax.vjp(head, *prim, has_aux=True)
            dq, dk, dv, do, *d_gate, dhnh, dC, dn = vjp((dg_ref[:, _head_cols(0, h)].astype(F32), dC_s[h], dn_s[h]))
            if h % 2 == 0:
                dq_pair, dk_pair = dq, dk
            else:
                dp_ref[:, _head_cols(M_OFF_Q, h // 2)] = (dq_pair + dq).astype(dp_ref.dtype)
                dp_ref[:, _head_cols(M_OFF_K, h // 2)] = (dk_pair + dk).astype(dp_ref.dtype)
            dp_ref[:, _head_cols(M_OFF_V, h)] = dv.astype(dp_ref.dtype)
            dp_ref[:, _head_cols(M_OFF_O, h)] = do.astype(dp_ref.dtype)
            d_terms = [a + g for a, g in zip(d_terms, d_gate)]
            dhn_ref[:, _head_cols(0, h)] += dhnh
            dC_s[h] = dC
            dn_s[h] = dn
        dG, dbias = gates_vjp(tuple(d_terms))
        dp_ref[:, M_OFF_G:M_OFF_G + LANES] = dG.astype(dp_ref.dtype)
        db_ref[...] += dbias

        @pl.when(pl.program_id(0) == NC - 1)
        def _():
            exchange.finish()

    return pl.pallas_call(
        body, name=name, grid=(NC,),
        in_specs=[ps, vec, hns, Cs, ns, ms, hv] + [any_spec] * nq,
        out_specs=[ps, vec, hns] + [any_spec] * nq,
        out_shape=[jax.ShapeDtypeStruct((S, M_PROJ), MXU_DTYPE), jax.ShapeDtypeStruct((1, LANES), F32),
                   jax.ShapeDtypeStruct((1, H * M_V), F32)] + _Exchange.out_shape(qs),
        scratch_shapes=list(_MLSTM_STATE[:2]) + _Exchange.semaphores(nq),
        compiler_params=_cparams(("arbitrary",)),
    )(proj, bias, hn, C_all, n_all, m_all, dgated, *qs)


A_NQ = A_QH * A_DH
A_NKV = 2 * A_KVH * A_DH
A_PAIRS = A_G // 2


def _sink_softmax_parts(s, sink):
    mx = jnp.maximum(jnp.max(s, axis=-1, keepdims=True), sink)
    e = jnp.exp(s - mx)
    e_sink = jnp.exp(sink - mx)
    r = 1.0 / (jnp.sum(e, axis=-1, keepdims=True) + e_sink)
    return e * r, e_sink * r


@jax.custom_vjp
def _sink_softmax(s, sink):
    return _sink_softmax_parts(s, sink)[0]


def _sink_softmax_fwd(s, sink):
    p, p_sink = _sink_softmax_parts(s, sink)
    return p, (p, p_sink)


def _sink_softmax_bwd(res, dp):
    p, p_sink = res
    t = jnp.sum(dp * p, axis=-1, keepdims=True)
    return p * (dp - t), -p_sink * t


_sink_softmax.defvjp(_sink_softmax_fwd, _sink_softmax_bwd)


def _band_bias():
    qi = np.arange(A_G * A_BLK)[:, None] % A_BLK
    ku = np.arange(2 * A_BLK)[None, :]
    diff = qi - (ku - A_BLK)
    band = (diff >= 0) & (diff < A_BLK)
    seen = np.stack([band, band & (ku >= A_BLK)])
    return jnp.asarray(np.where(seen, 0.0, -np.inf), F32)


_BAND_BIAS_SPEC = pl.BlockSpec((None, A_G * A_BLK, 2 * A_BLK), lambda n: (jnp.where(n == 0, 1, 0), 0, 0))


def _attn_group(Ps, KLO, KHI, VLO, VHI, sinks, bias):
    B = Ps[0].shape[0]
    R = len(Ps) * B
    q2 = jnp.concatenate(Ps, axis=0) * (A_DH ** -0.5)
    s = jnp.concatenate([_dot(q2, KLO, 1, 1), _dot(q2, KHI, 1, 1)], axis=0) + bias
    ri = lax.broadcasted_iota(jnp.int32, (2 * R, A_G), 0)
    head = 2 * lax.shift_right_logical(ri & (R - 1), B.bit_length() - 1) + lax.shift_right_logical(ri, R.bit_length() - 1)
    onehot = head == lax.broadcasted_iota(jnp.int32, (2 * R, A_G), 1)
    sink = jnp.sum(jnp.where(onehot, sinks, 0.0), axis=1, keepdims=True)
    p = _sink_softmax(s, sink)
    o = _dot(p[:R], VLO, 1, 0) + _dot(p[R:], VHI, 1, 0)
    return tuple(o[j * B:(j + 1) * B] for j in range(len(Ps)))


def _swap_halves_of_lanes(t):
    return pltpu.roll(t, LANES // 2, 1)


def _kv_operands(kvp_ref, kvc_ref, h):
    kk = jnp.concatenate([kvp_ref[:, :LANES], kvc_ref[:, :LANES]], axis=0)
    vv = jnp.concatenate([kvp_ref[:, LANES:], kvc_ref[:, LANES:]], axis=0)
    low = lax.broadcasted_iota(jnp.int32, kk.shape, 1) < A_DH
    own = low if h == 0 else jnp.logical_not(low)
    k_own = jnp.where(own, kk, 0.0)
    v_own = jnp.where(own, vv, 0.0)
    k_oth, v_oth = _swap_halves_of_lanes(k_own), _swap_halves_of_lanes(v_own)
    if h == 0:
        return own, k_own, k_oth, v_own, v_oth
    return own, k_oth, k_own, v_oth, v_own


def _pair_cols(h, j):
    c = (h * A_PAIRS + j) * LANES
    return slice(c, c + LANES)


def _attn_fwd(proj, sinks, *, name):
    S = proj.shape[0]
    NB = S // A_BLK
    kv_blk = A_NQ // A_NKV
    qs = pl.BlockSpec((A_BLK, A_NQ), lambda n: (n, 0))
    cur = pl.BlockSpec((A_BLK, A_NKV), lambda n: (n, kv_blk))
    prev = pl.BlockSpec((A_BLK, A_NKV), lambda n: (jnp.maximum(n - 1, 0), kv_blk))
    sk = pl.BlockSpec((A_KVH, A_G), lambda n: (0, 0))

    def body(q_ref, kvp_ref, kvc_ref, s_ref, bias_ref, o_ref):
        for h in range(A_KVH):
            _, KLO, KHI, VLO, VHI = _kv_operands(kvp_ref, kvc_ref, h)
            Ps = tuple(q_ref[:, _pair_cols(h, j)] for j in range(A_PAIRS))
            outs = _attn_group(Ps, KLO, KHI, VLO, VHI, s_ref[h:h + 1, :], bias_ref[...])
            for j in range(A_PAIRS):
                o_ref[:, _pair_cols(h, j)] = outs[j].astype(o_ref.dtype)

    return pl.pallas_call(
        body, name=name, grid=(NB,),
        in_specs=[qs, prev, cur, sk, _BAND_BIAS_SPEC], out_specs=qs,
        out_shape=jax.ShapeDtypeStruct((S, A_NQ), MXU_DTYPE),
        compiler_params=_cparams(("parallel",)),
    )(proj, proj, proj, sinks, _band_bias())


def _attn_bwd(proj, sinks, do, *, name):
    S = proj.shape[0]
    NB = S // A_BLK
    last = NB - 1
    kv_blk = A_NQ // A_NKV
    qs = pl.BlockSpec((A_BLK, A_NQ), lambda n: (jnp.minimum(n, last), 0))
    cur = pl.BlockSpec((A_BLK, A_NKV), lambda n: (jnp.minimum(n, last), kv_blk))
    prev = pl.BlockSpec((A_BLK, A_NKV), lambda n: (jnp.clip(n - 1, 0, last), kv_blk))
    sk = pl.BlockSpec((A_KVH, A_G), lambda n: (0, 0))
    lag = pl.BlockSpec((A_BLK, A_NKV), lambda n: (jnp.maximum(n - 1, 0), 0))
    cq_spec = pl.BlockSpec((1, A_NQ), lambda n: (0, 0))
    ckv_spec = pl.BlockSpec((1, A_NKV), lambda n: (0, 0))

    def body(q_ref, kvp_ref, kvc_ref, s_ref, do_ref, bias_ref, dq_ref, dkv_ref, ds_ref, cq_ref, ckv_ref, keep):
        n = pl.program_id(0)

        @pl.when(n == 0)
        def _():
            keep[...] = jnp.zeros_like(keep)
            ds_ref[...] = jnp.zeros_like(ds_ref)
            cq_ref[...] = jnp.zeros_like(cq_ref)
            ckv_ref[...] = jnp.zeros_like(ckv_ref)

        @pl.when(n < NB)
        def _():
            f = functools.partial(_attn_group, bias=bias_ref[...])
            dk = jnp.zeros((2 * A_BLK, LANES), F32)
            dv = jnp.zeros((2 * A_BLK, LANES), F32)
            ds_rows = []
            for h in range(A_KVH):
                own, KLO, KHI, VLO, VHI = _kv_operands(kvp_ref, kvc_ref, h)
                Ps = tuple(q_ref[:, _pair_cols(h, j)] for j in range(A_PAIRS))
                _, vjp = jax.vjp(f, Ps, KLO, KHI, VLO, VHI, s_ref[h:h + 1, :])
                dPs, dKLO, dKHI, dVLO, dVHI, dsk = vjp(
                    tuple(do_ref[:, _pair_cols(h, j)].astype(F32) for j in range(A_PAIRS)))
                for j in range(A_PAIRS):
                    dq_ref[:, _pair_cols(h, j)] = dPs[j].astype(dq_ref.dtype)
                    cq_ref[:, _pair_cols(h, j)] += jnp.sum(dPs[j], axis=0, keepdims=True)
                dk_own, dk_oth = (dKLO, dKHI) if h == 0 else (dKHI, dKLO)
                dv_own, dv_oth = (dVLO, dVHI) if h == 0 else (dVHI, dVLO)
                dk = dk + jnp.where(own, dk_own, 0.0) + _swap_halves_of_lanes(jnp.where(own, 0.0, dk_oth))
                dv = dv + jnp.where(own, dv_own, 0.0) + _swap_halves_of_lanes(jnp.where(own, 0.0, dv_oth))
                ds_rows.append(dsk)
            ds_ref[...] += jnp.concatenate(ds_rows, axis=0)
            dkv = jnp.concatenate([dk, dv], axis=1)
            done = keep[...] + dkv[:A_BLK]
            dkv_ref[...] = done.astype(dkv_ref.dtype)
            ckv_ref[...] += jnp.sum(done, axis=0, keepdims=True)
            keep[...] = dkv[A_BLK:]

        @pl.when(n == NB)
        def _():
            done = keep[...]
            dkv_ref[...] = done.astype(dkv_ref.dtype)
            ckv_ref[...] += jnp.sum(done, axis=0, keepdims=True)

    return pl.pallas_call(
        body, name=name, grid=(NB + 1,),
        in_specs=[qs, prev, cur, sk, qs, _BAND_BIAS_SPEC],
        out_specs=[qs, lag, sk, cq_spec, ckv_spec],
        out_shape=[jax.ShapeDtypeStruct((S, A_NQ), MXU_DTYPE), jax.ShapeDtypeStruct((S, A_NKV), MXU_DTYPE),
                   jax.ShapeDtypeStruct((A_KVH, A_G), F32), jax.ShapeDtypeStruct((1, A_NQ), F32),
                   jax.ShapeDtypeStruct((1, A_NKV), F32)],
        scratch_shapes=[pltpu.VMEM((A_BLK, A_NKV), F32)],
        compiler_params=_cparams(("arbitrary",)),
    )(proj, proj, proj, sinks, do, _band_bias())


HALO = 8


def _shift_rows(t, j):
    return pltpu.roll(t, j % t.shape[0], 0)


def _conv_taps(gate_ext):
    return _shift_rows(gate_ext, 2), _shift_rows(gate_ext, 1), gate_ext


def _conv_gate(g2, g1, g0, cw, cb):
    return cb + cw[0:1, :] * g2 + cw[1:2, :] * g1 + cw[2:3, :] * g0


def _convgate_fwd(u, cw, cb, *, name):
    S, F2 = u.shape
    F = F2 // 2
    tm = _pick(S, (512, 256, 128))
    hb = tm // HALO
    urow = pl.BlockSpec((tm, F2), lambda i: (i, 0))
    uprev = pl.BlockSpec((HALO, F), lambda i: (jnp.maximum(i * hb - 1, 0), 0))

    def body(u_ref, up_ref, cw_ref, cb_ref, a_ref):
        i = pl.program_id(0)
        prev = jnp.where(i > 0, up_ref[...], 0.0)
        gc = _conv_gate(*_conv_taps(jnp.concatenate([prev, u_ref[:, :F]], axis=0)), cw_ref[...], cb_ref[...])[HALO:]
        a_ref[...] = (gc * _sigmoid(gc) * u_ref[:, F:]).astype(a_ref.dtype)

    return pl.pallas_call(
        body, name=name, grid=(S // tm,),
        in_specs=[urow, uprev, pl.BlockSpec((3, F), lambda i: (0, 0)), pl.BlockSpec((1, F), lambda i: (0, 0))],
        out_specs=pl.BlockSpec((tm, F), lambda i: (i, 0)),
        out_shape=jax.ShapeDtypeStruct((S, F), MXU_DTYPE),
        compiler_params=_cparams(("parallel",)),
    )(u, u, cw, cb)


def _convgate_bwd(u, da, cw, cb, *, name):
    S, F2 = u.shape
    F = F2 // 2
    tm = _pick(S, (256, 128))
    hb = tm // HALO
    nt = S // tm
    nh = S // HALO
    urow = pl.BlockSpec((tm, F2), lambda i: (i, 0))
    uprev = pl.BlockSpec((HALO, F), lambda i: (jnp.maximum(i * hb - 1, 0), 0))
    unext = pl.BlockSpec((HALO, F2), lambda i: (jnp.minimum((i + 1) * hb, nh - 1), 0))
    darow = pl.BlockSpec((tm, F), lambda i: (i, 0))
    danext = pl.BlockSpec((HALO, F), lambda i: (jnp.minimum((i + 1) * hb, nh - 1), 0))

    def body(u_ref, up_ref, un_ref, da_ref, dan_ref, cw_ref, cb_ref, du_ref, dcw_ref, dcb_ref):
        i = pl.program_id(0)
        cwv = cw_ref[...]
        prev = jnp.where(i > 0, up_ref[...], 0.0)
        gate_ext = jnp.concatenate([prev, u_ref[:, :F], un_ref[:, :F]], axis=0)
        val_ext = jnp.concatenate([u_ref[:, F:], un_ref[:, F:]], axis=0)
        da_next = jnp.where(i < nt - 1, dan_ref[...].astype(F32), 0.0)
        da_ext = jnp.concatenate([da_ref[...].astype(F32), da_next], axis=0)
        g2, g1, g0 = (t[HALO:] for t in _conv_taps(gate_ext))
        gc = _conv_gate(g2, g1, g0, cwv, cb_ref[...])
        sg = _sigmoid(gc)
        silu = gc * sg
        dval = da_ext * silu
        dgc = da_ext * val_ext * (sg * (1.0 + gc * (1.0 - sg)))
        dgate = cwv[2:3, :] * dgc + cwv[1:2, :] * _shift_rows(dgc, -1) + cwv[0:1, :] * _shift_rows(dgc, -2)
        du_ref[:, :F] = dgate[:tm].astype(du_ref.dtype)
        du_ref[:, F:] = dval[:tm].astype(du_ref.dtype)
        dgc_c = dgc[:tm]
        dcw = jnp.concatenate([jnp.sum(dgc_c * g2[:tm], axis=0, keepdims=True),
                               jnp.sum(dgc_c * g1[:tm], axis=0, keepdims=True),
                               jnp.sum(dgc_c * g0[:tm], axis=0, keepdims=True)], axis=0)
        dcb = jnp.sum(dgc_c, axis=0, keepdims=True)

        @pl.when(i == 0)
        def _():
            dcw_ref[...] = dcw
            dcb_ref[...] = dcb

        @pl.when(i > 0)
        def _():
            dcw_ref[...] += dcw
            dcb_ref[...] += dcb

    return pl.pallas_call(
        body, name=name, grid=(nt,),
        in_specs=[urow, uprev, unext, darow, danext,
                  pl.BlockSpec((3, F), lambda i: (0, 0)), pl.BlockSpec((1, F), lambda i: (0, 0))],
        out_specs=[urow, pl.BlockSpec((3, F), lambda i: (0, 0)), pl.BlockSpec((1, F), lambda i: (0, 0))],
        out_shape=[jax.ShapeDtypeStruct((S, F2), MXU_DTYPE), jax.ShapeDtypeStruct((3, F), F32),
                   jax.ShapeDtypeStruct((1, F), F32)],
        compiler_params=_cparams(("arbitrary",)),
    )(u, u, u, da, da, cw, cb)


def _adamw_math(w, g, m, v):
    m = ADAM_B1 * m + (1.0 - ADAM_B1) * g
    v = ADAM_B2 * v + (1.0 - ADAM_B2) * (g * g)
    m_hat = m / (1.0 - ADAM_B1 ** ADAM_STEP)
    v_hat = v / (1.0 - ADAM_B2 ** ADAM_STEP)
    delta = -ADAM_LR * (m_hat / (jnp.sqrt(v_hat) + ADAM_EPS) + ADAM_WD * w)
    return delta, m, v


ADAMW_BLOCK_BYTES = 2 * 1024 * 1024


def _adamw_layers(w, gs, m, v, *, name):
    Lr, R, C = w.shape
    if C % LANES:
        flip = lambda t: jnp.swapaxes(t, -1, -2)
        return [flip(o) for o in _adamw_layers(flip(w), [flip(g) for g in gs], flip(m), flip(v), name=name)]
    tr = _pick(R, (256, 128, 64, 32, 16, 8))
    tc = C if tr * C * 4 <= ADAMW_BLOCK_BYTES else _pick(C, (256, 128))
    outs = None
    for layer, g in enumerate(gs):
        def body(w_ref, g_ref, m_ref, v_ref, *rest):
            go_ref, d_ref, nm_ref, nv_ref = rest[-4:]
            gv = g_ref[...]
            d, nm, nv = _adamw_math(w_ref[...], gv, m_ref[...], v_ref[...])
            go_ref[...] = gv
            d_ref[...] = d
            nm_ref[...] = nm
            nv_ref[...] = nv

        lay = pl.BlockSpec((None, tr, tc), lambda i, j, layer=layer: (layer, i, j))
        in_specs = [lay, pl.BlockSpec((tr, tc), lambda i, j: (i, j)), lay, lay]
        args = [w, g, m, v]
        aliases = {}
        if outs is not None:
            in_specs += [pl.BlockSpec(memory_space=pl.ANY)] * 4
            args += list(outs)
            aliases = {4 + t: t for t in range(4)}
        outs = pl.pallas_call(
            body, name=f"{name}_{layer}", grid=(R // tr, C // tc), in_specs=in_specs, out_specs=[lay] * 4,
            out_shape=[jax.ShapeDtypeStruct((Lr, R, C), F32)] * 4, input_output_aliases=aliases,
            compiler_params=_cparams(("parallel", "parallel")),
        )(*args)
    return outs


def _adamw_small(items, *, name):
    n = len(items)

    def body(*refs):
        ins, outs = refs[:4 * n], refs[4 * n:]
        for t in range(n):
            w, g, m, v = (r[...] for r in ins[4 * t:4 * t + 4])
            d, nm, nv = _adamw_math(w, g, m, v)
            outs[3 * t][...] = d
            outs[3 * t + 1][...] = nm
            outs[3 * t + 2][...] = nv

    flat = [a for it in items for a in it]
    out_shape = [jax.ShapeDtypeStruct(it[0].shape, F32) for it in items for _ in range(3)]
    vm = pl.BlockSpec(memory_space=pltpu.VMEM)
    res = pl.pallas_call(body, name=name, in_specs=[vm] * len(flat), out_specs=[vm] * len(out_shape),
                         out_shape=out_shape)(*flat)
    return [tuple(res[3 * t:3 * t + 3]) for t in range(n)]


def _place():
    return lax.axis_index("x"), lax.axis_index("y"), lax.axis_index("c")


_FLIPS = ((1, 0), (0, 1), (1, 1))


ROW_ALIGN = 16


def _half(rows, which):
    return pl.ds(pl.multiple_of(which * (rows // 2), ROW_ALIGN), rows // 2)


def _remote(src, dst, send_sems, recv_sems, k, to):
    return pltpu.make_async_remote_copy(src_ref=src, dst_ref=dst, send_sem=send_sems.at[k], recv_sem=recv_sems.at[k],
                                        device_id=to, device_id_type=MESH)


class _Gather:
    def __init__(self, w_refs, out_refs, send_sems, recv_sems):
        self.w_refs, self.out_refs, self.send_sems, self.recv_sems = w_refs, out_refs, send_sems, recv_sems
        self.pairs = [(i, j) for i in range(len(w_refs)) for j in range(3)]

    @staticmethod
    def out_shape(shards):
        return [jax.ShapeDtypeStruct((N_CHIPS,) + s.shape, s.dtype) for s in shards]

    @staticmethod
    def semaphores(n):
        return [pltpu.SemaphoreType.DMA((6 * n,)), pltpu.SemaphoreType.DMA((6 * n,))]

    def _where(self):
        x, y, c = _place()
        return x, y, c, [(x ^ fx, y ^ fy) for fx, fy in _FLIPS]

    def _over_ici(self, i, j, landed):
        x, y, c, chips = self._where()
        px, py = chips[j]
        mine = _half(self.w_refs[i].shape[0], c)
        if landed:
            src = dst = self.out_refs[i].at[2 * px + py, mine]
        else:
            src, dst = self.w_refs[i].at[mine], self.out_refs[i].at[2 * x + y, mine]
        return _remote(src, dst, self.send_sems, self.recv_sems, 6 * i + j, (px, py, c))

    def _over_d2d(self, i, j, which):
        x, y, c, chips = self._where()
        px, py = chips[j]
        blk = self.out_refs[i].at[2 * px + py, _half(self.w_refs[i].shape[0], which)]
        return _remote(blk, blk, self.send_sems, self.recv_sems, 6 * i + 3 + j, (x, y, 1 - c))

    def issue(self):
        for i, j in self.pairs:
            self._over_ici(i, j, False).start()

    def finish(self):
        c = lax.axis_index("c")
        for i, j in self.pairs:
            self._over_ici(i, j, True).wait_recv()
            self._over_d2d(i, j, c).start()
        for i, j in self.pairs:
            self._over_d2d(i, j, 1 - c).wait_recv()
        for i, j in self.pairs:
            self._over_ici(i, j, False).wait_send()
            self._over_d2d(i, j, c).wait_send()


def _swap_halves(gs, *, name):
    n = len(gs)
    any_spec = pl.BlockSpec(memory_space=pl.ANY)

    def body(*refs):
        swap = _Swap(refs[:n], refs[n:2 * n], refs[2 * n], refs[2 * n + 1])
        swap.issue()
        swap.finish()

    return pl.pallas_call(
        body, name=name, in_specs=[any_spec] * n, out_specs=[any_spec] * n,
        out_shape=_Swap.out_shape(gs), scratch_shapes=_Swap.semaphores(n),
    )(*gs)


class _Swap:
    def __init__(self, g_refs, out_refs, send_sems, recv_sems):
        self.g_refs, self.out_refs, self.send_sems, self.recv_sems = g_refs, out_refs, send_sems, recv_sems

    @staticmethod
    def out_shape(gs):
        return [jax.ShapeDtypeStruct((N_CHIPS, g.shape[1] // 2, g.shape[2]), g.dtype) for g in gs]

    @staticmethod
    def semaphores(n):
        return [pltpu.SemaphoreType.DMA((n,)), pltpu.SemaphoreType.DMA((n,))]

    def _copies(self):
        x, y, c = _place()
        return [_remote(g.at[:, _half(g.shape[1], 1 - c)], out, self.send_sems, self.recv_sems, i, (x, y, 1 - c))
                for i, (g, out) in enumerate(zip(self.g_refs, self.out_refs))]

    def issue(self):
        for cp in self._copies():
            cp.start()

    def finish(self):
        for cp in self._copies():
            cp.wait()


def _add_halves(g, got, c_arr, *, name):
    _, rows, cols = g.shape
    blk = (None, rows // 2, cols)

    def body(c_ref, g_ref, got_ref, o_ref):
        o_ref[...] = (g_ref[...] + got_ref[...]).astype(o_ref.dtype)

    return pl.pallas_call(
        body, name=name,
        grid_spec=pltpu.PrefetchScalarGridSpec(
            num_scalar_prefetch=1, grid=(N_CHIPS,),
            in_specs=[pl.BlockSpec(blk, lambda j, c_ref: (j, c_ref[0], 0)), pl.BlockSpec(blk, lambda j, c_ref: (j, 0, 0))],
            out_specs=pl.BlockSpec(blk, lambda j, c_ref: (j, 0, 0))),
        out_shape=jax.ShapeDtypeStruct((N_CHIPS, rows // 2, cols), WIRE_DTYPE),
        compiler_params=_cparams(("parallel",)),
    )(c_arr, g, got)


class _Exchange:
    def __init__(self, q_refs, out_refs, send_sems, recv_sems):
        self.q_refs, self.out_refs, self.send_sems, self.recv_sems = q_refs, out_refs, send_sems, recv_sems
        self.pairs = [(i, j) for i in range(len(q_refs)) for j in range(3)]

    @staticmethod
    def out_shape(qs):
        return [jax.ShapeDtypeStruct(q.shape, q.dtype) for q in qs]

    @staticmethod
    def semaphores(n):
        return [pltpu.SemaphoreType.DMA((3 * n,)), pltpu.SemaphoreType.DMA((3 * n,))]

    def _copy(self, i, j, landed):
        x, y, c = _place()
        px, py = [(x ^ fx, y ^ fy) for fx, fy in _FLIPS][j]
        if landed:
            src = dst = self.out_refs[i].at[2 * px + py]
        else:
            src, dst = self.q_refs[i].at[2 * px + py], self.out_refs[i].at[2 * x + y]
        return _remote(src, dst, self.send_sems, self.recv_sems, 3 * i + j, (px, py, c))

    def issue(self):
        for i, j in self.pairs:
            self._copy(i, j, False).start()

    def finish(self):
        for i, j in self.pairs:
            self._copy(i, j, True).wait_recv()
        for i, j in self.pairs:
            self._copy(i, j, False).wait_send()


def _sum_chips(q, r, place_arr, *, name):
    _, h, cols = q.shape
    blk = (None, h, cols)

    def body(p_ref, q_ref, r1_ref, r2_ref, r3_ref, o_ref):
        o_ref[...] = ((q_ref[...].astype(F32) + r1_ref[...].astype(F32)) + r2_ref[...].astype(F32)) + r3_ref[...].astype(F32)

    other = [pl.BlockSpec(blk, lambda i, p_ref, f=f: (p_ref[0] ^ f, 0, 0)) for f in (1, 2, 3)]
    return pl.pallas_call(
        body, name=name,
        grid_spec=pltpu.PrefetchScalarGridSpec(
            num_scalar_prefetch=1, grid=(1,),
            in_specs=[pl.BlockSpec(blk, lambda i, p_ref: (p_ref[0], 0, 0))] + other,
            out_specs=pl.BlockSpec((h, cols), lambda i, p_ref: (p_ref[1], 0))),
        out_shape=jax.ShapeDtypeStruct((2 * h, cols), F32),
        compiler_params=_cparams(("arbitrary",)),
    )(place_arr, q, r, r, r)


def _join_halves(fs, *, name):
    n = len(fs)
    any_spec = pl.BlockSpec(memory_space=pl.ANY)

    def body(*refs):
        out_refs, send_sems, recv_sems = refs[n:2 * n], refs[2 * n], refs[2 * n + 1]
        x, y, c = _place()
        sent = []
        for i in range(n):
            mine = out_refs[i].at[_half(fs[i].shape[0], c)]
            cp = _remote(mine, mine, send_sems, recv_sems, i, (x, y, 1 - c))
            cp.start()
            sent.append(cp)
        for i in range(n):
            its = out_refs[i].at[_half(fs[i].shape[0], 1 - c)]
            _remote(its, its, send_sems, recv_sems, i, (x, y, 1 - c)).wait_recv()
        for cp in sent:
            cp.wait_send()

    return pl.pallas_call(
        body, name=name, in_specs=[any_spec] * n, out_specs=[any_spec] * n,
        out_shape=[jax.ShapeDtypeStruct(f.shape, f.dtype) for f in fs], input_output_aliases={i: i for i in range(n)},
        scratch_shapes=[pltpu.SemaphoreType.DMA((n,)), pltpu.SemaphoreType.DMA((n,))],
    )(*fs)


def _allreduce_small(buf, *, name):
    R = buf.shape[0]
    vm = pl.BlockSpec(memory_space=pltpu.VMEM)

    def body(b_ref, o_ref, slots, send_sems, recv_sems):
        x, y, c = _place()
        me = 4 * x + 2 * y + c
        slots[me] = b_ref[...]
        sends = []
        for kk in range(1, 8):
            fx, fy, fc = (kk >> 2) & 1, (kk >> 1) & 1, kk & 1
            cp = pltpu.make_async_remote_copy(
                src_ref=b_ref, dst_ref=slots.at[me], send_sem=send_sems.at[kk - 1], recv_sem=recv_sems.at[kk - 1],
                device_id=(x ^ fx, y ^ fy, c ^ fc), device_id_type=MESH)
            cp.start()
            sends.append(cp)
        for kk in range(1, 8):
            fx, fy, fc = (kk >> 2) & 1, (kk >> 1) & 1, kk & 1
            peer = 4 * (x ^ fx) + 2 * (y ^ fy) + (c ^ fc)
            pltpu.make_async_remote_copy(
                src_ref=b_ref, dst_ref=slots.at[peer], send_sem=send_sems.at[kk - 1], recv_sem=recv_sems.at[kk - 1],
                device_id=(x ^ fx, y ^ fy, c ^ fc), device_id_type=MESH).wait_recv()
        for cp in sends:
            cp.wait_send()
        acc = slots[0]
        for d in range(1, 8):
            acc = acc + slots[d]
        o_ref[...] = acc

    return pl.pallas_call(
        body, name=name, in_specs=[vm], out_specs=vm,
        out_shape=jax.ShapeDtypeStruct((R, LANES), F32),
        scratch_shapes=[pltpu.VMEM((8, R, LANES), F32), pltpu.SemaphoreType.DMA((7,)), pltpu.SemaphoreType.DMA((7,))],
    )(buf)


def _pad_rows(v, mult=8 * LANES):
    flat = v.reshape(-1)
    n = flat.shape[0]
    tot = -(-n // mult) * mult
    return jnp.pad(flat, (0, tot - n)).reshape(-1, LANES)


def _pack_small(parts):
    return jnp.concatenate([_pad_rows(p.astype(F32)) for p in parts], axis=0)


def _unpack_small(buf, shapes):
    out, r = [], 0
    for sh in shapes:
        n = math.prod(sh)
        rows = -(-n // (8 * LANES)) * 8
        out.append(buf[r:r + rows].reshape(-1)[:n].reshape(sh))
        r += rows
    return out


def _cols_to_shards(w):
    *lead, K, N = w.shape
    t = w.reshape(*lead, K, N_CHIPS, N // N_CHIPS)
    return jnp.moveaxis(t, -2, 0)


def _shards_to_cols(t):
    t = jnp.moveaxis(t, 0, -2)
    *lead, K, _, n = t.shape
    return t.reshape(*lead, K, N_CHIPS * n)


_BIG = ("m_w_in", "m_w_out", "a_w_in", "a_w_out", "f_w_up", "f_w_down")


def kernel(x, m_w_in, m_gate_bias, m_head_norm, m_w_out, a_w_in, a_b_in, a_sinks, a_w_out, a_b_out, norm_mix_pre, norm_mix_post, norm_ffn_pre, norm_ffn_post, f_w_up, f_conv_w, f_conv_b, f_w_down, loss_target, m_m_w_in, m_m_gate_bias, m_m_head_norm, m_m_w_out, m_a_w_in, m_a_b_in, m_a_sinks, m_a_w_out, m_a_b_out, m_norm_mix_pre, m_norm_mix_post, m_norm_ffn_pre, m_norm_ffn_post, m_f_w_up, m_f_conv_w, m_f_conv_b, m_f_w_down, v_m_w_in, v_m_gate_bias, v_m_head_norm, v_m_w_out, v_a_w_in, v_a_b_in, v_a_sinks, v_a_w_out, v_a_b_out, v_norm_mix_pre, v_norm_mix_post, v_norm_ffn_pre, v_norm_ffn_post, v_f_w_up, v_f_conv_w, v_f_conv_b, v_f_w_down):
    params = dict(m_w_in=m_w_in, m_gate_bias=m_gate_bias, m_head_norm=m_head_norm, m_w_out=m_w_out, a_w_in=a_w_in,
                  a_b_in=a_b_in, a_sinks=a_sinks, a_w_out=a_w_out, a_b_out=a_b_out, norm_mix_pre=norm_mix_pre,
                  norm_mix_post=norm_mix_post, norm_ffn_pre=norm_ffn_pre, norm_ffn_post=norm_ffn_post, f_w_up=f_w_up,
                  f_conv_w=f_conv_w, f_conv_b=f_conv_b, f_w_down=f_w_down)
    mom1 = dict(m_w_in=m_m_w_in, m_gate_bias=m_m_gate_bias, m_head_norm=m_m_head_norm, m_w_out=m_m_w_out,
                a_w_in=m_a_w_in, a_b_in=m_a_b_in, a_sinks=m_a_sinks, a_w_out=m_a_w_out, a_b_out=m_a_b_out,
                norm_mix_pre=m_norm_mix_pre, norm_mix_post=m_norm_mix_post, norm_ffn_pre=m_norm_ffn_pre,
                norm_ffn_post=m_norm_ffn_post, f_w_up=m_f_w_up, f_conv_w=m_f_conv_w, f_conv_b=m_f_conv_b,
                f_w_down=m_f_w_down)
    mom2 = dict(m_w_in=v_m_w_in, m_gate_bias=v_m_gate_bias, m_head_norm=v_m_head_norm, m_w_out=v_m_w_out,
                a_w_in=v_a_w_in, a_b_in=v_a_b_in, a_sinks=v_a_sinks, a_w_out=v_a_w_out, a_b_out=v_a_b_out,
                norm_mix_pre=v_norm_mix_pre, norm_mix_post=v_norm_mix_post, norm_ffn_pre=v_norm_ffn_pre,
                norm_ffn_post=v_norm_ffn_post, f_w_up=v_f_w_up, f_conv_w=v_f_conv_w, f_conv_b=v_f_conv_b,
                f_w_down=v_f_w_down)
    order = list(params)

    mx, my, mc = _place()
    chip = 2 * mx + my
    h0 = x[0]
    target = loss_target[0]

    def two_d(t):
        return t.reshape(-1, t.shape[-1])

    early = ("m_w_in",)
    late = tuple(n for n in _BIG if n not in early)
    mine = {n: two_d(params[n]).astype(MXU_DTYPE) for n in _BIG}

    def with_own_slot(names, theirs):
        return {n: lax.dynamic_update_slice(t, mine[n][None], (chip, 0, 0)) for n, t in zip(names, theirs)}

    z0, *early_theirs = _rms_fwd(h0, norm_mix_pre[0:1], [mine[n] for n in early], out_dtype=MXU_DTYPE,
                                 name="mix_pre_norm0")
    gathered = with_own_slot(early, early_theirs)

    def in_place(shard, axis):
        width = shard.shape[axis]
        z = jnp.zeros(shard.shape[:axis] + (N_CHIPS * width,) + shard.shape[axis + 1:], F32)
        contrib = jnp.where(mc == 0, shard, 0.0)
        return lax.dynamic_update_slice_in_dim(z, contrib, chip * width, axis)

    sm_in = [in_place(a_b_in, 1), in_place(a_b_out, 1), in_place(f_conv_w, 2)]
    sm_full = _unpack_small(_allreduce_small(_pack_small(sm_in), name="gather_small"), [t.shape for t in sm_in])
    b_in_full, b_out_full, conv_w_full = sm_full

    W_in = _shards_to_cols(gathered["m_w_in"])
    W_all = jnp.pad(W_in, ((0, 0), (0, M_PROJ - W_in.shape[1])))
    gbias = jnp.pad(m_gate_bias[0].reshape(1, 2 * M_HEADS), ((0, 0), (0, LANES - 2 * M_HEADS)))

    grads = {}

    def ffn_fwd(i, z1, end):
        u = _mm(z1, W_up, b_layer=i, name=f"ffn_up{i}")
        a = _convgate_fwd(u, conv_w_full[i], f_conv_b[i:i + 1], name=f"ffn_act{i}")
        return u, a, _mm(a, W_down[i], rows=end, name=f"ffn_down{i}")

    proj = _mm(z0, W_all, name="mlstm_proj")
    gated, C_all, n_all, m_all, *late_theirs = _mlstm_fwd(proj, gbias, m_head_norm, [mine[n] for n in late],
                                                          name="mlstm_fwd")
    gathered.update(with_own_slot(late, late_theirs))
    W_mout = gathered["m_w_out"].reshape(D_MODEL, D_MODEL)
    A_in, A_out = _shards_to_cols(gathered["a_w_in"]), gathered["a_w_out"].reshape(D_MODEL, D_MODEL)
    W_up = gathered["f_w_up"]
    dsh = D_FF // N_CHIPS
    W_down = [gathered["f_w_down"][:, i * dsh:(i + 1) * dsh].reshape(D_FF, D_MODEL) for i in range(2)]
    zm0, h1, z1 = _mm(gated, W_mout, rows=_rows_sublayer_end(h0, norm_mix_post[0:1], norm_ffn_pre[0:1]), name="mlstm_out")
    u0, a0, (zf0, h2, y0) = ffn_fwd(0, z1, _rows_sublayer_end(h1, norm_ffn_post[0:1], norm_mix_pre[1:2]))

    aproj = _mm(y0, A_in, bias=b_in_full, name="attn_proj")
    sinks = a_sinks.reshape(A_KVH, A_G)
    ao = _attn_fwd(aproj, sinks, name="attn_fwd")
    zm1, h3, z3 = _mm(ao, A_out, bias=b_out_full, rows=_rows_sublayer_end(h2, norm_mix_post[1:2], norm_ffn_pre[1:2]),
                      name="attn_out")
    u1, a1, (zf1, dh, dzf1, g_post1, loss_lanes) = ffn_fwd(1, z3, _rows_last_sublayer_end(h3, norm_ffn_post[1:2], target))
    loss = lax.psum(loss_lanes[0, 0], ("x", "y", "c"))

    g_post, g_fpre, g_mpost, g_mpre = [None, g_post1], [None, None], [None, None], [None, None]
    dW_up, dW_down, dconv_w, dconv_b = [None, None], [None, None], [None, None], [None, None]

    def by_rows(g):
        return g.reshape(N_CHIPS, -1, g.shape[-1])

    def ffn_bwd(i, dzf, z1_, u, a, start, ride):
        da = _mm(dzf, W_down[i], tb=True, name=f"ffn_down_dx{i}")
        dW_down[i] = _mm(a, dzf, ta=True, name=f"ffn_down_dw{i}")
        du, dconv_w[i], dconv_b[i] = _convgate_bwd(u, da, conv_w_full[i], f_conv_b[i:i + 1], name=f"ffn_act_bwd{i}")
        dW_up[i] = _mm(z1_, du, ta=True, out_stacked=True, name=f"ffn_up_dw{i}")
        riders = (list(ride) + [by_rows(dW_down[i]), dW_up[i]]) if ride is not None else []
        outs = _mm(du, W_up, tb=True, b_layer=i, rows=start, ride=(_Swap, riders) if riders else None,
                   name=f"ffn_up_dx{i}")
        return outs[:len(start.outs)], riders, outs[len(start.outs):]

    (dh3, g_fpre[1], dzm1, g_mpost[1], db_out), _, _ = ffn_bwd(
        1, dzf1, z3, u1, a1, _rows_sublayer_start_bwd(h3, dh, zm1, norm_ffn_pre[1:2], norm_mix_post[1:2]), None)
    dao = _mm(dzm1, A_out, tb=True, name="attn_out_dx")
    dA_out = _mm(ao, dzm1, ta=True, name="attn_out_dw")
    daq, dakv, dsinks, cs_q, cs_kv = _attn_bwd(aproj, sinks, dao, name="attn_bwd")
    db_in = jnp.concatenate([cs_q, cs_kv], axis=1)
    dh2, g_mpre[1], dzf0, g_post[0], _ = _mm(
        daq, A_in[:, :A_NQ], tb=True, add=_mm(dakv, A_in[:, A_NQ:], tb=True, name="attn_proj_kv_dx"),
        rows=_rows_sublayer_start_bwd(h2, dh3, zf0, norm_mix_pre[1:2], norm_ffn_post[0:1]), name="attn_proj_q_dx")
    dA_in = jnp.concatenate([_mm(y0, daq, ta=True, name="attn_proj_q_dw"),
                             _mm(y0, dakv, ta=True, name="attn_proj_kv_dw")], axis=1)

    c_arr = jnp.reshape(mc, (1,)).astype(jnp.int32)
    place_arr = jnp.stack([chip, mc]).astype(jnp.int32)

    def add_halves(tags, gs, got):
        return [_add_halves(g, t, c_arr, name=f"grad_add_halves_{tag}") for tag, g, t in zip(tags, gs, got)]

    (dh1, g_fpre[0], dzm0, g_mpost[0], _), rode, rode_got = ffn_bwd(
        0, dzf0, z1, u0, a0, _rows_sublayer_start_bwd(h1, dh2, zm0, norm_ffn_pre[0:1], norm_mix_post[0:1]),
        [_cols_to_shards(dA_in), by_rows(dA_out), dW_up[1], by_rows(dW_down[1])])
    rode_tags = ["a_w_in", "a_w_out", "f_w_up1", "f_w_down1", "f_w_down0", "f_w_up0"]
    dgated = _mm(dzm0, W_mout, tb=True, name="mlstm_out_dx")
    dW_mout = _mm(gated, dzm0, ta=True, name="mlstm_out_dw")
    next_tags, next_gs = ["m_w_out"], [by_rows(dW_mout)]
    late_tags = rode_tags + next_tags
    late_part = (add_halves(rode_tags, rode, rode_got)
                 + add_halves(next_tags, next_gs, _swap_halves(next_gs, name="grad_swap_halves_late")))
    dproj, dgbias, dhn, *late_from = _mlstm_bwd(proj, gbias, m_head_norm, C_all, n_all, m_all, dgated, late_part,
                                                name="mlstm_bwd")
    dW_all = _mm(z0, dproj, ta=True, name="mlstm_proj_dw")
    dW_min = dW_all[:, :m_w_in.shape[-1] * N_CHIPS]

    early_tags, early_gs = ["m_w_in"], [_cols_to_shards(dW_min)]
    early_part = add_halves(early_tags, early_gs, _swap_halves(early_gs, name="grad_swap_halves_early"))
    grad_x, g_mpre[0], *early_from = _mm(
        dproj, W_all, tb=True, rows=_rows_first_sublayer_start_bwd(h0, dh1, norm_mix_pre[0:1]),
        ride=(_Exchange, early_part),
        name="mlstm_proj_dx")
    tags = early_tags + late_tags
    halves = [_sum_chips(q, r, place_arr, name=f"grad_sum_chips_{tag}")
              for tag, q, r in zip(tags, early_part + late_part, list(early_from) + list(late_from))]
    reduced = dict(zip(tags, _join_halves(halves, name="grad_join_halves")))
    layer_grads = dict(m_w_in=[reduced["m_w_in"]], m_w_out=[reduced["m_w_out"]], a_w_in=[reduced["a_w_in"]],
                       a_w_out=[reduced["a_w_out"]], f_w_up=[reduced["f_w_up0"], reduced["f_w_up1"]],
                       f_w_down=[reduced["f_w_down0"], reduced["f_w_down1"]])

    small_g = [
        dgbias[:, :2 * M_HEADS].reshape(1, 2, M_HEADS),
        dhn,
        dsinks.reshape(1, A_QH),
        db_in, db_out,
        jnp.concatenate(g_mpre), jnp.concatenate(g_mpost), jnp.concatenate(g_fpre), jnp.concatenate(g_post),
        jnp.stack(dconv_w), jnp.concatenate(dconv_b),
    ]
    small_names = ["m_gate_bias", "m_head_norm", "a_sinks", "a_b_in", "a_b_out", "norm_mix_pre", "norm_mix_post",
                   "norm_ffn_pre", "norm_ffn_post", "f_conv_w", "f_conv_b"]
    red = _unpack_small(_allreduce_small(_pack_small(small_g), name="reduce_small"), [t.shape for t in small_g])
    for n, t in zip(small_names, red):
        if n in ("a_b_in", "a_b_out", "f_conv_w"):
            axis = t.ndim - 1
            width = params[n].shape[axis]
            t = lax.dynamic_slice_in_dim(t, chip * width, width, axis)
        grads[n] = t

    deltas, new_m, new_v = {}, {}, {}
    for n in _BIG:
        grads[n], deltas[n], new_m[n], new_v[n] = _adamw_layers(params[n], layer_grads[n], mom1[n], mom2[n],
                                                                name=f"adamw_{n}")
    two = lambda t: t.reshape(-1, t.shape[-1])
    res = _adamw_small([(two(params[n]), two(grads[n]), two(mom1[n]), two(mom2[n])) for n in small_names],
                       name="adamw_small")
    for n, (d, nm, nv) in zip(small_names, res):
        sh = params[n].shape
        deltas[n], new_m[n], new_v[n] = d.reshape(sh), nm.reshape(sh), nv.reshape(sh)

    return (loss, grad_x[None], *[grads[n] for n in order], *[deltas[n] for n in order],
            *[new_m[n] for n in order], *[new_v[n] for n in order])
```
